```python
import math
import jax, jax.numpy as jnp
from jax import lax
import numpy as np

D_MODEL = 2048
BATCH = 8
SEQ = 2048
DEPTH = 1

MIX_WIDTH = D_MODEL
ATTN_HEADS = 8
HEAD_DIM = 128
ATTN_WIDTH = ATTN_HEADS * HEAD_DIM
LRU_HEADS = 8
LRU_WIDTH = MIX_WIDTH - ATTN_WIDTH
LRU_BLOCK = LRU_WIDTH // LRU_HEADS
CONV_WIDTH = 4
LRU_C = 8.0
D_FF = 4 * D_MODEL
Q_BLOCK = 128
N_MOD = 6
IN_COLS = 3 * ATTN_WIDTH + 2 * LRU_WIDTH
EPS = 1e-6

kernel_name = "hymba_stickbreak_rglru_sqrelu_adaln"


def rmsnorm(x, g):
    x32 = x.astype(jnp.float32)
    y = x32 * lax.rsqrt(jnp.mean(x32 * x32, axis=-1, keepdims=True) + EPS)
    return (y * g.astype(jnp.float32)).astype(x.dtype)


def stick_breaking_attention(q, k, v):
    B, S, H, Dh = q.shape
    scale = Dh ** -0.5
    outs = []
    for i in range(S // Q_BLOCK):
        q0 = i * Q_BLOCK
        kend = q0 + Q_BLOCK
        qb = q[:, q0:kend].astype(jnp.float32)
        kb = k[:, :kend].astype(jnp.float32)
        vb = v[:, :kend].astype(jnp.float32)
        z = jnp.einsum('bqhd,bkhd->bhqk', qb, kb) * scale
        t_idx = q0 + jnp.arange(Q_BLOCK)[:, None]
        s_idx = jnp.arange(kend)[None, :]
        causal = s_idx < t_idx
        log_beta = jax.nn.log_sigmoid(z)
        log_stay = jnp.where(causal, jax.nn.log_sigmoid(-z), 0.0)
        rev = lax.cumsum(log_stay, axis=3, reverse=True)
        after = jnp.concatenate([rev[..., 1:], jnp.zeros_like(rev[..., :1])], axis=-1)
        w = jnp.where(causal, jnp.exp(log_beta + after), 0.0)
        outs.append(jnp.einsum('bhqk,bkhd->bqhd', w, vb))
    return jnp.concatenate(outs, axis=1).astype(q.dtype)


def causal_depthwise_conv(x, w, b):
    C = x.shape[-1]
    y = lax.conv_general_dilated(
        x.astype(jnp.float32), w.astype(jnp.float32)[:, None, :],
        window_strides=(1,), padding=[(CONV_WIDTH - 1, 0)],
        dimension_numbers=('NWC', 'WIO', 'NWC'), feature_group_count=C)
    return y + b.astype(jnp.float32)


def rg_lru(x, w_a, b_a, w_x, b_x, lam):
    B, S, _ = x.shape
    xh = x.reshape(B, S, LRU_HEADS, LRU_BLOCK)
    r = jax.nn.sigmoid(jnp.einsum('bshc,hcd->bshd', xh, w_a.astype(jnp.float32)).reshape(B, S, LRU_WIDTH) + b_a)
    i = jax.nn.sigmoid(jnp.einsum('bshc,hcd->bshd', xh, w_x.astype(jnp.float32)).reshape(B, S, LRU_WIDTH) + b_x)
    log_a = -LRU_C * r * jax.nn.softplus(-lam.astype(jnp.float32))
    a = jnp.exp(log_a)
    u = jnp.sqrt(-jnp.expm1(2.0 * log_a)) * (i * x)

    def combine(left, right):
        a1, b1 = left
        a2, b2 = right
        return a1 * a2, a2 * b1 + b2

    _, h = lax.associative_scan(combine, (a, u), axis=1)
    return h


def _fwd_setup_inputs(seed: int = 0) -> dict:
    key = jax.random.key(seed)
    ks = jax.random.split(key, 24)
    f32 = jnp.float32
    nrm = lambda k, shape, s: jax.random.normal(k, shape, f32) * s
    u = jax.random.uniform(ks[11], (DEPTH, LRU_WIDTH), f32, 0.9, 0.999)
    a_base = u ** (1.0 / LRU_C)
    lru_lambda = jnp.log(a_base) - jnp.log1p(-a_base)
    return {
        "x": nrm(ks[0], (BATCH, SEQ, D_MODEL), 1.0),
        "c": nrm(ks[1], (BATCH, D_MODEL), 1.0),
        "w_ada": nrm(ks[2], (DEPTH, D_MODEL, N_MOD * D_MODEL), 0.5 * D_MODEL ** -0.5),
        "b_ada": nrm(ks[3], (DEPTH, N_MOD * D_MODEL), 0.02),
        "g_norm_mix": 1.0 + nrm(ks[4], (DEPTH, D_MODEL), 0.02),
        "w_in": nrm(ks[5], (DEPTH, D_MODEL, IN_COLS), D_MODEL ** -0.5),
        "w_conv": nrm(ks[6], (DEPTH, CONV_WIDTH, LRU_WIDTH), CONV_WIDTH ** -0.5),
        "b_conv": nrm(ks[7], (DEPTH, LRU_WIDTH), 0.02),
        "w_rg_a": nrm(ks[8], (DEPTH, LRU_HEADS, LRU_BLOCK, LRU_BLOCK), LRU_BLOCK ** -0.5),
        "b_rg_a": nrm(ks[9], (DEPTH, LRU_WIDTH), 0.02),
        "w_rg_x": nrm(ks[10], (DEPTH, LRU_HEADS, LRU_BLOCK, LRU_BLOCK), LRU_BLOCK ** -0.5),
        "b_rg_x": nrm(ks[12], (DEPTH, LRU_WIDTH), 0.02),
        "lru_lambda": lru_lambda,
        "g_attn_out": 1.0 + nrm(ks[13], (DEPTH, ATTN_WIDTH), 0.02),
        "g_lru_out": 1.0 + nrm(ks[14], (DEPTH, LRU_WIDTH), 0.02),
        "w_out": nrm(ks[15], (DEPTH, MIX_WIDTH, D_MODEL), MIX_WIDTH ** -0.5),
        "g_norm_mlp": 1.0 + nrm(ks[16], (DEPTH, D_MODEL), 0.02),
        "w_mlp_in": nrm(ks[17], (DEPTH, D_MODEL, D_FF), D_MODEL ** -0.5),
        "w_mlp_out": nrm(ks[18], (DEPTH, D_FF, D_MODEL), D_FF ** -0.5),
        "g_norm_final": 1.0 + nrm(ks[19], (D_MODEL,), 0.02),
    }


def _fwd_reference(x, c, w_ada, b_ada, g_norm_mix, w_in, w_conv, b_conv, w_rg_a, b_rg_a,
              w_rg_x, b_rg_x, lru_lambda, g_attn_out, g_lru_out, w_out, g_norm_mlp,
              w_mlp_in, w_mlp_out, g_norm_final):
    B, S, D = x.shape
    c_act = jax.nn.silu(c.astype(jnp.float32))
    for l in range(DEPTH):
        mod = (c_act @ w_ada[l].astype(jnp.float32) + b_ada[l]).reshape(B, N_MOD, D)
        sh1, sc1, gt1, sh2, sc2, gt2 = [mod[:, j][:, None, :] for j in range(N_MOD)]

        h = rmsnorm(x, g_norm_mix[l]).astype(jnp.float32) * (1.0 + sc1) + sh1
        proj = jnp.einsum('bsd,de->bse', h, w_in[l].astype(jnp.float32))
        q, k, v, xr, xg = jnp.split(
            proj, np.cumsum([ATTN_WIDTH, ATTN_WIDTH, ATTN_WIDTH, LRU_WIDTH]).tolist(), axis=-1)

        to_heads = lambda t: t.reshape(B, S, ATTN_HEADS, HEAD_DIM)
        o_attn = stick_breaking_attention(to_heads(q), to_heads(k), to_heads(v)).reshape(B, S, ATTN_WIDTH)

        xr = causal_depthwise_conv(xr, w_conv[l], b_conv[l])
        hr = rg_lru(xr, w_rg_a[l], b_rg_a[l], w_rg_x[l], b_rg_x[l], lru_lambda[l])
        o_lru = hr * jax.nn.gelu(xg, approximate=True)

        mixed = jnp.concatenate([rmsnorm(o_attn, g_attn_out[l]), rmsnorm(o_lru, g_lru_out[l])], axis=-1)
        y = jnp.einsum('bse,ed->bsd', mixed, w_out[l].astype(jnp.float32))
        x = (x.astype(jnp.float32) + gt1 * y).astype(x.dtype)

        h = rmsnorm(x, g_norm_mlp[l]).astype(jnp.float32) * (1.0 + sc2) + sh2
        hid = jnp.square(jax.nn.relu(jnp.einsum('bsd,df->bsf', h, w_mlp_in[l].astype(jnp.float32))))
        y = jnp.einsum('bsf,fd->bsd', hid, w_mlp_out[l].astype(jnp.float32))
        x = (x.astype(jnp.float32) + gt2 * y).astype(x.dtype)

    return rmsnorm(x, g_norm_final)


import jax as _jax
import jax.numpy as _jnp

TWIN_FORMAT = 'train_step'
FWD_PARAMS = ['x', 'c', 'w_ada', 'b_ada', 'g_norm_mix', 'w_in', 'w_conv', 'b_conv', 'w_rg_a', 'b_rg_a', 'w_rg_x', 'b_rg_x', 'lru_lambda', 'g_attn_out', 'g_lru_out', 'w_out', 'g_norm_mlp', 'w_mlp_in', 'w_mlp_out', 'g_norm_final']
TWIN_WEIGHTS = ['w_ada', 'b_ada', 'g_norm_mix', 'w_in', 'w_conv', 'b_conv', 'w_rg_a', 'b_rg_a', 'w_rg_x', 'b_rg_x', 'lru_lambda', 'g_attn_out', 'g_lru_out', 'w_out', 'g_norm_mlp', 'w_mlp_in', 'w_mlp_out', 'g_norm_final']
TWIN_DIFF_INPUT = 'x'
TWIN_INPUTS = ['x', 'c', 'w_ada', 'b_ada', 'g_norm_mix', 'w_in', 'w_conv', 'b_conv', 'w_rg_a', 'b_rg_a', 'w_rg_x', 'b_rg_x', 'lru_lambda', 'g_attn_out', 'g_lru_out', 'w_out', 'g_norm_mlp', 'w_mlp_in', 'w_mlp_out', 'g_norm_final', 'loss_target', 'm_w_ada', 'm_b_ada', 'm_g_norm_mix', 'm_w_in', 'm_w_conv', 'm_b_conv', 'm_w_rg_a', 'm_b_rg_a', 'm_w_rg_x', 'm_b_rg_x', 'm_lru_lambda', 'm_g_attn_out', 'm_g_lru_out', 'm_w_out', 'm_g_norm_mlp', 'm_w_mlp_in', 'm_w_mlp_out', 'm_g_norm_final', 'v_w_ada', 'v_b_ada', 'v_g_norm_mix', 'v_w_in', 'v_w_conv', 'v_b_conv', 'v_w_rg_a', 'v_b_rg_a', 'v_w_rg_x', 'v_b_rg_x', 'v_lru_lambda', 'v_g_attn_out', 'v_g_lru_out', 'v_w_out', 'v_g_norm_mlp', 'v_w_mlp_in', 'v_w_mlp_out', 'v_g_norm_final']
TWIN_OUTPUTS = ['loss', 'grad_x', 'grad_w_ada', 'grad_b_ada', 'grad_g_norm_mix', 'grad_w_in', 'grad_w_conv', 'grad_b_conv', 'grad_w_rg_a', 'grad_b_rg_a', 'grad_w_rg_x', 'grad_b_rg_x', 'grad_lru_lambda', 'grad_g_attn_out', 'grad_g_lru_out', 'grad_w_out', 'grad_g_norm_mlp', 'grad_w_mlp_in', 'grad_w_mlp_out', 'grad_g_norm_final', 'delta_w_ada', 'delta_b_ada', 'delta_g_norm_mix', 'delta_w_in', 'delta_w_conv', 'delta_b_conv', 'delta_w_rg_a', 'delta_b_rg_a', 'delta_w_rg_x', 'delta_b_rg_x', 'delta_lru_lambda', 'delta_g_attn_out', 'delta_g_lru_out', 'delta_w_out', 'delta_g_norm_mlp', 'delta_w_mlp_in', 'delta_w_mlp_out', 'delta_g_norm_final', 'new_m_w_ada', 'new_m_b_ada', 'new_m_g_norm_mix', 'new_m_w_in', 'new_m_w_conv', 'new_m_b_conv', 'new_m_w_rg_a', 'new_m_b_rg_a', 'new_m_w_rg_x', 'new_m_b_rg_x', 'new_m_lru_lambda', 'new_m_g_attn_out', 'new_m_g_lru_out', 'new_m_w_out', 'new_m_g_norm_mlp', 'new_m_w_mlp_in', 'new_m_w_mlp_out', 'new_m_g_norm_final', 'new_v_w_ada', 'new_v_b_ada', 'new_v_g_norm_mix', 'new_v_w_in', 'new_v_w_conv', 'new_v_b_conv', 'new_v_w_rg_a', 'new_v_b_rg_a', 'new_v_w_rg_x', 'new_v_b_rg_x', 'new_v_lru_lambda', 'new_v_g_attn_out', 'new_v_g_lru_out', 'new_v_w_out', 'new_v_g_norm_mlp', 'new_v_w_mlp_in', 'new_v_w_mlp_out', 'new_v_g_norm_final']
TWIN_LEAF_KINDS = {'loss': 'loss', 'grad_x': 'grad_x', 'grad_w_ada': 'grad_w', 'grad_b_ada': 'grad_w', 'grad_g_norm_mix': 'grad_w', 'grad_w_in': 'grad_w', 'grad_w_conv': 'grad_w', 'grad_b_conv': 'grad_w', 'grad_w_rg_a': 'grad_w', 'grad_b_rg_a': 'grad_w', 'grad_w_rg_x': 'grad_w', 'grad_b_rg_x': 'grad_w', 'grad_lru_lambda': 'grad_w', 'grad_g_attn_out': 'grad_w', 'grad_g_lru_out': 'grad_w', 'grad_w_out': 'grad_w', 'grad_g_norm_mlp': 'grad_w', 'grad_w_mlp_in': 'grad_w', 'grad_w_mlp_out': 'grad_w', 'grad_g_norm_final': 'grad_w', 'delta_w_ada': 'delta_w', 'delta_b_ada': 'delta_w', 'delta_g_norm_mix': 'delta_w', 'delta_w_in': 'delta_w', 'delta_w_conv': 'delta_w', 'delta_b_conv': 'delta_w', 'delta_w_rg_a': 'delta_w', 'delta_b_rg_a': 'delta_w', 'delta_w_rg_x': 'delta_w', 'delta_b_rg_x': 'delta_w', 'delta_lru_lambda': 'delta_w', 'delta_g_attn_out': 'delta_w', 'delta_g_lru_out': 'delta_w', 'delta_w_out': 'delta_w', 'delta_g_norm_mlp': 'delta_w', 'delta_w_mlp_in': 'delta_w', 'delta_w_mlp_out': 'delta_w', 'delta_g_norm_final': 'delta_w', 'new_m_w_ada': 'new_m', 'new_m_b_ada': 'new_m', 'new_m_g_norm_mix': 'new_m', 'new_m_w_in': 'new_m', 'new_m_w_conv': 'new_m', 'new_m_b_conv': 'new_m', 'new_m_w_rg_a': 'new_m', 'new_m_b_rg_a': 'new_m', 'new_m_w_rg_x': 'new_m', 'new_m_b_rg_x': 'new_m', 'new_m_lru_lambda': 'new_m', 'new_m_g_attn_out': 'new_m', 'new_m_g_lru_out': 'new_m', 'new_m_w_out': 'new_m', 'new_m_g_norm_mlp': 'new_m', 'new_m_w_mlp_in': 'new_m', 'new_m_w_mlp_out': 'new_m', 'new_m_g_norm_final': 'new_m', 'new_v_w_ada': 'new_v', 'new_v_b_ada': 'new_v', 'new_v_g_norm_mix': 'new_v', 'new_v_w_in': 'new_v', 'new_v_w_conv': 'new_v', 'new_v_b_conv': 'new_v', 'new_v_w_rg_a': 'new_v', 'new_v_b_rg_a': 'new_v', 'new_v_w_rg_x': 'new_v', 'new_v_b_rg_x': 'new_v', 'new_v_lru_lambda': 'new_v', 'new_v_g_attn_out': 'new_v', 'new_v_g_lru_out': 'new_v', 'new_v_w_out': 'new_v', 'new_v_g_norm_mlp': 'new_v', 'new_v_w_mlp_in': 'new_v', 'new_v_w_mlp_out': 'new_v', 'new_v_g_norm_final': 'new_v'}


def _forward(args):
    return _fwd_reference(*[args[k] for k in FWD_PARAMS])


def _output_shape():
    out = _jax.eval_shape(lambda: _forward(_fwd_setup_inputs(0)))
    return out.shape, out.dtype

N_MICROBATCH = 1
ADAM_LR = 0.001
ADAM_B1 = 0.9
ADAM_B2 = 0.999
ADAM_EPS = 1e-08
ADAM_WD = 0.01
ADAM_STEP = 10
PER_EXAMPLE_BATCH_AXIS = {'x': 0, 'c': 0, 'loss_target': 0}
SHARED_INPUTS = []
_WEIGHT_DTYPES = {'w_ada': _jnp.float32, 'b_ada': _jnp.float32, 'g_norm_mix': _jnp.float32, 'w_in': _jnp.float32, 'w_conv': _jnp.float32, 'b_conv': _jnp.float32, 'w_rg_a': _jnp.float32, 'b_rg_a': _jnp.float32, 'w_rg_x': _jnp.float32, 'b_rg_x': _jnp.float32, 'lru_lambda': _jnp.float32, 'g_attn_out': _jnp.float32, 'g_lru_out': _jnp.float32, 'w_out': _jnp.float32, 'g_norm_mlp': _jnp.float32, 'w_mlp_in': _jnp.float32, 'w_mlp_out': _jnp.float32, 'g_norm_final': _jnp.float32}
MOMENT_SCALE = {'w_ada': 3.353913e-02, 'b_ada': 5.830777e-02, 'g_norm_mix': 2.026376e-02, 'w_in': 1.577782e-02, 'w_conv': 2.074651e-02, 'b_conv': 6.265932e-02, 'w_rg_a': 2.714531e-03, 'b_rg_a': 3.369414e-03, 'w_rg_x': 5.087705e-03, 'b_rg_x': 7.218066e-03, 'lru_lambda': 8.855390e-03, 'g_attn_out': 1.899780e-02, 'g_lru_out': 1.857050e-02, 'w_out': 1.877516e-02, 'g_norm_mlp': 2.749025e-02, 'w_mlp_in': 1.416584e-02, 'w_mlp_out': 2.566224e-02, 'g_norm_final': 8.064934e+00}


def _to_microbatches(a, axis):
    t = _jnp.moveaxis(a, axis, 0)
    t = t.reshape((N_MICROBATCH, t.shape[0] // N_MICROBATCH) + t.shape[1:])
    return _jnp.moveaxis(t, 1, axis + 1)


def setup_inputs(seed: int = 0) -> dict:
    inp = _fwd_setup_inputs(seed)
    key = _jax.random.fold_in(_jax.random.key(seed), 7919)
    shape, _ = _output_shape()
    out = dict(inp)
    out["loss_target"] = _jax.random.normal(_jax.random.fold_in(key, 0), shape, _jnp.float32)
    for i, name in enumerate(TWIN_WEIGHTS):
        w = inp[name].astype(_jnp.float32)
        if MOMENT_SCALE is None:
            s = _jnp.sqrt(_jnp.mean(_jnp.square(w)) + 1e-30)
        else:
            s = MOMENT_SCALE[name]
        km, kv = _jax.random.split(_jax.random.fold_in(key, i + 1))
        out[name] = w
        out["m_" + name] = s * _jax.random.normal(km, w.shape, _jnp.float32)
        out["v_" + name] = (s * s) * _jax.random.uniform(kv, w.shape, _jnp.float32, 0.5, 1.5)
    if N_MICROBATCH > 1:
        for name, axis in PER_EXAMPLE_BATCH_AXIS.items():
            out[name] = _to_microbatches(out[name], axis)
    return {'x': out['x'], 'c': out['c'], 'w_ada': out['w_ada'], 'b_ada': out['b_ada'], 'g_norm_mix': out['g_norm_mix'], 'w_in': out['w_in'], 'w_conv': out['w_conv'], 'b_conv': out['b_conv'], 'w_rg_a': out['w_rg_a'], 'b_rg_a': out['b_rg_a'], 'w_rg_x': out['w_rg_x'], 'b_rg_x': out['b_rg_x'], 'lru_lambda': out['lru_lambda'], 'g_attn_out': out['g_attn_out'], 'g_lru_out': out['g_lru_out'], 'w_out': out['w_out'], 'g_norm_mlp': out['g_norm_mlp'], 'w_mlp_in': out['w_mlp_in'], 'w_mlp_out': out['w_mlp_out'], 'g_norm_final': out['g_norm_final'], 'loss_target': out['loss_target'], 'm_w_ada': out['m_w_ada'], 'm_b_ada': out['m_b_ada'], 'm_g_norm_mix': out['m_g_norm_mix'], 'm_w_in': out['m_w_in'], 'm_w_conv': out['m_w_conv'], 'm_b_conv': out['m_b_conv'], 'm_w_rg_a': out['m_w_rg_a'], 'm_b_rg_a': out['m_b_rg_a'], 'm_w_rg_x': out['m_w_rg_x'], 'm_b_rg_x': out['m_b_rg_x'], 'm_lru_lambda': out['m_lru_lambda'], 'm_g_attn_out': out['m_g_attn_out'], 'm_g_lru_out': out['m_g_lru_out'], 'm_w_out': out['m_w_out'], 'm_g_norm_mlp': out['m_g_norm_mlp'], 'm_w_mlp_in': out['m_w_mlp_in'], 'm_w_mlp_out': out['m_w_mlp_out'], 'm_g_norm_final': out['m_g_norm_final'], 'v_w_ada': out['v_w_ada'], 'v_b_ada': out['v_b_ada'], 'v_g_norm_mix': out['v_g_norm_mix'], 'v_w_in': out['v_w_in'], 'v_w_conv': out['v_w_conv'], 'v_b_conv': out['v_b_conv'], 'v_w_rg_a': out['v_w_rg_a'], 'v_b_rg_a': out['v_b_rg_a'], 'v_w_rg_x': out['v_w_rg_x'], 'v_b_rg_x': out['v_b_rg_x'], 'v_lru_lambda': out['v_lru_lambda'], 'v_g_attn_out': out['v_g_attn_out'], 'v_g_lru_out': out['v_g_lru_out'], 'v_w_out': out['v_w_out'], 'v_g_norm_mlp': out['v_g_norm_mlp'], 'v_w_mlp_in': out['v_w_mlp_in'], 'v_w_mlp_out': out['v_w_mlp_out'], 'v_g_norm_final': out['v_g_norm_final']}


def _loss(weights, diff, rest, loss_target):
    with _jax.named_scope("forward"):
        args = {**rest, TWIN_DIFF_INPUT: diff, **{k: w.astype(_WEIGHT_DTYPES[k]) for k, w in weights.items()}}
        y = _forward(args)
    with _jax.named_scope("loss_head"):
        err = _jnp.square(y.astype(_jnp.float32) - loss_target)
        return 0.5 * _jnp.sum(_jnp.mean(err, axis=-1)) if err.ndim else 0.5 * err


def _adamw(w, g, m, v):
    m = ADAM_B1 * m + (1.0 - ADAM_B1) * g
    v = ADAM_B2 * v + (1.0 - ADAM_B2) * _jnp.square(g)
    m_hat = m / (1.0 - ADAM_B1 ** ADAM_STEP)
    v_hat = v / (1.0 - ADAM_B2 ** ADAM_STEP)
    delta = -ADAM_LR * (m_hat / (_jnp.sqrt(v_hat) + ADAM_EPS) + ADAM_WD * w)
    return delta, m, v


def reference(x, c, w_ada, b_ada, g_norm_mix, w_in, w_conv, b_conv, w_rg_a, b_rg_a, w_rg_x, b_rg_x, lru_lambda, g_attn_out, g_lru_out, w_out, g_norm_mlp, w_mlp_in, w_mlp_out, g_norm_final, loss_target, m_w_ada, m_b_ada, m_g_norm_mix, m_w_in, m_w_conv, m_b_conv, m_w_rg_a, m_b_rg_a, m_w_rg_x, m_b_rg_x, m_lru_lambda, m_g_attn_out, m_g_lru_out, m_w_out, m_g_norm_mlp, m_w_mlp_in, m_w_mlp_out, m_g_norm_final, v_w_ada, v_b_ada, v_g_norm_mix, v_w_in, v_w_conv, v_b_conv, v_w_rg_a, v_b_rg_a, v_w_rg_x, v_b_rg_x, v_lru_lambda, v_g_attn_out, v_g_lru_out, v_w_out, v_g_norm_mlp, v_w_mlp_in, v_w_mlp_out, v_g_norm_final):
    given = dict(x=x, c=c, w_ada=w_ada, b_ada=b_ada, g_norm_mix=g_norm_mix, w_in=w_in, w_conv=w_conv, b_conv=b_conv, w_rg_a=w_rg_a, b_rg_a=b_rg_a, w_rg_x=w_rg_x, b_rg_x=b_rg_x, lru_lambda=lru_lambda, g_attn_out=g_attn_out, g_lru_out=g_lru_out, w_out=w_out, g_norm_mlp=g_norm_mlp, w_mlp_in=w_mlp_in, w_mlp_out=w_mlp_out, g_norm_final=g_norm_final, loss_target=loss_target, m_w_ada=m_w_ada, m_b_ada=m_b_ada, m_g_norm_mix=m_g_norm_mix, m_w_in=m_w_in, m_w_conv=m_w_conv, m_b_conv=m_b_conv, m_w_rg_a=m_w_rg_a, m_b_rg_a=m_b_rg_a, m_w_rg_x=m_w_rg_x, m_b_rg_x=m_b_rg_x, m_lru_lambda=m_lru_lambda, m_g_attn_out=m_g_attn_out, m_g_lru_out=m_g_lru_out, m_w_out=m_w_out, m_g_norm_mlp=m_g_norm_mlp, m_w_mlp_in=m_w_mlp_in, m_w_mlp_out=m_w_mlp_out, m_g_norm_final=m_g_norm_final, v_w_ada=v_w_ada, v_b_ada=v_b_ada, v_g_norm_mix=v_g_norm_mix, v_w_in=v_w_in, v_w_conv=v_w_conv, v_b_conv=v_b_conv, v_w_rg_a=v_w_rg_a, v_b_rg_a=v_b_rg_a, v_w_rg_x=v_w_rg_x, v_b_rg_x=v_b_rg_x, v_lru_lambda=v_lru_lambda, v_g_attn_out=v_g_attn_out, v_g_lru_out=v_g_lru_out, v_w_out=v_w_out, v_g_norm_mlp=v_g_norm_mlp, v_w_mlp_in=v_w_mlp_in, v_w_mlp_out=v_w_mlp_out, v_g_norm_final=v_g_norm_final)
    weights = {n: given[n] for n in TWIN_WEIGHTS}
    shared = {n: given[n] for n in SHARED_INPUTS}
    per_example = {n: given[n] for n in ['x', 'c']}
    grad_fn = _jax.value_and_grad(_loss, argnums=(0, 1))

    def one_microbatch(ex, loss_target):
        ex = dict(ex)
        diff = ex.pop(TWIN_DIFF_INPUT)
        return grad_fn(weights, diff, {**shared, **ex}, loss_target)

    if N_MICROBATCH == 1:
        loss, (grad_w, grad_x) = one_microbatch(per_example, given["loss_target"])
    else:
        def body(carry, xs):
            loss_sum, grad_sum = carry
            l_k, (gw_k, gx_k) = one_microbatch(xs[0], xs[1])
            with _jax.named_scope("update"):
                return (loss_sum + l_k, _jax.tree.map(_jnp.add, grad_sum, gw_k)), gx_k

        init = (_jnp.zeros((), _jnp.float32), _jax.tree.map(_jnp.zeros_like, weights))
        (loss, grad_w), grad_x = _jax.lax.scan(body, init, (per_example, given["loss_target"]))
    with _jax.named_scope("update"):
        delta_w, new_m, new_v = {}, {}, {}
        for n in TWIN_WEIGHTS:
            delta_w[n], new_m[n], new_v[n] = _adamw(weights[n], grad_w[n], given["m_" + n], given["v_" + n])
    return (loss, grad_x, *[grad_w[n] for n in TWIN_WEIGHTS], *[delta_w[n] for n in TWIN_WEIGHTS],
            *[new_m[n] for n in TWIN_WEIGHTS], *[new_v[n] for n in TWIN_WEIGHTS])
```

```python
import functools
import math

import jax
import jax.numpy as jnp
from jax import lax
from jax.experimental import pallas as pl
from jax.experimental.pallas import tpu as pltpu

F32 = jnp.float32
BF16 = jnp.bfloat16
SDS = jax.ShapeDtypeStruct
MESH = pl.DeviceIdType.MESH

EPS = 1e-6
HEAD = 128
N_MOD = 6
CONV_TAPS = 4
LRU_C = 8.0
ADAM_LR, ADAM_B1, ADAM_B2, ADAM_EPS, ADAM_WD, ADAM_STEP = 0.001, 0.9, 0.999, 1e-08, 0.01, 10
N_DEV = 8
N_CHIP = 4
LANES = 128
SUBLANES = 8
VMEM_LIMIT = 56 * 1024 * 1024
PACK_ROWS = 256


def _tile(dim, pref):
    t = min(dim, pref)
    while dim % t:
        t -= LANES
    return t


def _params(sem=None):
    return pltpu.CompilerParams(dimension_semantics=sem, vmem_limit_bytes=VMEM_LIMIT)


def _sigmoid(x):
    return 1.0 / (1.0 + jnp.exp(-x))


def _log_sigmoid(x):
    return jnp.minimum(x, 0.0) - jnp.log(1.0 + jnp.exp(-jnp.abs(x)))


def _gelu_parts(x):
    k0, k1 = math.sqrt(2.0 / math.pi), 0.044715
    t = jnp.tanh(k0 * (x + k1 * x * x * x))
    val = 0.5 * x * (1.0 + t)
    der = 0.5 * (1.0 + t) + 0.5 * x * (1.0 - t * t) * k0 * (1.0 + 3.0 * k1 * x * x)
    return val, der


def _dot(a, b):
    return jnp.dot(a, b, preferred_element_type=F32)


def _dot_nt(a, b):
    return lax.dot_general(a, b, (((1,), (1,)), ((), ())), preferred_element_type=F32)


def _dot_tn(a, b):
    return lax.dot_general(a, b, (((0,), (0,)), ((), ())), preferred_element_type=F32)


def _split_dot(x, tri):
    hi = x.astype(BF16)
    lo = (x - hi.astype(F32)).astype(BF16)
    return _dot(hi, tri) + _dot(lo, tri)


def _matmul(name, a, b, mode, m, n, k, out_dtypes, *, b_off=0, extras=(), extra_kinds=(), epilogue=None,
            tm=1024, tn=1024, tk=512):
    tm, tn, tk = _tile(m, tm), _tile(math.gcd(n, b_off) if b_off else n, tn), _tile(k, tk)
    assert b_off % tn == 0
    nk = k // tk
    n_ex, n_out = len(extras), len(out_dtypes)

    def body(a_ref, b_ref, *rest):
        ex, outs, acc = rest[:n_ex], rest[n_ex:n_ex + n_out], rest[-1]
        kk = pl.program_id(2)

        @pl.when(kk == 0)
        def _():
            acc[...] = jnp.zeros_like(acc)

        if mode == "nn":
            acc[...] += _dot(a_ref[...], b_ref[...])
        elif mode == "nt":
            acc[...] += _dot_nt(a_ref[...], b_ref[...])
        else:
            acc[...] += _dot_tn(a_ref[...], b_ref[...])

        @pl.when(kk == nk - 1)
        def _():
            res = epilogue(acc[...], *[e[...] for e in ex]) if epilogue else (acc[...],)
            for o, r in zip(outs, res):
                o[...] = r.astype(o.dtype)

    if mode == "nn":
        a_spec = pl.BlockSpec((tm, tk), lambda i, j, kk: (i, kk))
        b_spec = pl.BlockSpec((tk, tn), lambda i, j, kk: (kk, j + b_off // tn))
    elif mode == "nt":
        a_spec = pl.BlockSpec((tm, tk), lambda i, j, kk: (i, kk))
        b_spec = pl.BlockSpec((tn, tk), lambda i, j, kk: (j, kk + b_off // tk))
    else:
        a_spec = pl.BlockSpec((tk, tm), lambda i, j, kk: (kk, i))
        b_spec = pl.BlockSpec((tk, tn), lambda i, j, kk: (kk, j))
    tile_spec = pl.BlockSpec((tm, tn), lambda i, j, kk: (i, j))
    row_spec = pl.BlockSpec((1, tn), lambda i, j, kk: (0, j))
    ex_specs = [tile_spec if kind == "tile" else row_spec for kind in extra_kinds]
    outs = pl.pallas_call(
        body, grid=(m // tm, n // tn, nk),
        in_specs=[a_spec, b_spec] + ex_specs,
        out_specs=tuple(tile_spec for _ in out_dtypes),
        out_shape=tuple(SDS((m, n), dt) for dt in out_dtypes),
        scratch_shapes=[pltpu.VMEM((tm, tn), F32)],
        compiler_params=_params(("parallel", "parallel", "arbitrary")),
        name=name,
    )(a, b, *extras)
    return outs


def _row_specs(s, d, tr):
    row = pl.BlockSpec((tr, d), lambda i: (i, 0))
    vec = pl.BlockSpec((1, d), lambda i: (0, 0))
    col = pl.BlockSpec((tr, 1), lambda i: (i, 0))
    return row, vec, col


def _norm_mod_fwd(name, x, g, sc, sh):
    s, d = x.shape
    tr = _tile(s, 256)
    row, vec, col = _row_specs(s, d, tr)

    def body(x_ref, g_ref, sc_ref, sh_ref, h_ref, r_ref):
        xv = x_ref[...]
        r = lax.rsqrt(jnp.mean(xv * xv, axis=-1, keepdims=True) + EPS)
        h_ref[...] = ((xv * r * g_ref[...]) * (1.0 + sc_ref[...]) + sh_ref[...]).astype(BF16)
        r_ref[...] = r

    return pl.pallas_call(
        body, grid=(s // tr,), in_specs=[row, vec, vec, vec], out_specs=(row, col),
        out_shape=(SDS((s, d), BF16), SDS((s, 1), F32)),
        compiler_params=_params(("parallel",)), name=name)(x, g, sc, sh)


def _norm_mod_bwd(name, dh, xin, rstd, g, sc, dres):
    s, d = xin.shape
    tr = _tile(s, 256)
    row, vec, col = _row_specs(s, d, tr)

    def body(dh_ref, x_ref, r_ref, g_ref, sc_ref, dres_ref, dx_ref, dsh_ref, dsc_ref, dg_ref):
        @pl.when(pl.program_id(0) == 0)
        def _():
            dsh_ref[...] = jnp.zeros_like(dsh_ref)
            dsc_ref[...] = jnp.zeros_like(dsc_ref)
            dg_ref[...] = jnp.zeros_like(dg_ref)

        dh_v, xv, r, gv = dh_ref[...], x_ref[...], r_ref[...], g_ref[...]
        n0 = xv * r
        dsh_ref[...] += jnp.sum(dh_v, axis=0, keepdims=True)
        dsc_ref[...] += jnp.sum(dh_v * (n0 * gv), axis=0, keepdims=True)
        dn = dh_v * (1.0 + sc_ref[...])
        dg_ref[...] += jnp.sum(dn * n0, axis=0, keepdims=True)
        gy = dn * gv
        dot = jnp.mean(gy * xv, axis=-1, keepdims=True)
        dx_ref[...] = dres_ref[...] + r * gy - xv * (r * r * r * dot)

    return pl.pallas_call(
        body, grid=(s // tr,), in_specs=[row, row, col, vec, vec, row], out_specs=(row, vec, vec, vec),
        out_shape=(SDS((s, d), F32), SDS((1, d), F32), SDS((1, d), F32), SDS((1, d), F32)),
        compiler_params=_params(("arbitrary",)), name=name)(dh, xin, rstd, g, sc, dres)


def _gate_bwd(name, dx, y, gt):
    s, d = dx.shape
    tr = _tile(s, 256)
    row, vec, _ = _row_specs(s, d, tr)

    def body(dx_ref, y_ref, gt_ref, dy_ref, dgt_ref):
        @pl.when(pl.program_id(0) == 0)
        def _():
            dgt_ref[...] = jnp.zeros_like(dgt_ref)

        dxv = dx_ref[...]
        dy_ref[...] = (gt_ref[...] * dxv).astype(BF16)
        dgt_ref[...] += jnp.sum(dxv * y_ref[...], axis=0, keepdims=True)

    return pl.pallas_call(
        body, grid=(s // tr,), in_specs=[row, row, vec], out_specs=(row, vec),
        out_shape=(SDS((s, d), BF16), SDS((1, d), F32)),
        compiler_params=_params(("arbitrary",)), name=name)(dx, y, gt)


def _final_loss(x2, gf, tgt):
    s, d = x2.shape
    tr = _tile(s, 256)
    row, vec, _ = _row_specs(s, d, tr)
    lrow = pl.BlockSpec((1, LANES), lambda i: (0, 0))

    def body(x_ref, g_ref, t_ref, dx_ref, loss_ref, dg_ref):
        @pl.when(pl.program_id(0) == 0)
        def _():
            loss_ref[...] = jnp.zeros_like(loss_ref)
            dg_ref[...] = jnp.zeros_like(dg_ref)

        xv, gv = x_ref[...], g_ref[...]
        r = lax.rsqrt(jnp.mean(xv * xv, axis=-1, keepdims=True) + EPS)
        n0 = xv * r
        err = n0 * gv - t_ref[...]
        loss_ref[...] += jnp.sum(err * err) * (0.5 / d)
        dy = err * (1.0 / d)
        dg_ref[...] += jnp.sum(dy * n0, axis=0, keepdims=True)
        gy = dy * gv
        dot = jnp.mean(gy * xv, axis=-1, keepdims=True)
        dx_ref[...] = r * gy - xv * (r * r * r * dot)

    return pl.pallas_call(
        body, grid=(s // tr,), in_specs=[row, vec, row], out_specs=(row, lrow, vec),
        out_shape=(SDS((s, d), F32), SDS((1, LANES), F32), SDS((1, d), F32)),
        compiler_params=_params(("arbitrary",)), name="final_loss")(x2, gf, tgt)


def _mix_norm_fwd(oa, ol, ga, gl):
    s, w = oa.shape
    tr = _tile(s, 256)
    row, vec, col = _row_specs(s, w, tr)

    def body(oa_ref, ol_ref, ga_ref, gl_ref, mx_ref, ra_ref, rl_ref):
        a, l = oa_ref[...], ol_ref[...]
        ra = lax.rsqrt(jnp.mean(a * a, axis=-1, keepdims=True) + EPS)
        rl = lax.rsqrt(jnp.mean(l * l, axis=-1, keepdims=True) + EPS)
        mx_ref[:, :w] = (a * ra * ga_ref[...]).astype(BF16)
        mx_ref[:, w:] = (l * rl * gl_ref[...]).astype(BF16)
        ra_ref[...] = ra
        rl_ref[...] = rl

    return pl.pallas_call(
        body, grid=(s // tr,), in_specs=[row, row, vec, vec],
        out_specs=(pl.BlockSpec((tr, 2 * w), lambda i: (i, 0)), col, col),
        out_shape=(SDS((s, 2 * w), BF16), SDS((s, 1), F32), SDS((s, 1), F32)),
        compiler_params=_params(("parallel",)), name="mix_norm_fwd")(oa, ol, ga, gl)


def _mix_norm_bwd(dmx, oa, ol, ra, rl, ga, gl):
    s, w = oa.shape
    tr = _tile(s, 256)
    row, vec, col = _row_specs(s, w, tr)

    def body(dm_ref, oa_ref, ol_ref, ra_ref, rl_ref, ga_ref, gl_ref, doa_ref, dol_ref, dga_ref, dgl_ref):
        @pl.when(pl.program_id(0) == 0)
        def _():
            dga_ref[...] = jnp.zeros_like(dga_ref)
            dgl_ref[...] = jnp.zeros_like(dgl_ref)

        def one(dy, xv, r, gv, dg_ref):
            dg_ref[...] += jnp.sum(dy * (xv * r), axis=0, keepdims=True)
            gy = dy * gv
            dot = jnp.mean(gy * xv, axis=-1, keepdims=True)
            return r * gy - xv * (r * r * r * dot)

        doa_ref[...] = one(dm_ref[:, :w], oa_ref[...], ra_ref[...], ga_ref[...], dga_ref).astype(BF16)
        dol_ref[...] = one(dm_ref[:, w:], ol_ref[...], rl_ref[...], gl_ref[...], dgl_ref)

    return pl.pallas_call(
        body, grid=(s // tr,),
        in_specs=[pl.BlockSpec((tr, 2 * w), lambda i: (i, 0)), row, row, col, col, vec, vec],
        out_specs=(row, row, vec, vec),
        out_shape=(SDS((s, w), BF16), SDS((s, w), F32), SDS((1, w), F32), SDS((1, w), F32)),
        compiler_params=_params(("arbitrary",)), name="mix_norm_bwd")(dmx, oa, ol, ra, rl, ga, gl)


def _attn_block(q, k_blk, off, row_id, tq, tk, tri_after, csum):
    z = _dot_nt(q, k_blk) * (HEAD ** -0.5)
    col_id = off + lax.broadcasted_iota(jnp.int32, (tq, tk), 1)
    causal = col_id < row_id
    lb = _log_sigmoid(z)
    ls = jnp.where(causal, lb - z, 0.0)
    after = _split_dot(ls, tri_after) + csum
    w = jnp.where(causal, jnp.exp(lb + after), 0.0)
    return lb, ls, w, causal


def _attn_tiles(s):
    t = 256 if s >= 1024 else 128
    return t, t


def _attn_fwd(qkv, n_heads):
    s = qkv.shape[0]
    tq, tk = _attn_tiles(s)

    def body(q_ref, k_ref, v_ref, o_ref):
        qi = pl.program_id(1)
        q = q_ref[...]
        row_id = qi * tq + lax.broadcasted_iota(jnp.int32, (tq, tk), 0)
        r_i = lax.broadcasted_iota(jnp.int32, (tk, tk), 0)
        c_i = lax.broadcasted_iota(jnp.int32, (tk, tk), 1)
        tri_after = (r_i > c_i).astype(BF16)

        def step(it, carry):
            csum, o = carry
            off = pl.multiple_of((qi - it) * tk, tk)
            _, ls, w, _ = _attn_block(q, k_ref[pl.ds(off, tk), :], off, row_id, tq, tk, tri_after, csum)
            o = o + _dot(w.astype(BF16), v_ref[pl.ds(off, tk), :])
            return csum + jnp.sum(ls, axis=1, keepdims=True), o

        _, o = lax.fori_loop(0, qi + 1, step, (jnp.zeros((tq, 1), F32), jnp.zeros((tq, HEAD), F32)))
        o_ref[...] = o

    h = n_heads
    return pl.pallas_call(
        body, grid=(h, s // tq),
        in_specs=[pl.BlockSpec((tq, HEAD), lambda hh, i: (i, hh)),
                  pl.BlockSpec((s, HEAD), lambda hh, i: (0, h + hh)),
                  pl.BlockSpec((s, HEAD), lambda hh, i: (0, 2 * h + hh))],
        out_specs=pl.BlockSpec((tq, HEAD), lambda hh, i: (i, hh)),
        out_shape=SDS((s, h * HEAD), F32),
        compiler_params=_params(("parallel", "parallel")), name="attn_fwd")(qkv, qkv, qkv)


def _attn_bwd(qkv, do, n_heads):
    s = qkv.shape[0]
    tq, tk = _attn_tiles(s)
    nq = s // tq
    scale = HEAD ** -0.5

    def body(q_ref, k_ref, v_ref, do_ref, dq_ref, dk_ref, dv_ref, e_s, sg_s, dk_acc, dv_acc):
        qi = pl.program_id(1)

        @pl.when(qi == 0)
        def _():
            dk_acc[...] = jnp.zeros_like(dk_acc)
            dv_acc[...] = jnp.zeros_like(dv_acc)

        q = q_ref[...]
        dout = do_ref[...]
        row_id = qi * tq + lax.broadcasted_iota(jnp.int32, (tq, tk), 0)
        r_i = lax.broadcasted_iota(jnp.int32, (tk, tk), 0)
        c_i = lax.broadcasted_iota(jnp.int32, (tk, tk), 1)
        tri_after = (r_i > c_i).astype(BF16)
        tri_before = (r_i < c_i).astype(BF16)

        def pass1(it, csum):
            kb = qi - it
            off = pl.multiple_of(kb * tk, tk)
            lb, ls, w, _ = _attn_block(q, k_ref[pl.ds(off, tk), :], off, row_id, tq, tk, tri_after, csum)
            dw = _dot_nt(dout, v_ref[pl.ds(off, tk), :])
            e_s[kb] = dw * w
            sg_s[kb] = jnp.exp(lb)
            dv_acc[pl.ds(off, tk), :] += _dot_tn(w.astype(BF16), dout)
            return csum + jnp.sum(ls, axis=1, keepdims=True)

        lax.fori_loop(0, qi + 1, pass1, jnp.zeros((tq, 1), F32))

        def pass2(kb, carry):
            esum, dq = carry
            off = pl.multiple_of(kb * tk, tk)
            e, sg = e_s[kb], sg_s[kb]
            col_id = off + lax.broadcasted_iota(jnp.int32, (tq, tk), 1)
            before = _split_dot(e, tri_before) + esum
            dz = (e * (1.0 - sg) - jnp.where(col_id < row_id, before * sg, 0.0)) * scale
            dzb = dz.astype(BF16)
            dq = dq + _dot(dzb, k_ref[pl.ds(off, tk), :])
            dk_acc[pl.ds(off, tk), :] += _dot_tn(dzb, q)
            return esum + jnp.sum(e, axis=1, keepdims=True), dq

        _, dq = lax.fori_loop(0, qi + 1, pass2, (jnp.zeros((tq, 1), F32), jnp.zeros((tq, HEAD), F32)))
        dq_ref[...] = dq.astype(BF16)

        @pl.when(qi == nq - 1)
        def _():
            dk_ref[...] = dk_acc[...].astype(BF16)
            dv_ref[...] = dv_acc[...].astype(BF16)

    h = n_heads
    blk = pl.BlockSpec((tq, HEAD), lambda hh, i: (i, hh))
    full = pl.BlockSpec((s, HEAD), lambda hh, i: (0, hh))
    return pl.pallas_call(
        body, grid=(h, nq),
        in_specs=[blk,
                  pl.BlockSpec((s, HEAD), lambda hh, i: (0, h + hh)),
                  pl.BlockSpec((s, HEAD), lambda hh, i: (0, 2 * h + hh)),
                  blk],
        out_specs=(blk, full, full),
        out_shape=(SDS((s, h * HEAD), BF16),) * 3,
        scratch_shapes=[pltpu.VMEM((s // tk, tq, tk), F32), pltpu.VMEM((s // tk, tq, tk), F32),
                        pltpu.VMEM((s, HEAD), F32), pltpu.VMEM((s, HEAD), F32)],
        compiler_params=_params(("parallel", "arbitrary")), name="attn_bwd")(qkv, qkv, qkv, do)


def _lru_chunk(s):
    return 256 if s >= 1024 else 128


def _lru_gates(xc, wa, ba, wx, bx, sp):
    xb = xc.astype(BF16)
    r = _sigmoid(_dot(xb, wa) + ba)
    ig = _sigmoid(_dot(xb, wx) + bx)
    la = -LRU_C * r * sp
    a = jnp.exp(la)
    t = jnp.tanh(la)
    mult = jnp.sqrt(-2.0 * t / (1.0 - t))
    return r, ig, a, mult


def _softplus_neg(lam):
    return jnp.maximum(-lam, 0.0) + jnp.log(1.0 + jnp.exp(-jnp.abs(lam)))


def _lru_specs(s, n_blocks):
    seq0 = pl.BlockSpec((s, HEAD), lambda h: (0, h))
    seq1 = pl.BlockSpec((s, HEAD), lambda h: (0, n_blocks + h))
    taps = pl.BlockSpec((CONV_TAPS, HEAD), lambda h: (0, h))
    vec = pl.BlockSpec((1, HEAD), lambda h: (0, h))
    mat = pl.BlockSpec((None, HEAD, HEAD), lambda h: (h, 0, 0))
    return seq0, seq1, taps, vec, mat


def _lru_fwd(xrg, wconv, bconv, wa, ba, wx, bx, lam):
    s = xrg.shape[0]
    nb = wa.shape[0]
    tc = _lru_chunk(s)
    seq0, seq1, taps, vec, mat = _lru_specs(s, nb)
    pad = SUBLANES

    def body(xr_ref, xg_ref, wc_ref, bc_ref, wa_ref, ba_ref, wx_ref, bx_ref, lam_ref, o_ref, h_ref, pad_s, a_s, u_s):
        pad_s[0:pad, :] = jnp.zeros((pad, HEAD), F32)
        pad_s[pad:pad + s, :] = xr_ref[...]
        wab, wxb = wa_ref[...].astype(BF16), wx_ref[...].astype(BF16)
        sp = _softplus_neg(lam_ref[...])
        for c in range(s // tc):
            base = c * tc
            xc = bc_ref[...] + sum(wc_ref[i:i + 1, :] * pad_s[pl.ds(base + pad - (CONV_TAPS - 1) + i, tc), :]
                                   for i in range(CONV_TAPS))
            _, ig, a, mult = _lru_gates(xc, wab, ba_ref[...], wxb, bx_ref[...], sp)
            a_s[base:base + tc, :] = a
            u_s[base:base + tc, :] = mult * (ig * xc)

        row = lax.broadcasted_iota(jnp.int32, (SUBLANES, HEAD), 0)

        def chunk(ci, hprev):
            off = pl.multiple_of(ci * SUBLANES, SUBLANES)
            a8, b8 = a_s[pl.ds(off, SUBLANES), :], u_s[pl.ds(off, SUBLANES), :]
            for d in (1, 2, 4):
                a_sh = jnp.where(row < d, 1.0, pltpu.roll(a8, d, 0))
                b_sh = jnp.where(row < d, 0.0, pltpu.roll(b8, d, 0))
                b8 = a8 * b_sh + b8
                a8 = a8 * a_sh
            h8 = a8 * hprev + b8
            h_ref[pl.ds(off, SUBLANES), :] = h8
            return h8[SUBLANES - 1:SUBLANES, :]

        lax.fori_loop(0, s // SUBLANES, chunk, jnp.zeros((1, HEAD), F32), unroll=8)
        for c in range(s // tc):
            sl = slice(c * tc, (c + 1) * tc)
            gel, _ = _gelu_parts(xg_ref[sl, :])
            o_ref[sl, :] = h_ref[sl, :] * gel

    return pl.pallas_call(
        body, grid=(nb,),
        in_specs=[seq0, seq1, taps, vec, mat, vec, mat, vec, vec],
        out_specs=(seq0, seq0),
        out_shape=(SDS((s, nb * HEAD), F32), SDS((s, nb * HEAD), F32)),
        scratch_shapes=[pltpu.VMEM((s + pad, HEAD), F32), pltpu.VMEM((s, HEAD), F32), pltpu.VMEM((s, HEAD), F32)],
        compiler_params=_params(("parallel",)), name="lru_fwd")(xrg, xrg, wconv, bconv, wa, ba, wx, bx, lam)


def _lru_bwd(xrg, dol, hseq, wconv, bconv, wa, ba, wx, bx, lam):
    s = xrg.shape[0]
    nb = wa.shape[0]
    tc = _lru_chunk(s)
    seq0, seq1, taps, vec, mat = _lru_specs(s, nb)
    pad = SUBLANES

    def body(xr_ref, xg_ref, do_ref, h_ref, wc_ref, bc_ref, wa_ref, ba_ref, wx_ref, bx_ref, lam_ref,
             dxr_ref, dxg_ref, dwc_ref, dbc_ref, dwa_ref, dba_ref, dwx_ref, dbx_ref, dlam_ref,
             pad_s, hp_s, a_s, g_s, da_s, dxc_s):
        pad_s[0:pad, :] = jnp.zeros((pad, HEAD), F32)
        pad_s[pad:pad + s, :] = xr_ref[...]
        hp_s[0:pad, :] = jnp.zeros((pad, HEAD), F32)
        hp_s[pad:pad + s, :] = h_ref[...]
        a_s[s:s + pad, :] = jnp.zeros((pad, HEAD), F32)
        dxc_s[s:s + pad, :] = jnp.zeros((pad, HEAD), F32)
        wab, wxb = wa_ref[...].astype(BF16), wx_ref[...].astype(BF16)
        lam_v = lam_ref[...]
        sp = _softplus_neg(lam_v)

        def conv_in(c):
            base = c * tc
            wins = [pad_s[pl.ds(base + pad - (CONV_TAPS - 1) + i, tc), :] for i in range(CONV_TAPS)]
            xc = bc_ref[...] + sum(wc_ref[i:i + 1, :] * wins[i] for i in range(CONV_TAPS))
            return xc, wins

        for c in range(s // tc):
            sl = slice(c * tc, (c + 1) * tc)
            xc, _ = conv_in(c)
            _, _, a, _ = _lru_gates(xc, wab, ba_ref[...], wxb, bx_ref[...], sp)
            a_s[sl, :] = a
            gel, dgel = _gelu_parts(xg_ref[sl, :])
            dov = do_ref[sl, :]
            g_s[sl, :] = dov * gel
            dxg_ref[sl, :] = (dov * h_ref[sl, :] * dgel).astype(BF16)

        row = lax.broadcasted_iota(jnp.int32, (SUBLANES, HEAD), 0)
        n_chunks = s // SUBLANES

        def chunk(it, gnext):
            ci = n_chunks - 1 - it
            off = pl.multiple_of(ci * SUBLANES, SUBLANES)
            a8 = a_s[pl.ds(off, SUBLANES), :]
            a8n = a_s[pl.ds(off + SUBLANES, SUBLANES), :]
            c8 = pltpu.roll(jnp.where(row == 0, a8n, a8), SUBLANES - 1, 0)
            g8 = g_s[pl.ds(off, SUBLANES), :]
            for d in (1, 2, 4):
                c_sh = jnp.where(row >= SUBLANES - d, 1.0, pltpu.roll(c8, SUBLANES - d, 0))
                g_sh = jnp.where(row >= SUBLANES - d, 0.0, pltpu.roll(g8, SUBLANES - d, 0))
                g8 = c8 * g_sh + g8
                c8 = c8 * c_sh
            g8 = g8 + c8 * gnext
            g_s[pl.ds(off, SUBLANES), :] = g8
            h8 = hp_s[pl.ds(off + pad, SUBLANES), :]
            h8p = hp_s[pl.ds(off, SUBLANES), :]
            da_s[pl.ds(off, SUBLANES), :] = g8 * pltpu.roll(jnp.where(row == SUBLANES - 1, h8p, h8), 1, 0)
            return g8[0:1, :]

        lax.fori_loop(0, n_chunks, chunk, jnp.zeros((1, HEAD), F32), unroll=8)

        dsp = jnp.zeros((1, HEAD), F32)
        dbc = jnp.zeros((1, HEAD), F32)
        dba = jnp.zeros((1, HEAD), F32)
        dbx = jnp.zeros((1, HEAD), F32)
        dwa = jnp.zeros((HEAD, HEAD), F32)
        dwx = jnp.zeros((HEAD, HEAD), F32)
        dwc = [jnp.zeros((1, HEAD), F32) for _ in range(CONV_TAPS)]
        for c in range(s // tc):
            sl = slice(c * tc, (c + 1) * tc)
            xc, wins = conv_in(c)
            r, ig, a, mult = _lru_gates(xc, wab, ba_ref[...], wxb, bx_ref[...], sp)
            du, da = g_s[sl, :], da_s[sl, :]
            d_ix = du * mult
            dla = da * a - (du * ig * xc) * (a * a / mult)
            dsp = dsp + jnp.sum(dla * r, axis=0, keepdims=True) * (-LRU_C)
            dpa = (dla * (-LRU_C * sp)) * r * (1.0 - r)
            dpx = (d_ix * xc) * ig * (1.0 - ig)
            dpab, dpxb, xb = dpa.astype(BF16), dpx.astype(BF16), xc.astype(BF16)
            dxc = d_ix * ig + _dot_nt(dpab, wab) + _dot_nt(dpxb, wxb)
            dwa = dwa + _dot_tn(xb, dpab)
            dwx = dwx + _dot_tn(xb, dpxb)
            dba = dba + jnp.sum(dpa, axis=0, keepdims=True)
            dbx = dbx + jnp.sum(dpx, axis=0, keepdims=True)
            dbc = dbc + jnp.sum(dxc, axis=0, keepdims=True)
            for i in range(CONV_TAPS):
                dwc[i] = dwc[i] + jnp.sum(dxc * wins[i], axis=0, keepdims=True)
            dxc_s[sl, :] = dxc

        for c in range(s // tc):
            base = c * tc
            dxr = sum(wc_ref[i:i + 1, :] * dxc_s[pl.ds(base + (CONV_TAPS - 1) - i, tc), :] for i in range(CONV_TAPS))
            dxr_ref[base:base + tc, :] = dxr.astype(BF16)

        for i in range(CONV_TAPS):
            dwc_ref[i:i + 1, :] = dwc[i]
        dbc_ref[...] = dbc
        dwa_ref[...] = dwa
        dwx_ref[...] = dwx
        dba_ref[...] = dba
        dbx_ref[...] = dbx
        dlam_ref[...] = dsp * (-_sigmoid(-lam_v))

    w = nb * HEAD
    return pl.pallas_call(
        body, grid=(nb,),
        in_specs=[seq0, seq1, seq0, seq0, taps, vec, mat, vec, mat, vec, vec],
        out_specs=(seq0, seq0, taps, vec, mat, vec, mat, vec, vec),
        out_shape=(SDS((s, w), BF16), SDS((s, w), BF16), SDS((CONV_TAPS, w), F32), SDS((1, w), F32),
                   SDS((nb, HEAD, HEAD), F32), SDS((1, w), F32), SDS((nb, HEAD, HEAD), F32), SDS((1, w), F32),
                   SDS((1, w), F32)),
        scratch_shapes=[pltpu.VMEM((s + pad, HEAD), F32), pltpu.VMEM((s + pad, HEAD), F32),
                        pltpu.VMEM((s + pad, HEAD), F32), pltpu.VMEM((s, HEAD), F32),
                        pltpu.VMEM((s, HEAD), F32), pltpu.VMEM((s + pad, HEAD), F32)],
        compiler_params=_params(("parallel",)), name="lru_bwd",
    )(xrg, xrg, dol, hseq, wconv, bconv, wa, ba, wx, bx, lam)


def _ada_mod(c_all, w_sh, b_sh):
    n_ex, d = c_all.shape
    n = w_sh.shape[1]
    tn = _tile(n, 512)

    def body(c_ref, w_ref, b_ref, mod_ref, act_ref):
        cv = c_ref[...]
        act = cv * _sigmoid(cv)
        act_ref[...] = act
        mod_ref[...] = _dot(act.astype(BF16), w_ref[...].astype(BF16)) + b_ref[...]

    return pl.pallas_call(
        body, grid=(n // tn,),
        in_specs=[pl.BlockSpec((n_ex, d), lambda j: (0, 0)), pl.BlockSpec((d, tn), lambda j: (0, j)),
                  pl.BlockSpec((1, tn), lambda j: (0, j))],
        out_specs=(pl.BlockSpec((n_ex, tn), lambda j: (0, j)), pl.BlockSpec((n_ex, d), lambda j: (0, 0))),
        out_shape=(SDS((n_ex, n), F32), SDS((n_ex, d), F32)),
        compiler_params=_params(("arbitrary",)), name="ada_mod")(c_all, w_sh, b_sh)


def _adamw_math(w, g, m, v):
    m = ADAM_B1 * m + (1.0 - ADAM_B1) * g
    v = ADAM_B2 * v + (1.0 - ADAM_B2) * (g * g)
    m_hat = m / (1.0 - ADAM_B1 ** ADAM_STEP)
    v_hat = v / (1.0 - ADAM_B2 ** ADAM_STEP)
    delta = -ADAM_LR * (m_hat / (jnp.sqrt(v_hat) + ADAM_EPS) + ADAM_WD * w)
    return delta, m, v


def _adamw_plain(name, w, g, m, v):
    def body(w_ref, g_ref, m_ref, v_ref, d_ref, mo_ref, vo_ref):
        d_ref[...], mo_ref[...], vo_ref[...] = _adamw_math(w_ref[...], g_ref[...], m_ref[...], v_ref[...])

    return pl.pallas_call(body, out_shape=(SDS(w.shape, F32),) * 3, name=name)(w, g, m, v)


def _adamw_halves(name, c_arr, w, m, v, g_own, g_recv):
    r, n = w.shape
    rh = r // 2
    tr = _tile(rh, 256)
    nh = rh // tr

    def body(c_ref, w_ref, m_ref, v_ref, go_ref, gr_ref, g_ref, d_ref, mo_ref, vo_ref):
        own = (pl.program_id(0) // nh) == c_ref[0]
        g = jnp.where(own, go_ref[...], gr_ref[...])
        g_ref[...] = g
        d_ref[...], mo_ref[...], vo_ref[...] = _adamw_math(w_ref[...], g, m_ref[...], v_ref[...])

    full = pl.BlockSpec((tr, n), lambda i, c: (i, 0))
    half = pl.BlockSpec((tr, n), lambda i, c: (i % nh, 0))
    return pl.pallas_call(
        body,
        grid_spec=pltpu.PrefetchScalarGridSpec(
            num_scalar_prefetch=1, grid=(2 * nh,), in_specs=[full, full, full, half, half],
            out_specs=(full,) * 4),
        out_shape=(SDS((r, n), F32),) * 4,
        compiler_params=_params(("parallel",)), name=name)(c_arr, w, m, v, g_own, g_recv)


def _adamw_ada(w, m, v, act_t, dmod):
    d, n = w.shape
    n_ex = act_t.shape[1]
    tr = _tile(d, 256)

    def body(a_ref, dm_ref, w_ref, m_ref, v_ref, g_ref, d_ref, mo_ref, vo_ref):
        g = _dot(a_ref[...], dm_ref[...])
        g_ref[...] = g
        d_ref[...], mo_ref[...], vo_ref[...] = _adamw_math(w_ref[...], g, m_ref[...], v_ref[...])

    full = pl.BlockSpec((tr, n), lambda i: (i, 0))
    return pl.pallas_call(
        body, grid=(d // tr,),
        in_specs=[pl.BlockSpec((tr, n_ex), lambda i: (i, 0)), pl.BlockSpec((n_ex, n), lambda i: (0, 0)), full, full, full],
        out_specs=(full,) * 4, out_shape=(SDS((d, n), F32),) * 4,
        compiler_params=_params(("parallel",)), name="adamw_ada")(act_t, dmod, w, m, v)


def _small_reduce_adamw(parts, w, m, v):
    n_dev, r, _ = parts.shape
    tr = r if r <= PACK_ROWS else PACK_ROWS

    def body(p_ref, w_ref, m_ref, v_ref, g_ref, d_ref, mo_ref, vo_ref):
        g = p_ref[0]
        for k in range(1, n_dev):
            g = g + p_ref[k]
        g_ref[...] = g
        d_ref[...], mo_ref[...], vo_ref[...] = _adamw_math(w_ref[...], g, m_ref[...], v_ref[...])

    full = pl.BlockSpec((tr, LANES), lambda i: (i, 0))
    return pl.pallas_call(
        body, grid=(r // tr,),
        in_specs=[pl.BlockSpec((n_dev, tr, LANES), lambda i: (0, i, 0)), full, full, full],
        out_specs=(full,) * 4, out_shape=(SDS((r, LANES), F32),) * 4,
        compiler_params=_params(("parallel",)), name="small_reduce_adamw")(parts, w, m, v)


def _mesh_pos():
    return lax.axis_index("x"), lax.axis_index("y"), lax.axis_index("c")


def _other_chips(x, y):
    return [(1 - x, y), (x, 1 - y), (1 - x, 1 - y)]


def _all_gather_small(name, blk):
    r, n = blk.shape

    def body(x_ref, out_ref, send_sems, recv_sems, local_sem):
        x, y, c = _mesh_pos()
        me, sibling = (x, y, c), (x, y, 1 - c)
        chips = _other_chips(x, y)

        def rows(px, py, pc):
            return out_ref.at[4 * px + 2 * py + pc]

        def copy(k, block, to, src=None):
            return pltpu.make_async_remote_copy(
                src_ref=rows(*block) if src is None else src, dst_ref=rows(*block),
                send_sem=send_sems.at[k], recv_sem=recv_sems.at[k], device_id=to, device_id_type=MESH)

        mine = pltpu.make_async_copy(x_ref, rows(*me), local_sem)
        mine.start()
        first = [copy(0, me, sibling, src=x_ref)]
        first += [copy(1 + j, me, (*chip, c), src=x_ref) for j, chip in enumerate(chips)]
        for cp in first:
            cp.start()
        passed = [copy(4 + j, (*chip, c), sibling) for j, chip in enumerate(chips)]
        for j, chip in enumerate(chips):
            copy(1 + j, (*chip, c), me).wait_recv()
            passed[j].start()
        copy(0, sibling, me).wait_recv()
        for j, chip in enumerate(chips):
            copy(4 + j, (*chip, 1 - c), me).wait_recv()
        for cp in first + passed:
            cp.wait_send()
        mine.wait()

    return pl.pallas_call(
        body, out_shape=SDS((N_DEV, r, n), blk.dtype),
        in_specs=[pl.BlockSpec(memory_space=pltpu.VMEM)], out_specs=pl.BlockSpec(memory_space=pltpu.VMEM),
        scratch_shapes=[pltpu.SemaphoreType.DMA((7,)), pltpu.SemaphoreType.DMA((7,)), pltpu.SemaphoreType.DMA],
        compiler_params=pltpu.CompilerParams(vmem_limit_bytes=VMEM_LIMIT), name=name)(blk)


_ANY = pl.BlockSpec(memory_space=pl.ANY)


def _gather_weights(shards, kinds):
    nw = len(shards)

    def body(*refs):
        sh, full = refs[:nw], refs[nw:2 * nw]
        send_sems, recv_sems, fsend_sems, frecv_sems, local_sems = refs[2 * nw:]
        x, y, c = _mesh_pos()
        sibling = (x, y, 1 - c)
        chips = _other_chips(x, y)

        def region(w, px, py, half):
            rows, cols = sh[w].shape
            j = 2 * px + py
            rh = rows // 2
            if kinds[w] == "col":
                return full[w].at[pl.ds(half * rh, rh), pl.ds(j * cols, cols)]
            return full[w].at[pl.ds(j * rows + half * rh, rh), :]

        def my_half(w, half):
            rh = sh[w].shape[0] // 2
            return sh[w].at[pl.ds(half * rh, rh), :]

        local = []
        for w in range(nw):
            for half in range(2):
                cp = pltpu.make_async_copy(my_half(w, half), region(w, x, y, half), local_sems.at[2 * w + half])
                cp.start()
                local.append(cp)
        sends = []
        for w in range(nw):
            for k, chip in enumerate(chips):
                cp = pltpu.make_async_remote_copy(
                    src_ref=my_half(w, c), dst_ref=region(w, x, y, c),
                    send_sem=send_sems.at[3 * w + k], recv_sem=recv_sems.at[3 * w + k],
                    device_id=(*chip, c), device_id_type=MESH)
                cp.start()
                sends.append(cp)
        passed = []
        for w in range(nw):
            for k, chip in enumerate(chips):
                landed = region(w, *chip, c)
                pltpu.make_async_remote_copy(
                    src_ref=landed, dst_ref=landed, send_sem=send_sems.at[3 * w + k], recv_sem=recv_sems.at[3 * w + k],
                    device_id=(*chip, c), device_id_type=MESH).wait_recv()
                cp = pltpu.make_async_remote_copy(
                    src_ref=landed, dst_ref=landed, send_sem=fsend_sems.at[3 * w + k], recv_sem=frecv_sems.at[3 * w + k],
                    device_id=sibling, device_id_type=MESH)
                cp.start()
                passed.append(cp)
        for w in range(nw):
            for k, chip in enumerate(chips):
                other = region(w, *chip, 1 - c)
                pltpu.make_async_remote_copy(
                    src_ref=other, dst_ref=other, send_sem=fsend_sems.at[3 * w + k], recv_sem=frecv_sems.at[3 * w + k],
                    device_id=sibling, device_id_type=MESH).wait_recv()
        for cp in sends + passed:
            cp.wait_send()
        for cp in local:
            cp.wait()

    out_shape = []
    for s_, kind in zip(shards, kinds):
        r, n = s_.shape
        out_shape.append(SDS((r, N_CHIP * n) if kind == "col" else (N_CHIP * r, n), s_.dtype))
    return pl.pallas_call(
        body, out_shape=tuple(out_shape), in_specs=[_ANY] * nw, out_specs=tuple([_ANY] * nw),
        scratch_shapes=[pltpu.SemaphoreType.DMA((3 * nw,)), pltpu.SemaphoreType.DMA((3 * nw,)),
                        pltpu.SemaphoreType.DMA((3 * nw,)), pltpu.SemaphoreType.DMA((3 * nw,)),
                        pltpu.SemaphoreType.DMA((2 * nw,))],
        name="comm_gather_weights")(*shards)


def _half_of(ref, kind, half):
    if kind == "col":
        rh = ref.shape[0] // 2
        return ref.at[pl.ds(half * rh, rh), :]
    rh = ref.shape[1] // 2
    return ref.at[:, pl.ds(half * rh, rh), :]


def _half_shape(g, kind):
    if kind == "col":
        return (g.shape[0] // 2, g.shape[1])
    return (g.shape[0], g.shape[1] // 2, g.shape[2])


def _swap_halves(grads, kinds):
    nw = len(grads)

    def body(*refs):
        g, got = refs[:nw], refs[nw:2 * nw]
        send_sems, recv_sems = refs[2 * nw:]
        x, y, c = _mesh_pos()
        cps = []
        for w in range(nw):
            cp = pltpu.make_async_remote_copy(
                src_ref=_half_of(g[w], kinds[w], 1 - c), dst_ref=got[w],
                send_sem=send_sems.at[w], recv_sem=recv_sems.at[w], device_id=(x, y, 1 - c), device_id_type=MESH)
            cp.start()
            cps.append(cp)
        for cp in cps:
            cp.wait()

    return pl.pallas_call(
        body, out_shape=tuple(SDS(_half_shape(g, k), g.dtype) for g, k in zip(grads, kinds)),
        in_specs=[_ANY] * nw, out_specs=tuple([_ANY] * nw),
        scratch_shapes=[pltpu.SemaphoreType.DMA((nw,)), pltpu.SemaphoreType.DMA((nw,))],
        name="comm_swap_halves")(*grads)


def _add_halves(name, c_arr, g, got, kind):
    if kind == "col":
        rh, n = got.shape
        tr = _tile(rh, 256)
        nh = rh // tr
        g_spec = pl.BlockSpec((tr, n), lambda i, c: (c[0] * nh + i, 0))
        o_spec = pl.BlockSpec((tr, n), lambda i, c: (i, 0))
        grid = (nh,)
    else:
        nc, rh, n = got.shape
        tr = _tile(rh, 256)
        nh = rh // tr
        g_spec = pl.BlockSpec((None, tr, n), lambda j, i, c: (j, c[0] * nh + i, 0))
        o_spec = pl.BlockSpec((None, tr, n), lambda j, i, c: (j, i, 0))
        grid = (nc, nh)

    def body(c_ref, g_ref, r_ref, o_ref):
        o_ref[...] = (g_ref[...].astype(F32) + r_ref[...].astype(F32)).astype(o_ref.dtype)

    return pl.pallas_call(
        body,
        grid_spec=pltpu.PrefetchScalarGridSpec(num_scalar_prefetch=1, grid=grid, in_specs=[g_spec, o_spec], out_specs=o_spec),
        out_shape=SDS(got.shape, got.dtype),
        compiler_params=_params(("parallel",) * len(grid)), name=name)(c_arr, g, got)


def _scatter_partials(parts, kinds):
    nw = len(parts)

    def blk_shape(p, kind):
        return (p.shape[0], p.shape[1] // N_CHIP) if kind == "col" else (p.shape[1], p.shape[2])

    def body(*refs):
        p, got = refs[:nw], refs[nw:2 * nw]
        send_sems, recv_sems = refs[2 * nw:]
        x, y, c = _mesh_pos()
        chips = _other_chips(x, y)
        cps = []
        for w in range(nw):
            for k, (px, py) in enumerate(chips):
                j = 2 * px + py
                if kinds[w] == "col":
                    n = p[w].shape[1] // N_CHIP
                    src = p[w].at[:, pl.ds(j * n, n)]
                else:
                    src = p[w].at[j]
                cp = pltpu.make_async_remote_copy(
                    src_ref=src, dst_ref=got[w].at[k], send_sem=send_sems.at[3 * w + k], recv_sem=recv_sems.at[3 * w + k],
                    device_id=(px, py, c), device_id_type=MESH)
                cp.start()
                cps.append(cp)
        for cp in cps:
            cp.wait()

    return pl.pallas_call(
        body, out_shape=tuple(SDS((3,) + blk_shape(p, k), p.dtype) for p, k in zip(parts, kinds)),
        in_specs=[_ANY] * nw, out_specs=tuple([_ANY] * nw),
        scratch_shapes=[pltpu.SemaphoreType.DMA((3 * nw,)), pltpu.SemaphoreType.DMA((3 * nw,))],
        name="comm_scatter_partials")(*parts)


def _sum_partials(name, j_arr, part, got, kind):
    _, rh, n = got.shape
    tr = _tile(rh, 256)
    if kind == "col":
        p_spec = pl.BlockSpec((tr, n), lambda i, j: (i, j[0]))
    else:
        p_spec = pl.BlockSpec((None, tr, n), lambda i, j: (j[0], i, 0))

    def body(j_ref, p_ref, r_ref, o_ref):
        o_ref[...] = ((p_ref[...].astype(F32) + r_ref[0].astype(F32)) + r_ref[1].astype(F32)) + r_ref[2].astype(F32)

    return pl.pallas_call(
        body,
        grid_spec=pltpu.PrefetchScalarGridSpec(
            num_scalar_prefetch=1, grid=(rh // tr,),
            in_specs=[p_spec, pl.BlockSpec((3, tr, n), lambda i, j: (0, i, 0))],
            out_specs=pl.BlockSpec((tr, n), lambda i, j: (i, 0))),
        out_shape=SDS((rh, n), F32),
        compiler_params=_params(("parallel",)), name=name)(j_arr, part, got)


def _swap_reduced(halves):
    nw = len(halves)

    def body(*refs):
        h, got = refs[:nw], refs[nw:2 * nw]
        send_sems, recv_sems = refs[2 * nw:]
        x, y, c = _mesh_pos()
        cps = []
        for w in range(nw):
            cp = pltpu.make_async_remote_copy(
                src_ref=h[w], dst_ref=got[w], send_sem=send_sems.at[w], recv_sem=recv_sems.at[w],
                device_id=(x, y, 1 - c), device_id_type=MESH)
            cp.start()
            cps.append(cp)
        for cp in cps:
            cp.wait()

    return pl.pallas_call(
        body, out_shape=tuple(SDS(h.shape, h.dtype) for h in halves),
        in_specs=[_ANY] * nw, out_specs=tuple([_ANY] * nw),
        scratch_shapes=[pltpu.SemaphoreType.DMA((nw,)), pltpu.SemaphoreType.DMA((nw,))],
        name="comm_swap_reduced")(*halves)


def _pack(arrays):
    flat = [a.reshape(-1).astype(F32) for a in arrays]
    flat = [jnp.pad(f, (0, (-f.shape[0]) % LANES)) for f in flat]
    sizes = [f.shape[0] for f in flat]
    total = sum(sizes)
    rows = total // LANES
    tail = LANES * ((-rows) % (PACK_ROWS if rows > PACK_ROWS else SUBLANES))
    if tail:
        flat.append(jnp.zeros((tail,), F32))
    return jnp.concatenate(flat).reshape(-1, LANES), sizes


def _unpack(slab, sizes, shapes, lead=()):
    flat = slab.reshape(lead + (-1,))
    out, off = [], 0
    for sz, shp in zip(sizes, shapes):
        n = math.prod(shp)
        out.append(flat[..., off:off + n].reshape(lead + tuple(shp)))
        off += sz
    return out


def kernel(x, c, w_ada, b_ada, g_norm_mix, w_in, w_conv, b_conv, w_rg_a, b_rg_a, w_rg_x, b_rg_x, lru_lambda, g_attn_out, g_lru_out, w_out, g_norm_mlp, w_mlp_in, w_mlp_out, g_norm_final, loss_target, m_w_ada, m_b_ada, m_g_norm_mix, m_w_in, m_w_conv, m_b_conv, m_w_rg_a, m_b_rg_a, m_w_rg_x, m_b_rg_x, m_lru_lambda, m_g_attn_out, m_g_lru_out, m_w_out, m_g_norm_mlp, m_w_mlp_in, m_w_mlp_out, m_g_norm_final, v_w_ada, v_b_ada, v_g_norm_mix, v_w_in, v_w_conv, v_b_conv, v_w_rg_a, v_b_rg_a, v_w_rg_x, v_b_rg_x, v_lru_lambda, v_g_attn_out, v_g_lru_out, v_w_out, v_g_norm_mlp, v_w_mlp_in, v_w_mlp_out, v_g_norm_final):
    s, d = x.shape[1], x.shape[2]
    aw = d // 2
    nh = aw // HEAD
    f = w_mlp_out.shape[1] * N_CHIP
    n_ada = w_ada.shape[2]
    n_cv = w_conv.shape[2]
    ix, iy, ic = lax.axis_index("x"), lax.axis_index("y"), lax.axis_index("c")
    chip = 2 * ix + iy
    me = 2 * chip + ic
    c_arr = jnp.reshape(ic, (1,)).astype(jnp.int32)
    j_arr = jnp.reshape(chip, (1,)).astype(jnp.int32)

    x2d, tgt = x[0], loss_target[0]

    slab, sizes = _pack([c, w_conv])
    gathered = _all_gather_small("comm_gather_cond", slab)
    c_parts, cv_parts = _unpack(gathered, sizes, [(d,), (CONV_TAPS, n_cv)], lead=(N_DEV,))
    c_all = c_parts
    w_conv_full = jnp.concatenate([cv_parts[2 * j] for j in range(N_CHIP)], axis=-1)
    b_sh = lax.dynamic_slice(b_ada, (0, chip * n_ada), (1, n_ada))
    mod_part, act_all = _ada_mod(c_all, w_ada[0], b_sh)
    mod_g = _all_gather_small("comm_gather_mod", mod_part.reshape(-1, LANES))
    mod_g = mod_g.reshape(N_DEV, N_DEV, n_ada)
    mod = jnp.concatenate([lax.dynamic_index_in_dim(mod_g[2 * j], me, 0, keepdims=True) for j in range(N_CHIP)], axis=-1)
    sh1, sc1, gt1, sh2, sc2, gt2 = [mod[:, k * d:(k + 1) * d] for k in range(N_MOD)]

    kinds = ("col", "row", "col", "row")
    w_in_f, w_out_f, w_mi_f, w_mo_f = _gather_weights(
        [w_in[0].astype(BF16), w_out[0].astype(BF16), w_mlp_in[0].astype(BF16), w_mlp_out[0].astype(BF16)], kinds)

    h1, rstd1 = _norm_mod_fwd("norm_mod_fwd1", x2d, g_norm_mix, sc1, sh1)
    (qkv,) = _matmul("mm_qkv", h1, w_in_f, "nn", s, 3 * aw, d, (BF16,))
    (xrg,) = _matmul("mm_xrg", h1, w_in_f, "nn", s, 2 * aw, d, (F32,), b_off=3 * aw)
    o_attn = _attn_fwd(qkv, nh)
    wa3, wx3 = w_rg_a[0], w_rg_x[0]
    o_lru, hseq = _lru_fwd(xrg, w_conv_full, b_conv, wa3, b_rg_a, wx3, b_rg_x, lru_lambda)
    mixed, rstd_a, rstd_l = _mix_norm_fwd(o_attn, o_lru, g_attn_out, g_lru_out)

    def residual(acc, xin, gt):
        return acc, xin + gt * acc

    y1, x1 = _matmul("mm_out", mixed, w_out_f, "nn", s, d, d, (F32, F32), extras=(x2d, gt1),
                     extra_kinds=("tile", "row"), epilogue=residual)
    h2, rstd2 = _norm_mod_fwd("norm_mod_fwd2", x1, g_norm_mlp, sc2, sh2)

    def sq_relu(acc):
        r = jnp.maximum(acc, 0.0)
        return 2.0 * r, r * r

    r2, hid = _matmul("mm_mlp_in", h2, w_mi_f, "nn", s, f, d, (BF16, BF16), epilogue=sq_relu)
    y2, x2 = _matmul("mm_mlp_out", hid, w_mo_f, "nn", s, d, f, (F32, F32), extras=(x1, gt2),
                     extra_kinds=("tile", "row"), epilogue=residual)
    dx2, loss_row, dg_final = _final_loss(x2, g_norm_final.reshape(1, d), tgt)

    dy2, dgt2 = _gate_bwd("gate_bwd2", dx2, y2, gt2)
    (dpre,) = _matmul("mm_dhid", dy2, w_mo_f, "nt", s, f, d, (BF16,), extras=(r2,), extra_kinds=("tile",),
                      epilogue=lambda acc, r: (acc * r.astype(F32),))
    (g_mo,) = _matmul("mm_dw_mlp_out", hid, dy2, "tn", f, d, s, (BF16,))
    (dh2,) = _matmul("mm_dh2", dpre, w_mi_f, "nt", s, d, f, (F32,))
    (g_mi,) = _matmul("mm_dw_mlp_in", h2, dpre, "tn", d, f, s, (BF16,))
    dx1, dsh2, dsc2, dg_mlp = _norm_mod_bwd("norm_mod_bwd2", dh2, x1, rstd2, g_norm_mlp, sc2, dx2)
    dy1, dgt1 = _gate_bwd("gate_bwd1", dx1, y1, gt1)
    (dmixed,) = _matmul("mm_dmixed", dy1, w_out_f, "nt", s, d, d, (F32,))
    (g_out,) = _matmul("mm_dw_out", mixed, dy1, "tn", d, d, s, (BF16,))
    do_attn, do_lru, dg_attn, dg_lru = _mix_norm_bwd(dmixed, o_attn, o_lru, rstd_a, rstd_l, g_attn_out, g_lru_out)
    dq, dk, dv = _attn_bwd(qkv, do_attn, nh)
    dxr, dxg, dwconv, dbconv, dwa, dba, dwx, dbx, dlam = _lru_bwd(
        xrg, do_lru, hseq, w_conv_full, b_conv, wa3, b_rg_a, wx3, b_rg_x, lru_lambda)
    dproj = jnp.concatenate([dq, dk, dv, dxr, dxg], axis=-1)
    (dh1,) = _matmul("mm_dh1", dproj, w_in_f, "nt", s, d, 5 * aw, (F32,))
    (g_in,) = _matmul("mm_dw_in", h1, dproj, "tn", d, 5 * aw, s, (BF16,))
    grad_x, dsh1, dsc1, dg_mix = _norm_mod_bwd("norm_mod_bwd1", dh1, x2d, rstd1, g_norm_mix, sc1, dx1)

    dmod = jnp.concatenate([dsh1, dsc1, dgt1, dsh2, dsc2, dgt2], axis=-1)
    small_names = ["b_ada", "g_norm_mix", "b_conv", "w_rg_a", "b_rg_a", "w_rg_x", "b_rg_x", "lru_lambda",
                   "g_attn_out", "g_lru_out", "g_norm_mlp", "g_norm_final"]
    small_g = [dmod, dg_mix, dbconv, dwa, dba, dwx, dbx, dlam, dg_attn, dg_lru, dg_mlp, dg_final]
    small_w = [b_ada, g_norm_mix, b_conv, w_rg_a, b_rg_a, w_rg_x, b_rg_x, lru_lambda, g_attn_out, g_lru_out, g_norm_mlp, g_norm_final]
    small_m = [m_b_ada, m_g_norm_mix, m_b_conv, m_w_rg_a, m_b_rg_a, m_w_rg_x, m_b_rg_x, m_lru_lambda, m_g_attn_out, m_g_lru_out, m_g_norm_mlp, m_g_norm_final]
    small_v = [v_b_ada, v_g_norm_mix, v_b_conv, v_w_rg_a, v_b_rg_a, v_w_rg_x, v_b_rg_x, v_lru_lambda, v_g_attn_out, v_g_lru_out, v_g_norm_mlp, v_g_norm_final]
    extra_zero = [jnp.zeros_like(dwconv), jnp.zeros((LANES,), F32)]
    g_slab, g_sizes = _pack(small_g + [dwconv, loss_row])
    w_slab, _ = _pack(small_w + extra_zero)
    m_slab, _ = _pack(small_m + extra_zero)
    v_slab, _ = _pack(small_v + extra_zero)
    g_all = _all_gather_small("comm_gather_small_grads", g_slab)
    gs_slab, ds_slab, ms_slab, vs_slab = _small_reduce_adamw(g_all, w_slab, m_slab, v_slab)
    shapes = [w.shape for w in small_w] + [dwconv.shape, (LANES,)]
    gs = _unpack(gs_slab, g_sizes, shapes)
    ds = _unpack(ds_slab, g_sizes, shapes)
    ms = _unpack(ms_slab, g_sizes, shapes)
    vs = _unpack(vs_slab, g_sizes, shapes)
    small = {n: (gs[i], ds[i], ms[i], vs[i]) for i, n in enumerate(small_names)}
    loss = gs[-1][0]
    g_wconv = lax.dynamic_slice(gs[-2], (0, chip * n_cv), (CONV_TAPS, n_cv))
    d_wconv, m_wconv, v_wconv = _adamw_plain("adamw_conv", w_conv[0], g_wconv, m_w_conv[0], v_w_conv[0])
    small["w_conv"] = (g_wconv[None], d_wconv[None], m_wconv[None], v_wconv[None])

    dmod_all = _unpack(g_all, g_sizes, [(N_MOD * d,)], lead=(N_DEV,))[0]
    dmod_sel = lax.dynamic_slice(dmod_all, (0, chip * n_ada), (N_DEV, n_ada)).astype(BF16)
    act_t = act_all.T.astype(BF16)
    big = {"w_ada": _adamw_ada(w_ada[0], m_w_ada[0], v_w_ada[0], act_t, dmod_sel)}

    grads = [g_in, g_out.reshape(N_CHIP, d // N_CHIP, d), g_mi, g_mo.reshape(N_CHIP, f // N_CHIP, d)]
    got1 = _swap_halves(grads, kinds)
    parts = [_add_halves("add_halves_%d" % i, c_arr, g, r, k) for i, (g, r, k) in enumerate(zip(grads, got1, kinds))]
    got2 = _scatter_partials(parts, kinds)
    halves = [_sum_partials("sum_partials_%d" % i, j_arr, p, r, k) for i, (p, r, k) in enumerate(zip(parts, got2, kinds))]
    got3 = _swap_reduced(halves)
    for i, (n, w_, m_, v_) in enumerate([("w_in", w_in, m_w_in, v_w_in), ("w_out", w_out, m_w_out, v_w_out),
                                          ("w_mlp_in", w_mlp_in, m_w_mlp_in, v_w_mlp_in),
                                          ("w_mlp_out", w_mlp_out, m_w_mlp_out, v_w_mlp_out)]):
        big[n] = _adamw_halves("adamw_" + n, c_arr, w_[0], m_[0], v_[0], halves[i], got3[i])

    order = ["w_ada", "b_ada", "g_norm_mix", "w_in", "w_conv", "b_conv", "w_rg_a", "b_rg_a", "w_rg_x", "b_rg_x",
             "lru_lambda", "g_attn_out", "g_lru_out", "w_out", "g_norm_mlp", "w_mlp_in", "w_mlp_out", "g_norm_final"]
    res = {}
    for n in order:
        res[n] = tuple(t[None] for t in big[n]) if n in big else small[n]
    return (loss, grad_x[None],
            *[res[n][0] for n in order], *[res[n][1] for n in order],
            *[res[n][2] for n in order], *[res[n][3] for n in order])
```

```python
import functools
import math

import jax
import jax.numpy as jnp
from jax import lax
from jax.experimental import pallas as pl
from jax.experimental.pallas import tpu as pltpu

F32 = jnp.float32
BF16 = jnp.bfloat16
SDS = jax.ShapeDtypeStruct
MESH = pl.DeviceIdType.MESH

EPS = 1e-6
HEAD = 128
N_MOD = 6
CONV_TAPS = 4
LRU_C = 8.0
ADAM_LR, ADAM_B1, ADAM_B2, ADAM_EPS, ADAM_WD, ADAM_STEP = 0.001, 0.9, 0.999, 1e-08, 0.01, 10
N_DEV = 8
N_CHIP = 4
LANES = 128
SUBLANES = 8
VMEM_LIMIT = 56 * 1024 * 1024
PACK_ROWS = 256


def _tile(dim, pref):
    t = min(dim, pref)
    while dim % t:
        t -= LANES
    return t


def _params(sem=None):
    return pltpu.CompilerParams(dimension_semantics=sem, vmem_limit_bytes=VMEM_LIMIT)


def _sigmoid(x):
    return 1.0 / (1.0 + jnp.exp(-x))


def _log_sigmoid(x):
    return jnp.minimum(x, 0.0) - jnp.log(1.0 + jnp.exp(-jnp.abs(x)))


def _gelu_parts(x):
    k0, k1 = math.sqrt(2.0 / math.pi), 0.044715
    t = jnp.tanh(k0 * (x + k1 * x * x * x))
    val = 0.5 * x * (1.0 + t)
    der = 0.5 * (1.0 + t) + 0.5 * x * (1.0 - t * t) * k0 * (1.0 + 3.0 * k1 * x * x)
    return val, der


def _dot(a, b):
    return jnp.dot(a, b, preferred_element_type=F32)


def _dot_nt(a, b):
    return lax.dot_general(a, b, (((1,), (1,)), ((), ())), preferred_element_type=F32)


def _dot_tn(a, b):
    return lax.dot_general(a, b, (((0,), (0,)), ((), ())), preferred_element_type=F32)


def _split_dot(x, tri):
    hi = x.astype(BF16)
    lo = (x - hi.astype(F32)).astype(BF16)
    return _dot(hi, tri) + _dot(lo, tri)


def _matmul(name, a, b, mode, m, n, k, out_dtypes, *, b_off=0, extras=(), extra_kinds=(), epilogue=None,
            tm=1024, tn=1024, tk=512):
    tm, tn, tk = _tile(m, tm), _tile(math.gcd(n, b_off) if b_off else n, tn), _tile(k, tk)
    assert b_off % tn == 0
    nk = k // tk
    n_ex, n_out = len(extras), len(out_dtypes)

    def body(a_ref, b_ref, *rest):
        ex, outs, acc = rest[:n_ex], rest[n_ex:n_ex + n_out], rest[-1]
        kk = pl.program_id(2)

        @pl.when(kk == 0)
        def _():
            acc[...] = jnp.zeros_like(acc)

        if mode == "nn":
            acc[...] += _dot(a_ref[...], b_ref[...])
        elif mode == "nt":
            acc[...] += _dot_nt(a_ref[...], b_ref[...])
        else:
            acc[...] += _dot_tn(a_ref[...], b_ref[...])

        @pl.when(kk == nk - 1)
        def _():
            res = epilogue(acc[...], *[e[...] for e in ex]) if epilogue else (acc[...],)
            for o, r in zip(outs, res):
                o[...] = r.astype(o.dtype)

    if mode == "nn":
        a_spec = pl.BlockSpec((tm, tk), lambda i, j, kk: (i, kk))
        b_spec = pl.BlockSpec((tk, tn), lambda i, j, kk: (kk, j + b_off // tn))
    elif mode == "nt":
        a_spec = pl.BlockSpec((tm, tk), lambda i, j, kk: (i, kk))
        b_spec = pl.BlockSpec((tn, tk), lambda i, j, kk: (j, kk + b_off // tk))
    else:
        a_spec = pl.BlockSpec((tk, tm), lambda i, j, kk: (kk, i))
        b_spec = pl.BlockSpec((tk, tn), lambda i, j, kk: (kk, j))
    tile_spec = pl.BlockSpec((tm, tn), lambda i, j, kk: (i, j))
    row_spec = pl.BlockSpec((1, tn), lambda i, j, kk: (0, j))
    ex_specs = [tile_spec if kind == "tile" else row_spec for kind in extra_kinds]
    outs = pl.pallas_call(
        body, grid=(m // tm, n // tn, nk),
        in_specs=[a_spec, b_spec] + ex_specs,
        out_specs=tuple(tile_spec for _ in out_dtypes),
        out_shape=tuple(SDS((m, n), dt) for dt in out_dtypes),
        scratch_shapes=[pltpu.VMEM((tm, tn), F32)],
        compiler_params=_params(("parallel", "parallel", "arbitrary")),
        name=name,
    )(a, b, *extras)
    return outs


def _row_specs(s, d, tr):
    row = pl.BlockSpec((tr, d), lambda i: (i, 0))
    vec = pl.BlockSpec((1, d), lambda i: (0, 0))
    col = pl.BlockSpec((tr, 1), lambda i: (i, 0))
    return row, vec, col


def _norm_mod_fwd(name, x, g, sc, sh):
    s, d = x.shape
    tr = _tile(s, 256)
    row, vec, col = _row_specs(s, d, tr)

    def body(x_ref, g_ref, sc_ref, sh_ref, h_ref, r_ref):
        xv = x_ref[...]
        r = lax.rsqrt(jnp.mean(xv * xv, axis=-1, keepdims=True) + EPS)
        h_ref[...] = ((xv * r * g_ref[...]) * (1.0 + sc_ref[...]) + sh_ref[...]).astype(BF16)
        r_ref[...] = r

    return pl.pallas_call(
        body, grid=(s // tr,), in_specs=[row, vec, vec, vec], out_specs=(row, col),
        out_shape=(SDS((s, d), BF16), SDS((s, 1), F32)),
        compiler_params=_params(("parallel",)), name=name)(x, g, sc, sh)


def _norm_mod_bwd(name, dh, xin, rstd, g, sc, dres):
    s, d = xin.shape
    tr = _tile(s, 256)
    row, vec, col = _row_specs(s, d, tr)

    def body(dh_ref, x_ref, r_ref, g_ref, sc_ref, dres_ref, dx_ref, dsh_ref, dsc_ref, dg_ref):
        @pl.when(pl.program_id(0) == 0)
        def _():
            dsh_ref[...] = jnp.zeros_like(dsh_ref)
            dsc_ref[...] = jnp.zeros_like(dsc_ref)
            dg_ref[...] = jnp.zeros_like(dg_ref)

        dh_v, xv, r, gv = dh_ref[...], x_ref[...], r_ref[...], g_ref[...]
        n0 = xv * r
        dsh_ref[...] += jnp.sum(dh_v, axis=0, keepdims=True)
        dsc_ref[...] += jnp.sum(dh_v * (n0 * gv), axis=0, keepdims=True)
        dn = dh_v * (1.0 + sc_ref[...])
        dg_ref[...] += jnp.sum(dn * n0, axis=0, keepdims=True)
        gy = dn * gv
        dot = jnp.mean(gy * xv, axis=-1, keepdims=True)
        dx_ref[...] = dres_ref[...] + r * gy - xv * (r * r * r * dot)

    return pl.pallas_call(
        body, grid=(s // tr,), in_specs=[row, row, col, vec, vec, row], out_specs=(row, vec, vec, vec),
        out_shape=(SDS((s, d), F32), SDS((1, d), F32), SDS((1, d), F32), SDS((1, d), F32)),
        compiler_params=_params(("arbitrary",)), name=name)(dh, xin, rstd, g, sc, dres)


def _gate_bwd(name, dx, y, gt):
    s, d = dx.shape
    tr = _tile(s, 256)
    row, vec, _ = _row_specs(s, d, tr)

    def body(dx_ref, y_ref, gt_ref, dy_ref, dgt_ref):
        @pl.when(pl.program_id(0) == 0)
        def _():
            dgt_ref[...] = jnp.zeros_like(dgt_ref)

        dxv = dx_ref[...]
        dy_ref[...] = (gt_ref[...] * dxv).astype(BF16)
        dgt_ref[...] += jnp.sum(dxv * y_ref[...], axis=0, keepdims=True)

    return pl.pallas_call(
        body, grid=(s // tr,), in_specs=[row, row, vec], out_specs=(row, vec),
        out_shape=(SDS((s, d), BF16), SDS((1, d), F32)),
        compiler_params=_params(("arbitrary",)), name=name)(dx, y, gt)


def _final_loss(x2, gf, tgt):
    s, d = x2.shape
    tr = _tile(s, 256)
    row, vec, _ = _row_specs(s, d, tr)
    lrow = pl.BlockSpec((1, LANES), lambda i: (0, 0))

    def body(x_ref, g_ref, t_ref, dx_ref, loss_ref, dg_ref):
        @pl.when(pl.program_id(0) == 0)
        def _():
            loss_ref[...] = jnp.zeros_like(loss_ref)
            dg_ref[...] = jnp.zeros_like(dg_ref)

        xv, gv = x_ref[...], g_ref[...]
        r = lax.rsqrt(jnp.mean(xv * xv, axis=-1, keepdims=True) + EPS)
        n0 = xv * r
        err = n0 * gv - t_ref[...]
        loss_ref[...] += jnp.sum(err * err) * (0.5 / d)
        dy = err * (1.0 / d)
        dg_ref[...] += jnp.sum(dy * n0, axis=0, keepdims=True)
        gy = dy * gv
        dot = jnp.mean(gy * xv, axis=-1, keepdims=True)
        dx_ref[...] = r * gy - xv * (r * r * r * dot)

    return pl.pallas_call(
        body, grid=(s // tr,), in_specs=[row, vec, row], out_specs=(row, lrow, vec),
        out_shape=(SDS((s, d), F32), SDS((1, LANES), F32), SDS((1, d), F32)),
        compiler_params=_params(("arbitrary",)), name="final_loss")(x2, gf, tgt)


def _mix_norm_fwd(oa, ol, ga, gl):
    s, w = oa.shape
    tr = _tile(s, 256)
    row, vec, col = _row_specs(s, w, tr)

    def body(oa_ref, ol_ref, ga_ref, gl_ref, mx_ref, ra_ref, rl_ref):
        a, l = oa_ref[...], ol_ref[...]
        ra = lax.rsqrt(jnp.mean(a * a, axis=-1, keepdims=True) + EPS)
        rl = lax.rsqrt(jnp.mean(l * l, axis=-1, keepdims=True) + EPS)
        mx_ref[:, :w] = (a * ra * ga_ref[...]).astype(BF16)
        mx_ref[:, w:] = (l * rl * gl_ref[...]).astype(BF16)
        ra_ref[...] = ra
        rl_ref[...] = rl

    return pl.pallas_call(
        body, grid=(s // tr,), in_specs=[row, row, vec, vec],
        out_specs=(pl.BlockSpec((tr, 2 * w), lambda i: (i, 0)), col, col),
        out_shape=(SDS((s, 2 * w), BF16), SDS((s, 1), F32), SDS((s, 1), F32)),
        compiler_params=_params(("parallel",)), name="mix_norm_fwd")(oa, ol, ga, gl)


def _mix_norm_bwd(dmx, oa, ol, ra, rl, ga, gl):
    s, w = oa.shape
    tr = _tile(s, 256)
    row, vec, col = _row_specs(s, w, tr)

    def body(dm_ref, oa_ref, ol_ref, ra_ref, rl_ref, ga_ref, gl_ref, doa_ref, dol_ref, dga_ref, dgl_ref):
        @pl.when(pl.program_id(0) == 0)
        def _():
            dga_ref[...] = jnp.zeros_like(dga_ref)
            dgl_ref[...] = jnp.zeros_like(dgl_ref)

        def one(dy, xv, r, gv, dg_ref):
            dg_ref[...] += jnp.sum(dy * (xv * r), axis=0, keepdims=True)
            gy = dy * gv
            dot = jnp.mean(gy * xv, axis=-1, keepdims=True)
            return r * gy - xv * (r * r * r * dot)

        doa_ref[...] = one(dm_ref[:, :w], oa_ref[...], ra_ref[...], ga_ref[...], dga_ref).astype(BF16)
        dol_ref[...] = one(dm_ref[:, w:], ol_ref[...], rl_ref[...], gl_ref[...], dgl_ref)

    return pl.pallas_call(
        body, grid=(s // tr,),
        in_specs=[pl.BlockSpec((tr, 2 * w), lambda i: (i, 0)), row, row, col, col, vec, vec],
        out_specs=(row, row, vec, vec),
        out_shape=(SDS((s, w), BF16), SDS((s, w), F32), SDS((1, w), F32), SDS((1, w), F32)),
        compiler_params=_params(("arbitrary",)), name="mix_norm_bwd")(dmx, oa, ol, ra, rl, ga, gl)


def _attn_block(q, k_blk, off, row_id, tq, tk, tri_after, csum):
    z = _dot_nt(q, k_blk) * (HEAD ** -0.5)
    col_id = off + lax.broadcasted_iota(jnp.int32, (tq, tk), 1)
    causal = col_id < row_id
    lb = _log_sigmoid(z)
    ls = jnp.where(causal, lb - z, 0.0)
    after = _split_dot(ls, tri_after) + csum
    w = jnp.where(causal, jnp.exp(lb + after), 0.0)
    return lb, ls, w, causal


def _attn_tiles(s):
    t = 256 if s >= 1024 else 128
    return t, t


def _attn_fwd(qkv, n_heads):
    s = qkv.shape[0]
    tq, tk = _attn_tiles(s)

    def body(q_ref, k_ref, v_ref, o_ref):
        qi = pl.program_id(1)
        q = q_ref[...]
        row_id = qi * tq + lax.broadcasted_iota(jnp.int32, (tq, tk), 0)
        r_i = lax.broadcasted_iota(jnp.int32, (tk, tk), 0)
        c_i = lax.broadcasted_iota(jnp.int32, (tk, tk), 1)
        tri_after = (r_i > c_i).astype(BF16)

        def step(it, carry):
            csum, o = carry
            off = pl.multiple_of((qi - it) * tk, tk)
            _, ls, w, _ = _attn_block(q, k_ref[pl.ds(off, tk), :], off, row_id, tq, tk, tri_after, csum)
            o = o + _dot(w.astype(BF16), v_ref[pl.ds(off, tk), :])
            return csum + jnp.sum(ls, axis=1, keepdims=True), o

        _, o = lax.fori_loop(0, qi + 1, step, (jnp.zeros((tq, 1), F32), jnp.zeros((tq, HEAD), F32)))
        o_ref[...] = o

    h = n_heads
    return pl.pallas_call(
        body, grid=(h, s // tq),
        in_specs=[pl.BlockSpec((tq, HEAD), lambda hh, i: (i, hh)),
                  pl.BlockSpec((s, HEAD), lambda hh, i: (0, h + hh)),
                  pl.BlockSpec((s, HEAD), lambda hh, i: (0, 2 * h + hh))],
        out_specs=pl.BlockSpec((tq, HEAD), lambda hh, i: (i, hh)),
        out_shape=SDS((s, h * HEAD), F32),
        compiler_params=_params(("parallel", "parallel")), name="attn_fwd")(qkv, qkv, qkv)


def _attn_bwd(qkv, do, n_heads):
    s = qkv.shape[0]
    tq, tk = _attn_tiles(s)
    nq = s // tq
    scale = HEAD ** -0.5

    def body(q_ref, k_ref, v_ref, do_ref, dq_ref, dk_ref, dv_ref, e_s, sg_s, dk_acc, dv_acc):
        qi = pl.program_id(1)

        @pl.when(qi == 0)
        def _():
            dk_acc[...] = jnp.zeros_like(dk_acc)
            dv_acc[...] = jnp.zeros_like(dv_acc)

        q = q_ref[...]
        dout = do_ref[...]
        row_id = qi * tq + lax.broadcasted_iota(jnp.int32, (tq, tk), 0)
        r_i = lax.broadcasted_iota(jnp.int32, (tk, tk), 0)
        c_i = lax.broadcasted_iota(jnp.int32, (tk, tk), 1)
        tri_after = (r_i > c_i).astype(BF16)
        tri_before = (r_i < c_i).astype(BF16)

        def pass1(it, csum):
            kb = qi - it
            off = pl.multiple_of(kb * tk, tk)
            lb, ls, w, _ = _attn_block(q, k_ref[pl.ds(off, tk), :], off, row_id, tq, tk, tri_after, csum)
            dw = _dot_nt(dout, v_ref[pl.ds(off, tk), :])
            e_s[kb] = dw * w
            sg_s[kb] = jnp.exp(lb)
            dv_acc[pl.ds(off, tk), :] += _dot_tn(w.astype(BF16), dout)
            return csum + jnp.sum(ls, axis=1, keepdims=True)

        lax.fori_loop(0, qi + 1, pass1, jnp.zeros((tq, 1), F32))

        def pass2(kb, carry):
            esum, dq = carry
            off = pl.multiple_of(kb * tk, tk)
            e, sg = e_s[kb], sg_s[kb]
            col_id = off + lax.broadcasted_iota(jnp.int32, (tq, tk), 1)
            before = _split_dot(e, tri_before) + esum
            dz = (e * (1.0 - sg) - jnp.where(col_id < row_id, before * sg, 0.0)) * scale
            dzb = dz.astype(BF16)
            dq = dq + _dot(dzb, k_ref[pl.ds(off, tk), :])
            dk_acc[pl.ds(off, tk), :] += _dot_tn(dzb, q)
            return esum + jnp.sum(e, axis=1, keepdims=True), dq

        _, dq = lax.fori_loop(0, qi + 1, pass2, (jnp.zeros((tq, 1), F32), jnp.zeros((tq, HEAD), F32)))
        dq_ref[...] = dq.astype(BF16)

        @pl.when(qi == nq - 1)
        def _():
            dk_ref[...] = dk_acc[...].astype(BF16)
            dv_ref[...] = dv_acc[...].astype(BF16)

    h = n_heads
    blk = pl.BlockSpec((tq, HEAD), lambda hh, i: (i, hh))
    full = pl.BlockSpec((s, HEAD), lambda hh, i: (0, hh))
    return pl.pallas_call(
        body, grid=(h, nq),
        in_specs=[blk,
                  pl.BlockSpec((s, HEAD), lambda hh, i: (0, h + hh)),
                  pl.BlockSpec((s, HEAD), lambda hh, i: (0, 2 * h + hh)),
                  blk],
        out_specs=(blk, full, full),
        out_shape=(SDS((s, h * HEAD), BF16),) * 3,
        scratch_shapes=[pltpu.VMEM((s // tk, tq, tk), F32), pltpu.VMEM((s // tk, tq, tk), F32),
                        pltpu.VMEM((s, HEAD), F32), pltpu.VMEM((s, HEAD), F32)],
        compiler_params=_params(("parallel", "arbitrary")), name="attn_bwd")(qkv, qkv, qkv, do)


def _lru_chunk(s):
    return 256 if s >= 1024 else 128


def _lru_gates(xc, wa, ba, wx, bx, sp):
    xb = xc.astype(BF16)
    r = _sigmoid(_dot(xb, wa) + ba)
    ig = _sigmoid(_dot(xb, wx) + bx)
    la = -LRU_C * r * sp
    a = jnp.exp(la)
    t = jnp.tanh(la)
    mult = jnp.sqrt(-2.0 * t / (1.0 - t))
    return r, ig, a, mult


def _softplus_neg(lam):
    return jnp.maximum(-lam, 0.0) + jnp.log(1.0 + jnp.exp(-jnp.abs(lam)))


def _lru_specs(s, n_blocks):
    seq0 = pl.BlockSpec((s, HEAD), lambda h: (0, h))
    seq1 = pl.BlockSpec((s, HEAD), lambda h: (0, n_blocks + h))
    taps = pl.BlockSpec((CONV_TAPS, HEAD), lambda h: (0, h))
    vec = pl.BlockSpec((1, HEAD), lambda h: (0, h))
    mat = pl.BlockSpec((None, HEAD, HEAD), lambda h: (h, 0, 0))
    return seq0, seq1, taps, vec, mat


def _lru_fwd(xrg, wconv, bconv, wa, ba, wx, bx, lam):
    s = xrg.shape[0]
    nb = wa.shape[0]
    tc = _lru_chunk(s)
    seq0, seq1, taps, vec, mat = _lru_specs(s, nb)
    pad = SUBLANES

    def body(xr_ref, xg_ref, wc_ref, bc_ref, wa_ref, ba_ref, wx_ref, bx_ref, lam_ref, o_ref, h_ref, pad_s, a_s, u_s):
        pad_s[0:pad, :] = jnp.zeros((pad, HEAD), F32)
        pad_s[pad:pad + s, :] = xr_ref[...]
        wab, wxb = wa_ref[...].astype(BF16), wx_ref[...].astype(BF16)
        sp = _softplus_neg(lam_ref[...])
        for c in range(s // tc):
            base = c * tc
            xc = bc_ref[...] + sum(wc_ref[i:i + 1, :] * pad_s[pl.ds(base + pad - (CONV_TAPS - 1) + i, tc), :]
                                   for i in range(CONV_TAPS))
            _, ig, a, mult = _lru_gates(xc, wab, ba_ref[...], wxb, bx_ref[...], sp)
            a_s[base:base + tc, :] = a
            u_s[base:base + tc, :] = mult * (ig * xc)

        row = lax.broadcasted_iota(jnp.int32, (SUBLANES, HEAD), 0)

        def chunk(ci, hprev):
            off = pl.multiple_of(ci * SUBLANES, SUBLANES)
            a8, b8 = a_s[pl.ds(off, SUBLANES), :], u_s[pl.ds(off, SUBLANES), :]
            for d in (1, 2, 4):
                a_sh = jnp.where(row < d, 1.0, pltpu.roll(a8, d, 0))
                b_sh = jnp.where(row < d, 0.0, pltpu.roll(b8, d, 0))
                b8 = a8 * b_sh + b8
                a8 = a8 * a_sh
            h8 = a8 * hprev + b8
            h_ref[pl.ds(off, SUBLANES), :] = h8
            return h8[SUBLANES - 1:SUBLANES, :]

        lax.fori_loop(0, s // SUBLANES, chunk, jnp.zeros((1, HEAD), F32), unroll=8)
        for c in range(s // tc):
            sl = slice(c * tc, (c + 1) * tc)
            gel, _ = _gelu_parts(xg_ref[sl, :])
            o_ref[sl, :] = h_ref[sl, :] * gel

    return pl.pallas_call(
        body, grid=(nb,),
        in_specs=[seq0, seq1, taps, vec, mat, vec, mat, vec, vec],
        out_specs=(seq0, seq0),
        out_shape=(SDS((s, nb * HEAD), F32), SDS((s, nb * HEAD), F32)),
        scratch_shapes=[pltpu.VMEM((s + pad, HEAD), F32), pltpu.VMEM((s, HEAD), F32), pltpu.VMEM((s, HEAD), F32)],
        compiler_params=_params(("parallel",)), name="lru_fwd")(xrg, xrg, wconv, bconv, wa, ba, wx, bx, lam)


def _lru_bwd(xrg, dol, hseq, wconv, bconv, wa, ba, wx, bx, lam):
    s = xrg.shape[0]
    nb = wa.shape[0]
    tc = _lru_chunk(s)
    seq0, seq1, taps, vec, mat = _lru_specs(s, nb)
    pad = SUBLANES

    def body(xr_ref, xg_ref, do_ref, h_ref, wc_ref, bc_ref, wa_ref, ba_ref, wx_ref, bx_ref, lam_ref,
             dxr_ref, dxg_ref, dwc_ref, dbc_ref, dwa_ref, dba_ref, dwx_ref, dbx_ref, dlam_ref,
             pad_s, hp_s, a_s, g_s, da_s, dxc_s):
        pad_s[0:pad, :] = jnp.zeros((pad, HEAD), F32)
        pad_s[pad:pad + s, :] = xr_ref[...]
        hp_s[0:pad, :] = jnp.zeros((pad, HEAD), F32)
        hp_s[pad:pad + s, :] = h_ref[...]
        a_s[s:s + pad, :] = jnp.zeros((pad, HEAD), F32)
        dxc_s[s:s + pad, :] = jnp.zeros((pad, HEAD), F32)
        wab, wxb = wa_ref[...].astype(BF16), wx_ref[...].astype(BF16)
        lam_v = lam_ref[...]
        sp = _softplus_neg(lam_v)

        def conv_in(c):
            base = c * tc
            wins = [pad_s[pl.ds(base + pad - (CONV_TAPS - 1) + i, tc), :] for i in range(CONV_TAPS)]
            xc = bc_ref[...] + sum(wc_ref[i:i + 1, :] * wins[i] for i in range(CONV_TAPS))
            return xc, wins

        for c in range(s // tc):
            sl = slice(c * tc, (c + 1) * tc)
            xc, _ = conv_in(c)
            _, _, a, _ = _lru_gates(xc, wab, ba_ref[...], wxb, bx_ref[...], sp)
            a_s[sl, :] = a
            gel, dgel = _gelu_parts(xg_ref[sl, :])
            dov = do_ref[sl, :]
            g_s[sl, :] = dov * gel
            dxg_ref[sl, :] = (dov * h_ref[sl, :] * dgel).astype(BF16)

        row = lax.broadcasted_iota(jnp.int32, (SUBLANES, HEAD), 0)
        n_chunks = s // SUBLANES

        def chunk(it, gnext):
            ci = n_chunks - 1 - it
            off = pl.multiple_of(ci * SUBLANES, SUBLANES)
            a8 = a_s[pl.ds(off, SUBLANES), :]
            a8n = a_s[pl.ds(off + SUBLANES, SUBLANES), :]
            c8 = pltpu.roll(jnp.where(row == 0, a8n, a8), SUBLANES - 1, 0)
            g8 = g_s[pl.ds(off, SUBLANES), :]
            for d in (1, 2, 4):
                c_sh = jnp.where(row >= SUBLANES - d, 1.0, pltpu.roll(c8, SUBLANES - d, 0))
                g_sh = jnp.where(row >= SUBLANES - d, 0.0, pltpu.roll(g8, SUBLANES - d, 0))
                g8 = c8 * g_sh + g8
                c8 = c8 * c_sh
            g8 = g8 + c8 * gnext
            g_s[pl.ds(off, SUBLANES), :] = g8
            h8 = hp_s[pl.ds(off + pad, SUBLANES), :]
            h8p = hp_s[pl.ds(off, SUBLANES), :]
            da_s[pl.ds(off, SUBLANES), :] = g8 * pltpu.roll(jnp.where(row == SUBLANES - 1, h8p, h8), 1, 0)
            return g8[0:1, :]

        lax.fori_loop(0, n_chunks, chunk, jnp.zeros((1, HEAD), F32), unroll=8)

        dsp = jnp.zeros((1, HEAD), F32)
        dbc = jnp.zeros((1, HEAD), F32)
        dba = jnp.zeros((1, HEAD), F32)
        dbx = jnp.zeros((1, HEAD), F32)
        dwa = jnp.zeros((HEAD, HEAD), F32)
        dwx = jnp.zeros((HEAD, HEAD), F32)
        dwc = [jnp.zeros((1, HEAD), F32) for _ in range(CONV_TAPS)]
        for c in range(s // tc):
            sl = slice(c * tc, (c + 1) * tc)
            xc, wins = conv_in(c)
            r, ig, a, mult = _lru_gates(xc, wab, ba_ref[...], wxb, bx_ref[...], sp)
            du, da = g_s[sl, :], da_s[sl, :]
            d_ix = du * mult
            dla = da * a - (du * ig * xc) * (a * a / mult)
            dsp = dsp + jnp.sum(dla * r, axis=0, keepdims=True) * (-LRU_C)
            dpa = (dla * (-LRU_C * sp)) * r * (1.0 - r)
            dpx = (d_ix * xc) * ig * (1.0 - ig)
            dpab, dpxb, xb = dpa.astype(BF16), dpx.astype(BF16), xc.astype(BF16)
            dxc = d_ix * ig + _dot_nt(dpab, wab) + _dot_nt(dpxb, wxb)
            dwa = dwa + _dot_tn(xb, dpab)
            dwx = dwx + _dot_tn(xb, dpxb)
            dba = dba + jnp.sum(dpa, axis=0, keepdims=True)
            dbx = dbx + jnp.sum(dpx, axis=0, keepdims=True)
            dbc = dbc + jnp.sum(dxc, axis=0, keepdims=True)
            for i in range(CONV_TAPS):
                dwc[i] = dwc[i] + jnp.sum(dxc * wins[i], axis=0, keepdims=True)
            dxc_s[sl, :] = dxc

        for c in range(s // tc):
            base = c * tc
            dxr = sum(wc_ref[i:i + 1, :] * dxc_s[pl.ds(base + (CONV_TAPS - 1) - i, tc), :] for i in range(CONV_TAPS))
            dxr_ref[base:base + tc, :] = dxr.astype(BF16)

        for i in range(CONV_TAPS):
            dwc_ref[i:i + 1, :] = dwc[i]
        dbc_ref[...] = dbc
        dwa_ref[...] = dwa
        dwx_ref[...] = dwx
        dba_ref[...] = dba
        dbx_ref[...] = dbx
        dlam_ref[...] = dsp * (-_sigmoid(-lam_v))

    w = nb * HEAD
    return pl.pallas_call(
        body, grid=(nb,),
        in_specs=[seq0, seq1, seq0, seq0, taps, vec, mat, vec, mat, vec, vec],
        out_specs=(seq0, seq0, taps, vec, mat, vec, mat, vec, vec),
        out_shape=(SDS((s, w), BF16), SDS((s, w), BF16), SDS((CONV_TAPS, w), F32), SDS((1, w), F32),
                   SDS((nb, HEAD, HEAD), F32), SDS((1, w), F32), SDS((nb, HEAD, HEAD), F32), SDS((1, w), F32),
                   SDS((1, w), F32)),
        scratch_shapes=[pltpu.VMEM((s + pad, HEAD), F32), pltpu.VMEM((s + pad, HEAD), F32),
                        pltpu.VMEM((s + pad, HEAD), F32), pltpu.VMEM((s, HEAD), F32),
                        pltpu.VMEM((s, HEAD), F32), pltpu.VMEM((s + pad, HEAD), F32)],
        compiler_params=_params(("parallel",)), name="lru_bwd",
    )(xrg, xrg, dol, hseq, wconv, bconv, wa, ba, wx, bx, lam)


def _ada_mod(c_all, w_sh, b_sh):
    n_ex, d = c_all.shape
    n = w_sh.shape[1]
    tn = _tile(n, 512)

    def body(c_ref, w_ref, b_ref, mod_ref, act_ref):
        cv = c_ref[...]
        act = cv * _sigmoid(cv)
        act_ref[...] = act
        mod_ref[...] = _dot(act.astype(BF16), w_ref[...].astype(BF16)) + b_ref[...]

    return pl.pallas_call(
        body, grid=(n // tn,),
        in_specs=[pl.BlockSpec((n_ex, d), lambda j: (0, 0)), pl.BlockSpec((d, tn), lambda j: (0, j)),
                  pl.BlockSpec((1, tn), lambda j: (0, j))],
        out_specs=(pl.BlockSpec((n_ex, tn), lambda j: (0, j)), pl.BlockSpec((n_ex, d), lambda j: (0, 0))),
        out_shape=(SDS((n_ex, n), F32), SDS((n_ex, d), F32)),
        compiler_params=_params(("arbitrary",)), name="ada_mod")(c_all, w_sh, b_sh)


def _adamw_math(w, g, m, v):
    m = ADAM_B1 * m + (1.0 - ADAM_B1) * g
    v = ADAM_B2 * v + (1.0 - ADAM_B2) * (g * g)
    m_hat = m / (1.0 - ADAM_B1 ** ADAM_STEP)
    v_hat = v / (1.0 - ADAM_B2 ** ADAM_STEP)
    delta = -ADAM_LR * (m_hat / (jnp.sqrt(v_hat) + ADAM_EPS) + ADAM_WD * w)
    return delta, m, v


def _adamw_plain(name, w, g, m, v):
    def body(w_ref, g_ref, m_ref, v_ref, d_ref, mo_ref, vo_ref):
        d_ref[...], mo_ref[...], vo_ref[...] = _adamw_math(w_ref[...], g_ref[...], m_ref[...], v_ref[...])

    return pl.pallas_call(body, out_shape=(SDS(w.shape, F32),) * 3, name=name)(w, g, m, v)


def _adamw_halves(name, c_arr, w, m, v, g_own, g_recv):
    r, n = w.shape
    rh = r // 2
    tr = _tile(rh, 256)
    nh = rh // tr

    def body(c_ref, w_ref, m_ref, v_ref, go_ref, gr_ref, g_ref, d_ref, mo_ref, vo_ref):
        own = (pl.program_id(0) // nh) == c_ref[0]
        g = jnp.where(own, go_ref[...], gr_ref[...])
        g_ref[...] = g
        d_ref[...], mo_ref[...], vo_ref[...] = _adamw_math(w_ref[...], g, m_ref[...], v_ref[...])

    full = pl.BlockSpec((tr, n), lambda i, c: (i, 0))
    half = pl.BlockSpec((tr, n), lambda i, c: (i % nh, 0))
    return pl.pallas_call(
        body,
        grid_spec=pltpu.PrefetchScalarGridSpec(
            num_scalar_prefetch=1, grid=(2 * nh,), in_specs=[full, full, full, half, half],
            out_specs=(full,) * 4),
        out_shape=(SDS((r, n), F32),) * 4,
        compiler_params=_params(("parallel",)), name=name)(c_arr, w, m, v, g_own, g_recv)


def _adamw_ada(w, m, v, act_t, dmod):
    d, n = w.shape
    n_ex = act_t.shape[1]
    tr = _tile(d, 256)

    def body(a_ref, dm_ref, w_ref, m_ref, v_ref, g_ref, d_ref, mo_ref, vo_ref):
        g = _dot(a_ref[...], dm_ref[...])
        g_ref[...] = g
        d_ref[...], mo_ref[...], vo_ref[...] = _adamw_math(w_ref[...], g, m_ref[...], v_ref[...])

    full = pl.BlockSpec((tr, n), lambda i: (i, 0))
    return pl.pallas_call(
        body, grid=(d // tr,),
        in_specs=[pl.BlockSpec((tr, n_ex), lambda i: (i, 0)), pl.BlockSpec((n_ex, n), lambda i: (0, 0)), full, full, full],
        out_specs=(full,) * 4, out_shape=(SDS((d, n), F32),) * 4,
        compiler_params=_params(("parallel",)), name="adamw_ada")(act_t, dmod, w, m, v)


def _small_reduce_adamw(parts, w, m, v):
    n_dev, r, _ = parts.shape
    tr = r if r <= PACK_ROWS else PACK_ROWS

    def body(p_ref, w_ref, m_ref, v_ref, g_ref, d_ref, mo_ref, vo_ref):
        g = p_ref[0]
        for k in range(1, n_dev):
            g = g + p_ref[k]
        g_ref[...] = g
        d_ref[...], mo_ref[...], vo_ref[...] = _adamw_math(w_ref[...], g, m_ref[...], v_ref[...])

    full = pl.BlockSpec((tr, LANES), lambda i: (i, 0))
    return pl.pallas_call(
        body, grid=(r // tr,),
        in_specs=[pl.BlockSpec((n_dev, tr, LANES), lambda i: (0, i, 0)), full, full, full],
        out_specs=(full,) * 4, out_shape=(SDS((r, LANES), F32),) * 4,
        compiler_params=_params(("parallel",)), name="small_reduce_adamw")(parts, w, m, v)


def _mesh_pos():
    return lax.axis_index("x"), lax.axis_index("y"), lax.axis_index("c")


def _other_chips(x, y):
    return [(1 - x, y), (x, 1 - y), (1 - x, 1 - y)]


def _all_gather_small(name, blk):
    r, n = blk.shape

    def body(x_ref, out_ref, send_sems, recv_sems, local_sem):
        x, y, c = _mesh_pos()
        me, sibling = (x, y, c), (x, y, 1 - c)
        chips = _other_chips(x, y)

        def rows(px, py, pc):
            return out_ref.at[4 * px + 2 * py + pc]

        def copy(k, block, to, src=None):
            return pltpu.make_async_remote_copy(
                src_ref=rows(*block) if src is None else src, dst_ref=rows(*block),
                send_sem=send_sems.at[k], recv_sem=recv_sems.at[k], device_id=to, device_id_type=MESH)

        mine = pltpu.make_async_copy(x_ref, rows(*me), local_sem)
        mine.start()
        first = [copy(0, me, sibling, src=x_ref)]
        first += [copy(1 + j, me, (*chip, c), src=x_ref) for j, chip in enumerate(chips)]
        for cp in first:
            cp.start()
        passed = [copy(4 + j, (*chip, c), sibling) for j, chip in enumerate(chips)]
        for j, chip in enumerate(chips):
            copy(1 + j, (*chip, c), me).wait_recv()
            passed[j].start()
        copy(0, sibling, me).wait_recv()
        for j, chip in enumerate(chips):
            copy(4 + j, (*chip, 1 - c), me).wait_recv()
        for cp in first + passed:
            cp.wait_send()
        mine.wait()

    return pl.pallas_call(
        body, out_shape=SDS((N_DEV, r, n), blk.dtype),
        in_specs=[pl.BlockSpec(memory_space=pltpu.VMEM)], out_specs=pl.BlockSpec(memory_space=pltpu.VMEM),
        scratch_shapes=[pltpu.SemaphoreType.DMA((7,)), pltpu.SemaphoreType.DMA((7,)), pltpu.SemaphoreType.DMA],
        compiler_params=pltpu.CompilerParams(vmem_limit_bytes=VMEM_LIMIT), name=name)(blk)


_ANY = pl.BlockSpec(memory_space=pl.ANY)
_HBM = pl.BlockSpec(memory_space=pltpu.HBM)
_SEM = pl.BlockSpec(memory_space=pltpu.SEMAPHORE)
_EFFECT = pltpu.SideEffectType.DATAFLOW_SIDE_EFFECTING


def _hbm(a):
    return pltpu.with_memory_space_constraint(a, pltpu.HBM)


def _gather_start(shards, kinds, groups):
    nw = len(shards)
    ng = len(groups)

    def body(*refs):
        sh = refs[:nw]
        outs = refs[2 * nw:]
        send, recv, full, token, local_sems = outs[:ng], outs[ng:2 * ng], outs[2 * ng:2 * ng + nw], outs[2 * ng + nw], outs[-1]
        x, y, c = _mesh_pos()
        chips = _other_chips(x, y)
        local = []
        for w in range(nw):
            for half in range(2):
                cp = pltpu.make_async_copy(_shard_half(sh[w], half), _full_region(full[w], kinds[w], x, y, half),
                                           local_sems.at[2 * w + half])
                cp.start()
                local.append(cp)
        for g, ws in enumerate(groups):
            for li, w in enumerate(ws):
                for k, chip in enumerate(chips):
                    pltpu.make_async_remote_copy(
                        src_ref=_shard_half(sh[w], c), dst_ref=_full_region(full[w], kinds[w], x, y, c),
                        send_sem=send[g].at[3 * li + k], recv_sem=recv[g].at[3 * li + k],
                        device_id=(*chip, c), device_id_type=MESH).start()
        for cp in local:
            cp.wait()
        token[...] = jnp.zeros_like(token)

    fulls = []
    for s_, kind in zip(shards, kinds):
        r, n = s_.shape
        fulls.append(((r, N_CHIP * n) if kind == "col" else (N_CHIP * r, n), s_.dtype))
    sems = tuple(pltpu.SemaphoreType.DMA((3 * len(ws),)) for ws in groups)
    outs = pl.pallas_call(
        body,
        out_shape=sems + sems + tuple(pltpu.HBM(shp, dt) for shp, dt in fulls) + (SDS((SUBLANES, LANES), F32),),
        in_specs=[_HBM] * (2 * nw),
        out_specs=tuple([_SEM] * (2 * ng) + [_HBM] * nw + [pl.BlockSpec(memory_space=pltpu.VMEM)]),
        input_output_aliases={nw + w: 2 * ng + w for w in range(nw)},
        scratch_shapes=[pltpu.SemaphoreType.DMA((2 * nw,))],
        compiler_params=pltpu.CompilerParams(has_side_effects=_EFFECT),
        name="gather_start",
    )(*[_hbm(s_) for s_ in shards], *[_hbm(lax.empty(shp, dt)) for shp, dt in fulls])
    return outs[:ng], outs[ng:2 * ng], outs[2 * ng:2 * ng + nw], outs[2 * ng + nw]


def _shard_half(ref, half):
    rh = ref.shape[0] // 2
    return ref.at[pl.ds(half * rh, rh), :]


def _full_region(full, kind, px, py, half):
    j = 2 * px + py
    if kind == "col":
        rh, cols = full.shape[0] // 2, full.shape[1] // N_CHIP
        return full.at[pl.ds(half * rh, rh), pl.ds(j * cols, cols)]
    rows = full.shape[0] // N_CHIP
    rh = rows // 2
    return full.at[pl.ds(j * rows + half * rh, rh), :]


def _gather_pass(name, fulls, shards, kinds, send, recv, after):
    nw = len(fulls)

    def body(*refs):
        full_in, sh, send_r, recv_r = refs[:nw], refs[nw:2 * nw], refs[2 * nw], refs[2 * nw + 1]
        full, fsend, frecv = refs[2 * nw + 3:3 * nw + 3], refs[3 * nw + 3], refs[3 * nw + 4]
        x, y, c = _mesh_pos()
        chips = _other_chips(x, y)
        for w in range(nw):
            for k, chip in enumerate(chips):
                landed = _full_region(full[w], kinds[w], *chip, c)
                arrive = pltpu.make_async_remote_copy(
                    src_ref=_shard_half(sh[w], c), dst_ref=landed, send_sem=send_r.at[3 * w + k],
                    recv_sem=recv_r.at[3 * w + k], device_id=(*chip, c), device_id_type=MESH)
                arrive.wait_recv()
                arrive.wait_send()
                pltpu.make_async_remote_copy(
                    src_ref=landed, dst_ref=landed, send_sem=fsend.at[3 * w + k], recv_sem=frecv.at[3 * w + k],
                    device_id=(x, y, 1 - c), device_id_type=MESH).start()

    sem = pltpu.SemaphoreType.DMA((3 * nw,))
    outs = pl.pallas_call(
        body,
        out_shape=tuple(pltpu.HBM(f_.shape, f_.dtype) for f_ in fulls) + (sem, sem),
        in_specs=[_HBM] * (2 * nw) + [_SEM, _SEM, _ANY],
        out_specs=tuple([_HBM] * nw + [_SEM, _SEM]),
        input_output_aliases={w: w for w in range(nw)},
        compiler_params=pltpu.CompilerParams(has_side_effects=_EFFECT),
        name=name,
    )(*fulls, *shards, send, recv, after)
    return outs[:nw], outs[nw], outs[nw + 1]


def _gather_finish(name, fulls, kinds, fsend, frecv, after):
    nw = len(fulls)

    def body(*refs):
        fsend_r, frecv_r = refs[nw], refs[nw + 1]
        full = refs[nw + 3:2 * nw + 3]
        x, y, c = _mesh_pos()
        chips = _other_chips(x, y)
        for w in range(nw):
            for k, chip in enumerate(chips):
                cp = pltpu.make_async_remote_copy(
                    src_ref=_full_region(full[w], kinds[w], *chip, c), dst_ref=_full_region(full[w], kinds[w], *chip, 1 - c),
                    send_sem=fsend_r.at[3 * w + k], recv_sem=frecv_r.at[3 * w + k],
                    device_id=(x, y, 1 - c), device_id_type=MESH)
                cp.wait_send()
                cp.wait_recv()

    outs = pl.pallas_call(
        body,
        out_shape=tuple(pltpu.HBM(f_.shape, f_.dtype) for f_ in fulls),
        in_specs=[_HBM] * nw + [_SEM, _SEM, _ANY],
        out_specs=tuple([_HBM] * nw),
        input_output_aliases={w: w for w in range(nw)},
        compiler_params=pltpu.CompilerParams(has_side_effects=_EFFECT),
        name=name,
    )(*fulls, fsend, frecv, after)
    return list(outs)


def _half_of(ref, kind, half):
    if kind == "col":
        rh = ref.shape[0] // 2
        return ref.at[pl.ds(half * rh, rh), :]
    rh = ref.shape[1] // 2
    return ref.at[:, pl.ds(half * rh, rh), :]


def _half_shape(g, kind):
    if kind == "col":
        return (g.shape[0] // 2, g.shape[1])
    return (g.shape[0], g.shape[1] // 2, g.shape[2])


def _swap_halves(grads, kinds):
    nw = len(grads)

    def body(*refs):
        g, got = refs[:nw], refs[nw:2 * nw]
        send_sems, recv_sems = refs[2 * nw:]
        x, y, c = _mesh_pos()
        cps = []
        for w in range(nw):
            cp = pltpu.make_async_remote_copy(
                src_ref=_half_of(g[w], kinds[w], 1 - c), dst_ref=got[w],
                send_sem=send_sems.at[w], recv_sem=recv_sems.at[w], device_id=(x, y, 1 - c), device_id_type=MESH)
            cp.start()
            cps.append(cp)
        for cp in cps:
            cp.wait()

    return pl.pallas_call(
        body, out_shape=tuple(SDS(_half_shape(g, k), g.dtype) for g, k in zip(grads, kinds)),
        in_specs=[_ANY] * nw, out_specs=tuple([_ANY] * nw),
        scratch_shapes=[pltpu.SemaphoreType.DMA((nw,)), pltpu.SemaphoreType.DMA((nw,))],
        name="comm_swap_halves")(*grads)


def _add_halves(name, c_arr, g, got, kind):
    if kind == "col":
        rh, n = got.shape
        tr = _tile(rh, 256)
        nh = rh // tr
        g_spec = pl.BlockSpec((tr, n), lambda i, c: (c[0] * nh + i, 0))
        o_spec = pl.BlockSpec((tr, n), lambda i, c: (i, 0))
        grid = (nh,)
    else:
        nc, rh, n = got.shape
        tr = _tile(rh, 256)
        nh = rh // tr
        g_spec = pl.BlockSpec((None, tr, n), lambda j, i, c: (j, c[0] * nh + i, 0))
        o_spec = pl.BlockSpec((None, tr, n), lambda j, i, c: (j, i, 0))
        grid = (nc, nh)

    def body(c_ref, g_ref, r_ref, o_ref):
        o_ref[...] = (g_ref[...].astype(F32) + r_ref[...].astype(F32)).astype(o_ref.dtype)

    return pl.pallas_call(
        body,
        grid_spec=pltpu.PrefetchScalarGridSpec(num_scalar_prefetch=1, grid=grid, in_specs=[g_spec, o_spec], out_specs=o_spec),
        out_shape=SDS(got.shape, got.dtype),
        compiler_params=_params(("parallel",) * len(grid)), name=name)(c_arr, g, got)


def _scatter_partials(parts, kinds):
    nw = len(parts)

    def blk_shape(p, kind):
        return (p.shape[0], p.shape[1] // N_CHIP) if kind == "col" else (p.shape[1], p.shape[2])

    def body(*refs):
        p, got = refs[:nw], refs[nw:2 * nw]
        send_sems, recv_sems = refs[2 * nw:]
        x, y, c = _mesh_pos()
        chips = _other_chips(x, y)
        cps = []
        for w in range(nw):
            for k, (px, py) in enumerate(chips):
                j = 2 * px + py
                if kinds[w] == "col":
                    n = p[w].shape[1] // N_CHIP
                    src = p[w].at[:, pl.ds(j * n, n)]
                else:
                    src = p[w].at[j]
                cp = pltpu.make_async_remote_copy(
                    src_ref=src, dst_ref=got[w].at[k], send_sem=send_sems.at[3 * w + k], recv_sem=recv_sems.at[3 * w + k],
                    device_id=(px, py, c), device_id_type=MESH)
                cp.start()
                cps.append(cp)
        for cp in cps:
            cp.wait()

    return pl.pallas_call(
        body, out_shape=tuple(SDS((3,) + blk_shape(p, k), p.dtype) for p, k in zip(parts, kinds)),
        in_specs=[_ANY] * nw, out_specs=tuple([_ANY] * nw),
        scratch_shapes=[pltpu.SemaphoreType.DMA((3 * nw,)), pltpu.SemaphoreType.DMA((3 * nw,))],
        name="comm_scatter_partials")(*parts)


def _sum_partials(name, j_arr, part, got, kind):
    _, rh, n = got.shape
    tr = _tile(rh, 256)
    if kind == "col":
        p_spec = pl.BlockSpec((tr, n), lambda i, j: (i, j[0]))
    else:
        p_spec = pl.BlockSpec((None, tr, n), lambda i, j: (j[0], i, 0))

    def body(j_ref, p_ref, r_ref, o_ref):
        o_ref[...] = ((p_ref[...].astype(F32) + r_ref[0].astype(F32)) + r_ref[1].astype(F32)) + r_ref[2].astype(F32)

    return pl.pallas_call(
        body,
        grid_spec=pltpu.PrefetchScalarGridSpec(
            num_scalar_prefetch=1, grid=(rh // tr,),
            in_specs=[p_spec, pl.BlockSpec((3, tr, n), lambda i, j: (0, i, 0))],
            out_specs=pl.BlockSpec((tr, n), lambda i, j: (i, 0))),
        out_shape=SDS((rh, n), F32),
        compiler_params=_params(("parallel",)), name=name)(j_arr, part, got)


def _swap_reduced(halves):
    nw = len(halves)

    def body(*refs):
        h, got = refs[:nw], refs[nw:2 * nw]
        send_sems, recv_sems = refs[2 * nw:]
        x, y, c = _mesh_pos()
        cps = []
        for w in range(nw):
            cp = pltpu.make_async_remote_copy(
                src_ref=h[w], dst_ref=got[w], send_sem=send_sems.at[w], recv_sem=recv_sems.at[w],
                device_id=(x, y, 1 - c), device_id_type=MESH)
            cp.start()
            cps.append(cp)
        for cp in cps:
            cp.wait()

    return pl.pallas_call(
        body, out_shape=tuple(SDS(h.shape, h.dtype) for h in halves),
        in_specs=[_ANY] * nw, out_specs=tuple([_ANY] * nw),
        scratch_shapes=[pltpu.SemaphoreType.DMA((nw,)), pltpu.SemaphoreType.DMA((nw,))],
        name="comm_swap_reduced")(*halves)


def _pack(arrays):
    flat = [a.reshape(-1).astype(F32) for a in arrays]
    flat = [jnp.pad(f, (0, (-f.shape[0]) % LANES)) for f in flat]
    sizes = [f.shape[0] for f in flat]
    total = sum(sizes)
    rows = total // LANES
    tail = LANES * ((-rows) % (PACK_ROWS if rows > PACK_ROWS else SUBLANES))
    if tail:
        flat.append(jnp.zeros((tail,), F32))
    return jnp.concatenate(flat).reshape(-1, LANES), sizes


def _unpack(slab, sizes, shapes, lead=()):
    flat = slab.reshape(lead + (-1,))
    out, off = [], 0
    for sz, shp in zip(sizes, shapes):
        n = math.prod(shp)
        out.append(flat[..., off:off + n].reshape(lead + tuple(shp)))
        off += sz
    return out


def kernel(x, c, w_ada, b_ada, g_norm_mix, w_in, w_conv, b_conv, w_rg_a, b_rg_a, w_rg_x, b_rg_x, lru_lambda, g_attn_out, g_lru_out, w_out, g_norm_mlp, w_mlp_in, w_mlp_out, g_norm_final, loss_target, m_w_ada, m_b_ada, m_g_norm_mix, m_w_in, m_w_conv, m_b_conv, m_w_rg_a, m_b_rg_a, m_w_rg_x, m_b_rg_x, m_lru_lambda, m_g_attn_out, m_g_lru_out, m_w_out, m_g_norm_mlp, m_w_mlp_in, m_w_mlp_out, m_g_norm_final, v_w_ada, v_b_ada, v_g_norm_mix, v_w_in, v_w_conv, v_b_conv, v_w_rg_a, v_b_rg_a, v_w_rg_x, v_b_rg_x, v_lru_lambda, v_g_attn_out, v_g_lru_out, v_w_out, v_g_norm_mlp, v_w_mlp_in, v_w_mlp_out, v_g_norm_final):
    s, d = x.shape[1], x.shape[2]
    aw = d // 2
    nh = aw // HEAD
    f = w_mlp_out.shape[1] * N_CHIP
    n_ada = w_ada.shape[2]
    n_cv = w_conv.shape[2]
    ix, iy, ic = lax.axis_index("x"), lax.axis_index("y"), lax.axis_index("c")
    chip = 2 * ix + iy
    me = 2 * chip + ic
    c_arr = jnp.reshape(ic, (1,)).astype(jnp.int32)
    j_arr = jnp.reshape(chip, (1,)).astype(jnp.int32)

    x2d, tgt = x[0], loss_target[0]

    kinds = ("col", "row", "col", "row")
    groups = ((0,), (1, 2), (3,))
    shards = [w_in[0].astype(BF16), w_out[0].astype(BF16), w_mlp_in[0].astype(BF16), w_mlp_out[0].astype(BF16)]
    ag_send, ag_recv, ag_full, ag_token = _gather_start(shards, kinds, groups)

    def gather_pass(g, after):
        ws = groups[g]
        return _gather_pass("gather_pass_%d" % g, [ag_full[w] for w in ws], [shards[w] for w in ws],
                            [kinds[w] for w in ws], ag_send[g], ag_recv[g], after)

    slab, sizes = _pack([c + ag_token[0, 0], w_conv])
    gathered = _all_gather_small("comm_gather_cond", slab)
    c_parts, cv_parts = _unpack(gathered, sizes, [(d,), (CONV_TAPS, n_cv)], lead=(N_DEV,))
    c_all = c_parts
    w_conv_full = jnp.concatenate([cv_parts[2 * j] for j in range(N_CHIP)], axis=-1)
    b_sh = lax.dynamic_slice(b_ada, (0, chip * n_ada), (1, n_ada))
    mod_part, act_all = _ada_mod(c_all, w_ada[0], b_sh)
    mod_g = _all_gather_small("comm_gather_mod", mod_part.reshape(-1, LANES))
    mod_g = mod_g.reshape(N_DEV, N_DEV, n_ada)
    mod = jnp.concatenate([lax.dynamic_index_in_dim(mod_g[2 * j], me, 0, keepdims=True) for j in range(N_CHIP)], axis=-1)
    sh1, sc1, gt1, sh2, sc2, gt2 = [mod[:, k * d:(k + 1) * d] for k in range(N_MOD)]

    fl, fs, fr = gather_pass(0, mod)
    h1, rstd1 = _norm_mod_fwd("norm_mod_fwd1", x2d, g_norm_mix, sc1, sh1)
    (w_in_f,) = _gather_finish("gather_finish_0", fl, kinds[0:1], fs, fr, h1)
    (qkv,) = _matmul("mm_qkv", h1, w_in_f, "nn", s, 3 * aw, d, (BF16,))
    (xrg,) = _matmul("mm_xrg", h1, w_in_f, "nn", s, 2 * aw, d, (F32,), b_off=3 * aw)
    o_attn = _attn_fwd(qkv, nh)
    fl, fs, fr = gather_pass(1, o_attn)
    wa3, wx3 = w_rg_a[0], w_rg_x[0]
    o_lru, hseq = _lru_fwd(xrg, w_conv_full, b_conv, wa3, b_rg_a, wx3, b_rg_x, lru_lambda)
    mixed, rstd_a, rstd_l = _mix_norm_fwd(o_attn, o_lru, g_attn_out, g_lru_out)
    w_out_f, w_mi_f = _gather_finish("gather_finish_1", fl, kinds[1:3], fs, fr, mixed)

    def residual(acc, xin, gt):
        return acc, xin + gt * acc

    y1, x1 = _matmul("mm_out", mixed, w_out_f, "nn", s, d, d, (F32, F32), extras=(x2d, gt1),
                     extra_kinds=("tile", "row"), epilogue=residual)
    h2, rstd2 = _norm_mod_fwd("norm_mod_fwd2", x1, g_norm_mlp, sc2, sh2)
    fl, fs, fr = gather_pass(2, h2)

    def sq_relu(acc):
        r = jnp.maximum(acc, 0.0)
        return 2.0 * r, r * r

    r2, hid = _matmul("mm_mlp_in", h2, w_mi_f, "nn", s, f, d, (BF16, BF16), epilogue=sq_relu)
    (w_mo_f,) = _gather_finish("gather_finish_2", fl, kinds[3:4], fs, fr, hid)
    y2, x2 = _matmul("mm_mlp_out", hid, w_mo_f, "nn", s, d, f, (F32, F32), extras=(x1, gt2),
                     extra_kinds=("tile", "row"), epilogue=residual)
    dx2, loss_row, dg_final = _final_loss(x2, g_norm_final.reshape(1, d), tgt)

    dy2, dgt2 = _gate_bwd("gate_bwd2", dx2, y2, gt2)
    (dpre,) = _matmul("mm_dhid", dy2, w_mo_f, "nt", s, f, d, (BF16,), extras=(r2,), extra_kinds=("tile",),
                      epilogue=lambda acc, r: (acc * r.astype(F32),))
    (g_mo,) = _matmul("mm_dw_mlp_out", hid, dy2, "tn", f, d, s, (BF16,))
    (dh2,) = _matmul("mm_dh2", dpre, w_mi_f, "nt", s, d, f, (F32,))
    (g_mi,) = _matmul("mm_dw_mlp_in", h2, dpre, "tn", d, f, s, (BF16,))
    dx1, dsh2, dsc2, dg_mlp = _norm_mod_bwd("norm_mod_bwd2", dh2, x1, rstd2, g_norm_mlp, sc2, dx2)
    dy1, dgt1 = _gate_bwd("gate_bwd1", dx1, y1, gt1)
    (dmixed,) = _matmul("mm_dmixed", dy1, w_out_f, "nt", s, d, d, (F32,))
    (g_out,) = _matmul("mm_dw_out", mixed, dy1, "tn", d, d, s, (BF16,))
    do_attn, do_lru, dg_attn, dg_lru = _mix_norm_bwd(dmixed, o_attn, o_lru, rstd_a, rstd_l, g_attn_out, g_lru_out)
    dq, dk, dv = _attn_bwd(qkv, do_attn, nh)
    dxr, dxg, dwconv, dbconv, dwa, dba, dwx, dbx, dlam = _lru_bwd(
        xrg, do_lru, hseq, w_conv_full, b_conv, wa3, b_rg_a, wx3, b_rg_x, lru_lambda)
    dproj = jnp.concatenate([dq, dk, dv, dxr, dxg], axis=-1)
    (dh1,) = _matmul("mm_dh1", dproj, w_in_f, "nt", s, d, 5 * aw, (F32,))
    (g_in,) = _matmul("mm_dw_in", h1, dproj, "tn", d, 5 * aw, s, (BF16,))
    grad_x, dsh1, dsc1, dg_mix = _norm_mod_bwd("norm_mod_bwd1", dh1, x2d, rstd1, g_norm_mix, sc1, dx1)

    dmod = jnp.concatenate([dsh1, dsc1, dgt1, dsh2, dsc2, dgt2], axis=-1)
    small_names = ["b_ada", "g_norm_mix", "b_conv", "w_rg_a", "b_rg_a", "w_rg_x", "b_rg_x", "lru_lambda",
                   "g_attn_out", "g_lru_out", "g_norm_mlp", "g_norm_final"]
    small_g = [dmod, dg_mix, dbconv, dwa, dba, dwx, dbx, dlam, dg_attn, dg_lru, dg_mlp, dg_final]
    small_w = [b_ada, g_norm_mix, b_conv, w_rg_a, b_rg_a, w_rg_x, b_rg_x, lru_lambda, g_attn_out, g_lru_out, g_norm_mlp, g_norm_final]
    small_m = [m_b_ada, m_g_norm_mix, m_b_conv, m_w_rg_a, m_b_rg_a, m_w_rg_x, m_b_rg_x, m_lru_lambda, m_g_attn_out, m_g_lru_out, m_g_norm_mlp, m_g_norm_final]
    small_v = [v_b_ada, v_g_norm_mix, v_b_conv, v_w_rg_a, v_b_rg_a, v_w_rg_x, v_b_rg_x, v_lru_lambda, v_g_attn_out, v_g_lru_out, v_g_norm_mlp, v_g_norm_final]
    extra_zero = [jnp.zeros_like(dwconv), jnp.zeros((LANES,), F32)]
    g_slab, g_sizes = _pack(small_g + [dwconv, loss_row])
    w_slab, _ = _pack(small_w + extra_zero)
    m_slab, _ = _pack(small_m + extra_zero)
    v_slab, _ = _pack(small_v + extra_zero)
    g_all = _all_gather_small("comm_gather_small_grads", g_slab)
    gs_slab, ds_slab, ms_slab, vs_slab = _small_reduce_adamw(g_all, w_slab, m_slab, v_slab)
    shapes = [w.shape for w in small_w] + [dwconv.shape, (LANES,)]
    gs = _unpack(gs_slab, g_sizes, shapes)
    ds = _unpack(ds_slab, g_sizes, shapes)
    ms = _unpack(ms_slab, g_sizes, shapes)
    vs = _unpack(vs_slab, g_sizes, shapes)
    small = {n: (gs[i], ds[i], ms[i], vs[i]) for i, n in enumerate(small_names)}
    loss = gs[-1][0]
    g_wconv = lax.dynamic_slice(gs[-2], (0, chip * n_cv), (CONV_TAPS, n_cv))
    d_wconv, m_wconv, v_wconv = _adamw_plain("adamw_conv", w_conv[0], g_wconv, m_w_conv[0], v_w_conv[0])
    small["w_conv"] = (g_wconv[None], d_wconv[None], m_wconv[None], v_wconv[None])

    dmod_all = _unpack(g_all, g_sizes, [(N_MOD * d,)], lead=(N_DEV,))[0]
    dmod_sel = lax.dynamic_slice(dmod_all, (0, chip * n_ada), (N_DEV, n_ada)).astype(BF16)
    act_t = act_all.T.astype(BF16)
    big = {"w_ada": _adamw_ada(w_ada[0], m_w_ada[0], v_w_ada[0], act_t, dmod_sel)}

    grads = [g_in, g_out.reshape(N_CHIP, d // N_CHIP, d), g_mi, g_mo.reshape(N_CHIP, f // N_CHIP, d)]
    got1 = _swap_halves(grads, kinds)
    parts = [_add_halves("add_halves_%d" % i, c_arr, g, r, k) for i, (g, r, k) in enumerate(zip(grads, got1, kinds))]
    got2 = _scatter_partials(parts, kinds)
    halves = [_sum_partials("sum_partials_%d" % i, j_arr, p, r, k) for i, (p, r, k) in enumerate(zip(parts, got2, kinds))]
    got3 = _swap_reduced(halves)
    for i, (n, w_, m_, v_) in enumerate([("w_in", w_in, m_w_in, v_w_in), ("w_out", w_out, m_w_out, v_w_out),
                                          ("w_mlp_in", w_mlp_in, m_w_mlp_in, v_w_mlp_in),
                                          ("w_mlp_out", w_mlp_out, m_w_mlp_out, v_w_mlp_out)]):
        big[n] = _adamw_halves("adamw_" + n, c_arr, w_[0], m_[0], v_[0], halves[i], got3[i])

    order = ["w_ada", "b_ada", "g_norm_mix", "w_in", "w_conv", "b_conv", "w_rg_a", "b_rg_a", "w_rg_x", "b_rg_x",
             "lru_lambda", "g_attn_out", "g_lru_out", "w_out", "g_norm_mlp", "w_mlp_in", "w_mlp_out", "g_norm_final"]
    res = {}
    for n in order:
        res[n] = tuple(t[None] for t in big[n]) if n in big else small[n]
    return (loss, grad_x[None],
            *[res[n][0] for n in order], *[res[n][1] for n in order],
            *[res[n][2] for n in order], *[res[n][3] for n in order])
```

```python
import functools
import math

import jax
import jax.numpy as jnp
from jax import lax
from jax.experimental import pallas as pl
from jax.experimental.pallas import tpu as pltpu

F32 = jnp.float32
BF16 = jnp.bfloat16
SDS = jax.ShapeDtypeStruct
MESH = pl.DeviceIdType.MESH

EPS = 1e-6
HEAD = 128
N_MOD = 6
CONV_TAPS = 4
LRU_C = 8.0
ADAM_LR, ADAM_B1, ADAM_B2, ADAM_EPS, ADAM_WD, ADAM_STEP = 0.001, 0.9, 0.999, 1e-08, 0.01, 10
N_DEV = 8
N_CHIP = 4
LANES = 128
SUBLANES = 8
VMEM_LIMIT = 56 * 1024 * 1024
PACK_ROWS = 256


def _tile(dim, pref):
    t = min(dim, pref)
    while dim % t:
        t -= LANES
    return t


def _params(sem=None):
    return pltpu.CompilerParams(dimension_semantics=sem, vmem_limit_bytes=VMEM_LIMIT)


def _sigmoid(x):
    return 1.0 / (1.0 + jnp.exp(-x))


def _log_sigmoid(x):
    return jnp.minimum(x, 0.0) - jnp.log(1.0 + jnp.exp(-jnp.abs(x)))


def _gelu_parts(x):
    k0, k1 = math.sqrt(2.0 / math.pi), 0.044715
    t = jnp.tanh(k0 * (x + k1 * x * x * x))
    val = 0.5 * x * (1.0 + t)
    der = 0.5 * (1.0 + t) + 0.5 * x * (1.0 - t * t) * k0 * (1.0 + 3.0 * k1 * x * x)
    return val, der


def _dot(a, b):
    return jnp.dot(a, b, preferred_element_type=F32)


def _dot_nt(a, b):
    return lax.dot_general(a, b, (((1,), (1,)), ((), ())), preferred_element_type=F32)


def _dot_tn(a, b):
    return lax.dot_general(a, b, (((0,), (0,)), ((), ())), preferred_element_type=F32)


def _split_dot(x, tri):
    hi = x.astype(BF16)
    lo = (x - hi.astype(F32)).astype(BF16)
    return _dot(hi, tri) + _dot(lo, tri)


def _matmul(name, a, b, mode, m, n, k, out_dtypes, *, b_off=0, extras=(), extra_kinds=(), epilogue=None,
            tm=1024, tn=1024, tk=512):
    tm, tn, tk = _tile(m, tm), _tile(math.gcd(n, b_off) if b_off else n, tn), _tile(k, tk)
    assert b_off % tn == 0
    nk = k // tk
    n_ex, n_out = len(extras), len(out_dtypes)

    def body(a_ref, b_ref, *rest):
        ex, outs, acc = rest[:n_ex], rest[n_ex:n_ex + n_out], rest[-1]
        kk = pl.program_id(2)

        @pl.when(kk == 0)
        def _():
            acc[...] = jnp.zeros_like(acc)

        if mode == "nn":
            acc[...] += _dot(a_ref[...], b_ref[...])
        elif mode == "nt":
            acc[...] += _dot_nt(a_ref[...], b_ref[...])
        else:
            acc[...] += _dot_tn(a_ref[...], b_ref[...])

        @pl.when(kk == nk - 1)
        def _():
            res = epilogue(acc[...], *[e[...] for e in ex]) if epilogue else (acc[...],)
            for o, r in zip(outs, res):
                o[...] = r.astype(o.dtype)

    if mode == "nn":
        a_spec = pl.BlockSpec((tm, tk), lambda i, j, kk: (i, kk))
        b_spec = pl.BlockSpec((tk, tn), lambda i, j, kk: (kk, j + b_off // tn))
    elif mode == "nt":
        a_spec = pl.BlockSpec((tm, tk), lambda i, j, kk: (i, kk))
        b_spec = pl.BlockSpec((tn, tk), lambda i, j, kk: (j, kk + b_off // tk))
    else:
        a_spec = pl.BlockSpec((tk, tm), lambda i, j, kk: (kk, i))
        b_spec = pl.BlockSpec((tk, tn), lambda i, j, kk: (kk, j))
    tile_spec = pl.BlockSpec((tm, tn), lambda i, j, kk: (i, j))
    row_spec = pl.BlockSpec((1, tn), lambda i, j, kk: (0, j))
    ex_specs = [tile_spec if kind == "tile" else row_spec for kind in extra_kinds]
    outs = pl.pallas_call(
        body, grid=(m // tm, n // tn, nk),
        in_specs=[a_spec, b_spec] + ex_specs,
        out_specs=tuple(tile_spec for _ in out_dtypes),
        out_shape=tuple(SDS((m, n), dt) for dt in out_dtypes),
        scratch_shapes=[pltpu.VMEM((tm, tn), F32)],
        compiler_params=_params(("parallel", "parallel", "arbitrary")),
        name=name,
    )(a, b, *extras)
    return outs


def _row_specs(s, d, tr):
    row = pl.BlockSpec((tr, d), lambda i: (i, 0))
    vec = pl.BlockSpec((1, d), lambda i: (0, 0))
    col = pl.BlockSpec((tr, 1), lambda i: (i, 0))
    return row, vec, col


def _norm_mod_fwd(name, x, g, sc, sh):
    s, d = x.shape
    tr = _tile(s, 256)
    row, vec, col = _row_specs(s, d, tr)

    def body(x_ref, g_ref, sc_ref, sh_ref, h_ref, r_ref):
        xv = x_ref[...]
        r = lax.rsqrt(jnp.mean(xv * xv, axis=-1, keepdims=True) + EPS)
        h_ref[...] = ((xv * r * g_ref[...]) * (1.0 + sc_ref[...]) + sh_ref[...]).astype(BF16)
        r_ref[...] = r

    return pl.pallas_call(
        body, grid=(s // tr,), in_specs=[row, vec, vec, vec], out_specs=(row, col),
        out_shape=(SDS((s, d), BF16), SDS((s, 1), F32)),
        compiler_params=_params(("parallel",)), name=name)(x, g, sc, sh)


def _norm_mod_bwd(name, dh, xin, rstd, g, sc, dres):
    s, d = xin.shape
    tr = _tile(s, 256)
    row, vec, col = _row_specs(s, d, tr)

    def body(dh_ref, x_ref, r_ref, g_ref, sc_ref, dres_ref, dx_ref, dsh_ref, dsc_ref, dg_ref):
        @pl.when(pl.program_id(0) == 0)
        def _():
            dsh_ref[...] = jnp.zeros_like(dsh_ref)
            dsc_ref[...] = jnp.zeros_like(dsc_ref)
            dg_ref[...] = jnp.zeros_like(dg_ref)

        dh_v, xv, r, gv = dh_ref[...], x_ref[...], r_ref[...], g_ref[...]
        n0 = xv * r
        dsh_ref[...] += jnp.sum(dh_v, axis=0, keepdims=True)
        dsc_ref[...] += jnp.sum(dh_v * (n0 * gv), axis=0, keepdims=True)
        dn = dh_v * (1.0 + sc_ref[...])
        dg_ref[...] += jnp.sum(dn * n0, axis=0, keepdims=True)
        gy = dn * gv
        dot = jnp.mean(gy * xv, axis=-1, keepdims=True)
        dx_ref[...] = dres_ref[...] + r * gy - xv * (r * r * r * dot)

    return pl.pallas_call(
        body, grid=(s // tr,), in_specs=[row, row, col, vec, vec, row], out_specs=(row, vec, vec, vec),
        out_shape=(SDS((s, d), F32), SDS((1, d), F32), SDS((1, d), F32), SDS((1, d), F32)),
        compiler_params=_params(("arbitrary",)), name=name)(dh, xin, rstd, g, sc, dres)


def _gate_bwd(name, dx, y, gt):
    s, d = dx.shape
    tr = _tile(s, 256)
    row, vec, _ = _row_specs(s, d, tr)

    def body(dx_ref, y_ref, gt_ref, dy_ref, dgt_ref):
        @pl.when(pl.program_id(0) == 0)
        def _():
            dgt_ref[...] = jnp.zeros_like(dgt_ref)

        dxv = dx_ref[...]
        dy_ref[...] = (gt_ref[...] * dxv).astype(BF16)
        dgt_ref[...] += jnp.sum(dxv * y_ref[...], axis=0, keepdims=True)

    return pl.pallas_call(
        body, grid=(s // tr,), in_specs=[row, row, vec], out_specs=(row, vec),
        out_shape=(SDS((s, d), BF16), SDS((1, d), F32)),
        compiler_params=_params(("arbitrary",)), name=name)(dx, y, gt)


def _final_loss(x2, gf, tgt):
    s, d = x2.shape
    tr = _tile(s, 256)
    row, vec, _ = _row_specs(s, d, tr)
    lrow = pl.BlockSpec((1, LANES), lambda i: (0, 0))

    def body(x_ref, g_ref, t_ref, dx_ref, loss_ref, dg_ref):
        @pl.when(pl.program_id(0) == 0)
        def _():
            loss_ref[...] = jnp.zeros_like(loss_ref)
            dg_ref[...] = jnp.zeros_like(dg_ref)

        xv, gv = x_ref[...], g_ref[...]
        r = lax.rsqrt(jnp.mean(xv * xv, axis=-1, keepdims=True) + EPS)
        n0 = xv * r
        err = n0 * gv - t_ref[...]
        loss_ref[...] += jnp.sum(err * err) * (0.5 / d)
        dy = err * (1.0 / d)
        dg_ref[...] += jnp.sum(dy * n0, axis=0, keepdims=True)
        gy = dy * gv
        dot = jnp.mean(gy * xv, axis=-1, keepdims=True)
        dx_ref[...] = r * gy - xv * (r * r * r * dot)

    return pl.pallas_call(
        body, grid=(s // tr,), in_specs=[row, vec, row], out_specs=(row, lrow, vec),
        out_shape=(SDS((s, d), F32), SDS((1, LANES), F32), SDS((1, d), F32)),
        compiler_params=_params(("arbitrary",)), name="final_loss")(x2, gf, tgt)


def _mix_norm_fwd(oa, ol, ga, gl):
    s, w = oa.shape
    tr = _tile(s, 256)
    row, vec, col = _row_specs(s, w, tr)

    def body(oa_ref, ol_ref, ga_ref, gl_ref, mx_ref, ra_ref, rl_ref):
        a, l = oa_ref[...], ol_ref[...]
        ra = lax.rsqrt(jnp.mean(a * a, axis=-1, keepdims=True) + EPS)
        rl = lax.rsqrt(jnp.mean(l * l, axis=-1, keepdims=True) + EPS)
        mx_ref[:, :w] = (a * ra * ga_ref[...]).astype(BF16)
        mx_ref[:, w:] = (l * rl * gl_ref[...]).astype(BF16)
        ra_ref[...] = ra
        rl_ref[...] = rl

    return pl.pallas_call(
        body, grid=(s // tr,), in_specs=[row, row, vec, vec],
        out_specs=(pl.BlockSpec((tr, 2 * w), lambda i: (i, 0)), col, col),
        out_shape=(SDS((s, 2 * w), BF16), SDS((s, 1), F32), SDS((s, 1), F32)),
        compiler_params=_params(("parallel",)), name="mix_norm_fwd")(oa, ol, ga, gl)


def _mix_norm_bwd(dmx, oa, ol, ra, rl, ga, gl):
    s, w = oa.shape
    tr = _tile(s, 256)
    row, vec, col = _row_specs(s, w, tr)

    def body(dm_ref, oa_ref, ol_ref, ra_ref, rl_ref, ga_ref, gl_ref, doa_ref, dol_ref, dga_ref, dgl_ref):
        @pl.when(pl.program_id(0) == 0)
        def _():
            dga_ref[...] = jnp.zeros_like(dga_ref)
            dgl_ref[...] = jnp.zeros_like(dgl_ref)

        def one(dy, xv, r, gv, dg_ref):
            dg_ref[...] += jnp.sum(dy * (xv * r), axis=0, keepdims=True)
            gy = dy * gv
            dot = jnp.mean(gy * xv, axis=-1, keepdims=True)
            return r * gy - xv * (r * r * r * dot)

        doa_ref[...] = one(dm_ref[:, :w], oa_ref[...], ra_ref[...], ga_ref[...], dga_ref).astype(BF16)
        dol_ref[...] = one(dm_ref[:, w:], ol_ref[...], rl_ref[...], gl_ref[...], dgl_ref)

    return pl.pallas_call(
        body, grid=(s // tr,),
        in_specs=[pl.BlockSpec((tr, 2 * w), lambda i: (i, 0)), row, row, col, col, vec, vec],
        out_specs=(row, row, vec, vec),
        out_shape=(SDS((s, w), BF16), SDS((s, w), F32), SDS((1, w), F32), SDS((1, w), F32)),
        compiler_params=_params(("arbitrary",)), name="mix_norm_bwd")(dmx, oa, ol, ra, rl, ga, gl)


def _attn_block(q, k_blk, off, row_id, tq, tk, tri_after, csum):
    z = _dot_nt(q, k_blk) * (HEAD ** -0.5)
    col_id = off + lax.broadcasted_iota(jnp.int32, (tq, tk), 1)
    causal = col_id < row_id
    lb = _log_sigmoid(z)
    ls = jnp.where(causal, lb - z, 0.0)
    after = _split_dot(ls, tri_after) + csum
    w = jnp.where(causal, jnp.exp(lb + after), 0.0)
    return lb, ls, w, causal


def _attn_tiles(s):
    t = 256 if s >= 1024 else 128
    return t, t


def _attn_fwd(qkv, n_heads):
    s = qkv.shape[0]
    tq, tk = _attn_tiles(s)

    def body(q_ref, k_ref, v_ref, o_ref):
        qi = pl.program_id(1)
        q = q_ref[...]
        row_id = qi * tq + lax.broadcasted_iota(jnp.int32, (tq, tk), 0)
        r_i = lax.broadcasted_iota(jnp.int32, (tk, tk), 0)
        c_i = lax.broadcasted_iota(jnp.int32, (tk, tk), 1)
        tri_after = (r_i > c_i).astype(BF16)

        def step(it, carry):
            csum, o = carry
            off = pl.multiple_of((qi - it) * tk, tk)
            _, ls, w, _ = _attn_block(q, k_ref[pl.ds(off, tk), :], off, row_id, tq, tk, tri_after, csum)
            o = o + _dot(w.astype(BF16), v_ref[pl.ds(off, tk), :])
            return csum + jnp.sum(ls, axis=1, keepdims=True), o

        _, o = lax.fori_loop(0, qi + 1, step, (jnp.zeros((tq, 1), F32), jnp.zeros((tq, HEAD), F32)))
        o_ref[...] = o

    h = n_heads
    return pl.pallas_call(
        body, grid=(h, s // tq),
        in_specs=[pl.BlockSpec((tq, HEAD), lambda hh, i: (i, hh)),
                  pl.BlockSpec((s, HEAD), lambda hh, i: (0, h + hh)),
                  pl.BlockSpec((s, HEAD), lambda hh, i: (0, 2 * h + hh))],
        out_specs=pl.BlockSpec((tq, HEAD), lambda hh, i: (i, hh)),
        out_shape=SDS((s, h * HEAD), F32),
        compiler_params=_params(("parallel", "parallel")), name="attn_fwd")(qkv, qkv, qkv)


def _attn_bwd(qkv, do, n_heads):
    s = qkv.shape[0]
    tq, tk = _attn_tiles(s)
    nq = s // tq
    scale = HEAD ** -0.5

    def body(q_ref, k_ref, v_ref, do_ref, dq_ref, dk_ref, dv_ref, e_s, sg_s, dk_acc, dv_acc):
        qi = pl.program_id(1)

        @pl.when(qi == 0)
        def _():
            dk_acc[...] = jnp.zeros_like(dk_acc)
            dv_acc[...] = jnp.zeros_like(dv_acc)

        q = q_ref[...]
        dout = do_ref[...]
        row_id = qi * tq + lax.broadcasted_iota(jnp.int32, (tq, tk), 0)
        r_i = lax.broadcasted_iota(jnp.int32, (tk, tk), 0)
        c_i = lax.broadcasted_iota(jnp.int32, (tk, tk), 1)
        tri_after = (r_i > c_i).astype(BF16)
        tri_before = (r_i < c_i).astype(BF16)

        def pass1(it, csum):
            kb = qi - it
            off = pl.multiple_of(kb * tk, tk)
            lb, ls, w, _ = _attn_block(q, k_ref[pl.ds(off, tk), :], off, row_id, tq, tk, tri_after, csum)
            dw = _dot_nt(dout, v_ref[pl.ds(off, tk), :])
            e_s[kb] = dw * w
            sg_s[kb] = jnp.exp(lb)
            dv_acc[pl.ds(off, tk), :] += _dot_tn(w.astype(BF16), dout)
            return csum + jnp.sum(ls, axis=1, keepdims=True)

        lax.fori_loop(0, qi + 1, pass1, jnp.zeros((tq, 1), F32))

        def pass2(kb, carry):
            esum, dq = carry
            off = pl.multiple_of(kb * tk, tk)
            e, sg = e_s[kb], sg_s[kb]
            col_id = off + lax.broadcasted_iota(jnp.int32, (tq, tk), 1)
            before = _split_dot(e, tri_before) + esum
            dz = (e * (1.0 - sg) - jnp.where(col_id < row_id, before * sg, 0.0)) * scale
            dzb = dz.astype(BF16)
            dq = dq + _dot(dzb, k_ref[pl.ds(off, tk), :])
            dk_acc[pl.ds(off, tk), :] += _dot_tn(dzb, q)
            return esum + jnp.sum(e, axis=1, keepdims=True), dq

        _, dq = lax.fori_loop(0, qi + 1, pass2, (jnp.zeros((tq, 1), F32), jnp.zeros((tq, HEAD), F32)))
        dq_ref[...] = dq.astype(BF16)

        @pl.when(qi == nq - 1)
        def _():
            dk_ref[...] = dk_acc[...].astype(BF16)
            dv_ref[...] = dv_acc[...].astype(BF16)

    h = n_heads
    blk = pl.BlockSpec((tq, HEAD), lambda hh, i: (i, hh))
    full = pl.BlockSpec((s, HEAD), lambda hh, i: (0, hh))
    return pl.pallas_call(
        body, grid=(h, nq),
        in_specs=[blk,
                  pl.BlockSpec((s, HEAD), lambda hh, i: (0, h + hh)),
                  pl.BlockSpec((s, HEAD), lambda hh, i: (0, 2 * h + hh)),
                  blk],
        out_specs=(blk, full, full),
        out_shape=(SDS((s, h * HEAD), BF16),) * 3,
        scratch_shapes=[pltpu.VMEM((s // tk, tq, tk), F32), pltpu.VMEM((s // tk, tq, tk), F32),
                        pltpu.VMEM((s, HEAD), F32), pltpu.VMEM((s, HEAD), F32)],
        compiler_params=_params(("parallel", "arbitrary")), name="attn_bwd")(qkv, qkv, qkv, do)


def _lru_chunk(s):
    return 256 if s >= 1024 else 128


def _lru_gates(xc, wa, ba, wx, bx, sp):
    xb = xc.astype(BF16)
    r = _sigmoid(_dot(xb, wa) + ba)
    ig = _sigmoid(_dot(xb, wx) + bx)
    la = -LRU_C * r * sp
    a = jnp.exp(la)
    t = jnp.tanh(la)
    mult = jnp.sqrt(-2.0 * t / (1.0 - t))
    return r, ig, a, mult


def _softplus_neg(lam):
    return jnp.maximum(-lam, 0.0) + jnp.log(1.0 + jnp.exp(-jnp.abs(lam)))


def _lru_specs(s, n_blocks):
    seq0 = pl.BlockSpec((s, HEAD), lambda h: (0, h))
    seq1 = pl.BlockSpec((s, HEAD), lambda h: (0, n_blocks + h))
    taps = pl.BlockSpec((CONV_TAPS, HEAD), lambda h: (0, h))
    vec = pl.BlockSpec((1, HEAD), lambda h: (0, h))
    mat = pl.BlockSpec((None, HEAD, HEAD), lambda h: (h, 0, 0))
    return seq0, seq1, taps, vec, mat


def _lru_fwd(xrg, wconv, bconv, wa, ba, wx, bx, lam):
    s = xrg.shape[0]
    nb = wa.shape[0]
    tc = _lru_chunk(s)
    seq0, seq1, taps, vec, mat = _lru_specs(s, nb)
    pad = SUBLANES

    def body(xr_ref, xg_ref, wc_ref, bc_ref, wa_ref, ba_ref, wx_ref, bx_ref, lam_ref, o_ref, h_ref, pad_s, a_s, u_s):
        pad_s[0:pad, :] = jnp.zeros((pad, HEAD), F32)
        pad_s[pad:pad + s, :] = xr_ref[...]
        wab, wxb = wa_ref[...].astype(BF16), wx_ref[...].astype(BF16)
        sp = _softplus_neg(lam_ref[...])
        for c in range(s // tc):
            base = c * tc
            xc = bc_ref[...] + sum(wc_ref[i:i + 1, :] * pad_s[pl.ds(base + pad - (CONV_TAPS - 1) + i, tc), :]
                                   for i in range(CONV_TAPS))
            _, ig, a, mult = _lru_gates(xc, wab, ba_ref[...], wxb, bx_ref[...], sp)
            a_s[base:base + tc, :] = a
            u_s[base:base + tc, :] = mult * (ig * xc)

        row = lax.broadcasted_iota(jnp.int32, (SUBLANES, HEAD), 0)

        def chunk(ci, hprev):
            off = pl.multiple_of(ci * SUBLANES, SUBLANES)
            a8, b8 = a_s[pl.ds(off, SUBLANES), :], u_s[pl.ds(off, SUBLANES), :]
            for d in (1, 2, 4):
                a_sh = jnp.where(row < d, 1.0, pltpu.roll(a8, d, 0))
                b_sh = jnp.where(row < d, 0.0, pltpu.roll(b8, d, 0))
                b8 = a8 * b_sh + b8
                a8 = a8 * a_sh
            h8 = a8 * hprev + b8
            h_ref[pl.ds(off, SUBLANES), :] = h8
            return h8[SUBLANES - 1:SUBLANES, :]

        lax.fori_loop(0, s // SUBLANES, chunk, jnp.zeros((1, HEAD), F32), unroll=8)
        for c in range(s // tc):
            sl = slice(c * tc, (c + 1) * tc)
            gel, _ = _gelu_parts(xg_ref[sl, :])
            o_ref[sl, :] = h_ref[sl, :] * gel

    return pl.pallas_call(
        body, grid=(nb,),
        in_specs=[seq0, seq1, taps, vec, mat, vec, mat, vec, vec],
        out_specs=(seq0, seq0),
        out_shape=(SDS((s, nb * HEAD), F32), SDS((s, nb * HEAD), F32)),
        scratch_shapes=[pltpu.VMEM((s + pad, HEAD), F32), pltpu.VMEM((s, HEAD), F32), pltpu.VMEM((s, HEAD), F32)],
        compiler_params=_params(("parallel",)), name="lru_fwd")(xrg, xrg, wconv, bconv, wa, ba, wx, bx, lam)


def _lru_bwd(xrg, dol, hseq, wconv, bconv, wa, ba, wx, bx, lam):
    s = xrg.shape[0]
    nb = wa.shape[0]
    tc = _lru_chunk(s)
    seq0, seq1, taps, vec, mat = _lru_specs(s, nb)
    pad = SUBLANES

    def body(xr_ref, xg_ref, do_ref, h_ref, wc_ref, bc_ref, wa_ref, ba_ref, wx_ref, bx_ref, lam_ref,
             dxr_ref, dxg_ref, dwc_ref, dbc_ref, dwa_ref, dba_ref, dwx_ref, dbx_ref, dlam_ref,
             pad_s, hp_s, a_s, g_s, da_s, dxc_s):
        pad_s[0:pad, :] = jnp.zeros((pad, HEAD), F32)
        pad_s[pad:pad + s, :] = xr_ref[...]
        hp_s[0:pad, :] = jnp.zeros((pad, HEAD), F32)
        hp_s[pad:pad + s, :] = h_ref[...]
        a_s[s:s + pad, :] = jnp.zeros((pad, HEAD), F32)
        dxc_s[s:s + pad, :] = jnp.zeros((pad, HEAD), F32)
        wab, wxb = wa_ref[...].astype(BF16), wx_ref[...].astype(BF16)
        lam_v = lam_ref[...]
        sp = _softplus_neg(lam_v)

        def conv_in(c):
            base = c * tc
            wins = [pad_s[pl.ds(base + pad - (CONV_TAPS - 1) + i, tc), :] for i in range(CONV_TAPS)]
            xc = bc_ref[...] + sum(wc_ref[i:i + 1, :] * wins[i] for i in range(CONV_TAPS))
            return xc, wins

        for c in range(s // tc):
            sl = slice(c * tc, (c + 1) * tc)
            xc, _ = conv_in(c)
            _, _, a, _ = _lru_gates(xc, wab, ba_ref[...], wxb, bx_ref[...], sp)
            a_s[sl, :] = a
            gel, dgel = _gelu_parts(xg_ref[sl, :])
            dov = do_ref[sl, :]
            g_s[sl, :] = dov * gel
            dxg_ref[sl, :] = (dov * h_ref[sl, :] * dgel).astype(BF16)

        row = lax.broadcasted_iota(jnp.int32, (SUBLANES, HEAD), 0)
        n_chunks = s // SUBLANES

        def chunk(it, gnext):
            ci = n_chunks - 1 - it
            off = pl.multiple_of(ci * SUBLANES, SUBLANES)
            a8 = a_s[pl.ds(off, SUBLANES), :]
            a8n = a_s[pl.ds(off + SUBLANES, SUBLANES), :]
            c8 = pltpu.roll(jnp.where(row == 0, a8n, a8), SUBLANES - 1, 0)
            g8 = g_s[pl.ds(off, SUBLANES), :]
            for d in (1, 2, 4):
                c_sh = jnp.where(row >= SUBLANES - d, 1.0, pltpu.roll(c8, SUBLANES - d, 0))
                g_sh = jnp.where(row >= SUBLANES - d, 0.0, pltpu.roll(g8, SUBLANES - d, 0))
                g8 = c8 * g_sh + g8
                c8 = c8 * c_sh
            g8 = g8 + c8 * gnext
            g_s[pl.ds(off, SUBLANES), :] = g8
            h8 = hp_s[pl.ds(off + pad, SUBLANES), :]
            h8p = hp_s[pl.ds(off, SUBLANES), :]
            da_s[pl.ds(off, SUBLANES), :] = g8 * pltpu.roll(jnp.where(row == SUBLANES - 1, h8p, h8), 1, 0)
            return g8[0:1, :]

        lax.fori_loop(0, n_chunks, chunk, jnp.zeros((1, HEAD), F32), unroll=8)

        dsp = jnp.zeros((1, HEAD), F32)
        dbc = jnp.zeros((1, HEAD), F32)
        dba = jnp.zeros((1, HEAD), F32)
        dbx = jnp.zeros((1, HEAD), F32)
        dwa = jnp.zeros((HEAD, HEAD), F32)
        dwx = jnp.zeros((HEAD, HEAD), F32)
        dwc = [jnp.zeros((1, HEAD), F32) for _ in range(CONV_TAPS)]
        for c in range(s // tc):
            sl = slice(c * tc, (c + 1) * tc)
            xc, wins = conv_in(c)
            r, ig, a, mult = _lru_gates(xc, wab, ba_ref[...], wxb, bx_ref[...], sp)
            du, da = g_s[sl, :], da_s[sl, :]
            d_ix = du * mult
            dla = da * a - (du * ig * xc) * (a * a / mult)
            dsp = dsp + jnp.sum(dla * r, axis=0, keepdims=True) * (-LRU_C)
            dpa = (dla * (-LRU_C * sp)) * r * (1.0 - r)
            dpx = (d_ix * xc) * ig * (1.0 - ig)
            dpab, dpxb, xb = dpa.astype(BF16), dpx.astype(BF16), xc.astype(BF16)
            dxc = d_ix * ig + _dot_nt(dpab, wab) + _dot_nt(dpxb, wxb)
            dwa = dwa + _dot_tn(xb, dpab)
            dwx = dwx + _dot_tn(xb, dpxb)
            dba = dba + jnp.sum(dpa, axis=0, keepdims=True)
            dbx = dbx + jnp.sum(dpx, axis=0, keepdims=True)
            dbc = dbc + jnp.sum(dxc, axis=0, keepdims=True)
            for i in range(CONV_TAPS):
                dwc[i] = dwc[i] + jnp.sum(dxc * wins[i], axis=0, keepdims=True)
            dxc_s[sl, :] = dxc

        for c in range(s // tc):
            base = c * tc
            dxr = sum(wc_ref[i:i + 1, :] * dxc_s[pl.ds(base + (CONV_TAPS - 1) - i, tc), :] for i in range(CONV_TAPS))
            dxr_ref[base:base + tc, :] = dxr.astype(BF16)

        for i in range(CONV_TAPS):
            dwc_ref[i:i + 1, :] = dwc[i]
        dbc_ref[...] = dbc
        dwa_ref[...] = dwa
        dwx_ref[...] = dwx
        dba_ref[...] = dba
        dbx_ref[...] = dbx
        dlam_ref[...] = dsp * (-_sigmoid(-lam_v))

    w = nb * HEAD
    return pl.pallas_call(
        body, grid=(nb,),
        in_specs=[seq0, seq1, seq0, seq0, taps, vec, mat, vec, mat, vec, vec],
        out_specs=(seq0, seq0, taps, vec, mat, vec, mat, vec, vec),
        out_shape=(SDS((s, w), BF16), SDS((s, w), BF16), SDS((CONV_TAPS, w), F32), SDS((1, w), F32),
                   SDS((nb, HEAD, HEAD), F32), SDS((1, w), F32), SDS((nb, HEAD, HEAD), F32), SDS((1, w), F32),
                   SDS((1, w), F32)),
        scratch_shapes=[pltpu.VMEM((s + pad, HEAD), F32), pltpu.VMEM((s + pad, HEAD), F32),
                        pltpu.VMEM((s + pad, HEAD), F32), pltpu.VMEM((s, HEAD), F32),
                        pltpu.VMEM((s, HEAD), F32), pltpu.VMEM((s + pad, HEAD), F32)],
        compiler_params=_params(("parallel",)), name="lru_bwd",
    )(xrg, xrg, dol, hseq, wconv, bconv, wa, ba, wx, bx, lam)


def _ada_mod(c_all, w_sh, b_sh):
    n_ex, d = c_all.shape
    n = w_sh.shape[1]
    tn = _tile(n, 512)

    def body(c_ref, w_ref, b_ref, mod_ref, act_ref):
        cv = c_ref[...]
        act = cv * _sigmoid(cv)
        act_ref[...] = act
        mod_ref[...] = _dot(act.astype(BF16), w_ref[...].astype(BF16)) + b_ref[...]

    return pl.pallas_call(
        body, grid=(n // tn,),
        in_specs=[pl.BlockSpec((n_ex, d), lambda j: (0, 0)), pl.BlockSpec((d, tn), lambda j: (0, j)),
                  pl.BlockSpec((1, tn), lambda j: (0, j))],
        out_specs=(pl.BlockSpec((n_ex, tn), lambda j: (0, j)), pl.BlockSpec((n_ex, d), lambda j: (0, 0))),
        out_shape=(SDS((n_ex, n), F32), SDS((n_ex, d), F32)),
        compiler_params=_params(("arbitrary",)), name="ada_mod")(c_all, w_sh, b_sh)


def _adamw_math(w, g, m, v):
    m = ADAM_B1 * m + (1.0 - ADAM_B1) * g
    v = ADAM_B2 * v + (1.0 - ADAM_B2) * (g * g)
    m_hat = m / (1.0 - ADAM_B1 ** ADAM_STEP)
    v_hat = v / (1.0 - ADAM_B2 ** ADAM_STEP)
    delta = -ADAM_LR * (m_hat / (jnp.sqrt(v_hat) + ADAM_EPS) + ADAM_WD * w)
    return delta, m, v


def _adamw_plain(name, w, g, m, v):
    def body(w_ref, g_ref, m_ref, v_ref, d_ref, mo_ref, vo_ref):
        d_ref[...], mo_ref[...], vo_ref[...] = _adamw_math(w_ref[...], g_ref[...], m_ref[...], v_ref[...])

    return pl.pallas_call(body, out_shape=(SDS(w.shape, F32),) * 3, name=name)(w, g, m, v)


def _adamw_halves(name, c_arr, w, m, v, g_own, g_recv):
    r, n = w.shape
    rh = r // 2
    tr = _tile(rh, 256)
    nh = rh // tr

    def body(c_ref, w_ref, m_ref, v_ref, go_ref, gr_ref, g_ref, d_ref, mo_ref, vo_ref):
        own = (pl.program_id(0) // nh) == c_ref[0]
        g = jnp.where(own, go_ref[...], gr_ref[...])
        g_ref[...] = g
        d_ref[...], mo_ref[...], vo_ref[...] = _adamw_math(w_ref[...], g, m_ref[...], v_ref[...])

    full = pl.BlockSpec((tr, n), lambda i, c: (i, 0))
    half = pl.BlockSpec((tr, n), lambda i, c: (i % nh, 0))
    return pl.pallas_call(
        body,
        grid_spec=pltpu.PrefetchScalarGridSpec(
            num_scalar_prefetch=1, grid=(2 * nh,), in_specs=[full, full, full, half, half],
            out_specs=(full,) * 4),
        out_shape=(SDS((r, n), F32),) * 4,
        compiler_params=_params(("parallel",)), name=name)(c_arr, w, m, v, g_own, g_recv)


def _adamw_ada(w, m, v, act_t, dmod):
    d, n = w.shape
    n_ex = act_t.shape[1]
    tr = _tile(d, 256)

    def body(a_ref, dm_ref, w_ref, m_ref, v_ref, g_ref, d_ref, mo_ref, vo_ref):
        g = _dot(a_ref[...], dm_ref[...])
        g_ref[...] = g
        d_ref[...], mo_ref[...], vo_ref[...] = _adamw_math(w_ref[...], g, m_ref[...], v_ref[...])

    full = pl.BlockSpec((tr, n), lambda i: (i, 0))
    return pl.pallas_call(
        body, grid=(d // tr,),
        in_specs=[pl.BlockSpec((tr, n_ex), lambda i: (i, 0)), pl.BlockSpec((n_ex, n), lambda i: (0, 0)), full, full, full],
        out_specs=(full,) * 4, out_shape=(SDS((d, n), F32),) * 4,
        compiler_params=_params(("parallel",)), name="adamw_ada")(act_t, dmod, w, m, v)


def _small_reduce_adamw(parts, w, m, v):
    n_dev, r, _ = parts.shape
    tr = r if r <= PACK_ROWS else PACK_ROWS

    def body(p_ref, w_ref, m_ref, v_ref, g_ref, d_ref, mo_ref, vo_ref):
        g = p_ref[0]
        for k in range(1, n_dev):
            g = g + p_ref[k]
        g_ref[...] = g
        d_ref[...], mo_ref[...], vo_ref[...] = _adamw_math(w_ref[...], g, m_ref[...], v_ref[...])

    full = pl.BlockSpec((tr, LANES), lambda i: (i, 0))
    return pl.pallas_call(
        body, grid=(r // tr,),
        in_specs=[pl.BlockSpec((n_dev, tr, LANES), lambda i: (0, i, 0)), full, full, full],
        out_specs=(full,) * 4, out_shape=(SDS((r, LANES), F32),) * 4,
        compiler_params=_params(("parallel",)), name="small_reduce_adamw")(parts, w, m, v)


def _mesh_pos():
    return lax.axis_index("x"), lax.axis_index("y"), lax.axis_index("c")


def _other_chips(x, y):
    return [(1 - x, y), (x, 1 - y), (1 - x, 1 - y)]


def _all_gather_small(name, blk):
    r, n = blk.shape

    def body(x_ref, out_ref, send_sems, recv_sems, local_sem):
        x, y, c = _mesh_pos()
        me, sibling = (x, y, c), (x, y, 1 - c)
        chips = _other_chips(x, y)

        def rows(px, py, pc):
            return out_ref.at[4 * px + 2 * py + pc]

        def copy(k, block, to, src=None):
            return pltpu.make_async_remote_copy(
                src_ref=rows(*block) if src is None else src, dst_ref=rows(*block),
                send_sem=send_sems.at[k], recv_sem=recv_sems.at[k], device_id=to, device_id_type=MESH)

        mine = pltpu.make_async_copy(x_ref, rows(*me), local_sem)
        mine.start()
        first = [copy(0, me, sibling, src=x_ref)]
        first += [copy(1 + j, me, (*chip, c), src=x_ref) for j, chip in enumerate(chips)]
        for cp in first:
            cp.start()
        passed = [copy(4 + j, (*chip, c), sibling) for j, chip in enumerate(chips)]
        for j, chip in enumerate(chips):
            copy(1 + j, (*chip, c), me).wait_recv()
            passed[j].start()
        copy(0, sibling, me).wait_recv()
        for j, chip in enumerate(chips):
            copy(4 + j, (*chip, 1 - c), me).wait_recv()
        for cp in first + passed:
            cp.wait_send()
        mine.wait()

    return pl.pallas_call(
        body, out_shape=SDS((N_DEV, r, n), blk.dtype),
        in_specs=[pl.BlockSpec(memory_space=pltpu.VMEM)], out_specs=pl.BlockSpec(memory_space=pltpu.VMEM),
        scratch_shapes=[pltpu.SemaphoreType.DMA((7,)), pltpu.SemaphoreType.DMA((7,)), pltpu.SemaphoreType.DMA],
        compiler_params=pltpu.CompilerParams(vmem_limit_bytes=VMEM_LIMIT), name=name)(blk)


_ANY = pl.BlockSpec(memory_space=pl.ANY)
_HBM = pl.BlockSpec(memory_space=pltpu.HBM)
_SEM = pl.BlockSpec(memory_space=pltpu.SEMAPHORE)
_EFFECT = pltpu.SideEffectType.DATAFLOW_SIDE_EFFECTING


def _hbm(a):
    return pltpu.with_memory_space_constraint(a, pltpu.HBM)


def _place_cast(name, j_arr, shard, kind):
    r, n = shard.shape
    tr = _tile(r, 256)
    nr = r // tr
    if kind == "col":
        out_shape, o_spec = (r, N_CHIP * n), pl.BlockSpec((tr, n), lambda i, j: (i, j[0]))
    else:
        out_shape, o_spec = (N_CHIP * r, n), pl.BlockSpec((tr, n), lambda i, j: (j[0] * nr + i, 0))

    def body(j_ref, s_ref, o_ref):
        o_ref[...] = s_ref[...].astype(BF16)

    return pl.pallas_call(
        body,
        grid_spec=pltpu.PrefetchScalarGridSpec(
            num_scalar_prefetch=1, grid=(nr,), in_specs=[pl.BlockSpec((tr, n), lambda i, j: (i, 0))], out_specs=o_spec),
        out_shape=SDS(out_shape, BF16), compiler_params=_params(("parallel",)), name=name)(j_arr, shard)


def _gather_start(fulls, kinds, groups, after):
    nw = len(fulls)
    ng = len(groups)

    def body(*refs):
        outs = refs[nw + 1:]
        send, recv, full, token = outs[:ng], outs[ng:2 * ng], outs[2 * ng:2 * ng + nw], outs[2 * ng + nw]
        x, y, c = _mesh_pos()
        for g, ws in enumerate(groups):
            for li, w in enumerate(ws):
                mine = _full_region(full[w], kinds[w], x, y, c)
                for k, chip in enumerate(_other_chips(x, y)):
                    pltpu.make_async_remote_copy(
                        src_ref=mine, dst_ref=mine, send_sem=send[g].at[3 * li + k], recv_sem=recv[g].at[3 * li + k],
                        device_id=(*chip, c), device_id_type=MESH).start()
        token[...] = jnp.zeros_like(token)

    sems = tuple(pltpu.SemaphoreType.DMA((3 * len(ws),)) for ws in groups)
    outs = pl.pallas_call(
        body,
        out_shape=sems + sems + tuple(pltpu.HBM(f_.shape, f_.dtype) for f_ in fulls) + (SDS((SUBLANES, LANES), F32),),
        in_specs=[_HBM] * nw + [_ANY],
        out_specs=tuple([_SEM] * (2 * ng) + [_HBM] * nw + [pl.BlockSpec(memory_space=pltpu.VMEM)]),
        input_output_aliases={w: 2 * ng + w for w in range(nw)},
        compiler_params=pltpu.CompilerParams(has_side_effects=_EFFECT),
        name="gather_start",
    )(*[_hbm(f_) for f_ in fulls], after)
    return outs[:ng], outs[ng:2 * ng], outs[2 * ng:2 * ng + nw], outs[2 * ng + nw]


def _full_region(full, kind, px, py, half):
    j = 2 * px + py
    if kind == "col":
        rh, cols = full.shape[0] // 2, full.shape[1] // N_CHIP
        return full.at[pl.ds(half * rh, rh), pl.ds(j * cols, cols)]
    rows = full.shape[0] // N_CHIP
    rh = rows // 2
    return full.at[pl.ds(j * rows + half * rh, rh), :]


def _gather_pass(name, fulls, kinds, send, recv, after, thru):
    nw = len(fulls)

    def body(*refs):
        send_r, recv_r = refs[nw], refs[nw + 1]
        full, fsend, frecv = refs[nw + 4:2 * nw + 4], refs[2 * nw + 4], refs[2 * nw + 5]
        x, y, c = _mesh_pos()
        chips = _other_chips(x, y)
        for w in range(nw):
            for k, chip in enumerate(chips):
                landed = _full_region(full[w], kinds[w], *chip, c)
                arrive = pltpu.make_async_remote_copy(
                    src_ref=_full_region(full[w], kinds[w], x, y, c), dst_ref=landed, send_sem=send_r.at[3 * w + k],
                    recv_sem=recv_r.at[3 * w + k], device_id=(*chip, c), device_id_type=MESH)
                arrive.wait_recv()
                arrive.wait_send()
                pltpu.make_async_remote_copy(
                    src_ref=landed, dst_ref=landed, send_sem=fsend.at[3 * w + k], recv_sem=frecv.at[3 * w + k],
                    device_id=(x, y, 1 - c), device_id_type=MESH).start()

    sem = pltpu.SemaphoreType.DMA((3 * nw,))
    outs = pl.pallas_call(
        body,
        out_shape=tuple(pltpu.HBM(f_.shape, f_.dtype) for f_ in fulls) + (sem, sem, SDS(thru.shape, thru.dtype)),
        in_specs=[_HBM] * nw + [_SEM, _SEM, _ANY, _ANY],
        out_specs=tuple([_HBM] * nw + [_SEM, _SEM, _ANY]),
        input_output_aliases={**{w: w for w in range(nw)}, nw + 3: nw + 2},
        compiler_params=pltpu.CompilerParams(has_side_effects=_EFFECT),
        name=name,
    )(*fulls, send, recv, after, thru)
    return outs[:nw], outs[nw], outs[nw + 1], outs[nw + 2]


def _gather_finish(name, fulls, kinds, fsend, frecv, after):
    nw = len(fulls)

    def body(*refs):
        fsend_r, frecv_r = refs[nw], refs[nw + 1]
        full = refs[nw + 3:2 * nw + 3]
        x, y, c = _mesh_pos()
        chips = _other_chips(x, y)
        for w in range(nw):
            for k, chip in enumerate(chips):
                cp = pltpu.make_async_remote_copy(
                    src_ref=_full_region(full[w], kinds[w], *chip, c), dst_ref=_full_region(full[w], kinds[w], *chip, 1 - c),
                    send_sem=fsend_r.at[3 * w + k], recv_sem=frecv_r.at[3 * w + k],
                    device_id=(x, y, 1 - c), device_id_type=MESH)
                cp.wait_send()
                cp.wait_recv()

    outs = pl.pallas_call(
        body,
        out_shape=tuple(pltpu.HBM(f_.shape, f_.dtype) for f_ in fulls),
        in_specs=[_HBM] * nw + [_SEM, _SEM, _ANY],
        out_specs=tuple([_HBM] * nw),
        input_output_aliases={w: w for w in range(nw)},
        compiler_params=pltpu.CompilerParams(has_side_effects=_EFFECT),
        name=name,
    )(*fulls, fsend, frecv, after)
    return list(outs)


def _half_of(ref, kind, half):
    if kind == "col":
        rh = ref.shape[0] // 2
        return ref.at[pl.ds(half * rh, rh), :]
    rh = ref.shape[1] // 2
    return ref.at[:, pl.ds(half * rh, rh), :]


def _half_shape(g, kind):
    if kind == "col":
        return (g.shape[0] // 2, g.shape[1])
    return (g.shape[0], g.shape[1] // 2, g.shape[2])


def _plan_swap_half(kind):
    def plan(src, land, x, y, c):
        return [(_half_of(src, kind, 1 - c), land, (x, y, 1 - c))]
    return plan


def _plan_scatter(kind):
    def plan(src, land, x, y, c):
        out = []
        for k, (px, py) in enumerate(_other_chips(x, y)):
            j = 2 * px + py
            if kind == "col":
                n = src.shape[1] // N_CHIP
                blk = src.at[:, pl.ds(j * n, n)]
            else:
                blk = src.at[j]
            out.append((blk, land.at[k], (px, py, c)))
        return out
    return plan


def _plan_whole(src, land, x, y, c):
    return [(src, land, (x, y, 1 - c))]


def _split_start(name, src, land_shape, n, plan, thru):
    def body(src_in, land_in, thru_in, send, recv, src_ref, land_ref, thru_out):
        x, y, c = _mesh_pos()
        for k, (s_, d_, dev) in enumerate(plan(src_ref, land_ref, x, y, c)):
            pltpu.make_async_remote_copy(src_ref=s_, dst_ref=d_, send_sem=send.at[k], recv_sem=recv.at[k],
                                         device_id=dev, device_id_type=MESH).start()

    sem = pltpu.SemaphoreType.DMA((n,))
    return pl.pallas_call(
        body,
        out_shape=(sem, sem, pltpu.HBM(src.shape, src.dtype), pltpu.HBM(land_shape, src.dtype), SDS(thru.shape, thru.dtype)),
        in_specs=[_HBM, _HBM, _ANY], out_specs=(_SEM, _SEM, _HBM, _HBM, _ANY),
        input_output_aliases={0: 2, 1: 3, 2: 4},
        compiler_params=pltpu.CompilerParams(has_side_effects=_EFFECT), name=name,
    )(_hbm(src), _hbm(lax.empty(land_shape, src.dtype)), thru)


def _split_wait(name, send, recv, src, land, plan, after):
    def body(src_in, land_in, send_r, recv_r, after_r, src_ref, land_ref):
        x, y, c = _mesh_pos()
        for k, (s_, d_, dev) in enumerate(plan(src_ref, land_ref, x, y, c)):
            cp = pltpu.make_async_remote_copy(src_ref=s_, dst_ref=d_, send_sem=send_r.at[k], recv_sem=recv_r.at[k],
                                              device_id=dev, device_id_type=MESH)
            cp.wait_send()
            cp.wait_recv()

    return pl.pallas_call(
        body,
        out_shape=(pltpu.HBM(src.shape, src.dtype), pltpu.HBM(land.shape, land.dtype)),
        in_specs=[_HBM, _HBM, _SEM, _SEM, _ANY], out_specs=(_HBM, _HBM),
        input_output_aliases={0: 0, 1: 1},
        compiler_params=pltpu.CompilerParams(has_side_effects=_EFFECT), name=name,
    )(src, land, send, recv, after)


def _add_halves(name, c_arr, g, got, kind):
    if kind == "col":
        rh, n = got.shape
        tr = _tile(rh, 256)
        nh = rh // tr
        g_spec = pl.BlockSpec((tr, n), lambda i, c: (c[0] * nh + i, 0))
        o_spec = pl.BlockSpec((tr, n), lambda i, c: (i, 0))
        grid = (nh,)
    else:
        nc, rh, n = got.shape
        tr = _tile(rh, 256)
        nh = rh // tr
        g_spec = pl.BlockSpec((None, tr, n), lambda j, i, c: (j, c[0] * nh + i, 0))
        o_spec = pl.BlockSpec((None, tr, n), lambda j, i, c: (j, i, 0))
        grid = (nc, nh)

    def body(c_ref, g_ref, r_ref, o_ref):
        o_ref[...] = (g_ref[...].astype(F32) + r_ref[...].astype(F32)).astype(o_ref.dtype)

    return pl.pallas_call(
        body,
        grid_spec=pltpu.PrefetchScalarGridSpec(num_scalar_prefetch=1, grid=grid, in_specs=[g_spec, o_spec], out_specs=o_spec),
        out_shape=SDS(got.shape, got.dtype),
        compiler_params=_params(("parallel",) * len(grid)), name=name)(c_arr, g, got)


def _sum_partials(name, j_arr, part, got, kind):
    _, rh, n = got.shape
    tr = _tile(rh, 256)
    if kind == "col":
        p_spec = pl.BlockSpec((tr, n), lambda i, j: (i, j[0]))
    else:
        p_spec = pl.BlockSpec((None, tr, n), lambda i, j: (j[0], i, 0))

    def body(j_ref, p_ref, r_ref, o_ref):
        o_ref[...] = ((p_ref[...].astype(F32) + r_ref[0].astype(F32)) + r_ref[1].astype(F32)) + r_ref[2].astype(F32)

    return pl.pallas_call(
        body,
        grid_spec=pltpu.PrefetchScalarGridSpec(
            num_scalar_prefetch=1, grid=(rh // tr,),
            in_specs=[p_spec, pl.BlockSpec((3, tr, n), lambda i, j: (0, i, 0))],
            out_specs=pl.BlockSpec((tr, n), lambda i, j: (i, 0))),
        out_shape=SDS((rh, n), F32),
        compiler_params=_params(("parallel",)), name=name)(j_arr, part, got)


def _swap_reduced(name, halves):
    nw = len(halves)

    def body(*refs):
        h, got = refs[:nw], refs[nw:2 * nw]
        send_sems, recv_sems = refs[2 * nw:]
        x, y, c = _mesh_pos()
        cps = []
        for w in range(nw):
            cp = pltpu.make_async_remote_copy(
                src_ref=h[w], dst_ref=got[w], send_sem=send_sems.at[w], recv_sem=recv_sems.at[w],
                device_id=(x, y, 1 - c), device_id_type=MESH)
            cp.start()
            cps.append(cp)
        for cp in cps:
            cp.wait()

    return pl.pallas_call(
        body, out_shape=tuple(SDS(h.shape, h.dtype) for h in halves),
        in_specs=[_ANY] * nw, out_specs=tuple([_ANY] * nw),
        scratch_shapes=[pltpu.SemaphoreType.DMA((nw,)), pltpu.SemaphoreType.DMA((nw,))],
        name=name)(*halves)


def _pack(arrays):
    flat = [a.reshape(-1).astype(F32) for a in arrays]
    flat = [jnp.pad(f, (0, (-f.shape[0]) % LANES)) for f in flat]
    sizes = [f.shape[0] for f in flat]
    total = sum(sizes)
    rows = total // LANES
    tail = LANES * ((-rows) % (PACK_ROWS if rows > PACK_ROWS else SUBLANES))
    if tail:
        flat.append(jnp.zeros((tail,), F32))
    return jnp.concatenate(flat).reshape(-1, LANES), sizes


def _unpack(slab, sizes, shapes, lead=()):
    flat = slab.reshape(lead + (-1,))
    out, off = [], 0
    for sz, shp in zip(sizes, shapes):
        n = math.prod(shp)
        out.append(flat[..., off:off + n].reshape(lead + tuple(shp)))
        off += sz
    return out


def kernel(x, c, w_ada, b_ada, g_norm_mix, w_in, w_conv, b_conv, w_rg_a, b_rg_a, w_rg_x, b_rg_x, lru_lambda, g_attn_out, g_lru_out, w_out, g_norm_mlp, w_mlp_in, w_mlp_out, g_norm_final, loss_target, m_w_ada, m_b_ada, m_g_norm_mix, m_w_in, m_w_conv, m_b_conv, m_w_rg_a, m_b_rg_a, m_w_rg_x, m_b_rg_x, m_lru_lambda, m_g_attn_out, m_g_lru_out, m_w_out, m_g_norm_mlp, m_w_mlp_in, m_w_mlp_out, m_g_norm_final, v_w_ada, v_b_ada, v_g_norm_mix, v_w_in, v_w_conv, v_b_conv, v_w_rg_a, v_b_rg_a, v_w_rg_x, v_b_rg_x, v_lru_lambda, v_g_attn_out, v_g_lru_out, v_w_out, v_g_norm_mlp, v_w_mlp_in, v_w_mlp_out, v_g_norm_final):
    s, d = x.shape[1], x.shape[2]
    aw = d // 2
    nh = aw // HEAD
    f = w_mlp_out.shape[1] * N_CHIP
    n_ada = w_ada.shape[2]
    n_cv = w_conv.shape[2]
    ix, iy, ic = lax.axis_index("x"), lax.axis_index("y"), lax.axis_index("c")
    chip = 2 * ix + iy
    me = 2 * chip + ic
    c_arr = jnp.reshape(ic, (1,)).astype(jnp.int32)
    j_arr = jnp.reshape(chip, (1,)).astype(jnp.int32)

    x2d, tgt = x[0], loss_target[0]

    slab, sizes = _pack([c, w_conv])
    gathered = _all_gather_small("comm_gather_cond", slab)
    c_parts, cv_parts = _unpack(gathered, sizes, [(d,), (CONV_TAPS, n_cv)], lead=(N_DEV,))
    c_all = c_parts
    w_conv_full = jnp.concatenate([cv_parts[2 * j] for j in range(N_CHIP)], axis=-1)
    b_sh = lax.dynamic_slice(b_ada, (0, chip * n_ada), (1, n_ada))
    mod_part, act_all = _ada_mod(c_all, w_ada[0], b_sh)
    mod_g = _all_gather_small("comm_gather_mod", mod_part.reshape(-1, LANES))
    mod_g = mod_g.reshape(N_DEV, N_DEV, n_ada)
    mod = jnp.concatenate([lax.dynamic_index_in_dim(mod_g[2 * j], me, 0, keepdims=True) for j in range(N_CHIP)], axis=-1)
    sh1, sc1, gt1, sh2, sc2, gt2 = [mod[:, k * d:(k + 1) * d] for k in range(N_MOD)]

    kinds = ("col", "row", "col", "row")
    groups = ((0,), (1, 2), (3,))
    placed = [_place_cast("place_cast_%d" % i, j_arr, w_[0], k_)
              for i, (w_, k_) in enumerate(zip((w_in, w_out, w_mlp_in, w_mlp_out), kinds))]
    ag_send, ag_recv, ag_full, _ = _gather_start(placed, kinds, groups, mod)

    def gather_pass(g, after, thru):
        ws = groups[g]
        return _gather_pass("gather_pass_%d" % g, [ag_full[w] for w in ws], [kinds[w] for w in ws],
                            ag_send[g], ag_recv[g], after, thru)

    fl, fs, fr, sh1 = gather_pass(0, mod, sh1)
    h1, rstd1 = _norm_mod_fwd("norm_mod_fwd1", x2d, g_norm_mix, sc1, sh1)
    (w_in_f,) = _gather_finish("gather_finish_0", fl, kinds[0:1], fs, fr, h1)
    (qkv,) = _matmul("mm_qkv", h1, w_in_f, "nn", s, 3 * aw, d, (BF16,))
    (xrg,) = _matmul("mm_xrg", h1, w_in_f, "nn", s, 2 * aw, d, (F32,), b_off=3 * aw)
    o_attn = _attn_fwd(qkv, nh)
    fl, fs, fr, xrg = gather_pass(1, o_attn, xrg)
    wa3, wx3 = w_rg_a[0], w_rg_x[0]
    o_lru, hseq = _lru_fwd(xrg, w_conv_full, b_conv, wa3, b_rg_a, wx3, b_rg_x, lru_lambda)
    mixed, rstd_a, rstd_l = _mix_norm_fwd(o_attn, o_lru, g_attn_out, g_lru_out)
    w_out_f, w_mi_f = _gather_finish("gather_finish_1", fl, kinds[1:3], fs, fr, mixed)

    def residual(acc, xin, gt):
        return acc, xin + gt * acc

    y1, x1 = _matmul("mm_out", mixed, w_out_f, "nn", s, d, d, (F32, F32), extras=(x2d, gt1),
                     extra_kinds=("tile", "row"), epilogue=residual)
    h2, rstd2 = _norm_mod_fwd("norm_mod_fwd2", x1, g_norm_mlp, sc2, sh2)
    fl, fs, fr, h2 = gather_pass(2, x1, h2)

    def sq_relu(acc):
        r = jnp.maximum(acc, 0.0)
        return 2.0 * r, r * r

    r2, hid = _matmul("mm_mlp_in", h2, w_mi_f, "nn", s, f, d, (BF16, BF16), epilogue=sq_relu)
    (w_mo_f,) = _gather_finish("gather_finish_2", fl, kinds[3:4], fs, fr, hid)
    y2, x2 = _matmul("mm_mlp_out", hid, w_mo_f, "nn", s, d, f, (F32, F32), extras=(x1, gt2),
                     extra_kinds=("tile", "row"), epilogue=residual)
    dx2, loss_row, dg_final = _final_loss(x2, g_norm_final.reshape(1, d), tgt)

    def rs_begin(tag, g, kind, thru):
        send, recv, g, land, thru = _split_start("rs_swap_start_" + tag, g, _half_shape(g, kind), 1,
                                                 _plan_swap_half(kind), thru)
        return {"tag": tag, "kind": kind, "swap": (send, recv, g, land)}, thru

    def rs_mid(st, after, thru):
        tag, kind = st["tag"], st["kind"]
        g, got = _split_wait("rs_swap_wait_" + tag, *st["swap"], _plan_swap_half(kind), after)
        part = _add_halves("add_halves_" + tag, c_arr, g, got, kind)
        blk = (part.shape[0], part.shape[1] // N_CHIP) if kind == "col" else part.shape[1:]
        send, recv, part, land, thru = _split_start("rs_scatter_start_" + tag, part, (N_CHIP - 1,) + blk, N_CHIP - 1,
                                                    _plan_scatter(kind), thru)
        st["scatter"] = (send, recv, part, land)
        return thru

    def rs_end(st, after):
        tag, kind = st["tag"], st["kind"]
        part, got = _split_wait("rs_scatter_wait_" + tag, *st["scatter"], _plan_scatter(kind), after)
        return _sum_partials("sum_partials_" + tag, j_arr, part, got, kind)

    dy2, dgt2 = _gate_bwd("gate_bwd2", dx2, y2, gt2)
    (dpre,) = _matmul("mm_dhid", dy2, w_mo_f, "nt", s, f, d, (BF16,), extras=(r2,), extra_kinds=("tile",),
                      epilogue=lambda acc, r: (acc * r.astype(F32),))
    (g_mo,) = _matmul("mm_dw_mlp_out", hid, dy2, "tn", f, d, s, (BF16,))
    st_mo, dpre = rs_begin("mo", g_mo.reshape(N_CHIP, f // N_CHIP, d), "row", dpre)
    (dh2,) = _matmul("mm_dh2", dpre, w_mi_f, "nt", s, d, f, (F32,))
    (g_mi,) = _matmul("mm_dw_mlp_in", h2, dpre, "tn", d, f, s, (BF16,))
    dh2 = rs_mid(st_mo, g_mi, dh2)
    st_mi, dh2 = rs_begin("mi", g_mi, "col", dh2)
    dx1, dsh2, dsc2, dg_mlp = _norm_mod_bwd("norm_mod_bwd2", dh2, x1, rstd2, g_norm_mlp, sc2, dx2)
    dy1, dgt1 = _gate_bwd("gate_bwd1", dx1, y1, gt1)
    (dmixed,) = _matmul("mm_dmixed", dy1, w_out_f, "nt", s, d, d, (F32,))
    (g_out,) = _matmul("mm_dw_out", mixed, dy1, "tn", d, d, s, (BF16,))
    dmixed = rs_mid(st_mi, g_out, dmixed)
    st_out, dmixed = rs_begin("out", g_out.reshape(N_CHIP, d // N_CHIP, d), "row", dmixed)
    do_attn, do_lru, dg_attn, dg_lru = _mix_norm_bwd(dmixed, o_attn, o_lru, rstd_a, rstd_l, g_attn_out, g_lru_out)
    dq, dk, dv = _attn_bwd(qkv, do_attn, nh)
    do_lru = rs_mid(st_out, dq, do_lru)
    dxr, dxg, dwconv, dbconv, dwa, dba, dwx, dbx, dlam = _lru_bwd(
        xrg, do_lru, hseq, w_conv_full, b_conv, wa3, b_rg_a, wx3, b_rg_x, lru_lambda)
    dproj = jnp.concatenate([dq, dk, dv, dxr, dxg], axis=-1)
    (dh1,) = _matmul("mm_dh1", dproj, w_in_f, "nt", s, d, 5 * aw, (F32,))
    (g_in,) = _matmul("mm_dw_in", h1, dproj, "tn", d, 5 * aw, s, (BF16,))
    st_in, dh1 = rs_begin("in", g_in, "col", dh1)
    grad_x, dsh1, dsc1, dg_mix = _norm_mod_bwd("norm_mod_bwd1", dh1, x2d, rstd1, g_norm_mix, sc1, dx1)

    dmod = jnp.concatenate([dsh1, dsc1, dgt1, dsh2, dsc2, dgt2], axis=-1)
    small_names = ["b_ada", "g_norm_mix", "b_conv", "w_rg_a", "b_rg_a", "w_rg_x", "b_rg_x", "lru_lambda",
                   "g_attn_out", "g_lru_out", "g_norm_mlp", "g_norm_final"]
    small_g = [dmod, dg_mix, dbconv, dwa, dba, dwx, dbx, dlam, dg_attn, dg_lru, dg_mlp, dg_final]
    small_w = [b_ada, g_norm_mix, b_conv, w_rg_a, b_rg_a, w_rg_x, b_rg_x, lru_lambda, g_attn_out, g_lru_out, g_norm_mlp, g_norm_final]
    small_m = [m_b_ada, m_g_norm_mix, m_b_conv, m_w_rg_a, m_b_rg_a, m_w_rg_x, m_b_rg_x, m_lru_lambda, m_g_attn_out, m_g_lru_out, m_g_norm_mlp, m_g_norm_final]
    small_v = [v_b_ada, v_g_norm_mix, v_b_conv, v_w_rg_a, v_b_rg_a, v_w_rg_x, v_b_rg_x, v_lru_lambda, v_g_attn_out, v_g_lru_out, v_g_norm_mlp, v_g_norm_final]
    extra_zero = [jnp.zeros_like(dwconv), jnp.zeros((LANES,), F32)]
    g_slab, g_sizes = _pack(small_g + [dwconv, loss_row])
    w_slab, _ = _pack(small_w + extra_zero)
    m_slab, _ = _pack(small_m + extra_zero)
    v_slab, _ = _pack(small_v + extra_zero)
    g_all = _all_gather_small("comm_gather_small_grads", g_slab)
    gs_slab, ds_slab, ms_slab, vs_slab = _small_reduce_adamw(g_all, w_slab, m_slab, v_slab)
    shapes = [w.shape for w in small_w] + [dwconv.shape, (LANES,)]
    gs = _unpack(gs_slab, g_sizes, shapes)
    ds = _unpack(ds_slab, g_sizes, shapes)
    ms = _unpack(ms_slab, g_sizes, shapes)
    vs = _unpack(vs_slab, g_sizes, shapes)
    small = {n: (gs[i], ds[i], ms[i], vs[i]) for i, n in enumerate(small_names)}
    loss = gs[-1][0]
    g_wconv = lax.dynamic_slice(gs[-2], (0, chip * n_cv), (CONV_TAPS, n_cv))
    d_wconv, m_wconv, v_wconv = _adamw_plain("adamw_conv", w_conv[0], g_wconv, m_w_conv[0], v_w_conv[0])
    small["w_conv"] = (g_wconv[None], d_wconv[None], m_wconv[None], v_wconv[None])

    dmod_all = _unpack(g_all, g_sizes, [(N_MOD * d,)], lead=(N_DEV,))[0]
    dmod_sel = lax.dynamic_slice(dmod_all, (0, chip * n_ada), (N_DEV, n_ada)).astype(BF16)
    act_t = act_all.T.astype(BF16)
    dmod_sel = rs_mid(st_in, gs_slab, dmod_sel)
    big = {"w_ada": _adamw_ada(w_ada[0], m_w_ada[0], v_w_ada[0], act_t, dmod_sel)}

    half_mo = rs_end(st_mo, big["w_ada"][1])
    half_mi = rs_end(st_mi, half_mo)
    half_out = rs_end(st_out, half_mi)
    got_out, got_mi, got_mo = _swap_reduced("comm_swap_reduced_a", [half_out, half_mi, half_mo])
    big["w_out"] = _adamw_halves("adamw_w_out", c_arr, w_out[0], m_w_out[0], v_w_out[0], half_out, got_out)
    big["w_mlp_in"] = _adamw_halves("adamw_w_mlp_in", c_arr, w_mlp_in[0], m_w_mlp_in[0], v_w_mlp_in[0], half_mi, got_mi)
    big["w_mlp_out"] = _adamw_halves("adamw_w_mlp_out", c_arr, w_mlp_out[0], m_w_mlp_out[0], v_w_mlp_out[0], half_mo, got_mo)
    half_in = rs_end(st_in, big["w_mlp_out"][1])
    (got_in,) = _swap_reduced("comm_swap_reduced_b", [half_in])
    big["w_in"] = _adamw_halves("adamw_w_in", c_arr, w_in[0], m_w_in[0], v_w_in[0], half_in, got_in)

    order = ["w_ada", "b_ada", "g_norm_mix", "w_in", "w_conv", "b_conv", "w_rg_a", "b_rg_a", "w_rg_x", "b_rg_x",
             "lru_lambda", "g_attn_out", "g_lru_out", "w_out", "g_norm_mlp", "w_mlp_in", "w_mlp_out", "g_norm_final"]
    res = {}
    for n in order:
        res[n] = tuple(t[None] for t in big[n]) if n in big else small[n]
    return (loss, grad_x[None],
            *[res[n][0] for n in order], *[res[n][1] for n in order],
            *[res[n][2] for n in order], *[res[n][3] for n in order])
```

```python
import functools
import math

import jax
import jax.numpy as jnp
from jax import lax
from jax.experimental import pallas as pl
from jax.experimental.pallas import tpu as pltpu

F32 = jnp.float32
BF16 = jnp.bfloat16
SDS = jax.ShapeDtypeStruct
MESH = pl.DeviceIdType.MESH

EPS = 1e-6
HEAD = 128
N_MOD = 6
CONV_TAPS = 4
LRU_C = 8.0
ADAM_LR, ADAM_B1, ADAM_B2, ADAM_EPS, ADAM_WD, ADAM_STEP = 0.001, 0.9, 0.999, 1e-08, 0.01, 10
N_DEV = 8
N_CHIP = 4
LANES = 128
SUBLANES = 8
VMEM_LIMIT = 56 * 1024 * 1024
PACK_ROWS = 256
MM_TILE_M, MM_TILE_N, MM_TILE_K = 1024, 1024, 2048


def _tile(dim, pref):
    t = min(dim, pref)
    while dim % t:
        t -= LANES
    return t


def _params(sem=None):
    return pltpu.CompilerParams(dimension_semantics=sem, vmem_limit_bytes=VMEM_LIMIT)


def _sigmoid(x):
    return 1.0 / (1.0 + jnp.exp(-x))


def _log_sigmoid(x):
    return jnp.minimum(x, 0.0) - jnp.log(1.0 + jnp.exp(-jnp.abs(x)))


def _gelu_parts(x):
    k0, k1 = math.sqrt(2.0 / math.pi), 0.044715
    t = jnp.tanh(k0 * (x + k1 * x * x * x))
    val = 0.5 * x * (1.0 + t)
    der = 0.5 * (1.0 + t) + 0.5 * x * (1.0 - t * t) * k0 * (1.0 + 3.0 * k1 * x * x)
    return val, der


def _dot(a, b):
    return jnp.dot(a, b, preferred_element_type=F32)


def _dot_nt(a, b):
    return lax.dot_general(a, b, (((1,), (1,)), ((), ())), preferred_element_type=F32)


def _dot_tn(a, b):
    return lax.dot_general(a, b, (((0,), (0,)), ((), ())), preferred_element_type=F32)


def _split_dot(x, tri):
    hi = x.astype(BF16)
    lo = (x - hi.astype(F32)).astype(BF16)
    return _dot(hi, tri) + _dot(lo, tri)


def _matmul(name, a, b, mode, m, n, k, out_dtypes, *, b_off=0, extras=(), extra_kinds=(), epilogue=None,
            tm=MM_TILE_M, tn=MM_TILE_N, tk=MM_TILE_K):
    tm, tn, tk = _tile(m, tm), _tile(math.gcd(n, b_off) if b_off else n, tn), _tile(k, tk)
    assert b_off % tn == 0
    nk = k // tk
    n_ex, n_out = len(extras), len(out_dtypes)
    dot = {"nn": _dot, "nt": _dot_nt, "tn": _dot_tn}[mode]

    def body(a_ref, b_ref, *rest):
        ex, outs = rest[:n_ex], rest[n_ex:n_ex + n_out]

        def finish(total):
            res = epilogue(total, *[e[...] for e in ex]) if epilogue else (total,)
            for o, r in zip(outs, res):
                o[...] = r.astype(o.dtype)

        if nk == 1:
            finish(dot(a_ref[...], b_ref[...]))
            return
        acc = rest[-1]
        kk = pl.program_id(2)

        @pl.when(kk == 0)
        def _():
            acc[...] = dot(a_ref[...], b_ref[...])

        @pl.when(jnp.logical_and(kk > 0, kk < nk - 1))
        def _():
            acc[...] += dot(a_ref[...], b_ref[...])

        @pl.when(kk == nk - 1)
        def _():
            finish(acc[...] + dot(a_ref[...], b_ref[...]))

    if mode == "nn":
        a_spec = pl.BlockSpec((tm, tk), lambda i, j, kk: (i, kk))
        b_spec = pl.BlockSpec((tk, tn), lambda i, j, kk: (kk, j + b_off // tn))
    elif mode == "nt":
        a_spec = pl.BlockSpec((tm, tk), lambda i, j, kk: (i, kk))
        b_spec = pl.BlockSpec((tn, tk), lambda i, j, kk: (j, kk + b_off // tk))
    else:
        a_spec = pl.BlockSpec((tk, tm), lambda i, j, kk: (kk, i))
        b_spec = pl.BlockSpec((tk, tn), lambda i, j, kk: (kk, j))
    tile_spec = pl.BlockSpec((tm, tn), lambda i, j, kk: (i, j))
    row_spec = pl.BlockSpec((1, tn), lambda i, j, kk: (0, j))
    ex_specs = [tile_spec if kind == "tile" else row_spec for kind in extra_kinds]
    outs = pl.pallas_call(
        body, grid=(m // tm, n // tn, nk),
        in_specs=[a_spec, b_spec] + ex_specs,
        out_specs=tuple(tile_spec for _ in out_dtypes),
        out_shape=tuple(SDS((m, n), dt) for dt in out_dtypes),
        scratch_shapes=[pltpu.VMEM((tm, tn), F32)] if nk > 1 else [],
        compiler_params=_params(("parallel", "parallel", "arbitrary")),
        name=name,
    )(a, b, *extras)
    return outs


def _row_specs(s, d, tr):
    row = pl.BlockSpec((tr, d), lambda i: (i, 0))
    vec = pl.BlockSpec((1, d), lambda i: (0, 0))
    col = pl.BlockSpec((tr, 1), lambda i: (i, 0))
    return row, vec, col


def _norm_mod_fwd(name, x, g, sc, sh):
    s, d = x.shape
    tr = _tile(s, 256)
    row, vec, col = _row_specs(s, d, tr)

    def body(x_ref, g_ref, sc_ref, sh_ref, h_ref, r_ref):
        xv = x_ref[...]
        r = lax.rsqrt(jnp.mean(xv * xv, axis=-1, keepdims=True) + EPS)
        h_ref[...] = ((xv * r * g_ref[...]) * (1.0 + sc_ref[...]) + sh_ref[...]).astype(BF16)
        r_ref[...] = r

    return pl.pallas_call(
        body, grid=(s // tr,), in_specs=[row, vec, vec, vec], out_specs=(row, col),
        out_shape=(SDS((s, d), BF16), SDS((s, 1), F32)),
        compiler_params=_params(("parallel",)), name=name)(x, g, sc, sh)


def _norm_mod_bwd(name, dh, xin, rstd, g, sc, dres):
    s, d = xin.shape
    tr = _tile(s, 256)
    row, vec, col = _row_specs(s, d, tr)

    def body(dh_ref, x_ref, r_ref, g_ref, sc_ref, dres_ref, dx_ref, dsh_ref, dsc_ref, dg_ref):
        @pl.when(pl.program_id(0) == 0)
        def _():
            dsh_ref[...] = jnp.zeros_like(dsh_ref)
            dsc_ref[...] = jnp.zeros_like(dsc_ref)
            dg_ref[...] = jnp.zeros_like(dg_ref)

        dh_v, xv, r, gv = dh_ref[...], x_ref[...], r_ref[...], g_ref[...]
        n0 = xv * r
        dsh_ref[...] += jnp.sum(dh_v, axis=0, keepdims=True)
        dsc_ref[...] += jnp.sum(dh_v * (n0 * gv), axis=0, keepdims=True)
        dn = dh_v * (1.0 + sc_ref[...])
        dg_ref[...] += jnp.sum(dn * n0, axis=0, keepdims=True)
        gy = dn * gv
        dot = jnp.mean(gy * xv, axis=-1, keepdims=True)
        dx_ref[...] = dres_ref[...] + r * gy - xv * (r * r * r * dot)

    return pl.pallas_call(
        body, grid=(s // tr,), in_specs=[row, row, col, vec, vec, row], out_specs=(row, vec, vec, vec),
        out_shape=(SDS((s, d), F32), SDS((1, d), F32), SDS((1, d), F32), SDS((1, d), F32)),
        compiler_params=_params(("arbitrary",)), name=name)(dh, xin, rstd, g, sc, dres)


def _gate_bwd(name, dx, y, gt):
    s, d = dx.shape
    tr = _tile(s, 256)
    row, vec, _ = _row_specs(s, d, tr)

    def body(dx_ref, y_ref, gt_ref, dy_ref, dgt_ref):
        @pl.when(pl.program_id(0) == 0)
        def _():
            dgt_ref[...] = jnp.zeros_like(dgt_ref)

        dxv = dx_ref[...]
        dy_ref[...] = (gt_ref[...] * dxv).astype(BF16)
        dgt_ref[...] += jnp.sum(dxv * y_ref[...], axis=0, keepdims=True)

    return pl.pallas_call(
        body, grid=(s // tr,), in_specs=[row, row, vec], out_specs=(row, vec),
        out_shape=(SDS((s, d), BF16), SDS((1, d), F32)),
        compiler_params=_params(("arbitrary",)), name=name)(dx, y, gt)


def _final_loss(x2, gf, tgt):
    s, d = x2.shape
    tr = _tile(s, 256)
    row, vec, _ = _row_specs(s, d, tr)
    lrow = pl.BlockSpec((1, LANES), lambda i: (0, 0))

    def body(x_ref, g_ref, t_ref, dx_ref, loss_ref, dg_ref):
        @pl.when(pl.program_id(0) == 0)
        def _():
            loss_ref[...] = jnp.zeros_like(loss_ref)
            dg_ref[...] = jnp.zeros_like(dg_ref)

        xv, gv = x_ref[...], g_ref[...]
        r = lax.rsqrt(jnp.mean(xv * xv, axis=-1, keepdims=True) + EPS)
        n0 = xv * r
        err = n0 * gv - t_ref[...]
        loss_ref[...] += jnp.sum(err * err) * (0.5 / d)
        dy = err * (1.0 / d)
        dg_ref[...] += jnp.sum(dy * n0, axis=0, keepdims=True)
        gy = dy * gv
        dot = jnp.mean(gy * xv, axis=-1, keepdims=True)
        dx_ref[...] = r * gy - xv * (r * r * r * dot)

    return pl.pallas_call(
        body, grid=(s // tr,), in_specs=[row, vec, row], out_specs=(row, lrow, vec),
        out_shape=(SDS((s, d), F32), SDS((1, LANES), F32), SDS((1, d), F32)),
        compiler_params=_params(("arbitrary",)), name="final_loss")(x2, gf, tgt)


def _mix_norm_fwd(oa, ol, ga, gl):
    s, w = oa.shape
    tr = _tile(s, 256)
    row, vec, col = _row_specs(s, w, tr)

    def body(oa_ref, ol_ref, ga_ref, gl_ref, mx_ref, ra_ref, rl_ref):
        a, l = oa_ref[...], ol_ref[...]
        ra = lax.rsqrt(jnp.mean(a * a, axis=-1, keepdims=True) + EPS)
        rl = lax.rsqrt(jnp.mean(l * l, axis=-1, keepdims=True) + EPS)
        mx_ref[:, :w] = (a * ra * ga_ref[...]).astype(BF16)
        mx_ref[:, w:] = (l * rl * gl_ref[...]).astype(BF16)
        ra_ref[...] = ra
        rl_ref[...] = rl

    return pl.pallas_call(
        body, grid=(s // tr,), in_specs=[row, row, vec, vec],
        out_specs=(pl.BlockSpec((tr, 2 * w), lambda i: (i, 0)), col, col),
        out_shape=(SDS((s, 2 * w), BF16), SDS((s, 1), F32), SDS((s, 1), F32)),
        compiler_params=_params(("parallel",)), name="mix_norm_fwd")(oa, ol, ga, gl)


def _mix_norm_bwd(dmx, oa, ol, ra, rl, ga, gl):
    s, w = oa.shape
    tr = _tile(s, 256)
    row, vec, col = _row_specs(s, w, tr)

    def body(dm_ref, oa_ref, ol_ref, ra_ref, rl_ref, ga_ref, gl_ref, doa_ref, dol_ref, dga_ref, dgl_ref):
        @pl.when(pl.program_id(0) == 0)
        def _():
            dga_ref[...] = jnp.zeros_like(dga_ref)
            dgl_ref[...] = jnp.zeros_like(dgl_ref)

        def one(dy, xv, r, gv, dg_ref):
            dg_ref[...] += jnp.sum(dy * (xv * r), axis=0, keepdims=True)
            gy = dy * gv
            dot = jnp.mean(gy * xv, axis=-1, keepdims=True)
            return r * gy - xv * (r * r * r * dot)

        doa_ref[...] = one(dm_ref[:, :w], oa_ref[...], ra_ref[...], ga_ref[...], dga_ref).astype(BF16)
        dol_ref[...] = one(dm_ref[:, w:], ol_ref[...], rl_ref[...], gl_ref[...], dgl_ref)

    return pl.pallas_call(
        body, grid=(s // tr,),
        in_specs=[pl.BlockSpec((tr, 2 * w), lambda i: (i, 0)), row, row, col, col, vec, vec],
        out_specs=(row, row, vec, vec),
        out_shape=(SDS((s, w), BF16), SDS((s, w), F32), SDS((1, w), F32), SDS((1, w), F32)),
        compiler_params=_params(("arbitrary",)), name="mix_norm_bwd")(dmx, oa, ol, ra, rl, ga, gl)


def _attn_blocks(qs, ks, tri_after, csums, causal):
    zs = [_dot_nt(q, k) * (HEAD ** -0.5) for q, k in zip(qs, ks)]
    lbs = [_log_sigmoid(z) for z in zs]
    lss = [lb - z for lb, z in zip(lbs, zs)]
    if causal is not None:
        lss = [jnp.where(causal, ls, 0.0) for ls in lss]
    locs = [_split_dot(ls, tri_after) for ls in lss]
    ws = [jnp.exp(lb + (loc + cs)) for lb, loc, cs in zip(lbs, locs, csums)]
    if causal is not None:
        ws = [jnp.where(causal, w, 0.0) for w in ws]
    nxt = [cs + (loc[:, 0:1] + ls[:, 0:1]) for cs, loc, ls in zip(csums, locs, lss)]
    return lbs, ws, nxt


ATTN_HEADS_PER_STEP = 4


def _attn_tile(s):
    return 256 if s >= 1024 else 128


def _tri(t, after):
    r_i = lax.broadcasted_iota(jnp.int32, (t, t), 0)
    c_i = lax.broadcasted_iota(jnp.int32, (t, t), 1)
    return ((r_i > c_i) if after else (r_i < c_i)).astype(BF16)


def _attn_fwd(qkv, n_heads):
    s = qkv.shape[0]
    t = _attn_tile(s)
    hps = ATTN_HEADS_PER_STEP
    wid = hps * HEAD

    def body(q_ref, k_ref, v_ref, o_ref):
        qi = pl.program_id(1)
        tri_after = _tri(t, True)
        causal = lax.broadcasted_iota(jnp.int32, (t, t), 1) < lax.broadcasted_iota(jnp.int32, (t, t), 0)
        lanes = [slice(a * HEAD, (a + 1) * HEAD) for a in range(hps)]
        qs = [q_ref[:, ln] for ln in lanes]

        def block(off, carry, mask):
            ks = [k_ref[pl.ds(off, t), ln] for ln in lanes]
            _, ws, csums = _attn_blocks(qs, ks, tri_after, [cr[0] for cr in carry], mask)
            os_ = [cr[1] + _dot(w.astype(BF16), v_ref[pl.ds(off, t), ln]) for cr, w, ln in zip(carry, ws, lanes)]
            return tuple(zip(csums, os_))

        zero = tuple((jnp.zeros((t, 1), F32), jnp.zeros((t, HEAD), F32)) for _ in lanes)
        carry = block(pl.multiple_of(qi * t, t), zero, causal)
        carry = lax.fori_loop(1, qi + 1, lambda it, cr: block(pl.multiple_of((qi - it) * t, t), cr, None), carry)
        for a, ln in enumerate(lanes):
            o_ref[:, ln] = carry[a][1]

    hb = n_heads // hps
    return pl.pallas_call(
        body, grid=(hb, s // t),
        in_specs=[pl.BlockSpec((t, wid), lambda hh, i: (i, hh)),
                  pl.BlockSpec((s, wid), lambda hh, i: (0, hb + hh)),
                  pl.BlockSpec((s, wid), lambda hh, i: (0, 2 * hb + hh))],
        out_specs=pl.BlockSpec((t, wid), lambda hh, i: (i, hh)),
        out_shape=SDS((s, n_heads * HEAD), F32),
        compiler_params=_params(("parallel", "parallel")), name="attn_fwd")(qkv, qkv, qkv)


def _attn_bwd(qkv, do, n_heads):
    s = qkv.shape[0]
    t = _attn_tile(s)
    nq = s // t
    scale = HEAD ** -0.5
    hps = ATTN_HEADS_PER_STEP
    wid = hps * HEAD

    def body(q_ref, k_ref, v_ref, do_ref, dq_ref, dk_ref, dv_ref, e_s, sg_s, dk_acc, dv_acc):
        qi = pl.program_id(1)

        @pl.when(qi == 0)
        def _():
            dk_acc[...] = jnp.zeros_like(dk_acc)
            dv_acc[...] = jnp.zeros_like(dv_acc)

        tri_after = _tri(t, True)
        tri_before = _tri(t, False)
        causal = lax.broadcasted_iota(jnp.int32, (t, t), 1) < lax.broadcasted_iota(jnp.int32, (t, t), 0)
        lanes = [slice(a * HEAD, (a + 1) * HEAD) for a in range(hps)]
        qs = [q_ref[:, ln] for ln in lanes]
        douts = [do_ref[:, ln] for ln in lanes]

        def pass1(kb, csums, mask):
            off = pl.multiple_of(kb * t, t)
            ks = [k_ref[pl.ds(off, t), ln] for ln in lanes]
            dws = [_dot_nt(dout, v_ref[pl.ds(off, t), ln]) for dout, ln in zip(douts, lanes)]
            lbs, ws, nxt = _attn_blocks(qs, ks, tri_after, csums, mask)
            for a, ln in enumerate(lanes):
                e_s[a, kb] = dws[a] * ws[a]
                sg_s[a, kb] = jnp.exp(lbs[a])
            for a, ln in enumerate(lanes):
                dv_acc[pl.ds(off, t), ln] += _dot_tn(ws[a].astype(BF16), douts[a])
            return tuple(nxt)

        cs = pass1(qi, tuple(jnp.zeros((t, 1), F32) for _ in lanes), causal)
        lax.fori_loop(1, qi + 1, lambda it, c_: pass1(qi - it, c_, None), cs)

        def pass2(kb, carry, mask):
            off = pl.multiple_of(kb * t, t)
            es = [e_s[a, kb] for a in range(hps)]
            locs = [_split_dot(e, tri_before) for e in es]
            stays = [(loc + cr[0]) * sg_s[a, kb] for a, (loc, cr) in enumerate(zip(locs, carry))]
            if mask is not None:
                stays = [jnp.where(mask, st, 0.0) for st in stays]
            dzbs = [((e * (1.0 - sg_s[a, kb]) - st) * scale).astype(BF16) for a, (e, st) in enumerate(zip(es, stays))]
            dqs = [cr[1] + _dot(dzb, k_ref[pl.ds(off, t), ln]) for cr, dzb, ln in zip(carry, dzbs, lanes)]
            for a, ln in enumerate(lanes):
                dk_acc[pl.ds(off, t), ln] += _dot_tn(dzbs[a], qs[a])
            esums = [cr[0] + (loc[:, t - 1:t] + e[:, t - 1:t]) for cr, loc, e in zip(carry, locs, es)]
            return tuple(zip(esums, dqs))

        zero = tuple((jnp.zeros((t, 1), F32), jnp.zeros((t, HEAD), F32)) for _ in lanes)
        carry = lax.fori_loop(0, qi, lambda kb, cr: pass2(kb, cr, None), zero)
        carry = pass2(qi, carry, causal)
        for a, ln in enumerate(lanes):
            dq_ref[:, ln] = carry[a][1].astype(BF16)

        @pl.when(qi == nq - 1)
        def _():
            dk_ref[...] = dk_acc[...].astype(BF16)
            dv_ref[...] = dv_acc[...].astype(BF16)

    hb = n_heads // hps
    blk = pl.BlockSpec((t, wid), lambda hh, i: (i, hh))
    full = pl.BlockSpec((s, wid), lambda hh, i: (0, hh))
    return pl.pallas_call(
        body, grid=(hb, nq),
        in_specs=[blk,
                  pl.BlockSpec((s, wid), lambda hh, i: (0, hb + hh)),
                  pl.BlockSpec((s, wid), lambda hh, i: (0, 2 * hb + hh)),
                  blk],
        out_specs=(blk, full, full),
        out_shape=(SDS((s, n_heads * HEAD), BF16),) * 3,
        scratch_shapes=[pltpu.VMEM((hps, nq, t, t), F32), pltpu.VMEM((hps, nq, t, t), F32),
                        pltpu.VMEM((s, wid), F32), pltpu.VMEM((s, wid), F32)],
        compiler_params=_params(("parallel", "arbitrary")), name="attn_bwd")(qkv, qkv, qkv, do)


def _lru_chunk(s):
    return 256 if s >= 1024 else 128


def _lru_gates(xc, wa, ba, wx, bx, sp):
    xb = xc.astype(BF16)
    r = _sigmoid(_dot(xb, wa) + ba)
    ig = _sigmoid(_dot(xb, wx) + bx)
    la = -LRU_C * r * sp
    a = jnp.exp(la)
    t = jnp.tanh(la)
    mult = jnp.sqrt(-2.0 * t / (1.0 - t))
    return r, ig, a, mult


def _softplus_neg(lam):
    return jnp.maximum(-lam, 0.0) + jnp.log(1.0 + jnp.exp(-jnp.abs(lam)))


def _lru_specs(s, n_blocks):
    seq0 = pl.BlockSpec((s, HEAD), lambda h: (0, h))
    seq1 = pl.BlockSpec((s, HEAD), lambda h: (0, n_blocks + h))
    taps = pl.BlockSpec((CONV_TAPS, HEAD), lambda h: (0, h))
    vec = pl.BlockSpec((1, HEAD), lambda h: (0, h))
    mat = pl.BlockSpec((None, HEAD, HEAD), lambda h: (h, 0, 0))
    return seq0, seq1, taps, vec, mat


def _lru_fwd(xrg, wconv, bconv, wa, ba, wx, bx, lam):
    s = xrg.shape[0]
    nb = wa.shape[0]
    tc = _lru_chunk(s)
    seq0, seq1, taps, vec, mat = _lru_specs(s, nb)
    pad = SUBLANES

    def body(xr_ref, xg_ref, wc_ref, bc_ref, wa_ref, ba_ref, wx_ref, bx_ref, lam_ref, o_ref, h_ref, pad_s, a_s, u_s):
        pad_s[0:pad, :] = jnp.zeros((pad, HEAD), F32)
        pad_s[pad:pad + s, :] = xr_ref[...]
        wab, wxb = wa_ref[...].astype(BF16), wx_ref[...].astype(BF16)
        sp = _softplus_neg(lam_ref[...])
        for c in range(s // tc):
            base = c * tc
            xc = bc_ref[...] + sum(wc_ref[i:i + 1, :] * pad_s[pl.ds(base + pad - (CONV_TAPS - 1) + i, tc), :]
                                   for i in range(CONV_TAPS))
            _, ig, a, mult = _lru_gates(xc, wab, ba_ref[...], wxb, bx_ref[...], sp)
            a_s[base:base + tc, :] = a
            u_s[base:base + tc, :] = mult * (ig * xc)

        row = lax.broadcasted_iota(jnp.int32, (SUBLANES, HEAD), 0)

        def chunk(ci, hprev):
            off = pl.multiple_of(ci * SUBLANES, SUBLANES)
            a8, b8 = a_s[pl.ds(off, SUBLANES), :], u_s[pl.ds(off, SUBLANES), :]
            for d in (1, 2, 4):
                a_sh = jnp.where(row < d, 1.0, pltpu.roll(a8, d, 0))
                b_sh = jnp.where(row < d, 0.0, pltpu.roll(b8, d, 0))
                b8 = a8 * b_sh + b8
                a8 = a8 * a_sh
            h8 = a8 * hprev + b8
            h_ref[pl.ds(off, SUBLANES), :] = h8
            return h8[SUBLANES - 1:SUBLANES, :]

        lax.fori_loop(0, s // SUBLANES, chunk, jnp.zeros((1, HEAD), F32), unroll=8)
        for c in range(s // tc):
            sl = slice(c * tc, (c + 1) * tc)
            gel, _ = _gelu_parts(xg_ref[sl, :])
            o_ref[sl, :] = h_ref[sl, :] * gel

    return pl.pallas_call(
        body, grid=(nb,),
        in_specs=[seq0, seq1, taps, vec, mat, vec, mat, vec, vec],
        out_specs=(seq0, seq0),
        out_shape=(SDS((s, nb * HEAD), F32), SDS((s, nb * HEAD), F32)),
        scratch_shapes=[pltpu.VMEM((s + pad, HEAD), F32), pltpu.VMEM((s, HEAD), F32), pltpu.VMEM((s, HEAD), F32)],
        compiler_params=_params(("parallel",)), name="lru_fwd")(xrg, xrg, wconv, bconv, wa, ba, wx, bx, lam)


def _lru_bwd(xrg, dol, hseq, wconv, bconv, wa, ba, wx, bx, lam):
    s = xrg.shape[0]
    nb = wa.shape[0]
    tc = _lru_chunk(s)
    seq0, seq1, taps, vec, mat = _lru_specs(s, nb)
    pad = SUBLANES

    def body(xr_ref, xg_ref, do_ref, h_ref, wc_ref, bc_ref, wa_ref, ba_ref, wx_ref, bx_ref, lam_ref,
             dxr_ref, dxg_ref, dwc_ref, dbc_ref, dwa_ref, dba_ref, dwx_ref, dbx_ref, dlam_ref,
             pad_s, hp_s, a_s, g_s, da_s, dxc_s):
        pad_s[0:pad, :] = jnp.zeros((pad, HEAD), F32)
        pad_s[pad:pad + s, :] = xr_ref[...]
        hp_s[0:pad, :] = jnp.zeros((pad, HEAD), F32)
        hp_s[pad:pad + s, :] = h_ref[...]
        a_s[s:s + pad, :] = jnp.zeros((pad, HEAD), F32)
        dxc_s[s:s + pad, :] = jnp.zeros((pad, HEAD), F32)
        wab, wxb = wa_ref[...].astype(BF16), wx_ref[...].astype(BF16)
        lam_v = lam_ref[...]
        sp = _softplus_neg(lam_v)

        def conv_in(c):
            base = c * tc
            wins = [pad_s[pl.ds(base + pad - (CONV_TAPS - 1) + i, tc), :] for i in range(CONV_TAPS)]
            xc = bc_ref[...] + sum(wc_ref[i:i + 1, :] * wins[i] for i in range(CONV_TAPS))
            return xc, wins

        for c in range(s // tc):
            sl = slice(c * tc, (c + 1) * tc)
            xc, _ = conv_in(c)
            _, _, a, _ = _lru_gates(xc, wab, ba_ref[...], wxb, bx_ref[...], sp)
            a_s[sl, :] = a
            gel, dgel = _gelu_parts(xg_ref[sl, :])
            dov = do_ref[sl, :]
            g_s[sl, :] = dov * gel
            dxg_ref[sl, :] = (dov * h_ref[sl, :] * dgel).astype(BF16)

        row = lax.broadcasted_iota(jnp.int32, (SUBLANES, HEAD), 0)
        n_chunks = s // SUBLANES

        def chunk(it, gnext):
            ci = n_chunks - 1 - it
            off = pl.multiple_of(ci * SUBLANES, SUBLANES)
            a8 = a_s[pl.ds(off, SUBLANES), :]
            a8n = a_s[pl.ds(off + SUBLANES, SUBLANES), :]
            c8 = pltpu.roll(jnp.where(row == 0, a8n, a8), SUBLANES - 1, 0)
            g8 = g_s[pl.ds(off, SUBLANES), :]
            for d in (1, 2, 4):
                c_sh = jnp.where(row >= SUBLANES - d, 1.0, pltpu.roll(c8, SUBLANES - d, 0))
                g_sh = jnp.where(row >= SUBLANES - d, 0.0, pltpu.roll(g8, SUBLANES - d, 0))
                g8 = c8 * g_sh + g8
                c8 = c8 * c_sh
            g8 = g8 + c8 * gnext
            g_s[pl.ds(off, SUBLANES), :] = g8
            h8 = hp_s[pl.ds(off + pad, SUBLANES), :]
            h8p = hp_s[pl.ds(off, SUBLANES), :]
            da_s[pl.ds(off, SUBLANES), :] = g8 * pltpu.roll(jnp.where(row == SUBLANES - 1, h8p, h8), 1, 0)
            return g8[0:1, :]

        lax.fori_loop(0, n_chunks, chunk, jnp.zeros((1, HEAD), F32), unroll=8)

        dsp = jnp.zeros((1, HEAD), F32)
        dbc = jnp.zeros((1, HEAD), F32)
        dba = jnp.zeros((1, HEAD), F32)
        dbx = jnp.zeros((1, HEAD), F32)
        dwa = jnp.zeros((HEAD, HEAD), F32)
        dwx = jnp.zeros((HEAD, HEAD), F32)
        dwc = [jnp.zeros((1, HEAD), F32) for _ in range(CONV_TAPS)]
        for c in range(s // tc):
            sl = slice(c * tc, (c + 1) * tc)
            xc, wins = conv_in(c)
            r, ig, a, mult = _lru_gates(xc, wab, ba_ref[...], wxb, bx_ref[...], sp)
            du, da = g_s[sl, :], da_s[sl, :]
            d_ix = du * mult
            dla = da * a - (du * ig * xc) * (a * a / mult)
            dsp = dsp + jnp.sum(dla * r, axis=0, keepdims=True) * (-LRU_C)
            dpa = (dla * (-LRU_C * sp)) * r * (1.0 - r)
            dpx = (d_ix * xc) * ig * (1.0 - ig)
            dpab, dpxb, xb = dpa.astype(BF16), dpx.astype(BF16), xc.astype(BF16)
            dxc = d_ix * ig + _dot_nt(dpab, wab) + _dot_nt(dpxb, wxb)
            dwa = dwa + _dot_tn(xb, dpab)
            dwx = dwx + _dot_tn(xb, dpxb)
            dba = dba + jnp.sum(dpa, axis=0, keepdims=True)
            dbx = dbx + jnp.sum(dpx, axis=0, keepdims=True)
            dbc = dbc + jnp.sum(dxc, axis=0, keepdims=True)
            for i in range(CONV_TAPS):
                dwc[i] = dwc[i] + jnp.sum(dxc * wins[i], axis=0, keepdims=True)
            dxc_s[sl, :] = dxc

        for c in range(s // tc):
            base = c * tc
            dxr = sum(wc_ref[i:i + 1, :] * dxc_s[pl.ds(base + (CONV_TAPS - 1) - i, tc), :] for i in range(CONV_TAPS))
            dxr_ref[base:base + tc, :] = dxr.astype(BF16)

        for i in range(CONV_TAPS):
            dwc_ref[i:i + 1, :] = dwc[i]
        dbc_ref[...] = dbc
        dwa_ref[...] = dwa
        dwx_ref[...] = dwx
        dba_ref[...] = dba
        dbx_ref[...] = dbx
        dlam_ref[...] = dsp * (-_sigmoid(-lam_v))

    w = nb * HEAD
    return pl.pallas_call(
        body, grid=(nb,),
        in_specs=[seq0, seq1, seq0, seq0, taps, vec, mat, vec, mat, vec, vec],
        out_specs=(seq0, seq0, taps, vec, mat, vec, mat, vec, vec),
        out_shape=(SDS((s, w), BF16), SDS((s, w), BF16), SDS((CONV_TAPS, w), F32), SDS((1, w), F32),
                   SDS((nb, HEAD, HEAD), F32), SDS((1, w), F32), SDS((nb, HEAD, HEAD), F32), SDS((1, w), F32),
                   SDS((1, w), F32)),
        scratch_shapes=[pltpu.VMEM((s + pad, HEAD), F32), pltpu.VMEM((s + pad, HEAD), F32),
                        pltpu.VMEM((s + pad, HEAD), F32), pltpu.VMEM((s, HEAD), F32),
                        pltpu.VMEM((s, HEAD), F32), pltpu.VMEM((s + pad, HEAD), F32)],
        compiler_params=_params(("parallel",)), name="lru_bwd",
    )(xrg, xrg, dol, hseq, wconv, bconv, wa, ba, wx, bx, lam)


def _ada_mod(c_all, w_sh, b_sh):
    n_ex, d = c_all.shape
    n = w_sh.shape[1]
    tn = _tile(n, 512)

    def body(c_ref, w_ref, b_ref, mod_ref, act_ref):
        cv = c_ref[...]
        act = cv * _sigmoid(cv)
        act_ref[...] = act
        mod_ref[...] = _dot(act.astype(BF16), w_ref[...].astype(BF16)) + b_ref[...]

    return pl.pallas_call(
        body, grid=(n // tn,),
        in_specs=[pl.BlockSpec((n_ex, d), lambda j: (0, 0)), pl.BlockSpec((d, tn), lambda j: (0, j)),
                  pl.BlockSpec((1, tn), lambda j: (0, j))],
        out_specs=(pl.BlockSpec((n_ex, tn), lambda j: (0, j)), pl.BlockSpec((n_ex, d), lambda j: (0, 0))),
        out_shape=(SDS((n_ex, n), F32), SDS((n_ex, d), F32)),
        compiler_params=_params(("arbitrary",)), name="ada_mod")(c_all, w_sh, b_sh)


def _adamw_math(w, g, m, v):
    m = ADAM_B1 * m + (1.0 - ADAM_B1) * g
    v = ADAM_B2 * v + (1.0 - ADAM_B2) * (g * g)
    m_hat = m / (1.0 - ADAM_B1 ** ADAM_STEP)
    v_hat = v / (1.0 - ADAM_B2 ** ADAM_STEP)
    delta = -ADAM_LR * (m_hat / (jnp.sqrt(v_hat) + ADAM_EPS) + ADAM_WD * w)
    return delta, m, v


def _adamw_plain(name, w, g, m, v):
    def body(w_ref, g_ref, m_ref, v_ref, d_ref, mo_ref, vo_ref):
        d_ref[...], mo_ref[...], vo_ref[...] = _adamw_math(w_ref[...], g_ref[...], m_ref[...], v_ref[...])

    return pl.pallas_call(body, out_shape=(SDS(w.shape, F32),) * 3, name=name)(w, g, m, v)


def _adamw_halves(name, c_arr, w, m, v, g_own, g_recv):
    r, n = w.shape
    rh = r // 2
    tr = _tile(rh, 256)
    nh = rh // tr

    def body(c_ref, w_ref, m_ref, v_ref, go_ref, gr_ref, g_ref, d_ref, mo_ref, vo_ref):
        own = (pl.program_id(0) // nh) == c_ref[0]
        g = jnp.where(own, go_ref[...], gr_ref[...])
        g_ref[...] = g
        d_ref[...], mo_ref[...], vo_ref[...] = _adamw_math(w_ref[...], g, m_ref[...], v_ref[...])

    full = pl.BlockSpec((tr, n), lambda i, c: (i, 0))
    half = pl.BlockSpec((tr, n), lambda i, c: (i % nh, 0))
    return pl.pallas_call(
        body,
        grid_spec=pltpu.PrefetchScalarGridSpec(
            num_scalar_prefetch=1, grid=(2 * nh,), in_specs=[full, full, full, half, half],
            out_specs=(full,) * 4),
        out_shape=(SDS((r, n), F32),) * 4,
        compiler_params=_params(("parallel",)), name=name)(c_arr, w, m, v, g_own, g_recv)


def _adamw_ada(w, m, v, act_t, dmod):
    d, n = w.shape
    n_ex = act_t.shape[1]
    tr = _tile(d, 256)

    def body(a_ref, dm_ref, w_ref, m_ref, v_ref, g_ref, d_ref, mo_ref, vo_ref):
        g = _dot(a_ref[...], dm_ref[...])
        g_ref[...] = g
        d_ref[...], mo_ref[...], vo_ref[...] = _adamw_math(w_ref[...], g, m_ref[...], v_ref[...])

    full = pl.BlockSpec((tr, n), lambda i: (i, 0))
    return pl.pallas_call(
        body, grid=(d // tr,),
        in_specs=[pl.BlockSpec((tr, n_ex), lambda i: (i, 0)), pl.BlockSpec((n_ex, n), lambda i: (0, 0)), full, full, full],
        out_specs=(full,) * 4, out_shape=(SDS((d, n), F32),) * 4,
        compiler_params=_params(("parallel",)), name="adamw_ada")(act_t, dmod, w, m, v)


def _small_reduce_adamw(parts, w, m, v):
    n_dev, r, _ = parts.shape
    tr = r if r <= PACK_ROWS else PACK_ROWS

    def body(p_ref, w_ref, m_ref, v_ref, g_ref, d_ref, mo_ref, vo_ref):
        g = p_ref[0]
        for k in range(1, n_dev):
            g = g + p_ref[k]
        g_ref[...] = g
        d_ref[...], mo_ref[...], vo_ref[...] = _adamw_math(w_ref[...], g, m_ref[...], v_ref[...])

    full = pl.BlockSpec((tr, LANES), lambda i: (i, 0))
    return pl.pallas_call(
        body, grid=(r // tr,),
        in_specs=[pl.BlockSpec((n_dev, tr, LANES), lambda i: (0, i, 0)), full, full, full],
        out_specs=(full,) * 4, out_shape=(SDS((r, LANES), F32),) * 4,
        compiler_params=_params(("parallel",)), name="small_reduce_adamw")(parts, w, m, v)


def _mesh_pos():
    return lax.axis_index("x"), lax.axis_index("y"), lax.axis_index("c")


def _other_chips(x, y):
    return [(1 - x, y), (x, 1 - y), (1 - x, 1 - y)]


def _all_gather_small(name, blk):
    r, n = blk.shape

    def body(x_ref, out_ref, send_sems, recv_sems, local_sem):
        x, y, c = _mesh_pos()
        me, sibling = (x, y, c), (x, y, 1 - c)
        chips = _other_chips(x, y)

        def rows(px, py, pc):
            return out_ref.at[4 * px + 2 * py + pc]

        def copy(k, block, to, src=None):
            return pltpu.make_async_remote_copy(
                src_ref=rows(*block) if src is None else src, dst_ref=rows(*block),
                send_sem=send_sems.at[k], recv_sem=recv_sems.at[k], device_id=to, device_id_type=MESH)

        mine = pltpu.make_async_copy(x_ref, rows(*me), local_sem)
        mine.start()
        first = [copy(0, me, sibling, src=x_ref)]
        first += [copy(1 + j, me, (*chip, c), src=x_ref) for j, chip in enumerate(chips)]
        for cp in first:
            cp.start()
        passed = [copy(4 + j, (*chip, c), sibling) for j, chip in enumerate(chips)]
        for j, chip in enumerate(chips):
            copy(1 + j, (*chip, c), me).wait_recv()
            passed[j].start()
        copy(0, sibling, me).wait_recv()
        for j, chip in enumerate(chips):
            copy(4 + j, (*chip, 1 - c), me).wait_recv()
        for cp in first + passed:
            cp.wait_send()
        mine.wait()

    return pl.pallas_call(
        body, out_shape=SDS((N_DEV, r, n), blk.dtype),
        in_specs=[pl.BlockSpec(memory_space=pltpu.VMEM)], out_specs=pl.BlockSpec(memory_space=pltpu.VMEM),
        scratch_shapes=[pltpu.SemaphoreType.DMA((7,)), pltpu.SemaphoreType.DMA((7,)), pltpu.SemaphoreType.DMA],
        compiler_params=pltpu.CompilerParams(vmem_limit_bytes=VMEM_LIMIT), name=name)(blk)


_ANY = pl.BlockSpec(memory_space=pl.ANY)
_HBM = pl.BlockSpec(memory_space=pltpu.HBM)
_SEM = pl.BlockSpec(memory_space=pltpu.SEMAPHORE)
_EFFECT = pltpu.SideEffectType.DATAFLOW_SIDE_EFFECTING


def _hbm(a):
    return pltpu.with_memory_space_constraint(a, pltpu.HBM)


def _place_cast(name, j_arr, shard, kind, after):
    r, n = shard.shape
    tr = _tile(r, 256)
    nr = r // tr
    if kind == "col":
        out_shape, o_spec = (r, N_CHIP * n), pl.BlockSpec((tr, n), lambda i, j: (i, j[0]))
    else:
        out_shape, o_spec = (N_CHIP * r, n), pl.BlockSpec((tr, n), lambda i, j: (j[0] * nr + i, 0))

    def body(j_ref, s_ref, after_ref, o_ref):
        o_ref[...] = s_ref[...].astype(BF16)

    return pl.pallas_call(
        body,
        grid_spec=pltpu.PrefetchScalarGridSpec(
            num_scalar_prefetch=1, grid=(nr,), in_specs=[pl.BlockSpec((tr, n), lambda i, j: (i, 0)), _ANY],
            out_specs=o_spec),
        out_shape=SDS(out_shape, BF16), compiler_params=_params(("parallel",)), name=name)(j_arr, shard, after)


def _gather_start(name, fulls, kinds, groups, after):
    nw = len(fulls)
    ng = len(groups)

    def body(*refs):
        outs = refs[nw + 1:]
        send, recv, full, token = outs[:ng], outs[ng:2 * ng], outs[2 * ng:2 * ng + nw], outs[2 * ng + nw]
        x, y, c = _mesh_pos()
        for g, ws in enumerate(groups):
            for li, w in enumerate(ws):
                mine = _full_region(full[w], kinds[w], x, y, c)
                for k, chip in enumerate(_other_chips(x, y)):
                    pltpu.make_async_remote_copy(
                        src_ref=mine, dst_ref=mine, send_sem=send[g].at[3 * li + k], recv_sem=recv[g].at[3 * li + k],
                        device_id=(*chip, c), device_id_type=MESH).start()
        token[...] = jnp.zeros_like(token)

    sems = tuple(pltpu.SemaphoreType.DMA((3 * len(ws),)) for ws in groups)
    outs = pl.pallas_call(
        body,
        out_shape=sems + sems + tuple(pltpu.HBM(f_.shape, f_.dtype) for f_ in fulls) + (SDS((SUBLANES, LANES), F32),),
        in_specs=[_HBM] * nw + [_ANY],
        out_specs=tuple([_SEM] * (2 * ng) + [_HBM] * nw + [pl.BlockSpec(memory_space=pltpu.VMEM)]),
        input_output_aliases={w: 2 * ng + w for w in range(nw)},
        compiler_params=pltpu.CompilerParams(has_side_effects=_EFFECT),
        name=name,
    )(*[_hbm(f_) for f_ in fulls], after)
    return outs[:ng], outs[ng:2 * ng], outs[2 * ng:2 * ng + nw], outs[2 * ng + nw]


def _full_region(full, kind, px, py, half):
    j = 2 * px + py
    if kind == "col":
        rh, cols = full.shape[0] // 2, full.shape[1] // N_CHIP
        return full.at[pl.ds(half * rh, rh), pl.ds(j * cols, cols)]
    rows = full.shape[0] // N_CHIP
    rh = rows // 2
    return full.at[pl.ds(j * rows + half * rh, rh), :]


def _gather_pass(name, fulls, kinds, send, recv, after, thru):
    nw = len(fulls)

    def body(*refs):
        send_r, recv_r = refs[nw], refs[nw + 1]
        full, fsend, frecv = refs[nw + 4:2 * nw + 4], refs[2 * nw + 4], refs[2 * nw + 5]
        x, y, c = _mesh_pos()
        chips = _other_chips(x, y)
        for w in range(nw):
            for k, chip in enumerate(chips):
                landed = _full_region(full[w], kinds[w], *chip, c)
                arrive = pltpu.make_async_remote_copy(
                    src_ref=_full_region(full[w], kinds[w], x, y, c), dst_ref=landed, send_sem=send_r.at[3 * w + k],
                    recv_sem=recv_r.at[3 * w + k], device_id=(*chip, c), device_id_type=MESH)
                arrive.wait_recv()
                arrive.wait_send()
                pltpu.make_async_remote_copy(
                    src_ref=landed, dst_ref=landed, send_sem=fsend.at[3 * w + k], recv_sem=frecv.at[3 * w + k],
                    device_id=(x, y, 1 - c), device_id_type=MESH).start()

    sem = pltpu.SemaphoreType.DMA((3 * nw,))
    outs = pl.pallas_call(
        body,
        out_shape=tuple(pltpu.HBM(f_.shape, f_.dtype) for f_ in fulls) + (sem, sem, SDS(thru.shape, thru.dtype)),
        in_specs=[_HBM] * nw + [_SEM, _SEM, _ANY, _ANY],
        out_specs=tuple([_HBM] * nw + [_SEM, _SEM, _ANY]),
        input_output_aliases={**{w: w for w in range(nw)}, nw + 3: nw + 2},
        compiler_params=pltpu.CompilerParams(has_side_effects=_EFFECT),
        name=name,
    )(*fulls, send, recv, after, thru)
    return outs[:nw], outs[nw], outs[nw + 1], outs[nw + 2]


def _gather_finish(name, fulls, kinds, fsend, frecv, after):
    nw = len(fulls)

    def body(*refs):
        fsend_r, frecv_r = refs[nw], refs[nw + 1]
        full = refs[nw + 3:2 * nw + 3]
        x, y, c = _mesh_pos()
        chips = _other_chips(x, y)
        for w in range(nw):
            for k, chip in enumerate(chips):
                cp = pltpu.make_async_remote_copy(
                    src_ref=_full_region(full[w], kinds[w], *chip, c), dst_ref=_full_region(full[w], kinds[w], *chip, 1 - c),
                    send_sem=fsend_r.at[3 * w + k], recv_sem=frecv_r.at[3 * w + k],
                    device_id=(x, y, 1 - c), device_id_type=MESH)
                cp.wait_send()
                cp.wait_recv()

    outs = pl.pallas_call(
        body,
        out_shape=tuple(pltpu.HBM(f_.shape, f_.dtype) for f_ in fulls),
        in_specs=[_HBM] * nw + [_SEM, _SEM, _ANY],
        out_specs=tuple([_HBM] * nw),
        input_output_aliases={w: w for w in range(nw)},
        compiler_params=pltpu.CompilerParams(has_side_effects=_EFFECT),
        name=name,
    )(*fulls, fsend, frecv, after)
    return list(outs)


def _half_of(ref, kind, half):
    if kind == "col":
        rh = ref.shape[0] // 2
        return ref.at[pl.ds(half * rh, rh), :]
    rh = ref.shape[1] // 2
    return ref.at[:, pl.ds(half * rh, rh), :]


def _half_shape(g, kind):
    if kind == "col":
        return (g.shape[0] // 2, g.shape[1])
    return (g.shape[0], g.shape[1] // 2, g.shape[2])


def _plan_swap_half(kind):
    def plan(src, land, x, y, c):
        return [(_half_of(src, kind, 1 - c), land, (x, y, 1 - c))]
    return plan


def _plan_scatter(kind):
    def plan(src, land, x, y, c):
        out = []
        for k, (px, py) in enumerate(_other_chips(x, y)):
            j = 2 * px + py
            if kind == "col":
                n = src.shape[1] // N_CHIP
                blk = src.at[:, pl.ds(j * n, n)]
            else:
                blk = src.at[j]
            out.append((blk, land.at[k], (px, py, c)))
        return out
    return plan


def _plan_whole(src, land, x, y, c):
    return [(src, land, (x, y, 1 - c))]


def _split_start(name, src, land_shape, n, plan, thru):
    def body(src_in, land_in, thru_in, send, recv, src_ref, land_ref, thru_out):
        x, y, c = _mesh_pos()
        for k, (s_, d_, dev) in enumerate(plan(src_ref, land_ref, x, y, c)):
            pltpu.make_async_remote_copy(src_ref=s_, dst_ref=d_, send_sem=send.at[k], recv_sem=recv.at[k],
                                         device_id=dev, device_id_type=MESH).start()

    sem = pltpu.SemaphoreType.DMA((n,))
    return pl.pallas_call(
        body,
        out_shape=(sem, sem, pltpu.HBM(src.shape, src.dtype), pltpu.HBM(land_shape, src.dtype), SDS(thru.shape, thru.dtype)),
        in_specs=[_HBM, _HBM, _ANY], out_specs=(_SEM, _SEM, _HBM, _HBM, _ANY),
        input_output_aliases={0: 2, 1: 3, 2: 4},
        compiler_params=pltpu.CompilerParams(has_side_effects=_EFFECT), name=name,
    )(_hbm(src), _hbm(lax.empty(land_shape, src.dtype)), thru)


def _split_wait(name, send, recv, src, land, plan, after):
    def body(src_in, land_in, send_r, recv_r, after_r, src_ref, land_ref):
        x, y, c = _mesh_pos()
        for k, (s_, d_, dev) in enumerate(plan(src_ref, land_ref, x, y, c)):
            cp = pltpu.make_async_remote_copy(src_ref=s_, dst_ref=d_, send_sem=send_r.at[k], recv_sem=recv_r.at[k],
                                              device_id=dev, device_id_type=MESH)
            cp.wait_send()
            cp.wait_recv()

    return pl.pallas_call(
        body,
        out_shape=(pltpu.HBM(src.shape, src.dtype), pltpu.HBM(land.shape, land.dtype)),
        in_specs=[_HBM, _HBM, _SEM, _SEM, _ANY], out_specs=(_HBM, _HBM),
        input_output_aliases={0: 0, 1: 1},
        compiler_params=pltpu.CompilerParams(has_side_effects=_EFFECT), name=name,
    )(src, land, send, recv, after)


def _add_halves(name, c_arr, g, got, kind):
    if kind == "col":
        rh, n = got.shape
        tr = _tile(rh, 256)
        nh = rh // tr
        g_spec = pl.BlockSpec((tr, n), lambda i, c: (c[0] * nh + i, 0))
        o_spec = pl.BlockSpec((tr, n), lambda i, c: (i, 0))
        grid = (nh,)
    else:
        nc, rh, n = got.shape
        tr = _tile(rh, 256)
        nh = rh // tr
        g_spec = pl.BlockSpec((None, tr, n), lambda j, i, c: (j, c[0] * nh + i, 0))
        o_spec = pl.BlockSpec((None, tr, n), lambda j, i, c: (j, i, 0))
        grid = (nc, nh)

    def body(c_ref, g_ref, r_ref, o_ref):
        o_ref[...] = (g_ref[...].astype(F32) + r_ref[...].astype(F32)).astype(o_ref.dtype)

    return pl.pallas_call(
        body,
        grid_spec=pltpu.PrefetchScalarGridSpec(num_scalar_prefetch=1, grid=grid, in_specs=[g_spec, o_spec], out_specs=o_spec),
        out_shape=SDS(got.shape, got.dtype),
        compiler_params=_params(("parallel",) * len(grid)), name=name)(c_arr, g, got)


def _sum_partials(name, j_arr, part, got, kind):
    _, rh, n = got.shape
    tr = _tile(rh, 256)
    if kind == "col":
        p_spec = pl.BlockSpec((tr, n), lambda i, j: (i, j[0]))
    else:
        p_spec = pl.BlockSpec((None, tr, n), lambda i, j: (j[0], i, 0))

    def body(j_ref, p_ref, r_ref, o_ref):
        o_ref[...] = ((p_ref[...].astype(F32) + r_ref[0].astype(F32)) + r_ref[1].astype(F32)) + r_ref[2].astype(F32)

    return pl.pallas_call(
        body,
        grid_spec=pltpu.PrefetchScalarGridSpec(
            num_scalar_prefetch=1, grid=(rh // tr,),
            in_specs=[p_spec, pl.BlockSpec((3, tr, n), lambda i, j: (0, i, 0))],
            out_specs=pl.BlockSpec((tr, n), lambda i, j: (i, 0))),
        out_shape=SDS((rh, n), F32),
        compiler_params=_params(("parallel",)), name=name)(j_arr, part, got)


def _swap_reduced(name, halves):
    nw = len(halves)

    def body(*refs):
        h, got = refs[:nw], refs[nw:2 * nw]
        send_sems, recv_sems = refs[2 * nw:]
        x, y, c = _mesh_pos()
        cps = []
        for w in range(nw):
            cp = pltpu.make_async_remote_copy(
                src_ref=h[w], dst_ref=got[w], send_sem=send_sems.at[w], recv_sem=recv_sems.at[w],
                device_id=(x, y, 1 - c), device_id_type=MESH)
            cp.start()
            cps.append(cp)
        for cp in cps:
            cp.wait()

    return pl.pallas_call(
        body, out_shape=tuple(SDS(h.shape, h.dtype) for h in halves),
        in_specs=[_ANY] * nw, out_specs=tuple([_ANY] * nw),
        scratch_shapes=[pltpu.SemaphoreType.DMA((nw,)), pltpu.SemaphoreType.DMA((nw,))],
        name=name)(*halves)


def _pack(arrays):
    flat = [a.reshape(-1).astype(F32) for a in arrays]
    flat = [jnp.pad(f, (0, (-f.shape[0]) % LANES)) for f in flat]
    sizes = [f.shape[0] for f in flat]
    total = sum(sizes)
    rows = total // LANES
    tail = LANES * ((-rows) % (PACK_ROWS if rows > PACK_ROWS else SUBLANES))
    if tail:
        flat.append(jnp.zeros((tail,), F32))
    return jnp.concatenate(flat).reshape(-1, LANES), sizes


def _unpack(slab, sizes, shapes, lead=()):
    flat = slab.reshape(lead + (-1,))
    out, off = [], 0
    for sz, shp in zip(sizes, shapes):
        n = math.prod(shp)
        out.append(flat[..., off:off + n].reshape(lead + tuple(shp)))
        off += sz
    return out


def kernel(x, c, w_ada, b_ada, g_norm_mix, w_in, w_conv, b_conv, w_rg_a, b_rg_a, w_rg_x, b_rg_x, lru_lambda, g_attn_out, g_lru_out, w_out, g_norm_mlp, w_mlp_in, w_mlp_out, g_norm_final, loss_target, m_w_ada, m_b_ada, m_g_norm_mix, m_w_in, m_w_conv, m_b_conv, m_w_rg_a, m_b_rg_a, m_w_rg_x, m_b_rg_x, m_lru_lambda, m_g_attn_out, m_g_lru_out, m_w_out, m_g_norm_mlp, m_w_mlp_in, m_w_mlp_out, m_g_norm_final, v_w_ada, v_b_ada, v_g_norm_mix, v_w_in, v_w_conv, v_b_conv, v_w_rg_a, v_b_rg_a, v_w_rg_x, v_b_rg_x, v_lru_lambda, v_g_attn_out, v_g_lru_out, v_w_out, v_g_norm_mlp, v_w_mlp_in, v_w_mlp_out, v_g_norm_final):
    s, d = x.shape[1], x.shape[2]
    aw = d // 2
    nh = aw // HEAD
    f = w_mlp_out.shape[1] * N_CHIP
    n_ada = w_ada.shape[2]
    n_cv = w_conv.shape[2]
    ix, iy, ic = lax.axis_index("x"), lax.axis_index("y"), lax.axis_index("c")
    chip = 2 * ix + iy
    me = 2 * chip + ic
    c_arr = jnp.reshape(ic, (1,)).astype(jnp.int32)
    j_arr = jnp.reshape(chip, (1,)).astype(jnp.int32)

    x2d, tgt = x[0], loss_target[0]

    slab, sizes = _pack([c, w_conv])
    gathered = _all_gather_small("comm_gather_cond", slab)
    c_parts, cv_parts = _unpack(gathered, sizes, [(d,), (CONV_TAPS, n_cv)], lead=(N_DEV,))
    c_all = c_parts
    w_conv_full = jnp.concatenate([cv_parts[2 * j] for j in range(N_CHIP)], axis=-1)
    b_sh = lax.dynamic_slice(b_ada, (0, chip * n_ada), (1, n_ada))
    mod_part, act_all = _ada_mod(c_all, w_ada[0], b_sh)
    mod_g = _all_gather_small("comm_gather_mod", mod_part.reshape(-1, LANES))
    mod_g = mod_g.reshape(N_DEV, N_DEV, n_ada)
    mod = jnp.concatenate([lax.dynamic_index_in_dim(mod_g[2 * j], me, 0, keepdims=True) for j in range(N_CHIP)], axis=-1)
    sh1, sc1, gt1, sh2, sc2, gt2 = [mod[:, k * d:(k + 1) * d] for k in range(N_MOD)]

    kinds = ("col", "row", "col", "row")
    groups = ((0,), (1, 2), (3,))
    big_w = (w_in, w_out, w_mlp_in, w_mlp_out)
    placed = _place_cast("place_cast_0", j_arr, w_in[0], kinds[0], mod)
    send_a, recv_a, full_a, token = _gather_start("gather_start_a", [placed], kinds[:1], groups[:1], mod)
    placed = [_place_cast("place_cast_%d" % i, j_arr, big_w[i][0], kinds[i], token) for i in (1, 2, 3)]
    send_b, recv_b, full_b, _ = _gather_start("gather_start_b", placed, kinds[1:], ((0, 1), (2,)), token)
    ag_send, ag_recv, ag_full = send_a + send_b, recv_a + recv_b, full_a + full_b

    def gather_pass(g, after, thru):
        ws = groups[g]
        return _gather_pass("gather_pass_%d" % g, [ag_full[w] for w in ws], [kinds[w] for w in ws],
                            ag_send[g], ag_recv[g], after, thru)

    fl, fs, fr, sh1 = gather_pass(0, mod, sh1)
    h1, rstd1 = _norm_mod_fwd("norm_mod_fwd1", x2d, g_norm_mix, sc1, sh1)
    (w_in_f,) = _gather_finish("gather_finish_0", fl, kinds[0:1], fs, fr, h1)
    (qkv,) = _matmul("mm_qkv", h1, w_in_f, "nn", s, 3 * aw, d, (BF16,))
    (xrg,) = _matmul("mm_xrg", h1, w_in_f, "nn", s, 2 * aw, d, (F32,), b_off=3 * aw)
    o_attn = _attn_fwd(qkv, nh)
    fl, fs, fr, xrg = gather_pass(1, o_attn, xrg)
    wa3, wx3 = w_rg_a[0], w_rg_x[0]
    o_lru, hseq = _lru_fwd(xrg, w_conv_full, b_conv, wa3, b_rg_a, wx3, b_rg_x, lru_lambda)
    mixed, rstd_a, rstd_l = _mix_norm_fwd(o_attn, o_lru, g_attn_out, g_lru_out)
    w_out_f, w_mi_f = _gather_finish("gather_finish_1", fl, kinds[1:3], fs, fr, mixed)

    def residual(acc, xin, gt):
        return acc, xin + gt * acc

    y1, x1 = _matmul("mm_out", mixed, w_out_f, "nn", s, d, d, (F32, F32), extras=(x2d, gt1),
                     extra_kinds=("tile", "row"), epilogue=residual)
    h2, rstd2 = _norm_mod_fwd("norm_mod_fwd2", x1, g_norm_mlp, sc2, sh2)
    fl, fs, fr, h2 = gather_pass(2, x1, h2)

    def sq_relu(acc):
        r = jnp.maximum(acc, 0.0)
        return 2.0 * r, r * r

    r2, hid = _matmul("mm_mlp_in", h2, w_mi_f, "nn", s, f, d, (BF16, BF16), epilogue=sq_relu)
    (w_mo_f,) = _gather_finish("gather_finish_2", fl, kinds[3:4], fs, fr, hid)
    y2, x2 = _matmul("mm_mlp_out", hid, w_mo_f, "nn", s, d, f, (F32, F32), extras=(x1, gt2),
                     extra_kinds=("tile", "row"), epilogue=residual)
    dx2, loss_row, dg_final = _final_loss(x2, g_norm_final.reshape(1, d), tgt)

    def rs_begin(tag, g, kind, thru):
        send, recv, g, land, thru = _split_start("rs_swap_start_" + tag, g, _half_shape(g, kind), 1,
                                                 _plan_swap_half(kind), thru)
        return {"tag": tag, "kind": kind, "swap": (send, recv, g, land)}, thru

    def rs_mid(st, after, thru):
        tag, kind = st["tag"], st["kind"]
        g, got = _split_wait("rs_swap_wait_" + tag, *st["swap"], _plan_swap_half(kind), after)
        part = _add_halves("add_halves_" + tag, c_arr, g, got, kind)
        blk = (part.shape[0], part.shape[1] // N_CHIP) if kind == "col" else part.shape[1:]
        send, recv, part, land, thru = _split_start("rs_scatter_start_" + tag, part, (N_CHIP - 1,) + blk, N_CHIP - 1,
                                                    _plan_scatter(kind), thru)
        st["scatter"] = (send, recv, part, land)
        return thru

    def rs_end(st, after):
        tag, kind = st["tag"], st["kind"]
        part, got = _split_wait("rs_scatter_wait_" + tag, *st["scatter"], _plan_scatter(kind), after)
        return _sum_partials("sum_partials_" + tag, j_arr, part, got, kind)

    dy2, dgt2 = _gate_bwd("gate_bwd2", dx2, y2, gt2)
    (dpre,) = _matmul("mm_dhid", dy2, w_mo_f, "nt", s, f, d, (BF16,), extras=(r2,), extra_kinds=("tile",),
                      epilogue=lambda acc, r: (acc * r.astype(F32),))
    (g_mo,) = _matmul("mm_dw_mlp_out", hid, dy2, "tn", f, d, s, (BF16,))
    st_mo, dpre = rs_begin("mo", g_mo.reshape(N_CHIP, f // N_CHIP, d), "row", dpre)
    (dh2,) = _matmul("mm_dh2", dpre, w_mi_f, "nt", s, d, f, (F32,))
    (g_mi,) = _matmul("mm_dw_mlp_in", h2, dpre, "tn", d, f, s, (BF16,))
    dh2 = rs_mid(st_mo, g_mi, dh2)
    st_mi, dh2 = rs_begin("mi", g_mi, "col", dh2)
    dx1, dsh2, dsc2, dg_mlp = _norm_mod_bwd("norm_mod_bwd2", dh2, x1, rstd2, g_norm_mlp, sc2, dx2)
    dy1, dgt1 = _gate_bwd("gate_bwd1", dx1, y1, gt1)
    (dmixed,) = _matmul("mm_dmixed", dy1, w_out_f, "nt", s, d, d, (F32,))
    (g_out,) = _matmul("mm_dw_out", mixed, dy1, "tn", d, d, s, (BF16,))
    dmixed = rs_mid(st_mi, g_out, dmixed)
    st_out, dmixed = rs_begin("out", g_out.reshape(N_CHIP, d // N_CHIP, d), "row", dmixed)
    do_attn, do_lru, dg_attn, dg_lru = _mix_norm_bwd(dmixed, o_attn, o_lru, rstd_a, rstd_l, g_attn_out, g_lru_out)
    dq, dk, dv = _attn_bwd(qkv, do_attn, nh)
    do_lru = rs_mid(st_out, dq, do_lru)
    dxr, dxg, dwconv, dbconv, dwa, dba, dwx, dbx, dlam = _lru_bwd(
        xrg, do_lru, hseq, w_conv_full, b_conv, wa3, b_rg_a, wx3, b_rg_x, lru_lambda)
    dproj = jnp.concatenate([dq, dk, dv, dxr, dxg], axis=-1)
    (dh1,) = _matmul("mm_dh1", dproj, w_in_f, "nt", s, d, 5 * aw, (F32,))
    (g_in,) = _matmul("mm_dw_in", h1, dproj, "tn", d, 5 * aw, s, (BF16,))
    st_in, dh1 = rs_begin("in", g_in, "col", dh1)
    grad_x, dsh1, dsc1, dg_mix = _norm_mod_bwd("norm_mod_bwd1", dh1, x2d, rstd1, g_norm_mix, sc1, dx1)

    dmod = jnp.concatenate([dsh1, dsc1, dgt1, dsh2, dsc2, dgt2], axis=-1)
    small_names = ["b_ada", "g_norm_mix", "b_conv", "w_rg_a", "b_rg_a", "w_rg_x", "b_rg_x", "lru_lambda",
                   "g_attn_out", "g_lru_out", "g_norm_mlp", "g_norm_final"]
    small_g = [dmod, dg_mix, dbconv, dwa, dba, dwx, dbx, dlam, dg_attn, dg_lru, dg_mlp, dg_final]
    small_w = [b_ada, g_norm_mix, b_conv, w_rg_a, b_rg_a, w_rg_x, b_rg_x, lru_lambda, g_attn_out, g_lru_out, g_norm_mlp, g_norm_final]
    small_m = [m_b_ada, m_g_norm_mix, m_b_conv, m_w_rg_a, m_b_rg_a, m_w_rg_x, m_b_rg_x, m_lru_lambda, m_g_attn_out, m_g_lru_out, m_g_norm_mlp, m_g_norm_final]
    small_v = [v_b_ada, v_g_norm_mix, v_b_conv, v_w_rg_a, v_b_rg_a, v_w_rg_x, v_b_rg_x, v_lru_lambda, v_g_attn_out, v_g_lru_out, v_g_norm_mlp, v_g_norm_final]
    extra_zero = [jnp.zeros_like(dwconv), jnp.zeros((LANES,), F32)]
    g_slab, g_sizes = _pack(small_g + [dwconv, loss_row])
    w_slab, _ = _pack(small_w + extra_zero)
    m_slab, _ = _pack(small_m + extra_zero)
    v_slab, _ = _pack(small_v + extra_zero)
    g_all = _all_gather_small("comm_gather_small_grads", g_slab)
    gs_slab, ds_slab, ms_slab, vs_slab = _small_reduce_adamw(g_all, w_slab, m_slab, v_slab)
    shapes = [w.shape for w in small_w] + [dwconv.shape, (LANES,)]
    gs = _unpack(gs_slab, g_sizes, shapes)
    ds = _unpack(ds_slab, g_sizes, shapes)
    ms = _unpack(ms_slab, g_sizes, shapes)
    vs = _unpack(vs_slab, g_sizes, shapes)
    small = {n: (gs[i], ds[i], ms[i], vs[i]) for i, n in enumerate(small_names)}
    loss = gs[-1][0]
    g_wconv = lax.dynamic_slice(gs[-2], (0, chip * n_cv), (CONV_TAPS, n_cv))
    d_wconv, m_wconv, v_wconv = _adamw_plain("adamw_conv", w_conv[0], g_wconv, m_w_conv[0], v_w_conv[0])
    small["w_conv"] = (g_wconv[None], d_wconv[None], m_wconv[None], v_wconv[None])

    dmod_all = _unpack(g_all, g_sizes, [(N_MOD * d,)], lead=(N_DEV,))[0]
    dmod_sel = lax.dynamic_slice(dmod_all, (0, chip * n_ada), (N_DEV, n_ada)).astype(BF16)
    act_t = act_all.T.astype(BF16)
    dmod_sel = rs_mid(st_in, gs_slab, dmod_sel)
    big = {"w_ada": _adamw_ada(w_ada[0], m_w_ada[0], v_w_ada[0], act_t, dmod_sel)}

    half_mo = rs_end(st_mo, big["w_ada"][1])
    half_mi = rs_end(st_mi, half_mo)
    half_out = rs_end(st_out, half_mi)
    got_out, got_mi, got_mo = _swap_reduced("comm_swap_reduced_a", [half_out, half_mi, half_mo])
    big["w_out"] = _adamw_halves("adamw_w_out", c_arr, w_out[0], m_w_out[0], v_w_out[0], half_out, got_out)
    big["w_mlp_in"] = _adamw_halves("adamw_w_mlp_in", c_arr, w_mlp_in[0], m_w_mlp_in[0], v_w_mlp_in[0], half_mi, got_mi)
    big["w_mlp_out"] = _adamw_halves("adamw_w_mlp_out", c_arr, w_mlp_out[0], m_w_mlp_out[0], v_w_mlp_out[0], half_mo, got_mo)
    half_in = rs_end(st_in, big["w_mlp_out"][1])
    (got_in,) = _swap_reduced("comm_swap_reduced_b", [half_in])
    big["w_in"] = _adamw_halves("adamw_w_in", c_arr, w_in[0], m_w_in[0], v_w_in[0], half_in, got_in)

    order = ["w_ada", "b_ada", "g_norm_mix", "w_in", "w_conv", "b_conv", "w_rg_a", "b_rg_a", "w_rg_x", "b_rg_x",
             "lru_lambda", "g_attn_out", "g_lru_out", "w_out", "g_norm_mlp", "w_mlp_in", "w_mlp_out", "g_norm_final"]
    res = {}
    for n in order:
        res[n] = tuple(t[None] for t in big[n]) if n in big else small[n]
    return (loss, grad_x[None],
            *[res[n][0] for n in order], *[res[n][1] for n in order],
            *[res[n][2] for n in order], *[res[n][3] for n in order])
```

```python
import functools
import math

import jax
import jax.numpy as jnp
from jax import lax
from jax.experimental import pallas as pl
from jax.experimental.pallas import tpu as pltpu

F32 = jnp.float32
BF16 = jnp.bfloat16
SDS = jax.ShapeDtypeStruct
MESH = pl.DeviceIdType.MESH

EPS = 1e-6
HEAD = 128
N_MOD = 6
CONV_TAPS = 4
LRU_C = 8.0
ADAM_LR, ADAM_B1, ADAM_B2, ADAM_EPS, ADAM_WD, ADAM_STEP = 0.001, 0.9, 0.999, 1e-08, 0.01, 10
N_DEV = 8
N_CHIP = 4
LANES = 128
SUBLANES = 8
VMEM_LIMIT = 56 * 1024 * 1024
PACK_ROWS = 256
MM_TILE_M, MM_TILE_N, MM_TILE_K = 1024, 1024, 2048


def _tile(dim, pref):
    t = min(dim, pref)
    while dim % t:
        t -= LANES
    return t


def _params(sem=None):
    return pltpu.CompilerParams(dimension_semantics=sem, vmem_limit_bytes=VMEM_LIMIT)


def _sigmoid(x):
    return 1.0 / (1.0 + jnp.exp(-x))


def _log_sigmoid(x):
    return jnp.minimum(x, 0.0) - jnp.log(1.0 + jnp.exp(-jnp.abs(x)))


def _gelu_parts(x):
    k0, k1 = math.sqrt(2.0 / math.pi), 0.044715
    t = jnp.tanh(k0 * (x + k1 * x * x * x))
    val = 0.5 * x * (1.0 + t)
    der = 0.5 * (1.0 + t) + 0.5 * x * (1.0 - t * t) * k0 * (1.0 + 3.0 * k1 * x * x)
    return val, der


def _dot(a, b):
    return jnp.dot(a, b, preferred_element_type=F32)


def _dot_nt(a, b):
    return lax.dot_general(a, b, (((1,), (1,)), ((), ())), preferred_element_type=F32)


def _dot_tn(a, b):
    return lax.dot_general(a, b, (((0,), (0,)), ((), ())), preferred_element_type=F32)


def _split_dot(x, tri):
    hi = x.astype(BF16)
    lo = (x - hi.astype(F32)).astype(BF16)
    return _dot(hi, tri) + _dot(lo, tri)


def _matmul(name, a, b, mode, m, n, k, out_dtypes, *, b_off=0, extras=(), extra_kinds=(), epilogue=None,
            tm=MM_TILE_M, tn=MM_TILE_N, tk=MM_TILE_K):
    tm, tn, tk = _tile(m, tm), _tile(math.gcd(n, b_off) if b_off else n, tn), _tile(k, tk)
    assert b_off % tn == 0
    nk = k // tk
    n_ex, n_out = len(extras), len(out_dtypes)
    dot = {"nn": _dot, "nt": _dot_nt, "tn": _dot_tn}[mode]

    def body(a_ref, b_ref, *rest):
        ex, outs = rest[:n_ex], rest[n_ex:n_ex + n_out]

        def finish(total):
            res = epilogue(total, *[e[...] for e in ex]) if epilogue else (total,)
            for o, r in zip(outs, res):
                o[...] = r.astype(o.dtype)

        if nk == 1:
            finish(dot(a_ref[...], b_ref[...]))
            return
        acc = rest[-1]
        kk = pl.program_id(2)

        @pl.when(kk == 0)
        def _():
            acc[...] = dot(a_ref[...], b_ref[...])

        @pl.when(jnp.logical_and(kk > 0, kk < nk - 1))
        def _():
            acc[...] += dot(a_ref[...], b_ref[...])

        @pl.when(kk == nk - 1)
        def _():
            finish(acc[...] + dot(a_ref[...], b_ref[...]))

    if mode == "nn":
        a_spec = pl.BlockSpec((tm, tk), lambda i, j, kk: (i, kk))
        b_spec = pl.BlockSpec((tk, tn), lambda i, j, kk: (kk, j + b_off // tn))
    elif mode == "nt":
        a_spec = pl.BlockSpec((tm, tk), lambda i, j, kk: (i, kk))
        b_spec = pl.BlockSpec((tn, tk), lambda i, j, kk: (j, kk + b_off // tk))
    else:
        a_spec = pl.BlockSpec((tk, tm), lambda i, j, kk: (kk, i))
        b_spec = pl.BlockSpec((tk, tn), lambda i, j, kk: (kk, j))
    tile_spec = pl.BlockSpec((tm, tn), lambda i, j, kk: (i, j))
    row_spec = pl.BlockSpec((1, tn), lambda i, j, kk: (0, j))
    ex_specs = [tile_spec if kind == "tile" else row_spec for kind in extra_kinds]
    outs = pl.pallas_call(
        body, grid=(m // tm, n // tn, nk),
        in_specs=[a_spec, b_spec] + ex_specs,
        out_specs=tuple(tile_spec for _ in out_dtypes),
        out_shape=tuple(SDS((m, n), dt) for dt in out_dtypes),
        scratch_shapes=[pltpu.VMEM((tm, tn), F32)] if nk > 1 else [],
        compiler_params=_params(("parallel", "parallel", "arbitrary")),
        name=name,
    )(a, b, *extras)
    return outs


def _row_specs(s, d, tr):
    row = pl.BlockSpec((tr, d), lambda i: (i, 0))
    vec = pl.BlockSpec((1, d), lambda i: (0, 0))
    col = pl.BlockSpec((tr, 1), lambda i: (i, 0))
    return row, vec, col


def _norm_mod_fwd(name, x, g, sc, sh):
    s, d = x.shape
    tr = _tile(s, 256)
    row, vec, col = _row_specs(s, d, tr)

    def body(x_ref, g_ref, sc_ref, sh_ref, h_ref, r_ref):
        xv = x_ref[...]
        r = lax.rsqrt(jnp.mean(xv * xv, axis=-1, keepdims=True) + EPS)
        h_ref[...] = ((xv * r * g_ref[...]) * (1.0 + sc_ref[...]) + sh_ref[...]).astype(BF16)
        r_ref[...] = r

    return pl.pallas_call(
        body, grid=(s // tr,), in_specs=[row, vec, vec, vec], out_specs=(row, col),
        out_shape=(SDS((s, d), BF16), SDS((s, 1), F32)),
        compiler_params=_params(("parallel",)), name=name)(x, g, sc, sh)


def _norm_mod_bwd(name, dh, xin, rstd, g, sc, dres, gate=None):
    s, d = xin.shape
    tr = _tile(s, 256)
    row, vec, col = _row_specs(s, d, tr)

    n_gate = 2 if gate is not None else 0

    def body(dh_ref, x_ref, r_ref, g_ref, sc_ref, dres_ref, *rest):
        gate_in, gate_out = rest[:n_gate], rest[n_gate + 4:]
        dx_ref, dsh_ref, dsc_ref, dg_ref = rest[n_gate:n_gate + 4]

        @pl.when(pl.program_id(0) == 0)
        def _():
            for ref in (dsh_ref, dsc_ref, dg_ref) + tuple(gate_out[1:]):
                ref[...] = jnp.zeros_like(ref)

        dh_v, xv, r, gv = dh_ref[...], x_ref[...], r_ref[...], g_ref[...]
        n0 = xv * r
        dsh_ref[...] += jnp.sum(dh_v, axis=0, keepdims=True)
        dsc_ref[...] += jnp.sum(dh_v * (n0 * gv), axis=0, keepdims=True)
        dn = dh_v * (1.0 + sc_ref[...])
        dg_ref[...] += jnp.sum(dn * n0, axis=0, keepdims=True)
        gy = dn * gv
        dot = jnp.mean(gy * xv, axis=-1, keepdims=True)
        dxv = dres_ref[...] + r * gy - xv * (r * r * r * dot)
        dx_ref[...] = dxv
        if gate is not None:
            y_ref, gt_ref = gate_in
            dy_ref, dgt_ref = gate_out
            dy_ref[...] = (gt_ref[...] * dxv).astype(BF16)
            dgt_ref[...] += jnp.sum(dxv * y_ref[...], axis=0, keepdims=True)

    vecs = SDS((1, d), F32)
    gate_args = tuple(gate) if gate is not None else ()
    return pl.pallas_call(
        body, grid=(s // tr,),
        in_specs=[row, row, col, vec, vec, row] + ([row, vec] if gate is not None else []),
        out_specs=(row, vec, vec, vec) + ((row, vec) if gate is not None else ()),
        out_shape=(SDS((s, d), F32), vecs, vecs, vecs) + ((SDS((s, d), BF16), vecs) if gate is not None else ()),
        compiler_params=_params(("arbitrary",)), name=name)(dh, xin, rstd, g, sc, dres, *gate_args)


def _final_loss(x2, gf, tgt, y, gt):
    s, d = x2.shape
    tr = _tile(s, 256)
    row, vec, _ = _row_specs(s, d, tr)
    lrow = pl.BlockSpec((1, LANES), lambda i: (0, 0))

    def body(x_ref, g_ref, t_ref, y_ref, gt_ref, dx_ref, loss_ref, dg_ref, dy_ref, dgt_ref):
        @pl.when(pl.program_id(0) == 0)
        def _():
            loss_ref[...] = jnp.zeros_like(loss_ref)
            dg_ref[...] = jnp.zeros_like(dg_ref)
            dgt_ref[...] = jnp.zeros_like(dgt_ref)

        xv, gv = x_ref[...], g_ref[...]
        r = lax.rsqrt(jnp.mean(xv * xv, axis=-1, keepdims=True) + EPS)
        n0 = xv * r
        err = n0 * gv - t_ref[...]
        loss_ref[...] += jnp.sum(err * err) * (0.5 / d)
        dy = err * (1.0 / d)
        dg_ref[...] += jnp.sum(dy * n0, axis=0, keepdims=True)
        gy = dy * gv
        dot = jnp.mean(gy * xv, axis=-1, keepdims=True)
        dxv = r * gy - xv * (r * r * r * dot)
        dx_ref[...] = dxv
        dy_ref[...] = (gt_ref[...] * dxv).astype(BF16)
        dgt_ref[...] += jnp.sum(dxv * y_ref[...], axis=0, keepdims=True)

    return pl.pallas_call(
        body, grid=(s // tr,), in_specs=[row, vec, row, row, vec], out_specs=(row, lrow, vec, row, vec),
        out_shape=(SDS((s, d), F32), SDS((1, LANES), F32), SDS((1, d), F32), SDS((s, d), BF16), SDS((1, d), F32)),
        compiler_params=_params(("arbitrary",)), name="final_loss")(x2, gf, tgt, y, gt)


def _mix_norm_fwd(oa, ol, ga, gl):
    s, w = oa.shape
    tr = _tile(s, 256)
    row, vec, col = _row_specs(s, w, tr)

    def body(oa_ref, ol_ref, ga_ref, gl_ref, mx_ref, ra_ref, rl_ref):
        a, l = oa_ref[...], ol_ref[...]
        ra = lax.rsqrt(jnp.mean(a * a, axis=-1, keepdims=True) + EPS)
        rl = lax.rsqrt(jnp.mean(l * l, axis=-1, keepdims=True) + EPS)
        mx_ref[:, :w] = (a * ra * ga_ref[...]).astype(BF16)
        mx_ref[:, w:] = (l * rl * gl_ref[...]).astype(BF16)
        ra_ref[...] = ra
        rl_ref[...] = rl

    return pl.pallas_call(
        body, grid=(s // tr,), in_specs=[row, row, vec, vec],
        out_specs=(pl.BlockSpec((tr, 2 * w), lambda i: (i, 0)), col, col),
        out_shape=(SDS((s, 2 * w), BF16), SDS((s, 1), F32), SDS((s, 1), F32)),
        compiler_params=_params(("parallel",)), name="mix_norm_fwd")(oa, ol, ga, gl)


def _mix_norm_bwd(dmx, oa, ol, ra, rl, ga, gl):
    s, w = oa.shape
    tr = _tile(s, 256)
    row, vec, col = _row_specs(s, w, tr)

    def body(dm_ref, oa_ref, ol_ref, ra_ref, rl_ref, ga_ref, gl_ref, doa_ref, dol_ref, dga_ref, dgl_ref):
        @pl.when(pl.program_id(0) == 0)
        def _():
            dga_ref[...] = jnp.zeros_like(dga_ref)
            dgl_ref[...] = jnp.zeros_like(dgl_ref)

        def one(dy, xv, r, gv, dg_ref):
            dg_ref[...] += jnp.sum(dy * (xv * r), axis=0, keepdims=True)
            gy = dy * gv
            dot = jnp.mean(gy * xv, axis=-1, keepdims=True)
            return r * gy - xv * (r * r * r * dot)

        doa_ref[...] = one(dm_ref[:, :w], oa_ref[...], ra_ref[...], ga_ref[...], dga_ref).astype(BF16)
        dol_ref[...] = one(dm_ref[:, w:], ol_ref[...], rl_ref[...], gl_ref[...], dgl_ref)

    return pl.pallas_call(
        body, grid=(s // tr,),
        in_specs=[pl.BlockSpec((tr, 2 * w), lambda i: (i, 0)), row, row, col, col, vec, vec],
        out_specs=(row, row, vec, vec),
        out_shape=(SDS((s, w), BF16), SDS((s, w), F32), SDS((1, w), F32), SDS((1, w), F32)),
        compiler_params=_params(("arbitrary",)), name="mix_norm_bwd")(dmx, oa, ol, ra, rl, ga, gl)


def _attn_blocks(qs, ks, tri_after, csums, causal):
    zs = [_dot_nt(q, k) * (HEAD ** -0.5) for q, k in zip(qs, ks)]
    lbs = [_log_sigmoid(z) for z in zs]
    lss = [lb - z for lb, z in zip(lbs, zs)]
    if causal is not None:
        lss = [jnp.where(causal, ls, 0.0) for ls in lss]
    locs = [_split_dot(ls, tri_after) for ls in lss]
    ws = [jnp.exp(lb + (loc + cs)) for lb, loc, cs in zip(lbs, locs, csums)]
    if causal is not None:
        ws = [jnp.where(causal, w, 0.0) for w in ws]
    nxt = [cs + (loc[:, 0:1] + ls[:, 0:1]) for cs, loc, ls in zip(csums, locs, lss)]
    return lbs, ws, nxt


ATTN_HEADS_PER_STEP = 4


def _attn_tile(s):
    return 256 if s >= 1024 else 128


def _tri(t, after):
    r_i = lax.broadcasted_iota(jnp.int32, (t, t), 0)
    c_i = lax.broadcasted_iota(jnp.int32, (t, t), 1)
    return ((r_i > c_i) if after else (r_i < c_i)).astype(BF16)


def _attn_fwd(qkv, n_heads):
    s = qkv.shape[0]
    t = _attn_tile(s)
    hps = ATTN_HEADS_PER_STEP
    wid = hps * HEAD

    def body(q_ref, k_ref, v_ref, o_ref):
        qi = pl.program_id(1)
        tri_after = _tri(t, True)
        causal = lax.broadcasted_iota(jnp.int32, (t, t), 1) < lax.broadcasted_iota(jnp.int32, (t, t), 0)
        lanes = [slice(a * HEAD, (a + 1) * HEAD) for a in range(hps)]
        qs = [q_ref[:, ln] for ln in lanes]

        def block(off, carry, mask):
            ks = [k_ref[pl.ds(off, t), ln] for ln in lanes]
            _, ws, csums = _attn_blocks(qs, ks, tri_after, [cr[0] for cr in carry], mask)
            os_ = [cr[1] + _dot(w.astype(BF16), v_ref[pl.ds(off, t), ln]) for cr, w, ln in zip(carry, ws, lanes)]
            return tuple(zip(csums, os_))

        zero = tuple((jnp.zeros((t, 1), F32), jnp.zeros((t, HEAD), F32)) for _ in lanes)
        carry = block(pl.multiple_of(qi * t, t), zero, causal)
        carry = lax.fori_loop(1, qi + 1, lambda it, cr: block(pl.multiple_of((qi - it) * t, t), cr, None), carry)
        for a, ln in enumerate(lanes):
            o_ref[:, ln] = carry[a][1]

    hb = n_heads // hps
    return pl.pallas_call(
        body, grid=(hb, s // t),
        in_specs=[pl.BlockSpec((t, wid), lambda hh, i: (i, hh)),
                  pl.BlockSpec((s, wid), lambda hh, i: (0, hb + hh)),
                  pl.BlockSpec((s, wid), lambda hh, i: (0, 2 * hb + hh))],
        out_specs=pl.BlockSpec((t, wid), lambda hh, i: (i, hh)),
        out_shape=SDS((s, n_heads * HEAD), F32),
        compiler_params=_params(("parallel", "parallel")), name="attn_fwd")(qkv, qkv, qkv)


def _attn_bwd(qkv, do, n_heads):
    s = qkv.shape[0]
    t = _attn_tile(s)
    nq = s // t
    scale = HEAD ** -0.5
    hps = ATTN_HEADS_PER_STEP
    wid = hps * HEAD

    def body(q_ref, k_ref, v_ref, do_ref, dq_ref, dk_ref, dv_ref, e_s, sg_s, dk_acc, dv_acc):
        qi = pl.program_id(1)

        @pl.when(qi == 0)
        def _():
            dk_acc[...] = jnp.zeros_like(dk_acc)
            dv_acc[...] = jnp.zeros_like(dv_acc)

        tri_after = _tri(t, True)
        tri_before = _tri(t, False)
        causal = lax.broadcasted_iota(jnp.int32, (t, t), 1) < lax.broadcasted_iota(jnp.int32, (t, t), 0)
        lanes = [slice(a * HEAD, (a + 1) * HEAD) for a in range(hps)]
        qs = [q_ref[:, ln] for ln in lanes]
        douts = [do_ref[:, ln] for ln in lanes]

        def pass1(kb, csums, mask):
            off = pl.multiple_of(kb * t, t)
            ks = [k_ref[pl.ds(off, t), ln] for ln in lanes]
            dws = [_dot_nt(dout, v_ref[pl.ds(off, t), ln]) for dout, ln in zip(douts, lanes)]
            lbs, ws, nxt = _attn_blocks(qs, ks, tri_after, csums, mask)
            for a, ln in enumerate(lanes):
                e_s[a, kb] = dws[a] * ws[a]
                sg_s[a, kb] = jnp.exp(lbs[a])
            for a, ln in enumerate(lanes):
                dv_acc[pl.ds(off, t), ln] += _dot_tn(ws[a].astype(BF16), douts[a])
            return tuple(nxt)

        cs = pass1(qi, tuple(jnp.zeros((t, 1), F32) for _ in lanes), causal)
        lax.fori_loop(1, qi + 1, lambda it, c_: pass1(qi - it, c_, None), cs)

        def pass2(kb, carry, mask):
            off = pl.multiple_of(kb * t, t)
            es = [e_s[a, kb] for a in range(hps)]
            locs = [_split_dot(e, tri_before) for e in es]
            stays = [(loc + cr[0]) * sg_s[a, kb] for a, (loc, cr) in enumerate(zip(locs, carry))]
            if mask is not None:
                stays = [jnp.where(mask, st, 0.0) for st in stays]
            dzbs = [((e * (1.0 - sg_s[a, kb]) - st) * scale).astype(BF16) for a, (e, st) in enumerate(zip(es, stays))]
            dqs = [cr[1] + _dot(dzb, k_ref[pl.ds(off, t), ln]) for cr, dzb, ln in zip(carry, dzbs, lanes)]
            for a, ln in enumerate(lanes):
                dk_acc[pl.ds(off, t), ln] += _dot_tn(dzbs[a], qs[a])
            esums = [cr[0] + (loc[:, t - 1:t] + e[:, t - 1:t]) for cr, loc, e in zip(carry, locs, es)]
            return tuple(zip(esums, dqs))

        zero = tuple((jnp.zeros((t, 1), F32), jnp.zeros((t, HEAD), F32)) for _ in lanes)
        carry = lax.fori_loop(0, qi, lambda kb, cr: pass2(kb, cr, None), zero)
        carry = pass2(qi, carry, causal)
        for a, ln in enumerate(lanes):
            dq_ref[:, ln] = carry[a][1].astype(BF16)

        @pl.when(qi == nq - 1)
        def _():
            dk_ref[...] = dk_acc[...].astype(BF16)
            dv_ref[...] = dv_acc[...].astype(BF16)

    hb = n_heads // hps
    blk = pl.BlockSpec((t, wid), lambda hh, i: (i, hh))
    full = pl.BlockSpec((s, wid), lambda hh, i: (0, hh))
    return pl.pallas_call(
        body, grid=(hb, nq),
        in_specs=[blk,
                  pl.BlockSpec((s, wid), lambda hh, i: (0, hb + hh)),
                  pl.BlockSpec((s, wid), lambda hh, i: (0, 2 * hb + hh)),
                  blk],
        out_specs=(blk, full, full),
        out_shape=(SDS((s, n_heads * HEAD), BF16),) * 3,
        scratch_shapes=[pltpu.VMEM((hps, nq, t, t), F32), pltpu.VMEM((hps, nq, t, t), F32),
                        pltpu.VMEM((s, wid), F32), pltpu.VMEM((s, wid), F32)],
        compiler_params=_params(("parallel", "arbitrary")), name="attn_bwd")(qkv, qkv, qkv, do)


def _lru_chunk(s):
    return 256 if s >= 1024 else 128


def _lru_gates(xc, wa, ba, wx, bx, sp):
    xb = xc.astype(BF16)
    r = _sigmoid(_dot(xb, wa) + ba)
    ig = _sigmoid(_dot(xb, wx) + bx)
    la = -LRU_C * r * sp
    a = jnp.exp(la)
    t = jnp.tanh(la)
    mult = jnp.sqrt(-2.0 * t / (1.0 - t))
    return r, ig, a, mult


def _softplus_neg(lam):
    return jnp.maximum(-lam, 0.0) + jnp.log(1.0 + jnp.exp(-jnp.abs(lam)))


LRU_BLOCKS_PER_STEP = 1


def _lru_specs(s, n_blocks):
    bps = min(LRU_BLOCKS_PER_STEP, n_blocks)
    wid = bps * HEAD
    seq0 = pl.BlockSpec((s, wid), lambda h: (0, h))
    seq1 = pl.BlockSpec((s, wid), lambda h: (0, n_blocks // bps + h))
    taps = pl.BlockSpec((CONV_TAPS, wid), lambda h: (0, h))
    vec = pl.BlockSpec((1, wid), lambda h: (0, h))
    mat = pl.BlockSpec((bps, HEAD, HEAD), lambda h: (h, 0, 0))
    return bps, seq0, seq1, taps, vec, mat


def _per_block(one_block, n_2d, n_mat_pos, bps):
    def body(*refs):
        for a in range(bps):
            views = [r.at[a] if i in n_mat_pos else r.at[:, pl.ds(a * HEAD, HEAD)] for i, r in enumerate(refs[:n_2d])]
            one_block(*views, *refs[n_2d:])
    return body


def _lru_fwd(xrg, wconv, bconv, wa, ba, wx, bx, lam):
    s = xrg.shape[0]
    nb = wa.shape[0]
    tc = _lru_chunk(s)
    bps, seq0, seq1, taps, vec, mat = _lru_specs(s, nb)
    pad = SUBLANES

    def one_block(xr_ref, xg_ref, wc_ref, bc_ref, wa_ref, ba_ref, wx_ref, bx_ref, lam_ref, o_ref, h_ref, pad_s, a_s, u_s):
        pad_s[0:pad, :] = jnp.zeros((pad, HEAD), F32)
        pad_s[pad:pad + s, :] = xr_ref[...]
        wab, wxb = wa_ref[...].astype(BF16), wx_ref[...].astype(BF16)
        sp = _softplus_neg(lam_ref[...])
        for c in range(s // tc):
            base = c * tc
            xc = bc_ref[...] + sum(wc_ref[i:i + 1, :] * pad_s[pl.ds(base + pad - (CONV_TAPS - 1) + i, tc), :]
                                   for i in range(CONV_TAPS))
            _, ig, a, mult = _lru_gates(xc, wab, ba_ref[...], wxb, bx_ref[...], sp)
            a_s[base:base + tc, :] = a
            u_s[base:base + tc, :] = mult * (ig * xc)

        row = lax.broadcasted_iota(jnp.int32, (SUBLANES, HEAD), 0)

        def chunk(ci, hprev):
            off = pl.multiple_of(ci * SUBLANES, SUBLANES)
            a8, b8 = a_s[pl.ds(off, SUBLANES), :], u_s[pl.ds(off, SUBLANES), :]
            for d in (1, 2, 4):
                a_sh = jnp.where(row < d, 1.0, pltpu.roll(a8, d, 0))
                b_sh = jnp.where(row < d, 0.0, pltpu.roll(b8, d, 0))
                b8 = a8 * b_sh + b8
                a8 = a8 * a_sh
            h8 = a8 * hprev + b8
            h_ref[pl.ds(off, SUBLANES), :] = h8
            return h8[SUBLANES - 1:SUBLANES, :]

        lax.fori_loop(0, s // SUBLANES, chunk, jnp.zeros((1, HEAD), F32), unroll=8)
        for c in range(s // tc):
            sl = slice(c * tc, (c + 1) * tc)
            gel, _ = _gelu_parts(xg_ref[sl, :])
            o_ref[sl, :] = h_ref[sl, :] * gel

    return pl.pallas_call(
        _per_block(one_block, 11, (4, 6), bps), grid=(nb // bps,),
        in_specs=[seq0, seq1, taps, vec, mat, vec, mat, vec, vec],
        out_specs=(seq0, seq0),
        out_shape=(SDS((s, nb * HEAD), F32), SDS((s, nb * HEAD), F32)),
        scratch_shapes=[pltpu.VMEM((s + pad, HEAD), F32), pltpu.VMEM((s, HEAD), F32), pltpu.VMEM((s, HEAD), F32)],
        compiler_params=_params(("parallel",)), name="lru_fwd")(xrg, xrg, wconv, bconv, wa, ba, wx, bx, lam)


def _lru_bwd(xrg, dol, hseq, wconv, bconv, wa, ba, wx, bx, lam):
    s = xrg.shape[0]
    nb = wa.shape[0]
    tc = _lru_chunk(s)
    bps, seq0, seq1, taps, vec, mat = _lru_specs(s, nb)
    pad = SUBLANES

    def one_block(xr_ref, xg_ref, do_ref, h_ref, wc_ref, bc_ref, wa_ref, ba_ref, wx_ref, bx_ref, lam_ref,
             dxr_ref, dxg_ref, dwc_ref, dbc_ref, dwa_ref, dba_ref, dwx_ref, dbx_ref, dlam_ref,
             pad_s, hp_s, a_s, g_s, da_s, dxc_s):
        pad_s[0:pad, :] = jnp.zeros((pad, HEAD), F32)
        pad_s[pad:pad + s, :] = xr_ref[...]
        hp_s[0:pad, :] = jnp.zeros((pad, HEAD), F32)
        hp_s[pad:pad + s, :] = h_ref[...]
        a_s[s:s + pad, :] = jnp.zeros((pad, HEAD), F32)
        dxc_s[s:s + pad, :] = jnp.zeros((pad, HEAD), F32)
        wab, wxb = wa_ref[...].astype(BF16), wx_ref[...].astype(BF16)
        lam_v = lam_ref[...]
        sp = _softplus_neg(lam_v)

        def conv_in(c):
            base = c * tc
            wins = [pad_s[pl.ds(base + pad - (CONV_TAPS - 1) + i, tc), :] for i in range(CONV_TAPS)]
            xc = bc_ref[...] + sum(wc_ref[i:i + 1, :] * wins[i] for i in range(CONV_TAPS))
            return xc, wins

        for c in range(s // tc):
            sl = slice(c * tc, (c + 1) * tc)
            xc, _ = conv_in(c)
            _, _, a, _ = _lru_gates(xc, wab, ba_ref[...], wxb, bx_ref[...], sp)
            a_s[sl, :] = a
            gel, dgel = _gelu_parts(xg_ref[sl, :])
            dov = do_ref[sl, :]
            g_s[sl, :] = dov * gel
            dxg_ref[sl, :] = (dov * h_ref[sl, :] * dgel).astype(BF16)

        row = lax.broadcasted_iota(jnp.int32, (SUBLANES, HEAD), 0)
        n_chunks = s // SUBLANES

        def chunk(it, gnext):
            ci = n_chunks - 1 - it
            off = pl.multiple_of(ci * SUBLANES, SUBLANES)
            a8 = a_s[pl.ds(off, SUBLANES), :]
            a8n = a_s[pl.ds(off + SUBLANES, SUBLANES), :]
            c8 = pltpu.roll(jnp.where(row == 0, a8n, a8), SUBLANES - 1, 0)
            g8 = g_s[pl.ds(off, SUBLANES), :]
            for d in (1, 2, 4):
                c_sh = jnp.where(row >= SUBLANES - d, 1.0, pltpu.roll(c8, SUBLANES - d, 0))
                g_sh = jnp.where(row >= SUBLANES - d, 0.0, pltpu.roll(g8, SUBLANES - d, 0))
                g8 = c8 * g_sh + g8
                c8 = c8 * c_sh
            g8 = g8 + c8 * gnext
            g_s[pl.ds(off, SUBLANES), :] = g8
            h8 = hp_s[pl.ds(off + pad, SUBLANES), :]
            h8p = hp_s[pl.ds(off, SUBLANES), :]
            da_s[pl.ds(off, SUBLANES), :] = g8 * pltpu.roll(jnp.where(row == SUBLANES - 1, h8p, h8), 1, 0)
            return g8[0:1, :]

        lax.fori_loop(0, n_chunks, chunk, jnp.zeros((1, HEAD), F32), unroll=8)

        dsp = jnp.zeros((1, HEAD), F32)
        dbc = jnp.zeros((1, HEAD), F32)
        dba = jnp.zeros((1, HEAD), F32)
        dbx = jnp.zeros((1, HEAD), F32)
        dwa = jnp.zeros((HEAD, HEAD), F32)
        dwx = jnp.zeros((HEAD, HEAD), F32)
        dwc = [jnp.zeros((1, HEAD), F32) for _ in range(CONV_TAPS)]
        for c in range(s // tc):
            sl = slice(c * tc, (c + 1) * tc)
            xc, wins = conv_in(c)
            r, ig, a, mult = _lru_gates(xc, wab, ba_ref[...], wxb, bx_ref[...], sp)
            du, da = g_s[sl, :], da_s[sl, :]
            d_ix = du * mult
            dla = da * a - (du * ig * xc) * (a * a / mult)
            dsp = dsp + jnp.sum(dla * r, axis=0, keepdims=True) * (-LRU_C)
            dpa = (dla * (-LRU_C * sp)) * r * (1.0 - r)
            dpx = (d_ix * xc) * ig * (1.0 - ig)
            dpab, dpxb, xb = dpa.astype(BF16), dpx.astype(BF16), xc.astype(BF16)
            dxc = d_ix * ig + _dot_nt(dpab, wab) + _dot_nt(dpxb, wxb)
            dwa = dwa + _dot_tn(xb, dpab)
            dwx = dwx + _dot_tn(xb, dpxb)
            dba = dba + jnp.sum(dpa, axis=0, keepdims=True)
            dbx = dbx + jnp.sum(dpx, axis=0, keepdims=True)
            dbc = dbc + jnp.sum(dxc, axis=0, keepdims=True)
            for i in range(CONV_TAPS):
                dwc[i] = dwc[i] + jnp.sum(dxc * wins[i], axis=0, keepdims=True)
            dxc_s[sl, :] = dxc

        for c in range(s // tc):
            base = c * tc
            dxr = sum(wc_ref[i:i + 1, :] * dxc_s[pl.ds(base + (CONV_TAPS - 1) - i, tc), :] for i in range(CONV_TAPS))
            dxr_ref[base:base + tc, :] = dxr.astype(BF16)

        for i in range(CONV_TAPS):
            dwc_ref[i:i + 1, :] = dwc[i]
        dbc_ref[...] = dbc
        dwa_ref[...] = dwa
        dwx_ref[...] = dwx
        dba_ref[...] = dba
        dbx_ref[...] = dbx
        dlam_ref[...] = dsp * (-_sigmoid(-lam_v))

    w = nb * HEAD
    return pl.pallas_call(
        _per_block(one_block, 20, (6, 8, 15, 17), bps), grid=(nb // bps,),
        in_specs=[seq0, seq1, seq0, seq0, taps, vec, mat, vec, mat, vec, vec],
        out_specs=(seq0, seq0, taps, vec, mat, vec, mat, vec, vec),
        out_shape=(SDS((s, w), BF16), SDS((s, w), BF16), SDS((CONV_TAPS, w), F32), SDS((1, w), F32),
                   SDS((nb, HEAD, HEAD), F32), SDS((1, w), F32), SDS((nb, HEAD, HEAD), F32), SDS((1, w), F32),
                   SDS((1, w), F32)),
        scratch_shapes=[pltpu.VMEM((s + pad, HEAD), F32), pltpu.VMEM((s + pad, HEAD), F32),
                        pltpu.VMEM((s + pad, HEAD), F32), pltpu.VMEM((s, HEAD), F32),
                        pltpu.VMEM((s, HEAD), F32), pltpu.VMEM((s + pad, HEAD), F32)],
        compiler_params=_params(("parallel",)), name="lru_bwd",
    )(xrg, xrg, dol, hseq, wconv, bconv, wa, ba, wx, bx, lam)


def _ada_mod(c_all, w_sh, b_sh):
    n_ex, d = c_all.shape
    n = w_sh.shape[1]
    tn = _tile(n, 512)

    def body(c_ref, w_ref, b_ref, mod_ref, act_ref):
        cv = c_ref[...]
        act = cv * _sigmoid(cv)
        act_ref[...] = act
        mod_ref[...] = _dot(act.astype(BF16), w_ref[...].astype(BF16)) + b_ref[...]

    return pl.pallas_call(
        body, grid=(n // tn,),
        in_specs=[pl.BlockSpec((n_ex, d), lambda j: (0, 0)), pl.BlockSpec((d, tn), lambda j: (0, j)),
                  pl.BlockSpec((1, tn), lambda j: (0, j))],
        out_specs=(pl.BlockSpec((n_ex, tn), lambda j: (0, j)), pl.BlockSpec((n_ex, d), lambda j: (0, 0))),
        out_shape=(SDS((n_ex, n), F32), SDS((n_ex, d), F32)),
        compiler_params=_params(("arbitrary",)), name="ada_mod")(c_all, w_sh, b_sh)


def _adamw_math(w, g, m, v):
    m = ADAM_B1 * m + (1.0 - ADAM_B1) * g
    v = ADAM_B2 * v + (1.0 - ADAM_B2) * (g * g)
    m_hat = m / (1.0 - ADAM_B1 ** ADAM_STEP)
    v_hat = v / (1.0 - ADAM_B2 ** ADAM_STEP)
    delta = -ADAM_LR * (m_hat / (jnp.sqrt(v_hat) + ADAM_EPS) + ADAM_WD * w)
    return delta, m, v


def _adamw_plain(name, w, g, m, v):
    def body(w_ref, g_ref, m_ref, v_ref, d_ref, mo_ref, vo_ref):
        d_ref[...], mo_ref[...], vo_ref[...] = _adamw_math(w_ref[...], g_ref[...], m_ref[...], v_ref[...])

    return pl.pallas_call(body, out_shape=(SDS(w.shape, F32),) * 3, name=name)(w, g, m, v)


def _adamw_halves(name, c_arr, w, m, v, g_own, g_recv):
    r, n = w.shape
    rh = r // 2
    tr = _tile(rh, 256)
    nh = rh // tr

    def body(c_ref, w_ref, m_ref, v_ref, go_ref, gr_ref, g_ref, d_ref, mo_ref, vo_ref):
        own = (pl.program_id(0) // nh) == c_ref[0]
        g = jnp.where(own, go_ref[...], gr_ref[...])
        g_ref[...] = g
        d_ref[...], mo_ref[...], vo_ref[...] = _adamw_math(w_ref[...], g, m_ref[...], v_ref[...])

    full = pl.BlockSpec((tr, n), lambda i, c: (i, 0))
    own = pl.BlockSpec((tr, n), lambda i, c: (jnp.where(i // nh == c[0], i % nh, 0), 0))
    recv = pl.BlockSpec((tr, n), lambda i, c: (jnp.where(i // nh == c[0], 0, i % nh), 0))
    return pl.pallas_call(
        body,
        grid_spec=pltpu.PrefetchScalarGridSpec(
            num_scalar_prefetch=1, grid=(2 * nh,), in_specs=[full, full, full, own, recv],
            out_specs=(full,) * 4),
        out_shape=(SDS((r, n), F32),) * 4,
        compiler_params=_params(("parallel",)), name=name)(c_arr, w, m, v, g_own, g_recv)


def _adamw_ada(w, m, v, act_t, dmod):
    d, n = w.shape
    n_ex = act_t.shape[1]
    tr = _tile(d, 256)

    def body(a_ref, dm_ref, w_ref, m_ref, v_ref, g_ref, d_ref, mo_ref, vo_ref):
        g = _dot(a_ref[...], dm_ref[...])
        g_ref[...] = g
        d_ref[...], mo_ref[...], vo_ref[...] = _adamw_math(w_ref[...], g, m_ref[...], v_ref[...])

    full = pl.BlockSpec((tr, n), lambda i: (i, 0))
    return pl.pallas_call(
        body, grid=(d // tr,),
        in_specs=[pl.BlockSpec((tr, n_ex), lambda i: (i, 0)), pl.BlockSpec((n_ex, n), lambda i: (0, 0)), full, full, full],
        out_specs=(full,) * 4, out_shape=(SDS((d, n), F32),) * 4,
        compiler_params=_params(("parallel",)), name="adamw_ada")(act_t, dmod, w, m, v)


def _small_reduce_adamw(parts, w, m, v):
    n_dev, r, _ = parts.shape
    tr = r if r <= PACK_ROWS else PACK_ROWS

    def body(p_ref, w_ref, m_ref, v_ref, g_ref, d_ref, mo_ref, vo_ref):
        g = p_ref[0]
        for k in range(1, n_dev):
            g = g + p_ref[k]
        g_ref[...] = g
        d_ref[...], mo_ref[...], vo_ref[...] = _adamw_math(w_ref[...], g, m_ref[...], v_ref[...])

    full = pl.BlockSpec((tr, LANES), lambda i: (i, 0))
    return pl.pallas_call(
        body, grid=(r // tr,),
        in_specs=[pl.BlockSpec((n_dev, tr, LANES), lambda i: (0, i, 0)), full, full, full],
        out_specs=(full,) * 4, out_shape=(SDS((r, LANES), F32),) * 4,
        compiler_params=_params(("parallel",)), name="small_reduce_adamw")(parts, w, m, v)


def _mesh_pos():
    return lax.axis_index("x"), lax.axis_index("y"), lax.axis_index("c")


def _other_chips(x, y):
    return [(1 - x, y), (x, 1 - y), (1 - x, 1 - y)]


def _all_gather_small(name, blk):
    r, n = blk.shape

    def body(x_ref, out_ref, send_sems, recv_sems, local_sem):
        x, y, c = _mesh_pos()
        me, sibling = (x, y, c), (x, y, 1 - c)
        chips = _other_chips(x, y)

        def rows(px, py, pc):
            return out_ref.at[4 * px + 2 * py + pc]

        def copy(k, block, to, src=None):
            return pltpu.make_async_remote_copy(
                src_ref=rows(*block) if src is None else src, dst_ref=rows(*block),
                send_sem=send_sems.at[k], recv_sem=recv_sems.at[k], device_id=to, device_id_type=MESH)

        mine = pltpu.make_async_copy(x_ref, rows(*me), local_sem)
        mine.start()
        first = [copy(0, me, sibling, src=x_ref)]
        first += [copy(1 + j, me, (*chip, c), src=x_ref) for j, chip in enumerate(chips)]
        for cp in first:
            cp.start()
        passed = [copy(4 + j, (*chip, c), sibling) for j, chip in enumerate(chips)]
        for j, chip in enumerate(chips):
            copy(1 + j, (*chip, c), me).wait_recv()
            passed[j].start()
        copy(0, sibling, me).wait_recv()
        for j, chip in enumerate(chips):
            copy(4 + j, (*chip, 1 - c), me).wait_recv()
        for cp in first + passed:
            cp.wait_send()
        mine.wait()

    return pl.pallas_call(
        body, out_shape=SDS((N_DEV, r, n), blk.dtype),
        in_specs=[pl.BlockSpec(memory_space=pltpu.VMEM)], out_specs=pl.BlockSpec(memory_space=pltpu.VMEM),
        scratch_shapes=[pltpu.SemaphoreType.DMA((7,)), pltpu.SemaphoreType.DMA((7,)), pltpu.SemaphoreType.DMA],
        compiler_params=pltpu.CompilerParams(vmem_limit_bytes=VMEM_LIMIT), name=name)(blk)


_ANY = pl.BlockSpec(memory_space=pl.ANY)
_HBM = pl.BlockSpec(memory_space=pltpu.HBM)
_SEM = pl.BlockSpec(memory_space=pltpu.SEMAPHORE)
_EFFECT = pltpu.SideEffectType.DATAFLOW_SIDE_EFFECTING


def _hbm(a):
    return pltpu.with_memory_space_constraint(a, pltpu.HBM)


def _place_cast(name, j_arr, shard, kind, after):
    r, n = shard.shape
    tr = _tile(r, 256)
    nr = r // tr
    if kind == "col":
        out_shape, o_spec = (r, N_CHIP * n), pl.BlockSpec((tr, n), lambda i, j: (i, j[0]))
    else:
        out_shape, o_spec = (N_CHIP * r, n), pl.BlockSpec((tr, n), lambda i, j: (j[0] * nr + i, 0))

    def body(j_ref, s_ref, after_ref, o_ref):
        o_ref[...] = s_ref[...].astype(BF16)

    return pl.pallas_call(
        body,
        grid_spec=pltpu.PrefetchScalarGridSpec(
            num_scalar_prefetch=1, grid=(nr,), in_specs=[pl.BlockSpec((tr, n), lambda i, j: (i, 0)), _ANY],
            out_specs=o_spec),
        out_shape=SDS(out_shape, BF16), compiler_params=_params(("parallel",)), name=name)(j_arr, shard, after)


def _gather_start(name, fulls, kinds, groups, after):
    nw = len(fulls)
    ng = len(groups)

    def body(*refs):
        outs = refs[nw + 1:]
        send, recv, full, token = outs[:ng], outs[ng:2 * ng], outs[2 * ng:2 * ng + nw], outs[2 * ng + nw]
        x, y, c = _mesh_pos()
        for g, ws in enumerate(groups):
            for li, w in enumerate(ws):
                mine = _full_region(full[w], kinds[w], x, y, c)
                for k, chip in enumerate(_other_chips(x, y)):
                    pltpu.make_async_remote_copy(
                        src_ref=mine, dst_ref=mine, send_sem=send[g].at[3 * li + k], recv_sem=recv[g].at[3 * li + k],
                        device_id=(*chip, c), device_id_type=MESH).start()
        token[...] = jnp.zeros_like(token)

    sems = tuple(pltpu.SemaphoreType.DMA((3 * len(ws),)) for ws in groups)
    outs = pl.pallas_call(
        body,
        out_shape=sems + sems + tuple(pltpu.HBM(f_.shape, f_.dtype) for f_ in fulls) + (SDS((SUBLANES, LANES), F32),),
        in_specs=[_HBM] * nw + [_ANY],
        out_specs=tuple([_SEM] * (2 * ng) + [_HBM] * nw + [pl.BlockSpec(memory_space=pltpu.VMEM)]),
        input_output_aliases={w: 2 * ng + w for w in range(nw)},
        compiler_params=pltpu.CompilerParams(has_side_effects=_EFFECT),
        name=name,
    )(*[_hbm(f_) for f_ in fulls], after)
    return outs[:ng], outs[ng:2 * ng], outs[2 * ng:2 * ng + nw], outs[2 * ng + nw]


def _full_region(full, kind, px, py, half):
    j = 2 * px + py
    if kind == "col":
        rh, cols = full.shape[0] // 2, full.shape[1] // N_CHIP
        return full.at[pl.ds(half * rh, rh), pl.ds(j * cols, cols)]
    rows = full.shape[0] // N_CHIP
    rh = rows // 2
    return full.at[pl.ds(j * rows + half * rh, rh), :]


def _gather_pass(name, fulls, kinds, send, recv, after, thru):
    nw = len(fulls)

    def body(*refs):
        send_r, recv_r = refs[nw], refs[nw + 1]
        full, fsend, frecv = refs[nw + 4:2 * nw + 4], refs[2 * nw + 4], refs[2 * nw + 5]
        x, y, c = _mesh_pos()
        chips = _other_chips(x, y)
        for w in range(nw):
            for k, chip in enumerate(chips):
                landed = _full_region(full[w], kinds[w], *chip, c)
                arrive = pltpu.make_async_remote_copy(
                    src_ref=_full_region(full[w], kinds[w], x, y, c), dst_ref=landed, send_sem=send_r.at[3 * w + k],
                    recv_sem=recv_r.at[3 * w + k], device_id=(*chip, c), device_id_type=MESH)
                arrive.wait_recv()
                arrive.wait_send()
                pltpu.make_async_remote_copy(
                    src_ref=landed, dst_ref=landed, send_sem=fsend.at[3 * w + k], recv_sem=frecv.at[3 * w + k],
                    device_id=(x, y, 1 - c), device_id_type=MESH).start()

    sem = pltpu.SemaphoreType.DMA((3 * nw,))
    outs = pl.pallas_call(
        body,
        out_shape=tuple(pltpu.HBM(f_.shape, f_.dtype) for f_ in fulls) + (sem, sem, SDS(thru.shape, thru.dtype)),
        in_specs=[_HBM] * nw + [_SEM, _SEM, _ANY, _ANY],
        out_specs=tuple([_HBM] * nw + [_SEM, _SEM, _ANY]),
        input_output_aliases={**{w: w for w in range(nw)}, nw + 3: nw + 2},
        compiler_params=pltpu.CompilerParams(has_side_effects=_EFFECT),
        name=name,
    )(*fulls, send, recv, after, thru)
    return outs[:nw], outs[nw], outs[nw + 1], outs[nw + 2]


def _gather_finish(name, fulls, kinds, fsend, frecv, after):
    nw = len(fulls)

    def body(*refs):
        fsend_r, frecv_r = refs[nw], refs[nw + 1]
        full = refs[nw + 3:2 * nw + 3]
        x, y, c = _mesh_pos()
        chips = _other_chips(x, y)
        for w in range(nw):
            for k, chip in enumerate(chips):
                cp = pltpu.make_async_remote_copy(
                    src_ref=_full_region(full[w], kinds[w], *chip, c), dst_ref=_full_region(full[w], kinds[w], *chip, 1 - c),
                    send_sem=fsend_r.at[3 * w + k], recv_sem=frecv_r.at[3 * w + k],
                    device_id=(x, y, 1 - c), device_id_type=MESH)
                cp.wait_send()
                cp.wait_recv()

    outs = pl.pallas_call(
        body,
        out_shape=tuple(pltpu.HBM(f_.shape, f_.dtype) for f_ in fulls),
        in_specs=[_HBM] * nw + [_SEM, _SEM, _ANY],
        out_specs=tuple([_HBM] * nw),
        input_output_aliases={w: w for w in range(nw)},
        compiler_params=pltpu.CompilerParams(has_side_effects=_EFFECT),
        name=name,
    )(*fulls, fsend, frecv, after)
    return list(outs)


def _half_of(ref, kind, half):
    if kind == "col":
        rh = ref.shape[0] // 2
        return ref.at[pl.ds(half * rh, rh), :]
    rh = ref.shape[1] // 2
    return ref.at[:, pl.ds(half * rh, rh), :]


def _half_shape(g, kind):
    if kind == "col":
        return (g.shape[0] // 2, g.shape[1])
    return (g.shape[0], g.shape[1] // 2, g.shape[2])


def _plan_swap_half(kind):
    def plan(src, land, x, y, c):
        return [(_half_of(src, kind, 1 - c), land, (x, y, 1 - c))]
    return plan


def _plan_scatter(kind):
    def plan(src, land, x, y, c):
        out = []
        for k, (px, py) in enumerate(_other_chips(x, y)):
            j = 2 * px + py
            if kind == "col":
                n = src.shape[1] // N_CHIP
                blk = src.at[:, pl.ds(j * n, n)]
            else:
                blk = src.at[j]
            out.append((blk, land.at[k], (px, py, c)))
        return out
    return plan


def _plan_whole(src, land, x, y, c):
    return [(src, land, (x, y, 1 - c))]


def _split_start(name, src, land_shape, n, plan, thru):
    def body(src_in, land_in, thru_in, send, recv, src_ref, land_ref, thru_out):
        x, y, c = _mesh_pos()
        for k, (s_, d_, dev) in enumerate(plan(src_ref, land_ref, x, y, c)):
            pltpu.make_async_remote_copy(src_ref=s_, dst_ref=d_, send_sem=send.at[k], recv_sem=recv.at[k],
                                         device_id=dev, device_id_type=MESH).start()

    sem = pltpu.SemaphoreType.DMA((n,))
    return pl.pallas_call(
        body,
        out_shape=(sem, sem, pltpu.HBM(src.shape, src.dtype), pltpu.HBM(land_shape, src.dtype), SDS(thru.shape, thru.dtype)),
        in_specs=[_HBM, _HBM, _ANY], out_specs=(_SEM, _SEM, _HBM, _HBM, _ANY),
        input_output_aliases={0: 2, 1: 3, 2: 4},
        compiler_params=pltpu.CompilerParams(has_side_effects=_EFFECT), name=name,
    )(_hbm(src), _hbm(lax.empty(land_shape, src.dtype)), thru)


def _split_wait(name, send, recv, src, land, plan, after):
    def body(src_in, land_in, send_r, recv_r, after_r, src_ref, land_ref):
        x, y, c = _mesh_pos()
        for k, (s_, d_, dev) in enumerate(plan(src_ref, land_ref, x, y, c)):
            cp = pltpu.make_async_remote_copy(src_ref=s_, dst_ref=d_, send_sem=send_r.at[k], recv_sem=recv_r.at[k],
                                              device_id=dev, device_id_type=MESH)
            cp.wait_send()
            cp.wait_recv()

    return pl.pallas_call(
        body,
        out_shape=(pltpu.HBM(src.shape, src.dtype), pltpu.HBM(land.shape, land.dtype)),
        in_specs=[_HBM, _HBM, _SEM, _SEM, _ANY], out_specs=(_HBM, _HBM),
        input_output_aliases={0: 0, 1: 1},
        compiler_params=pltpu.CompilerParams(has_side_effects=_EFFECT), name=name,
    )(src, land, send, recv, after)


def _add_halves(name, c_arr, g, got, kind):
    if kind == "col":
        rh, n = got.shape
        tr = _tile(rh, 256)
        nh = rh // tr
        g_spec = pl.BlockSpec((tr, n), lambda i, c: (c[0] * nh + i, 0))
        o_spec = pl.BlockSpec((tr, n), lambda i, c: (i, 0))
        grid = (nh,)
    else:
        nc, rh, n = got.shape
        tr = _tile(rh, 256)
        nh = rh // tr
        g_spec = pl.BlockSpec((None, tr, n), lambda j, i, c: (j, c[0] * nh + i, 0))
        o_spec = pl.BlockSpec((None, tr, n), lambda j, i, c: (j, i, 0))
        grid = (nc, nh)

    def body(c_ref, g_ref, r_ref, o_ref):
        o_ref[...] = (g_ref[...].astype(F32) + r_ref[...].astype(F32)).astype(o_ref.dtype)

    return pl.pallas_call(
        body,
        grid_spec=pltpu.PrefetchScalarGridSpec(num_scalar_prefetch=1, grid=grid, in_specs=[g_spec, o_spec], out_specs=o_spec),
        out_shape=SDS(got.shape, got.dtype),
        compiler_params=_params(("parallel",) * len(grid)), name=name)(c_arr, g, got)


def _sum_partials(name, j_arr, part, got, kind):
    _, rh, n = got.shape
    tr = _tile(rh, 256)
    if kind == "col":
        p_spec = pl.BlockSpec((tr, n), lambda i, j: (i, j[0]))
    else:
        p_spec = pl.BlockSpec((None, tr, n), lambda i, j: (j[0], i, 0))

    def body(j_ref, p_ref, r_ref, o_ref):
        o_ref[...] = ((p_ref[...].astype(F32) + r_ref[0].astype(F32)) + r_ref[1].astype(F32)) + r_ref[2].astype(F32)

    return pl.pallas_call(
        body,
        grid_spec=pltpu.PrefetchScalarGridSpec(
            num_scalar_prefetch=1, grid=(rh // tr,),
            in_specs=[p_spec, pl.BlockSpec((3, tr, n), lambda i, j: (0, i, 0))],
            out_specs=pl.BlockSpec((tr, n), lambda i, j: (i, 0))),
        out_shape=SDS((rh, n), F32),
        compiler_params=_params(("parallel",)), name=name)(j_arr, part, got)


def _swap_reduced(name, halves):
    nw = len(halves)

    def body(*refs):
        h, got = refs[:nw], refs[nw:2 * nw]
        send_sems, recv_sems = refs[2 * nw:]
        x, y, c = _mesh_pos()
        cps = []
        for w in range(nw):
            cp = pltpu.make_async_remote_copy(
                src_ref=h[w], dst_ref=got[w], send_sem=send_sems.at[w], recv_sem=recv_sems.at[w],
                device_id=(x, y, 1 - c), device_id_type=MESH)
            cp.start()
            cps.append(cp)
        for cp in cps:
            cp.wait()

    return pl.pallas_call(
        body, out_shape=tuple(SDS(h.shape, h.dtype) for h in halves),
        in_specs=[_ANY] * nw, out_specs=tuple([_ANY] * nw),
        scratch_shapes=[pltpu.SemaphoreType.DMA((nw,)), pltpu.SemaphoreType.DMA((nw,))],
        name=name)(*halves)


def _pack(arrays):
    flat = [a.reshape(-1).astype(F32) for a in arrays]
    flat = [jnp.pad(f, (0, (-f.shape[0]) % LANES)) for f in flat]
    sizes = [f.shape[0] for f in flat]
    total = sum(sizes)
    rows = total // LANES
    tail = LANES * ((-rows) % (PACK_ROWS if rows > PACK_ROWS else SUBLANES))
    if tail:
        flat.append(jnp.zeros((tail,), F32))
    return jnp.concatenate(flat).reshape(-1, LANES), sizes


def _unpack(slab, sizes, shapes, lead=()):
    flat = slab.reshape(lead + (-1,))
    out, off = [], 0
    for sz, shp in zip(sizes, shapes):
        n = math.prod(shp)
        out.append(flat[..., off:off + n].reshape(lead + tuple(shp)))
        off += sz
    return out


def kernel(x, c, w_ada, b_ada, g_norm_mix, w_in, w_conv, b_conv, w_rg_a, b_rg_a, w_rg_x, b_rg_x, lru_lambda, g_attn_out, g_lru_out, w_out, g_norm_mlp, w_mlp_in, w_mlp_out, g_norm_final, loss_target, m_w_ada, m_b_ada, m_g_norm_mix, m_w_in, m_w_conv, m_b_conv, m_w_rg_a, m_b_rg_a, m_w_rg_x, m_b_rg_x, m_lru_lambda, m_g_attn_out, m_g_lru_out, m_w_out, m_g_norm_mlp, m_w_mlp_in, m_w_mlp_out, m_g_norm_final, v_w_ada, v_b_ada, v_g_norm_mix, v_w_in, v_w_conv, v_b_conv, v_w_rg_a, v_b_rg_a, v_w_rg_x, v_b_rg_x, v_lru_lambda, v_g_attn_out, v_g_lru_out, v_w_out, v_g_norm_mlp, v_w_mlp_in, v_w_mlp_out, v_g_norm_final):
    s, d = x.shape[1], x.shape[2]
    aw = d // 2
    nh = aw // HEAD
    f = w_mlp_out.shape[1] * N_CHIP
    n_ada = w_ada.shape[2]
    n_cv = w_conv.shape[2]
    ix, iy, ic = lax.axis_index("x"), lax.axis_index("y"), lax.axis_index("c")
    chip = 2 * ix + iy
    me = 2 * chip + ic
    c_arr = jnp.reshape(ic, (1,)).astype(jnp.int32)
    j_arr = jnp.reshape(chip, (1,)).astype(jnp.int32)

    x2d, tgt = x[0], loss_target[0]

    slab, sizes = _pack([c, w_conv])
    gathered = _all_gather_small("comm_gather_cond", slab)
    c_parts, cv_parts = _unpack(gathered, sizes, [(d,), (CONV_TAPS, n_cv)], lead=(N_DEV,))
    c_all = c_parts
    w_conv_full = jnp.concatenate([cv_parts[2 * j] for j in range(N_CHIP)], axis=-1)
    b_sh = lax.dynamic_slice(b_ada, (0, chip * n_ada), (1, n_ada))
    mod_part, act_all = _ada_mod(c_all, w_ada[0], b_sh)
    mod_g = _all_gather_small("comm_gather_mod", mod_part.reshape(-1, LANES))
    mod_g = mod_g.reshape(N_DEV, N_DEV, n_ada)
    mod = jnp.concatenate([lax.dynamic_index_in_dim(mod_g[2 * j], me, 0, keepdims=True) for j in range(N_CHIP)], axis=-1)
    sh1, sc1, gt1, sh2, sc2, gt2 = [mod[:, k * d:(k + 1) * d] for k in range(N_MOD)]

    kinds = ("col", "row", "col", "row")
    groups = ((0,), (1, 2), (3,))
    big_w = (w_in, w_out, w_mlp_in, w_mlp_out)
    placed = _place_cast("place_cast_0", j_arr, w_in[0], kinds[0], mod)
    send_a, recv_a, full_a, token = _gather_start("gather_start_a", [placed], kinds[:1], groups[:1], mod)
    placed = [_place_cast("place_cast_%d" % i, j_arr, big_w[i][0], kinds[i], token) for i in (1, 2, 3)]
    send_b, recv_b, full_b, _ = _gather_start("gather_start_b", placed, kinds[1:], ((0, 1), (2,)), token)
    ag_send, ag_recv, ag_full = send_a + send_b, recv_a + recv_b, full_a + full_b

    def gather_pass(g, after, thru):
        ws = groups[g]
        return _gather_pass("gather_pass_%d" % g, [ag_full[w] for w in ws], [kinds[w] for w in ws],
                            ag_send[g], ag_recv[g], after, thru)

    fl, fs, fr, sh1 = gather_pass(0, mod, sh1)
    h1, rstd1 = _norm_mod_fwd("norm_mod_fwd1", x2d, g_norm_mix, sc1, sh1)
    (w_in_f,) = _gather_finish("gather_finish_0", fl, kinds[0:1], fs, fr, h1)
    (qkv,) = _matmul("mm_qkv", h1, w_in_f, "nn", s, 3 * aw, d, (BF16,))
    (xrg,) = _matmul("mm_xrg", h1, w_in_f, "nn", s, 2 * aw, d, (F32,), b_off=3 * aw)
    o_attn = _attn_fwd(qkv, nh)
    fl, fs, fr, xrg = gather_pass(1, o_attn, xrg)
    wa3, wx3 = w_rg_a[0], w_rg_x[0]
    o_lru, hseq = _lru_fwd(xrg, w_conv_full, b_conv, wa3, b_rg_a, wx3, b_rg_x, lru_lambda)
    mixed, rstd_a, rstd_l = _mix_norm_fwd(o_attn, o_lru, g_attn_out, g_lru_out)
    w_out_f, w_mi_f = _gather_finish("gather_finish_1", fl, kinds[1:3], fs, fr, mixed)

    def residual(acc, xin, gt):
        return acc, xin + gt * acc

    y1, x1 = _matmul("mm_out", mixed, w_out_f, "nn", s, d, d, (F32, F32), extras=(x2d, gt1),
                     extra_kinds=("tile", "row"), epilogue=residual)
    h2, rstd2 = _norm_mod_fwd("norm_mod_fwd2", x1, g_norm_mlp, sc2, sh2)

    def sq_relu(acc):
        r = jnp.maximum(acc, 0.0)
        return 2.0 * r, r * r

    r2, hid = _matmul("mm_mlp_in", h2, w_mi_f, "nn", s, f, d, (BF16, BF16), epilogue=sq_relu)
    fl, fs, fr, hid = gather_pass(2, r2, hid)
    (w_mo_f,) = _gather_finish("gather_finish_2", fl, kinds[3:4], fs, fr, hid)
    y2, x2 = _matmul("mm_mlp_out", hid, w_mo_f, "nn", s, d, f, (F32, F32), extras=(x1, gt2),
                     extra_kinds=("tile", "row"), epilogue=residual)
    dx2, loss_row, dg_final, dy2, dgt2 = _final_loss(x2, g_norm_final.reshape(1, d), tgt, y2, gt2)

    def rs_begin(tag, g, kind, thru):
        send, recv, g, land, thru = _split_start("rs_swap_start_" + tag, g, _half_shape(g, kind), 1,
                                                 _plan_swap_half(kind), thru)
        return {"tag": tag, "kind": kind, "swap": (send, recv, g, land)}, thru

    def rs_mid(st, after, thru):
        tag, kind = st["tag"], st["kind"]
        g, got = _split_wait("rs_swap_wait_" + tag, *st["swap"], _plan_swap_half(kind), after)
        part = _add_halves("add_halves_" + tag, c_arr, g, got, kind)
        blk = (part.shape[0], part.shape[1] // N_CHIP) if kind == "col" else part.shape[1:]
        send, recv, part, land, thru = _split_start("rs_scatter_start_" + tag, part, (N_CHIP - 1,) + blk, N_CHIP - 1,
                                                    _plan_scatter(kind), thru)
        st["scatter"] = (send, recv, part, land)
        return thru

    def rs_end(st, after):
        tag, kind = st["tag"], st["kind"]
        part, got = _split_wait("rs_scatter_wait_" + tag, *st["scatter"], _plan_scatter(kind), after)
        return _sum_partials("sum_partials_" + tag, j_arr, part, got, kind)

    (dpre,) = _matmul("mm_dhid", dy2, w_mo_f, "nt", s, f, d, (BF16,), extras=(r2,), extra_kinds=("tile",),
                      epilogue=lambda acc, r: (acc * r.astype(F32),))
    (g_mo,) = _matmul("mm_dw_mlp_out", hid, dy2, "tn", f, d, s, (BF16,))
    st_mo, dpre = rs_begin("mo", g_mo.reshape(N_CHIP, f // N_CHIP, d), "row", dpre)
    (dh2,) = _matmul("mm_dh2", dpre, w_mi_f, "nt", s, d, f, (F32,))
    (g_mi,) = _matmul("mm_dw_mlp_in", h2, dpre, "tn", d, f, s, (BF16,))
    dh2 = rs_mid(st_mo, g_mi, dh2)
    st_mi, dh2 = rs_begin("mi", g_mi, "col", dh2)
    dx1, dsh2, dsc2, dg_mlp, dy1, dgt1 = _norm_mod_bwd("norm_mod_bwd2", dh2, x1, rstd2, g_norm_mlp, sc2, dx2,
                                                       gate=(y1, gt1))
    (dmixed,) = _matmul("mm_dmixed", dy1, w_out_f, "nt", s, d, d, (F32,))
    (g_out,) = _matmul("mm_dw_out", mixed, dy1, "tn", d, d, s, (BF16,))
    dmixed = rs_mid(st_mi, g_out, dmixed)
    st_out, dmixed = rs_begin("out", g_out.reshape(N_CHIP, d // N_CHIP, d), "row", dmixed)
    do_attn, do_lru, dg_attn, dg_lru = _mix_norm_bwd(dmixed, o_attn, o_lru, rstd_a, rstd_l, g_attn_out, g_lru_out)
    dq, dk, dv = _attn_bwd(qkv, do_attn, nh)
    do_lru = rs_mid(st_out, dq, do_lru)
    dxr, dxg, dwconv, dbconv, dwa, dba, dwx, dbx, dlam = _lru_bwd(
        xrg, do_lru, hseq, w_conv_full, b_conv, wa3, b_rg_a, wx3, b_rg_x, lru_lambda)
    dproj = jnp.concatenate([dq, dk, dv, dxr, dxg], axis=-1)
    (dh1,) = _matmul("mm_dh1", dproj, w_in_f, "nt", s, d, 5 * aw, (F32,))
    (g_in,) = _matmul("mm_dw_in", h1, dproj, "tn", d, 5 * aw, s, (BF16,))
    st_in, dh1 = rs_begin("in", g_in, "col", dh1)
    grad_x, dsh1, dsc1, dg_mix = _norm_mod_bwd("norm_mod_bwd1", dh1, x2d, rstd1, g_norm_mix, sc1, dx1)

    dmod = jnp.concatenate([dsh1, dsc1, dgt1, dsh2, dsc2, dgt2], axis=-1)
    small_names = ["b_ada", "g_norm_mix", "b_conv", "w_rg_a", "b_rg_a", "w_rg_x", "b_rg_x", "lru_lambda",
                   "g_attn_out", "g_lru_out", "g_norm_mlp", "g_norm_final"]
    small_g = [dmod, dg_mix, dbconv, dwa, dba, dwx, dbx, dlam, dg_attn, dg_lru, dg_mlp, dg_final]
    small_w = [b_ada, g_norm_mix, b_conv, w_rg_a, b_rg_a, w_rg_x, b_rg_x, lru_lambda, g_attn_out, g_lru_out, g_norm_mlp, g_norm_final]
    small_m = [m_b_ada, m_g_norm_mix, m_b_conv, m_w_rg_a, m_b_rg_a, m_w_rg_x, m_b_rg_x, m_lru_lambda, m_g_attn_out, m_g_lru_out, m_g_norm_mlp, m_g_norm_final]
    small_v = [v_b_ada, v_g_norm_mix, v_b_conv, v_w_rg_a, v_b_rg_a, v_w_rg_x, v_b_rg_x, v_lru_lambda, v_g_attn_out, v_g_lru_out, v_g_norm_mlp, v_g_norm_final]
    extra_zero = [jnp.zeros_like(dwconv), jnp.zeros((LANES,), F32)]
    g_slab, g_sizes = _pack(small_g + [dwconv, loss_row])
    w_slab, _ = _pack(small_w + extra_zero)
    m_slab, _ = _pack(small_m + extra_zero)
    v_slab, _ = _pack(small_v + extra_zero)
    g_all = _all_gather_small("comm_gather_small_grads", g_slab)
    gs_slab, ds_slab, ms_slab, vs_slab = _small_reduce_adamw(g_all, w_slab, m_slab, v_slab)
    shapes = [w.shape for w in small_w] + [dwconv.shape, (LANES,)]
    gs = _unpack(gs_slab, g_sizes, shapes)
    ds = _unpack(ds_slab, g_sizes, shapes)
    ms = _unpack(ms_slab, g_sizes, shapes)
    vs = _unpack(vs_slab, g_sizes, shapes)
    small = {n: (gs[i], ds[i], ms[i], vs[i]) for i, n in enumerate(small_names)}
    loss = gs[-1][0]
    g_wconv = lax.dynamic_slice(gs[-2], (0, chip * n_cv), (CONV_TAPS, n_cv))
    d_wconv, m_wconv, v_wconv = _adamw_plain("adamw_conv", w_conv[0], g_wconv, m_w_conv[0], v_w_conv[0])
    small["w_conv"] = (g_wconv[None], d_wconv[None], m_wconv[None], v_wconv[None])

    dmod_all = _unpack(g_all, g_sizes, [(N_MOD * d,)], lead=(N_DEV,))[0]
    dmod_sel = lax.dynamic_slice(dmod_all, (0, chip * n_ada), (N_DEV, n_ada)).astype(BF16)
    act_t = act_all.T.astype(BF16)
    dmod_sel = rs_mid(st_in, gs_slab, dmod_sel)
    big = {"w_ada": _adamw_ada(w_ada[0], m_w_ada[0], v_w_ada[0], act_t, dmod_sel)}

    tok = loss_row

    def reduced_begin(tag, half, tok_):
        send, recv, half, land, tok_ = _split_start("rs_reduced_start_" + tag, half, half.shape, 1, _plan_whole, tok_)
        return (send, recv, half, land), tok_

    def reduced_end(tag, st, after):
        return _split_wait("rs_reduced_wait_" + tag, *st, _plan_whole, after)

    sw_mo, tok = reduced_begin("mo", rs_end(st_mo, big["w_ada"][1]), tok)
    sw_mi, tok = reduced_begin("mi", rs_end(st_mi, tok), tok)
    sw_out, tok = reduced_begin("out", rs_end(st_out, tok), tok)
    half_mo, got_mo = reduced_end("mo", sw_mo, tok)
    big["w_mlp_out"] = _adamw_halves("adamw_w_mlp_out", c_arr, w_mlp_out[0], m_w_mlp_out[0], v_w_mlp_out[0], half_mo, got_mo)
    half_mi, got_mi = reduced_end("mi", sw_mi, big["w_mlp_out"][1])
    big["w_mlp_in"] = _adamw_halves("adamw_w_mlp_in", c_arr, w_mlp_in[0], m_w_mlp_in[0], v_w_mlp_in[0], half_mi, got_mi)
    half_out, got_out = reduced_end("out", sw_out, big["w_mlp_in"][1])
    big["w_out"] = _adamw_halves("adamw_w_out", c_arr, w_out[0], m_w_out[0], v_w_out[0], half_out, got_out)
    half_in = rs_end(st_in, big["w_out"][1])
    (got_in,) = _swap_reduced("comm_swap_reduced_in", [half_in])
    big["w_in"] = _adamw_halves("adamw_w_in", c_arr, w_in[0], m_w_in[0], v_w_in[0], half_in, got_in)

    order = ["w_ada", "b_ada", "g_norm_mix", "w_in", "w_conv", "b_conv", "w_rg_a", "b_rg_a", "w_rg_x", "b_rg_x",
             "lru_lambda", "g_attn_out", "g_lru_out", "w_out", "g_norm_mlp", "w_mlp_in", "w_mlp_out", "g_norm_final"]
    res = {}
    for n in order:
        res[n] = tuple(t[None] for t in big[n]) if n in big else small[n]
    return (loss, grad_x[None],
            *[res[n][0] for n in order], *[res[n][1] for n in order],
            *[res[n][2] for n in order], *[res[n][3] for n in order])
```

```python
import functools
import math

import jax
import jax.numpy as jnp
from jax import lax
from jax.experimental import pallas as pl
from jax.experimental.pallas import tpu as pltpu

F32 = jnp.float32
BF16 = jnp.bfloat16
SDS = jax.ShapeDtypeStruct
MESH = pl.DeviceIdType.MESH

EPS = 1e-6
HEAD = 128
N_MOD = 6
CONV_TAPS = 4
LRU_C = 8.0
ADAM_LR, ADAM_B1, ADAM_B2, ADAM_EPS, ADAM_WD, ADAM_STEP = 0.001, 0.9, 0.999, 1e-08, 0.01, 10
N_DEV = 8
N_CHIP = 4
LANES = 128
SUBLANES = 8
VMEM_LIMIT = 56 * 1024 * 1024
PACK_ROWS = 256
MM_TILE_M, MM_TILE_N, MM_TILE_K = 1024, 1024, 2048


def _tile(dim, pref):
    t = min(dim, pref)
    while dim % t:
        t -= LANES
    return t


def _params(sem=None):
    return pltpu.CompilerParams(dimension_semantics=sem, vmem_limit_bytes=VMEM_LIMIT)


def _sigmoid(x):
    return 1.0 / (1.0 + jnp.exp(-x))


def _log_sigmoid(x):
    return jnp.minimum(x, 0.0) - jnp.log(1.0 + jnp.exp(-jnp.abs(x)))


def _gelu_parts(x):
    k0, k1 = math.sqrt(2.0 / math.pi), 0.044715
    t = jnp.tanh(k0 * (x + k1 * x * x * x))
    val = 0.5 * x * (1.0 + t)
    der = 0.5 * (1.0 + t) + 0.5 * x * (1.0 - t * t) * k0 * (1.0 + 3.0 * k1 * x * x)
    return val, der


def _dot(a, b):
    return jnp.dot(a, b, preferred_element_type=F32)


def _dot_nt(a, b):
    return lax.dot_general(a, b, (((1,), (1,)), ((), ())), preferred_element_type=F32)


def _dot_tn(a, b):
    return lax.dot_general(a, b, (((0,), (0,)), ((), ())), preferred_element_type=F32)


def _split_dot(x, tri):
    hi = x.astype(BF16)
    lo = (x - hi.astype(F32)).astype(BF16)
    return _dot(hi, tri) + _dot(lo, tri)


def _matmul(name, a, b, mode, m, n, k, out_dtypes, *, b_off=0, extras=(), extra_kinds=(), epilogue=None,
            tm=MM_TILE_M, tn=MM_TILE_N, tk=MM_TILE_K):
    tm, tn, tk = _tile(m, tm), _tile(math.gcd(n, b_off) if b_off else n, tn), _tile(k, tk)
    assert b_off % tn == 0
    nk = k // tk
    n_ex, n_out = len(extras), len(out_dtypes)
    dot = {"nn": _dot, "nt": _dot_nt, "tn": _dot_tn}[mode]

    def body(a_ref, b_ref, *rest):
        ex, outs = rest[:n_ex], rest[n_ex:n_ex + n_out]

        def finish(total):
            res = epilogue(total, *[e[...] for e in ex]) if epilogue else (total,)
            for o, r in zip(outs, res):
                o[...] = r.astype(o.dtype)

        if nk == 1:
            finish(dot(a_ref[...], b_ref[...]))
            return
        acc = rest[-1]
        kk = pl.program_id(2)

        @pl.when(kk == 0)
        def _():
            acc[...] = dot(a_ref[...], b_ref[...])

        @pl.when(jnp.logical_and(kk > 0, kk < nk - 1))
        def _():
            acc[...] += dot(a_ref[...], b_ref[...])

        @pl.when(kk == nk - 1)
        def _():
            finish(acc[...] + dot(a_ref[...], b_ref[...]))

    if mode == "nn":
        a_spec = pl.BlockSpec((tm, tk), lambda i, j, kk: (i, kk))
        b_spec = pl.BlockSpec((tk, tn), lambda i, j, kk: (kk, j + b_off // tn))
    elif mode == "nt":
        a_spec = pl.BlockSpec((tm, tk), lambda i, j, kk: (i, kk))
        b_spec = pl.BlockSpec((tn, tk), lambda i, j, kk: (j, kk + b_off // tk))
    else:
        a_spec = pl.BlockSpec((tk, tm), lambda i, j, kk: (kk, i))
        b_spec = pl.BlockSpec((tk, tn), lambda i, j, kk: (kk, j))
    tile_spec = pl.BlockSpec((tm, tn), lambda i, j, kk: (i, j))
    row_spec = pl.BlockSpec((1, tn), lambda i, j, kk: (0, j))
    ex_specs = [tile_spec if kind == "tile" else row_spec for kind in extra_kinds]
    outs = pl.pallas_call(
        body, grid=(m // tm, n // tn, nk),
        in_specs=[a_spec, b_spec] + ex_specs,
        out_specs=tuple(tile_spec for _ in out_dtypes),
        out_shape=tuple(SDS((m, n), dt) for dt in out_dtypes),
        scratch_shapes=[pltpu.VMEM((tm, tn), F32)] if nk > 1 else [],
        compiler_params=_params(("parallel", "parallel", "arbitrary")),
        name=name,
    )(a, b, *extras)
    return outs


def _row_specs(s, d, tr):
    row = pl.BlockSpec((tr, d), lambda i: (i, 0))
    vec = pl.BlockSpec((1, d), lambda i: (0, 0))
    col = pl.BlockSpec((tr, 1), lambda i: (i, 0))
    return row, vec, col


def _norm_mod_fwd(name, x, g, sc, sh):
    s, d = x.shape
    tr = _tile(s, 256)
    row, vec, col = _row_specs(s, d, tr)

    def body(x_ref, g_ref, sc_ref, sh_ref, h_ref, r_ref):
        xv = x_ref[...]
        r = lax.rsqrt(jnp.mean(xv * xv, axis=-1, keepdims=True) + EPS)
        h_ref[...] = ((xv * r * g_ref[...]) * (1.0 + sc_ref[...]) + sh_ref[...]).astype(BF16)
        r_ref[...] = r

    return pl.pallas_call(
        body, grid=(s // tr,), in_specs=[row, vec, vec, vec], out_specs=(row, col),
        out_shape=(SDS((s, d), BF16), SDS((s, 1), F32)),
        compiler_params=_params(("parallel",)), name=name)(x, g, sc, sh)


def _norm_mod_bwd(name, dh, xin, rstd, g, sc, dres, gate=None):
    s, d = xin.shape
    tr = _tile(s, 256)
    row, vec, col = _row_specs(s, d, tr)

    n_gate = 2 if gate is not None else 0

    def body(dh_ref, x_ref, r_ref, g_ref, sc_ref, dres_ref, *rest):
        gate_in, gate_out = rest[:n_gate], rest[n_gate + 4:]
        dx_ref, dsh_ref, dsc_ref, dg_ref = rest[n_gate:n_gate + 4]

        @pl.when(pl.program_id(0) == 0)
        def _():
            for ref in (dsh_ref, dsc_ref, dg_ref) + tuple(gate_out[1:]):
                ref[...] = jnp.zeros_like(ref)

        dh_v, xv, r, gv = dh_ref[...], x_ref[...], r_ref[...], g_ref[...]
        n0 = xv * r
        dsh_ref[...] += jnp.sum(dh_v, axis=0, keepdims=True)
        dsc_ref[...] += jnp.sum(dh_v * (n0 * gv), axis=0, keepdims=True)
        dn = dh_v * (1.0 + sc_ref[...])
        dg_ref[...] += jnp.sum(dn * n0, axis=0, keepdims=True)
        gy = dn * gv
        dot = jnp.mean(gy * xv, axis=-1, keepdims=True)
        dxv = dres_ref[...] + r * gy - xv * (r * r * r * dot)
        dx_ref[...] = dxv
        if gate is not None:
            y_ref, gt_ref = gate_in
            dy_ref, dgt_ref = gate_out
            dy_ref[...] = (gt_ref[...] * dxv).astype(BF16)
            dgt_ref[...] += jnp.sum(dxv * y_ref[...], axis=0, keepdims=True)

    vecs = SDS((1, d), F32)
    gate_args = tuple(gate) if gate is not None else ()
    return pl.pallas_call(
        body, grid=(s // tr,),
        in_specs=[row, row, col, vec, vec, row] + ([row, vec] if gate is not None else []),
        out_specs=(row, vec, vec, vec) + ((row, vec) if gate is not None else ()),
        out_shape=(SDS((s, d), F32), vecs, vecs, vecs) + ((SDS((s, d), BF16), vecs) if gate is not None else ()),
        compiler_params=_params(("arbitrary",)), name=name)(dh, xin, rstd, g, sc, dres, *gate_args)


def _final_loss(x2, gf, tgt, y, gt):
    s, d = x2.shape
    tr = _tile(s, 256)
    row, vec, _ = _row_specs(s, d, tr)
    lrow = pl.BlockSpec((1, LANES), lambda i: (0, 0))

    def body(x_ref, g_ref, t_ref, y_ref, gt_ref, dx_ref, loss_ref, dg_ref, dy_ref, dgt_ref):
        @pl.when(pl.program_id(0) == 0)
        def _():
            loss_ref[...] = jnp.zeros_like(loss_ref)
            dg_ref[...] = jnp.zeros_like(dg_ref)
            dgt_ref[...] = jnp.zeros_like(dgt_ref)

        xv, gv = x_ref[...], g_ref[...]
        r = lax.rsqrt(jnp.mean(xv * xv, axis=-1, keepdims=True) + EPS)
        n0 = xv * r
        err = n0 * gv - t_ref[...]
        loss_ref[...] += jnp.sum(err * err) * (0.5 / d)
        dy = err * (1.0 / d)
        dg_ref[...] += jnp.sum(dy * n0, axis=0, keepdims=True)
        gy = dy * gv
        dot = jnp.mean(gy * xv, axis=-1, keepdims=True)
        dxv = r * gy - xv * (r * r * r * dot)
        dx_ref[...] = dxv
        dy_ref[...] = (gt_ref[...] * dxv).astype(BF16)
        dgt_ref[...] += jnp.sum(dxv * y_ref[...], axis=0, keepdims=True)

    return pl.pallas_call(
        body, grid=(s // tr,), in_specs=[row, vec, row, row, vec], out_specs=(row, lrow, vec, row, vec),
        out_shape=(SDS((s, d), F32), SDS((1, LANES), F32), SDS((1, d), F32), SDS((s, d), BF16), SDS((1, d), F32)),
        compiler_params=_params(("arbitrary",)), name="final_loss")(x2, gf, tgt, y, gt)


def _mix_norm_fwd(oa, ol, ga, gl):
    s, w = oa.shape
    tr = _tile(s, 256)
    row, vec, col = _row_specs(s, w, tr)

    def body(oa_ref, ol_ref, ga_ref, gl_ref, mx_ref, ra_ref, rl_ref):
        a, l = oa_ref[...], ol_ref[...]
        ra = lax.rsqrt(jnp.mean(a * a, axis=-1, keepdims=True) + EPS)
        rl = lax.rsqrt(jnp.mean(l * l, axis=-1, keepdims=True) + EPS)
        mx_ref[:, :w] = (a * ra * ga_ref[...]).astype(BF16)
        mx_ref[:, w:] = (l * rl * gl_ref[...]).astype(BF16)
        ra_ref[...] = ra
        rl_ref[...] = rl

    return pl.pallas_call(
        body, grid=(s // tr,), in_specs=[row, row, vec, vec],
        out_specs=(pl.BlockSpec((tr, 2 * w), lambda i: (i, 0)), col, col),
        out_shape=(SDS((s, 2 * w), BF16), SDS((s, 1), F32), SDS((s, 1), F32)),
        compiler_params=_params(("parallel",)), name="mix_norm_fwd")(oa, ol, ga, gl)


def _mix_norm_bwd(dmx, oa, ol, ra, rl, ga, gl):
    s, w = oa.shape
    tr = _tile(s, 256)
    row, vec, col = _row_specs(s, w, tr)

    def body(dm_ref, oa_ref, ol_ref, ra_ref, rl_ref, ga_ref, gl_ref, doa_ref, dol_ref, dga_ref, dgl_ref):
        @pl.when(pl.program_id(0) == 0)
        def _():
            dga_ref[...] = jnp.zeros_like(dga_ref)
            dgl_ref[...] = jnp.zeros_like(dgl_ref)

        def one(dy, xv, r, gv, dg_ref):
            dg_ref[...] += jnp.sum(dy * (xv * r), axis=0, keepdims=True)
            gy = dy * gv
            dot = jnp.mean(gy * xv, axis=-1, keepdims=True)
            return r * gy - xv * (r * r * r * dot)

        doa_ref[...] = one(dm_ref[:, :w], oa_ref[...], ra_ref[...], ga_ref[...], dga_ref).astype(BF16)
        dol_ref[...] = one(dm_ref[:, w:], ol_ref[...], rl_ref[...], gl_ref[...], dgl_ref)

    return pl.pallas_call(
        body, grid=(s // tr,),
        in_specs=[pl.BlockSpec((tr, 2 * w), lambda i: (i, 0)), row, row, col, col, vec, vec],
        out_specs=(row, row, vec, vec),
        out_shape=(SDS((s, w), BF16), SDS((s, w), F32), SDS((1, w), F32), SDS((1, w), F32)),
        compiler_params=_params(("arbitrary",)), name="mix_norm_bwd")(dmx, oa, ol, ra, rl, ga, gl)


def _attn_blocks(qs, ks, tri_after, csums, causal):
    zs = [_dot_nt(q, k) * (HEAD ** -0.5) for q, k in zip(qs, ks)]
    lbs = [_log_sigmoid(z) for z in zs]
    lss = [lb - z for lb, z in zip(lbs, zs)]
    if causal is not None:
        lss = [jnp.where(causal, ls, 0.0) for ls in lss]
    locs = [_split_dot(ls, tri_after) for ls in lss]
    ws = [jnp.exp(lb + (loc + cs)) for lb, loc, cs in zip(lbs, locs, csums)]
    if causal is not None:
        ws = [jnp.where(causal, w, 0.0) for w in ws]
    nxt = [cs + (loc[:, 0:1] + ls[:, 0:1]) for cs, loc, ls in zip(csums, locs, lss)]
    return lbs, ws, nxt


ATTN_HEADS_PER_STEP = 4


def _attn_tile(s):
    return 256 if s >= 1024 else 128


def _tri(t, after):
    r_i = lax.broadcasted_iota(jnp.int32, (t, t), 0)
    c_i = lax.broadcasted_iota(jnp.int32, (t, t), 1)
    return ((r_i > c_i) if after else (r_i < c_i)).astype(BF16)


def _attn_fwd(qkv, n_heads):
    s = qkv.shape[0]
    t = _attn_tile(s)
    hps = ATTN_HEADS_PER_STEP
    wid = hps * HEAD

    def body(q_ref, k_ref, v_ref, o_ref):
        qi = pl.program_id(1)
        tri_after = _tri(t, True)
        causal = lax.broadcasted_iota(jnp.int32, (t, t), 1) < lax.broadcasted_iota(jnp.int32, (t, t), 0)
        lanes = [slice(a * HEAD, (a + 1) * HEAD) for a in range(hps)]
        qs = [q_ref[:, ln] for ln in lanes]

        def block(off, carry, mask):
            ks = [k_ref[pl.ds(off, t), ln] for ln in lanes]
            _, ws, csums = _attn_blocks(qs, ks, tri_after, [cr[0] for cr in carry], mask)
            os_ = [cr[1] + _dot(w.astype(BF16), v_ref[pl.ds(off, t), ln]) for cr, w, ln in zip(carry, ws, lanes)]
            return tuple(zip(csums, os_))

        zero = tuple((jnp.zeros((t, 1), F32), jnp.zeros((t, HEAD), F32)) for _ in lanes)
        carry = block(pl.multiple_of(qi * t, t), zero, causal)
        carry = lax.fori_loop(1, qi + 1, lambda it, cr: block(pl.multiple_of((qi - it) * t, t), cr, None), carry)
        for a, ln in enumerate(lanes):
            o_ref[:, ln] = carry[a][1]

    hb = n_heads // hps
    return pl.pallas_call(
        body, grid=(hb, s // t),
        in_specs=[pl.BlockSpec((t, wid), lambda hh, i: (i, hh)),
                  pl.BlockSpec((s, wid), lambda hh, i: (0, hb + hh)),
                  pl.BlockSpec((s, wid), lambda hh, i: (0, 2 * hb + hh))],
        out_specs=pl.BlockSpec((t, wid), lambda hh, i: (i, hh)),
        out_shape=SDS((s, n_heads * HEAD), F32),
        compiler_params=_params(("parallel", "parallel")), name="attn_fwd")(qkv, qkv, qkv)


def _attn_bwd(qkv, do, n_heads):
    s = qkv.shape[0]
    t = _attn_tile(s)
    nq = s // t
    scale = HEAD ** -0.5
    hps = ATTN_HEADS_PER_STEP
    wid = hps * HEAD

    def body(q_ref, k_ref, v_ref, do_ref, dq_ref, dk_ref, dv_ref, e_s, sg_s, dk_acc, dv_acc):
        qi = pl.program_id(1)

        @pl.when(qi == 0)
        def _():
            dk_acc[...] = jnp.zeros_like(dk_acc)
            dv_acc[...] = jnp.zeros_like(dv_acc)

        tri_after = _tri(t, True)
        tri_before = _tri(t, False)
        causal = lax.broadcasted_iota(jnp.int32, (t, t), 1) < lax.broadcasted_iota(jnp.int32, (t, t), 0)
        lanes = [slice(a * HEAD, (a + 1) * HEAD) for a in range(hps)]
        qs = [q_ref[:, ln] for ln in lanes]
        douts = [do_ref[:, ln] for ln in lanes]

        def pass1(kb, csums, mask):
            off = pl.multiple_of(kb * t, t)
            ks = [k_ref[pl.ds(off, t), ln] for ln in lanes]
            dws = [_dot_nt(dout, v_ref[pl.ds(off, t), ln]) for dout, ln in zip(douts, lanes)]
            lbs, ws, nxt = _attn_blocks(qs, ks, tri_after, csums, mask)
            for a, ln in enumerate(lanes):
                e_s[a, kb] = dws[a] * ws[a]
                sg_s[a, kb] = jnp.exp(lbs[a])
            for a, ln in enumerate(lanes):
                dv_acc[pl.ds(off, t), ln] += _dot_tn(ws[a].astype(BF16), douts[a])
            return tuple(nxt)

        cs = pass1(qi, tuple(jnp.zeros((t, 1), F32) for _ in lanes), causal)
        lax.fori_loop(1, qi + 1, lambda it, c_: pass1(qi - it, c_, None), cs)

        def pass2(kb, carry, mask):
            off = pl.multiple_of(kb * t, t)
            es = [e_s[a, kb] for a in range(hps)]
            locs = [_split_dot(e, tri_before) for e in es]
            stays = [(loc + cr[0]) * sg_s[a, kb] for a, (loc, cr) in enumerate(zip(locs, carry))]
            if mask is not None:
                stays = [jnp.where(mask, st, 0.0) for st in stays]
            dzbs = [((e * (1.0 - sg_s[a, kb]) - st) * scale).astype(BF16) for a, (e, st) in enumerate(zip(es, stays))]
            dqs = [cr[1] + _dot(dzb, k_ref[pl.ds(off, t), ln]) for cr, dzb, ln in zip(carry, dzbs, lanes)]
            for a, ln in enumerate(lanes):
                dk_acc[pl.ds(off, t), ln] += _dot_tn(dzbs[a], qs[a])
            esums = [cr[0] + (loc[:, t - 1:t] + e[:, t - 1:t]) for cr, loc, e in zip(carry, locs, es)]
            return tuple(zip(esums, dqs))

        zero = tuple((jnp.zeros((t, 1), F32), jnp.zeros((t, HEAD), F32)) for _ in lanes)
        carry = lax.fori_loop(0, qi, lambda kb, cr: pass2(kb, cr, None), zero)
        carry = pass2(qi, carry, causal)
        for a, ln in enumerate(lanes):
            dq_ref[:, ln] = carry[a][1].astype(BF16)

        @pl.when(qi == nq - 1)
        def _():
            dk_ref[...] = dk_acc[...].astype(BF16)
            dv_ref[...] = dv_acc[...].astype(BF16)

    hb = n_heads // hps
    blk = pl.BlockSpec((t, wid), lambda hh, i: (i, hh))
    full = pl.BlockSpec((s, wid), lambda hh, i: (0, hh))
    return pl.pallas_call(
        body, grid=(hb, nq),
        in_specs=[blk,
                  pl.BlockSpec((s, wid), lambda hh, i: (0, hb + hh)),
                  pl.BlockSpec((s, wid), lambda hh, i: (0, 2 * hb + hh)),
                  blk],
        out_specs=(blk, full, full),
        out_shape=(SDS((s, n_heads * HEAD), BF16),) * 3,
        scratch_shapes=[pltpu.VMEM((hps, nq, t, t), F32), pltpu.VMEM((hps, nq, t, t), F32),
                        pltpu.VMEM((s, wid), F32), pltpu.VMEM((s, wid), F32)],
        compiler_params=_params(("parallel", "arbitrary")), name="attn_bwd")(qkv, qkv, qkv, do)


def _lru_chunk(s):
    return 256 if s >= 1024 else 128


def _lru_gates(xc, wa, ba, wx, bx, sp):
    xb = xc.astype(BF16)
    r = _sigmoid(_dot(xb, wa) + ba)
    ig = _sigmoid(_dot(xb, wx) + bx)
    la = -LRU_C * r * sp
    a = jnp.exp(la)
    t = jnp.tanh(la)
    mult = jnp.sqrt(-2.0 * t / (1.0 - t))
    return r, ig, a, mult


def _softplus_neg(lam):
    return jnp.maximum(-lam, 0.0) + jnp.log(1.0 + jnp.exp(-jnp.abs(lam)))


LRU_BLOCKS_PER_STEP = 1


def _lru_specs(s, n_blocks):
    bps = min(LRU_BLOCKS_PER_STEP, n_blocks)
    wid = bps * HEAD
    seq0 = pl.BlockSpec((s, wid), lambda h: (0, h))
    seq1 = pl.BlockSpec((s, wid), lambda h: (0, n_blocks // bps + h))
    taps = pl.BlockSpec((CONV_TAPS, wid), lambda h: (0, h))
    vec = pl.BlockSpec((1, wid), lambda h: (0, h))
    mat = pl.BlockSpec((bps, HEAD, HEAD), lambda h: (h, 0, 0))
    return bps, seq0, seq1, taps, vec, mat


def _per_block(one_block, n_2d, n_mat_pos, bps):
    def body(*refs):
        for a in range(bps):
            views = [r.at[a] if i in n_mat_pos else r.at[:, pl.ds(a * HEAD, HEAD)] for i, r in enumerate(refs[:n_2d])]
            one_block(*views, *refs[n_2d:])
    return body


def _lru_fwd(xrg, wconv, bconv, wa, ba, wx, bx, lam):
    s = xrg.shape[0]
    nb = wa.shape[0]
    tc = _lru_chunk(s)
    bps, seq0, seq1, taps, vec, mat = _lru_specs(s, nb)
    pad = SUBLANES

    def one_block(xr_ref, xg_ref, wc_ref, bc_ref, wa_ref, ba_ref, wx_ref, bx_ref, lam_ref, o_ref, h_ref, pad_s, a_s, u_s):
        pad_s[0:pad, :] = jnp.zeros((pad, HEAD), F32)
        pad_s[pad:pad + s, :] = xr_ref[...]
        wab, wxb = wa_ref[...].astype(BF16), wx_ref[...].astype(BF16)
        sp = _softplus_neg(lam_ref[...])
        for c in range(s // tc):
            base = c * tc
            xc = bc_ref[...] + sum(wc_ref[i:i + 1, :] * pad_s[pl.ds(base + pad - (CONV_TAPS - 1) + i, tc), :]
                                   for i in range(CONV_TAPS))
            _, ig, a, mult = _lru_gates(xc, wab, ba_ref[...], wxb, bx_ref[...], sp)
            a_s[base:base + tc, :] = a
            u_s[base:base + tc, :] = mult * (ig * xc)

        row = lax.broadcasted_iota(jnp.int32, (SUBLANES, HEAD), 0)

        def chunk(ci, hprev):
            off = pl.multiple_of(ci * SUBLANES, SUBLANES)
            a8, b8 = a_s[pl.ds(off, SUBLANES), :], u_s[pl.ds(off, SUBLANES), :]
            for d in (1, 2, 4):
                a_sh = jnp.where(row < d, 1.0, pltpu.roll(a8, d, 0))
                b_sh = jnp.where(row < d, 0.0, pltpu.roll(b8, d, 0))
                b8 = a8 * b_sh + b8
                a8 = a8 * a_sh
            h8 = a8 * hprev + b8
            h_ref[pl.ds(off, SUBLANES), :] = h8
            return h8[SUBLANES - 1:SUBLANES, :]

        lax.fori_loop(0, s // SUBLANES, chunk, jnp.zeros((1, HEAD), F32), unroll=8)
        for c in range(s // tc):
            sl = slice(c * tc, (c + 1) * tc)
            gel, _ = _gelu_parts(xg_ref[sl, :])
            o_ref[sl, :] = h_ref[sl, :] * gel

    return pl.pallas_call(
        _per_block(one_block, 11, (4, 6), bps), grid=(nb // bps,),
        in_specs=[seq0, seq1, taps, vec, mat, vec, mat, vec, vec],
        out_specs=(seq0, seq0),
        out_shape=(SDS((s, nb * HEAD), F32), SDS((s, nb * HEAD), F32)),
        scratch_shapes=[pltpu.VMEM((s + pad, HEAD), F32), pltpu.VMEM((s, HEAD), F32), pltpu.VMEM((s, HEAD), F32)],
        compiler_params=_params(("parallel",)), name="lru_fwd")(xrg, xrg, wconv, bconv, wa, ba, wx, bx, lam)


def _lru_bwd(xrg, dol, hseq, wconv, bconv, wa, ba, wx, bx, lam):
    s = xrg.shape[0]
    nb = wa.shape[0]
    tc = _lru_chunk(s)
    bps, seq0, seq1, taps, vec, mat = _lru_specs(s, nb)
    pad = SUBLANES

    def one_block(xr_ref, xg_ref, do_ref, h_ref, wc_ref, bc_ref, wa_ref, ba_ref, wx_ref, bx_ref, lam_ref,
             dxr_ref, dxg_ref, dwc_ref, dbc_ref, dwa_ref, dba_ref, dwx_ref, dbx_ref, dlam_ref,
             pad_s, hp_s, a_s, g_s, da_s, dxc_s):
        pad_s[0:pad, :] = jnp.zeros((pad, HEAD), F32)
        pad_s[pad:pad + s, :] = xr_ref[...]
        hp_s[0:pad, :] = jnp.zeros((pad, HEAD), F32)
        hp_s[pad:pad + s, :] = h_ref[...]
        a_s[s:s + pad, :] = jnp.zeros((pad, HEAD), F32)
        dxc_s[s:s + pad, :] = jnp.zeros((pad, HEAD), F32)
        wab, wxb = wa_ref[...].astype(BF16), wx_ref[...].astype(BF16)
        lam_v = lam_ref[...]
        sp = _softplus_neg(lam_v)

        def conv_in(c):
            base = c * tc
            wins = [pad_s[pl.ds(base + pad - (CONV_TAPS - 1) + i, tc), :] for i in range(CONV_TAPS)]
            xc = bc_ref[...] + sum(wc_ref[i:i + 1, :] * wins[i] for i in range(CONV_TAPS))
            return xc, wins

        for c in range(s // tc):
            sl = slice(c * tc, (c + 1) * tc)
            xc, _ = conv_in(c)
            _, _, a, _ = _lru_gates(xc, wab, ba_ref[...], wxb, bx_ref[...], sp)
            a_s[sl, :] = a
            gel, dgel = _gelu_parts(xg_ref[sl, :])
            dov = do_ref[sl, :]
            g_s[sl, :] = dov * gel
            dxg_ref[sl, :] = (dov * h_ref[sl, :] * dgel).astype(BF16)

        row = lax.broadcasted_iota(jnp.int32, (SUBLANES, HEAD), 0)
        n_chunks = s // SUBLANES

        def chunk(it, gnext):
            ci = n_chunks - 1 - it
            off = pl.multiple_of(ci * SUBLANES, SUBLANES)
            a8 = a_s[pl.ds(off, SUBLANES), :]
            a8n = a_s[pl.ds(off + SUBLANES, SUBLANES), :]
            c8 = pltpu.roll(jnp.where(row == 0, a8n, a8), SUBLANES - 1, 0)
            g8 = g_s[pl.ds(off, SUBLANES), :]
            for d in (1, 2, 4):
                c_sh = jnp.where(row >= SUBLANES - d, 1.0, pltpu.roll(c8, SUBLANES - d, 0))
                g_sh = jnp.where(row >= SUBLANES - d, 0.0, pltpu.roll(g8, SUBLANES - d, 0))
                g8 = c8 * g_sh + g8
                c8 = c8 * c_sh
            g8 = g8 + c8 * gnext
            g_s[pl.ds(off, SUBLANES), :] = g8
            h8 = hp_s[pl.ds(off + pad, SUBLANES), :]
            h8p = hp_s[pl.ds(off, SUBLANES), :]
            da_s[pl.ds(off, SUBLANES), :] = g8 * pltpu.roll(jnp.where(row == SUBLANES - 1, h8p, h8), 1, 0)
            return g8[0:1, :]

        lax.fori_loop(0, n_chunks, chunk, jnp.zeros((1, HEAD), F32), unroll=8)

        dsp = jnp.zeros((1, HEAD), F32)
        dbc = jnp.zeros((1, HEAD), F32)
        dba = jnp.zeros((1, HEAD), F32)
        dbx = jnp.zeros((1, HEAD), F32)
        dwa = jnp.zeros((HEAD, HEAD), F32)
        dwx = jnp.zeros((HEAD, HEAD), F32)
        dwc = [jnp.zeros((1, HEAD), F32) for _ in range(CONV_TAPS)]
        for c in range(s // tc):
            sl = slice(c * tc, (c + 1) * tc)
            xc, wins = conv_in(c)
            r, ig, a, mult = _lru_gates(xc, wab, ba_ref[...], wxb, bx_ref[...], sp)
            du, da = g_s[sl, :], da_s[sl, :]
            d_ix = du * mult
            dla = da * a - (du * ig * xc) * (a * a / mult)
            dsp = dsp + jnp.sum(dla * r, axis=0, keepdims=True) * (-LRU_C)
            dpa = (dla * (-LRU_C * sp)) * r * (1.0 - r)
            dpx = (d_ix * xc) * ig * (1.0 - ig)
            dpab, dpxb, xb = dpa.astype(BF16), dpx.astype(BF16), xc.astype(BF16)
            dxc = d_ix * ig + _dot_nt(dpab, wab) + _dot_nt(dpxb, wxb)
            dwa = dwa + _dot_tn(xb, dpab)
            dwx = dwx + _dot_tn(xb, dpxb)
            dba = dba + jnp.sum(dpa, axis=0, keepdims=True)
            dbx = dbx + jnp.sum(dpx, axis=0, keepdims=True)
            dbc = dbc + jnp.sum(dxc, axis=0, keepdims=True)
            for i in range(CONV_TAPS):
                dwc[i] = dwc[i] + jnp.sum(dxc * wins[i], axis=0, keepdims=True)
            dxc_s[sl, :] = dxc

        for c in range(s // tc):
            base = c * tc
            dxr = sum(wc_ref[i:i + 1, :] * dxc_s[pl.ds(base + (CONV_TAPS - 1) - i, tc), :] for i in range(CONV_TAPS))
            dxr_ref[base:base + tc, :] = dxr.astype(BF16)

        for i in range(CONV_TAPS):
            dwc_ref[i:i + 1, :] = dwc[i]
        dbc_ref[...] = dbc
        dwa_ref[...] = dwa
        dwx_ref[...] = dwx
        dba_ref[...] = dba
        dbx_ref[...] = dbx
        dlam_ref[...] = dsp * (-_sigmoid(-lam_v))

    w = nb * HEAD
    return pl.pallas_call(
        _per_block(one_block, 20, (6, 8, 15, 17), bps), grid=(nb // bps,),
        in_specs=[seq0, seq1, seq0, seq0, taps, vec, mat, vec, mat, vec, vec],
        out_specs=(seq0, seq0, taps, vec, mat, vec, mat, vec, vec),
        out_shape=(SDS((s, w), BF16), SDS((s, w), BF16), SDS((CONV_TAPS, w), F32), SDS((1, w), F32),
                   SDS((nb, HEAD, HEAD), F32), SDS((1, w), F32), SDS((nb, HEAD, HEAD), F32), SDS((1, w), F32),
                   SDS((1, w), F32)),
        scratch_shapes=[pltpu.VMEM((s + pad, HEAD), F32), pltpu.VMEM((s + pad, HEAD), F32),
                        pltpu.VMEM((s + pad, HEAD), F32), pltpu.VMEM((s, HEAD), F32),
                        pltpu.VMEM((s, HEAD), F32), pltpu.VMEM((s + pad, HEAD), F32)],
        compiler_params=_params(("parallel",)), name="lru_bwd",
    )(xrg, xrg, dol, hseq, wconv, bconv, wa, ba, wx, bx, lam)


def _ada_mod(c_all, w_sh, b_sh):
    n_ex, d = c_all.shape
    n = w_sh.shape[1]
    tn = _tile(n, 512)

    def body(c_ref, w_ref, b_ref, mod_ref, act_ref):
        cv = c_ref[...]
        act = cv * _sigmoid(cv)
        act_ref[...] = act
        mod_ref[...] = _dot(act.astype(BF16), w_ref[...].astype(BF16)) + b_ref[...]

    return pl.pallas_call(
        body, grid=(n // tn,),
        in_specs=[pl.BlockSpec((n_ex, d), lambda j: (0, 0)), pl.BlockSpec((d, tn), lambda j: (0, j)),
                  pl.BlockSpec((1, tn), lambda j: (0, j))],
        out_specs=(pl.BlockSpec((n_ex, tn), lambda j: (0, j)), pl.BlockSpec((n_ex, d), lambda j: (0, 0))),
        out_shape=(SDS((n_ex, n), F32), SDS((n_ex, d), F32)),
        compiler_params=_params(("arbitrary",)), name="ada_mod")(c_all, w_sh, b_sh)


def _adamw_math(w, g, m, v):
    m = ADAM_B1 * m + (1.0 - ADAM_B1) * g
    v = ADAM_B2 * v + (1.0 - ADAM_B2) * (g * g)
    m_hat = m / (1.0 - ADAM_B1 ** ADAM_STEP)
    v_hat = v / (1.0 - ADAM_B2 ** ADAM_STEP)
    delta = -ADAM_LR * (m_hat / (jnp.sqrt(v_hat) + ADAM_EPS) + ADAM_WD * w)
    return delta, m, v


def _adamw_plain(name, w, g, m, v):
    def body(w_ref, g_ref, m_ref, v_ref, d_ref, mo_ref, vo_ref):
        d_ref[...], mo_ref[...], vo_ref[...] = _adamw_math(w_ref[...], g_ref[...], m_ref[...], v_ref[...])

    return pl.pallas_call(body, out_shape=(SDS(w.shape, F32),) * 3, name=name)(w, g, m, v)


def _adamw_halves(name, c_arr, w, m, v, g_own, g_recv):
    r, n = w.shape
    rh = r // 2
    tr = _tile(rh, 256)
    nh = rh // tr

    def body(c_ref, w_ref, m_ref, v_ref, go_ref, gr_ref, g_ref, d_ref, mo_ref, vo_ref):
        own = (pl.program_id(0) // nh) == c_ref[0]
        g = jnp.where(own, go_ref[...], gr_ref[...])
        g_ref[...] = g
        d_ref[...], mo_ref[...], vo_ref[...] = _adamw_math(w_ref[...], g, m_ref[...], v_ref[...])

    full = pl.BlockSpec((tr, n), lambda i, c: (i, 0))
    own = pl.BlockSpec((tr, n), lambda i, c: (jnp.where(i // nh == c[0], i % nh, 0), 0))
    recv = pl.BlockSpec((tr, n), lambda i, c: (jnp.where(i // nh == c[0], 0, i % nh), 0))
    return pl.pallas_call(
        body,
        grid_spec=pltpu.PrefetchScalarGridSpec(
            num_scalar_prefetch=1, grid=(2 * nh,), in_specs=[full, full, full, own, recv],
            out_specs=(full,) * 4),
        out_shape=(SDS((r, n), F32),) * 4,
        compiler_params=_params(("parallel",)), name=name)(c_arr, w, m, v, g_own, g_recv)


def _adamw_ada(w, m, v, act_t, dmod):
    d, n = w.shape
    n_ex = act_t.shape[1]
    tr = _tile(d, 256)

    def body(a_ref, dm_ref, w_ref, m_ref, v_ref, g_ref, d_ref, mo_ref, vo_ref):
        g = _dot(a_ref[...], dm_ref[...])
        g_ref[...] = g
        d_ref[...], mo_ref[...], vo_ref[...] = _adamw_math(w_ref[...], g, m_ref[...], v_ref[...])

    full = pl.BlockSpec((tr, n), lambda i: (i, 0))
    return pl.pallas_call(
        body, grid=(d // tr,),
        in_specs=[pl.BlockSpec((tr, n_ex), lambda i: (i, 0)), pl.BlockSpec((n_ex, n), lambda i: (0, 0)), full, full, full],
        out_specs=(full,) * 4, out_shape=(SDS((d, n), F32),) * 4,
        compiler_params=_params(("parallel",)), name="adamw_ada")(act_t, dmod, w, m, v)


def _small_reduce_adamw(parts, w, m, v):
    n_dev, r, _ = parts.shape
    tr = r if r <= PACK_ROWS else PACK_ROWS

    def body(p_ref, w_ref, m_ref, v_ref, g_ref, d_ref, mo_ref, vo_ref):
        g = p_ref[0]
        for k in range(1, n_dev):
            g = g + p_ref[k]
        g_ref[...] = g
        d_ref[...], mo_ref[...], vo_ref[...] = _adamw_math(w_ref[...], g, m_ref[...], v_ref[...])

    full = pl.BlockSpec((tr, LANES), lambda i: (i, 0))
    return pl.pallas_call(
        body, grid=(r // tr,),
        in_specs=[pl.BlockSpec((n_dev, tr, LANES), lambda i: (0, i, 0)), full, full, full],
        out_specs=(full,) * 4, out_shape=(SDS((r, LANES), F32),) * 4,
        compiler_params=_params(("parallel",)), name="small_reduce_adamw")(parts, w, m, v)


def _mesh_pos():
    return lax.axis_index("x"), lax.axis_index("y"), lax.axis_index("c")


def _other_chips(x, y):
    return [(1 - x, y), (x, 1 - y), (1 - x, 1 - y)]


def _all_gather_small(name, blk):
    r, n = blk.shape

    def body(x_ref, out_ref, send_sems, recv_sems, local_sem):
        x, y, c = _mesh_pos()
        me, sibling = (x, y, c), (x, y, 1 - c)
        chips = _other_chips(x, y)

        def rows(px, py, pc):
            return out_ref.at[4 * px + 2 * py + pc]

        def copy(k, block, to, src=None):
            return pltpu.make_async_remote_copy(
                src_ref=rows(*block) if src is None else src, dst_ref=rows(*block),
                send_sem=send_sems.at[k], recv_sem=recv_sems.at[k], device_id=to, device_id_type=MESH)

        mine = pltpu.make_async_copy(x_ref, rows(*me), local_sem)
        mine.start()
        first = [copy(0, me, sibling, src=x_ref)]
        first += [copy(1 + j, me, (*chip, c), src=x_ref) for j, chip in enumerate(chips)]
        for cp in first:
            cp.start()
        passed = [copy(4 + j, (*chip, c), sibling) for j, chip in enumerate(chips)]
        for j, chip in enumerate(chips):
            copy(1 + j, (*chip, c), me).wait_recv()
            passed[j].start()
        copy(0, sibling, me).wait_recv()
        for j, chip in enumerate(chips):
            copy(4 + j, (*chip, 1 - c), me).wait_recv()
        for cp in first + passed:
            cp.wait_send()
        mine.wait()

    return pl.pallas_call(
        body, out_shape=SDS((N_DEV, r, n), blk.dtype),
        in_specs=[pl.BlockSpec(memory_space=pltpu.VMEM)], out_specs=pl.BlockSpec(memory_space=pltpu.VMEM),
        scratch_shapes=[pltpu.SemaphoreType.DMA((7,)), pltpu.SemaphoreType.DMA((7,)), pltpu.SemaphoreType.DMA],
        compiler_params=pltpu.CompilerParams(vmem_limit_bytes=VMEM_LIMIT), name=name)(blk)


_ANY = pl.BlockSpec(memory_space=pl.ANY)
_HBM = pl.BlockSpec(memory_space=pltpu.HBM)
_SEM = pl.BlockSpec(memory_space=pltpu.SEMAPHORE)
_EFFECT = pltpu.SideEffectType.DATAFLOW_SIDE_EFFECTING


def _hbm(a):
    return pltpu.with_memory_space_constraint(a, pltpu.HBM)


def _place_cast(name, j_arr, shard, kind, after):
    r, n = shard.shape
    tr = _tile(r, 256)
    nr = r // tr
    if kind == "col":
        out_shape, o_spec = (r, N_CHIP * n), pl.BlockSpec((tr, n), lambda i, j: (i, j[0]))
    else:
        out_shape, o_spec = (N_CHIP * r, n), pl.BlockSpec((tr, n), lambda i, j: (j[0] * nr + i, 0))

    def body(j_ref, s_ref, after_ref, o_ref):
        o_ref[...] = s_ref[...].astype(BF16)

    return pl.pallas_call(
        body,
        grid_spec=pltpu.PrefetchScalarGridSpec(
            num_scalar_prefetch=1, grid=(nr,), in_specs=[pl.BlockSpec((tr, n), lambda i, j: (i, 0)), _ANY],
            out_specs=o_spec),
        out_shape=SDS(out_shape, BF16), compiler_params=_params(("parallel",)), name=name)(j_arr, shard, after)


def _gather_start(name, fulls, kinds, groups, after):
    nw = len(fulls)
    ng = len(groups)

    def body(*refs):
        outs = refs[nw + 1:]
        send, recv, full, token = outs[:ng], outs[ng:2 * ng], outs[2 * ng:2 * ng + nw], outs[2 * ng + nw]
        x, y, c = _mesh_pos()
        for g, ws in enumerate(groups):
            for li, w in enumerate(ws):
                mine = _full_region(full[w], kinds[w], x, y, c)
                for k, chip in enumerate(_other_chips(x, y)):
                    pltpu.make_async_remote_copy(
                        src_ref=mine, dst_ref=mine, send_sem=send[g].at[3 * li + k], recv_sem=recv[g].at[3 * li + k],
                        device_id=(*chip, c), device_id_type=MESH).start()
        token[...] = jnp.zeros_like(token)

    sems = tuple(pltpu.SemaphoreType.DMA((3 * len(ws),)) for ws in groups)
    outs = pl.pallas_call(
        body,
        out_shape=sems + sems + tuple(pltpu.HBM(f_.shape, f_.dtype) for f_ in fulls) + (SDS((SUBLANES, LANES), F32),),
        in_specs=[_HBM] * nw + [_ANY],
        out_specs=tuple([_SEM] * (2 * ng) + [_HBM] * nw + [pl.BlockSpec(memory_space=pltpu.VMEM)]),
        input_output_aliases={w: 2 * ng + w for w in range(nw)},
        compiler_params=pltpu.CompilerParams(has_side_effects=_EFFECT),
        name=name,
    )(*[_hbm(f_) for f_ in fulls], after)
    return outs[:ng], outs[ng:2 * ng], outs[2 * ng:2 * ng + nw], outs[2 * ng + nw]


def _full_region(full, kind, px, py, half):
    j = 2 * px + py
    if kind == "col":
        rh, cols = full.shape[0] // 2, full.shape[1] // N_CHIP
        return full.at[pl.ds(half * rh, rh), pl.ds(j * cols, cols)]
    rows = full.shape[0] // N_CHIP
    rh = rows // 2
    return full.at[pl.ds(j * rows + half * rh, rh), :]


def _gather_pass(name, fulls, kinds, send, recv, after, thru):
    nw = len(fulls)

    def body(*refs):
        send_r, recv_r = refs[nw], refs[nw + 1]
        full, fsend, frecv = refs[nw + 4:2 * nw + 4], refs[2 * nw + 4], refs[2 * nw + 5]
        x, y, c = _mesh_pos()
        chips = _other_chips(x, y)
        for w in range(nw):
            for k, chip in enumerate(chips):
                landed = _full_region(full[w], kinds[w], *chip, c)
                arrive = pltpu.make_async_remote_copy(
                    src_ref=_full_region(full[w], kinds[w], x, y, c), dst_ref=landed, send_sem=send_r.at[3 * w + k],
                    recv_sem=recv_r.at[3 * w + k], device_id=(*chip, c), device_id_type=MESH)
                arrive.wait_recv()
                arrive.wait_send()
                pltpu.make_async_remote_copy(
                    src_ref=landed, dst_ref=landed, send_sem=fsend.at[3 * w + k], recv_sem=frecv.at[3 * w + k],
                    device_id=(x, y, 1 - c), device_id_type=MESH).start()

    sem = pltpu.SemaphoreType.DMA((3 * nw,))
    outs = pl.pallas_call(
        body,
        out_shape=tuple(pltpu.HBM(f_.shape, f_.dtype) for f_ in fulls) + (sem, sem, SDS(thru.shape, thru.dtype)),
        in_specs=[_HBM] * nw + [_SEM, _SEM, _ANY, _ANY],
        out_specs=tuple([_HBM] * nw + [_SEM, _SEM, _ANY]),
        input_output_aliases={**{w: w for w in range(nw)}, nw + 3: nw + 2},
        compiler_params=pltpu.CompilerParams(has_side_effects=_EFFECT),
        name=name,
    )(*fulls, send, recv, after, thru)
    return outs[:nw], outs[nw], outs[nw + 1], outs[nw + 2]


def _gather_finish(name, fulls, kinds, fsend, frecv, after):
    nw = len(fulls)

    def body(*refs):
        fsend_r, frecv_r = refs[nw], refs[nw + 1]
        full = refs[nw + 3:2 * nw + 3]
        x, y, c = _mesh_pos()
        chips = _other_chips(x, y)
        for w in range(nw):
            for k, chip in enumerate(chips):
                cp = pltpu.make_async_remote_copy(
                    src_ref=_full_region(full[w], kinds[w], *chip, c), dst_ref=_full_region(full[w], kinds[w], *chip, 1 - c),
                    send_sem=fsend_r.at[3 * w + k], recv_sem=frecv_r.at[3 * w + k],
                    device_id=(x, y, 1 - c), device_id_type=MESH)
                cp.wait_send()
                cp.wait_recv()

    outs = pl.pallas_call(
        body,
        out_shape=tuple(pltpu.HBM(f_.shape, f_.dtype) for f_ in fulls),
        in_specs=[_HBM] * nw + [_SEM, _SEM, _ANY],
        out_specs=tuple([_HBM] * nw),
        input_output_aliases={w: w for w in range(nw)},
        compiler_params=pltpu.CompilerParams(has_side_effects=_EFFECT),
        name=name,
    )(*fulls, fsend, frecv, after)
    return list(outs)


def _half_of(ref, kind, half):
    if kind == "col":
        rh = ref.shape[0] // 2
        return ref.at[pl.ds(half * rh, rh), :]
    rh = ref.shape[1] // 2
    return ref.at[:, pl.ds(half * rh, rh), :]


def _half_shape(g, kind):
    if kind == "col":
        return (g.shape[0] // 2, g.shape[1])
    return (g.shape[0], g.shape[1] // 2, g.shape[2])


def _plan_swap_half(kind):
    def plan(src, land, x, y, c):
        return [(_half_of(src, kind, 1 - c), land, (x, y, 1 - c))]
    return plan


def _plan_scatter(kind):
    def plan(src, land, x, y, c):
        out = []
        for k, (px, py) in enumerate(_other_chips(x, y)):
            j = 2 * px + py
            if kind == "col":
                n = src.shape[1] // N_CHIP
                blk = src.at[:, pl.ds(j * n, n)]
            else:
                blk = src.at[j]
            out.append((blk, land.at[k], (px, py, c)))
        return out
    return plan


def _plan_whole(src, land, x, y, c):
    return [(src, land, (x, y, 1 - c))]


def _split_start(name, src, land_shape, n, plan, thru):
    def body(src_in, land_in, thru_in, send, recv, src_ref, land_ref, thru_out):
        x, y, c = _mesh_pos()
        for k, (s_, d_, dev) in enumerate(plan(src_ref, land_ref, x, y, c)):
            pltpu.make_async_remote_copy(src_ref=s_, dst_ref=d_, send_sem=send.at[k], recv_sem=recv.at[k],
                                         device_id=dev, device_id_type=MESH).start()

    sem = pltpu.SemaphoreType.DMA((n,))
    return pl.pallas_call(
        body,
        out_shape=(sem, sem, pltpu.HBM(src.shape, src.dtype), pltpu.HBM(land_shape, src.dtype), SDS(thru.shape, thru.dtype)),
        in_specs=[_HBM, _HBM, _ANY], out_specs=(_SEM, _SEM, _HBM, _HBM, _ANY),
        input_output_aliases={0: 2, 1: 3, 2: 4},
        compiler_params=pltpu.CompilerParams(has_side_effects=_EFFECT), name=name,
    )(_hbm(src), _hbm(lax.empty(land_shape, src.dtype)), thru)


def _split_wait(name, send, recv, src, land, plan, after):
    def body(src_in, land_in, send_r, recv_r, after_r, src_ref, land_ref):
        x, y, c = _mesh_pos()
        for k, (s_, d_, dev) in enumerate(plan(src_ref, land_ref, x, y, c)):
            cp = pltpu.make_async_remote_copy(src_ref=s_, dst_ref=d_, send_sem=send_r.at[k], recv_sem=recv_r.at[k],
                                              device_id=dev, device_id_type=MESH)
            cp.wait_send()
            cp.wait_recv()

    return pl.pallas_call(
        body,
        out_shape=(pltpu.HBM(src.shape, src.dtype), pltpu.HBM(land.shape, land.dtype)),
        in_specs=[_HBM, _HBM, _SEM, _SEM, _ANY], out_specs=(_HBM, _HBM),
        input_output_aliases={0: 0, 1: 1},
        compiler_params=pltpu.CompilerParams(has_side_effects=_EFFECT), name=name,
    )(src, land, send, recv, after)


def _dev_row(buf, px, py, pc):
    return buf.at[4 * px + 2 * py + pc]


def _plan_gather_own(buf, land, x, y, c):
    own = _dev_row(buf, x, y, c)
    return [(own, own, (x, y, 1 - c))] + [(own, own, (px, py, c)) for px, py in _other_chips(x, y)]


def _plan_gather_pass(buf, land, x, y, c):
    return [(_dev_row(buf, px, py, c), _dev_row(buf, px, py, c), (x, y, 1 - c)) for px, py in _other_chips(x, y)]


def _split_start_inplace(name, buf, n, plan, thru):
    def body(buf_in, thru_in, send, recv, buf_ref, thru_out):
        x, y, c = _mesh_pos()
        for k, (s_, d_, dev) in enumerate(plan(buf_ref, buf_ref, x, y, c)):
            pltpu.make_async_remote_copy(src_ref=s_, dst_ref=d_, send_sem=send.at[k], recv_sem=recv.at[k],
                                         device_id=dev, device_id_type=MESH).start()

    sem = pltpu.SemaphoreType.DMA((n,))
    return pl.pallas_call(
        body, out_shape=(sem, sem, pltpu.HBM(buf.shape, buf.dtype), SDS(thru.shape, thru.dtype)),
        in_specs=[_HBM, _ANY], out_specs=(_SEM, _SEM, _HBM, _ANY), input_output_aliases={0: 2, 1: 3},
        compiler_params=pltpu.CompilerParams(has_side_effects=_EFFECT), name=name)(_hbm(buf), thru)


def _split_wait_inplace(name, send, recv, buf, plan, after):
    def body(buf_in, send_r, recv_r, after_r, buf_ref):
        x, y, c = _mesh_pos()
        for k, (s_, d_, dev) in enumerate(plan(buf_ref, buf_ref, x, y, c)):
            cp = pltpu.make_async_remote_copy(src_ref=s_, dst_ref=d_, send_sem=send_r.at[k], recv_sem=recv_r.at[k],
                                              device_id=dev, device_id_type=MESH)
            cp.wait_send()
            cp.wait_recv()

    return pl.pallas_call(
        body, out_shape=pltpu.HBM(buf.shape, buf.dtype), in_specs=[_HBM, _SEM, _SEM, _ANY], out_specs=_HBM,
        input_output_aliases={0: 0}, compiler_params=pltpu.CompilerParams(has_side_effects=_EFFECT), name=name,
    )(buf, send, recv, after)


def _place_row(name, me_arr, slab):
    r, n = slab.shape
    tr = r if r <= PACK_ROWS else PACK_ROWS

    def body(me_ref, s_ref, o_ref):
        o_ref[...] = s_ref[...]

    return pl.pallas_call(
        body,
        grid_spec=pltpu.PrefetchScalarGridSpec(
            num_scalar_prefetch=1, grid=(r // tr,), in_specs=[pl.BlockSpec((tr, n), lambda i, me: (i, 0))],
            out_specs=pl.BlockSpec((None, tr, n), lambda i, me: (me[0], i, 0))),
        out_shape=SDS((N_DEV, r, n), slab.dtype), compiler_params=_params(("parallel",)), name=name)(me_arr, slab)


def _add_halves(name, c_arr, g, got, kind):
    if kind == "col":
        rh, n = got.shape
        tr = _tile(rh, 256)
        nh = rh // tr
        g_spec = pl.BlockSpec((tr, n), lambda i, c: (c[0] * nh + i, 0))
        o_spec = pl.BlockSpec((tr, n), lambda i, c: (i, 0))
        grid = (nh,)
    else:
        nc, rh, n = got.shape
        tr = _tile(rh, 256)
        nh = rh // tr
        g_spec = pl.BlockSpec((None, tr, n), lambda j, i, c: (j, c[0] * nh + i, 0))
        o_spec = pl.BlockSpec((None, tr, n), lambda j, i, c: (j, i, 0))
        grid = (nc, nh)

    def body(c_ref, g_ref, r_ref, o_ref):
        o_ref[...] = (g_ref[...].astype(F32) + r_ref[...].astype(F32)).astype(o_ref.dtype)

    return pl.pallas_call(
        body,
        grid_spec=pltpu.PrefetchScalarGridSpec(num_scalar_prefetch=1, grid=grid, in_specs=[g_spec, o_spec], out_specs=o_spec),
        out_shape=SDS(got.shape, got.dtype),
        compiler_params=_params(("parallel",) * len(grid)), name=name)(c_arr, g, got)


def _sum_partials(name, j_arr, part, got, kind):
    _, rh, n = got.shape
    tr = _tile(rh, 256)
    if kind == "col":
        p_spec = pl.BlockSpec((tr, n), lambda i, j: (i, j[0]))
    else:
        p_spec = pl.BlockSpec((None, tr, n), lambda i, j: (j[0], i, 0))

    def body(j_ref, p_ref, r_ref, o_ref):
        o_ref[...] = ((p_ref[...].astype(F32) + r_ref[0].astype(F32)) + r_ref[1].astype(F32)) + r_ref[2].astype(F32)

    return pl.pallas_call(
        body,
        grid_spec=pltpu.PrefetchScalarGridSpec(
            num_scalar_prefetch=1, grid=(rh // tr,),
            in_specs=[p_spec, pl.BlockSpec((3, tr, n), lambda i, j: (0, i, 0))],
            out_specs=pl.BlockSpec((tr, n), lambda i, j: (i, 0))),
        out_shape=SDS((rh, n), F32),
        compiler_params=_params(("parallel",)), name=name)(j_arr, part, got)


def _swap_reduced(name, halves):
    nw = len(halves)

    def body(*refs):
        h, got = refs[:nw], refs[nw:2 * nw]
        send_sems, recv_sems = refs[2 * nw:]
        x, y, c = _mesh_pos()
        cps = []
        for w in range(nw):
            cp = pltpu.make_async_remote_copy(
                src_ref=h[w], dst_ref=got[w], send_sem=send_sems.at[w], recv_sem=recv_sems.at[w],
                device_id=(x, y, 1 - c), device_id_type=MESH)
            cp.start()
            cps.append(cp)
        for cp in cps:
            cp.wait()

    return pl.pallas_call(
        body, out_shape=tuple(SDS(h.shape, h.dtype) for h in halves),
        in_specs=[_ANY] * nw, out_specs=tuple([_ANY] * nw),
        scratch_shapes=[pltpu.SemaphoreType.DMA((nw,)), pltpu.SemaphoreType.DMA((nw,))],
        name=name)(*halves)


def _pack(arrays):
    flat = [a.reshape(-1).astype(F32) for a in arrays]
    flat = [jnp.pad(f, (0, (-f.shape[0]) % LANES)) for f in flat]
    sizes = [f.shape[0] for f in flat]
    total = sum(sizes)
    rows = total // LANES
    tail = LANES * ((-rows) % (PACK_ROWS if rows > PACK_ROWS else SUBLANES))
    if tail:
        flat.append(jnp.zeros((tail,), F32))
    return jnp.concatenate(flat).reshape(-1, LANES), sizes


def _unpack(slab, sizes, shapes, lead=()):
    flat = slab.reshape(lead + (-1,))
    out, off = [], 0
    for sz, shp in zip(sizes, shapes):
        n = math.prod(shp)
        out.append(flat[..., off:off + n].reshape(lead + tuple(shp)))
        off += sz
    return out


def kernel(x, c, w_ada, b_ada, g_norm_mix, w_in, w_conv, b_conv, w_rg_a, b_rg_a, w_rg_x, b_rg_x, lru_lambda, g_attn_out, g_lru_out, w_out, g_norm_mlp, w_mlp_in, w_mlp_out, g_norm_final, loss_target, m_w_ada, m_b_ada, m_g_norm_mix, m_w_in, m_w_conv, m_b_conv, m_w_rg_a, m_b_rg_a, m_w_rg_x, m_b_rg_x, m_lru_lambda, m_g_attn_out, m_g_lru_out, m_w_out, m_g_norm_mlp, m_w_mlp_in, m_w_mlp_out, m_g_norm_final, v_w_ada, v_b_ada, v_g_norm_mix, v_w_in, v_w_conv, v_b_conv, v_w_rg_a, v_b_rg_a, v_w_rg_x, v_b_rg_x, v_lru_lambda, v_g_attn_out, v_g_lru_out, v_w_out, v_g_norm_mlp, v_w_mlp_in, v_w_mlp_out, v_g_norm_final):
    s, d = x.shape[1], x.shape[2]
    aw = d // 2
    nh = aw // HEAD
    f = w_mlp_out.shape[1] * N_CHIP
    n_ada = w_ada.shape[2]
    n_cv = w_conv.shape[2]
    ix, iy, ic = lax.axis_index("x"), lax.axis_index("y"), lax.axis_index("c")
    chip = 2 * ix + iy
    me = 2 * chip + ic
    c_arr = jnp.reshape(ic, (1,)).astype(jnp.int32)
    j_arr = jnp.reshape(chip, (1,)).astype(jnp.int32)

    x2d, tgt = x[0], loss_target[0]

    slab, sizes = _pack([c, w_conv])
    gathered = _all_gather_small("comm_gather_cond", slab)
    c_parts, cv_parts = _unpack(gathered, sizes, [(d,), (CONV_TAPS, n_cv)], lead=(N_DEV,))
    c_all = c_parts
    w_conv_full = jnp.concatenate([cv_parts[2 * j] for j in range(N_CHIP)], axis=-1)
    b_sh = lax.dynamic_slice(b_ada, (0, chip * n_ada), (1, n_ada))
    mod_part, act_all = _ada_mod(c_all, w_ada[0], b_sh)
    mod_g = _all_gather_small("comm_gather_mod", mod_part.reshape(-1, LANES))
    mod_g = mod_g.reshape(N_DEV, N_DEV, n_ada)
    mod = jnp.concatenate([lax.dynamic_index_in_dim(mod_g[2 * j], me, 0, keepdims=True) for j in range(N_CHIP)], axis=-1)
    sh1, sc1, gt1, sh2, sc2, gt2 = [mod[:, k * d:(k + 1) * d] for k in range(N_MOD)]

    kinds = ("col", "row", "col", "row")
    groups = ((0,), (1, 2), (3,))
    big_w = (w_in, w_out, w_mlp_in, w_mlp_out)
    placed = _place_cast("place_cast_0", j_arr, w_in[0], kinds[0], mod)
    send_a, recv_a, full_a, token = _gather_start("gather_start_a", [placed], kinds[:1], groups[:1], mod)
    placed = [_place_cast("place_cast_%d" % i, j_arr, big_w[i][0], kinds[i], token) for i in (1, 2, 3)]
    send_b, recv_b, full_b, _ = _gather_start("gather_start_b", placed, kinds[1:], ((0, 1), (2,)), token)
    ag_send, ag_recv, ag_full = send_a + send_b, recv_a + recv_b, full_a + full_b

    def gather_pass(g, after, thru):
        ws = groups[g]
        return _gather_pass("gather_pass_%d" % g, [ag_full[w] for w in ws], [kinds[w] for w in ws],
                            ag_send[g], ag_recv[g], after, thru)

    fl, fs, fr, sh1 = gather_pass(0, mod, sh1)
    h1, rstd1 = _norm_mod_fwd("norm_mod_fwd1", x2d, g_norm_mix, sc1, sh1)
    (w_in_f,) = _gather_finish("gather_finish_0", fl, kinds[0:1], fs, fr, h1)
    (qkv,) = _matmul("mm_qkv", h1, w_in_f, "nn", s, 3 * aw, d, (BF16,))
    (xrg,) = _matmul("mm_xrg", h1, w_in_f, "nn", s, 2 * aw, d, (F32,), b_off=3 * aw)
    o_attn = _attn_fwd(qkv, nh)
    fl, fs, fr, xrg = gather_pass(1, o_attn, xrg)
    wa3, wx3 = w_rg_a[0], w_rg_x[0]
    o_lru, hseq = _lru_fwd(xrg, w_conv_full, b_conv, wa3, b_rg_a, wx3, b_rg_x, lru_lambda)
    mixed, rstd_a, rstd_l = _mix_norm_fwd(o_attn, o_lru, g_attn_out, g_lru_out)
    w_out_f, w_mi_f = _gather_finish("gather_finish_1", fl, kinds[1:3], fs, fr, mixed)

    def residual(acc, xin, gt):
        return acc, xin + gt * acc

    y1, x1 = _matmul("mm_out", mixed, w_out_f, "nn", s, d, d, (F32, F32), extras=(x2d, gt1),
                     extra_kinds=("tile", "row"), epilogue=residual)
    h2, rstd2 = _norm_mod_fwd("norm_mod_fwd2", x1, g_norm_mlp, sc2, sh2)

    def sq_relu(acc):
        r = jnp.maximum(acc, 0.0)
        return 2.0 * r, r * r

    r2, hid = _matmul("mm_mlp_in", h2, w_mi_f, "nn", s, f, d, (BF16, BF16), epilogue=sq_relu)
    fl, fs, fr, hid = gather_pass(2, r2, hid)
    (w_mo_f,) = _gather_finish("gather_finish_2", fl, kinds[3:4], fs, fr, hid)
    y2, x2 = _matmul("mm_mlp_out", hid, w_mo_f, "nn", s, d, f, (F32, F32), extras=(x1, gt2),
                     extra_kinds=("tile", "row"), epilogue=residual)
    dx2, loss_row, dg_final, dy2, dgt2 = _final_loss(x2, g_norm_final.reshape(1, d), tgt, y2, gt2)

    def rs_begin(tag, g, kind, thru):
        send, recv, g, land, thru = _split_start("rs_swap_start_" + tag, g, _half_shape(g, kind), 1,
                                                 _plan_swap_half(kind), thru)
        return {"tag": tag, "kind": kind, "swap": (send, recv, g, land)}, thru

    def rs_mid(st, after, thru):
        tag, kind = st["tag"], st["kind"]
        g, got = _split_wait("rs_swap_wait_" + tag, *st["swap"], _plan_swap_half(kind), after)
        part = _add_halves("add_halves_" + tag, c_arr, g, got, kind)
        blk = (part.shape[0], part.shape[1] // N_CHIP) if kind == "col" else part.shape[1:]
        send, recv, part, land, thru = _split_start("rs_scatter_start_" + tag, part, (N_CHIP - 1,) + blk, N_CHIP - 1,
                                                    _plan_scatter(kind), thru)
        st["scatter"] = (send, recv, part, land)
        return thru

    def rs_end(st, after):
        tag, kind = st["tag"], st["kind"]
        part, got = _split_wait("rs_scatter_wait_" + tag, *st["scatter"], _plan_scatter(kind), after)
        return _sum_partials("sum_partials_" + tag, j_arr, part, got, kind)

    (dpre,) = _matmul("mm_dhid", dy2, w_mo_f, "nt", s, f, d, (BF16,), extras=(r2,), extra_kinds=("tile",),
                      epilogue=lambda acc, r: (acc * r.astype(F32),))
    (g_mo,) = _matmul("mm_dw_mlp_out", hid, dy2, "tn", f, d, s, (BF16,))
    st_mo, dpre = rs_begin("mo", g_mo.reshape(N_CHIP, f // N_CHIP, d), "row", dpre)
    (dh2,) = _matmul("mm_dh2", dpre, w_mi_f, "nt", s, d, f, (F32,))
    (g_mi,) = _matmul("mm_dw_mlp_in", h2, dpre, "tn", d, f, s, (BF16,))
    dh2 = rs_mid(st_mo, g_mi, dh2)
    st_mi, dh2 = rs_begin("mi", g_mi, "col", dh2)
    dx1, dsh2, dsc2, dg_mlp, dy1, dgt1 = _norm_mod_bwd("norm_mod_bwd2", dh2, x1, rstd2, g_norm_mlp, sc2, dx2,
                                                       gate=(y1, gt1))
    (dmixed,) = _matmul("mm_dmixed", dy1, w_out_f, "nt", s, d, d, (F32,))
    (g_out,) = _matmul("mm_dw_out", mixed, dy1, "tn", d, d, s, (BF16,))
    dmixed = rs_mid(st_mi, g_out, dmixed)
    st_out, dmixed = rs_begin("out", g_out.reshape(N_CHIP, d // N_CHIP, d), "row", dmixed)
    do_attn, do_lru, dg_attn, dg_lru = _mix_norm_bwd(dmixed, o_attn, o_lru, rstd_a, rstd_l, g_attn_out, g_lru_out)
    dq, dk, dv = _attn_bwd(qkv, do_attn, nh)
    do_lru = rs_mid(st_out, dq, do_lru)
    dxr, dxg, dwconv, dbconv, dwa, dba, dwx, dbx, dlam = _lru_bwd(
        xrg, do_lru, hseq, w_conv_full, b_conv, wa3, b_rg_a, wx3, b_rg_x, lru_lambda)
    dproj = jnp.concatenate([dq, dk, dv, dxr, dxg], axis=-1)
    (dh1,) = _matmul("mm_dh1", dproj, w_in_f, "nt", s, d, 5 * aw, (F32,))
    (g_in,) = _matmul("mm_dw_in", h1, dproj, "tn", d, 5 * aw, s, (BF16,))
    st_in, dh1 = rs_begin("in", g_in, "col", dh1)
    grad_x, dsh1, dsc1, dg_mix = _norm_mod_bwd("norm_mod_bwd1", dh1, x2d, rstd1, g_norm_mix, sc1, dx1)

    dmod = jnp.concatenate([dsh1, dsc1, dgt1, dsh2, dsc2, dgt2], axis=-1)
    small_names = ["b_ada", "g_norm_mix", "b_conv", "w_rg_a", "b_rg_a", "w_rg_x", "b_rg_x", "lru_lambda",
                   "g_attn_out", "g_lru_out", "g_norm_mlp", "g_norm_final"]
    small_g = [dmod, dg_mix, dbconv, dwa, dba, dwx, dbx, dlam, dg_attn, dg_lru, dg_mlp, dg_final]
    small_w = [b_ada, g_norm_mix, b_conv, w_rg_a, b_rg_a, w_rg_x, b_rg_x, lru_lambda, g_attn_out, g_lru_out, g_norm_mlp, g_norm_final]
    small_m = [m_b_ada, m_g_norm_mix, m_b_conv, m_w_rg_a, m_b_rg_a, m_w_rg_x, m_b_rg_x, m_lru_lambda, m_g_attn_out, m_g_lru_out, m_g_norm_mlp, m_g_norm_final]
    small_v = [v_b_ada, v_g_norm_mix, v_b_conv, v_w_rg_a, v_b_rg_a, v_w_rg_x, v_b_rg_x, v_lru_lambda, v_g_attn_out, v_g_lru_out, v_g_norm_mlp, v_g_norm_final]
    extra_zero = [jnp.zeros_like(dwconv), jnp.zeros((LANES,), F32)]
    g_slab, g_sizes = _pack(small_g + [dwconv, loss_row])
    w_slab, _ = _pack(small_w + extra_zero)
    m_slab, _ = _pack(small_m + extra_zero)
    v_slab, _ = _pack(small_v + extra_zero)
    me_arr = jnp.reshape(me, (1,)).astype(jnp.int32)
    g_buf = _place_row("place_small_grads", me_arr, g_slab)
    sg_send, sg_recv, g_buf, tok = _split_start_inplace("sg_gather_start", g_buf, N_CHIP, _plan_gather_own, loss_row)
    tok = rs_mid(st_in, tok, tok)

    def reduced_begin(tag, half, tok_):
        send, recv, half, land, tok_ = _split_start("rs_reduced_start_" + tag, half, half.shape, 1, _plan_whole, tok_)
        return (send, recv, half, land), tok_

    def reduced_end(tag, st, after):
        return _split_wait("rs_reduced_wait_" + tag, *st, _plan_whole, after)

    sw_mo, tok = reduced_begin("mo", rs_end(st_mo, tok), tok)
    sw_mi, tok = reduced_begin("mi", rs_end(st_mi, tok), tok)
    sw_out, tok = reduced_begin("out", rs_end(st_out, tok), tok)
    g_buf = _split_wait_inplace("sg_gather_wait", sg_send, sg_recv, g_buf, _plan_gather_own, tok)
    sg_send, sg_recv, g_buf, tok = _split_start_inplace("sg_pass_start", g_buf, N_CHIP - 1, _plan_gather_pass, tok)
    half_mo, got_mo = reduced_end("mo", sw_mo, tok)
    big = {"w_mlp_out": _adamw_halves("adamw_w_mlp_out", c_arr, w_mlp_out[0], m_w_mlp_out[0], v_w_mlp_out[0],
                                      half_mo, got_mo)}
    g_all = _split_wait_inplace("sg_pass_wait", sg_send, sg_recv, g_buf, _plan_gather_pass, big["w_mlp_out"][1])
    gs_slab, ds_slab, ms_slab, vs_slab = _small_reduce_adamw(g_all, w_slab, m_slab, v_slab)
    shapes = [w.shape for w in small_w] + [dwconv.shape, (LANES,)]
    gs = _unpack(gs_slab, g_sizes, shapes)
    ds = _unpack(ds_slab, g_sizes, shapes)
    ms = _unpack(ms_slab, g_sizes, shapes)
    vs = _unpack(vs_slab, g_sizes, shapes)
    small = {n: (gs[i], ds[i], ms[i], vs[i]) for i, n in enumerate(small_names)}
    loss = gs[-1][0]
    g_wconv = lax.dynamic_slice(gs[-2], (0, chip * n_cv), (CONV_TAPS, n_cv))
    d_wconv, m_wconv, v_wconv = _adamw_plain("adamw_conv", w_conv[0], g_wconv, m_w_conv[0], v_w_conv[0])
    small["w_conv"] = (g_wconv[None], d_wconv[None], m_wconv[None], v_wconv[None])

    dmod_all = g_all[:, :N_MOD * d // LANES, :].reshape(N_DEV, N_MOD * d)
    dmod_sel = lax.dynamic_slice(dmod_all, (0, chip * n_ada), (N_DEV, n_ada)).astype(BF16)
    act_t = act_all.T.astype(BF16)
    big["w_ada"] = _adamw_ada(w_ada[0], m_w_ada[0], v_w_ada[0], act_t, dmod_sel)

    half_mi, got_mi = reduced_end("mi", sw_mi, big["w_ada"][1])
    big["w_mlp_in"] = _adamw_halves("adamw_w_mlp_in", c_arr, w_mlp_in[0], m_w_mlp_in[0], v_w_mlp_in[0], half_mi, got_mi)
    half_out, got_out = reduced_end("out", sw_out, big["w_mlp_in"][1])
    big["w_out"] = _adamw_halves("adamw_w_out", c_arr, w_out[0], m_w_out[0], v_w_out[0], half_out, got_out)
    half_in = rs_end(st_in, big["w_out"][1])
    (got_in,) = _swap_reduced("comm_swap_reduced_in", [half_in])
    big["w_in"] = _adamw_halves("adamw_w_in", c_arr, w_in[0], m_w_in[0], v_w_in[0], half_in, got_in)

    order = ["w_ada", "b_ada", "g_norm_mix", "w_in", "w_conv", "b_conv", "w_rg_a", "b_rg_a", "w_rg_x", "b_rg_x",
             "lru_lambda", "g_attn_out", "g_lru_out", "w_out", "g_norm_mlp", "w_mlp_in", "w_mlp_out", "g_norm_final"]
    res = {}
    for n in order:
        res[n] = tuple(t[None] for t in big[n]) if n in big else small[n]
    return (loss, grad_x[None],
            *[res[n][0] for n in order], *[res[n][1] for n in order],
            *[res[n][2] for n in order], *[res[n][3] for n in order])
```

```python
import functools
import math

import jax
import jax.numpy as jnp
from jax import lax
from jax.experimental import pallas as pl
from jax.experimental.pallas import tpu as pltpu

F32 = jnp.float32
BF16 = jnp.bfloat16
SDS = jax.ShapeDtypeStruct
MESH = pl.DeviceIdType.MESH

EPS = 1e-6
HEAD = 128
N_MOD = 6
CONV_TAPS = 4
LRU_C = 8.0
ADAM_LR, ADAM_B1, ADAM_B2, ADAM_EPS, ADAM_WD, ADAM_STEP = 0.001, 0.9, 0.999, 1e-08, 0.01, 10
N_DEV = 8
N_CHIP = 4
LANES = 128
SUBLANES = 8
VMEM_LIMIT = 56 * 1024 * 1024
PACK_ROWS = 256
MM_TILE_M, MM_TILE_N, MM_TILE_K = 1024, 1024, 2048


def _tile(dim, pref):
    t = min(dim, pref)
    while dim % t:
        t -= LANES
    return t


def _params(sem=None):
    return pltpu.CompilerParams(dimension_semantics=sem, vmem_limit_bytes=VMEM_LIMIT)


def _sigmoid(x):
    return 1.0 / (1.0 + jnp.exp(-x))


def _log_sigmoid(x):
    return jnp.minimum(x, 0.0) - jnp.log(1.0 + jnp.exp(-jnp.abs(x)))


def _gelu_parts(x):
    k0, k1 = math.sqrt(2.0 / math.pi), 0.044715
    t = jnp.tanh(k0 * (x + k1 * x * x * x))
    val = 0.5 * x * (1.0 + t)
    der = 0.5 * (1.0 + t) + 0.5 * x * (1.0 - t * t) * k0 * (1.0 + 3.0 * k1 * x * x)
    return val, der


def _dot(a, b):
    return jnp.dot(a, b, preferred_element_type=F32)


def _dot_nt(a, b):
    return lax.dot_general(a, b, (((1,), (1,)), ((), ())), preferred_element_type=F32)


def _dot_tn(a, b):
    return lax.dot_general(a, b, (((0,), (0,)), ((), ())), preferred_element_type=F32)


def _split_dot(x, tri):
    hi = x.astype(BF16)
    lo = (x - hi.astype(F32)).astype(BF16)
    return _dot(hi, tri) + _dot(lo, tri)


def _matmul(name, a, b, mode, m, n, k, out_dtypes, *, b_off=0, extras=(), extra_kinds=(), epilogue=None,
            tm=MM_TILE_M, tn=MM_TILE_N, tk=MM_TILE_K):
    tm, tn, tk = _tile(m, tm), _tile(math.gcd(n, b_off) if b_off else n, tn), _tile(k, tk)
    assert b_off % tn == 0
    nk = k // tk
    n_ex, n_out = len(extras), len(out_dtypes)
    dot = {"nn": _dot, "nt": _dot_nt, "tn": _dot_tn}[mode]

    def body(a_ref, b_ref, *rest):
        ex, outs = rest[:n_ex], rest[n_ex:n_ex + n_out]

        def finish(total):
            res = epilogue(total, *[e[...] for e in ex]) if epilogue else (total,)
            for o, r in zip(outs, res):
                o[...] = r.astype(o.dtype)

        if nk == 1:
            finish(dot(a_ref[...], b_ref[...]))
            return
        acc = rest[-1]
        kk = pl.program_id(2)

        @pl.when(kk == 0)
        def _():
            acc[...] = dot(a_ref[...], b_ref[...])

        @pl.when(jnp.logical_and(kk > 0, kk < nk - 1))
        def _():
            acc[...] += dot(a_ref[...], b_ref[...])

        @pl.when(kk == nk - 1)
        def _():
            finish(acc[...] + dot(a_ref[...], b_ref[...]))

    if mode == "nn":
        a_spec = pl.BlockSpec((tm, tk), lambda i, j, kk: (i, kk))
        b_spec = pl.BlockSpec((tk, tn), lambda i, j, kk: (kk, j + b_off // tn))
    elif mode == "nt":
        a_spec = pl.BlockSpec((tm, tk), lambda i, j, kk: (i, kk))
        b_spec = pl.BlockSpec((tn, tk), lambda i, j, kk: (j, kk + b_off // tk))
    else:
        a_spec = pl.BlockSpec((tk, tm), lambda i, j, kk: (kk, i))
        b_spec = pl.BlockSpec((tk, tn), lambda i, j, kk: (kk, j))
    tile_spec = pl.BlockSpec((tm, tn), lambda i, j, kk: (i, j))
    row_spec = pl.BlockSpec((1, tn), lambda i, j, kk: (0, j))
    ex_specs = [tile_spec if kind == "tile" else row_spec for kind in extra_kinds]
    outs = pl.pallas_call(
        body, grid=(m // tm, n // tn, nk),
        in_specs=[a_spec, b_spec] + ex_specs,
        out_specs=tuple(tile_spec for _ in out_dtypes),
        out_shape=tuple(SDS((m, n), dt) for dt in out_dtypes),
        scratch_shapes=[pltpu.VMEM((tm, tn), F32)] if nk > 1 else [],
        compiler_params=_params(("parallel", "parallel", "arbitrary")),
        name=name,
    )(a, b, *extras)
    return outs


def _row_specs(s, d, tr):
    row = pl.BlockSpec((tr, d), lambda i: (i, 0))
    vec = pl.BlockSpec((1, d), lambda i: (0, 0))
    col = pl.BlockSpec((tr, 1), lambda i: (i, 0))
    return row, vec, col


def _norm_mod_fwd(name, x, g, sc, sh):
    s, d = x.shape
    tr = _tile(s, 256)
    row, vec, col = _row_specs(s, d, tr)

    def body(x_ref, g_ref, sc_ref, sh_ref, h_ref, r_ref):
        xv = x_ref[...]
        r = lax.rsqrt(jnp.mean(xv * xv, axis=-1, keepdims=True) + EPS)
        h_ref[...] = ((xv * r * g_ref[...]) * (1.0 + sc_ref[...]) + sh_ref[...]).astype(BF16)
        r_ref[...] = r

    return pl.pallas_call(
        body, grid=(s // tr,), in_specs=[row, vec, vec, vec], out_specs=(row, col),
        out_shape=(SDS((s, d), BF16), SDS((s, 1), F32)),
        compiler_params=_params(("parallel",)), name=name)(x, g, sc, sh)


def _norm_mod_bwd(name, dh, xin, rstd, g, sc, dres, gate=None):
    s, d = xin.shape
    tr = _tile(s, 256)
    row, vec, col = _row_specs(s, d, tr)

    n_gate = 2 if gate is not None else 0

    def body(dh_ref, x_ref, r_ref, g_ref, sc_ref, dres_ref, *rest):
        gate_in, gate_out = rest[:n_gate], rest[n_gate + 4:]
        dx_ref, dsh_ref, dsc_ref, dg_ref = rest[n_gate:n_gate + 4]

        @pl.when(pl.program_id(0) == 0)
        def _():
            for ref in (dsh_ref, dsc_ref, dg_ref) + tuple(gate_out[1:]):
                ref[...] = jnp.zeros_like(ref)

        dh_v, xv, r, gv = dh_ref[...], x_ref[...], r_ref[...], g_ref[...]
        n0 = xv * r
        dsh_ref[...] += jnp.sum(dh_v, axis=0, keepdims=True)
        dsc_ref[...] += jnp.sum(dh_v * (n0 * gv), axis=0, keepdims=True)
        dn = dh_v * (1.0 + sc_ref[...])
        dg_ref[...] += jnp.sum(dn * n0, axis=0, keepdims=True)
        gy = dn * gv
        dot = jnp.mean(gy * xv, axis=-1, keepdims=True)
        dxv = dres_ref[...] + r * gy - xv * (r * r * r * dot)
        dx_ref[...] = dxv
        if gate is not None:
            y_ref, gt_ref = gate_in
            dy_ref, dgt_ref = gate_out
            dy_ref[...] = (gt_ref[...] * dxv).astype(BF16)
            dgt_ref[...] += jnp.sum(dxv * y_ref[...], axis=0, keepdims=True)

    vecs = SDS((1, d), F32)
    gate_args = tuple(gate) if gate is not None else ()
    return pl.pallas_call(
        body, grid=(s // tr,),
        in_specs=[row, row, col, vec, vec, row] + ([row, vec] if gate is not None else []),
        out_specs=(row, vec, vec, vec) + ((row, vec) if gate is not None else ()),
        out_shape=(SDS((s, d), F32), vecs, vecs, vecs) + ((SDS((s, d), BF16), vecs) if gate is not None else ()),
        compiler_params=_params(("arbitrary",)), name=name)(dh, xin, rstd, g, sc, dres, *gate_args)


def _final_loss(x2, gf, tgt, y, gt):
    s, d = x2.shape
    tr = _tile(s, 256)
    row, vec, _ = _row_specs(s, d, tr)
    lrow = pl.BlockSpec((1, LANES), lambda i: (0, 0))

    def body(x_ref, g_ref, t_ref, y_ref, gt_ref, dx_ref, loss_ref, dg_ref, dy_ref, dgt_ref):
        @pl.when(pl.program_id(0) == 0)
        def _():
            loss_ref[...] = jnp.zeros_like(loss_ref)
            dg_ref[...] = jnp.zeros_like(dg_ref)
            dgt_ref[...] = jnp.zeros_like(dgt_ref)

        xv, gv = x_ref[...], g_ref[...]
        r = lax.rsqrt(jnp.mean(xv * xv, axis=-1, keepdims=True) + EPS)
        n0 = xv * r
        err = n0 * gv - t_ref[...]
        loss_ref[...] += jnp.sum(err * err) * (0.5 / d)
        dy = err * (1.0 / d)
        dg_ref[...] += jnp.sum(dy * n0, axis=0, keepdims=True)
        gy = dy * gv
        dot = jnp.mean(gy * xv, axis=-1, keepdims=True)
        dxv = r * gy - xv * (r * r * r * dot)
        dx_ref[...] = dxv
        dy_ref[...] = (gt_ref[...] * dxv).astype(BF16)
        dgt_ref[...] += jnp.sum(dxv * y_ref[...], axis=0, keepdims=True)

    return pl.pallas_call(
        body, grid=(s // tr,), in_specs=[row, vec, row, row, vec], out_specs=(row, lrow, vec, row, vec),
        out_shape=(SDS((s, d), F32), SDS((1, LANES), F32), SDS((1, d), F32), SDS((s, d), BF16), SDS((1, d), F32)),
        compiler_params=_params(("arbitrary",)), name="final_loss")(x2, gf, tgt, y, gt)


def _mix_norm_fwd(oa, ol, ga, gl):
    s, w = oa.shape
    tr = _tile(s, 256)
    row, vec, col = _row_specs(s, w, tr)

    def body(oa_ref, ol_ref, ga_ref, gl_ref, mx_ref, ra_ref, rl_ref):
        a, l = oa_ref[...], ol_ref[...]
        ra = lax.rsqrt(jnp.mean(a * a, axis=-1, keepdims=True) + EPS)
        rl = lax.rsqrt(jnp.mean(l * l, axis=-1, keepdims=True) + EPS)
        mx_ref[:, :w] = (a * ra * ga_ref[...]).astype(BF16)
        mx_ref[:, w:] = (l * rl * gl_ref[...]).astype(BF16)
        ra_ref[...] = ra
        rl_ref[...] = rl

    return pl.pallas_call(
        body, grid=(s // tr,), in_specs=[row, row, vec, vec],
        out_specs=(pl.BlockSpec((tr, 2 * w), lambda i: (i, 0)), col, col),
        out_shape=(SDS((s, 2 * w), BF16), SDS((s, 1), F32), SDS((s, 1), F32)),
        compiler_params=_params(("parallel",)), name="mix_norm_fwd")(oa, ol, ga, gl)


def _mix_norm_bwd(dmx, oa, ol, ra, rl, ga, gl):
    s, w = oa.shape
    tr = _tile(s, 256)
    row, vec, col = _row_specs(s, w, tr)

    def body(dm_ref, oa_ref, ol_ref, ra_ref, rl_ref, ga_ref, gl_ref, doa_ref, dol_ref, dga_ref, dgl_ref):
        @pl.when(pl.program_id(0) == 0)
        def _():
            dga_ref[...] = jnp.zeros_like(dga_ref)
            dgl_ref[...] = jnp.zeros_like(dgl_ref)

        def one(dy, xv, r, gv, dg_ref):
            dg_ref[...] += jnp.sum(dy * (xv * r), axis=0, keepdims=True)
            gy = dy * gv
            dot = jnp.mean(gy * xv, axis=-1, keepdims=True)
            return r * gy - xv * (r * r * r * dot)

        doa_ref[...] = one(dm_ref[:, :w], oa_ref[...], ra_ref[...], ga_ref[...], dga_ref).astype(BF16)
        dol_ref[...] = one(dm_ref[:, w:], ol_ref[...], rl_ref[...], gl_ref[...], dgl_ref)

    return pl.pallas_call(
        body, grid=(s // tr,),
        in_specs=[pl.BlockSpec((tr, 2 * w), lambda i: (i, 0)), row, row, col, col, vec, vec],
        out_specs=(row, row, vec, vec),
        out_shape=(SDS((s, w), BF16), SDS((s, w), F32), SDS((1, w), F32), SDS((1, w), F32)),
        compiler_params=_params(("arbitrary",)), name="mix_norm_bwd")(dmx, oa, ol, ra, rl, ga, gl)


def _attn_blocks(qs, ks, tri_after, csums, causal):
    zs = [_dot_nt(q, k) * (HEAD ** -0.5) for q, k in zip(qs, ks)]
    lbs = [_log_sigmoid(z) for z in zs]
    lss = [lb - z for lb, z in zip(lbs, zs)]
    if causal is not None:
        lss = [jnp.where(causal, ls, 0.0) for ls in lss]
    locs = [_split_dot(ls, tri_after) for ls in lss]
    ws = [jnp.exp(lb + (loc + cs)) for lb, loc, cs in zip(lbs, locs, csums)]
    if causal is not None:
        ws = [jnp.where(causal, w, 0.0) for w in ws]
    nxt = [cs + (loc[:, 0:1] + ls[:, 0:1]) for cs, loc, ls in zip(csums, locs, lss)]
    return lbs, ws, nxt


ATTN_HEADS_PER_STEP = 4


def _attn_tile(s):
    return 256 if s >= 1024 else 128


def _tri(t, after):
    r_i = lax.broadcasted_iota(jnp.int32, (t, t), 0)
    c_i = lax.broadcasted_iota(jnp.int32, (t, t), 1)
    return ((r_i > c_i) if after else (r_i < c_i)).astype(BF16)


def _attn_fwd(qkv, n_heads):
    s = qkv.shape[0]
    t = _attn_tile(s)
    hps = ATTN_HEADS_PER_STEP
    wid = hps * HEAD

    def body(q_ref, k_ref, v_ref, o_ref):
        qi = pl.program_id(1)
        tri_after = _tri(t, True)
        causal = lax.broadcasted_iota(jnp.int32, (t, t), 1) < lax.broadcasted_iota(jnp.int32, (t, t), 0)
        lanes = [slice(a * HEAD, (a + 1) * HEAD) for a in range(hps)]
        qs = [q_ref[:, ln] for ln in lanes]

        def block(off, carry, mask):
            ks = [k_ref[pl.ds(off, t), ln] for ln in lanes]
            _, ws, csums = _attn_blocks(qs, ks, tri_after, [cr[0] for cr in carry], mask)
            os_ = [cr[1] + _dot(w.astype(BF16), v_ref[pl.ds(off, t), ln]) for cr, w, ln in zip(carry, ws, lanes)]
            return tuple(zip(csums, os_))

        zero = tuple((jnp.zeros((t, 1), F32), jnp.zeros((t, HEAD), F32)) for _ in lanes)
        carry = block(pl.multiple_of(qi * t, t), zero, causal)
        carry = lax.fori_loop(1, qi + 1, lambda it, cr: block(pl.multiple_of((qi - it) * t, t), cr, None), carry)
        for a, ln in enumerate(lanes):
            o_ref[:, ln] = carry[a][1]

    hb = n_heads // hps
    return pl.pallas_call(
        body, grid=(hb, s // t),
        in_specs=[pl.BlockSpec((t, wid), lambda hh, i: (i, hh)),
                  pl.BlockSpec((s, wid), lambda hh, i: (0, hb + hh)),
                  pl.BlockSpec((s, wid), lambda hh, i: (0, 2 * hb + hh))],
        out_specs=pl.BlockSpec((t, wid), lambda hh, i: (i, hh)),
        out_shape=SDS((s, n_heads * HEAD), F32),
        compiler_params=_params(("parallel", "parallel")), name="attn_fwd")(qkv, qkv, qkv)


def _attn_bwd(qkv, do, n_heads):
    s = qkv.shape[0]
    t = _attn_tile(s)
    nq = s // t
    scale = HEAD ** -0.5
    hps = ATTN_HEADS_PER_STEP
    wid = hps * HEAD

    def body(q_ref, k_ref, v_ref, do_ref, dq_ref, dk_ref, dv_ref, e_s, sg_s, dk_acc, dv_acc):
        qi = pl.program_id(1)

        @pl.when(qi == 0)
        def _():
            dk_acc[...] = jnp.zeros_like(dk_acc)
            dv_acc[...] = jnp.zeros_like(dv_acc)

        tri_after = _tri(t, True)
        tri_before = _tri(t, False)
        causal = lax.broadcasted_iota(jnp.int32, (t, t), 1) < lax.broadcasted_iota(jnp.int32, (t, t), 0)
        lanes = [slice(a * HEAD, (a + 1) * HEAD) for a in range(hps)]
        qs = [q_ref[:, ln] for ln in lanes]
        douts = [do_ref[:, ln] for ln in lanes]

        def pass1(kb, csums, mask):
            off = pl.multiple_of(kb * t, t)
            ks = [k_ref[pl.ds(off, t), ln] for ln in lanes]
            dws = [_dot_nt(dout, v_ref[pl.ds(off, t), ln]) for dout, ln in zip(douts, lanes)]
            lbs, ws, nxt = _attn_blocks(qs, ks, tri_after, csums, mask)
            for a, ln in enumerate(lanes):
                e_s[a, kb] = dws[a] * ws[a]
                sg_s[a, kb] = jnp.exp(lbs[a])
            for a, ln in enumerate(lanes):
                dv_acc[pl.ds(off, t), ln] += _dot_tn(ws[a].astype(BF16), douts[a])
            return tuple(nxt)

        cs = pass1(qi, tuple(jnp.zeros((t, 1), F32) for _ in lanes), causal)
        lax.fori_loop(1, qi + 1, lambda it, c_: pass1(qi - it, c_, None), cs)

        def pass2(kb, carry, mask):
            off = pl.multiple_of(kb * t, t)
            es = [e_s[a, kb] for a in range(hps)]
            locs = [_split_dot(e, tri_before) for e in es]
            stays = [(loc + cr[0]) * sg_s[a, kb] for a, (loc, cr) in enumerate(zip(locs, carry))]
            if mask is not None:
                stays = [jnp.where(mask, st, 0.0) for st in stays]
            dzbs = [((e * (1.0 - sg_s[a, kb]) - st) * scale).astype(BF16) for a, (e, st) in enumerate(zip(es, stays))]
            dqs = [cr[1] + _dot(dzb, k_ref[pl.ds(off, t), ln]) for cr, dzb, ln in zip(carry, dzbs, lanes)]
            for a, ln in enumerate(lanes):
                dk_acc[pl.ds(off, t), ln] += _dot_tn(dzbs[a], qs[a])
            esums = [cr[0] + (loc[:, t - 1:t] + e[:, t - 1:t]) for cr, loc, e in zip(carry, locs, es)]
            return tuple(zip(esums, dqs))

        zero = tuple((jnp.zeros((t, 1), F32), jnp.zeros((t, HEAD), F32)) for _ in lanes)
        carry = lax.fori_loop(0, qi, lambda kb, cr: pass2(kb, cr, None), zero)
        carry = pass2(qi, carry, causal)
        for a, ln in enumerate(lanes):
            dq_ref[:, ln] = carry[a][1].astype(BF16)

        @pl.when(qi == nq - 1)
        def _():
            dk_ref[...] = dk_acc[...].astype(BF16)
            dv_ref[...] = dv_acc[...].astype(BF16)

    hb = n_heads // hps
    blk = pl.BlockSpec((t, wid), lambda hh, i: (i, hh))
    full = pl.BlockSpec((s, wid), lambda hh, i: (0, hh))
    return pl.pallas_call(
        body, grid=(hb, nq),
        in_specs=[blk,
                  pl.BlockSpec((s, wid), lambda hh, i: (0, hb + hh)),
                  pl.BlockSpec((s, wid), lambda hh, i: (0, 2 * hb + hh)),
                  blk],
        out_specs=(blk, full, full),
        out_shape=(SDS((s, n_heads * HEAD), BF16),) * 3,
        scratch_shapes=[pltpu.VMEM((hps, nq, t, t), F32), pltpu.VMEM((hps, nq, t, t), F32),
                        pltpu.VMEM((s, wid), F32), pltpu.VMEM((s, wid), F32)],
        compiler_params=_params(("parallel", "arbitrary")), name="attn_bwd")(qkv, qkv, qkv, do)


def _lru_chunk(s):
    return 256 if s >= 1024 else 128


def _lru_gates(xc, wa, ba, wx, bx, sp):
    xb = xc.astype(BF16)
    r = _sigmoid(_dot(xb, wa) + ba)
    ig = _sigmoid(_dot(xb, wx) + bx)
    la = -LRU_C * r * sp
    a = jnp.exp(la)
    t = jnp.tanh(la)
    mult = jnp.sqrt(-2.0 * t / (1.0 - t))
    return r, ig, a, mult


def _softplus_neg(lam):
    return jnp.maximum(-lam, 0.0) + jnp.log(1.0 + jnp.exp(-jnp.abs(lam)))


LRU_BLOCKS_PER_STEP = 1


def _lru_specs(s, n_blocks):
    bps = min(LRU_BLOCKS_PER_STEP, n_blocks)
    wid = bps * HEAD
    seq0 = pl.BlockSpec((s, wid), lambda h: (0, h))
    seq1 = pl.BlockSpec((s, wid), lambda h: (0, n_blocks // bps + h))
    taps = pl.BlockSpec((CONV_TAPS, wid), lambda h: (0, h))
    vec = pl.BlockSpec((1, wid), lambda h: (0, h))
    mat = pl.BlockSpec((bps, HEAD, HEAD), lambda h: (h, 0, 0))
    return bps, seq0, seq1, taps, vec, mat


def _per_block(one_block, n_2d, n_mat_pos, bps):
    def body(*refs):
        for a in range(bps):
            views = [r.at[a] if i in n_mat_pos else r.at[:, pl.ds(a * HEAD, HEAD)] for i, r in enumerate(refs[:n_2d])]
            one_block(*views, *refs[n_2d:])
    return body


def _lru_fwd(xrg, wconv, bconv, wa, ba, wx, bx, lam):
    s = xrg.shape[0]
    nb = wa.shape[0]
    tc = _lru_chunk(s)
    bps, seq0, seq1, taps, vec, mat = _lru_specs(s, nb)
    pad = SUBLANES

    def one_block(xr_ref, xg_ref, wc_ref, bc_ref, wa_ref, ba_ref, wx_ref, bx_ref, lam_ref, o_ref, h_ref, pad_s, a_s, u_s):
        pad_s[0:pad, :] = jnp.zeros((pad, HEAD), F32)
        pad_s[pad:pad + s, :] = xr_ref[...]
        wab, wxb = wa_ref[...].astype(BF16), wx_ref[...].astype(BF16)
        sp = _softplus_neg(lam_ref[...])
        for c in range(s // tc):
            base = c * tc
            xc = bc_ref[...] + sum(wc_ref[i:i + 1, :] * pad_s[pl.ds(base + pad - (CONV_TAPS - 1) + i, tc), :]
                                   for i in range(CONV_TAPS))
            _, ig, a, mult = _lru_gates(xc, wab, ba_ref[...], wxb, bx_ref[...], sp)
            a_s[base:base + tc, :] = a
            u_s[base:base + tc, :] = mult * (ig * xc)

        row = lax.broadcasted_iota(jnp.int32, (SUBLANES, HEAD), 0)

        def chunk(ci, hprev):
            off = pl.multiple_of(ci * SUBLANES, SUBLANES)
            a8, b8 = a_s[pl.ds(off, SUBLANES), :], u_s[pl.ds(off, SUBLANES), :]
            for d in (1, 2, 4):
                a_sh = jnp.where(row < d, 1.0, pltpu.roll(a8, d, 0))
                b_sh = jnp.where(row < d, 0.0, pltpu.roll(b8, d, 0))
                b8 = a8 * b_sh + b8
                a8 = a8 * a_sh
            h8 = a8 * hprev + b8
            h_ref[pl.ds(off, SUBLANES), :] = h8
            return h8[SUBLANES - 1:SUBLANES, :]

        lax.fori_loop(0, s // SUBLANES, chunk, jnp.zeros((1, HEAD), F32), unroll=8)
        for c in range(s // tc):
            sl = slice(c * tc, (c + 1) * tc)
            gel, _ = _gelu_parts(xg_ref[sl, :])
            o_ref[sl, :] = h_ref[sl, :] * gel

    return pl.pallas_call(
        _per_block(one_block, 11, (4, 6), bps), grid=(nb // bps,),
        in_specs=[seq0, seq1, taps, vec, mat, vec, mat, vec, vec],
        out_specs=(seq0, seq0),
        out_shape=(SDS((s, nb * HEAD), F32), SDS((s, nb * HEAD), F32)),
        scratch_shapes=[pltpu.VMEM((s + pad, HEAD), F32), pltpu.VMEM((s, HEAD), F32), pltpu.VMEM((s, HEAD), F32)],
        compiler_params=_params(("parallel",)), name="lru_fwd")(xrg, xrg, wconv, bconv, wa, ba, wx, bx, lam)


def _lru_bwd(xrg, dol, hseq, wconv, bconv, wa, ba, wx, bx, lam):
    s = xrg.shape[0]
    nb = wa.shape[0]
    tc = _lru_chunk(s)
    bps, seq0, seq1, taps, vec, mat = _lru_specs(s, nb)
    pad = SUBLANES

    def one_block(xr_ref, xg_ref, do_ref, h_ref, wc_ref, bc_ref, wa_ref, ba_ref, wx_ref, bx_ref, lam_ref,
             dxr_ref, dxg_ref, dwc_ref, dbc_ref, dwa_ref, dba_ref, dwx_ref, dbx_ref, dlam_ref,
             pad_s, hp_s, a_s, g_s, da_s, dxc_s):
        pad_s[0:pad, :] = jnp.zeros((pad, HEAD), F32)
        pad_s[pad:pad + s, :] = xr_ref[...]
        hp_s[0:pad, :] = jnp.zeros((pad, HEAD), F32)
        hp_s[pad:pad + s, :] = h_ref[...]
        a_s[s:s + pad, :] = jnp.zeros((pad, HEAD), F32)
        dxc_s[s:s + pad, :] = jnp.zeros((pad, HEAD), F32)
        wab, wxb = wa_ref[...].astype(BF16), wx_ref[...].astype(BF16)
        lam_v = lam_ref[...]
        sp = _softplus_neg(lam_v)

        def conv_in(c):
            base = c * tc
            wins = [pad_s[pl.ds(base + pad - (CONV_TAPS - 1) + i, tc), :] for i in range(CONV_TAPS)]
            xc = bc_ref[...] + sum(wc_ref[i:i + 1, :] * wins[i] for i in range(CONV_TAPS))
            return xc, wins

        for c in range(s // tc):
            sl = slice(c * tc, (c + 1) * tc)
            xc, _ = conv_in(c)
            _, _, a, _ = _lru_gates(xc, wab, ba_ref[...], wxb, bx_ref[...], sp)
            a_s[sl, :] = a
            gel, dgel = _gelu_parts(xg_ref[sl, :])
            dov = do_ref[sl, :]
            g_s[sl, :] = dov * gel
            dxg_ref[sl, :] = (dov * h_ref[sl, :] * dgel).astype(BF16)

        row = lax.broadcasted_iota(jnp.int32, (SUBLANES, HEAD), 0)
        n_chunks = s // SUBLANES

        def chunk(it, gnext):
            ci = n_chunks - 1 - it
            off = pl.multiple_of(ci * SUBLANES, SUBLANES)
            a8 = a_s[pl.ds(off, SUBLANES), :]
            a8n = a_s[pl.ds(off + SUBLANES, SUBLANES), :]
            c8 = pltpu.roll(jnp.where(row == 0, a8n, a8), SUBLANES - 1, 0)
            g8 = g_s[pl.ds(off, SUBLANES), :]
            for d in (1, 2, 4):
                c_sh = jnp.where(row >= SUBLANES - d, 1.0, pltpu.roll(c8, SUBLANES - d, 0))
                g_sh = jnp.where(row >= SUBLANES - d, 0.0, pltpu.roll(g8, SUBLANES - d, 0))
                g8 = c8 * g_sh + g8
                c8 = c8 * c_sh
            g8 = g8 + c8 * gnext
            g_s[pl.ds(off, SUBLANES), :] = g8
            h8 = hp_s[pl.ds(off + pad, SUBLANES), :]
            h8p = hp_s[pl.ds(off, SUBLANES), :]
            da_s[pl.ds(off, SUBLANES), :] = g8 * pltpu.roll(jnp.where(row == SUBLANES - 1, h8p, h8), 1, 0)
            return g8[0:1, :]

        lax.fori_loop(0, n_chunks, chunk, jnp.zeros((1, HEAD), F32), unroll=8)

        dsp = jnp.zeros((1, HEAD), F32)
        dbc = jnp.zeros((1, HEAD), F32)
        dba = jnp.zeros((1, HEAD), F32)
        dbx = jnp.zeros((1, HEAD), F32)
        dwa = jnp.zeros((HEAD, HEAD), F32)
        dwx = jnp.zeros((HEAD, HEAD), F32)
        dwc = [jnp.zeros((1, HEAD), F32) for _ in range(CONV_TAPS)]
        for c in range(s // tc):
            sl = slice(c * tc, (c + 1) * tc)
            xc, wins = conv_in(c)
            r, ig, a, mult = _lru_gates(xc, wab, ba_ref[...], wxb, bx_ref[...], sp)
            du, da = g_s[sl, :], da_s[sl, :]
            d_ix = du * mult
            dla = da * a - (du * ig * xc) * (a * a / mult)
            dsp = dsp + jnp.sum(dla * r, axis=0, keepdims=True) * (-LRU_C)
            dpa = (dla * (-LRU_C * sp)) * r * (1.0 - r)
            dpx = (d_ix * xc) * ig * (1.0 - ig)
            dpab, dpxb, xb = dpa.astype(BF16), dpx.astype(BF16), xc.astype(BF16)
            dxc = d_ix * ig + _dot_nt(dpab, wab) + _dot_nt(dpxb, wxb)
            dwa = dwa + _dot_tn(xb, dpab)
            dwx = dwx + _dot_tn(xb, dpxb)
            dba = dba + jnp.sum(dpa, axis=0, keepdims=True)
            dbx = dbx + jnp.sum(dpx, axis=0, keepdims=True)
            dbc = dbc + jnp.sum(dxc, axis=0, keepdims=True)
            for i in range(CONV_TAPS):
                dwc[i] = dwc[i] + jnp.sum(dxc * wins[i], axis=0, keepdims=True)
            dxc_s[sl, :] = dxc

        for c in range(s // tc):
            base = c * tc
            dxr = sum(wc_ref[i:i + 1, :] * dxc_s[pl.ds(base + (CONV_TAPS - 1) - i, tc), :] for i in range(CONV_TAPS))
            dxr_ref[base:base + tc, :] = dxr.astype(BF16)

        for i in range(CONV_TAPS):
            dwc_ref[i:i + 1, :] = dwc[i]
        dbc_ref[...] = dbc
        dwa_ref[...] = dwa
        dwx_ref[...] = dwx
        dba_ref[...] = dba
        dbx_ref[...] = dbx
        dlam_ref[...] = dsp * (-_sigmoid(-lam_v))

    w = nb * HEAD
    return pl.pallas_call(
        _per_block(one_block, 20, (6, 8, 15, 17), bps), grid=(nb // bps,),
        in_specs=[seq0, seq1, seq0, seq0, taps, vec, mat, vec, mat, vec, vec],
        out_specs=(seq0, seq0, taps, vec, mat, vec, mat, vec, vec),
        out_shape=(SDS((s, w), BF16), SDS((s, w), BF16), SDS((CONV_TAPS, w), F32), SDS((1, w), F32),
                   SDS((nb, HEAD, HEAD), F32), SDS((1, w), F32), SDS((nb, HEAD, HEAD), F32), SDS((1, w), F32),
                   SDS((1, w), F32)),
        scratch_shapes=[pltpu.VMEM((s + pad, HEAD), F32), pltpu.VMEM((s + pad, HEAD), F32),
                        pltpu.VMEM((s + pad, HEAD), F32), pltpu.VMEM((s, HEAD), F32),
                        pltpu.VMEM((s, HEAD), F32), pltpu.VMEM((s + pad, HEAD), F32)],
        compiler_params=_params(("parallel",)), name="lru_bwd",
    )(xrg, xrg, dol, hseq, wconv, bconv, wa, ba, wx, bx, lam)


def _ada_mod(c_all, w_sh, b_sh):
    n_ex, d = c_all.shape
    n = w_sh.shape[1]
    tn = _tile(n, 512)

    def body(c_ref, w_ref, b_ref, mod_ref, act_ref):
        cv = c_ref[...]
        act = cv * _sigmoid(cv)
        act_ref[...] = act
        mod_ref[...] = _dot(act.astype(BF16), w_ref[...].astype(BF16)) + b_ref[...]

    return pl.pallas_call(
        body, grid=(n // tn,),
        in_specs=[pl.BlockSpec((n_ex, d), lambda j: (0, 0)), pl.BlockSpec((d, tn), lambda j: (0, j)),
                  pl.BlockSpec((1, tn), lambda j: (0, j))],
        out_specs=(pl.BlockSpec((n_ex, tn), lambda j: (0, j)), pl.BlockSpec((n_ex, d), lambda j: (0, 0))),
        out_shape=(SDS((n_ex, n), F32), SDS((n_ex, d), F32)),
        compiler_params=_params(("arbitrary",)), name="ada_mod")(c_all, w_sh, b_sh)


def _adamw_math(w, g, m, v):
    m = ADAM_B1 * m + (1.0 - ADAM_B1) * g
    v = ADAM_B2 * v + (1.0 - ADAM_B2) * (g * g)
    m_hat = m / (1.0 - ADAM_B1 ** ADAM_STEP)
    v_hat = v / (1.0 - ADAM_B2 ** ADAM_STEP)
    delta = -ADAM_LR * (m_hat / (jnp.sqrt(v_hat) + ADAM_EPS) + ADAM_WD * w)
    return delta, m, v


def _adamw_plain(name, w, g, m, v):
    def body(w_ref, g_ref, m_ref, v_ref, d_ref, mo_ref, vo_ref):
        d_ref[...], mo_ref[...], vo_ref[...] = _adamw_math(w_ref[...], g_ref[...], m_ref[...], v_ref[...])

    return pl.pallas_call(body, out_shape=(SDS(w.shape, F32),) * 3, name=name)(w, g, m, v)


def _adamw_halves(name, c_arr, w, m, v, g_own, g_recv):
    r, n = w.shape
    rh = r // 2
    tr = _tile(rh, 256)
    nh = rh // tr

    def body(c_ref, w_ref, m_ref, v_ref, go_ref, gr_ref, g_ref, d_ref, mo_ref, vo_ref):
        own = (pl.program_id(0) // nh) == c_ref[0]
        g = jnp.where(own, go_ref[...], gr_ref[...])
        g_ref[...] = g
        d_ref[...], mo_ref[...], vo_ref[...] = _adamw_math(w_ref[...], g, m_ref[...], v_ref[...])

    full = pl.BlockSpec((tr, n), lambda i, c: (i, 0))
    own = pl.BlockSpec((tr, n), lambda i, c: (jnp.where(i // nh == c[0], i % nh, 0), 0))
    recv = pl.BlockSpec((tr, n), lambda i, c: (jnp.where(i // nh == c[0], 0, i % nh), 0))
    return pl.pallas_call(
        body,
        grid_spec=pltpu.PrefetchScalarGridSpec(
            num_scalar_prefetch=1, grid=(2 * nh,), in_specs=[full, full, full, own, recv],
            out_specs=(full,) * 4),
        out_shape=(SDS((r, n), F32),) * 4,
        compiler_params=_params(("parallel",)), name=name)(c_arr, w, m, v, g_own, g_recv)


def _adamw_ada(w, m, v, act_t, dmod):
    d, n = w.shape
    n_ex = act_t.shape[1]
    tr = _tile(d, 256)

    def body(a_ref, dm_ref, w_ref, m_ref, v_ref, g_ref, d_ref, mo_ref, vo_ref):
        g = _dot(a_ref[...], dm_ref[...])
        g_ref[...] = g
        d_ref[...], mo_ref[...], vo_ref[...] = _adamw_math(w_ref[...], g, m_ref[...], v_ref[...])

    full = pl.BlockSpec((tr, n), lambda i: (i, 0))
    return pl.pallas_call(
        body, grid=(d // tr,),
        in_specs=[pl.BlockSpec((tr, n_ex), lambda i: (i, 0)), pl.BlockSpec((n_ex, n), lambda i: (0, 0)), full, full, full],
        out_specs=(full,) * 4, out_shape=(SDS((d, n), F32),) * 4,
        compiler_params=_params(("parallel",)), name="adamw_ada")(act_t, dmod, w, m, v)


def _small_reduce_adamw(parts, w, m, v):
    n_dev, r, _ = parts.shape
    tr = r if r <= PACK_ROWS else PACK_ROWS

    def body(p_ref, w_ref, m_ref, v_ref, g_ref, d_ref, mo_ref, vo_ref):
        g = p_ref[0]
        for k in range(1, n_dev):
            g = g + p_ref[k]
        g_ref[...] = g
        d_ref[...], mo_ref[...], vo_ref[...] = _adamw_math(w_ref[...], g, m_ref[...], v_ref[...])

    full = pl.BlockSpec((tr, LANES), lambda i: (i, 0))
    return pl.pallas_call(
        body, grid=(r // tr,),
        in_specs=[pl.BlockSpec((n_dev, tr, LANES), lambda i: (0, i, 0)), full, full, full],
        out_specs=(full,) * 4, out_shape=(SDS((r, LANES), F32),) * 4,
        compiler_params=_params(("parallel",)), name="small_reduce_adamw")(parts, w, m, v)


def _mesh_pos():
    return lax.axis_index("x"), lax.axis_index("y"), lax.axis_index("c")


def _other_chips(x, y):
    return [(1 - x, y), (x, 1 - y), (1 - x, 1 - y)]


def _all_gather_small(name, blk, after=()):
    r, n = blk.shape
    n_after = len(after)

    def body(x_ref, *rest):
        out_ref, send_sems, recv_sems, local_sem = rest[n_after:]
        x, y, c = _mesh_pos()
        me, sibling = (x, y, c), (x, y, 1 - c)
        chips = _other_chips(x, y)

        def rows(px, py, pc):
            return out_ref.at[4 * px + 2 * py + pc]

        def copy(k, block, to, src=None):
            return pltpu.make_async_remote_copy(
                src_ref=rows(*block) if src is None else src, dst_ref=rows(*block),
                send_sem=send_sems.at[k], recv_sem=recv_sems.at[k], device_id=to, device_id_type=MESH)

        mine = pltpu.make_async_copy(x_ref, rows(*me), local_sem)
        mine.start()
        first = [copy(0, me, sibling, src=x_ref)]
        first += [copy(1 + j, me, (*chip, c), src=x_ref) for j, chip in enumerate(chips)]
        for cp in first:
            cp.start()
        passed = [copy(4 + j, (*chip, c), sibling) for j, chip in enumerate(chips)]
        for j, chip in enumerate(chips):
            copy(1 + j, (*chip, c), me).wait_recv()
            passed[j].start()
        copy(0, sibling, me).wait_recv()
        for j, chip in enumerate(chips):
            copy(4 + j, (*chip, 1 - c), me).wait_recv()
        for cp in first + passed:
            cp.wait_send()
        mine.wait()

    return pl.pallas_call(
        body, out_shape=SDS((N_DEV, r, n), blk.dtype),
        in_specs=[pl.BlockSpec(memory_space=pltpu.VMEM)] + [pl.BlockSpec(memory_space=pl.ANY)] * n_after,
        out_specs=pl.BlockSpec(memory_space=pltpu.VMEM),
        scratch_shapes=[pltpu.SemaphoreType.DMA((7,)), pltpu.SemaphoreType.DMA((7,)), pltpu.SemaphoreType.DMA],
        compiler_params=pltpu.CompilerParams(vmem_limit_bytes=VMEM_LIMIT), name=name)(blk, *after)


_ANY = pl.BlockSpec(memory_space=pl.ANY)
_HBM = pl.BlockSpec(memory_space=pltpu.HBM)
_SEM = pl.BlockSpec(memory_space=pltpu.SEMAPHORE)
_EFFECT = pltpu.SideEffectType.DATAFLOW_SIDE_EFFECTING


def _hbm(a):
    return pltpu.with_memory_space_constraint(a, pltpu.HBM)


def _place_cast(name, j_arr, shard, kind, after):
    r, n = shard.shape
    tr = _tile(r, 256)
    nr = r // tr
    if kind == "col":
        out_shape, o_spec = (r, N_CHIP * n), pl.BlockSpec((tr, n), lambda i, j: (i, j[0]))
    else:
        out_shape, o_spec = (N_CHIP * r, n), pl.BlockSpec((tr, n), lambda i, j: (j[0] * nr + i, 0))

    def body(j_ref, s_ref, after_ref, o_ref, tok_ref):
        o_ref[...] = s_ref[...].astype(BF16)
        tok_ref[...] = jnp.zeros_like(tok_ref)

    return pl.pallas_call(
        body,
        grid_spec=pltpu.PrefetchScalarGridSpec(
            num_scalar_prefetch=1, grid=(nr,), in_specs=[pl.BlockSpec((tr, n), lambda i, j: (i, 0)), _ANY],
            out_specs=(o_spec, pl.BlockSpec((SUBLANES, LANES), lambda i, j: (0, 0)))),
        out_shape=(SDS(out_shape, BF16), SDS((SUBLANES, LANES), F32)),
        compiler_params=_params(("arbitrary",)), name=name)(j_arr, shard, after)


def _leg_direct(full, kind, x, y, c):
    mine = _full_region(full, kind, x, y, c)
    return [(mine, mine, (1 - x, y, c)), (mine, mine, (x, 1 - y, c))]


def _leg_relay(full, kind, x, y, c):
    fx, fy = jnp.where(c == 0, 1 - x, x), jnp.where(c == 0, y, 1 - y)
    tx, ty = jnp.where(c == 0, x, 1 - x), jnp.where(c == 0, 1 - y, y)
    got = _full_region(full, kind, fx, fy, c)
    return [(got, got, (tx, ty, c))]


def _leg_d2d(which):
    def leg(full, kind, x, y, c):
        chips = _other_chips(x, y)
        return [(_full_region(full, kind, *chips[k], c), _full_region(full, kind, *chips[k], c), (x, y, 1 - c))
                for k in which]
    return leg


_LEGS = {"direct": (_leg_direct, 2), "relay": (_leg_relay, 1), "d2d_near": (_leg_d2d((0, 1)), 2),
         "d2d_far": (_leg_d2d((2,)), 1)}


def _gather_call(name, fulls, kinds, waits, starts, after, thru):
    nw, n_wait, n_start = len(fulls), len(waits), len(starts)

    def body(*refs):
        wait_sems = refs[nw:nw + 2 * n_wait]
        outs = refs[nw + 2 * n_wait + 2:]
        full, start_sems = outs[:nw], outs[nw:nw + 2 * n_start]
        x, y, c = _mesh_pos()
        for i, (leg, ws, _, _) in enumerate(waits):
            fn, per = _LEGS[leg]
            for li, w in enumerate(ws):
                for k, (s_, d_, dev) in enumerate(fn(full[w], kinds[w], x, y, c)):
                    cp = pltpu.make_async_remote_copy(
                        src_ref=s_, dst_ref=d_, send_sem=wait_sems[2 * i].at[per * li + k],
                        recv_sem=wait_sems[2 * i + 1].at[per * li + k], device_id=dev, device_id_type=MESH)
                    cp.wait_recv()
                    cp.wait_send()
        for i, (leg, ws) in enumerate(starts):
            fn, per = _LEGS[leg]
            for li, w in enumerate(ws):
                for k, (s_, d_, dev) in enumerate(fn(full[w], kinds[w], x, y, c)):
                    pltpu.make_async_remote_copy(
                        src_ref=s_, dst_ref=d_, send_sem=start_sems[2 * i].at[per * li + k],
                        recv_sem=start_sems[2 * i + 1].at[per * li + k], device_id=dev, device_id_type=MESH).start()

    sems = []
    for leg, ws in starts:
        sems += [pltpu.SemaphoreType.DMA((_LEGS[leg][1] * len(ws),))] * 2
    wait_args = []
    for _, _, s_, r_ in waits:
        wait_args += [s_, r_]
    outs = pl.pallas_call(
        body,
        out_shape=tuple(pltpu.HBM(f_.shape, f_.dtype) for f_ in fulls) + tuple(sems) + (SDS(thru.shape, thru.dtype),),
        in_specs=[_HBM] * nw + [_SEM] * (2 * n_wait) + [_ANY, _ANY],
        out_specs=tuple([_HBM] * nw + [_SEM] * (2 * n_start) + [_ANY]),
        input_output_aliases={**{w: w for w in range(nw)}, nw + 2 * n_wait + 1: nw + 2 * n_start},
        compiler_params=pltpu.CompilerParams(has_side_effects=_EFFECT),
        name=name,
    )(*[_hbm(f_) for f_ in fulls], *wait_args, after, thru)
    pairs = [(outs[nw + 2 * i], outs[nw + 2 * i + 1]) for i in range(n_start)]
    return list(outs[:nw]), pairs, outs[nw + 2 * n_start]


def _full_region(full, kind, px, py, half):
    j = 2 * px + py
    if kind == "col":
        rh, cols = full.shape[0] // 2, full.shape[1] // N_CHIP
        return full.at[pl.ds(half * rh, rh), pl.ds(j * cols, cols)]
    rows = full.shape[0] // N_CHIP
    rh = rows // 2
    return full.at[pl.ds(j * rows + half * rh, rh), :]


def _half_of(ref, kind, half):
    if kind == "col":
        rh = ref.shape[0] // 2
        return ref.at[pl.ds(half * rh, rh), :]
    rh = ref.shape[1] // 2
    return ref.at[:, pl.ds(half * rh, rh), :]


def _half_shape(g, kind):
    if kind == "col":
        return (g.shape[0] // 2, g.shape[1])
    return (g.shape[0], g.shape[1] // 2, g.shape[2])


def _plan_swap_half(kind):
    def plan(src, land, x, y, c):
        return [(_half_of(src, kind, 1 - c), land, (x, y, 1 - c))]
    return plan


def _plan_scatter(kind):
    def plan(src, land, x, y, c):
        out = []
        for k, (px, py) in enumerate(_other_chips(x, y)):
            j = 2 * px + py
            if kind == "col":
                n = src.shape[1] // N_CHIP
                blk = src.at[:, pl.ds(j * n, n)]
            else:
                blk = src.at[j]
            out.append((blk, land.at[k], (px, py, c)))
        return out
    return plan


def _plan_whole(src, land, x, y, c):
    return [(src, land, (x, y, 1 - c))]


def _split_start(name, src, land_shape, n, plan, thru):
    def body(src_in, land_in, thru_in, send, recv, src_ref, land_ref, thru_out):
        x, y, c = _mesh_pos()
        for k, (s_, d_, dev) in enumerate(plan(src_ref, land_ref, x, y, c)):
            pltpu.make_async_remote_copy(src_ref=s_, dst_ref=d_, send_sem=send.at[k], recv_sem=recv.at[k],
                                         device_id=dev, device_id_type=MESH).start()

    sem = pltpu.SemaphoreType.DMA((n,))
    return pl.pallas_call(
        body,
        out_shape=(sem, sem, pltpu.HBM(src.shape, src.dtype), pltpu.HBM(land_shape, src.dtype), SDS(thru.shape, thru.dtype)),
        in_specs=[_HBM, _HBM, _ANY], out_specs=(_SEM, _SEM, _HBM, _HBM, _ANY),
        input_output_aliases={0: 2, 1: 3, 2: 4},
        compiler_params=pltpu.CompilerParams(has_side_effects=_EFFECT), name=name,
    )(_hbm(src), _hbm(lax.empty(land_shape, src.dtype)), thru)


def _split_wait(name, send, recv, src, land, plan, after):
    def body(src_in, land_in, send_r, recv_r, after_r, src_ref, land_ref):
        x, y, c = _mesh_pos()
        for k, (s_, d_, dev) in enumerate(plan(src_ref, land_ref, x, y, c)):
            cp = pltpu.make_async_remote_copy(src_ref=s_, dst_ref=d_, send_sem=send_r.at[k], recv_sem=recv_r.at[k],
                                              device_id=dev, device_id_type=MESH)
            cp.wait_send()
            cp.wait_recv()

    return pl.pallas_call(
        body,
        out_shape=(pltpu.HBM(src.shape, src.dtype), pltpu.HBM(land.shape, land.dtype)),
        in_specs=[_HBM, _HBM, _SEM, _SEM, _ANY], out_specs=(_HBM, _HBM),
        input_output_aliases={0: 0, 1: 1},
        compiler_params=pltpu.CompilerParams(has_side_effects=_EFFECT), name=name,
    )(src, land, send, recv, after)


def _dev_row(buf, px, py, pc):
    return buf.at[4 * px + 2 * py + pc]


def _plan_gather_own(buf, land, x, y, c):
    own = _dev_row(buf, x, y, c)
    return [(own, own, (x, y, 1 - c))] + [(own, own, (px, py, c)) for px, py in _other_chips(x, y)]


def _plan_gather_pass(buf, land, x, y, c):
    return [(_dev_row(buf, px, py, c), _dev_row(buf, px, py, c), (x, y, 1 - c)) for px, py in _other_chips(x, y)]


def _split_start_inplace(name, buf, n, plan, thru):
    def body(buf_in, thru_in, send, recv, buf_ref, thru_out):
        x, y, c = _mesh_pos()
        for k, (s_, d_, dev) in enumerate(plan(buf_ref, buf_ref, x, y, c)):
            pltpu.make_async_remote_copy(src_ref=s_, dst_ref=d_, send_sem=send.at[k], recv_sem=recv.at[k],
                                         device_id=dev, device_id_type=MESH).start()

    sem = pltpu.SemaphoreType.DMA((n,))
    return pl.pallas_call(
        body, out_shape=(sem, sem, pltpu.HBM(buf.shape, buf.dtype), SDS(thru.shape, thru.dtype)),
        in_specs=[_HBM, _ANY], out_specs=(_SEM, _SEM, _HBM, _ANY), input_output_aliases={0: 2, 1: 3},
        compiler_params=pltpu.CompilerParams(has_side_effects=_EFFECT), name=name)(_hbm(buf), thru)


def _split_wait_inplace(name, send, recv, buf, plan, after):
    def body(buf_in, send_r, recv_r, after_r, buf_ref):
        x, y, c = _mesh_pos()
        for k, (s_, d_, dev) in enumerate(plan(buf_ref, buf_ref, x, y, c)):
            cp = pltpu.make_async_remote_copy(src_ref=s_, dst_ref=d_, send_sem=send_r.at[k], recv_sem=recv_r.at[k],
                                              device_id=dev, device_id_type=MESH)
            cp.wait_send()
            cp.wait_recv()

    return pl.pallas_call(
        body, out_shape=pltpu.HBM(buf.shape, buf.dtype), in_specs=[_HBM, _SEM, _SEM, _ANY], out_specs=_HBM,
        input_output_aliases={0: 0}, compiler_params=pltpu.CompilerParams(has_side_effects=_EFFECT), name=name,
    )(buf, send, recv, after)


def _place_row(name, me_arr, slab):
    r, n = slab.shape
    tr = r if r <= PACK_ROWS else PACK_ROWS

    def body(me_ref, s_ref, o_ref):
        o_ref[...] = s_ref[...]

    return pl.pallas_call(
        body,
        grid_spec=pltpu.PrefetchScalarGridSpec(
            num_scalar_prefetch=1, grid=(r // tr,), in_specs=[pl.BlockSpec((tr, n), lambda i, me: (i, 0))],
            out_specs=pl.BlockSpec((None, tr, n), lambda i, me: (me[0], i, 0))),
        out_shape=SDS((N_DEV, r, n), slab.dtype), compiler_params=_params(("parallel",)), name=name)(me_arr, slab)


def _add_halves(name, c_arr, g, got, kind):
    if kind == "col":
        rh, n = got.shape
        tr = _tile(rh, 256)
        nh = rh // tr
        g_spec = pl.BlockSpec((tr, n), lambda i, c: (c[0] * nh + i, 0))
        o_spec = pl.BlockSpec((tr, n), lambda i, c: (i, 0))
        grid = (nh,)
    else:
        nc, rh, n = got.shape
        tr = _tile(rh, 256)
        nh = rh // tr
        g_spec = pl.BlockSpec((None, tr, n), lambda j, i, c: (j, c[0] * nh + i, 0))
        o_spec = pl.BlockSpec((None, tr, n), lambda j, i, c: (j, i, 0))
        grid = (nc, nh)

    def body(c_ref, g_ref, r_ref, o_ref):
        o_ref[...] = (g_ref[...].astype(F32) + r_ref[...].astype(F32)).astype(o_ref.dtype)

    return pl.pallas_call(
        body,
        grid_spec=pltpu.PrefetchScalarGridSpec(num_scalar_prefetch=1, grid=grid, in_specs=[g_spec, o_spec], out_specs=o_spec),
        out_shape=SDS(got.shape, got.dtype),
        compiler_params=_params(("parallel",) * len(grid)), name=name)(c_arr, g, got)


def _sum_partials(name, j_arr, part, got, kind):
    _, rh, n = got.shape
    tr = _tile(rh, 256)
    if kind == "col":
        p_spec = pl.BlockSpec((tr, n), lambda i, j: (i, j[0]))
    else:
        p_spec = pl.BlockSpec((None, tr, n), lambda i, j: (j[0], i, 0))

    def body(j_ref, p_ref, r_ref, o_ref):
        o_ref[...] = ((p_ref[...].astype(F32) + r_ref[0].astype(F32)) + r_ref[1].astype(F32)) + r_ref[2].astype(F32)

    return pl.pallas_call(
        body,
        grid_spec=pltpu.PrefetchScalarGridSpec(
            num_scalar_prefetch=1, grid=(rh // tr,),
            in_specs=[p_spec, pl.BlockSpec((3, tr, n), lambda i, j: (0, i, 0))],
            out_specs=pl.BlockSpec((tr, n), lambda i, j: (i, 0))),
        out_shape=SDS((rh, n), F32),
        compiler_params=_params(("parallel",)), name=name)(j_arr, part, got)


def _swap_reduced(name, halves):
    nw = len(halves)

    def body(*refs):
        h, got = refs[:nw], refs[nw:2 * nw]
        send_sems, recv_sems = refs[2 * nw:]
        x, y, c = _mesh_pos()
        cps = []
        for w in range(nw):
            cp = pltpu.make_async_remote_copy(
                src_ref=h[w], dst_ref=got[w], send_sem=send_sems.at[w], recv_sem=recv_sems.at[w],
                device_id=(x, y, 1 - c), device_id_type=MESH)
            cp.start()
            cps.append(cp)
        for cp in cps:
            cp.wait()

    return pl.pallas_call(
        body, out_shape=tuple(SDS(h.shape, h.dtype) for h in halves),
        in_specs=[_ANY] * nw, out_specs=tuple([_ANY] * nw),
        scratch_shapes=[pltpu.SemaphoreType.DMA((nw,)), pltpu.SemaphoreType.DMA((nw,))],
        name=name)(*halves)


def _pack(arrays):
    flat = [a.reshape(-1).astype(F32) for a in arrays]
    flat = [jnp.pad(f, (0, (-f.shape[0]) % LANES)) for f in flat]
    sizes = [f.shape[0] for f in flat]
    total = sum(sizes)
    rows = total // LANES
    tail = LANES * ((-rows) % (PACK_ROWS if rows > PACK_ROWS else SUBLANES))
    if tail:
        flat.append(jnp.zeros((tail,), F32))
    return jnp.concatenate(flat).reshape(-1, LANES), sizes


def _unpack(slab, sizes, shapes, lead=()):
    flat = slab.reshape(lead + (-1,))
    out, off = [], 0
    for sz, shp in zip(sizes, shapes):
        n = math.prod(shp)
        out.append(flat[..., off:off + n].reshape(lead + tuple(shp)))
        off += sz
    return out


def kernel(x, c, w_ada, b_ada, g_norm_mix, w_in, w_conv, b_conv, w_rg_a, b_rg_a, w_rg_x, b_rg_x, lru_lambda, g_attn_out, g_lru_out, w_out, g_norm_mlp, w_mlp_in, w_mlp_out, g_norm_final, loss_target, m_w_ada, m_b_ada, m_g_norm_mix, m_w_in, m_w_conv, m_b_conv, m_w_rg_a, m_b_rg_a, m_w_rg_x, m_b_rg_x, m_lru_lambda, m_g_attn_out, m_g_lru_out, m_w_out, m_g_norm_mlp, m_w_mlp_in, m_w_mlp_out, m_g_norm_final, v_w_ada, v_b_ada, v_g_norm_mix, v_w_in, v_w_conv, v_b_conv, v_w_rg_a, v_b_rg_a, v_w_rg_x, v_b_rg_x, v_lru_lambda, v_g_attn_out, v_g_lru_out, v_w_out, v_g_norm_mlp, v_w_mlp_in, v_w_mlp_out, v_g_norm_final):
    s, d = x.shape[1], x.shape[2]
    aw = d // 2
    nh = aw // HEAD
    f = w_mlp_out.shape[1] * N_CHIP
    n_ada = w_ada.shape[2]
    n_cv = w_conv.shape[2]
    ix, iy, ic = lax.axis_index("x"), lax.axis_index("y"), lax.axis_index("c")
    chip = 2 * ix + iy
    me = 2 * chip + ic
    c_arr = jnp.reshape(ic, (1,)).astype(jnp.int32)
    j_arr = jnp.reshape(chip, (1,)).astype(jnp.int32)

    x2d, tgt = x[0], loss_target[0]

    k_in, k_out, k_mi, k_mo = kinds = ("col", "row", "col", "row")
    slab, sizes = _pack([c, w_conv])
    p_in, _ = _place_cast("place_cast_0", j_arr, w_in[0], k_in, c)
    (f_in,), (dir_in,), slab = _gather_call("gather_0", [p_in], [k_in], [], [("direct", [0])], c, slab)
    p_out, tok = _place_cast("place_cast_1", j_arr, w_out[0], k_out, slab)
    p_mi, tok = _place_cast("place_cast_2", j_arr, w_mlp_in[0], k_mi, tok)
    p_mo, tok = _place_cast("place_cast_3", j_arr, w_mlp_out[0], k_mo, tok)

    gathered = _all_gather_small("comm_gather_cond", slab, after=(tok,))
    c_parts, cv_parts = _unpack(gathered, sizes, [(d,), (CONV_TAPS, n_cv)], lead=(N_DEV,))
    c_all = c_parts
    w_conv_full = jnp.concatenate([cv_parts[2 * j] for j in range(N_CHIP)], axis=-1)
    b_sh = lax.dynamic_slice(b_ada, (0, chip * n_ada), (1, n_ada))
    (f_in,), (rel_in, near_in), c_all = _gather_call(
        "gather_1", [f_in], [k_in], [("direct", [0], *dir_in)], [("relay", [0]), ("d2d_near", [0])], gathered, c_all)
    mod_part, act_all = _ada_mod(c_all, w_ada[0], b_sh)
    mod_g = _all_gather_small("comm_gather_mod", mod_part.reshape(-1, LANES))
    mod_g = mod_g.reshape(N_DEV, N_DEV, n_ada)
    mod = jnp.concatenate([lax.dynamic_index_in_dim(mod_g[2 * j], me, 0, keepdims=True) for j in range(N_CHIP)], axis=-1)
    sh1, sc1, gt1, sh2, sc2, gt2 = [mod[:, k * d:(k + 1) * d] for k in range(N_MOD)]

    (f_in, f_out, f_mi, f_mo), (far_in, dir_om, dir_mo), sh1 = _gather_call(
        "gather_2", [f_in, p_out, p_mi, p_mo], kinds, [("relay", [0], *rel_in)],
        [("d2d_far", [0]), ("direct", [1, 2]), ("direct", [3])], mod, sh1)

    h1, rstd1 = _norm_mod_fwd("norm_mod_fwd1", x2d, g_norm_mix, sc1, sh1)
    (w_in_f,), _, h1 = _gather_call("gather_3", [f_in], [k_in],
                                    [("d2d_near", [0], *near_in), ("d2d_far", [0], *far_in)], [], rstd1, h1)
    (qkv,) = _matmul("mm_qkv", h1, w_in_f, "nn", s, 3 * aw, d, (BF16,))
    (xrg,) = _matmul("mm_xrg", h1, w_in_f, "nn", s, 2 * aw, d, (F32,), b_off=3 * aw)
    o_attn = _attn_fwd(qkv, nh)
    (f_out, f_mi), (rel_om, near_om), xrg = _gather_call(
        "gather_4", [f_out, f_mi], [k_out, k_mi], [("direct", [0, 1], *dir_om)],
        [("relay", [0, 1]), ("d2d_near", [0, 1])], o_attn, xrg)
    wa3, wx3 = w_rg_a[0], w_rg_x[0]
    o_lru, hseq = _lru_fwd(xrg, w_conv_full, b_conv, wa3, b_rg_a, wx3, b_rg_x, lru_lambda)
    mixed, rstd_a, rstd_l = _mix_norm_fwd(o_attn, o_lru, g_attn_out, g_lru_out)
    (f_out, f_mi, f_mo), (far_om, rel_mo, near_mo), mixed = _gather_call(
        "gather_5", [f_out, f_mi, f_mo], [k_out, k_mi, k_mo], [("relay", [0, 1], *rel_om), ("direct", [2], *dir_mo)],
        [("d2d_far", [0, 1]), ("relay", [2]), ("d2d_near", [2])], rstd_a, mixed)
    (w_out_f, w_mi_f), _, mixed = _gather_call(
        "gather_6", [f_out, f_mi], [k_out, k_mi], [("d2d_near", [0, 1], *near_om), ("d2d_far", [0, 1], *far_om)], [],
        rstd_l, mixed)

    def residual(acc, xin, gt):
        return acc, xin + gt * acc

    y1, x1 = _matmul("mm_out", mixed, w_out_f, "nn", s, d, d, (F32, F32), extras=(x2d, gt1),
                     extra_kinds=("tile", "row"), epilogue=residual)
    h2, rstd2 = _norm_mod_fwd("norm_mod_fwd2", x1, g_norm_mlp, sc2, sh2)
    (f_mo,), (far_mo,), h2 = _gather_call("gather_7", [f_mo], [k_mo], [("relay", [0], *rel_mo)],
                                          [("d2d_far", [0])], rstd2, h2)

    def sq_relu(acc):
        r = jnp.maximum(acc, 0.0)
        return 2.0 * r, r * r

    r2, hid = _matmul("mm_mlp_in", h2, w_mi_f, "nn", s, f, d, (BF16, BF16), epilogue=sq_relu)
    (w_mo_f,), _, hid = _gather_call("gather_8", [f_mo], [k_mo],
                                     [("d2d_near", [0], *near_mo), ("d2d_far", [0], *far_mo)], [], r2, hid)
    y2, x2 = _matmul("mm_mlp_out", hid, w_mo_f, "nn", s, d, f, (F32, F32), extras=(x1, gt2),
                     extra_kinds=("tile", "row"), epilogue=residual)
    dx2, loss_row, dg_final, dy2, dgt2 = _final_loss(x2, g_norm_final.reshape(1, d), tgt, y2, gt2)

    def rs_begin(tag, g, kind, thru):
        send, recv, g, land, thru = _split_start("rs_swap_start_" + tag, g, _half_shape(g, kind), 1,
                                                 _plan_swap_half(kind), thru)
        return {"tag": tag, "kind": kind, "swap": (send, recv, g, land)}, thru

    def rs_mid(st, after, thru):
        tag, kind = st["tag"], st["kind"]
        g, got = _split_wait("rs_swap_wait_" + tag, *st["swap"], _plan_swap_half(kind), after)
        part = _add_halves("add_halves_" + tag, c_arr, g, got, kind)
        blk = (part.shape[0], part.shape[1] // N_CHIP) if kind == "col" else part.shape[1:]
        send, recv, part, land, thru = _split_start("rs_scatter_start_" + tag, part, (N_CHIP - 1,) + blk, N_CHIP - 1,
                                                    _plan_scatter(kind), thru)
        st["scatter"] = (send, recv, part, land)
        return thru

    def rs_end(st, after):
        tag, kind = st["tag"], st["kind"]
        part, got = _split_wait("rs_scatter_wait_" + tag, *st["scatter"], _plan_scatter(kind), after)
        return _sum_partials("sum_partials_" + tag, j_arr, part, got, kind)

    (dpre,) = _matmul("mm_dhid", dy2, w_mo_f, "nt", s, f, d, (BF16,), extras=(r2,), extra_kinds=("tile",),
                      epilogue=lambda acc, r: (acc * r.astype(F32),))
    (g_mo,) = _matmul("mm_dw_mlp_out", hid, dy2, "tn", f, d, s, (BF16,))
    st_mo, dpre = rs_begin("mo", g_mo.reshape(N_CHIP, f // N_CHIP, d), "row", dpre)
    (dh2,) = _matmul("mm_dh2", dpre, w_mi_f, "nt", s, d, f, (F32,))
    (g_mi,) = _matmul("mm_dw_mlp_in", h2, dpre, "tn", d, f, s, (BF16,))
    dh2 = rs_mid(st_mo, g_mi, dh2)
    st_mi, dh2 = rs_begin("mi", g_mi, "col", dh2)
    dx1, dsh2, dsc2, dg_mlp, dy1, dgt1 = _norm_mod_bwd("norm_mod_bwd2", dh2, x1, rstd2, g_norm_mlp, sc2, dx2,
                                                       gate=(y1, gt1))
    (dmixed,) = _matmul("mm_dmixed", dy1, w_out_f, "nt", s, d, d, (F32,))
    (g_out,) = _matmul("mm_dw_out", mixed, dy1, "tn", d, d, s, (BF16,))
    dmixed = rs_mid(st_mi, g_out, dmixed)
    st_out, dmixed = rs_begin("out", g_out.reshape(N_CHIP, d // N_CHIP, d), "row", dmixed)
    do_attn, do_lru, dg_attn, dg_lru = _mix_norm_bwd(dmixed, o_attn, o_lru, rstd_a, rstd_l, g_attn_out, g_lru_out)
    dq, dk, dv = _attn_bwd(qkv, do_attn, nh)
    do_lru = rs_mid(st_out, dq, do_lru)
    dxr, dxg, dwconv, dbconv, dwa, dba, dwx, dbx, dlam = _lru_bwd(
        xrg, do_lru, hseq, w_conv_full, b_conv, wa3, b_rg_a, wx3, b_rg_x, lru_lambda)
    dproj = jnp.concatenate([dq, dk, dv, dxr, dxg], axis=-1)
    (dh1,) = _matmul("mm_dh1", dproj, w_in_f, "nt", s, d, 5 * aw, (F32,))
    (g_in,) = _matmul("mm_dw_in", h1, dproj, "tn", d, 5 * aw, s, (BF16,))
    st_in, dh1 = rs_begin("in", g_in, "col", dh1)
    grad_x, dsh1, dsc1, dg_mix = _norm_mod_bwd("norm_mod_bwd1", dh1, x2d, rstd1, g_norm_mix, sc1, dx1)

    dmod = jnp.concatenate([dsh1, dsc1, dgt1, dsh2, dsc2, dgt2], axis=-1)
    small_names = ["b_ada", "g_norm_mix", "b_conv", "w_rg_a", "b_rg_a", "w_rg_x", "b_rg_x", "lru_lambda",
                   "g_attn_out", "g_lru_out", "g_norm_mlp", "g_norm_final"]
    small_g = [dmod, dg_mix, dbconv, dwa, dba, dwx, dbx, dlam, dg_attn, dg_lru, dg_mlp, dg_final]
    small_w = [b_ada, g_norm_mix, b_conv, w_rg_a, b_rg_a, w_rg_x, b_rg_x, lru_lambda, g_attn_out, g_lru_out, g_norm_mlp, g_norm_final]
    small_m = [m_b_ada, m_g_norm_mix, m_b_conv, m_w_rg_a, m_b_rg_a, m_w_rg_x, m_b_rg_x, m_lru_lambda, m_g_attn_out, m_g_lru_out, m_g_norm_mlp, m_g_norm_final]
    small_v = [v_b_ada, v_g_norm_mix, v_b_conv, v_w_rg_a, v_b_rg_a, v_w_rg_x, v_b_rg_x, v_lru_lambda, v_g_attn_out, v_g_lru_out, v_g_norm_mlp, v_g_norm_final]
    extra_zero = [jnp.zeros_like(dwconv), jnp.zeros((LANES,), F32)]
    g_slab, g_sizes = _pack(small_g + [dwconv, loss_row])
    w_slab, _ = _pack(small_w + extra_zero)
    m_slab, _ = _pack(small_m + extra_zero)
    v_slab, _ = _pack(small_v + extra_zero)
    me_arr = jnp.reshape(me, (1,)).astype(jnp.int32)
    g_buf = _place_row("place_small_grads", me_arr, g_slab)
    sg_send, sg_recv, g_buf, tok = _split_start_inplace("sg_gather_start", g_buf, N_CHIP, _plan_gather_own, loss_row)
    tok = rs_mid(st_in, tok, tok)

    def reduced_begin(tag, half, tok_):
        send, recv, half, land, tok_ = _split_start("rs_reduced_start_" + tag, half, half.shape, 1, _plan_whole, tok_)
        return (send, recv, half, land), tok_

    def reduced_end(tag, st, after):
        return _split_wait("rs_reduced_wait_" + tag, *st, _plan_whole, after)

    sw_mo, tok = reduced_begin("mo", rs_end(st_mo, tok), tok)
    sw_mi, tok = reduced_begin("mi", rs_end(st_mi, tok), tok)
    sw_out, tok = reduced_begin("out", rs_end(st_out, tok), tok)
    g_buf = _split_wait_inplace("sg_gather_wait", sg_send, sg_recv, g_buf, _plan_gather_own, tok)
    sg_send, sg_recv, g_buf, tok = _split_start_inplace("sg_pass_start", g_buf, N_CHIP - 1, _plan_gather_pass, tok)
    half_mo, got_mo = reduced_end("mo", sw_mo, tok)
    big = {"w_mlp_out": _adamw_halves("adamw_w_mlp_out", c_arr, w_mlp_out[0], m_w_mlp_out[0], v_w_mlp_out[0],
                                      half_mo, got_mo)}
    g_all = _split_wait_inplace("sg_pass_wait", sg_send, sg_recv, g_buf, _plan_gather_pass, big["w_mlp_out"][1])
    gs_slab, ds_slab, ms_slab, vs_slab = _small_reduce_adamw(g_all, w_slab, m_slab, v_slab)
    shapes = [w.shape for w in small_w] + [dwconv.shape, (LANES,)]
    gs = _unpack(gs_slab, g_sizes, shapes)
    ds = _unpack(ds_slab, g_sizes, shapes)
    ms = _unpack(ms_slab, g_sizes, shapes)
    vs = _unpack(vs_slab, g_sizes, shapes)
    small = {n: (gs[i], ds[i], ms[i], vs[i]) for i, n in enumerate(small_names)}
    loss = gs[-1][0]
    g_wconv = lax.dynamic_slice(gs[-2], (0, chip * n_cv), (CONV_TAPS, n_cv))
    d_wconv, m_wconv, v_wconv = _adamw_plain("adamw_conv", w_conv[0], g_wconv, m_w_conv[0], v_w_conv[0])
    small["w_conv"] = (g_wconv[None], d_wconv[None], m_wconv[None], v_wconv[None])

    dmod_all = g_all[:, :N_MOD * d // LANES, :].reshape(N_DEV, N_MOD * d)
    dmod_sel = lax.dynamic_slice(dmod_all, (0, chip * n_ada), (N_DEV, n_ada)).astype(BF16)
    act_t = act_all.T.astype(BF16)
    big["w_ada"] = _adamw_ada(w_ada[0], m_w_ada[0], v_w_ada[0], act_t, dmod_sel)

    half_mi, got_mi = reduced_end("mi", sw_mi, big["w_ada"][1])
    big["w_mlp_in"] = _adamw_halves("adamw_w_mlp_in", c_arr, w_mlp_in[0], m_w_mlp_in[0], v_w_mlp_in[0], half_mi, got_mi)
    half_out, got_out = reduced_end("out", sw_out, big["w_mlp_in"][1])
    big["w_out"] = _adamw_halves("adamw_w_out", c_arr, w_out[0], m_w_out[0], v_w_out[0], half_out, got_out)
    half_in = rs_end(st_in, big["w_out"][1])
    (got_in,) = _swap_reduced("comm_swap_reduced_in", [half_in])
    big["w_in"] = _adamw_halves("adamw_w_in", c_arr, w_in[0], m_w_in[0], v_w_in[0], half_in, got_in)

    order = ["w_ada", "b_ada", "g_norm_mix", "w_in", "w_conv", "b_conv", "w_rg_a", "b_rg_a", "w_rg_x", "b_rg_x",
             "lru_lambda", "g_attn_out", "g_lru_out", "w_out", "g_norm_mlp", "w_mlp_in", "w_mlp_out", "g_norm_final"]
    res = {}
    for n in order:
        res[n] = tuple(t[None] for t in big[n]) if n in big else small[n]
    return (loss, grad_x[None],
            *[res[n][0] for n in order], *[res[n][1] for n in order],
            *[res[n][2] for n in order], *[res[n][3] for n in order])
```

```python
import functools
import math

import jax
import jax.numpy as jnp
from jax import lax
from jax.experimental import pallas as pl
from jax.experimental.pallas import tpu as pltpu

F32 = jnp.float32
BF16 = jnp.bfloat16
SDS = jax.ShapeDtypeStruct
MESH = pl.DeviceIdType.MESH

EPS = 1e-6
HEAD = 128
N_MOD = 6
CONV_TAPS = 4
LRU_C = 8.0
ADAM_LR, ADAM_B1, ADAM_B2, ADAM_EPS, ADAM_WD, ADAM_STEP = 0.001, 0.9, 0.999, 1e-08, 0.01, 10
N_DEV = 8
N_CHIP = 4
LANES = 128
SUBLANES = 8
VMEM_LIMIT = 56 * 1024 * 1024
PACK_ROWS = 256
MM_TILE_M, MM_TILE_N, MM_TILE_K = 1024, 1024, 2048
ROW_TILE = 256
ROW_SPLIT = 1


def _tile(dim, pref):
    t = min(dim, pref)
    while dim % t:
        t -= LANES
    return t


def _params(sem=None):
    return pltpu.CompilerParams(dimension_semantics=sem, vmem_limit_bytes=VMEM_LIMIT)


def _sigmoid(x):
    return 1.0 / (1.0 + jnp.exp(-x))


def _log_sigmoid(x):
    return jnp.minimum(x, 0.0) - jnp.log(1.0 + jnp.exp(-jnp.abs(x)))


def _gelu_parts(x):
    k0, k1 = math.sqrt(2.0 / math.pi), 0.044715
    t = jnp.tanh(k0 * (x + k1 * x * x * x))
    val = 0.5 * x * (1.0 + t)
    der = 0.5 * (1.0 + t) + 0.5 * x * (1.0 - t * t) * k0 * (1.0 + 3.0 * k1 * x * x)
    return val, der


def _dot(a, b):
    return jnp.dot(a, b, preferred_element_type=F32)


def _dot_nt(a, b):
    return lax.dot_general(a, b, (((1,), (1,)), ((), ())), preferred_element_type=F32)


def _dot_tn(a, b):
    return lax.dot_general(a, b, (((0,), (0,)), ((), ())), preferred_element_type=F32)


def _split_dot(x, tri):
    hi = x.astype(BF16)
    lo = (x - hi.astype(F32)).astype(BF16)
    return _dot(hi, tri) + _dot(lo, tri)


def _matmul(name, a, b, mode, m, n, k, out_dtypes, *, b_off=0, extras=(), extra_kinds=(), epilogue=None,
            tm=MM_TILE_M, tn=MM_TILE_N, tk=MM_TILE_K):
    tm, tn, tk = _tile(m, tm), _tile(math.gcd(n, b_off) if b_off else n, tn), _tile(k, tk)
    assert b_off % tn == 0
    nk = k // tk
    n_ex, n_out = len(extras), len(out_dtypes)
    dot = {"nn": _dot, "nt": _dot_nt, "tn": _dot_tn}[mode]

    def body(a_ref, b_ref, *rest):
        ex, outs = rest[:n_ex], rest[n_ex:n_ex + n_out]

        def finish(total):
            res = epilogue(total, *[e[...] for e in ex]) if epilogue else (total,)
            for o, r in zip(outs, res):
                o[...] = r.astype(o.dtype)

        if nk == 1:
            finish(dot(a_ref[...], b_ref[...]))
            return
        acc = rest[-1]
        kk = pl.program_id(2)

        @pl.when(kk == 0)
        def _():
            acc[...] = dot(a_ref[...], b_ref[...])

        @pl.when(jnp.logical_and(kk > 0, kk < nk - 1))
        def _():
            acc[...] += dot(a_ref[...], b_ref[...])

        @pl.when(kk == nk - 1)
        def _():
            finish(acc[...] + dot(a_ref[...], b_ref[...]))

    if mode == "nn":
        a_spec = pl.BlockSpec((tm, tk), lambda i, j, kk: (i, kk))
        b_spec = pl.BlockSpec((tk, tn), lambda i, j, kk: (kk, j + b_off // tn))
    elif mode == "nt":
        a_spec = pl.BlockSpec((tm, tk), lambda i, j, kk: (i, kk))
        b_spec = pl.BlockSpec((tn, tk), lambda i, j, kk: (j, kk + b_off // tk))
    else:
        a_spec = pl.BlockSpec((tk, tm), lambda i, j, kk: (kk, i))
        b_spec = pl.BlockSpec((tk, tn), lambda i, j, kk: (kk, j))
    tile_spec = pl.BlockSpec((tm, tn), lambda i, j, kk: (i, j))
    row_spec = pl.BlockSpec((1, tn), lambda i, j, kk: (0, j))
    ex_specs = [tile_spec if kind == "tile" else row_spec for kind in extra_kinds]
    outs = pl.pallas_call(
        body, grid=(m // tm, n // tn, nk),
        in_specs=[a_spec, b_spec] + ex_specs,
        out_specs=tuple(tile_spec for _ in out_dtypes),
        out_shape=tuple(SDS((m, n), dt) for dt in out_dtypes),
        scratch_shapes=[pltpu.VMEM((tm, tn), F32)] if nk > 1 else [],
        compiler_params=_params(("parallel", "parallel", "arbitrary")),
        name=name,
    )(a, b, *extras)
    return outs


def _row_specs(s, d, tr):
    row = "row"
    vec = pl.BlockSpec((1, d), lambda i: (0, 0))
    col = pl.BlockSpec((tr, 1), lambda i: (i, 0))
    return row, vec, col


class _ColChunks:
    def __init__(self, refs):
        self.refs = refs

    def __getitem__(self, idx):
        return jnp.concatenate([r[...] for r in self.refs], axis=-1)


def _rows_call(name, body, grid, in_specs, out_specs, out_shape, semantics, args):
    in_x, args_x, groups = [], [], []
    for spec, arr in zip(in_specs, args):
        if isinstance(spec, str):
            rows, d = arr.shape
            tr, dc = rows // grid[0], d // ROW_SPLIT
            in_x += [pl.BlockSpec((tr, dc), functools.partial(lambda i, jj: (i, jj), jj=j)) for j in range(ROW_SPLIT)]
            args_x += [arr] * ROW_SPLIT
            groups.append(ROW_SPLIT)
        else:
            in_x.append(spec)
            args_x.append(arr)
            groups.append(1)
    out_x = [pl.BlockSpec((sh.shape[0] // grid[0], sh.shape[1]), lambda i: (i, 0)) if isinstance(spec, str) else spec
             for spec, sh in zip(out_specs, out_shape)]

    def wrapped(*refs):
        views, k = [], 0
        for g in groups:
            views.append(_ColChunks(refs[k:k + g]) if g > 1 else refs[k])
            k += g
        body(*views, *refs[k:])

    return pl.pallas_call(
        wrapped, grid=grid, in_specs=in_x, out_specs=tuple(out_x), out_shape=tuple(out_shape),
        compiler_params=_params(semantics), name=name)(*args_x)


def _norm_mod_fwd(name, x, g, sc, sh):
    s, d = x.shape
    tr = _tile(s, ROW_TILE)
    row, vec, col = _row_specs(s, d, tr)

    def body(x_ref, g_ref, sc_ref, sh_ref, h_ref, r_ref):
        xv = x_ref[...]
        r = lax.rsqrt(jnp.mean(xv * xv, axis=-1, keepdims=True) + EPS)
        h_ref[...] = ((xv * r * g_ref[...]) * (1.0 + sc_ref[...]) + sh_ref[...]).astype(BF16)
        r_ref[...] = r

    return _rows_call(name, body, (s // tr,), [row, vec, vec, vec], (row, col),
                      (SDS((s, d), BF16), SDS((s, 1), F32)), ("parallel",), (x, g, sc, sh))


def _norm_mod_bwd(name, dh, xin, rstd, g, sc, dres, gate=None):
    s, d = xin.shape
    tr = _tile(s, ROW_TILE)
    row, vec, col = _row_specs(s, d, tr)

    n_gate = 2 if gate is not None else 0

    def body(dh_ref, x_ref, r_ref, g_ref, sc_ref, dres_ref, *rest):
        gate_in, gate_out = rest[:n_gate], rest[n_gate + 4:]
        dx_ref, dsh_ref, dsc_ref, dg_ref = rest[n_gate:n_gate + 4]

        @pl.when(pl.program_id(0) == 0)
        def _():
            for ref in (dsh_ref, dsc_ref, dg_ref) + tuple(gate_out[1:]):
                ref[...] = jnp.zeros_like(ref)

        dh_v, xv, r, gv = dh_ref[...], x_ref[...], r_ref[...], g_ref[...]
        n0 = xv * r
        dsh_ref[...] += jnp.sum(dh_v, axis=0, keepdims=True)
        dsc_ref[...] += jnp.sum(dh_v * (n0 * gv), axis=0, keepdims=True)
        dn = dh_v * (1.0 + sc_ref[...])
        dg_ref[...] += jnp.sum(dn * n0, axis=0, keepdims=True)
        gy = dn * gv
        dot = jnp.mean(gy * xv, axis=-1, keepdims=True)
        dxv = dres_ref[...] + r * gy - xv * (r * r * r * dot)
        dx_ref[...] = dxv
        if gate is not None:
            y_ref, gt_ref = gate_in
            dy_ref, dgt_ref = gate_out
            dy_ref[...] = (gt_ref[...] * dxv).astype(BF16)
            dgt_ref[...] += jnp.sum(dxv * y_ref[...], axis=0, keepdims=True)

    vecs = SDS((1, d), F32)
    gate_args = tuple(gate) if gate is not None else ()
    return _rows_call(
        name, body, (s // tr,),
        [row, row, col, vec, vec, row] + ([row, vec] if gate is not None else []),
        (row, vec, vec, vec) + ((row, vec) if gate is not None else ()),
        (SDS((s, d), F32), vecs, vecs, vecs) + ((SDS((s, d), BF16), vecs) if gate is not None else ()),
        ("arbitrary",), (dh, xin, rstd, g, sc, dres, *gate_args))


def _final_loss(x2, gf, tgt, y, gt):
    s, d = x2.shape
    tr = _tile(s, ROW_TILE)
    row, vec, _ = _row_specs(s, d, tr)
    lrow = pl.BlockSpec((1, LANES), lambda i: (0, 0))

    def body(x_ref, g_ref, t_ref, y_ref, gt_ref, dx_ref, loss_ref, dg_ref, dy_ref, dgt_ref):
        @pl.when(pl.program_id(0) == 0)
        def _():
            loss_ref[...] = jnp.zeros_like(loss_ref)
            dg_ref[...] = jnp.zeros_like(dg_ref)
            dgt_ref[...] = jnp.zeros_like(dgt_ref)

        xv, gv = x_ref[...], g_ref[...]
        r = lax.rsqrt(jnp.mean(xv * xv, axis=-1, keepdims=True) + EPS)
        n0 = xv * r
        err = n0 * gv - t_ref[...]
        loss_ref[...] += jnp.sum(err * err) * (0.5 / d)
        dy = err * (1.0 / d)
        dg_ref[...] += jnp.sum(dy * n0, axis=0, keepdims=True)
        gy = dy * gv
        dot = jnp.mean(gy * xv, axis=-1, keepdims=True)
        dxv = r * gy - xv * (r * r * r * dot)
        dx_ref[...] = dxv
        dy_ref[...] = (gt_ref[...] * dxv).astype(BF16)
        dgt_ref[...] += jnp.sum(dxv * y_ref[...], axis=0, keepdims=True)

    return _rows_call(
        "final_loss", body, (s // tr,), [row, vec, row, row, vec], (row, lrow, vec, row, vec),
        (SDS((s, d), F32), SDS((1, LANES), F32), SDS((1, d), F32), SDS((s, d), BF16), SDS((1, d), F32)),
        ("arbitrary",), (x2, gf, tgt, y, gt))


def _mix_norm_fwd(oa, ol, ga, gl):
    s, w = oa.shape
    tr = _tile(s, ROW_TILE)
    row, vec, col = _row_specs(s, w, tr)

    def body(oa_ref, ol_ref, ga_ref, gl_ref, mx_ref, ra_ref, rl_ref):
        a, l = oa_ref[...], ol_ref[...]
        ra = lax.rsqrt(jnp.mean(a * a, axis=-1, keepdims=True) + EPS)
        rl = lax.rsqrt(jnp.mean(l * l, axis=-1, keepdims=True) + EPS)
        mx_ref[:, :w] = (a * ra * ga_ref[...]).astype(BF16)
        mx_ref[:, w:] = (l * rl * gl_ref[...]).astype(BF16)
        ra_ref[...] = ra
        rl_ref[...] = rl

    return _rows_call(
        "mix_norm_fwd", body, (s // tr,), [row, row, vec, vec], (row, col, col),
        (SDS((s, 2 * w), BF16), SDS((s, 1), F32), SDS((s, 1), F32)), ("parallel",), (oa, ol, ga, gl))


def _mix_norm_bwd(dmx, oa, ol, ra, rl, ga, gl):
    s, w = oa.shape
    tr = _tile(s, ROW_TILE)
    row, vec, col = _row_specs(s, w, tr)

    def body(dm_ref, oa_ref, ol_ref, ra_ref, rl_ref, ga_ref, gl_ref, doa_ref, dol_ref, dga_ref, dgl_ref):
        @pl.when(pl.program_id(0) == 0)
        def _():
            dga_ref[...] = jnp.zeros_like(dga_ref)
            dgl_ref[...] = jnp.zeros_like(dgl_ref)

        def one(dy, xv, r, gv, dg_ref):
            dg_ref[...] += jnp.sum(dy * (xv * r), axis=0, keepdims=True)
            gy = dy * gv
            dot = jnp.mean(gy * xv, axis=-1, keepdims=True)
            return r * gy - xv * (r * r * r * dot)

        dm = dm_ref[...]
        doa_ref[...] = one(dm[:, :w], oa_ref[...], ra_ref[...], ga_ref[...], dga_ref).astype(BF16)
        dol_ref[...] = one(dm[:, w:], ol_ref[...], rl_ref[...], gl_ref[...], dgl_ref)

    return _rows_call(
        "mix_norm_bwd", body, (s // tr,), [row, row, row, col, col, vec, vec], (row, row, vec, vec),
        (SDS((s, w), BF16), SDS((s, w), F32), SDS((1, w), F32), SDS((1, w), F32)), ("arbitrary",),
        (dmx, oa, ol, ra, rl, ga, gl))


def _attn_blocks(qs, ks, tri_after, csums, causal):
    zs = [_dot_nt(q, k) * (HEAD ** -0.5) for q, k in zip(qs, ks)]
    lbs = [_log_sigmoid(z) for z in zs]
    lss = [lb - z for lb, z in zip(lbs, zs)]
    if causal is not None:
        lss = [jnp.where(causal, ls, 0.0) for ls in lss]
    locs = [_split_dot(ls, tri_after) for ls in lss]
    ws = [jnp.exp(lb + (loc + cs)) for lb, loc, cs in zip(lbs, locs, csums)]
    if causal is not None:
        ws = [jnp.where(causal, w, 0.0) for w in ws]
    nxt = [cs + (loc[:, 0:1] + ls[:, 0:1]) for cs, loc, ls in zip(csums, locs, lss)]
    return lbs, ws, nxt


ATTN_HEADS_PER_STEP = 4


def _attn_tile(s):
    return 256 if s >= 1024 else 128


def _tri(t, after):
    r_i = lax.broadcasted_iota(jnp.int32, (t, t), 0)
    c_i = lax.broadcasted_iota(jnp.int32, (t, t), 1)
    return ((r_i > c_i) if after else (r_i < c_i)).astype(BF16)


def _attn_fwd(qkv, n_heads):
    s = qkv.shape[0]
    t = _attn_tile(s)
    hps = ATTN_HEADS_PER_STEP
    wid = hps * HEAD

    nq = s // t

    def body(q_ref, k_ref, v_ref, o_ref, w_ref, sg_ref):
        qi = pl.program_id(1)
        tri_after = _tri(t, True)
        causal = lax.broadcasted_iota(jnp.int32, (t, t), 1) < lax.broadcasted_iota(jnp.int32, (t, t), 0)
        lanes = [slice(a * HEAD, (a + 1) * HEAD) for a in range(hps)]
        qs = [q_ref[:, ln] for ln in lanes]

        def block(kb, carry, mask):
            off = pl.multiple_of(kb * t, t)
            ks = [k_ref[pl.ds(off, t), ln] for ln in lanes]
            lbs, ws, csums = _attn_blocks(qs, ks, tri_after, [cr[0] for cr in carry], mask)
            wbs = [w.astype(BF16) for w in ws]
            for a in range(hps):
                w_ref[a, kb] = wbs[a]
                sg_ref[a, kb] = jnp.exp(lbs[a]).astype(BF16)
            os_ = [cr[1] + _dot(wb, v_ref[pl.ds(off, t), ln]) for cr, wb, ln in zip(carry, wbs, lanes)]
            return tuple(zip(csums, os_))

        zero = tuple((jnp.zeros((t, 1), F32), jnp.zeros((t, HEAD), F32)) for _ in lanes)
        carry = block(qi, zero, causal)
        carry = lax.fori_loop(1, qi + 1, lambda it, cr: block(qi - it, cr, None), carry)
        for a, ln in enumerate(lanes):
            o_ref[:, ln] = carry[a][1]

    hb = n_heads // hps
    kept = pl.BlockSpec((None, hps, nq, t, t), lambda hh, i: (hh * nq + i, 0, 0, 0, 0))
    kept_shape = SDS((hb * nq, hps, nq, t, t), BF16)
    return pl.pallas_call(
        body, grid=(hb, nq),
        in_specs=[pl.BlockSpec((t, wid), lambda hh, i: (i, hh)),
                  pl.BlockSpec((s, wid), lambda hh, i: (0, hb + hh)),
                  pl.BlockSpec((s, wid), lambda hh, i: (0, 2 * hb + hh))],
        out_specs=(pl.BlockSpec((t, wid), lambda hh, i: (i, hh)), kept, kept),
        out_shape=(SDS((s, n_heads * HEAD), F32), kept_shape, kept_shape),
        compiler_params=_params(("parallel", "parallel")), name="attn_fwd")(qkv, qkv, qkv)


def _attn_bwd(qkv, do, w_kept, sg_kept, n_heads):
    s = qkv.shape[0]
    t = _attn_tile(s)
    nq = s // t
    scale = HEAD ** -0.5
    hps = ATTN_HEADS_PER_STEP
    wid = hps * HEAD

    def body(q_ref, k_ref, v_ref, do_ref, w_ref, sg_ref, dq_ref, dk_ref, dv_ref, dk_acc, dv_acc):
        qi = pl.program_id(1)

        @pl.when(qi == 0)
        def _():
            dk_acc[...] = jnp.zeros_like(dk_acc)
            dv_acc[...] = jnp.zeros_like(dv_acc)

        tri_before = _tri(t, False)
        causal = lax.broadcasted_iota(jnp.int32, (t, t), 1) < lax.broadcasted_iota(jnp.int32, (t, t), 0)
        lanes = [slice(a * HEAD, (a + 1) * HEAD) for a in range(hps)]
        qs = [q_ref[:, ln] for ln in lanes]
        douts = [do_ref[:, ln] for ln in lanes]

        def block(kb, carry, mask):
            off = pl.multiple_of(kb * t, t)
            wbs = [w_ref[a, kb] for a in range(hps)]
            dws = [_dot_nt(dout, v_ref[pl.ds(off, t), ln]) for dout, ln in zip(douts, lanes)]
            for a, ln in enumerate(lanes):
                dv_acc[pl.ds(off, t), ln] += _dot_tn(wbs[a], douts[a])
            es = [dw * wb.astype(F32) for dw, wb in zip(dws, wbs)]
            locs = [_split_dot(e, tri_before) for e in es]
            sgs = [sg_ref[a, kb].astype(F32) for a in range(hps)]
            stays = [(loc + cr[0]) * sg for loc, cr, sg in zip(locs, carry, sgs)]
            if mask is not None:
                stays = [jnp.where(mask, st, 0.0) for st in stays]
            dzbs = [((e * (1.0 - sg) - st) * scale).astype(BF16) for e, sg, st in zip(es, sgs, stays)]
            dqs = [cr[1] + _dot(dzb, k_ref[pl.ds(off, t), ln]) for cr, dzb, ln in zip(carry, dzbs, lanes)]
            for a, ln in enumerate(lanes):
                dk_acc[pl.ds(off, t), ln] += _dot_tn(dzbs[a], qs[a])
            esums = [cr[0] + (loc[:, t - 1:t] + e[:, t - 1:t]) for cr, loc, e in zip(carry, locs, es)]
            return tuple(zip(esums, dqs))

        zero = tuple((jnp.zeros((t, 1), F32), jnp.zeros((t, HEAD), F32)) for _ in lanes)
        carry = lax.fori_loop(0, qi, lambda kb, cr: block(kb, cr, None), zero)
        carry = block(qi, carry, causal)
        for a, ln in enumerate(lanes):
            dq_ref[:, ln] = carry[a][1].astype(BF16)

        @pl.when(qi == nq - 1)
        def _():
            dk_ref[...] = dk_acc[...].astype(BF16)
            dv_ref[...] = dv_acc[...].astype(BF16)

    hb = n_heads // hps
    blk = pl.BlockSpec((t, wid), lambda hh, i: (i, hh))
    full = pl.BlockSpec((s, wid), lambda hh, i: (0, hh))
    kept = pl.BlockSpec((None, hps, nq, t, t), lambda hh, i: (hh * nq + i, 0, 0, 0, 0))
    return pl.pallas_call(
        body, grid=(hb, nq),
        in_specs=[blk,
                  pl.BlockSpec((s, wid), lambda hh, i: (0, hb + hh)),
                  pl.BlockSpec((s, wid), lambda hh, i: (0, 2 * hb + hh)),
                  blk, kept, kept],
        out_specs=(blk, full, full),
        out_shape=(SDS((s, n_heads * HEAD), BF16),) * 3,
        scratch_shapes=[pltpu.VMEM((s, wid), F32), pltpu.VMEM((s, wid), F32)],
        compiler_params=_params(("parallel", "arbitrary")), name="attn_bwd")(qkv, qkv, qkv, do, w_kept, sg_kept)


def _lru_chunk(s):
    return 256 if s >= 1024 else 128


def _lru_gates(xc, wa, ba, wx, bx, sp):
    xb = xc.astype(BF16)
    r = _sigmoid(_dot(xb, wa) + ba)
    ig = _sigmoid(_dot(xb, wx) + bx)
    la = -LRU_C * r * sp
    a = jnp.exp(la)
    t = jnp.tanh(la)
    mult = jnp.sqrt(-2.0 * t / (1.0 - t))
    return r, ig, a, mult


def _softplus_neg(lam):
    return jnp.maximum(-lam, 0.0) + jnp.log(1.0 + jnp.exp(-jnp.abs(lam)))


LRU_BLOCKS_PER_STEP = 1


def _lru_specs(s, n_blocks):
    bps = min(LRU_BLOCKS_PER_STEP, n_blocks)
    wid = bps * HEAD
    seq0 = pl.BlockSpec((s, wid), lambda h: (0, h))
    seq1 = pl.BlockSpec((s, wid), lambda h: (0, n_blocks // bps + h))
    taps = pl.BlockSpec((CONV_TAPS, wid), lambda h: (0, h))
    vec = pl.BlockSpec((1, wid), lambda h: (0, h))
    mat = pl.BlockSpec((bps, HEAD, HEAD), lambda h: (h, 0, 0))
    return bps, seq0, seq1, taps, vec, mat


def _per_block(one_block, n_2d, n_mat_pos, bps):
    def body(*refs):
        for a in range(bps):
            views = [r.at[a] if i in n_mat_pos else r.at[:, pl.ds(a * HEAD, HEAD)] for i, r in enumerate(refs[:n_2d])]
            one_block(*views, *refs[n_2d:])
    return body


def _lru_fwd(xrg, wconv, bconv, wa, ba, wx, bx, lam):
    s = xrg.shape[0]
    nb = wa.shape[0]
    tc = _lru_chunk(s)
    bps, seq0, seq1, taps, vec, mat = _lru_specs(s, nb)
    pad = SUBLANES

    def one_block(xr_ref, xg_ref, wc_ref, bc_ref, wa_ref, ba_ref, wx_ref, bx_ref, lam_ref, o_ref, h_ref, pad_s, a_s, u_s):
        pad_s[0:pad, :] = jnp.zeros((pad, HEAD), F32)
        pad_s[pad:pad + s, :] = xr_ref[...]
        wab, wxb = wa_ref[...].astype(BF16), wx_ref[...].astype(BF16)
        sp = _softplus_neg(lam_ref[...])
        for c in range(s // tc):
            base = c * tc
            xc = bc_ref[...] + sum(wc_ref[i:i + 1, :] * pad_s[pl.ds(base + pad - (CONV_TAPS - 1) + i, tc), :]
                                   for i in range(CONV_TAPS))
            _, ig, a, mult = _lru_gates(xc, wab, ba_ref[...], wxb, bx_ref[...], sp)
            a_s[base:base + tc, :] = a
            u_s[base:base + tc, :] = mult * (ig * xc)

        row = lax.broadcasted_iota(jnp.int32, (SUBLANES, HEAD), 0)

        def chunk(ci, hprev):
            off = pl.multiple_of(ci * SUBLANES, SUBLANES)
            a8, b8 = a_s[pl.ds(off, SUBLANES), :], u_s[pl.ds(off, SUBLANES), :]
            for d in (1, 2, 4):
                a_sh = jnp.where(row < d, 1.0, pltpu.roll(a8, d, 0))
                b_sh = jnp.where(row < d, 0.0, pltpu.roll(b8, d, 0))
                b8 = a8 * b_sh + b8
                a8 = a8 * a_sh
            h8 = a8 * hprev + b8
            h_ref[pl.ds(off, SUBLANES), :] = h8
            return h8[SUBLANES - 1:SUBLANES, :]

        lax.fori_loop(0, s // SUBLANES, chunk, jnp.zeros((1, HEAD), F32), unroll=8)
        for c in range(s // tc):
            sl = slice(c * tc, (c + 1) * tc)
            gel, _ = _gelu_parts(xg_ref[sl, :])
            o_ref[sl, :] = h_ref[sl, :] * gel

    return pl.pallas_call(
        _per_block(one_block, 11, (4, 6), bps), grid=(nb // bps,),
        in_specs=[seq0, seq1, taps, vec, mat, vec, mat, vec, vec],
        out_specs=(seq0, seq0),
        out_shape=(SDS((s, nb * HEAD), F32), SDS((s, nb * HEAD), F32)),
        scratch_shapes=[pltpu.VMEM((s + pad, HEAD), F32), pltpu.VMEM((s, HEAD), F32), pltpu.VMEM((s, HEAD), F32)],
        compiler_params=_params(("parallel",)), name="lru_fwd")(xrg, xrg, wconv, bconv, wa, ba, wx, bx, lam)


def _lru_bwd(xrg, dol, hseq, wconv, bconv, wa, ba, wx, bx, lam):
    s = xrg.shape[0]
    nb = wa.shape[0]
    tc = _lru_chunk(s)
    bps, seq0, seq1, taps, vec, mat = _lru_specs(s, nb)
    pad = SUBLANES

    def one_block(xr_ref, xg_ref, do_ref, h_ref, wc_ref, bc_ref, wa_ref, ba_ref, wx_ref, bx_ref, lam_ref,
             dxr_ref, dxg_ref, dwc_ref, dbc_ref, dwa_ref, dba_ref, dwx_ref, dbx_ref, dlam_ref,
             pad_s, hp_s, a_s, g_s, da_s, dxc_s):
        pad_s[0:pad, :] = jnp.zeros((pad, HEAD), F32)
        pad_s[pad:pad + s, :] = xr_ref[...]
        hp_s[0:pad, :] = jnp.zeros((pad, HEAD), F32)
        hp_s[pad:pad + s, :] = h_ref[...]
        a_s[s:s + pad, :] = jnp.zeros((pad, HEAD), F32)
        dxc_s[s:s + pad, :] = jnp.zeros((pad, HEAD), F32)
        wab, wxb = wa_ref[...].astype(BF16), wx_ref[...].astype(BF16)
        lam_v = lam_ref[...]
        sp = _softplus_neg(lam_v)

        def conv_in(c):
            base = c * tc
            wins = [pad_s[pl.ds(base + pad - (CONV_TAPS - 1) + i, tc), :] for i in range(CONV_TAPS)]
            xc = bc_ref[...] + sum(wc_ref[i:i + 1, :] * wins[i] for i in range(CONV_TAPS))
            return xc, wins

        for c in range(s // tc):
            sl = slice(c * tc, (c + 1) * tc)
            xc, _ = conv_in(c)
            _, _, a, _ = _lru_gates(xc, wab, ba_ref[...], wxb, bx_ref[...], sp)
            a_s[sl, :] = a
            gel, dgel = _gelu_parts(xg_ref[sl, :])
            dov = do_ref[sl, :]
            g_s[sl, :] = dov * gel
            dxg_ref[sl, :] = (dov * h_ref[sl, :] * dgel).astype(BF16)

        row = lax.broadcasted_iota(jnp.int32, (SUBLANES, HEAD), 0)
        n_chunks = s // SUBLANES

        def chunk(it, gnext):
            ci = n_chunks - 1 - it
            off = pl.multiple_of(ci * SUBLANES, SUBLANES)
            a8 = a_s[pl.ds(off, SUBLANES), :]
            a8n = a_s[pl.ds(off + SUBLANES, SUBLANES), :]
            c8 = pltpu.roll(jnp.where(row == 0, a8n, a8), SUBLANES - 1, 0)
            g8 = g_s[pl.ds(off, SUBLANES), :]
            for d in (1, 2, 4):
                c_sh = jnp.where(row >= SUBLANES - d, 1.0, pltpu.roll(c8, SUBLANES - d, 0))
                g_sh = jnp.where(row >= SUBLANES - d, 0.0, pltpu.roll(g8, SUBLANES - d, 0))
                g8 = c8 * g_sh + g8
                c8 = c8 * c_sh
            g8 = g8 + c8 * gnext
            g_s[pl.ds(off, SUBLANES), :] = g8
            h8 = hp_s[pl.ds(off + pad, SUBLANES), :]
            h8p = hp_s[pl.ds(off, SUBLANES), :]
            da_s[pl.ds(off, SUBLANES), :] = g8 * pltpu.roll(jnp.where(row == SUBLANES - 1, h8p, h8), 1, 0)
            return g8[0:1, :]

        lax.fori_loop(0, n_chunks, chunk, jnp.zeros((1, HEAD), F32), unroll=8)

        dsp = jnp.zeros((1, HEAD), F32)
        dbc = jnp.zeros((1, HEAD), F32)
        dba = jnp.zeros((1, HEAD), F32)
        dbx = jnp.zeros((1, HEAD), F32)
        dwa = jnp.zeros((HEAD, HEAD), F32)
        dwx = jnp.zeros((HEAD, HEAD), F32)
        dwc = [jnp.zeros((1, HEAD), F32) for _ in range(CONV_TAPS)]
        for c in range(s // tc):
            sl = slice(c * tc, (c + 1) * tc)
            xc, wins = conv_in(c)
            r, ig, a, mult = _lru_gates(xc, wab, ba_ref[...], wxb, bx_ref[...], sp)
            du, da = g_s[sl, :], da_s[sl, :]
            d_ix = du * mult
            dla = da * a - (du * ig * xc) * (a * a / mult)
            dsp = dsp + jnp.sum(dla * r, axis=0, keepdims=True) * (-LRU_C)
            dpa = (dla * (-LRU_C * sp)) * r * (1.0 - r)
            dpx = (d_ix * xc) * ig * (1.0 - ig)
            dpab, dpxb, xb = dpa.astype(BF16), dpx.astype(BF16), xc.astype(BF16)
            dxc = d_ix * ig + _dot_nt(dpab, wab) + _dot_nt(dpxb, wxb)
            dwa = dwa + _dot_tn(xb, dpab)
            dwx = dwx + _dot_tn(xb, dpxb)
            dba = dba + jnp.sum(dpa, axis=0, keepdims=True)
            dbx = dbx + jnp.sum(dpx, axis=0, keepdims=True)
            dbc = dbc + jnp.sum(dxc, axis=0, keepdims=True)
            for i in range(CONV_TAPS):
                dwc[i] = dwc[i] + jnp.sum(dxc * wins[i], axis=0, keepdims=True)
            dxc_s[sl, :] = dxc

        for c in range(s // tc):
            base = c * tc
            dxr = sum(wc_ref[i:i + 1, :] * dxc_s[pl.ds(base + (CONV_TAPS - 1) - i, tc), :] for i in range(CONV_TAPS))
            dxr_ref[base:base + tc, :] = dxr.astype(BF16)

        for i in range(CONV_TAPS):
            dwc_ref[i:i + 1, :] = dwc[i]
        dbc_ref[...] = dbc
        dwa_ref[...] = dwa
        dwx_ref[...] = dwx
        dba_ref[...] = dba
        dbx_ref[...] = dbx
        dlam_ref[...] = dsp * (-_sigmoid(-lam_v))

    w = nb * HEAD
    return pl.pallas_call(
        _per_block(one_block, 20, (6, 8, 15, 17), bps), grid=(nb // bps,),
        in_specs=[seq0, seq1, seq0, seq0, taps, vec, mat, vec, mat, vec, vec],
        out_specs=(seq0, seq0, taps, vec, mat, vec, mat, vec, vec),
        out_shape=(SDS((s, w), BF16), SDS((s, w), BF16), SDS((CONV_TAPS, w), F32), SDS((1, w), F32),
                   SDS((nb, HEAD, HEAD), F32), SDS((1, w), F32), SDS((nb, HEAD, HEAD), F32), SDS((1, w), F32),
                   SDS((1, w), F32)),
        scratch_shapes=[pltpu.VMEM((s + pad, HEAD), F32), pltpu.VMEM((s + pad, HEAD), F32),
                        pltpu.VMEM((s + pad, HEAD), F32), pltpu.VMEM((s, HEAD), F32),
                        pltpu.VMEM((s, HEAD), F32), pltpu.VMEM((s + pad, HEAD), F32)],
        compiler_params=_params(("parallel",)), name="lru_bwd",
    )(xrg, xrg, dol, hseq, wconv, bconv, wa, ba, wx, bx, lam)


def _ada_mod(c_all, w_sh, b_sh):
    n_ex, d = c_all.shape
    n = w_sh.shape[1]
    tn = _tile(n, 512)

    def body(c_ref, w_ref, b_ref, mod_ref, act_ref):
        cv = c_ref[...]
        act = cv * _sigmoid(cv)
        act_ref[...] = act
        mod_ref[...] = _dot(act.astype(BF16), w_ref[...].astype(BF16)) + b_ref[...]

    return pl.pallas_call(
        body, grid=(n // tn,),
        in_specs=[pl.BlockSpec((n_ex, d), lambda j: (0, 0)), pl.BlockSpec((d, tn), lambda j: (0, j)),
                  pl.BlockSpec((1, tn), lambda j: (0, j))],
        out_specs=(pl.BlockSpec((n_ex, tn), lambda j: (0, j)), pl.BlockSpec((n_ex, d), lambda j: (0, 0))),
        out_shape=(SDS((n_ex, n), F32), SDS((n_ex, d), F32)),
        compiler_params=_params(("arbitrary",)), name="ada_mod")(c_all, w_sh, b_sh)


def _adamw_math(w, g, m, v):
    m = ADAM_B1 * m + (1.0 - ADAM_B1) * g
    v = ADAM_B2 * v + (1.0 - ADAM_B2) * (g * g)
    m_hat = m / (1.0 - ADAM_B1 ** ADAM_STEP)
    v_hat = v / (1.0 - ADAM_B2 ** ADAM_STEP)
    delta = -ADAM_LR * (m_hat / (jnp.sqrt(v_hat) + ADAM_EPS) + ADAM_WD * w)
    return delta, m, v


def _adamw_plain(name, w, g, m, v):
    def body(w_ref, g_ref, m_ref, v_ref, d_ref, mo_ref, vo_ref):
        d_ref[...], mo_ref[...], vo_ref[...] = _adamw_math(w_ref[...], g_ref[...], m_ref[...], v_ref[...])

    return pl.pallas_call(body, out_shape=(SDS(w.shape, F32),) * 3, name=name)(w, g, m, v)


def _adamw_halves(name, c_arr, w, m, v, g_own, g_recv):
    r, n = w.shape
    rh = r // 2
    tr = _tile(rh, 256)
    nh = rh // tr

    def body(c_ref, w_ref, m_ref, v_ref, go_ref, gr_ref, g_ref, d_ref, mo_ref, vo_ref):
        own = (pl.program_id(0) // nh) == c_ref[0]
        g = jnp.where(own, go_ref[...], gr_ref[...])
        g_ref[...] = g
        d_ref[...], mo_ref[...], vo_ref[...] = _adamw_math(w_ref[...], g, m_ref[...], v_ref[...])

    full = pl.BlockSpec((tr, n), lambda i, c: (i, 0))
    own = pl.BlockSpec((tr, n), lambda i, c: (jnp.where(i // nh == c[0], i % nh, 0), 0))
    recv = pl.BlockSpec((tr, n), lambda i, c: (jnp.where(i // nh == c[0], 0, i % nh), 0))
    return pl.pallas_call(
        body,
        grid_spec=pltpu.PrefetchScalarGridSpec(
            num_scalar_prefetch=1, grid=(2 * nh,), in_specs=[full, full, full, own, recv],
            out_specs=(full,) * 4),
        out_shape=(SDS((r, n), F32),) * 4,
        compiler_params=_params(("parallel",)), name=name)(c_arr, w, m, v, g_own, g_recv)


def _adamw_ada(w, m, v, act_t, dmod):
    d, n = w.shape
    n_ex = act_t.shape[1]
    tr = _tile(d, 256)

    def body(a_ref, dm_ref, w_ref, m_ref, v_ref, g_ref, d_ref, mo_ref, vo_ref):
        g = _dot(a_ref[...], dm_ref[...])
        g_ref[...] = g
        d_ref[...], mo_ref[...], vo_ref[...] = _adamw_math(w_ref[...], g, m_ref[...], v_ref[...])

    full = pl.BlockSpec((tr, n), lambda i: (i, 0))
    return pl.pallas_call(
        body, grid=(d // tr,),
        in_specs=[pl.BlockSpec((tr, n_ex), lambda i: (i, 0)), pl.BlockSpec((n_ex, n), lambda i: (0, 0)), full, full, full],
        out_specs=(full,) * 4, out_shape=(SDS((d, n), F32),) * 4,
        compiler_params=_params(("parallel",)), name="adamw_ada")(act_t, dmod, w, m, v)


def _small_reduce_adamw(parts, w, m, v):
    n_dev, r, _ = parts.shape
    tr = r if r <= PACK_ROWS else PACK_ROWS

    def body(p_ref, w_ref, m_ref, v_ref, g_ref, d_ref, mo_ref, vo_ref):
        g = p_ref[0]
        for k in range(1, n_dev):
            g = g + p_ref[k]
        g_ref[...] = g
        d_ref[...], mo_ref[...], vo_ref[...] = _adamw_math(w_ref[...], g, m_ref[...], v_ref[...])

    full = pl.BlockSpec((tr, LANES), lambda i: (i, 0))
    return pl.pallas_call(
        body, grid=(r // tr,),
        in_specs=[pl.BlockSpec((n_dev, tr, LANES), lambda i: (0, i, 0)), full, full, full],
        out_specs=(full,) * 4, out_shape=(SDS((r, LANES), F32),) * 4,
        compiler_params=_params(("parallel",)), name="small_reduce_adamw")(parts, w, m, v)


def _mesh_pos():
    return lax.axis_index("x"), lax.axis_index("y"), lax.axis_index("c")


def _other_chips(x, y):
    return [(1 - x, y), (x, 1 - y), (1 - x, 1 - y)]


def _all_gather_small(name, blk, after=()):
    r, n = blk.shape
    n_after = len(after)

    def body(x_ref, *rest):
        out_ref, send_sems, recv_sems, local_sem = rest[n_after:]
        x, y, c = _mesh_pos()
        me, sibling = (x, y, c), (x, y, 1 - c)
        chips = _other_chips(x, y)

        def rows(px, py, pc):
            return out_ref.at[4 * px + 2 * py + pc]

        def copy(k, block, to, src=None):
            return pltpu.make_async_remote_copy(
                src_ref=rows(*block) if src is None else src, dst_ref=rows(*block),
                send_sem=send_sems.at[k], recv_sem=recv_sems.at[k], device_id=to, device_id_type=MESH)

        mine = pltpu.make_async_copy(x_ref, rows(*me), local_sem)
        mine.start()
        first = [copy(0, me, sibling, src=x_ref)]
        first += [copy(1 + j, me, (*chip, c), src=x_ref) for j, chip in enumerate(chips)]
        for cp in first:
            cp.start()
        passed = [copy(4 + j, (*chip, c), sibling) for j, chip in enumerate(chips)]
        for j, chip in enumerate(chips):
            copy(1 + j, (*chip, c), me).wait_recv()
            passed[j].start()
        copy(0, sibling, me).wait_recv()
        for j, chip in enumerate(chips):
            copy(4 + j, (*chip, 1 - c), me).wait_recv()
        for cp in first + passed:
            cp.wait_send()
        mine.wait()

    return pl.pallas_call(
        body, out_shape=SDS((N_DEV, r, n), blk.dtype),
        in_specs=[pl.BlockSpec(memory_space=pltpu.VMEM)] + [pl.BlockSpec(memory_space=pl.ANY)] * n_after,
        out_specs=pl.BlockSpec(memory_space=pltpu.VMEM),
        scratch_shapes=[pltpu.SemaphoreType.DMA((7,)), pltpu.SemaphoreType.DMA((7,)), pltpu.SemaphoreType.DMA],
        compiler_params=pltpu.CompilerParams(vmem_limit_bytes=VMEM_LIMIT), name=name)(blk, *after)


_ANY = pl.BlockSpec(memory_space=pl.ANY)
_HBM = pl.BlockSpec(memory_space=pltpu.HBM)
_SEM = pl.BlockSpec(memory_space=pltpu.SEMAPHORE)
_EFFECT = pltpu.SideEffectType.DATAFLOW_SIDE_EFFECTING


def _hbm(a):
    return pltpu.with_memory_space_constraint(a, pltpu.HBM)


def _place_cast(name, j_arr, shard, kind, after):
    r, n = shard.shape
    tr = _tile(r, 256)
    nr = r // tr
    if kind == "col":
        out_shape, o_spec = (r, N_CHIP * n), pl.BlockSpec((tr, n), lambda i, j: (i, j[0]))
    else:
        out_shape, o_spec = (N_CHIP * r, n), pl.BlockSpec((tr, n), lambda i, j: (j[0] * nr + i, 0))

    def body(j_ref, s_ref, after_ref, o_ref, tok_ref):
        o_ref[...] = s_ref[...].astype(BF16)
        tok_ref[...] = jnp.zeros_like(tok_ref)

    return pl.pallas_call(
        body,
        grid_spec=pltpu.PrefetchScalarGridSpec(
            num_scalar_prefetch=1, grid=(nr,), in_specs=[pl.BlockSpec((tr, n), lambda i, j: (i, 0)), _ANY],
            out_specs=(o_spec, pl.BlockSpec((SUBLANES, LANES), lambda i, j: (0, 0)))),
        out_shape=(SDS(out_shape, BF16), SDS((SUBLANES, LANES), F32)),
        compiler_params=_params(("arbitrary",)), name=name)(j_arr, shard, after)


def _leg_direct(full, kind, x, y, c):
    mine = _full_region(full, kind, x, y, c)
    return [(mine, mine, (1 - x, y, c)), (mine, mine, (x, 1 - y, c))]


def _leg_relay(full, kind, x, y, c):
    fx, fy = jnp.where(c == 0, 1 - x, x), jnp.where(c == 0, y, 1 - y)
    tx, ty = jnp.where(c == 0, x, 1 - x), jnp.where(c == 0, 1 - y, y)
    got = _full_region(full, kind, fx, fy, c)
    return [(got, got, (tx, ty, c))]


def _leg_d2d(which):
    def leg(full, kind, x, y, c):
        chips = _other_chips(x, y)
        return [(_full_region(full, kind, *chips[k], c), _full_region(full, kind, *chips[k], c), (x, y, 1 - c))
                for k in which]
    return leg


_LEGS = {"direct": (_leg_direct, 2), "relay": (_leg_relay, 1), "d2d_near": (_leg_d2d((0, 1)), 2),
         "d2d_far": (_leg_d2d((2,)), 1)}


def _gather_call(name, fulls, kinds, waits, starts, after, thru):
    nw, n_wait, n_start = len(fulls), len(waits), len(starts)

    def body(*refs):
        wait_sems = refs[nw:nw + 2 * n_wait]
        outs = refs[nw + 2 * n_wait + 2:]
        full, start_sems = outs[:nw], outs[nw:nw + 2 * n_start]
        x, y, c = _mesh_pos()
        for i, (leg, ws, _, _) in enumerate(waits):
            fn, per = _LEGS[leg]
            for li, w in enumerate(ws):
                for k, (s_, d_, dev) in enumerate(fn(full[w], kinds[w], x, y, c)):
                    cp = pltpu.make_async_remote_copy(
                        src_ref=s_, dst_ref=d_, send_sem=wait_sems[2 * i].at[per * li + k],
                        recv_sem=wait_sems[2 * i + 1].at[per * li + k], device_id=dev, device_id_type=MESH)
                    cp.wait_recv()
                    cp.wait_send()
        for i, (leg, ws) in enumerate(starts):
            fn, per = _LEGS[leg]
            for li, w in enumerate(ws):
                for k, (s_, d_, dev) in enumerate(fn(full[w], kinds[w], x, y, c)):
                    pltpu.make_async_remote_copy(
                        src_ref=s_, dst_ref=d_, send_sem=start_sems[2 * i].at[per * li + k],
                        recv_sem=start_sems[2 * i + 1].at[per * li + k], device_id=dev, device_id_type=MESH).start()

    sems = []
    for leg, ws in starts:
        sems += [pltpu.SemaphoreType.DMA((_LEGS[leg][1] * len(ws),))] * 2
    wait_args = []
    for _, _, s_, r_ in waits:
        wait_args += [s_, r_]
    outs = pl.pallas_call(
        body,
        out_shape=tuple(pltpu.HBM(f_.shape, f_.dtype) for f_ in fulls) + tuple(sems) + (SDS(thru.shape, thru.dtype),),
        in_specs=[_HBM] * nw + [_SEM] * (2 * n_wait) + [_ANY, _ANY],
        out_specs=tuple([_HBM] * nw + [_SEM] * (2 * n_start) + [_ANY]),
        input_output_aliases={**{w: w for w in range(nw)}, nw + 2 * n_wait + 1: nw + 2 * n_start},
        compiler_params=pltpu.CompilerParams(has_side_effects=_EFFECT),
        name=name,
    )(*[_hbm(f_) for f_ in fulls], *wait_args, after, thru)
    pairs = [(outs[nw + 2 * i], outs[nw + 2 * i + 1]) for i in range(n_start)]
    return list(outs[:nw]), pairs, outs[nw + 2 * n_start]


def _full_region(full, kind, px, py, half):
    j = 2 * px + py
    if kind == "col":
        rh, cols = full.shape[0] // 2, full.shape[1] // N_CHIP
        return full.at[pl.ds(half * rh, rh), pl.ds(j * cols, cols)]
    rows = full.shape[0] // N_CHIP
    rh = rows // 2
    return full.at[pl.ds(j * rows + half * rh, rh), :]


def _half_of(ref, kind, half):
    if kind == "col":
        rh = ref.shape[0] // 2
        return ref.at[pl.ds(half * rh, rh), :]
    rh = ref.shape[1] // 2
    return ref.at[:, pl.ds(half * rh, rh), :]


def _half_shape(g, kind):
    if kind == "col":
        return (g.shape[0] // 2, g.shape[1])
    return (g.shape[0], g.shape[1] // 2, g.shape[2])


def _plan_swap_half(kind):
    def plan(src, land, x, y, c):
        return [(_half_of(src, kind, 1 - c), land, (x, y, 1 - c))]
    return plan


def _plan_scatter(kind):
    def plan(src, land, x, y, c):
        out = []
        for k, (px, py) in enumerate(_other_chips(x, y)):
            j = 2 * px + py
            if kind == "col":
                n = src.shape[1] // N_CHIP
                blk = src.at[:, pl.ds(j * n, n)]
            else:
                blk = src.at[j]
            out.append((blk, land.at[k], (px, py, c)))
        return out
    return plan


def _plan_whole(src, land, x, y, c):
    return [(src, land, (x, y, 1 - c))]


def _split_start(name, src, land_shape, n, plan, thru):
    def body(src_in, land_in, thru_in, send, recv, src_ref, land_ref, thru_out):
        x, y, c = _mesh_pos()
        for k, (s_, d_, dev) in enumerate(plan(src_ref, land_ref, x, y, c)):
            pltpu.make_async_remote_copy(src_ref=s_, dst_ref=d_, send_sem=send.at[k], recv_sem=recv.at[k],
                                         device_id=dev, device_id_type=MESH).start()

    sem = pltpu.SemaphoreType.DMA((n,))
    return pl.pallas_call(
        body,
        out_shape=(sem, sem, pltpu.HBM(src.shape, src.dtype), pltpu.HBM(land_shape, src.dtype), SDS(thru.shape, thru.dtype)),
        in_specs=[_HBM, _HBM, _ANY], out_specs=(_SEM, _SEM, _HBM, _HBM, _ANY),
        input_output_aliases={0: 2, 1: 3, 2: 4},
        compiler_params=pltpu.CompilerParams(has_side_effects=_EFFECT), name=name,
    )(_hbm(src), _hbm(lax.empty(land_shape, src.dtype)), thru)


def _split_wait(name, send, recv, src, land, plan, after):
    def body(src_in, land_in, send_r, recv_r, after_r, src_ref, land_ref):
        x, y, c = _mesh_pos()
        for k, (s_, d_, dev) in enumerate(plan(src_ref, land_ref, x, y, c)):
            cp = pltpu.make_async_remote_copy(src_ref=s_, dst_ref=d_, send_sem=send_r.at[k], recv_sem=recv_r.at[k],
                                              device_id=dev, device_id_type=MESH)
            cp.wait_send()
            cp.wait_recv()

    return pl.pallas_call(
        body,
        out_shape=(pltpu.HBM(src.shape, src.dtype), pltpu.HBM(land.shape, land.dtype)),
        in_specs=[_HBM, _HBM, _SEM, _SEM, _ANY], out_specs=(_HBM, _HBM),
        input_output_aliases={0: 0, 1: 1},
        compiler_params=pltpu.CompilerParams(has_side_effects=_EFFECT), name=name,
    )(src, land, send, recv, after)


def _dev_row(buf, px, py, pc):
    return buf.at[4 * px + 2 * py + pc]


def _plan_gather_own(buf, land, x, y, c):
    own = _dev_row(buf, x, y, c)
    return [(own, own, (x, y, 1 - c))] + [(own, own, (px, py, c)) for px, py in _other_chips(x, y)]


def _plan_gather_pass(buf, land, x, y, c):
    return [(_dev_row(buf, px, py, c), _dev_row(buf, px, py, c), (x, y, 1 - c)) for px, py in _other_chips(x, y)]


def _split_start_inplace(name, buf, n, plan, thru):
    def body(buf_in, thru_in, send, recv, buf_ref, thru_out):
        x, y, c = _mesh_pos()
        for k, (s_, d_, dev) in enumerate(plan(buf_ref, buf_ref, x, y, c)):
            pltpu.make_async_remote_copy(src_ref=s_, dst_ref=d_, send_sem=send.at[k], recv_sem=recv.at[k],
                                         device_id=dev, device_id_type=MESH).start()

    sem = pltpu.SemaphoreType.DMA((n,))
    return pl.pallas_call(
        body, out_shape=(sem, sem, pltpu.HBM(buf.shape, buf.dtype), SDS(thru.shape, thru.dtype)),
        in_specs=[_HBM, _ANY], out_specs=(_SEM, _SEM, _HBM, _ANY), input_output_aliases={0: 2, 1: 3},
        compiler_params=pltpu.CompilerParams(has_side_effects=_EFFECT), name=name)(_hbm(buf), thru)


def _split_wait_inplace(name, send, recv, buf, plan, after):
    def body(buf_in, send_r, recv_r, after_r, buf_ref):
        x, y, c = _mesh_pos()
        for k, (s_, d_, dev) in enumerate(plan(buf_ref, buf_ref, x, y, c)):
            cp = pltpu.make_async_remote_copy(src_ref=s_, dst_ref=d_, send_sem=send_r.at[k], recv_sem=recv_r.at[k],
                                              device_id=dev, device_id_type=MESH)
            cp.wait_send()
            cp.wait_recv()

    return pl.pallas_call(
        body, out_shape=pltpu.HBM(buf.shape, buf.dtype), in_specs=[_HBM, _SEM, _SEM, _ANY], out_specs=_HBM,
        input_output_aliases={0: 0}, compiler_params=pltpu.CompilerParams(has_side_effects=_EFFECT), name=name,
    )(buf, send, recv, after)


def _place_row(name, me_arr, slab):
    r, n = slab.shape
    tr = r if r <= PACK_ROWS else PACK_ROWS

    def body(me_ref, s_ref, o_ref):
        o_ref[...] = s_ref[...]

    return pl.pallas_call(
        body,
        grid_spec=pltpu.PrefetchScalarGridSpec(
            num_scalar_prefetch=1, grid=(r // tr,), in_specs=[pl.BlockSpec((tr, n), lambda i, me: (i, 0))],
            out_specs=pl.BlockSpec((None, tr, n), lambda i, me: (me[0], i, 0))),
        out_shape=SDS((N_DEV, r, n), slab.dtype), compiler_params=_params(("parallel",)), name=name)(me_arr, slab)


def _add_halves(name, c_arr, g, got, kind):
    if kind == "col":
        rh, n = got.shape
        tr = _tile(rh, 256)
        nh = rh // tr
        g_spec = pl.BlockSpec((tr, n), lambda i, c: (c[0] * nh + i, 0))
        o_spec = pl.BlockSpec((tr, n), lambda i, c: (i, 0))
        grid = (nh,)
    else:
        nc, rh, n = got.shape
        tr = _tile(rh, 256)
        nh = rh // tr
        g_spec = pl.BlockSpec((None, tr, n), lambda j, i, c: (j, c[0] * nh + i, 0))
        o_spec = pl.BlockSpec((None, tr, n), lambda j, i, c: (j, i, 0))
        grid = (nc, nh)

    def body(c_ref, g_ref, r_ref, o_ref):
        o_ref[...] = (g_ref[...].astype(F32) + r_ref[...].astype(F32)).astype(o_ref.dtype)

    return pl.pallas_call(
        body,
        grid_spec=pltpu.PrefetchScalarGridSpec(num_scalar_prefetch=1, grid=grid, in_specs=[g_spec, o_spec], out_specs=o_spec),
        out_shape=SDS(got.shape, got.dtype),
        compiler_params=_params(("parallel",) * len(grid)), name=name)(c_arr, g, got)


def _sum_partials(name, j_arr, part, got, kind):
    _, rh, n = got.shape
    tr = _tile(rh, 256)
    if kind == "col":
        p_spec = pl.BlockSpec((tr, n), lambda i, j: (i, j[0]))
    else:
        p_spec = pl.BlockSpec((None, tr, n), lambda i, j: (j[0], i, 0))

    def body(j_ref, p_ref, r_ref, o_ref):
        o_ref[...] = ((p_ref[...].astype(F32) + r_ref[0].astype(F32)) + r_ref[1].astype(F32)) + r_ref[2].astype(F32)

    return pl.pallas_call(
        body,
        grid_spec=pltpu.PrefetchScalarGridSpec(
            num_scalar_prefetch=1, grid=(rh // tr,),
            in_specs=[p_spec, pl.BlockSpec((3, tr, n), lambda i, j: (0, i, 0))],
            out_specs=pl.BlockSpec((tr, n), lambda i, j: (i, 0))),
        out_shape=SDS((rh, n), F32),
        compiler_params=_params(("parallel",)), name=name)(j_arr, part, got)


def _swap_reduced(name, halves):
    nw = len(halves)

    def body(*refs):
        h, got = refs[:nw], refs[nw:2 * nw]
        send_sems, recv_sems = refs[2 * nw:]
        x, y, c = _mesh_pos()
        cps = []
        for w in range(nw):
            cp = pltpu.make_async_remote_copy(
                src_ref=h[w], dst_ref=got[w], send_sem=send_sems.at[w], recv_sem=recv_sems.at[w],
                device_id=(x, y, 1 - c), device_id_type=MESH)
            cp.start()
            cps.append(cp)
        for cp in cps:
            cp.wait()

    return pl.pallas_call(
        body, out_shape=tuple(SDS(h.shape, h.dtype) for h in halves),
        in_specs=[_ANY] * nw, out_specs=tuple([_ANY] * nw),
        scratch_shapes=[pltpu.SemaphoreType.DMA((nw,)), pltpu.SemaphoreType.DMA((nw,))],
        name=name)(*halves)


def _pack(arrays):
    flat = [a.reshape(-1).astype(F32) for a in arrays]
    flat = [jnp.pad(f, (0, (-f.shape[0]) % LANES)) for f in flat]
    sizes = [f.shape[0] for f in flat]
    total = sum(sizes)
    rows = total // LANES
    tail = LANES * ((-rows) % (PACK_ROWS if rows > PACK_ROWS else SUBLANES))
    if tail:
        flat.append(jnp.zeros((tail,), F32))
    return jnp.concatenate(flat).reshape(-1, LANES), sizes


def _unpack(slab, sizes, shapes, lead=()):
    flat = slab.reshape(lead + (-1,))
    out, off = [], 0
    for sz, shp in zip(sizes, shapes):
        n = math.prod(shp)
        out.append(flat[..., off:off + n].reshape(lead + tuple(shp)))
        off += sz
    return out


def kernel(x, c, w_ada, b_ada, g_norm_mix, w_in, w_conv, b_conv, w_rg_a, b_rg_a, w_rg_x, b_rg_x, lru_lambda, g_attn_out, g_lru_out, w_out, g_norm_mlp, w_mlp_in, w_mlp_out, g_norm_final, loss_target, m_w_ada, m_b_ada, m_g_norm_mix, m_w_in, m_w_conv, m_b_conv, m_w_rg_a, m_b_rg_a, m_w_rg_x, m_b_rg_x, m_lru_lambda, m_g_attn_out, m_g_lru_out, m_w_out, m_g_norm_mlp, m_w_mlp_in, m_w_mlp_out, m_g_norm_final, v_w_ada, v_b_ada, v_g_norm_mix, v_w_in, v_w_conv, v_b_conv, v_w_rg_a, v_b_rg_a, v_w_rg_x, v_b_rg_x, v_lru_lambda, v_g_attn_out, v_g_lru_out, v_w_out, v_g_norm_mlp, v_w_mlp_in, v_w_mlp_out, v_g_norm_final):
    s, d = x.shape[1], x.shape[2]
    aw = d // 2
    nh = aw // HEAD
    f = w_mlp_out.shape[1] * N_CHIP
    n_ada = w_ada.shape[2]
    n_cv = w_conv.shape[2]
    ix, iy, ic = lax.axis_index("x"), lax.axis_index("y"), lax.axis_index("c")
    chip = 2 * ix + iy
    me = 2 * chip + ic
    c_arr = jnp.reshape(ic, (1,)).astype(jnp.int32)
    j_arr = jnp.reshape(chip, (1,)).astype(jnp.int32)

    x2d, tgt = x[0], loss_target[0]

    k_in, k_out, k_mi, k_mo = kinds = ("col", "row", "col", "row")
    slab, sizes = _pack([c, w_conv])
    p_in, _ = _place_cast("place_cast_0", j_arr, w_in[0], k_in, c)
    (f_in,), (dir_in,), slab = _gather_call("gather_0", [p_in], [k_in], [], [("direct", [0])], c, slab)
    p_out, tok = _place_cast("place_cast_1", j_arr, w_out[0], k_out, slab)
    p_mi, tok = _place_cast("place_cast_2", j_arr, w_mlp_in[0], k_mi, tok)
    p_mo, tok = _place_cast("place_cast_3", j_arr, w_mlp_out[0], k_mo, tok)

    gathered = _all_gather_small("comm_gather_cond", slab, after=(tok,))
    c_parts, cv_parts = _unpack(gathered, sizes, [(d,), (CONV_TAPS, n_cv)], lead=(N_DEV,))
    c_all = c_parts
    w_conv_full = jnp.concatenate([cv_parts[2 * j] for j in range(N_CHIP)], axis=-1)
    b_sh = lax.dynamic_slice(b_ada, (0, chip * n_ada), (1, n_ada))
    (f_in,), (rel_in, near_in), c_all = _gather_call(
        "gather_1", [f_in], [k_in], [("direct", [0], *dir_in)], [("relay", [0]), ("d2d_near", [0])], gathered, c_all)
    mod_part, act_all = _ada_mod(c_all, w_ada[0], b_sh)
    mod_g = _all_gather_small("comm_gather_mod", mod_part.reshape(-1, LANES))
    mod_g = mod_g.reshape(N_DEV, N_DEV, n_ada)
    mod = jnp.concatenate([lax.dynamic_index_in_dim(mod_g[2 * j], me, 0, keepdims=True) for j in range(N_CHIP)], axis=-1)
    sh1, sc1, gt1, sh2, sc2, gt2 = [mod[:, k * d:(k + 1) * d] for k in range(N_MOD)]

    (f_in, f_out, f_mi, f_mo), (far_in, dir_om, dir_mo), sh1 = _gather_call(
        "gather_2", [f_in, p_out, p_mi, p_mo], kinds, [("relay", [0], *rel_in)],
        [("d2d_far", [0]), ("direct", [1, 2]), ("direct", [3])], mod, sh1)

    h1, rstd1 = _norm_mod_fwd("norm_mod_fwd1", x2d, g_norm_mix, sc1, sh1)
    (w_in_f,), _, h1 = _gather_call("gather_3", [f_in], [k_in],
                                    [("d2d_near", [0], *near_in), ("d2d_far", [0], *far_in)], [], rstd1, h1)
    (qkv,) = _matmul("mm_qkv", h1, w_in_f, "nn", s, 3 * aw, d, (BF16,))
    (xrg,) = _matmul("mm_xrg", h1, w_in_f, "nn", s, 2 * aw, d, (F32,), b_off=3 * aw)
    o_attn, attn_w, attn_sg = _attn_fwd(qkv, nh)
    (f_out, f_mi), (rel_om, near_om), xrg = _gather_call(
        "gather_4", [f_out, f_mi], [k_out, k_mi], [("direct", [0, 1], *dir_om)],
        [("relay", [0, 1]), ("d2d_near", [0, 1])], o_attn, xrg)
    wa3, wx3 = w_rg_a[0], w_rg_x[0]
    o_lru, hseq = _lru_fwd(xrg, w_conv_full, b_conv, wa3, b_rg_a, wx3, b_rg_x, lru_lambda)
    mixed, rstd_a, rstd_l = _mix_norm_fwd(o_attn, o_lru, g_attn_out, g_lru_out)
    (f_out, f_mi, f_mo), (far_om, rel_mo, near_mo), mixed = _gather_call(
        "gather_5", [f_out, f_mi, f_mo], [k_out, k_mi, k_mo], [("relay", [0, 1], *rel_om), ("direct", [2], *dir_mo)],
        [("d2d_far", [0, 1]), ("relay", [2]), ("d2d_near", [2])], rstd_a, mixed)
    (w_out_f, w_mi_f), _, mixed = _gather_call(
        "gather_6", [f_out, f_mi], [k_out, k_mi], [("d2d_near", [0, 1], *near_om), ("d2d_far", [0, 1], *far_om)], [],
        rstd_l, mixed)

    def residual(acc, xin, gt):
        return acc, xin + gt * acc

    y1, x1 = _matmul("mm_out", mixed, w_out_f, "nn", s, d, d, (F32, F32), extras=(x2d, gt1),
                     extra_kinds=("tile", "row"), epilogue=residual)
    h2, rstd2 = _norm_mod_fwd("norm_mod_fwd2", x1, g_norm_mlp, sc2, sh2)
    (f_mo,), (far_mo,), h2 = _gather_call("gather_7", [f_mo], [k_mo], [("relay", [0], *rel_mo)],
                                          [("d2d_far", [0])], rstd2, h2)

    def sq_relu(acc):
        r = jnp.maximum(acc, 0.0)
        return 2.0 * r, r * r

    r2, hid = _matmul("mm_mlp_in", h2, w_mi_f, "nn", s, f, d, (BF16, BF16), epilogue=sq_relu)
    (w_mo_f,), _, hid = _gather_call("gather_8", [f_mo], [k_mo],
                                     [("d2d_near", [0], *near_mo), ("d2d_far", [0], *far_mo)], [], r2, hid)
    y2, x2 = _matmul("mm_mlp_out", hid, w_mo_f, "nn", s, d, f, (F32, F32), extras=(x1, gt2),
                     extra_kinds=("tile", "row"), epilogue=residual)
    dx2, loss_row, dg_final, dy2, dgt2 = _final_loss(x2, g_norm_final.reshape(1, d), tgt, y2, gt2)

    def rs_begin(tag, g, kind, thru):
        send, recv, g, land, thru = _split_start("rs_swap_start_" + tag, g, _half_shape(g, kind), 1,
                                                 _plan_swap_half(kind), thru)
        return {"tag": tag, "kind": kind, "swap": (send, recv, g, land)}, thru

    def rs_mid(st, after, thru):
        tag, kind = st["tag"], st["kind"]
        g, got = _split_wait("rs_swap_wait_" + tag, *st["swap"], _plan_swap_half(kind), after)
        part = _add_halves("add_halves_" + tag, c_arr, g, got, kind)
        blk = (part.shape[0], part.shape[1] // N_CHIP) if kind == "col" else part.shape[1:]
        send, recv, part, land, thru = _split_start("rs_scatter_start_" + tag, part, (N_CHIP - 1,) + blk, N_CHIP - 1,
                                                    _plan_scatter(kind), thru)
        st["scatter"] = (send, recv, part, land)
        return thru

    def rs_end(st, after):
        tag, kind = st["tag"], st["kind"]
        part, got = _split_wait("rs_scatter_wait_" + tag, *st["scatter"], _plan_scatter(kind), after)
        return _sum_partials("sum_partials_" + tag, j_arr, part, got, kind)

    (dpre,) = _matmul("mm_dhid", dy2, w_mo_f, "nt", s, f, d, (BF16,), extras=(r2,), extra_kinds=("tile",),
                      epilogue=lambda acc, r: (acc * r.astype(F32),))
    (g_mo,) = _matmul("mm_dw_mlp_out", hid, dy2, "tn", f, d, s, (BF16,))
    st_mo, dpre = rs_begin("mo", g_mo.reshape(N_CHIP, f // N_CHIP, d), "row", dpre)
    (dh2,) = _matmul("mm_dh2", dpre, w_mi_f, "nt", s, d, f, (F32,))
    (g_mi,) = _matmul("mm_dw_mlp_in", h2, dpre, "tn", d, f, s, (BF16,))
    dh2 = rs_mid(st_mo, g_mi, dh2)
    st_mi, dh2 = rs_begin("mi", g_mi, "col", dh2)
    dx1, dsh2, dsc2, dg_mlp, dy1, dgt1 = _norm_mod_bwd("norm_mod_bwd2", dh2, x1, rstd2, g_norm_mlp, sc2, dx2,
                                                       gate=(y1, gt1))
    (dmixed,) = _matmul("mm_dmixed", dy1, w_out_f, "nt", s, d, d, (F32,))
    (g_out,) = _matmul("mm_dw_out", mixed, dy1, "tn", d, d, s, (BF16,))
    dmixed = rs_mid(st_mi, g_out, dmixed)
    st_out, dmixed = rs_begin("out", g_out.reshape(N_CHIP, d // N_CHIP, d), "row", dmixed)
    do_attn, do_lru, dg_attn, dg_lru = _mix_norm_bwd(dmixed, o_attn, o_lru, rstd_a, rstd_l, g_attn_out, g_lru_out)
    dq, dk, dv = _attn_bwd(qkv, do_attn, attn_w, attn_sg, nh)
    do_lru = rs_mid(st_out, dq, do_lru)
    dxr, dxg, dwconv, dbconv, dwa, dba, dwx, dbx, dlam = _lru_bwd(
        xrg, do_lru, hseq, w_conv_full, b_conv, wa3, b_rg_a, wx3, b_rg_x, lru_lambda)
    dproj = jnp.concatenate([dq, dk, dv, dxr, dxg], axis=-1)
    (dh1,) = _matmul("mm_dh1", dproj, w_in_f, "nt", s, d, 5 * aw, (F32,))
    (g_in,) = _matmul("mm_dw_in", h1, dproj, "tn", d, 5 * aw, s, (BF16,))
    st_in, dh1 = rs_begin("in", g_in, "col", dh1)
    grad_x, dsh1, dsc1, dg_mix = _norm_mod_bwd("norm_mod_bwd1", dh1, x2d, rstd1, g_norm_mix, sc1, dx1)

    dmod = jnp.concatenate([dsh1, dsc1, dgt1, dsh2, dsc2, dgt2], axis=-1)
    small_names = ["b_ada", "g_norm_mix", "b_conv", "w_rg_a", "b_rg_a", "w_rg_x", "b_rg_x", "lru_lambda",
                   "g_attn_out", "g_lru_out", "g_norm_mlp", "g_norm_final"]
    small_g = [dmod, dg_mix, dbconv, dwa, dba, dwx, dbx, dlam, dg_attn, dg_lru, dg_mlp, dg_final]
    small_w = [b_ada, g_norm_mix, b_conv, w_rg_a, b_rg_a, w_rg_x, b_rg_x, lru_lambda, g_attn_out, g_lru_out, g_norm_mlp, g_norm_final]
    small_m = [m_b_ada, m_g_norm_mix, m_b_conv, m_w_rg_a, m_b_rg_a, m_w_rg_x, m_b_rg_x, m_lru_lambda, m_g_attn_out, m_g_lru_out, m_g_norm_mlp, m_g_norm_final]
    small_v = [v_b_ada, v_g_norm_mix, v_b_conv, v_w_rg_a, v_b_rg_a, v_w_rg_x, v_b_rg_x, v_lru_lambda, v_g_attn_out, v_g_lru_out, v_g_norm_mlp, v_g_norm_final]
    extra_zero = [jnp.zeros_like(dwconv), jnp.zeros((LANES,), F32)]
    g_slab, g_sizes = _pack(small_g + [dwconv, loss_row])
    w_slab, _ = _pack(small_w + extra_zero)
    m_slab, _ = _pack(small_m + extra_zero)
    v_slab, _ = _pack(small_v + extra_zero)
    me_arr = jnp.reshape(me, (1,)).astype(jnp.int32)
    g_buf = _place_row("place_small_grads", me_arr, g_slab)
    sg_send, sg_recv, g_buf, tok = _split_start_inplace("sg_gather_start", g_buf, N_CHIP, _plan_gather_own, loss_row)
    tok = rs_mid(st_in, tok, tok)

    def reduced_begin(tag, half, tok_):
        send, recv, half, land, tok_ = _split_start("rs_reduced_start_" + tag, half, half.shape, 1, _plan_whole, tok_)
        return (send, recv, half, land), tok_

    def reduced_end(tag, st, after):
        return _split_wait("rs_reduced_wait_" + tag, *st, _plan_whole, after)

    sw_mo, tok = reduced_begin("mo", rs_end(st_mo, tok), tok)
    sw_mi, tok = reduced_begin("mi", rs_end(st_mi, tok), tok)
    sw_out, tok = reduced_begin("out", rs_end(st_out, tok), tok)
    g_buf = _split_wait_inplace("sg_gather_wait", sg_send, sg_recv, g_buf, _plan_gather_own, tok)
    sg_send, sg_recv, g_buf, tok = _split_start_inplace("sg_pass_start", g_buf, N_CHIP - 1, _plan_gather_pass, tok)
    half_mo, got_mo = reduced_end("mo", sw_mo, tok)
    big = {"w_mlp_out": _adamw_halves("adamw_w_mlp_out", c_arr, w_mlp_out[0], m_w_mlp_out[0], v_w_mlp_out[0],
                                      half_mo, got_mo)}
    g_all = _split_wait_inplace("sg_pass_wait", sg_send, sg_recv, g_buf, _plan_gather_pass, big["w_mlp_out"][1])
    gs_slab, ds_slab, ms_slab, vs_slab = _small_reduce_adamw(g_all, w_slab, m_slab, v_slab)
    shapes = [w.shape for w in small_w] + [dwconv.shape, (LANES,)]
    gs = _unpack(gs_slab, g_sizes, shapes)
    ds = _unpack(ds_slab, g_sizes, shapes)
    ms = _unpack(ms_slab, g_sizes, shapes)
    vs = _unpack(vs_slab, g_sizes, shapes)
    small = {n: (gs[i], ds[i], ms[i], vs[i]) for i, n in enumerate(small_names)}
    loss = gs[-1][0]
    g_wconv = lax.dynamic_slice(gs[-2], (0, chip * n_cv), (CONV_TAPS, n_cv))
    d_wconv, m_wconv, v_wconv = _adamw_plain("adamw_conv", w_conv[0], g_wconv, m_w_conv[0], v_w_conv[0])
    small["w_conv"] = (g_wconv[None], d_wconv[None], m_wconv[None], v_wconv[None])

    dmod_all = g_all[:, :N_MOD * d // LANES, :].reshape(N_DEV, N_MOD * d)
    dmod_sel = lax.dynamic_slice(dmod_all, (0, chip * n_ada), (N_DEV, n_ada)).astype(BF16)
    act_t = act_all.T.astype(BF16)
    big["w_ada"] = _adamw_ada(w_ada[0], m_w_ada[0], v_w_ada[0], act_t, dmod_sel)

    half_mi, got_mi = reduced_end("mi", sw_mi, big["w_ada"][1])
    big["w_mlp_in"] = _adamw_halves("adamw_w_mlp_in", c_arr, w_mlp_in[0], m_w_mlp_in[0], v_w_mlp_in[0], half_mi, got_mi)
    half_out, got_out = reduced_end("out", sw_out, big["w_mlp_in"][1])
    big["w_out"] = _adamw_halves("adamw_w_out", c_arr, w_out[0], m_w_out[0], v_w_out[0], half_out, got_out)
    half_in = rs_end(st_in, big["w_out"][1])
    (got_in,) = _swap_reduced("comm_swap_reduced_in", [half_in])
    big["w_in"] = _adamw_halves("adamw_w_in", c_arr, w_in[0], m_w_in[0], v_w_in[0], half_in, got_in)

    order = ["w_ada", "b_ada", "g_norm_mix", "w_in", "w_conv", "b_conv", "w_rg_a", "b_rg_a", "w_rg_x", "b_rg_x",
             "lru_lambda", "g_attn_out", "g_lru_out", "w_out", "g_norm_mlp", "w_mlp_in", "w_mlp_out", "g_norm_final"]
    res = {}
    for n in order:
        res[n] = tuple(t[None] for t in big[n]) if n in big else small[n]
    return (loss, grad_x[None],
            *[res[n][0] for n in order], *[res[n][1] for n in order],
            *[res[n][2] for n in order], *[res[n][3] for n in order])
```

```python
import functools
import math

import jax
import jax.numpy as jnp
from jax import lax
from jax.experimental import pallas as pl
from jax.experimental.pallas import tpu as pltpu

F32 = jnp.float32
BF16 = jnp.bfloat16
SDS = jax.ShapeDtypeStruct
MESH = pl.DeviceIdType.MESH

EPS = 1e-6
HEAD = 128
N_MOD = 6
CONV_TAPS = 4
LRU_C = 8.0
ADAM_LR, ADAM_B1, ADAM_B2, ADAM_EPS, ADAM_WD, ADAM_STEP = 0.001, 0.9, 0.999, 1e-08, 0.01, 10
N_DEV = 8
N_CHIP = 4
LANES = 128
SUBLANES = 8
VMEM_LIMIT = 56 * 1024 * 1024
PACK_ROWS = 256
MM_TILE_M, MM_TILE_N, MM_TILE_K = 1024, 1024, 2048
ROW_TILE = 256
ROW_SPLIT = 1


def _tile(dim, pref):
    t = min(dim, pref)
    while dim % t:
        t -= LANES
    return t


def _params(sem=None):
    return pltpu.CompilerParams(dimension_semantics=sem, vmem_limit_bytes=VMEM_LIMIT)


def _sigmoid(x):
    return 1.0 / (1.0 + jnp.exp(-x))


def _log_sigmoid(x):
    return jnp.minimum(x, 0.0) - jnp.log(1.0 + jnp.exp(-jnp.abs(x)))


def _gelu_parts(x):
    k0, k1 = math.sqrt(2.0 / math.pi), 0.044715
    t = jnp.tanh(k0 * (x + k1 * x * x * x))
    val = 0.5 * x * (1.0 + t)
    der = 0.5 * (1.0 + t) + 0.5 * x * (1.0 - t * t) * k0 * (1.0 + 3.0 * k1 * x * x)
    return val, der


def _dot(a, b):
    return jnp.dot(a, b, preferred_element_type=F32)


def _dot_nt(a, b):
    return lax.dot_general(a, b, (((1,), (1,)), ((), ())), preferred_element_type=F32)


def _dot_tn(a, b):
    return lax.dot_general(a, b, (((0,), (0,)), ((), ())), preferred_element_type=F32)


def _split_dot(x, tri):
    hi = x.astype(BF16)
    lo = (x - hi.astype(F32)).astype(BF16)
    return _dot(hi, tri) + _dot(lo, tri)


def _matmul(name, a, b, mode, m, n, k, out_dtypes, *, b_off=0, extras=(), extra_kinds=(), epilogue=None,
            tm=MM_TILE_M, tn=MM_TILE_N, tk=MM_TILE_K):
    tm, tn, tk = _tile(m, tm), _tile(math.gcd(n, b_off) if b_off else n, tn), _tile(k, tk)
    assert b_off % tn == 0
    nk = k // tk
    n_ex, n_out = len(extras), len(out_dtypes)
    dot = {"nn": _dot, "nt": _dot_nt, "tn": _dot_tn}[mode]

    def body(a_ref, b_ref, *rest):
        ex, outs = rest[:n_ex], rest[n_ex:n_ex + n_out]

        def finish(total):
            res = epilogue(total, *[e[...] for e in ex]) if epilogue else (total,)
            for o, r in zip(outs, res):
                o[...] = r.astype(o.dtype)

        if nk == 1:
            finish(dot(a_ref[...], b_ref[...]))
            return
        acc = rest[-1]
        kk = pl.program_id(2)

        @pl.when(kk == 0)
        def _():
            acc[...] = dot(a_ref[...], b_ref[...])

        @pl.when(jnp.logical_and(kk > 0, kk < nk - 1))
        def _():
            acc[...] += dot(a_ref[...], b_ref[...])

        @pl.when(kk == nk - 1)
        def _():
            finish(acc[...] + dot(a_ref[...], b_ref[...]))

    if mode == "nn":
        a_spec = pl.BlockSpec((tm, tk), lambda i, j, kk: (i, kk))
        b_spec = pl.BlockSpec((tk, tn), lambda i, j, kk: (kk, j + b_off // tn))
    elif mode == "nt":
        a_spec = pl.BlockSpec((tm, tk), lambda i, j, kk: (i, kk))
        b_spec = pl.BlockSpec((tn, tk), lambda i, j, kk: (j, kk + b_off // tk))
    else:
        a_spec = pl.BlockSpec((tk, tm), lambda i, j, kk: (kk, i))
        b_spec = pl.BlockSpec((tk, tn), lambda i, j, kk: (kk, j))
    tile_spec = pl.BlockSpec((tm, tn), lambda i, j, kk: (i, j))
    row_spec = pl.BlockSpec((1, tn), lambda i, j, kk: (0, j))
    ex_specs = [tile_spec if kind == "tile" else row_spec for kind in extra_kinds]
    outs = pl.pallas_call(
        body, grid=(m // tm, n // tn, nk),
        in_specs=[a_spec, b_spec] + ex_specs,
        out_specs=tuple(tile_spec for _ in out_dtypes),
        out_shape=tuple(SDS((m, n), dt) for dt in out_dtypes),
        scratch_shapes=[pltpu.VMEM((tm, tn), F32)] if nk > 1 else [],
        compiler_params=_params(("parallel", "parallel", "arbitrary")),
        name=name,
    )(a, b, *extras)
    return outs


def _row_specs(s, d, tr):
    row = "row"
    vec = pl.BlockSpec((1, d), lambda i: (0, 0))
    col = pl.BlockSpec((tr, 1), lambda i: (i, 0))
    return row, vec, col


class _ColChunks:
    def __init__(self, refs):
        self.refs = refs

    def __getitem__(self, idx):
        return jnp.concatenate([r[...] for r in self.refs], axis=-1)


def _rows_call(name, body, grid, in_specs, out_specs, out_shape, semantics, args):
    in_x, args_x, groups = [], [], []
    for spec, arr in zip(in_specs, args):
        if isinstance(spec, str):
            rows, d = arr.shape
            tr, dc = rows // grid[0], d // ROW_SPLIT
            in_x += [pl.BlockSpec((tr, dc), functools.partial(lambda i, jj: (i, jj), jj=j)) for j in range(ROW_SPLIT)]
            args_x += [arr] * ROW_SPLIT
            groups.append(ROW_SPLIT)
        else:
            in_x.append(spec)
            args_x.append(arr)
            groups.append(1)
    out_x = [pl.BlockSpec((sh.shape[0] // grid[0], sh.shape[1]), lambda i: (i, 0)) if isinstance(spec, str) else spec
             for spec, sh in zip(out_specs, out_shape)]

    def wrapped(*refs):
        views, k = [], 0
        for g in groups:
            views.append(_ColChunks(refs[k:k + g]) if g > 1 else refs[k])
            k += g
        body(*views, *refs[k:])

    return pl.pallas_call(
        wrapped, grid=grid, in_specs=in_x, out_specs=tuple(out_x), out_shape=tuple(out_shape),
        compiler_params=_params(semantics), name=name)(*args_x)


def _norm_mod_fwd(name, x, g, sc, sh):
    s, d = x.shape
    tr = _tile(s, ROW_TILE)
    row, vec, col = _row_specs(s, d, tr)

    def body(x_ref, g_ref, sc_ref, sh_ref, h_ref, r_ref):
        xv = x_ref[...]
        r = lax.rsqrt(jnp.mean(xv * xv, axis=-1, keepdims=True) + EPS)
        h_ref[...] = ((xv * r * g_ref[...]) * (1.0 + sc_ref[...]) + sh_ref[...]).astype(BF16)
        r_ref[...] = r

    return _rows_call(name, body, (s // tr,), [row, vec, vec, vec], (row, col),
                      (SDS((s, d), BF16), SDS((s, 1), F32)), ("parallel",), (x, g, sc, sh))


def _norm_mod_bwd(name, dh, xin, rstd, g, sc, dres, gate=None):
    s, d = xin.shape
    tr = _tile(s, ROW_TILE)
    row, vec, col = _row_specs(s, d, tr)

    n_gate = 2 if gate is not None else 0

    def body(dh_ref, x_ref, r_ref, g_ref, sc_ref, dres_ref, *rest):
        gate_in, gate_out = rest[:n_gate], rest[n_gate + 4:]
        dx_ref, dsh_ref, dsc_ref, dg_ref = rest[n_gate:n_gate + 4]

        @pl.when(pl.program_id(0) == 0)
        def _():
            for ref in (dsh_ref, dsc_ref, dg_ref) + tuple(gate_out[1:]):
                ref[...] = jnp.zeros_like(ref)

        dh_v, xv, r, gv = dh_ref[...].astype(F32), x_ref[...], r_ref[...], g_ref[...]
        n0 = xv * r
        dsh_ref[...] += jnp.sum(dh_v, axis=0, keepdims=True)
        dsc_ref[...] += jnp.sum(dh_v * (n0 * gv), axis=0, keepdims=True)
        dn = dh_v * (1.0 + sc_ref[...])
        dg_ref[...] += jnp.sum(dn * n0, axis=0, keepdims=True)
        gy = dn * gv
        dot = jnp.mean(gy * xv, axis=-1, keepdims=True)
        dxv = dres_ref[...] + r * gy - xv * (r * r * r * dot)
        dx_ref[...] = dxv
        if gate is not None:
            y_ref, gt_ref = gate_in
            dy_ref, dgt_ref = gate_out
            dy_ref[...] = (gt_ref[...] * dxv).astype(BF16)
            dgt_ref[...] += jnp.sum(dxv * y_ref[...], axis=0, keepdims=True)

    vecs = SDS((1, d), F32)
    gate_args = tuple(gate) if gate is not None else ()
    return _rows_call(
        name, body, (s // tr,),
        [row, row, col, vec, vec, row] + ([row, vec] if gate is not None else []),
        (row, vec, vec, vec) + ((row, vec) if gate is not None else ()),
        (SDS((s, d), F32), vecs, vecs, vecs) + ((SDS((s, d), BF16), vecs) if gate is not None else ()),
        ("arbitrary",), (dh, xin, rstd, g, sc, dres, *gate_args))


def _final_loss(x2, gf, tgt, y, gt):
    s, d = x2.shape
    tr = _tile(s, ROW_TILE)
    row, vec, _ = _row_specs(s, d, tr)
    lrow = pl.BlockSpec((1, LANES), lambda i: (0, 0))

    def body(x_ref, g_ref, t_ref, y_ref, gt_ref, dx_ref, loss_ref, dg_ref, dy_ref, dgt_ref):
        @pl.when(pl.program_id(0) == 0)
        def _():
            loss_ref[...] = jnp.zeros_like(loss_ref)
            dg_ref[...] = jnp.zeros_like(dg_ref)
            dgt_ref[...] = jnp.zeros_like(dgt_ref)

        xv, gv = x_ref[...], g_ref[...]
        r = lax.rsqrt(jnp.mean(xv * xv, axis=-1, keepdims=True) + EPS)
        n0 = xv * r
        err = n0 * gv - t_ref[...]
        loss_ref[...] += jnp.sum(err * err) * (0.5 / d)
        dy = err * (1.0 / d)
        dg_ref[...] += jnp.sum(dy * n0, axis=0, keepdims=True)
        gy = dy * gv
        dot = jnp.mean(gy * xv, axis=-1, keepdims=True)
        dxv = r * gy - xv * (r * r * r * dot)
        dx_ref[...] = dxv
        dy_ref[...] = (gt_ref[...] * dxv).astype(BF16)
        dgt_ref[...] += jnp.sum(dxv * y_ref[...], axis=0, keepdims=True)

    return _rows_call(
        "final_loss", body, (s // tr,), [row, vec, row, row, vec], (row, lrow, vec, row, vec),
        (SDS((s, d), F32), SDS((1, LANES), F32), SDS((1, d), F32), SDS((s, d), BF16), SDS((1, d), F32)),
        ("arbitrary",), (x2, gf, tgt, y, gt))


def _mix_norm_fwd(oa, ol, ga, gl):
    s, w = oa.shape
    tr = _tile(s, ROW_TILE)
    row, vec, col = _row_specs(s, w, tr)

    def body(oa_ref, ol_ref, ga_ref, gl_ref, mx_ref, ra_ref, rl_ref):
        a, l = oa_ref[...], ol_ref[...]
        ra = lax.rsqrt(jnp.mean(a * a, axis=-1, keepdims=True) + EPS)
        rl = lax.rsqrt(jnp.mean(l * l, axis=-1, keepdims=True) + EPS)
        mx_ref[:, :w] = (a * ra * ga_ref[...]).astype(BF16)
        mx_ref[:, w:] = (l * rl * gl_ref[...]).astype(BF16)
        ra_ref[...] = ra
        rl_ref[...] = rl

    return _rows_call(
        "mix_norm_fwd", body, (s // tr,), [row, row, vec, vec], (row, col, col),
        (SDS((s, 2 * w), BF16), SDS((s, 1), F32), SDS((s, 1), F32)), ("parallel",), (oa, ol, ga, gl))


def _mix_norm_bwd(dmx, oa, ol, ra, rl, ga, gl):
    s, w = oa.shape
    tr = _tile(s, ROW_TILE)
    row, vec, col = _row_specs(s, w, tr)

    def body(dm_ref, oa_ref, ol_ref, ra_ref, rl_ref, ga_ref, gl_ref, doa_ref, dol_ref, dga_ref, dgl_ref):
        @pl.when(pl.program_id(0) == 0)
        def _():
            dga_ref[...] = jnp.zeros_like(dga_ref)
            dgl_ref[...] = jnp.zeros_like(dgl_ref)

        def one(dy, xv, r, gv, dg_ref):
            dg_ref[...] += jnp.sum(dy * (xv * r), axis=0, keepdims=True)
            gy = dy * gv
            dot = jnp.mean(gy * xv, axis=-1, keepdims=True)
            return r * gy - xv * (r * r * r * dot)

        dm = dm_ref[...].astype(F32)
        doa_ref[...] = one(dm[:, :w], oa_ref[...], ra_ref[...], ga_ref[...], dga_ref).astype(BF16)
        dol_ref[...] = one(dm[:, w:], ol_ref[...], rl_ref[...], gl_ref[...], dgl_ref)

    return _rows_call(
        "mix_norm_bwd", body, (s // tr,), [row, row, row, col, col, vec, vec], (row, row, vec, vec),
        (SDS((s, w), BF16), SDS((s, w), F32), SDS((1, w), F32), SDS((1, w), F32)), ("arbitrary",),
        (dmx, oa, ol, ra, rl, ga, gl))


def _attn_blocks(qs, ks, tri_after, csums, causal):
    zs = [_dot_nt(q, k) * (HEAD ** -0.5) for q, k in zip(qs, ks)]
    lbs = [_log_sigmoid(z) for z in zs]
    lss = [lb - z for lb, z in zip(lbs, zs)]
    if causal is not None:
        lss = [jnp.where(causal, ls, 0.0) for ls in lss]
    locs = [_split_dot(ls, tri_after) for ls in lss]
    ws = [jnp.exp(lb + (loc + cs)) for lb, loc, cs in zip(lbs, locs, csums)]
    if causal is not None:
        ws = [jnp.where(causal, w, 0.0) for w in ws]
    nxt = [cs + (loc[:, 0:1] + ls[:, 0:1]) for cs, loc, ls in zip(csums, locs, lss)]
    return lbs, ws, nxt


ATTN_HEADS_PER_STEP = 4


def _attn_tile(s):
    return 256 if s >= 1024 else 128


def _tri(t, after):
    r_i = lax.broadcasted_iota(jnp.int32, (t, t), 0)
    c_i = lax.broadcasted_iota(jnp.int32, (t, t), 1)
    return ((r_i > c_i) if after else (r_i < c_i)).astype(BF16)


def _attn_fwd(qkv, n_heads):
    s = qkv.shape[0]
    t = _attn_tile(s)
    hps = ATTN_HEADS_PER_STEP
    wid = hps * HEAD

    nq = s // t

    def body(q_ref, k_ref, v_ref, o_ref, w_ref, sg_ref):
        qi = pl.program_id(1)
        tri_after = _tri(t, True)
        causal = lax.broadcasted_iota(jnp.int32, (t, t), 1) < lax.broadcasted_iota(jnp.int32, (t, t), 0)
        lanes = [slice(a * HEAD, (a + 1) * HEAD) for a in range(hps)]
        qs = [q_ref[:, ln] for ln in lanes]

        def block(kb, carry, mask):
            off = pl.multiple_of(kb * t, t)
            ks = [k_ref[pl.ds(off, t), ln] for ln in lanes]
            lbs, ws, csums = _attn_blocks(qs, ks, tri_after, [cr[0] for cr in carry], mask)
            wbs = [w.astype(BF16) for w in ws]
            for a in range(hps):
                w_ref[a, kb] = wbs[a]
                sg_ref[a, kb] = jnp.exp(lbs[a]).astype(BF16)
            os_ = [cr[1] + _dot(wb, v_ref[pl.ds(off, t), ln]) for cr, wb, ln in zip(carry, wbs, lanes)]
            return tuple(zip(csums, os_))

        zero = tuple((jnp.zeros((t, 1), F32), jnp.zeros((t, HEAD), F32)) for _ in lanes)
        carry = block(qi, zero, causal)
        carry = lax.fori_loop(1, qi + 1, lambda it, cr: block(qi - it, cr, None), carry)
        for a, ln in enumerate(lanes):
            o_ref[:, ln] = carry[a][1]

    hb = n_heads // hps
    kept = pl.BlockSpec((None, hps, nq, t, t), lambda hh, i: (hh * nq + i, 0, 0, 0, 0))
    kept_shape = SDS((hb * nq, hps, nq, t, t), BF16)
    return pl.pallas_call(
        body, grid=(hb, nq),
        in_specs=[pl.BlockSpec((t, wid), lambda hh, i: (i, hh)),
                  pl.BlockSpec((s, wid), lambda hh, i: (0, hb + hh)),
                  pl.BlockSpec((s, wid), lambda hh, i: (0, 2 * hb + hh))],
        out_specs=(pl.BlockSpec((t, wid), lambda hh, i: (i, hh)), kept, kept),
        out_shape=(SDS((s, n_heads * HEAD), F32), kept_shape, kept_shape),
        compiler_params=_params(("parallel", "parallel")), name="attn_fwd")(qkv, qkv, qkv)


def _attn_bwd(qkv, do, w_kept, sg_kept, n_heads):
    s = qkv.shape[0]
    t = _attn_tile(s)
    nq = s // t
    scale = HEAD ** -0.5
    hps = ATTN_HEADS_PER_STEP
    wid = hps * HEAD

    def body(q_ref, k_ref, v_ref, do_ref, w_ref, sg_ref, dq_ref, dk_ref, dv_ref, dk_acc, dv_acc):
        qi = pl.program_id(1)

        @pl.when(qi == 0)
        def _():
            dk_acc[...] = jnp.zeros_like(dk_acc)
            dv_acc[...] = jnp.zeros_like(dv_acc)

        tri_before = _tri(t, False)
        causal = lax.broadcasted_iota(jnp.int32, (t, t), 1) < lax.broadcasted_iota(jnp.int32, (t, t), 0)
        lanes = [slice(a * HEAD, (a + 1) * HEAD) for a in range(hps)]
        qs = [q_ref[:, ln] for ln in lanes]
        douts = [do_ref[:, ln] for ln in lanes]

        def block(kb, carry, mask):
            off = pl.multiple_of(kb * t, t)
            wbs = [w_ref[a, kb] for a in range(hps)]
            dws = [_dot_nt(dout, v_ref[pl.ds(off, t), ln]) for dout, ln in zip(douts, lanes)]
            for a, ln in enumerate(lanes):
                dv_acc[pl.ds(off, t), ln] += _dot_tn(wbs[a], douts[a])
            es = [dw * wb.astype(F32) for dw, wb in zip(dws, wbs)]
            locs = [_split_dot(e, tri_before) for e in es]
            sgs = [sg_ref[a, kb].astype(F32) for a in range(hps)]
            stays = [(loc + cr[0]) * sg for loc, cr, sg in zip(locs, carry, sgs)]
            if mask is not None:
                stays = [jnp.where(mask, st, 0.0) for st in stays]
            dzbs = [((e * (1.0 - sg) - st) * scale).astype(BF16) for e, sg, st in zip(es, sgs, stays)]
            dqs = [cr[1] + _dot(dzb, k_ref[pl.ds(off, t), ln]) for cr, dzb, ln in zip(carry, dzbs, lanes)]
            for a, ln in enumerate(lanes):
                dk_acc[pl.ds(off, t), ln] += _dot_tn(dzbs[a], qs[a])
            esums = [cr[0] + (loc[:, t - 1:t] + e[:, t - 1:t]) for cr, loc, e in zip(carry, locs, es)]
            return tuple(zip(esums, dqs))

        zero = tuple((jnp.zeros((t, 1), F32), jnp.zeros((t, HEAD), F32)) for _ in lanes)
        carry = lax.fori_loop(0, qi, lambda kb, cr: block(kb, cr, None), zero)
        carry = block(qi, carry, causal)
        for a, ln in enumerate(lanes):
            dq_ref[:, ln] = carry[a][1].astype(BF16)

        @pl.when(qi == nq - 1)
        def _():
            dk_ref[...] = dk_acc[...].astype(BF16)
            dv_ref[...] = dv_acc[...].astype(BF16)

    hb = n_heads // hps
    blk = pl.BlockSpec((t, wid), lambda hh, i: (i, hh))
    full = pl.BlockSpec((s, wid), lambda hh, i: (0, hh))
    kept = pl.BlockSpec((None, hps, nq, t, t), lambda hh, i: (hh * nq + i, 0, 0, 0, 0))
    return pl.pallas_call(
        body, grid=(hb, nq),
        in_specs=[blk,
                  pl.BlockSpec((s, wid), lambda hh, i: (0, hb + hh)),
                  pl.BlockSpec((s, wid), lambda hh, i: (0, 2 * hb + hh)),
                  blk, kept, kept],
        out_specs=(blk, full, full),
        out_shape=(SDS((s, n_heads * HEAD), BF16),) * 3,
        scratch_shapes=[pltpu.VMEM((s, wid), F32), pltpu.VMEM((s, wid), F32)],
        compiler_params=_params(("parallel", "arbitrary")), name="attn_bwd")(qkv, qkv, qkv, do, w_kept, sg_kept)


def _lru_chunk(s):
    return 256 if s >= 1024 else 128


def _lru_gates(xc, wa, ba, wx, bx, sp):
    xb = xc.astype(BF16)
    r = _sigmoid(_dot(xb, wa) + ba)
    ig = _sigmoid(_dot(xb, wx) + bx)
    la = -LRU_C * r * sp
    a = jnp.exp(la)
    t = jnp.tanh(la)
    mult = jnp.sqrt(-2.0 * t / (1.0 - t))
    return r, ig, a, mult


def _softplus_neg(lam):
    return jnp.maximum(-lam, 0.0) + jnp.log(1.0 + jnp.exp(-jnp.abs(lam)))


LRU_BLOCKS_PER_STEP = 1


def _lru_specs(s, n_blocks):
    bps = min(LRU_BLOCKS_PER_STEP, n_blocks)
    wid = bps * HEAD
    seq0 = pl.BlockSpec((s, wid), lambda h: (0, h))
    seq1 = pl.BlockSpec((s, wid), lambda h: (0, n_blocks // bps + h))
    taps = pl.BlockSpec((CONV_TAPS, wid), lambda h: (0, h))
    vec = pl.BlockSpec((1, wid), lambda h: (0, h))
    mat = pl.BlockSpec((bps, HEAD, HEAD), lambda h: (h, 0, 0))
    return bps, seq0, seq1, taps, vec, mat


def _per_block(one_block, n_2d, n_mat_pos, bps):
    def body(*refs):
        for a in range(bps):
            views = [r.at[a] if i in n_mat_pos else r.at[:, pl.ds(a * HEAD, HEAD)] for i, r in enumerate(refs[:n_2d])]
            one_block(*views, *refs[n_2d:])
    return body


def _lru_fwd(xrg, wconv, bconv, wa, ba, wx, bx, lam):
    s = xrg.shape[0]
    nb = wa.shape[0]
    tc = _lru_chunk(s)
    bps, seq0, seq1, taps, vec, mat = _lru_specs(s, nb)
    pad = SUBLANES

    def one_block(xr_ref, xg_ref, wc_ref, bc_ref, wa_ref, ba_ref, wx_ref, bx_ref, lam_ref, o_ref, h_ref, pad_s, a_s, u_s):
        pad_s[0:pad, :] = jnp.zeros((pad, HEAD), F32)
        pad_s[pad:pad + s, :] = xr_ref[...]
        wab, wxb = wa_ref[...].astype(BF16), wx_ref[...].astype(BF16)
        sp = _softplus_neg(lam_ref[...])
        for c in range(s // tc):
            base = c * tc
            xc = bc_ref[...] + sum(wc_ref[i:i + 1, :] * pad_s[pl.ds(base + pad - (CONV_TAPS - 1) + i, tc), :]
                                   for i in range(CONV_TAPS))
            _, ig, a, mult = _lru_gates(xc, wab, ba_ref[...], wxb, bx_ref[...], sp)
            a_s[base:base + tc, :] = a
            u_s[base:base + tc, :] = mult * (ig * xc)

        row = lax.broadcasted_iota(jnp.int32, (SUBLANES, HEAD), 0)

        def chunk(ci, hprev):
            off = pl.multiple_of(ci * SUBLANES, SUBLANES)
            a8, b8 = a_s[pl.ds(off, SUBLANES), :], u_s[pl.ds(off, SUBLANES), :]
            for d in (1, 2, 4):
                a_sh = jnp.where(row < d, 1.0, pltpu.roll(a8, d, 0))
                b_sh = jnp.where(row < d, 0.0, pltpu.roll(b8, d, 0))
                b8 = a8 * b_sh + b8
                a8 = a8 * a_sh
            h8 = a8 * hprev + b8
            h_ref[pl.ds(off, SUBLANES), :] = h8
            return h8[SUBLANES - 1:SUBLANES, :]

        lax.fori_loop(0, s // SUBLANES, chunk, jnp.zeros((1, HEAD), F32), unroll=8)
        for c in range(s // tc):
            sl = slice(c * tc, (c + 1) * tc)
            gel, _ = _gelu_parts(xg_ref[sl, :])
            o_ref[sl, :] = h_ref[sl, :] * gel

    return pl.pallas_call(
        _per_block(one_block, 11, (4, 6), bps), grid=(nb // bps,),
        in_specs=[seq0, seq1, taps, vec, mat, vec, mat, vec, vec],
        out_specs=(seq0, seq0),
        out_shape=(SDS((s, nb * HEAD), F32), SDS((s, nb * HEAD), F32)),
        scratch_shapes=[pltpu.VMEM((s + pad, HEAD), F32), pltpu.VMEM((s, HEAD), F32), pltpu.VMEM((s, HEAD), F32)],
        compiler_params=_params(("parallel",)), name="lru_fwd")(xrg, xrg, wconv, bconv, wa, ba, wx, bx, lam)


def _lru_bwd(xrg, dol, hseq, wconv, bconv, wa, ba, wx, bx, lam):
    s = xrg.shape[0]
    nb = wa.shape[0]
    tc = _lru_chunk(s)
    bps, seq0, seq1, taps, vec, mat = _lru_specs(s, nb)
    pad = SUBLANES

    def one_block(xr_ref, xg_ref, do_ref, h_ref, wc_ref, bc_ref, wa_ref, ba_ref, wx_ref, bx_ref, lam_ref,
             dxr_ref, dxg_ref, dwc_ref, dbc_ref, dwa_ref, dba_ref, dwx_ref, dbx_ref, dlam_ref,
             pad_s, hp_s, a_s, g_s, da_s, dxc_s):
        pad_s[0:pad, :] = jnp.zeros((pad, HEAD), F32)
        pad_s[pad:pad + s, :] = xr_ref[...]
        hp_s[0:pad, :] = jnp.zeros((pad, HEAD), F32)
        hp_s[pad:pad + s, :] = h_ref[...]
        a_s[s:s + pad, :] = jnp.zeros((pad, HEAD), F32)
        dxc_s[s:s + pad, :] = jnp.zeros((pad, HEAD), F32)
        wab, wxb = wa_ref[...].astype(BF16), wx_ref[...].astype(BF16)
        lam_v = lam_ref[...]
        sp = _softplus_neg(lam_v)

        def conv_in(c):
            base = c * tc
            wins = [pad_s[pl.ds(base + pad - (CONV_TAPS - 1) + i, tc), :] for i in range(CONV_TAPS)]
            xc = bc_ref[...] + sum(wc_ref[i:i + 1, :] * wins[i] for i in range(CONV_TAPS))
            return xc, wins

        for c in range(s // tc):
            sl = slice(c * tc, (c + 1) * tc)
            xc, _ = conv_in(c)
            _, _, a, _ = _lru_gates(xc, wab, ba_ref[...], wxb, bx_ref[...], sp)
            a_s[sl, :] = a
            gel, dgel = _gelu_parts(xg_ref[sl, :])
            dov = do_ref[sl, :]
            g_s[sl, :] = dov * gel
            dxg_ref[sl, :] = (dov * h_ref[sl, :] * dgel).astype(BF16)

        row = lax.broadcasted_iota(jnp.int32, (SUBLANES, HEAD), 0)
        n_chunks = s // SUBLANES

        def chunk(it, gnext):
            ci = n_chunks - 1 - it
            off = pl.multiple_of(ci * SUBLANES, SUBLANES)
            a8 = a_s[pl.ds(off, SUBLANES), :]
            a8n = a_s[pl.ds(off + SUBLANES, SUBLANES), :]
            c8 = pltpu.roll(jnp.where(row == 0, a8n, a8), SUBLANES - 1, 0)
            g8 = g_s[pl.ds(off, SUBLANES), :]
            for d in (1, 2, 4):
                c_sh = jnp.where(row >= SUBLANES - d, 1.0, pltpu.roll(c8, SUBLANES - d, 0))
                g_sh = jnp.where(row >= SUBLANES - d, 0.0, pltpu.roll(g8, SUBLANES - d, 0))
                g8 = c8 * g_sh + g8
                c8 = c8 * c_sh
            g8 = g8 + c8 * gnext
            g_s[pl.ds(off, SUBLANES), :] = g8
            h8 = hp_s[pl.ds(off + pad, SUBLANES), :]
            h8p = hp_s[pl.ds(off, SUBLANES), :]
            da_s[pl.ds(off, SUBLANES), :] = g8 * pltpu.roll(jnp.where(row == SUBLANES - 1, h8p, h8), 1, 0)
            return g8[0:1, :]

        lax.fori_loop(0, n_chunks, chunk, jnp.zeros((1, HEAD), F32), unroll=8)

        dsp = jnp.zeros((1, HEAD), F32)
        dbc = jnp.zeros((1, HEAD), F32)
        dba = jnp.zeros((1, HEAD), F32)
        dbx = jnp.zeros((1, HEAD), F32)
        dwa = jnp.zeros((HEAD, HEAD), F32)
        dwx = jnp.zeros((HEAD, HEAD), F32)
        dwc = [jnp.zeros((1, HEAD), F32) for _ in range(CONV_TAPS)]
        for c in range(s // tc):
            sl = slice(c * tc, (c + 1) * tc)
            xc, wins = conv_in(c)
            r, ig, a, mult = _lru_gates(xc, wab, ba_ref[...], wxb, bx_ref[...], sp)
            du, da = g_s[sl, :], da_s[sl, :]
            d_ix = du * mult
            dla = da * a - (du * ig * xc) * (a * a / mult)
            dsp = dsp + jnp.sum(dla * r, axis=0, keepdims=True) * (-LRU_C)
            dpa = (dla * (-LRU_C * sp)) * r * (1.0 - r)
            dpx = (d_ix * xc) * ig * (1.0 - ig)
            dpab, dpxb, xb = dpa.astype(BF16), dpx.astype(BF16), xc.astype(BF16)
            dxc = d_ix * ig + _dot_nt(dpab, wab) + _dot_nt(dpxb, wxb)
            dwa = dwa + _dot_tn(xb, dpab)
            dwx = dwx + _dot_tn(xb, dpxb)
            dba = dba + jnp.sum(dpa, axis=0, keepdims=True)
            dbx = dbx + jnp.sum(dpx, axis=0, keepdims=True)
            dbc = dbc + jnp.sum(dxc, axis=0, keepdims=True)
            for i in range(CONV_TAPS):
                dwc[i] = dwc[i] + jnp.sum(dxc * wins[i], axis=0, keepdims=True)
            dxc_s[sl, :] = dxc

        for c in range(s // tc):
            base = c * tc
            dxr = sum(wc_ref[i:i + 1, :] * dxc_s[pl.ds(base + (CONV_TAPS - 1) - i, tc), :] for i in range(CONV_TAPS))
            dxr_ref[base:base + tc, :] = dxr.astype(BF16)

        for i in range(CONV_TAPS):
            dwc_ref[i:i + 1, :] = dwc[i]
        dbc_ref[...] = dbc
        dwa_ref[...] = dwa
        dwx_ref[...] = dwx
        dba_ref[...] = dba
        dbx_ref[...] = dbx
        dlam_ref[...] = dsp * (-_sigmoid(-lam_v))

    w = nb * HEAD
    return pl.pallas_call(
        _per_block(one_block, 20, (6, 8, 15, 17), bps), grid=(nb // bps,),
        in_specs=[seq0, seq1, seq0, seq0, taps, vec, mat, vec, mat, vec, vec],
        out_specs=(seq0, seq0, taps, vec, mat, vec, mat, vec, vec),
        out_shape=(SDS((s, w), BF16), SDS((s, w), BF16), SDS((CONV_TAPS, w), F32), SDS((1, w), F32),
                   SDS((nb, HEAD, HEAD), F32), SDS((1, w), F32), SDS((nb, HEAD, HEAD), F32), SDS((1, w), F32),
                   SDS((1, w), F32)),
        scratch_shapes=[pltpu.VMEM((s + pad, HEAD), F32), pltpu.VMEM((s + pad, HEAD), F32),
                        pltpu.VMEM((s + pad, HEAD), F32), pltpu.VMEM((s, HEAD), F32),
                        pltpu.VMEM((s, HEAD), F32), pltpu.VMEM((s + pad, HEAD), F32)],
        compiler_params=_params(("parallel",)), name="lru_bwd",
    )(xrg, xrg, dol, hseq, wconv, bconv, wa, ba, wx, bx, lam)


def _ada_mod(c_all, w_sh, b_sh):
    n_ex, d = c_all.shape
    n = w_sh.shape[1]
    tn = _tile(n, 512)

    def body(c_ref, w_ref, b_ref, mod_ref, act_ref):
        cv = c_ref[...]
        act = cv * _sigmoid(cv)
        act_ref[...] = act
        mod_ref[...] = _dot(act.astype(BF16), w_ref[...].astype(BF16)) + b_ref[...]

    return pl.pallas_call(
        body, grid=(n // tn,),
        in_specs=[pl.BlockSpec((n_ex, d), lambda j: (0, 0)), pl.BlockSpec((d, tn), lambda j: (0, j)),
                  pl.BlockSpec((1, tn), lambda j: (0, j))],
        out_specs=(pl.BlockSpec((n_ex, tn), lambda j: (0, j)), pl.BlockSpec((n_ex, d), lambda j: (0, 0))),
        out_shape=(SDS((n_ex, n), F32), SDS((n_ex, d), F32)),
        compiler_params=_params(("arbitrary",)), name="ada_mod")(c_all, w_sh, b_sh)


def _adamw_math(w, g, m, v):
    m = ADAM_B1 * m + (1.0 - ADAM_B1) * g
    v = ADAM_B2 * v + (1.0 - ADAM_B2) * (g * g)
    m_hat = m / (1.0 - ADAM_B1 ** ADAM_STEP)
    v_hat = v / (1.0 - ADAM_B2 ** ADAM_STEP)
    delta = -ADAM_LR * (m_hat / (jnp.sqrt(v_hat) + ADAM_EPS) + ADAM_WD * w)
    return delta, m, v


def _adamw_plain(name, w, g, m, v):
    def body(w_ref, g_ref, m_ref, v_ref, d_ref, mo_ref, vo_ref):
        d_ref[...], mo_ref[...], vo_ref[...] = _adamw_math(w_ref[...], g_ref[...], m_ref[...], v_ref[...])

    return pl.pallas_call(body, out_shape=(SDS(w.shape, F32),) * 3, name=name)(w, g, m, v)


def _adamw_halves(name, c_arr, w, m, v, g_own, g_recv):
    r, n = w.shape
    rh = r // 2
    tr = _tile(rh, 256)
    nh = rh // tr

    def body(c_ref, w_ref, m_ref, v_ref, go_ref, gr_ref, g_ref, d_ref, mo_ref, vo_ref):
        own = (pl.program_id(0) // nh) == c_ref[0]
        g = jnp.where(own, go_ref[...], gr_ref[...])
        g_ref[...] = g
        d_ref[...], mo_ref[...], vo_ref[...] = _adamw_math(w_ref[...], g, m_ref[...], v_ref[...])

    full = pl.BlockSpec((tr, n), lambda i, c: (i, 0))
    own = pl.BlockSpec((tr, n), lambda i, c: (jnp.where(i // nh == c[0], i % nh, 0), 0))
    recv = pl.BlockSpec((tr, n), lambda i, c: (jnp.where(i // nh == c[0], 0, i % nh), 0))
    return pl.pallas_call(
        body,
        grid_spec=pltpu.PrefetchScalarGridSpec(
            num_scalar_prefetch=1, grid=(2 * nh,), in_specs=[full, full, full, own, recv],
            out_specs=(full,) * 4),
        out_shape=(SDS((r, n), F32),) * 4,
        compiler_params=_params(("parallel",)), name=name)(c_arr, w, m, v, g_own, g_recv)


def _adamw_ada(w, m, v, act_t, dmod):
    d, n = w.shape
    n_ex = act_t.shape[1]
    tr = _tile(d, 256)

    def body(a_ref, dm_ref, w_ref, m_ref, v_ref, g_ref, d_ref, mo_ref, vo_ref):
        g = _dot(a_ref[...], dm_ref[...])
        g_ref[...] = g
        d_ref[...], mo_ref[...], vo_ref[...] = _adamw_math(w_ref[...], g, m_ref[...], v_ref[...])

    full = pl.BlockSpec((tr, n), lambda i: (i, 0))
    return pl.pallas_call(
        body, grid=(d // tr,),
        in_specs=[pl.BlockSpec((tr, n_ex), lambda i: (i, 0)), pl.BlockSpec((n_ex, n), lambda i: (0, 0)), full, full, full],
        out_specs=(full,) * 4, out_shape=(SDS((d, n), F32),) * 4,
        compiler_params=_params(("parallel",)), name="adamw_ada")(act_t, dmod, w, m, v)


def _small_reduce_adamw(parts, w, m, v):
    n_dev, r, _ = parts.shape
    tr = r if r <= PACK_ROWS else PACK_ROWS

    def body(p_ref, w_ref, m_ref, v_ref, g_ref, d_ref, mo_ref, vo_ref):
        g = p_ref[0]
        for k in range(1, n_dev):
            g = g + p_ref[k]
        g_ref[...] = g
        d_ref[...], mo_ref[...], vo_ref[...] = _adamw_math(w_ref[...], g, m_ref[...], v_ref[...])

    full = pl.BlockSpec((tr, LANES), lambda i: (i, 0))
    return pl.pallas_call(
        body, grid=(r // tr,),
        in_specs=[pl.BlockSpec((n_dev, tr, LANES), lambda i: (0, i, 0)), full, full, full],
        out_specs=(full,) * 4, out_shape=(SDS((r, LANES), F32),) * 4,
        compiler_params=_params(("parallel",)), name="small_reduce_adamw")(parts, w, m, v)


def _mesh_pos():
    return lax.axis_index("x"), lax.axis_index("y"), lax.axis_index("c")


def _other_chips(x, y):
    return [(1 - x, y), (x, 1 - y), (1 - x, 1 - y)]


def _all_gather_small(name, blk, after=()):
    r, n = blk.shape
    n_after = len(after)

    def body(x_ref, *rest):
        out_ref, send_sems, recv_sems, local_sem = rest[n_after:]
        x, y, c = _mesh_pos()
        me, sibling = (x, y, c), (x, y, 1 - c)
        chips = _other_chips(x, y)

        def rows(px, py, pc):
            return out_ref.at[4 * px + 2 * py + pc]

        def copy(k, block, to, src=None):
            return pltpu.make_async_remote_copy(
                src_ref=rows(*block) if src is None else src, dst_ref=rows(*block),
                send_sem=send_sems.at[k], recv_sem=recv_sems.at[k], device_id=to, device_id_type=MESH)

        mine = pltpu.make_async_copy(x_ref, rows(*me), local_sem)
        mine.start()
        first = [copy(0, me, sibling, src=x_ref)]
        first += [copy(1 + j, me, (*chip, c), src=x_ref) for j, chip in enumerate(chips)]
        for cp in first:
            cp.start()
        passed = [copy(4 + j, (*chip, c), sibling) for j, chip in enumerate(chips)]
        for j, chip in enumerate(chips):
            copy(1 + j, (*chip, c), me).wait_recv()
            passed[j].start()
        copy(0, sibling, me).wait_recv()
        for j, chip in enumerate(chips):
            copy(4 + j, (*chip, 1 - c), me).wait_recv()
        for cp in first + passed:
            cp.wait_send()
        mine.wait()

    return pl.pallas_call(
        body, out_shape=SDS((N_DEV, r, n), blk.dtype),
        in_specs=[pl.BlockSpec(memory_space=pltpu.VMEM)] + [pl.BlockSpec(memory_space=pl.ANY)] * n_after,
        out_specs=pl.BlockSpec(memory_space=pltpu.VMEM),
        scratch_shapes=[pltpu.SemaphoreType.DMA((7,)), pltpu.SemaphoreType.DMA((7,)), pltpu.SemaphoreType.DMA],
        compiler_params=pltpu.CompilerParams(vmem_limit_bytes=VMEM_LIMIT), name=name)(blk, *after)


_ANY = pl.BlockSpec(memory_space=pl.ANY)
_HBM = pl.BlockSpec(memory_space=pltpu.HBM)
_SEM = pl.BlockSpec(memory_space=pltpu.SEMAPHORE)
_EFFECT = pltpu.SideEffectType.DATAFLOW_SIDE_EFFECTING


def _hbm(a):
    return pltpu.with_memory_space_constraint(a, pltpu.HBM)


def _place_cast(name, j_arr, shard, kind, after):
    r, n = shard.shape
    tr = _tile(r, 256)
    nr = r // tr
    if kind == "col":
        out_shape, o_spec = (r, N_CHIP * n), pl.BlockSpec((tr, n), lambda i, j: (i, j[0]))
    else:
        out_shape, o_spec = (N_CHIP * r, n), pl.BlockSpec((tr, n), lambda i, j: (j[0] * nr + i, 0))

    def body(j_ref, s_ref, after_ref, o_ref, tok_ref):
        o_ref[...] = s_ref[...].astype(BF16)
        tok_ref[...] = jnp.zeros_like(tok_ref)

    return pl.pallas_call(
        body,
        grid_spec=pltpu.PrefetchScalarGridSpec(
            num_scalar_prefetch=1, grid=(nr,), in_specs=[pl.BlockSpec((tr, n), lambda i, j: (i, 0)), _ANY],
            out_specs=(o_spec, pl.BlockSpec((SUBLANES, LANES), lambda i, j: (0, 0)))),
        out_shape=(SDS(out_shape, BF16), SDS((SUBLANES, LANES), F32)),
        compiler_params=_params(("arbitrary",)), name=name)(j_arr, shard, after)


def _leg_direct(full, kind, x, y, c):
    mine = _full_region(full, kind, x, y, c)
    return [(mine, mine, (1 - x, y, c)), (mine, mine, (x, 1 - y, c))]


def _leg_relay(full, kind, x, y, c):
    fx, fy = jnp.where(c == 0, 1 - x, x), jnp.where(c == 0, y, 1 - y)
    tx, ty = jnp.where(c == 0, x, 1 - x), jnp.where(c == 0, 1 - y, y)
    got = _full_region(full, kind, fx, fy, c)
    return [(got, got, (tx, ty, c))]


def _leg_d2d(which):
    def leg(full, kind, x, y, c):
        chips = _other_chips(x, y)
        return [(_full_region(full, kind, *chips[k], c), _full_region(full, kind, *chips[k], c), (x, y, 1 - c))
                for k in which]
    return leg


_LEGS = {"direct": (_leg_direct, 2), "relay": (_leg_relay, 1), "d2d_near": (_leg_d2d((0, 1)), 2),
         "d2d_far": (_leg_d2d((2,)), 1)}


def _gather_call(name, fulls, kinds, waits, starts, after, thru):
    nw, n_wait, n_start = len(fulls), len(waits), len(starts)

    def body(*refs):
        wait_sems = refs[nw:nw + 2 * n_wait]
        outs = refs[nw + 2 * n_wait + 2:]
        full, start_sems = outs[:nw], outs[nw:nw + 2 * n_start]
        x, y, c = _mesh_pos()
        for i, (leg, ws, _, _) in enumerate(waits):
            fn, per = _LEGS[leg]
            for li, w in enumerate(ws):
                for k, (s_, d_, dev) in enumerate(fn(full[w], kinds[w], x, y, c)):
                    cp = pltpu.make_async_remote_copy(
                        src_ref=s_, dst_ref=d_, send_sem=wait_sems[2 * i].at[per * li + k],
                        recv_sem=wait_sems[2 * i + 1].at[per * li + k], device_id=dev, device_id_type=MESH)
                    cp.wait_recv()
                    cp.wait_send()
        for i, (leg, ws) in enumerate(starts):
            fn, per = _LEGS[leg]
            for li, w in enumerate(ws):
                for k, (s_, d_, dev) in enumerate(fn(full[w], kinds[w], x, y, c)):
                    pltpu.make_async_remote_copy(
                        src_ref=s_, dst_ref=d_, send_sem=start_sems[2 * i].at[per * li + k],
                        recv_sem=start_sems[2 * i + 1].at[per * li + k], device_id=dev, device_id_type=MESH).start()

    sems = []
    for leg, ws in starts:
        sems += [pltpu.SemaphoreType.DMA((_LEGS[leg][1] * len(ws),))] * 2
    wait_args = []
    for _, _, s_, r_ in waits:
        wait_args += [s_, r_]
    outs = pl.pallas_call(
        body,
        out_shape=tuple(pltpu.HBM(f_.shape, f_.dtype) for f_ in fulls) + tuple(sems) + (SDS(thru.shape, thru.dtype),),
        in_specs=[_HBM] * nw + [_SEM] * (2 * n_wait) + [_ANY, _ANY],
        out_specs=tuple([_HBM] * nw + [_SEM] * (2 * n_start) + [_ANY]),
        input_output_aliases={**{w: w for w in range(nw)}, nw + 2 * n_wait + 1: nw + 2 * n_start},
        compiler_params=pltpu.CompilerParams(has_side_effects=_EFFECT),
        name=name,
    )(*[_hbm(f_) for f_ in fulls], *wait_args, after, thru)
    pairs = [(outs[nw + 2 * i], outs[nw + 2 * i + 1]) for i in range(n_start)]
    return list(outs[:nw]), pairs, outs[nw + 2 * n_start]


def _full_region(full, kind, px, py, half):
    j = 2 * px + py
    if kind == "col":
        rh, cols = full.shape[0] // 2, full.shape[1] // N_CHIP
        return full.at[pl.ds(half * rh, rh), pl.ds(j * cols, cols)]
    rows = full.shape[0] // N_CHIP
    rh = rows // 2
    return full.at[pl.ds(j * rows + half * rh, rh), :]


def _half_of(ref, kind, half):
    if kind == "col":
        rh = ref.shape[0] // 2
        return ref.at[pl.ds(half * rh, rh), :]
    rh = ref.shape[1] // 2
    return ref.at[:, pl.ds(half * rh, rh), :]


def _half_shape(g, kind):
    if kind == "col":
        return (g.shape[0] // 2, g.shape[1])
    return (g.shape[0], g.shape[1] // 2, g.shape[2])


def _plan_swap_half(kind):
    def plan(src, land, x, y, c):
        return [(_half_of(src, kind, 1 - c), land, (x, y, 1 - c))]
    return plan


def _plan_scatter(kind):
    def plan(src, land, x, y, c):
        out = []
        for k, (px, py) in enumerate(_other_chips(x, y)):
            j = 2 * px + py
            if kind == "col":
                n = src.shape[1] // N_CHIP
                blk = src.at[:, pl.ds(j * n, n)]
            else:
                blk = src.at[j]
            out.append((blk, land.at[k], (px, py, c)))
        return out
    return plan


def _plan_whole(src, land, x, y, c):
    return [(src, land, (x, y, 1 - c))]


def _split_start(name, src, land_shape, n, plan, thru):
    def body(src_in, land_in, thru_in, send, recv, src_ref, land_ref, thru_out):
        x, y, c = _mesh_pos()
        for k, (s_, d_, dev) in enumerate(plan(src_ref, land_ref, x, y, c)):
            pltpu.make_async_remote_copy(src_ref=s_, dst_ref=d_, send_sem=send.at[k], recv_sem=recv.at[k],
                                         device_id=dev, device_id_type=MESH).start()

    sem = pltpu.SemaphoreType.DMA((n,))
    return pl.pallas_call(
        body,
        out_shape=(sem, sem, pltpu.HBM(src.shape, src.dtype), pltpu.HBM(land_shape, src.dtype), SDS(thru.shape, thru.dtype)),
        in_specs=[_HBM, _HBM, _ANY], out_specs=(_SEM, _SEM, _HBM, _HBM, _ANY),
        input_output_aliases={0: 2, 1: 3, 2: 4},
        compiler_params=pltpu.CompilerParams(has_side_effects=_EFFECT), name=name,
    )(_hbm(src), _hbm(lax.empty(land_shape, src.dtype)), thru)


def _split_wait(name, send, recv, src, land, plan, after):
    def body(src_in, land_in, send_r, recv_r, after_r, src_ref, land_ref):
        x, y, c = _mesh_pos()
        for k, (s_, d_, dev) in enumerate(plan(src_ref, land_ref, x, y, c)):
            cp = pltpu.make_async_remote_copy(src_ref=s_, dst_ref=d_, send_sem=send_r.at[k], recv_sem=recv_r.at[k],
                                              device_id=dev, device_id_type=MESH)
            cp.wait_send()
            cp.wait_recv()

    return pl.pallas_call(
        body,
        out_shape=(pltpu.HBM(src.shape, src.dtype), pltpu.HBM(land.shape, land.dtype)),
        in_specs=[_HBM, _HBM, _SEM, _SEM, _ANY], out_specs=(_HBM, _HBM),
        input_output_aliases={0: 0, 1: 1},
        compiler_params=pltpu.CompilerParams(has_side_effects=_EFFECT), name=name,
    )(src, land, send, recv, after)


def _dev_row(buf, px, py, pc):
    return buf.at[4 * px + 2 * py + pc]


def _plan_gather_own(buf, land, x, y, c):
    own = _dev_row(buf, x, y, c)
    return [(own, own, (x, y, 1 - c))] + [(own, own, (px, py, c)) for px, py in _other_chips(x, y)]


def _plan_gather_pass(buf, land, x, y, c):
    return [(_dev_row(buf, px, py, c), _dev_row(buf, px, py, c), (x, y, 1 - c)) for px, py in _other_chips(x, y)]


def _split_start_inplace(name, buf, n, plan, thru):
    def body(buf_in, thru_in, send, recv, buf_ref, thru_out):
        x, y, c = _mesh_pos()
        for k, (s_, d_, dev) in enumerate(plan(buf_ref, buf_ref, x, y, c)):
            pltpu.make_async_remote_copy(src_ref=s_, dst_ref=d_, send_sem=send.at[k], recv_sem=recv.at[k],
                                         device_id=dev, device_id_type=MESH).start()

    sem = pltpu.SemaphoreType.DMA((n,))
    return pl.pallas_call(
        body, out_shape=(sem, sem, pltpu.HBM(buf.shape, buf.dtype), SDS(thru.shape, thru.dtype)),
        in_specs=[_HBM, _ANY], out_specs=(_SEM, _SEM, _HBM, _ANY), input_output_aliases={0: 2, 1: 3},
        compiler_params=pltpu.CompilerParams(has_side_effects=_EFFECT), name=name)(_hbm(buf), thru)


def _split_wait_inplace(name, send, recv, buf, plan, after):
    def body(buf_in, send_r, recv_r, after_r, buf_ref):
        x, y, c = _mesh_pos()
        for k, (s_, d_, dev) in enumerate(plan(buf_ref, buf_ref, x, y, c)):
            cp = pltpu.make_async_remote_copy(src_ref=s_, dst_ref=d_, send_sem=send_r.at[k], recv_sem=recv_r.at[k],
                                              device_id=dev, device_id_type=MESH)
            cp.wait_send()
            cp.wait_recv()

    return pl.pallas_call(
        body, out_shape=pltpu.HBM(buf.shape, buf.dtype), in_specs=[_HBM, _SEM, _SEM, _ANY], out_specs=_HBM,
        input_output_aliases={0: 0}, compiler_params=pltpu.CompilerParams(has_side_effects=_EFFECT), name=name,
    )(buf, send, recv, after)


def _place_row(name, me_arr, slab):
    r, n = slab.shape
    tr = r if r <= PACK_ROWS else PACK_ROWS

    def body(me_ref, s_ref, o_ref):
        o_ref[...] = s_ref[...]

    return pl.pallas_call(
        body,
        grid_spec=pltpu.PrefetchScalarGridSpec(
            num_scalar_prefetch=1, grid=(r // tr,), in_specs=[pl.BlockSpec((tr, n), lambda i, me: (i, 0))],
            out_specs=pl.BlockSpec((None, tr, n), lambda i, me: (me[0], i, 0))),
        out_shape=SDS((N_DEV, r, n), slab.dtype), compiler_params=_params(("parallel",)), name=name)(me_arr, slab)


def _add_halves(name, c_arr, g, got, kind):
    if kind == "col":
        rh, n = got.shape
        tr = _tile(rh, 256)
        nh = rh // tr
        g_spec = pl.BlockSpec((tr, n), lambda i, c: (c[0] * nh + i, 0))
        o_spec = pl.BlockSpec((tr, n), lambda i, c: (i, 0))
        grid = (nh,)
    else:
        nc, rh, n = got.shape
        tr = _tile(rh, 256)
        nh = rh // tr
        g_spec = pl.BlockSpec((None, tr, n), lambda j, i, c: (j, c[0] * nh + i, 0))
        o_spec = pl.BlockSpec((None, tr, n), lambda j, i, c: (j, i, 0))
        grid = (nc, nh)

    def body(c_ref, g_ref, r_ref, o_ref):
        o_ref[...] = (g_ref[...].astype(F32) + r_ref[...].astype(F32)).astype(o_ref.dtype)

    return pl.pallas_call(
        body,
        grid_spec=pltpu.PrefetchScalarGridSpec(num_scalar_prefetch=1, grid=grid, in_specs=[g_spec, o_spec], out_specs=o_spec),
        out_shape=SDS(got.shape, got.dtype),
        compiler_params=_params(("parallel",) * len(grid)), name=name)(c_arr, g, got)


def _sum_partials(name, j_arr, part, got, kind):
    _, rh, n = got.shape
    tr = _tile(rh, 256)
    if kind == "col":
        p_spec = pl.BlockSpec((tr, n), lambda i, j: (i, j[0]))
    else:
        p_spec = pl.BlockSpec((None, tr, n), lambda i, j: (j[0], i, 0))

    def body(j_ref, p_ref, r_ref, o_ref):
        o_ref[...] = ((p_ref[...].astype(F32) + r_ref[0].astype(F32)) + r_ref[1].astype(F32)) + r_ref[2].astype(F32)

    return pl.pallas_call(
        body,
        grid_spec=pltpu.PrefetchScalarGridSpec(
            num_scalar_prefetch=1, grid=(rh // tr,),
            in_specs=[p_spec, pl.BlockSpec((3, tr, n), lambda i, j: (0, i, 0))],
            out_specs=pl.BlockSpec((tr, n), lambda i, j: (i, 0))),
        out_shape=SDS((rh, n), F32),
        compiler_params=_params(("parallel",)), name=name)(j_arr, part, got)


def _swap_reduced(name, halves):
    nw = len(halves)

    def body(*refs):
        h, got = refs[:nw], refs[nw:2 * nw]
        send_sems, recv_sems = refs[2 * nw:]
        x, y, c = _mesh_pos()
        cps = []
        for w in range(nw):
            cp = pltpu.make_async_remote_copy(
                src_ref=h[w], dst_ref=got[w], send_sem=send_sems.at[w], recv_sem=recv_sems.at[w],
                device_id=(x, y, 1 - c), device_id_type=MESH)
            cp.start()
            cps.append(cp)
        for cp in cps:
            cp.wait()

    return pl.pallas_call(
        body, out_shape=tuple(SDS(h.shape, h.dtype) for h in halves),
        in_specs=[_ANY] * nw, out_specs=tuple([_ANY] * nw),
        scratch_shapes=[pltpu.SemaphoreType.DMA((nw,)), pltpu.SemaphoreType.DMA((nw,))],
        name=name)(*halves)


def _pack(arrays):
    flat = [a.reshape(-1).astype(F32) for a in arrays]
    flat = [jnp.pad(f, (0, (-f.shape[0]) % LANES)) for f in flat]
    sizes = [f.shape[0] for f in flat]
    total = sum(sizes)
    rows = total // LANES
    tail = LANES * ((-rows) % (PACK_ROWS if rows > PACK_ROWS else SUBLANES))
    if tail:
        flat.append(jnp.zeros((tail,), F32))
    return jnp.concatenate(flat).reshape(-1, LANES), sizes


def _unpack(slab, sizes, shapes, lead=()):
    flat = slab.reshape(lead + (-1,))
    out, off = [], 0
    for sz, shp in zip(sizes, shapes):
        n = math.prod(shp)
        out.append(flat[..., off:off + n].reshape(lead + tuple(shp)))
        off += sz
    return out


def kernel(x, c, w_ada, b_ada, g_norm_mix, w_in, w_conv, b_conv, w_rg_a, b_rg_a, w_rg_x, b_rg_x, lru_lambda, g_attn_out, g_lru_out, w_out, g_norm_mlp, w_mlp_in, w_mlp_out, g_norm_final, loss_target, m_w_ada, m_b_ada, m_g_norm_mix, m_w_in, m_w_conv, m_b_conv, m_w_rg_a, m_b_rg_a, m_w_rg_x, m_b_rg_x, m_lru_lambda, m_g_attn_out, m_g_lru_out, m_w_out, m_g_norm_mlp, m_w_mlp_in, m_w_mlp_out, m_g_norm_final, v_w_ada, v_b_ada, v_g_norm_mix, v_w_in, v_w_conv, v_b_conv, v_w_rg_a, v_b_rg_a, v_w_rg_x, v_b_rg_x, v_lru_lambda, v_g_attn_out, v_g_lru_out, v_w_out, v_g_norm_mlp, v_w_mlp_in, v_w_mlp_out, v_g_norm_final):
    s, d = x.shape[1], x.shape[2]
    aw = d // 2
    nh = aw // HEAD
    f = w_mlp_out.shape[1] * N_CHIP
    n_ada = w_ada.shape[2]
    n_cv = w_conv.shape[2]
    ix, iy, ic = lax.axis_index("x"), lax.axis_index("y"), lax.axis_index("c")
    chip = 2 * ix + iy
    me = 2 * chip + ic
    c_arr = jnp.reshape(ic, (1,)).astype(jnp.int32)
    j_arr = jnp.reshape(chip, (1,)).astype(jnp.int32)

    x2d, tgt = x[0], loss_target[0]

    k_in, k_out, k_mi, k_mo = kinds = ("col", "row", "col", "row")
    slab, sizes = _pack([c, w_conv])
    p_in, _ = _place_cast("place_cast_0", j_arr, w_in[0], k_in, c)
    (f_in,), (dir_in,), slab = _gather_call("gather_0", [p_in], [k_in], [], [("direct", [0])], c, slab)
    p_out, tok = _place_cast("place_cast_1", j_arr, w_out[0], k_out, slab)
    p_mi, tok = _place_cast("place_cast_2", j_arr, w_mlp_in[0], k_mi, tok)
    p_mo, tok = _place_cast("place_cast_3", j_arr, w_mlp_out[0], k_mo, tok)

    gathered = _all_gather_small("comm_gather_cond", slab, after=(tok,))
    c_parts, cv_parts = _unpack(gathered, sizes, [(d,), (CONV_TAPS, n_cv)], lead=(N_DEV,))
    c_all = c_parts
    w_conv_full = jnp.concatenate([cv_parts[2 * j] for j in range(N_CHIP)], axis=-1)
    b_sh = lax.dynamic_slice(b_ada, (0, chip * n_ada), (1, n_ada))
    (f_in,), (rel_in, near_in), c_all = _gather_call(
        "gather_1", [f_in], [k_in], [("direct", [0], *dir_in)], [("relay", [0]), ("d2d_near", [0])], gathered, c_all)
    mod_part, act_all = _ada_mod(c_all, w_ada[0], b_sh)
    mod_g = _all_gather_small("comm_gather_mod", mod_part.reshape(-1, LANES))
    mod_g = mod_g.reshape(N_DEV, N_DEV, n_ada)
    mod = jnp.concatenate([lax.dynamic_index_in_dim(mod_g[2 * j], me, 0, keepdims=True) for j in range(N_CHIP)], axis=-1)
    sh1, sc1, gt1, sh2, sc2, gt2 = [mod[:, k * d:(k + 1) * d] for k in range(N_MOD)]

    (f_in, f_out, f_mi, f_mo), (far_in, dir_om, dir_mo), sh1 = _gather_call(
        "gather_2", [f_in, p_out, p_mi, p_mo], kinds, [("relay", [0], *rel_in)],
        [("d2d_far", [0]), ("direct", [1, 2]), ("direct", [3])], mod, sh1)

    h1, rstd1 = _norm_mod_fwd("norm_mod_fwd1", x2d, g_norm_mix, sc1, sh1)
    (w_in_f,), _, h1 = _gather_call("gather_3", [f_in], [k_in],
                                    [("d2d_near", [0], *near_in), ("d2d_far", [0], *far_in)], [], rstd1, h1)
    (qkv,) = _matmul("mm_qkv", h1, w_in_f, "nn", s, 3 * aw, d, (BF16,))
    (xrg,) = _matmul("mm_xrg", h1, w_in_f, "nn", s, 2 * aw, d, (F32,), b_off=3 * aw)
    o_attn, attn_w, attn_sg = _attn_fwd(qkv, nh)
    (f_out, f_mi), (rel_om, near_om), xrg = _gather_call(
        "gather_4", [f_out, f_mi], [k_out, k_mi], [("direct", [0, 1], *dir_om)],
        [("relay", [0, 1]), ("d2d_near", [0, 1])], o_attn, xrg)
    wa3, wx3 = w_rg_a[0], w_rg_x[0]
    o_lru, hseq = _lru_fwd(xrg, w_conv_full, b_conv, wa3, b_rg_a, wx3, b_rg_x, lru_lambda)
    mixed, rstd_a, rstd_l = _mix_norm_fwd(o_attn, o_lru, g_attn_out, g_lru_out)
    (f_out, f_mi, f_mo), (far_om, rel_mo, near_mo), mixed = _gather_call(
        "gather_5", [f_out, f_mi, f_mo], [k_out, k_mi, k_mo], [("relay", [0, 1], *rel_om), ("direct", [2], *dir_mo)],
        [("d2d_far", [0, 1]), ("relay", [2]), ("d2d_near", [2])], rstd_a, mixed)
    (w_out_f, w_mi_f), _, mixed = _gather_call(
        "gather_6", [f_out, f_mi], [k_out, k_mi], [("d2d_near", [0, 1], *near_om), ("d2d_far", [0, 1], *far_om)], [],
        rstd_l, mixed)

    def residual(acc, xin, gt):
        return acc, xin + gt * acc

    y1, x1 = _matmul("mm_out", mixed, w_out_f, "nn", s, d, d, (BF16, F32), extras=(x2d, gt1),
                     extra_kinds=("tile", "row"), epilogue=residual)
    h2, rstd2 = _norm_mod_fwd("norm_mod_fwd2", x1, g_norm_mlp, sc2, sh2)
    (f_mo,), (far_mo,), h2 = _gather_call("gather_7", [f_mo], [k_mo], [("relay", [0], *rel_mo)],
                                          [("d2d_far", [0])], rstd2, h2)

    def sq_relu(acc):
        r = jnp.maximum(acc, 0.0)
        return 2.0 * r, r * r

    r2, hid = _matmul("mm_mlp_in", h2, w_mi_f, "nn", s, f, d, (BF16, BF16), epilogue=sq_relu)
    (w_mo_f,), _, hid = _gather_call("gather_8", [f_mo], [k_mo],
                                     [("d2d_near", [0], *near_mo), ("d2d_far", [0], *far_mo)], [], r2, hid)
    y2, x2 = _matmul("mm_mlp_out", hid, w_mo_f, "nn", s, d, f, (BF16, F32), extras=(x1, gt2),
                     extra_kinds=("tile", "row"), epilogue=residual)
    dx2, loss_row, dg_final, dy2, dgt2 = _final_loss(x2, g_norm_final.reshape(1, d), tgt, y2, gt2)

    def rs_begin(tag, g, kind, thru):
        send, recv, g, land, thru = _split_start("rs_swap_start_" + tag, g, _half_shape(g, kind), 1,
                                                 _plan_swap_half(kind), thru)
        return {"tag": tag, "kind": kind, "swap": (send, recv, g, land)}, thru

    def rs_mid(st, after, thru):
        tag, kind = st["tag"], st["kind"]
        g, got = _split_wait("rs_swap_wait_" + tag, *st["swap"], _plan_swap_half(kind), after)
        part = _add_halves("add_halves_" + tag, c_arr, g, got, kind)
        blk = (part.shape[0], part.shape[1] // N_CHIP) if kind == "col" else part.shape[1:]
        send, recv, part, land, thru = _split_start("rs_scatter_start_" + tag, part, (N_CHIP - 1,) + blk, N_CHIP - 1,
                                                    _plan_scatter(kind), thru)
        st["scatter"] = (send, recv, part, land)
        return thru

    def rs_end(st, after):
        tag, kind = st["tag"], st["kind"]
        part, got = _split_wait("rs_scatter_wait_" + tag, *st["scatter"], _plan_scatter(kind), after)
        return _sum_partials("sum_partials_" + tag, j_arr, part, got, kind)

    (dpre,) = _matmul("mm_dhid", dy2, w_mo_f, "nt", s, f, d, (BF16,), extras=(r2,), extra_kinds=("tile",),
                      epilogue=lambda acc, r: (acc * r.astype(F32),))
    (g_mo,) = _matmul("mm_dw_mlp_out", hid, dy2, "tn", f, d, s, (BF16,))
    st_mo, dpre = rs_begin("mo", g_mo.reshape(N_CHIP, f // N_CHIP, d), "row", dpre)
    (dh2,) = _matmul("mm_dh2", dpre, w_mi_f, "nt", s, d, f, (BF16,))
    (g_mi,) = _matmul("mm_dw_mlp_in", h2, dpre, "tn", d, f, s, (BF16,))
    dh2 = rs_mid(st_mo, g_mi, dh2)
    st_mi, dh2 = rs_begin("mi", g_mi, "col", dh2)
    dx1, dsh2, dsc2, dg_mlp, dy1, dgt1 = _norm_mod_bwd("norm_mod_bwd2", dh2, x1, rstd2, g_norm_mlp, sc2, dx2,
                                                       gate=(y1, gt1))
    (dmixed,) = _matmul("mm_dmixed", dy1, w_out_f, "nt", s, d, d, (BF16,))
    (g_out,) = _matmul("mm_dw_out", mixed, dy1, "tn", d, d, s, (BF16,))
    dmixed = rs_mid(st_mi, g_out, dmixed)
    st_out, dmixed = rs_begin("out", g_out.reshape(N_CHIP, d // N_CHIP, d), "row", dmixed)
    do_attn, do_lru, dg_attn, dg_lru = _mix_norm_bwd(dmixed, o_attn, o_lru, rstd_a, rstd_l, g_attn_out, g_lru_out)
    dq, dk, dv = _attn_bwd(qkv, do_attn, attn_w, attn_sg, nh)
    do_lru = rs_mid(st_out, dq, do_lru)
    dxr, dxg, dwconv, dbconv, dwa, dba, dwx, dbx, dlam = _lru_bwd(
        xrg, do_lru, hseq, w_conv_full, b_conv, wa3, b_rg_a, wx3, b_rg_x, lru_lambda)
    dproj = jnp.concatenate([dq, dk, dv, dxr, dxg], axis=-1)
    (dh1,) = _matmul("mm_dh1", dproj, w_in_f, "nt", s, d, 5 * aw, (BF16,))
    (g_in,) = _matmul("mm_dw_in", h1, dproj, "tn", d, 5 * aw, s, (BF16,))
    st_in, dh1 = rs_begin("in", g_in, "col", dh1)
    grad_x, dsh1, dsc1, dg_mix = _norm_mod_bwd("norm_mod_bwd1", dh1, x2d, rstd1, g_norm_mix, sc1, dx1)

    dmod = jnp.concatenate([dsh1, dsc1, dgt1, dsh2, dsc2, dgt2], axis=-1)
    small_names = ["b_ada", "g_norm_mix", "b_conv", "w_rg_a", "b_rg_a", "w_rg_x", "b_rg_x", "lru_lambda",
                   "g_attn_out", "g_lru_out", "g_norm_mlp", "g_norm_final"]
    small_g = [dmod, dg_mix, dbconv, dwa, dba, dwx, dbx, dlam, dg_attn, dg_lru, dg_mlp, dg_final]
    small_w = [b_ada, g_norm_mix, b_conv, w_rg_a, b_rg_a, w_rg_x, b_rg_x, lru_lambda, g_attn_out, g_lru_out, g_norm_mlp, g_norm_final]
    small_m = [m_b_ada, m_g_norm_mix, m_b_conv, m_w_rg_a, m_b_rg_a, m_w_rg_x, m_b_rg_x, m_lru_lambda, m_g_attn_out, m_g_lru_out, m_g_norm_mlp, m_g_norm_final]
    small_v = [v_b_ada, v_g_norm_mix, v_b_conv, v_w_rg_a, v_b_rg_a, v_w_rg_x, v_b_rg_x, v_lru_lambda, v_g_attn_out, v_g_lru_out, v_g_norm_mlp, v_g_norm_final]
    extra_zero = [jnp.zeros_like(dwconv), jnp.zeros((LANES,), F32)]
    g_slab, g_sizes = _pack(small_g + [dwconv, loss_row])
    w_slab, _ = _pack(small_w + extra_zero)
    m_slab, _ = _pack(small_m + extra_zero)
    v_slab, _ = _pack(small_v + extra_zero)
    me_arr = jnp.reshape(me, (1,)).astype(jnp.int32)
    g_buf = _place_row("place_small_grads", me_arr, g_slab)
    sg_send, sg_recv, g_buf, tok = _split_start_inplace("sg_gather_start", g_buf, N_CHIP, _plan_gather_own, loss_row)
    tok = rs_mid(st_in, tok, tok)

    def reduced_begin(tag, half, tok_):
        send, recv, half, land, tok_ = _split_start("rs_reduced_start_" + tag, half, half.shape, 1, _plan_whole, tok_)
        return (send, recv, half, land), tok_

    def reduced_end(tag, st, after):
        return _split_wait("rs_reduced_wait_" + tag, *st, _plan_whole, after)

    sw_mo, tok = reduced_begin("mo", rs_end(st_mo, tok), tok)
    sw_mi, tok = reduced_begin("mi", rs_end(st_mi, tok), tok)
    sw_out, tok = reduced_begin("out", rs_end(st_out, tok), tok)
    g_buf = _split_wait_inplace("sg_gather_wait", sg_send, sg_recv, g_buf, _plan_gather_own, tok)
    sg_send, sg_recv, g_buf, tok = _split_start_inplace("sg_pass_start", g_buf, N_CHIP - 1, _plan_gather_pass, tok)
    half_mo, got_mo = reduced_end("mo", sw_mo, tok)
    big = {"w_mlp_out": _adamw_halves("adamw_w_mlp_out", c_arr, w_mlp_out[0], m_w_mlp_out[0], v_w_mlp_out[0],
                                      half_mo, got_mo)}
    g_all = _split_wait_inplace("sg_pass_wait", sg_send, sg_recv, g_buf, _plan_gather_pass, big["w_mlp_out"][1])
    gs_slab, ds_slab, ms_slab, vs_slab = _small_reduce_adamw(g_all, w_slab, m_slab, v_slab)
    shapes = [w.shape for w in small_w] + [dwconv.shape, (LANES,)]
    gs = _unpack(gs_slab, g_sizes, shapes)
    ds = _unpack(ds_slab, g_sizes, shapes)
    ms = _unpack(ms_slab, g_sizes, shapes)
    vs = _unpack(vs_slab, g_sizes, shapes)
    small = {n: (gs[i], ds[i], ms[i], vs[i]) for i, n in enumerate(small_names)}
    loss = gs[-1][0]
    g_wconv = lax.dynamic_slice(gs[-2], (0, chip * n_cv), (CONV_TAPS, n_cv))
    d_wconv, m_wconv, v_wconv = _adamw_plain("adamw_conv", w_conv[0], g_wconv, m_w_conv[0], v_w_conv[0])
    small["w_conv"] = (g_wconv[None], d_wconv[None], m_wconv[None], v_wconv[None])

    dmod_all = g_all[:, :N_MOD * d // LANES, :].reshape(N_DEV, N_MOD * d)
    dmod_sel = lax.dynamic_slice(dmod_all, (0, chip * n_ada), (N_DEV, n_ada)).astype(BF16)
    act_t = act_all.T.astype(BF16)
    big["w_ada"] = _adamw_ada(w_ada[0], m_w_ada[0], v_w_ada[0], act_t, dmod_sel)

    half_mi, got_mi = reduced_end("mi", sw_mi, big["w_ada"][1])
    big["w_mlp_in"] = _adamw_halves("adamw_w_mlp_in", c_arr, w_mlp_in[0], m_w_mlp_in[0], v_w_mlp_in[0], half_mi, got_mi)
    half_out, got_out = reduced_end("out", sw_out, big["w_mlp_in"][1])
    big["w_out"] = _adamw_halves("adamw_w_out", c_arr, w_out[0], m_w_out[0], v_w_out[0], half_out, got_out)
    half_in = rs_end(st_in, big["w_out"][1])
    (got_in,) = _swap_reduced("comm_swap_reduced_in", [half_in])
    big["w_in"] = _adamw_halves("adamw_w_in", c_arr, w_in[0], m_w_in[0], v_w_in[0], half_in, got_in)

    order = ["w_ada", "b_ada", "g_norm_mix", "w_in", "w_conv", "b_conv", "w_rg_a", "b_rg_a", "w_rg_x", "b_rg_x",
             "lru_lambda", "g_attn_out", "g_lru_out", "w_out", "g_norm_mlp", "w_mlp_in", "w_mlp_out", "g_norm_final"]
    res = {}
    for n in order:
        res[n] = tuple(t[None] for t in big[n]) if n in big else small[n]
    return (loss, grad_x[None],
            *[res[n][0] for n in order], *[res[n][1] for n in order],
            *[res[n][2] for n in order], *[res[n][3] for n in order])
```

```python
import functools
import math

import jax
import jax.numpy as jnp
from jax import lax
from jax.experimental import pallas as pl
from jax.experimental.pallas import tpu as pltpu

F32 = jnp.float32
BF16 = jnp.bfloat16
SDS = jax.ShapeDtypeStruct
MESH = pl.DeviceIdType.MESH

EPS = 1e-6
HEAD = 128
N_MOD = 6
CONV_TAPS = 4
LRU_C = 8.0
ADAM_LR, ADAM_B1, ADAM_B2, ADAM_EPS, ADAM_WD, ADAM_STEP = 0.001, 0.9, 0.999, 1e-08, 0.01, 10
N_DEV = 8
N_CHIP = 4
LANES = 128
SUBLANES = 8
VMEM_LIMIT = 56 * 1024 * 1024
PACK_ROWS = 256
MM_TILE_M, MM_TILE_N, MM_TILE_K = 1024, 1024, 2048
ROW_TILE = 256
ROW_SPLIT = 1


def _tile(dim, pref):
    t = min(dim, pref)
    while dim % t:
        t -= LANES
    return t


def _params(sem=None):
    return pltpu.CompilerParams(dimension_semantics=sem, vmem_limit_bytes=VMEM_LIMIT)


def _sigmoid(x):
    return 1.0 / (1.0 + jnp.exp(-x))


def _log_sigmoid(x):
    return jnp.minimum(x, 0.0) - jnp.log(1.0 + jnp.exp(-jnp.abs(x)))


def _gelu_parts(x):
    k0, k1 = math.sqrt(2.0 / math.pi), 0.044715
    t = jnp.tanh(k0 * (x + k1 * x * x * x))
    val = 0.5 * x * (1.0 + t)
    der = 0.5 * (1.0 + t) + 0.5 * x * (1.0 - t * t) * k0 * (1.0 + 3.0 * k1 * x * x)
    return val, der


def _dot(a, b):
    return jnp.dot(a, b, preferred_element_type=F32)


def _dot_nt(a, b):
    return lax.dot_general(a, b, (((1,), (1,)), ((), ())), preferred_element_type=F32)


def _dot_tn(a, b):
    return lax.dot_general(a, b, (((0,), (0,)), ((), ())), preferred_element_type=F32)


def _split_dot(x, tri):
    hi = x.astype(BF16)
    lo = (x - hi.astype(F32)).astype(BF16)
    return _dot(hi, tri) + _dot(lo, tri)


def _matmul(name, a, b, mode, m, n, k, out_dtypes, *, b_off=0, extras=(), extra_kinds=(), epilogue=None,
            tm=MM_TILE_M, tn=MM_TILE_N, tk=MM_TILE_K, m_half=None):
    tm, tn, tk = _tile(m, tm), _tile(math.gcd(n, b_off) if b_off else n, tn), _tile(k, tk)
    assert b_off % tn == 0
    nk = k // tk
    n_ex, n_out = len(extras), len(out_dtypes)
    dot = {"nn": _dot, "nt": _dot_nt, "tn": _dot_tn}[mode]
    n_pre = 0 if m_half is None else 1
    m_out = m if m_half is None else m // 2

    def body(*refs):
        a_ref, b_ref, *rest = refs[n_pre:]
        ex, outs = rest[:n_ex], rest[n_ex:n_ex + n_out]

        def finish(total):
            res = epilogue(total, *[e[...] for e in ex]) if epilogue else (total,)
            for o, r in zip(outs, res):
                o[...] = r.astype(o.dtype)

        if nk == 1:
            finish(dot(a_ref[...], b_ref[...]))
            return
        acc = rest[-1]
        kk = pl.program_id(2)

        @pl.when(kk == 0)
        def _():
            acc[...] = dot(a_ref[...], b_ref[...])

        @pl.when(jnp.logical_and(kk > 0, kk < nk - 1))
        def _():
            acc[...] += dot(a_ref[...], b_ref[...])

        @pl.when(kk == nk - 1)
        def _():
            finish(acc[...] + dot(a_ref[...], b_ref[...]))

    if mode == "nn":
        a_spec = pl.BlockSpec((tm, tk), lambda i, j, kk, *_: (i, kk))
        b_spec = pl.BlockSpec((tk, tn), lambda i, j, kk, *_: (kk, j + b_off // tn))
    elif mode == "nt":
        a_spec = pl.BlockSpec((tm, tk), lambda i, j, kk, *_: (i, kk))
        b_spec = pl.BlockSpec((tn, tk), lambda i, j, kk, *_: (j, kk + b_off // tk))
    elif m_half is None:
        a_spec = pl.BlockSpec((tk, tm), lambda i, j, kk: (kk, i))
        b_spec = pl.BlockSpec((tk, tn), lambda i, j, kk: (kk, j))
    else:
        a_spec = pl.BlockSpec((tk, tm), lambda i, j, kk, h: (kk, 2 * i + h[0]))
        b_spec = pl.BlockSpec((tk, tn), lambda i, j, kk, h: (kk, j))
    tile_spec = pl.BlockSpec((tm, tn), lambda i, j, kk, *_: (i, j))
    row_spec = pl.BlockSpec((1, tn), lambda i, j, kk, *_: (0, j))
    in_specs = [a_spec, b_spec] + [tile_spec if kind == "tile" else row_spec for kind in extra_kinds]
    out_specs = tuple(tile_spec for _ in out_dtypes)
    out_shape = tuple(SDS((m_out, n), dt) for dt in out_dtypes)
    scratch = [pltpu.VMEM((tm, tn), F32)] if nk > 1 else []
    grid = (m_out // tm, n // tn, nk)
    semantics = _params(("parallel", "parallel", "arbitrary"))
    if m_half is None:
        return pl.pallas_call(body, grid=grid, in_specs=in_specs, out_specs=out_specs, out_shape=out_shape,
                              scratch_shapes=scratch, compiler_params=semantics, name=name)(a, b, *extras)
    assert mode == "tn" and (m // tm) % 2 == 0
    return pl.pallas_call(
        body,
        grid_spec=pltpu.PrefetchScalarGridSpec(num_scalar_prefetch=1, grid=grid, in_specs=in_specs, out_specs=out_specs,
                                               scratch_shapes=scratch),
        out_shape=out_shape, compiler_params=semantics, name=name)(m_half, a, b, *extras)


def _row_specs(s, d, tr):
    row = "row"
    vec = pl.BlockSpec((1, d), lambda i: (0, 0))
    col = pl.BlockSpec((tr, 1), lambda i: (i, 0))
    return row, vec, col


class _ColChunks:
    def __init__(self, refs):
        self.refs = refs

    def __getitem__(self, idx):
        return jnp.concatenate([r[...] for r in self.refs], axis=-1)


def _rows_call(name, body, grid, in_specs, out_specs, out_shape, semantics, args):
    in_x, args_x, groups = [], [], []
    for spec, arr in zip(in_specs, args):
        if isinstance(spec, str):
            rows, d = arr.shape
            tr, dc = rows // grid[0], d // ROW_SPLIT
            in_x += [pl.BlockSpec((tr, dc), functools.partial(lambda i, jj: (i, jj), jj=j)) for j in range(ROW_SPLIT)]
            args_x += [arr] * ROW_SPLIT
            groups.append(ROW_SPLIT)
        else:
            in_x.append(spec)
            args_x.append(arr)
            groups.append(1)
    out_x = [pl.BlockSpec((sh.shape[0] // grid[0], sh.shape[1]), lambda i: (i, 0)) if isinstance(spec, str) else spec
             for spec, sh in zip(out_specs, out_shape)]

    def wrapped(*refs):
        views, k = [], 0
        for g in groups:
            views.append(_ColChunks(refs[k:k + g]) if g > 1 else refs[k])
            k += g
        body(*views, *refs[k:])

    return pl.pallas_call(
        wrapped, grid=grid, in_specs=in_x, out_specs=tuple(out_x), out_shape=tuple(out_shape),
        compiler_params=_params(semantics), name=name)(*args_x)


def _norm_mod_fwd(name, x, g, sc, sh):
    s, d = x.shape
    tr = _tile(s, ROW_TILE)
    row, vec, col = _row_specs(s, d, tr)

    def body(x_ref, g_ref, sc_ref, sh_ref, h_ref, r_ref):
        xv = x_ref[...]
        r = lax.rsqrt(jnp.mean(xv * xv, axis=-1, keepdims=True) + EPS)
        h_ref[...] = ((xv * r * g_ref[...]) * (1.0 + sc_ref[...]) + sh_ref[...]).astype(BF16)
        r_ref[...] = r

    return _rows_call(name, body, (s // tr,), [row, vec, vec, vec], (row, col),
                      (SDS((s, d), BF16), SDS((s, 1), F32)), ("parallel",), (x, g, sc, sh))


def _norm_mod_bwd(name, dh, xin, rstd, g, sc, dres, gate=None):
    s, d = xin.shape
    tr = _tile(s, ROW_TILE)
    row, vec, col = _row_specs(s, d, tr)

    n_gate = 2 if gate is not None else 0

    def body(dh_ref, x_ref, r_ref, g_ref, sc_ref, dres_ref, *rest):
        gate_in, gate_out = rest[:n_gate], rest[n_gate + 4:]
        dx_ref, dsh_ref, dsc_ref, dg_ref = rest[n_gate:n_gate + 4]

        @pl.when(pl.program_id(0) == 0)
        def _():
            for ref in (dsh_ref, dsc_ref, dg_ref) + tuple(gate_out[1:]):
                ref[...] = jnp.zeros_like(ref)

        dh_v, xv, r, gv = dh_ref[...].astype(F32), x_ref[...], r_ref[...], g_ref[...]
        n0 = xv * r
        dsh_ref[...] += jnp.sum(dh_v, axis=0, keepdims=True)
        dsc_ref[...] += jnp.sum(dh_v * (n0 * gv), axis=0, keepdims=True)
        dn = dh_v * (1.0 + sc_ref[...])
        dg_ref[...] += jnp.sum(dn * n0, axis=0, keepdims=True)
        gy = dn * gv
        dot = jnp.mean(gy * xv, axis=-1, keepdims=True)
        dxv = dres_ref[...] + r * gy - xv * (r * r * r * dot)
        dx_ref[...] = dxv
        if gate is not None:
            y_ref, gt_ref = gate_in
            dy_ref, dgt_ref = gate_out
            dy_ref[...] = (gt_ref[...] * dxv).astype(BF16)
            dgt_ref[...] += jnp.sum(dxv * y_ref[...], axis=0, keepdims=True)

    vecs = SDS((1, d), F32)
    gate_args = tuple(gate) if gate is not None else ()
    return _rows_call(
        name, body, (s // tr,),
        [row, row, col, vec, vec, row] + ([row, vec] if gate is not None else []),
        (row, vec, vec, vec) + ((row, vec) if gate is not None else ()),
        (SDS((s, d), F32), vecs, vecs, vecs) + ((SDS((s, d), BF16), vecs) if gate is not None else ()),
        ("arbitrary",), (dh, xin, rstd, g, sc, dres, *gate_args))


def _final_loss(x2, gf, tgt, y, gt):
    s, d = x2.shape
    tr = _tile(s, ROW_TILE)
    row, vec, _ = _row_specs(s, d, tr)
    lrow = pl.BlockSpec((1, LANES), lambda i: (0, 0))

    def body(x_ref, g_ref, t_ref, y_ref, gt_ref, dx_ref, loss_ref, dg_ref, dy_ref, dgt_ref):
        @pl.when(pl.program_id(0) == 0)
        def _():
            loss_ref[...] = jnp.zeros_like(loss_ref)
            dg_ref[...] = jnp.zeros_like(dg_ref)
            dgt_ref[...] = jnp.zeros_like(dgt_ref)

        xv, gv = x_ref[...], g_ref[...]
        r = lax.rsqrt(jnp.mean(xv * xv, axis=-1, keepdims=True) + EPS)
        n0 = xv * r
        err = n0 * gv - t_ref[...]
        loss_ref[...] += jnp.sum(err * err) * (0.5 / d)
        dy = err * (1.0 / d)
        dg_ref[...] += jnp.sum(dy * n0, axis=0, keepdims=True)
        gy = dy * gv
        dot = jnp.mean(gy * xv, axis=-1, keepdims=True)
        dxv = r * gy - xv * (r * r * r * dot)
        dx_ref[...] = dxv
        dy_ref[...] = (gt_ref[...] * dxv).astype(BF16)
        dgt_ref[...] += jnp.sum(dxv * y_ref[...], axis=0, keepdims=True)

    return _rows_call(
        "final_loss", body, (s // tr,), [row, vec, row, row, vec], (row, lrow, vec, row, vec),
        (SDS((s, d), F32), SDS((1, LANES), F32), SDS((1, d), F32), SDS((s, d), BF16), SDS((1, d), F32)),
        ("arbitrary",), (x2, gf, tgt, y, gt))


def _mix_norm_fwd(oa, ol, ga, gl):
    s, w = oa.shape
    tr = _tile(s, ROW_TILE)
    row, vec, col = _row_specs(s, w, tr)

    def body(oa_ref, ol_ref, ga_ref, gl_ref, mx_ref, ra_ref, rl_ref):
        a, l = oa_ref[...], ol_ref[...]
        ra = lax.rsqrt(jnp.mean(a * a, axis=-1, keepdims=True) + EPS)
        rl = lax.rsqrt(jnp.mean(l * l, axis=-1, keepdims=True) + EPS)
        mx_ref[:, :w] = (a * ra * ga_ref[...]).astype(BF16)
        mx_ref[:, w:] = (l * rl * gl_ref[...]).astype(BF16)
        ra_ref[...] = ra
        rl_ref[...] = rl

    return _rows_call(
        "mix_norm_fwd", body, (s // tr,), [row, row, vec, vec], (row, col, col),
        (SDS((s, 2 * w), BF16), SDS((s, 1), F32), SDS((s, 1), F32)), ("parallel",), (oa, ol, ga, gl))


def _mix_norm_bwd(dmx, oa, ol, ra, rl, ga, gl):
    s, w = oa.shape
    tr = _tile(s, ROW_TILE)
    row, vec, col = _row_specs(s, w, tr)

    def body(dm_ref, oa_ref, ol_ref, ra_ref, rl_ref, ga_ref, gl_ref, doa_ref, dol_ref, dga_ref, dgl_ref):
        @pl.when(pl.program_id(0) == 0)
        def _():
            dga_ref[...] = jnp.zeros_like(dga_ref)
            dgl_ref[...] = jnp.zeros_like(dgl_ref)

        def one(dy, xv, r, gv, dg_ref):
            dg_ref[...] += jnp.sum(dy * (xv * r), axis=0, keepdims=True)
            gy = dy * gv
            dot = jnp.mean(gy * xv, axis=-1, keepdims=True)
            return r * gy - xv * (r * r * r * dot)

        dm = dm_ref[...].astype(F32)
        doa_ref[...] = one(dm[:, :w], oa_ref[...], ra_ref[...], ga_ref[...], dga_ref).astype(BF16)
        dol_ref[...] = one(dm[:, w:], ol_ref[...], rl_ref[...], gl_ref[...], dgl_ref)

    return _rows_call(
        "mix_norm_bwd", body, (s // tr,), [row, row, row, col, col, vec, vec], (row, row, vec, vec),
        (SDS((s, w), BF16), SDS((s, w), F32), SDS((1, w), F32), SDS((1, w), F32)), ("arbitrary",),
        (dmx, oa, ol, ra, rl, ga, gl))


def _attn_blocks(qs, ks, tri_after, csums, causal):
    zs = [_dot_nt(q, k) * (HEAD ** -0.5) for q, k in zip(qs, ks)]
    lbs = [_log_sigmoid(z) for z in zs]
    lss = [lb - z for lb, z in zip(lbs, zs)]
    if causal is not None:
        lss = [jnp.where(causal, ls, 0.0) for ls in lss]
    locs = [_split_dot(ls, tri_after) for ls in lss]
    ws = [jnp.exp(lb + (loc + cs)) for lb, loc, cs in zip(lbs, locs, csums)]
    if causal is not None:
        ws = [jnp.where(causal, w, 0.0) for w in ws]
    nxt = [cs + (loc[:, 0:1] + ls[:, 0:1]) for cs, loc, ls in zip(csums, locs, lss)]
    return lbs, ws, nxt


ATTN_HEADS_PER_STEP = 4


def _attn_tile(s):
    return 256 if s >= 1024 else 128


def _tri(t, after):
    r_i = lax.broadcasted_iota(jnp.int32, (t, t), 0)
    c_i = lax.broadcasted_iota(jnp.int32, (t, t), 1)
    return ((r_i > c_i) if after else (r_i < c_i)).astype(BF16)


def _attn_fwd(qkv, n_heads):
    s = qkv.shape[0]
    t = _attn_tile(s)
    hps = ATTN_HEADS_PER_STEP
    wid = hps * HEAD

    nq = s // t

    def body(q_ref, k_ref, v_ref, o_ref, w_ref, sg_ref):
        qi = pl.program_id(1)
        tri_after = _tri(t, True)
        causal = lax.broadcasted_iota(jnp.int32, (t, t), 1) < lax.broadcasted_iota(jnp.int32, (t, t), 0)
        lanes = [slice(a * HEAD, (a + 1) * HEAD) for a in range(hps)]
        qs = [q_ref[:, ln] for ln in lanes]

        def block(kb, carry, mask):
            off = pl.multiple_of(kb * t, t)
            ks = [k_ref[pl.ds(off, t), ln] for ln in lanes]
            lbs, ws, csums = _attn_blocks(qs, ks, tri_after, [cr[0] for cr in carry], mask)
            wbs = [w.astype(BF16) for w in ws]
            for a in range(hps):
                w_ref[a, kb] = wbs[a]
                sg_ref[a, kb] = jnp.exp(lbs[a]).astype(BF16)
            os_ = [cr[1] + _dot(wb, v_ref[pl.ds(off, t), ln]) for cr, wb, ln in zip(carry, wbs, lanes)]
            return tuple(zip(csums, os_))

        zero = tuple((jnp.zeros((t, 1), F32), jnp.zeros((t, HEAD), F32)) for _ in lanes)
        carry = block(qi, zero, causal)
        carry = lax.fori_loop(1, qi + 1, lambda it, cr: block(qi - it, cr, None), carry)
        for a, ln in enumerate(lanes):
            o_ref[:, ln] = carry[a][1]

    hb = n_heads // hps
    kept = pl.BlockSpec((None, hps, nq, t, t), lambda hh, i: (hh * nq + i, 0, 0, 0, 0))
    kept_shape = SDS((hb * nq, hps, nq, t, t), BF16)
    return pl.pallas_call(
        body, grid=(hb, nq),
        in_specs=[pl.BlockSpec((t, wid), lambda hh, i: (i, hh)),
                  pl.BlockSpec((s, wid), lambda hh, i: (0, hb + hh)),
                  pl.BlockSpec((s, wid), lambda hh, i: (0, 2 * hb + hh))],
        out_specs=(pl.BlockSpec((t, wid), lambda hh, i: (i, hh)), kept, kept),
        out_shape=(SDS((s, n_heads * HEAD), F32), kept_shape, kept_shape),
        compiler_params=_params(("parallel", "parallel")), name="attn_fwd")(qkv, qkv, qkv)


def _attn_bwd(qkv, do, w_kept, sg_kept, n_heads):
    s = qkv.shape[0]
    t = _attn_tile(s)
    nq = s // t
    scale = HEAD ** -0.5
    hps = ATTN_HEADS_PER_STEP
    wid = hps * HEAD

    def body(q_ref, k_ref, v_ref, do_ref, w_ref, sg_ref, dq_ref, dk_ref, dv_ref, dk_acc, dv_acc):
        qi = pl.program_id(1)

        @pl.when(qi == 0)
        def _():
            dk_acc[...] = jnp.zeros_like(dk_acc)
            dv_acc[...] = jnp.zeros_like(dv_acc)

        tri_before = _tri(t, False)
        causal = lax.broadcasted_iota(jnp.int32, (t, t), 1) < lax.broadcasted_iota(jnp.int32, (t, t), 0)
        lanes = [slice(a * HEAD, (a + 1) * HEAD) for a in range(hps)]
        qs = [q_ref[:, ln] for ln in lanes]
        douts = [do_ref[:, ln] for ln in lanes]

        def block(kb, carry, mask):
            off = pl.multiple_of(kb * t, t)
            wbs = [w_ref[a, kb] for a in range(hps)]
            dws = [_dot_nt(dout, v_ref[pl.ds(off, t), ln]) for dout, ln in zip(douts, lanes)]
            for a, ln in enumerate(lanes):
                dv_acc[pl.ds(off, t), ln] += _dot_tn(wbs[a], douts[a])
            es = [dw * wb.astype(F32) for dw, wb in zip(dws, wbs)]
            locs = [_split_dot(e, tri_before) for e in es]
            sgs = [sg_ref[a, kb].astype(F32) for a in range(hps)]
            stays = [(loc + cr[0]) * sg for loc, cr, sg in zip(locs, carry, sgs)]
            if mask is not None:
                stays = [jnp.where(mask, st, 0.0) for st in stays]
            dzbs = [((e * (1.0 - sg) - st) * scale).astype(BF16) for e, sg, st in zip(es, sgs, stays)]
            dqs = [cr[1] + _dot(dzb, k_ref[pl.ds(off, t), ln]) for cr, dzb, ln in zip(carry, dzbs, lanes)]
            for a, ln in enumerate(lanes):
                dk_acc[pl.ds(off, t), ln] += _dot_tn(dzbs[a], qs[a])
            esums = [cr[0] + (loc[:, t - 1:t] + e[:, t - 1:t]) for cr, loc, e in zip(carry, locs, es)]
            return tuple(zip(esums, dqs))

        zero = tuple((jnp.zeros((t, 1), F32), jnp.zeros((t, HEAD), F32)) for _ in lanes)
        carry = lax.fori_loop(0, qi, lambda kb, cr: block(kb, cr, None), zero)
        carry = block(qi, carry, causal)
        for a, ln in enumerate(lanes):
            dq_ref[:, ln] = carry[a][1].astype(BF16)

        @pl.when(qi == nq - 1)
        def _():
            dk_ref[...] = dk_acc[...].astype(BF16)
            dv_ref[...] = dv_acc[...].astype(BF16)

    hb = n_heads // hps
    blk = pl.BlockSpec((t, wid), lambda hh, i: (i, hh))
    full = pl.BlockSpec((s, wid), lambda hh, i: (0, hh))
    kept = pl.BlockSpec((None, hps, nq, t, t), lambda hh, i: (hh * nq + i, 0, 0, 0, 0))
    return pl.pallas_call(
        body, grid=(hb, nq),
        in_specs=[blk,
                  pl.BlockSpec((s, wid), lambda hh, i: (0, hb + hh)),
                  pl.BlockSpec((s, wid), lambda hh, i: (0, 2 * hb + hh)),
                  blk, kept, kept],
        out_specs=(blk, full, full),
        out_shape=(SDS((s, n_heads * HEAD), BF16),) * 3,
        scratch_shapes=[pltpu.VMEM((s, wid), F32), pltpu.VMEM((s, wid), F32)],
        compiler_params=_params(("parallel", "arbitrary")), name="attn_bwd")(qkv, qkv, qkv, do, w_kept, sg_kept)


def _lru_chunk(s):
    return 256 if s >= 1024 else 128


def _lru_gates(xc, wa, ba, wx, bx, sp):
    xb = xc.astype(BF16)
    r = _sigmoid(_dot(xb, wa) + ba)
    ig = _sigmoid(_dot(xb, wx) + bx)
    la = -LRU_C * r * sp
    a = jnp.exp(la)
    t = jnp.tanh(la)
    mult = jnp.sqrt(-2.0 * t / (1.0 - t))
    return r, ig, a, mult


def _softplus_neg(lam):
    return jnp.maximum(-lam, 0.0) + jnp.log(1.0 + jnp.exp(-jnp.abs(lam)))


LRU_BLOCKS_PER_STEP = 1


def _lru_specs(s, n_blocks):
    bps = min(LRU_BLOCKS_PER_STEP, n_blocks)
    wid = bps * HEAD
    seq0 = pl.BlockSpec((s, wid), lambda h: (0, h))
    seq1 = pl.BlockSpec((s, wid), lambda h: (0, n_blocks // bps + h))
    taps = pl.BlockSpec((CONV_TAPS, wid), lambda h: (0, h))
    vec = pl.BlockSpec((1, wid), lambda h: (0, h))
    mat = pl.BlockSpec((bps, HEAD, HEAD), lambda h: (h, 0, 0))
    return bps, seq0, seq1, taps, vec, mat


def _per_block(one_block, n_2d, n_mat_pos, bps):
    def body(*refs):
        for a in range(bps):
            views = [r.at[a] if i in n_mat_pos else r.at[:, pl.ds(a * HEAD, HEAD)] for i, r in enumerate(refs[:n_2d])]
            one_block(*views, *refs[n_2d:])
    return body


def _lru_fwd(xrg, wconv, bconv, wa, ba, wx, bx, lam):
    s = xrg.shape[0]
    nb = wa.shape[0]
    tc = _lru_chunk(s)
    bps, seq0, seq1, taps, vec, mat = _lru_specs(s, nb)
    pad = SUBLANES

    def one_block(xr_ref, xg_ref, wc_ref, bc_ref, wa_ref, ba_ref, wx_ref, bx_ref, lam_ref, o_ref, h_ref, pad_s, a_s, u_s):
        pad_s[0:pad, :] = jnp.zeros((pad, HEAD), F32)
        pad_s[pad:pad + s, :] = xr_ref[...]
        wab, wxb = wa_ref[...].astype(BF16), wx_ref[...].astype(BF16)
        sp = _softplus_neg(lam_ref[...])
        for c in range(s // tc):
            base = c * tc
            xc = bc_ref[...] + sum(wc_ref[i:i + 1, :] * pad_s[pl.ds(base + pad - (CONV_TAPS - 1) + i, tc), :]
                                   for i in range(CONV_TAPS))
            _, ig, a, mult = _lru_gates(xc, wab, ba_ref[...], wxb, bx_ref[...], sp)
            a_s[base:base + tc, :] = a
            u_s[base:base + tc, :] = mult * (ig * xc)

        row = lax.broadcasted_iota(jnp.int32, (SUBLANES, HEAD), 0)

        def chunk(ci, hprev):
            off = pl.multiple_of(ci * SUBLANES, SUBLANES)
            a8, b8 = a_s[pl.ds(off, SUBLANES), :], u_s[pl.ds(off, SUBLANES), :]
            for d in (1, 2, 4):
                a_sh = jnp.where(row < d, 1.0, pltpu.roll(a8, d, 0))
                b_sh = jnp.where(row < d, 0.0, pltpu.roll(b8, d, 0))
                b8 = a8 * b_sh + b8
                a8 = a8 * a_sh
            h8 = a8 * hprev + b8
            h_ref[pl.ds(off, SUBLANES), :] = h8
            return h8[SUBLANES - 1:SUBLANES, :]

        lax.fori_loop(0, s // SUBLANES, chunk, jnp.zeros((1, HEAD), F32), unroll=8)
        for c in range(s // tc):
            sl = slice(c * tc, (c + 1) * tc)
            gel, _ = _gelu_parts(xg_ref[sl, :])
            o_ref[sl, :] = h_ref[sl, :] * gel

    return pl.pallas_call(
        _per_block(one_block, 11, (4, 6), bps), grid=(nb // bps,),
        in_specs=[seq0, seq1, taps, vec, mat, vec, mat, vec, vec],
        out_specs=(seq0, seq0),
        out_shape=(SDS((s, nb * HEAD), F32), SDS((s, nb * HEAD), F32)),
        scratch_shapes=[pltpu.VMEM((s + pad, HEAD), F32), pltpu.VMEM((s, HEAD), F32), pltpu.VMEM((s, HEAD), F32)],
        compiler_params=_params(("parallel",)), name="lru_fwd")(xrg, xrg, wconv, bconv, wa, ba, wx, bx, lam)


def _lru_bwd(xrg, dol, hseq, wconv, bconv, wa, ba, wx, bx, lam):
    s = xrg.shape[0]
    nb = wa.shape[0]
    tc = _lru_chunk(s)
    bps, seq0, seq1, taps, vec, mat = _lru_specs(s, nb)
    pad = SUBLANES

    def one_block(xr_ref, xg_ref, do_ref, h_ref, wc_ref, bc_ref, wa_ref, ba_ref, wx_ref, bx_ref, lam_ref,
             dxr_ref, dxg_ref, dwc_ref, dbc_ref, dwa_ref, dba_ref, dwx_ref, dbx_ref, dlam_ref,
             pad_s, hp_s, a_s, g_s, da_s, dxc_s):
        pad_s[0:pad, :] = jnp.zeros((pad, HEAD), F32)
        pad_s[pad:pad + s, :] = xr_ref[...]
        hp_s[0:pad, :] = jnp.zeros((pad, HEAD), F32)
        hp_s[pad:pad + s, :] = h_ref[...]
        a_s[s:s + pad, :] = jnp.zeros((pad, HEAD), F32)
        dxc_s[s:s + pad, :] = jnp.zeros((pad, HEAD), F32)
        wab, wxb = wa_ref[...].astype(BF16), wx_ref[...].astype(BF16)
        lam_v = lam_ref[...]
        sp = _softplus_neg(lam_v)

        def conv_in(c):
            base = c * tc
            wins = [pad_s[pl.ds(base + pad - (CONV_TAPS - 1) + i, tc), :] for i in range(CONV_TAPS)]
            xc = bc_ref[...] + sum(wc_ref[i:i + 1, :] * wins[i] for i in range(CONV_TAPS))
            return xc, wins

        for c in range(s // tc):
            sl = slice(c * tc, (c + 1) * tc)
            xc, _ = conv_in(c)
            _, _, a, _ = _lru_gates(xc, wab, ba_ref[...], wxb, bx_ref[...], sp)
            a_s[sl, :] = a
            gel, dgel = _gelu_parts(xg_ref[sl, :])
            dov = do_ref[sl, :]
            g_s[sl, :] = dov * gel
            dxg_ref[sl, :] = (dov * h_ref[sl, :] * dgel).astype(BF16)

        row = lax.broadcasted_iota(jnp.int32, (SUBLANES, HEAD), 0)
        n_chunks = s // SUBLANES

        def chunk(it, gnext):
            ci = n_chunks - 1 - it
            off = pl.multiple_of(ci * SUBLANES, SUBLANES)
            a8 = a_s[pl.ds(off, SUBLANES), :]
            a8n = a_s[pl.ds(off + SUBLANES, SUBLANES), :]
            c8 = pltpu.roll(jnp.where(row == 0, a8n, a8), SUBLANES - 1, 0)
            g8 = g_s[pl.ds(off, SUBLANES), :]
            for d in (1, 2, 4):
                c_sh = jnp.where(row >= SUBLANES - d, 1.0, pltpu.roll(c8, SUBLANES - d, 0))
                g_sh = jnp.where(row >= SUBLANES - d, 0.0, pltpu.roll(g8, SUBLANES - d, 0))
                g8 = c8 * g_sh + g8
                c8 = c8 * c_sh
            g8 = g8 + c8 * gnext
            g_s[pl.ds(off, SUBLANES), :] = g8
            h8 = hp_s[pl.ds(off + pad, SUBLANES), :]
            h8p = hp_s[pl.ds(off, SUBLANES), :]
            da_s[pl.ds(off, SUBLANES), :] = g8 * pltpu.roll(jnp.where(row == SUBLANES - 1, h8p, h8), 1, 0)
            return g8[0:1, :]

        lax.fori_loop(0, n_chunks, chunk, jnp.zeros((1, HEAD), F32), unroll=8)

        dsp = jnp.zeros((1, HEAD), F32)
        dbc = jnp.zeros((1, HEAD), F32)
        dba = jnp.zeros((1, HEAD), F32)
        dbx = jnp.zeros((1, HEAD), F32)
        dwa = jnp.zeros((HEAD, HEAD), F32)
        dwx = jnp.zeros((HEAD, HEAD), F32)
        dwc = [jnp.zeros((1, HEAD), F32) for _ in range(CONV_TAPS)]
        for c in range(s // tc):
            sl = slice(c * tc, (c + 1) * tc)
            xc, wins = conv_in(c)
            r, ig, a, mult = _lru_gates(xc, wab, ba_ref[...], wxb, bx_ref[...], sp)
            du, da = g_s[sl, :], da_s[sl, :]
            d_ix = du * mult
            dla = da * a - (du * ig * xc) * (a * a / mult)
            dsp = dsp + jnp.sum(dla * r, axis=0, keepdims=True) * (-LRU_C)
            dpa = (dla * (-LRU_C * sp)) * r * (1.0 - r)
            dpx = (d_ix * xc) * ig * (1.0 - ig)
            dpab, dpxb, xb = dpa.astype(BF16), dpx.astype(BF16), xc.astype(BF16)
            dxc = d_ix * ig + _dot_nt(dpab, wab) + _dot_nt(dpxb, wxb)
            dwa = dwa + _dot_tn(xb, dpab)
            dwx = dwx + _dot_tn(xb, dpxb)
            dba = dba + jnp.sum(dpa, axis=0, keepdims=True)
            dbx = dbx + jnp.sum(dpx, axis=0, keepdims=True)
            dbc = dbc + jnp.sum(dxc, axis=0, keepdims=True)
            for i in range(CONV_TAPS):
                dwc[i] = dwc[i] + jnp.sum(dxc * wins[i], axis=0, keepdims=True)
            dxc_s[sl, :] = dxc

        for c in range(s // tc):
            base = c * tc
            dxr = sum(wc_ref[i:i + 1, :] * dxc_s[pl.ds(base + (CONV_TAPS - 1) - i, tc), :] for i in range(CONV_TAPS))
            dxr_ref[base:base + tc, :] = dxr.astype(BF16)

        for i in range(CONV_TAPS):
            dwc_ref[i:i + 1, :] = dwc[i]
        dbc_ref[...] = dbc
        dwa_ref[...] = dwa
        dwx_ref[...] = dwx
        dba_ref[...] = dba
        dbx_ref[...] = dbx
        dlam_ref[...] = dsp * (-_sigmoid(-lam_v))

    w = nb * HEAD
    return pl.pallas_call(
        _per_block(one_block, 20, (6, 8, 15, 17), bps), grid=(nb // bps,),
        in_specs=[seq0, seq1, seq0, seq0, taps, vec, mat, vec, mat, vec, vec],
        out_specs=(seq0, seq0, taps, vec, mat, vec, mat, vec, vec),
        out_shape=(SDS((s, w), BF16), SDS((s, w), BF16), SDS((CONV_TAPS, w), F32), SDS((1, w), F32),
                   SDS((nb, HEAD, HEAD), F32), SDS((1, w), F32), SDS((nb, HEAD, HEAD), F32), SDS((1, w), F32),
                   SDS((1, w), F32)),
        scratch_shapes=[pltpu.VMEM((s + pad, HEAD), F32), pltpu.VMEM((s + pad, HEAD), F32),
                        pltpu.VMEM((s + pad, HEAD), F32), pltpu.VMEM((s, HEAD), F32),
                        pltpu.VMEM((s, HEAD), F32), pltpu.VMEM((s + pad, HEAD), F32)],
        compiler_params=_params(("parallel",)), name="lru_bwd",
    )(xrg, xrg, dol, hseq, wconv, bconv, wa, ba, wx, bx, lam)


def _ada_mod(c_all, w_sh, b_sh):
    n_ex, d = c_all.shape
    n = w_sh.shape[1]
    tn = _tile(n, 512)

    def body(c_ref, w_ref, b_ref, mod_ref, act_ref):
        cv = c_ref[...]
        act = cv * _sigmoid(cv)
        act_ref[...] = act
        mod_ref[...] = _dot(act.astype(BF16), w_ref[...].astype(BF16)) + b_ref[...]

    return pl.pallas_call(
        body, grid=(n // tn,),
        in_specs=[pl.BlockSpec((n_ex, d), lambda j: (0, 0)), pl.BlockSpec((d, tn), lambda j: (0, j)),
                  pl.BlockSpec((1, tn), lambda j: (0, j))],
        out_specs=(pl.BlockSpec((n_ex, tn), lambda j: (0, j)), pl.BlockSpec((n_ex, d), lambda j: (0, 0))),
        out_shape=(SDS((n_ex, n), F32), SDS((n_ex, d), F32)),
        compiler_params=_params(("arbitrary",)), name="ada_mod")(c_all, w_sh, b_sh)


def _adamw_math(w, g, m, v):
    m = ADAM_B1 * m + (1.0 - ADAM_B1) * g
    v = ADAM_B2 * v + (1.0 - ADAM_B2) * (g * g)
    m_hat = m / (1.0 - ADAM_B1 ** ADAM_STEP)
    v_hat = v / (1.0 - ADAM_B2 ** ADAM_STEP)
    delta = -ADAM_LR * (m_hat / (jnp.sqrt(v_hat) + ADAM_EPS) + ADAM_WD * w)
    return delta, m, v


def _adamw_plain(name, w, g, m, v):
    def body(w_ref, g_ref, m_ref, v_ref, d_ref, mo_ref, vo_ref):
        d_ref[...], mo_ref[...], vo_ref[...] = _adamw_math(w_ref[...], g_ref[...], m_ref[...], v_ref[...])

    return pl.pallas_call(body, out_shape=(SDS(w.shape, F32),) * 3, name=name)(w, g, m, v)


def _adamw_halves(name, c_arr, w, m, v, g_own, g_recv):
    r, n = w.shape
    rh = r // 2
    tr = _tile(rh, 256)
    nh = rh // tr

    def body(c_ref, w_ref, m_ref, v_ref, go_ref, gr_ref, g_ref, d_ref, mo_ref, vo_ref):
        own = (pl.program_id(0) // nh) == c_ref[0]
        g = jnp.where(own, go_ref[...], gr_ref[...])
        g_ref[...] = g
        d_ref[...], mo_ref[...], vo_ref[...] = _adamw_math(w_ref[...], g, m_ref[...], v_ref[...])

    full = pl.BlockSpec((tr, n), lambda i, c: (i, 0))
    own = pl.BlockSpec((tr, n), lambda i, c: (jnp.where(i // nh == c[0], i % nh, 0), 0))
    recv = pl.BlockSpec((tr, n), lambda i, c: (jnp.where(i // nh == c[0], 0, i % nh), 0))
    return pl.pallas_call(
        body,
        grid_spec=pltpu.PrefetchScalarGridSpec(
            num_scalar_prefetch=1, grid=(2 * nh,), in_specs=[full, full, full, own, recv],
            out_specs=(full,) * 4),
        out_shape=(SDS((r, n), F32),) * 4,
        compiler_params=_params(("parallel",)), name=name)(c_arr, w, m, v, g_own, g_recv)


def _adamw_ada(w, m, v, act_t, dmod):
    d, n = w.shape
    n_ex = act_t.shape[1]
    tr = _tile(d, 256)

    def body(a_ref, dm_ref, w_ref, m_ref, v_ref, g_ref, d_ref, mo_ref, vo_ref):
        g = _dot(a_ref[...], dm_ref[...])
        g_ref[...] = g
        d_ref[...], mo_ref[...], vo_ref[...] = _adamw_math(w_ref[...], g, m_ref[...], v_ref[...])

    full = pl.BlockSpec((tr, n), lambda i: (i, 0))
    return pl.pallas_call(
        body, grid=(d // tr,),
        in_specs=[pl.BlockSpec((tr, n_ex), lambda i: (i, 0)), pl.BlockSpec((n_ex, n), lambda i: (0, 0)), full, full, full],
        out_specs=(full,) * 4, out_shape=(SDS((d, n), F32),) * 4,
        compiler_params=_params(("parallel",)), name="adamw_ada")(act_t, dmod, w, m, v)


def _small_reduce_adamw(parts, w, m, v):
    n_dev, r, _ = parts.shape
    tr = r if r <= PACK_ROWS else PACK_ROWS

    def body(p_ref, w_ref, m_ref, v_ref, g_ref, d_ref, mo_ref, vo_ref):
        g = p_ref[0]
        for k in range(1, n_dev):
            g = g + p_ref[k]
        g_ref[...] = g
        d_ref[...], mo_ref[...], vo_ref[...] = _adamw_math(w_ref[...], g, m_ref[...], v_ref[...])

    full = pl.BlockSpec((tr, LANES), lambda i: (i, 0))
    return pl.pallas_call(
        body, grid=(r // tr,),
        in_specs=[pl.BlockSpec((n_dev, tr, LANES), lambda i: (0, i, 0)), full, full, full],
        out_specs=(full,) * 4, out_shape=(SDS((r, LANES), F32),) * 4,
        compiler_params=_params(("parallel",)), name="small_reduce_adamw")(parts, w, m, v)


def _mesh_pos():
    return lax.axis_index("x"), lax.axis_index("y"), lax.axis_index("c")


def _other_chips(x, y):
    return [(1 - x, y), (x, 1 - y), (1 - x, 1 - y)]


def _all_gather_small(name, blk, after=()):
    r, n = blk.shape
    n_after = len(after)

    def body(x_ref, *rest):
        out_ref, send_sems, recv_sems, local_sem = rest[n_after:]
        x, y, c = _mesh_pos()
        me, sibling = (x, y, c), (x, y, 1 - c)
        chips = _other_chips(x, y)

        def rows(px, py, pc):
            return out_ref.at[4 * px + 2 * py + pc]

        def copy(k, block, to, src=None):
            return pltpu.make_async_remote_copy(
                src_ref=rows(*block) if src is None else src, dst_ref=rows(*block),
                send_sem=send_sems.at[k], recv_sem=recv_sems.at[k], device_id=to, device_id_type=MESH)

        mine = pltpu.make_async_copy(x_ref, rows(*me), local_sem)
        mine.start()
        first = [copy(0, me, sibling, src=x_ref)]
        first += [copy(1 + j, me, (*chip, c), src=x_ref) for j, chip in enumerate(chips)]
        for cp in first:
            cp.start()
        passed = [copy(4 + j, (*chip, c), sibling) for j, chip in enumerate(chips)]
        for j, chip in enumerate(chips):
            copy(1 + j, (*chip, c), me).wait_recv()
            passed[j].start()
        copy(0, sibling, me).wait_recv()
        for j, chip in enumerate(chips):
            copy(4 + j, (*chip, 1 - c), me).wait_recv()
        for cp in first + passed:
            cp.wait_send()
        mine.wait()

    return pl.pallas_call(
        body, out_shape=SDS((N_DEV, r, n), blk.dtype),
        in_specs=[pl.BlockSpec(memory_space=pltpu.VMEM)] + [pl.BlockSpec(memory_space=pl.ANY)] * n_after,
        out_specs=pl.BlockSpec(memory_space=pltpu.VMEM),
        scratch_shapes=[pltpu.SemaphoreType.DMA((7,)), pltpu.SemaphoreType.DMA((7,)), pltpu.SemaphoreType.DMA],
        compiler_params=pltpu.CompilerParams(vmem_limit_bytes=VMEM_LIMIT), name=name)(blk, *after)


_ANY = pl.BlockSpec(memory_space=pl.ANY)
_HBM = pl.BlockSpec(memory_space=pltpu.HBM)
_SEM = pl.BlockSpec(memory_space=pltpu.SEMAPHORE)
_EFFECT = pltpu.SideEffectType.DATAFLOW_SIDE_EFFECTING


def _hbm(a):
    return pltpu.with_memory_space_constraint(a, pltpu.HBM)


def _place_cast(name, j_arr, shard, kind, after):
    r, n = shard.shape
    tr = _tile(r, 256)
    nr = r // tr
    if kind == "col":
        out_shape, o_spec = (r, N_CHIP * n), pl.BlockSpec((tr, n), lambda i, j: (i, j[0]))
    else:
        out_shape, o_spec = (N_CHIP * r, n), pl.BlockSpec((tr, n), lambda i, j: (j[0] * nr + i, 0))

    def body(j_ref, s_ref, after_ref, o_ref, tok_ref):
        o_ref[...] = s_ref[...].astype(BF16)
        tok_ref[...] = jnp.zeros_like(tok_ref)

    return pl.pallas_call(
        body,
        grid_spec=pltpu.PrefetchScalarGridSpec(
            num_scalar_prefetch=1, grid=(nr,), in_specs=[pl.BlockSpec((tr, n), lambda i, j: (i, 0)), _ANY],
            out_specs=(o_spec, pl.BlockSpec((SUBLANES, LANES), lambda i, j: (0, 0)))),
        out_shape=(SDS(out_shape, BF16), SDS((SUBLANES, LANES), F32)),
        compiler_params=_params(("arbitrary",)), name=name)(j_arr, shard, after)


def _leg_direct(full, kind, x, y, c):
    mine = _full_region(full, kind, x, y, c)
    return [(mine, mine, (1 - x, y, c)), (mine, mine, (x, 1 - y, c))]


def _leg_relay(full, kind, x, y, c):
    fx, fy = jnp.where(c == 0, 1 - x, x), jnp.where(c == 0, y, 1 - y)
    tx, ty = jnp.where(c == 0, x, 1 - x), jnp.where(c == 0, 1 - y, y)
    got = _full_region(full, kind, fx, fy, c)
    return [(got, got, (tx, ty, c))]


def _leg_d2d(which):
    def leg(full, kind, x, y, c):
        chips = _other_chips(x, y)
        return [(_full_region(full, kind, *chips[k], c), _full_region(full, kind, *chips[k], c), (x, y, 1 - c))
                for k in which]
    return leg


_LEGS = {"direct": (_leg_direct, 2), "relay": (_leg_relay, 1), "d2d_near": (_leg_d2d((0, 1)), 2),
         "d2d_far": (_leg_d2d((2,)), 1)}


def _gather_call(name, fulls, kinds, waits, starts, after, thru):
    nw, n_wait, n_start = len(fulls), len(waits), len(starts)

    def body(*refs):
        wait_sems = refs[nw:nw + 2 * n_wait]
        outs = refs[nw + 2 * n_wait + 2:]
        full, start_sems = outs[:nw], outs[nw:nw + 2 * n_start]
        x, y, c = _mesh_pos()
        for i, (leg, ws, _, _) in enumerate(waits):
            fn, per = _LEGS[leg]
            for li, w in enumerate(ws):
                for k, (s_, d_, dev) in enumerate(fn(full[w], kinds[w], x, y, c)):
                    cp = pltpu.make_async_remote_copy(
                        src_ref=s_, dst_ref=d_, send_sem=wait_sems[2 * i].at[per * li + k],
                        recv_sem=wait_sems[2 * i + 1].at[per * li + k], device_id=dev, device_id_type=MESH)
                    cp.wait_recv()
                    cp.wait_send()
        for i, (leg, ws) in enumerate(starts):
            fn, per = _LEGS[leg]
            for li, w in enumerate(ws):
                for k, (s_, d_, dev) in enumerate(fn(full[w], kinds[w], x, y, c)):
                    pltpu.make_async_remote_copy(
                        src_ref=s_, dst_ref=d_, send_sem=start_sems[2 * i].at[per * li + k],
                        recv_sem=start_sems[2 * i + 1].at[per * li + k], device_id=dev, device_id_type=MESH).start()

    sems = []
    for leg, ws in starts:
        sems += [pltpu.SemaphoreType.DMA((_LEGS[leg][1] * len(ws),))] * 2
    wait_args = []
    for _, _, s_, r_ in waits:
        wait_args += [s_, r_]
    outs = pl.pallas_call(
        body,
        out_shape=tuple(pltpu.HBM(f_.shape, f_.dtype) for f_ in fulls) + tuple(sems) + (SDS(thru.shape, thru.dtype),),
        in_specs=[_HBM] * nw + [_SEM] * (2 * n_wait) + [_ANY, _ANY],
        out_specs=tuple([_HBM] * nw + [_SEM] * (2 * n_start) + [_ANY]),
        input_output_aliases={**{w: w for w in range(nw)}, nw + 2 * n_wait + 1: nw + 2 * n_start},
        compiler_params=pltpu.CompilerParams(has_side_effects=_EFFECT),
        name=name,
    )(*[_hbm(f_) for f_ in fulls], *wait_args, after, thru)
    pairs = [(outs[nw + 2 * i], outs[nw + 2 * i + 1]) for i in range(n_start)]
    return list(outs[:nw]), pairs, outs[nw + 2 * n_start]


def _full_region(full, kind, px, py, half):
    j = 2 * px + py
    if kind == "col":
        rh, cols = full.shape[0] // 2, full.shape[1] // N_CHIP
        return full.at[pl.ds(half * rh, rh), pl.ds(j * cols, cols)]
    rows = full.shape[0] // N_CHIP
    rh = rows // 2
    return full.at[pl.ds(j * rows + half * rh, rh), :]


def _plan_scatter(kind):
    def plan(src, land, x, y, c):
        out = []
        for k, (px, py) in enumerate(_other_chips(x, y)):
            j = 2 * px + py
            if kind == "col":
                n = src.shape[1] // N_CHIP
                blk = src.at[:, pl.ds(j * n, n)]
            else:
                blk = src.at[j]
            out.append((blk, land.at[k], (px, py, c)))
        return out
    return plan


def _plan_whole(src, land, x, y, c):
    return [(src, land, (x, y, 1 - c))]


def _split_start(name, src, land_shape, n, plan, thru):
    def body(src_in, land_in, thru_in, send, recv, src_ref, land_ref, thru_out):
        x, y, c = _mesh_pos()
        for k, (s_, d_, dev) in enumerate(plan(src_ref, land_ref, x, y, c)):
            pltpu.make_async_remote_copy(src_ref=s_, dst_ref=d_, send_sem=send.at[k], recv_sem=recv.at[k],
                                         device_id=dev, device_id_type=MESH).start()

    sem = pltpu.SemaphoreType.DMA((n,))
    return pl.pallas_call(
        body,
        out_shape=(sem, sem, pltpu.HBM(src.shape, src.dtype), pltpu.HBM(land_shape, src.dtype), SDS(thru.shape, thru.dtype)),
        in_specs=[_HBM, _HBM, _ANY], out_specs=(_SEM, _SEM, _HBM, _HBM, _ANY),
        input_output_aliases={0: 2, 1: 3, 2: 4},
        compiler_params=pltpu.CompilerParams(has_side_effects=_EFFECT), name=name,
    )(_hbm(src), _hbm(lax.empty(land_shape, src.dtype)), thru)


def _split_wait(name, send, recv, src, land, plan, after):
    def body(src_in, land_in, send_r, recv_r, after_r, src_ref, land_ref):
        x, y, c = _mesh_pos()
        for k, (s_, d_, dev) in enumerate(plan(src_ref, land_ref, x, y, c)):
            cp = pltpu.make_async_remote_copy(src_ref=s_, dst_ref=d_, send_sem=send_r.at[k], recv_sem=recv_r.at[k],
                                              device_id=dev, device_id_type=MESH)
            cp.wait_send()
            cp.wait_recv()

    return pl.pallas_call(
        body,
        out_shape=(pltpu.HBM(src.shape, src.dtype), pltpu.HBM(land.shape, land.dtype)),
        in_specs=[_HBM, _HBM, _SEM, _SEM, _ANY], out_specs=(_HBM, _HBM),
        input_output_aliases={0: 0, 1: 1},
        compiler_params=pltpu.CompilerParams(has_side_effects=_EFFECT), name=name,
    )(src, land, send, recv, after)


def _dev_row(buf, px, py, pc):
    return buf.at[4 * px + 2 * py + pc]


def _plan_gather_own(buf, land, x, y, c):
    own = _dev_row(buf, x, y, c)
    return [(own, own, (x, y, 1 - c))] + [(own, own, (px, py, c)) for px, py in _other_chips(x, y)]


def _plan_gather_pass(buf, land, x, y, c):
    return [(_dev_row(buf, px, py, c), _dev_row(buf, px, py, c), (x, y, 1 - c)) for px, py in _other_chips(x, y)]


def _split_start_inplace(name, buf, n, plan, thru):
    def body(buf_in, thru_in, send, recv, buf_ref, thru_out):
        x, y, c = _mesh_pos()
        for k, (s_, d_, dev) in enumerate(plan(buf_ref, buf_ref, x, y, c)):
            pltpu.make_async_remote_copy(src_ref=s_, dst_ref=d_, send_sem=send.at[k], recv_sem=recv.at[k],
                                         device_id=dev, device_id_type=MESH).start()

    sem = pltpu.SemaphoreType.DMA((n,))
    return pl.pallas_call(
        body, out_shape=(sem, sem, pltpu.HBM(buf.shape, buf.dtype), SDS(thru.shape, thru.dtype)),
        in_specs=[_HBM, _ANY], out_specs=(_SEM, _SEM, _HBM, _ANY), input_output_aliases={0: 2, 1: 3},
        compiler_params=pltpu.CompilerParams(has_side_effects=_EFFECT), name=name)(_hbm(buf), thru)


def _split_wait_inplace(name, send, recv, buf, plan, after):
    def body(buf_in, send_r, recv_r, after_r, buf_ref):
        x, y, c = _mesh_pos()
        for k, (s_, d_, dev) in enumerate(plan(buf_ref, buf_ref, x, y, c)):
            cp = pltpu.make_async_remote_copy(src_ref=s_, dst_ref=d_, send_sem=send_r.at[k], recv_sem=recv_r.at[k],
                                              device_id=dev, device_id_type=MESH)
            cp.wait_send()
            cp.wait_recv()

    return pl.pallas_call(
        body, out_shape=pltpu.HBM(buf.shape, buf.dtype), in_specs=[_HBM, _SEM, _SEM, _ANY], out_specs=_HBM,
        input_output_aliases={0: 0}, compiler_params=pltpu.CompilerParams(has_side_effects=_EFFECT), name=name,
    )(buf, send, recv, after)


def _place_row(name, me_arr, slab):
    r, n = slab.shape
    tr = r if r <= PACK_ROWS else PACK_ROWS

    def body(me_ref, s_ref, o_ref):
        o_ref[...] = s_ref[...]

    return pl.pallas_call(
        body,
        grid_spec=pltpu.PrefetchScalarGridSpec(
            num_scalar_prefetch=1, grid=(r // tr,), in_specs=[pl.BlockSpec((tr, n), lambda i, me: (i, 0))],
            out_specs=pl.BlockSpec((None, tr, n), lambda i, me: (me[0], i, 0))),
        out_shape=SDS((N_DEV, r, n), slab.dtype), compiler_params=_params(("parallel",)), name=name)(me_arr, slab)


def _sum_partials(name, j_arr, part, got, kind):
    _, rh, n = got.shape
    tr = _tile(rh, 256)
    if kind == "col":
        p_spec = pl.BlockSpec((tr, n), lambda i, j: (i, j[0]))
    else:
        p_spec = pl.BlockSpec((None, tr, n), lambda i, j: (j[0], i, 0))

    def body(j_ref, p_ref, r_ref, o_ref):
        o_ref[...] = ((p_ref[...].astype(F32) + r_ref[0].astype(F32)) + r_ref[1].astype(F32)) + r_ref[2].astype(F32)

    return pl.pallas_call(
        body,
        grid_spec=pltpu.PrefetchScalarGridSpec(
            num_scalar_prefetch=1, grid=(rh // tr,),
            in_specs=[p_spec, pl.BlockSpec((3, tr, n), lambda i, j: (0, i, 0))],
            out_specs=pl.BlockSpec((tr, n), lambda i, j: (i, 0))),
        out_shape=SDS((rh, n), F32),
        compiler_params=_params(("parallel",)), name=name)(j_arr, part, got)


def _swap_reduced(name, halves):
    nw = len(halves)

    def body(*refs):
        h, got = refs[:nw], refs[nw:2 * nw]
        send_sems, recv_sems = refs[2 * nw:]
        x, y, c = _mesh_pos()
        cps = []
        for w in range(nw):
            cp = pltpu.make_async_remote_copy(
                src_ref=h[w], dst_ref=got[w], send_sem=send_sems.at[w], recv_sem=recv_sems.at[w],
                device_id=(x, y, 1 - c), device_id_type=MESH)
            cp.start()
            cps.append(cp)
        for cp in cps:
            cp.wait()

    return pl.pallas_call(
        body, out_shape=tuple(SDS(h.shape, h.dtype) for h in halves),
        in_specs=[_ANY] * nw, out_specs=tuple([_ANY] * nw),
        scratch_shapes=[pltpu.SemaphoreType.DMA((nw,)), pltpu.SemaphoreType.DMA((nw,))],
        name=name)(*halves)


def _pack(arrays):
    flat = [a.reshape(-1).astype(F32) for a in arrays]
    flat = [jnp.pad(f, (0, (-f.shape[0]) % LANES)) for f in flat]
    sizes = [f.shape[0] for f in flat]
    total = sum(sizes)
    rows = total // LANES
    tail = LANES * ((-rows) % (PACK_ROWS if rows > PACK_ROWS else SUBLANES))
    if tail:
        flat.append(jnp.zeros((tail,), F32))
    return jnp.concatenate(flat).reshape(-1, LANES), sizes


def _unpack(slab, sizes, shapes, lead=()):
    flat = slab.reshape(lead + (-1,))
    out, off = [], 0
    for sz, shp in zip(sizes, shapes):
        n = math.prod(shp)
        out.append(flat[..., off:off + n].reshape(lead + tuple(shp)))
        off += sz
    return out


def kernel(x, c, w_ada, b_ada, g_norm_mix, w_in, w_conv, b_conv, w_rg_a, b_rg_a, w_rg_x, b_rg_x, lru_lambda, g_attn_out, g_lru_out, w_out, g_norm_mlp, w_mlp_in, w_mlp_out, g_norm_final, loss_target, m_w_ada, m_b_ada, m_g_norm_mix, m_w_in, m_w_conv, m_b_conv, m_w_rg_a, m_b_rg_a, m_w_rg_x, m_b_rg_x, m_lru_lambda, m_g_attn_out, m_g_lru_out, m_w_out, m_g_norm_mlp, m_w_mlp_in, m_w_mlp_out, m_g_norm_final, v_w_ada, v_b_ada, v_g_norm_mix, v_w_in, v_w_conv, v_b_conv, v_w_rg_a, v_b_rg_a, v_w_rg_x, v_b_rg_x, v_lru_lambda, v_g_attn_out, v_g_lru_out, v_w_out, v_g_norm_mlp, v_w_mlp_in, v_w_mlp_out, v_g_norm_final):
    s, d = x.shape[1], x.shape[2]
    aw = d // 2
    nh = aw // HEAD
    f = w_mlp_out.shape[1] * N_CHIP
    n_ada = w_ada.shape[2]
    n_cv = w_conv.shape[2]
    ix, iy, ic = lax.axis_index("x"), lax.axis_index("y"), lax.axis_index("c")
    chip = 2 * ix + iy
    me = 2 * chip + ic
    c_arr = jnp.reshape(ic, (1,)).astype(jnp.int32)
    j_arr = jnp.reshape(chip, (1,)).astype(jnp.int32)

    x2d, tgt = x[0], loss_target[0]

    k_in, k_out, k_mi, k_mo = kinds = ("col", "row", "col", "row")
    slab, sizes = _pack([c, w_conv])
    p_in, _ = _place_cast("place_cast_0", j_arr, w_in[0], k_in, c)
    (f_in,), (dir_in,), slab = _gather_call("gather_0", [p_in], [k_in], [], [("direct", [0])], c, slab)
    p_out, tok = _place_cast("place_cast_1", j_arr, w_out[0], k_out, slab)
    p_mi, tok = _place_cast("place_cast_2", j_arr, w_mlp_in[0], k_mi, tok)
    p_mo, tok = _place_cast("place_cast_3", j_arr, w_mlp_out[0], k_mo, tok)

    gathered = _all_gather_small("comm_gather_cond", slab, after=(tok,))
    c_parts, cv_parts = _unpack(gathered, sizes, [(d,), (CONV_TAPS, n_cv)], lead=(N_DEV,))
    c_all = c_parts
    w_conv_full = jnp.concatenate([cv_parts[2 * j] for j in range(N_CHIP)], axis=-1)
    b_sh = lax.dynamic_slice(b_ada, (0, chip * n_ada), (1, n_ada))
    (f_in,), (rel_in, near_in), c_all = _gather_call(
        "gather_1", [f_in], [k_in], [("direct", [0], *dir_in)], [("relay", [0]), ("d2d_near", [0])], gathered, c_all)
    mod_part, act_all = _ada_mod(c_all, w_ada[0], b_sh)
    mod_g = _all_gather_small("comm_gather_mod", mod_part.reshape(-1, LANES))
    mod_g = mod_g.reshape(N_DEV, N_DEV, n_ada)
    mod = jnp.concatenate([lax.dynamic_index_in_dim(mod_g[2 * j], me, 0, keepdims=True) for j in range(N_CHIP)], axis=-1)
    sh1, sc1, gt1, sh2, sc2, gt2 = [mod[:, k * d:(k + 1) * d] for k in range(N_MOD)]

    (f_in, f_out, f_mi, f_mo), (far_in, dir_om, dir_mo), sh1 = _gather_call(
        "gather_2", [f_in, p_out, p_mi, p_mo], kinds, [("relay", [0], *rel_in)],
        [("d2d_far", [0]), ("direct", [1, 2]), ("direct", [3])], mod, sh1)

    h1, rstd1 = _norm_mod_fwd("norm_mod_fwd1", x2d, g_norm_mix, sc1, sh1)
    (w_in_f,), _, h1 = _gather_call("gather_3", [f_in], [k_in],
                                    [("d2d_near", [0], *near_in), ("d2d_far", [0], *far_in)], [], rstd1, h1)
    (qkv,) = _matmul("mm_qkv", h1, w_in_f, "nn", s, 3 * aw, d, (BF16,))
    (xrg,) = _matmul("mm_xrg", h1, w_in_f, "nn", s, 2 * aw, d, (F32,), b_off=3 * aw)
    o_attn, attn_w, attn_sg = _attn_fwd(qkv, nh)
    (f_out, f_mi), (rel_om, near_om), xrg = _gather_call(
        "gather_4", [f_out, f_mi], [k_out, k_mi], [("direct", [0, 1], *dir_om)],
        [("relay", [0, 1]), ("d2d_near", [0, 1])], o_attn, xrg)
    wa3, wx3 = w_rg_a[0], w_rg_x[0]
    o_lru, hseq = _lru_fwd(xrg, w_conv_full, b_conv, wa3, b_rg_a, wx3, b_rg_x, lru_lambda)
    mixed, rstd_a, rstd_l = _mix_norm_fwd(o_attn, o_lru, g_attn_out, g_lru_out)
    (f_out, f_mi, f_mo), (far_om, rel_mo, near_mo), mixed = _gather_call(
        "gather_5", [f_out, f_mi, f_mo], [k_out, k_mi, k_mo], [("relay", [0, 1], *rel_om), ("direct", [2], *dir_mo)],
        [("d2d_far", [0, 1]), ("relay", [2]), ("d2d_near", [2])], rstd_a, mixed)
    (w_out_f, w_mi_f), _, mixed = _gather_call(
        "gather_6", [f_out, f_mi], [k_out, k_mi], [("d2d_near", [0, 1], *near_om), ("d2d_far", [0, 1], *far_om)], [],
        rstd_l, mixed)

    def residual(acc, xin, gt):
        return acc, xin + gt * acc

    y1, x1 = _matmul("mm_out", mixed, w_out_f, "nn", s, d, d, (BF16, F32), extras=(x2d, gt1),
                     extra_kinds=("tile", "row"), epilogue=residual)
    h2, rstd2 = _norm_mod_fwd("norm_mod_fwd2", x1, g_norm_mlp, sc2, sh2)
    (f_mo,), (far_mo,), h2 = _gather_call("gather_7", [f_mo], [k_mo], [("relay", [0], *rel_mo)],
                                          [("d2d_far", [0])], rstd2, h2)

    def sq_relu(acc):
        r = jnp.maximum(acc, 0.0)
        return 2.0 * r, r * r

    r2, hid = _matmul("mm_mlp_in", h2, w_mi_f, "nn", s, f, d, (BF16, BF16), epilogue=sq_relu)
    (w_mo_f,), _, hid = _gather_call("gather_8", [f_mo], [k_mo],
                                     [("d2d_near", [0], *near_mo), ("d2d_far", [0], *far_mo)], [], r2, hid)
    y2, x2 = _matmul("mm_mlp_out", hid, w_mo_f, "nn", s, d, f, (BF16, F32), extras=(x1, gt2),
                     extra_kinds=("tile", "row"), epilogue=residual)
    dx2, loss_row, dg_final, dy2, dgt2 = _final_loss(x2, g_norm_final.reshape(1, d), tgt, y2, gt2)

    oc_arr = 1 - c_arr

    def dw_half(name, st, h_arr, got=None):
        add = {} if got is None else dict(extras=(got,), extra_kinds=("tile",),
                                          epilogue=lambda acc, g_: (acc + g_.astype(F32),))
        (out,) = _matmul(name, st["a"], st["dy"], "tn", st["m"], st["n"], s, (BF16,), tm=st["tm"], m_half=h_arr, **add)
        return out

    def rs_begin(tag, kind, xa, dy, m, n, thru):
        st = {"tag": tag, "kind": kind, "a": xa, "dy": dy, "m": m, "n": n,
              "tm": m // (2 * N_CHIP) if kind == "row" else m // 2}
        first = dw_half("mm_dw_%s_a" % tag, st, oc_arr)
        send, recv, first, land, thru = _split_start("rs_swap_start_" + tag, first, first.shape, 1, _plan_whole, thru)
        st["swap"] = (send, recv, first, land)
        return st, thru

    def rs_mid(st, after, thru):
        tag, kind = st["tag"], st["kind"]
        _, got = _split_wait("rs_swap_wait_" + tag, *st["swap"], _plan_whole, after)
        part = dw_half("mm_dw_%s_b" % tag, st, c_arr, got)
        if kind == "row":
            part = part.reshape(N_CHIP, st["m"] // (2 * N_CHIP), st["n"])
        blk = (part.shape[0], part.shape[1] // N_CHIP) if kind == "col" else part.shape[1:]
        send, recv, part, land, thru = _split_start("rs_scatter_start_" + tag, part, (N_CHIP - 1,) + blk, N_CHIP - 1,
                                                    _plan_scatter(kind), thru)
        st["scatter"] = (send, recv, part, land)
        return thru

    def rs_end(st, after):
        tag, kind = st["tag"], st["kind"]
        part, got = _split_wait("rs_scatter_wait_" + tag, *st["scatter"], _plan_scatter(kind), after)
        return _sum_partials("sum_partials_" + tag, j_arr, part, got, kind)

    (dpre,) = _matmul("mm_dhid", dy2, w_mo_f, "nt", s, f, d, (BF16,), extras=(r2,), extra_kinds=("tile",),
                      epilogue=lambda acc, r: (acc * r.astype(F32),))
    st_mo, dpre = rs_begin("mo", "row", hid, dy2, f, d, dpre)
    (dh2,) = _matmul("mm_dh2", dpre, w_mi_f, "nt", s, d, f, (BF16,))
    dh2 = rs_mid(st_mo, dh2, dh2)
    st_mi, dh2 = rs_begin("mi", "col", h2, dpre, d, f, dh2)
    dx1, dsh2, dsc2, dg_mlp, dy1, dgt1 = _norm_mod_bwd("norm_mod_bwd2", dh2, x1, rstd2, g_norm_mlp, sc2, dx2,
                                                       gate=(y1, gt1))
    (dmixed,) = _matmul("mm_dmixed", dy1, w_out_f, "nt", s, d, d, (BF16,))
    dmixed = rs_mid(st_mi, dmixed, dmixed)
    st_out, dmixed = rs_begin("out", "row", mixed, dy1, d, d, dmixed)
    do_attn, do_lru, dg_attn, dg_lru = _mix_norm_bwd(dmixed, o_attn, o_lru, rstd_a, rstd_l, g_attn_out, g_lru_out)
    dq, dk, dv = _attn_bwd(qkv, do_attn, attn_w, attn_sg, nh)
    do_lru = rs_mid(st_out, dq, do_lru)
    dxr, dxg, dwconv, dbconv, dwa, dba, dwx, dbx, dlam = _lru_bwd(
        xrg, do_lru, hseq, w_conv_full, b_conv, wa3, b_rg_a, wx3, b_rg_x, lru_lambda)
    dproj = jnp.concatenate([dq, dk, dv, dxr, dxg], axis=-1)
    st_in, dproj = rs_begin("in", "col", h1, dproj, d, 5 * aw, dproj)
    st_in["dy"] = dproj
    (dh1,) = _matmul("mm_dh1", dproj, w_in_f, "nt", s, d, 5 * aw, (BF16,))
    dh1 = rs_mid(st_in, dh1, dh1)
    grad_x, dsh1, dsc1, dg_mix = _norm_mod_bwd("norm_mod_bwd1", dh1, x2d, rstd1, g_norm_mix, sc1, dx1)

    dmod = jnp.concatenate([dsh1, dsc1, dgt1, dsh2, dsc2, dgt2], axis=-1)
    small_names = ["b_ada", "g_norm_mix", "b_conv", "w_rg_a", "b_rg_a", "w_rg_x", "b_rg_x", "lru_lambda",
                   "g_attn_out", "g_lru_out", "g_norm_mlp", "g_norm_final"]
    small_g = [dmod, dg_mix, dbconv, dwa, dba, dwx, dbx, dlam, dg_attn, dg_lru, dg_mlp, dg_final]
    small_w = [b_ada, g_norm_mix, b_conv, w_rg_a, b_rg_a, w_rg_x, b_rg_x, lru_lambda, g_attn_out, g_lru_out, g_norm_mlp, g_norm_final]
    small_m = [m_b_ada, m_g_norm_mix, m_b_conv, m_w_rg_a, m_b_rg_a, m_w_rg_x, m_b_rg_x, m_lru_lambda, m_g_attn_out, m_g_lru_out, m_g_norm_mlp, m_g_norm_final]
    small_v = [v_b_ada, v_g_norm_mix, v_b_conv, v_w_rg_a, v_b_rg_a, v_w_rg_x, v_b_rg_x, v_lru_lambda, v_g_attn_out, v_g_lru_out, v_g_norm_mlp, v_g_norm_final]
    extra_zero = [jnp.zeros_like(dwconv), jnp.zeros((LANES,), F32)]
    g_slab, g_sizes = _pack(small_g + [dwconv, loss_row])
    w_slab, _ = _pack(small_w + extra_zero)
    m_slab, _ = _pack(small_m + extra_zero)
    v_slab, _ = _pack(small_v + extra_zero)
    me_arr = jnp.reshape(me, (1,)).astype(jnp.int32)
    g_buf = _place_row("place_small_grads", me_arr, g_slab)
    sg_send, sg_recv, g_buf, tok = _split_start_inplace("sg_gather_start", g_buf, N_CHIP, _plan_gather_own, loss_row)

    def reduced_begin(tag, half, tok_):
        send, recv, half, land, tok_ = _split_start("rs_reduced_start_" + tag, half, half.shape, 1, _plan_whole, tok_)
        return (send, recv, half, land), tok_

    def reduced_end(tag, st, after):
        return _split_wait("rs_reduced_wait_" + tag, *st, _plan_whole, after)

    sw_mo, tok = reduced_begin("mo", rs_end(st_mo, tok), tok)
    sw_mi, tok = reduced_begin("mi", rs_end(st_mi, tok), tok)
    sw_out, tok = reduced_begin("out", rs_end(st_out, tok), tok)
    half_mo, got_mo = reduced_end("mo", sw_mo, tok)
    big = {"w_mlp_out": _adamw_halves("adamw_w_mlp_out", c_arr, w_mlp_out[0], m_w_mlp_out[0], v_w_mlp_out[0],
                                      half_mo, got_mo)}
    half_mi, got_mi = reduced_end("mi", sw_mi, big["w_mlp_out"][1])
    big["w_mlp_in"] = _adamw_halves("adamw_w_mlp_in", c_arr, w_mlp_in[0], m_w_mlp_in[0], v_w_mlp_in[0], half_mi, got_mi)
    half_out, got_out = reduced_end("out", sw_out, big["w_mlp_in"][1])
    big["w_out"] = _adamw_halves("adamw_w_out", c_arr, w_out[0], m_w_out[0], v_w_out[0], half_out, got_out)
    g_buf = _split_wait_inplace("sg_gather_wait", sg_send, sg_recv, g_buf, _plan_gather_own, big["w_out"][1])
    sg_send, sg_recv, g_buf, tok = _split_start_inplace("sg_pass_start", g_buf, N_CHIP - 1, _plan_gather_pass, tok)
    half_in = rs_end(st_in, tok)
    g_all = _split_wait_inplace("sg_pass_wait", sg_send, sg_recv, g_buf, _plan_gather_pass, half_in)
    gs_slab, ds_slab, ms_slab, vs_slab = _small_reduce_adamw(g_all, w_slab, m_slab, v_slab)
    shapes = [w.shape for w in small_w] + [dwconv.shape, (LANES,)]
    gs = _unpack(gs_slab, g_sizes, shapes)
    ds = _unpack(ds_slab, g_sizes, shapes)
    ms = _unpack(ms_slab, g_sizes, shapes)
    vs = _unpack(vs_slab, g_sizes, shapes)
    small = {n: (gs[i], ds[i], ms[i], vs[i]) for i, n in enumerate(small_names)}
    loss = gs[-1][0]
    g_wconv = lax.dynamic_slice(gs[-2], (0, chip * n_cv), (CONV_TAPS, n_cv))
    d_wconv, m_wconv, v_wconv = _adamw_plain("adamw_conv", w_conv[0], g_wconv, m_w_conv[0], v_w_conv[0])
    small["w_conv"] = (g_wconv[None], d_wconv[None], m_wconv[None], v_wconv[None])

    dmod_all = g_all[:, :N_MOD * d // LANES, :].reshape(N_DEV, N_MOD * d)
    dmod_sel = lax.dynamic_slice(dmod_all, (0, chip * n_ada), (N_DEV, n_ada)).astype(BF16)
    act_t = act_all.T.astype(BF16)
    big["w_ada"] = _adamw_ada(w_ada[0], m_w_ada[0], v_w_ada[0], act_t, dmod_sel)

    (got_in,) = _swap_reduced("comm_swap_reduced_in", [half_in])
    big["w_in"] = _adamw_halves("adamw_w_in", c_arr, w_in[0], m_w_in[0], v_w_in[0], half_in, got_in)

    order = ["w_ada", "b_ada", "g_norm_mix", "w_in", "w_conv", "b_conv", "w_rg_a", "b_rg_a", "w_rg_x", "b_rg_x",
             "lru_lambda", "g_attn_out", "g_lru_out", "w_out", "g_norm_mlp", "w_mlp_in", "w_mlp_out", "g_norm_final"]
    res = {}
    for n in order:
        res[n] = tuple(t[None] for t in big[n]) if n in big else small[n]
    return (loss, grad_x[None],
            *[res[n][0] for n in order], *[res[n][1] for n in order],
            *[res[n][2] for n in order], *[res[n][3] for n in order])
```

```python
import functools
import math

import jax
import jax.numpy as jnp
from jax import lax
from jax.experimental import pallas as pl
from jax.experimental.pallas import tpu as pltpu

F32 = jnp.float32
BF16 = jnp.bfloat16
SDS = jax.ShapeDtypeStruct
MESH = pl.DeviceIdType.MESH

EPS = 1e-6
HEAD = 128
N_MOD = 6
CONV_TAPS = 4
LRU_C = 8.0
ADAM_LR, ADAM_B1, ADAM_B2, ADAM_EPS, ADAM_WD, ADAM_STEP = 0.001, 0.9, 0.999, 1e-08, 0.01, 10
N_DEV = 8
N_CHIP = 4
LANES = 128
SUBLANES = 8
VMEM_LIMIT = 56 * 1024 * 1024
PACK_ROWS = 256
MM_TILE_M, MM_TILE_N, MM_TILE_K = 1024, 1024, 2048
ROW_TILE = 256
ROW_SPLIT = 1


def _tile(dim, pref):
    t = min(dim, pref)
    while dim % t:
        t -= LANES
    return t


def _params(sem=None):
    return pltpu.CompilerParams(dimension_semantics=sem, vmem_limit_bytes=VMEM_LIMIT)


def _sigmoid(x):
    return 1.0 / (1.0 + jnp.exp(-x))


def _log_sigmoid(x):
    return jnp.minimum(x, 0.0) - jnp.log(1.0 + jnp.exp(-jnp.abs(x)))


def _gelu_parts(x):
    k0, k1 = math.sqrt(2.0 / math.pi), 0.044715
    t = jnp.tanh(k0 * (x + k1 * x * x * x))
    val = 0.5 * x * (1.0 + t)
    der = 0.5 * (1.0 + t) + 0.5 * x * (1.0 - t * t) * k0 * (1.0 + 3.0 * k1 * x * x)
    return val, der


def _dot(a, b):
    return jnp.dot(a, b, preferred_element_type=F32)


def _dot_nt(a, b):
    return lax.dot_general(a, b, (((1,), (1,)), ((), ())), preferred_element_type=F32)


def _dot_tn(a, b):
    return lax.dot_general(a, b, (((0,), (0,)), ((), ())), preferred_element_type=F32)


def _split_dot(x, tri):
    hi = x.astype(BF16)
    lo = (x - hi.astype(F32)).astype(BF16)
    return _dot(hi, tri) + _dot(lo, tri)


def _matmul(name, a, b, mode, m, n, k, out_dtypes, *, b_off=0, extras=(), extra_kinds=(), epilogue=None,
            tm=MM_TILE_M, tn=MM_TILE_N, tk=MM_TILE_K, m_half=None):
    tm, tn, tk = _tile(m, tm), _tile(math.gcd(n, b_off) if b_off else n, tn), _tile(k, tk)
    assert b_off % tn == 0
    nk = k // tk
    n_ex, n_out = len(extras), len(out_dtypes)
    dot = {"nn": _dot, "nt": _dot_nt, "tn": _dot_tn}[mode]
    n_pre = 0 if m_half is None else 1
    m_out = m if m_half is None else m // 2

    def body(*refs):
        a_ref, b_ref, *rest = refs[n_pre:]
        ex, outs = rest[:n_ex], rest[n_ex:n_ex + n_out]

        def finish(total):
            res = epilogue(total, *[e[...] for e in ex]) if epilogue else (total,)
            for o, r in zip(outs, res):
                o[...] = r.astype(o.dtype)

        if nk == 1:
            finish(dot(a_ref[...], b_ref[...]))
            return
        acc = rest[-1]
        kk = pl.program_id(2)

        @pl.when(kk == 0)
        def _():
            acc[...] = dot(a_ref[...], b_ref[...])

        @pl.when(jnp.logical_and(kk > 0, kk < nk - 1))
        def _():
            acc[...] += dot(a_ref[...], b_ref[...])

        @pl.when(kk == nk - 1)
        def _():
            finish(acc[...] + dot(a_ref[...], b_ref[...]))

    if mode == "nn":
        a_spec = pl.BlockSpec((tm, tk), lambda i, j, kk, *_: (i, kk))
        b_spec = pl.BlockSpec((tk, tn), lambda i, j, kk, *_: (kk, j + b_off // tn))
    elif mode == "nt":
        a_spec = pl.BlockSpec((tm, tk), lambda i, j, kk, *_: (i, kk))
        b_spec = pl.BlockSpec((tn, tk), lambda i, j, kk, *_: (j, kk + b_off // tk))
    elif m_half is None:
        a_spec = pl.BlockSpec((tk, tm), lambda i, j, kk: (kk, i))
        b_spec = pl.BlockSpec((tk, tn), lambda i, j, kk: (kk, j))
    else:
        a_spec = pl.BlockSpec((tk, tm), lambda i, j, kk, h: (kk, 2 * i + h[0]))
        b_spec = pl.BlockSpec((tk, tn), lambda i, j, kk, h: (kk, j))
    tile_spec = pl.BlockSpec((tm, tn), lambda i, j, kk, *_: (i, j))
    row_spec = pl.BlockSpec((1, tn), lambda i, j, kk, *_: (0, j))
    in_specs = [a_spec, b_spec] + [tile_spec if kind == "tile" else row_spec for kind in extra_kinds]
    out_specs = tuple(tile_spec for _ in out_dtypes)
    out_shape = tuple(SDS((m_out, n), dt) for dt in out_dtypes)
    scratch = [pltpu.VMEM((tm, tn), F32)] if nk > 1 else []
    grid = (m_out // tm, n // tn, nk)
    semantics = _params(("parallel", "parallel", "arbitrary"))
    if m_half is None:
        return pl.pallas_call(body, grid=grid, in_specs=in_specs, out_specs=out_specs, out_shape=out_shape,
                              scratch_shapes=scratch, compiler_params=semantics, name=name)(a, b, *extras)
    assert mode == "tn" and (m // tm) % 2 == 0
    return pl.pallas_call(
        body,
        grid_spec=pltpu.PrefetchScalarGridSpec(num_scalar_prefetch=1, grid=grid, in_specs=in_specs, out_specs=out_specs,
                                               scratch_shapes=scratch),
        out_shape=out_shape, compiler_params=semantics, name=name)(m_half, a, b, *extras)


def _row_specs(s, d, tr):
    row = "row"
    vec = pl.BlockSpec((1, d), lambda i: (0, 0))
    col = pl.BlockSpec((tr, 1), lambda i: (i, 0))
    return row, vec, col


class _ColChunks:
    def __init__(self, refs):
        self.refs = refs

    def __getitem__(self, idx):
        return jnp.concatenate([r[...] for r in self.refs], axis=-1)


def _rows_call(name, body, grid, in_specs, out_specs, out_shape, semantics, args):
    in_x, args_x, groups = [], [], []
    for spec, arr in zip(in_specs, args):
        if isinstance(spec, str):
            rows, d = arr.shape
            tr, dc = rows // grid[0], d // ROW_SPLIT
            in_x += [pl.BlockSpec((tr, dc), functools.partial(lambda i, jj: (i, jj), jj=j)) for j in range(ROW_SPLIT)]
            args_x += [arr] * ROW_SPLIT
            groups.append(ROW_SPLIT)
        else:
            in_x.append(spec)
            args_x.append(arr)
            groups.append(1)
    out_x = [pl.BlockSpec((sh.shape[0] // grid[0], sh.shape[1]), lambda i: (i, 0)) if isinstance(spec, str) else spec
             for spec, sh in zip(out_specs, out_shape)]

    def wrapped(*refs):
        views, k = [], 0
        for g in groups:
            views.append(_ColChunks(refs[k:k + g]) if g > 1 else refs[k])
            k += g
        body(*views, *refs[k:])

    return pl.pallas_call(
        wrapped, grid=grid, in_specs=in_x, out_specs=tuple(out_x), out_shape=tuple(out_shape),
        compiler_params=_params(semantics), name=name)(*args_x)


def _norm_mod_fwd(name, x, g, sc, sh):
    s, d = x.shape
    tr = _tile(s, ROW_TILE)
    row, vec, col = _row_specs(s, d, tr)

    def body(x_ref, g_ref, sc_ref, sh_ref, h_ref, r_ref):
        xv = x_ref[...]
        r = lax.rsqrt(jnp.mean(xv * xv, axis=-1, keepdims=True) + EPS)
        h_ref[...] = ((xv * r * g_ref[...]) * (1.0 + sc_ref[...]) + sh_ref[...]).astype(BF16)
        r_ref[...] = r

    return _rows_call(name, body, (s // tr,), [row, vec, vec, vec], (row, col),
                      (SDS((s, d), BF16), SDS((s, 1), F32)), ("parallel",), (x, g, sc, sh))


def _norm_mod_bwd(name, dh, xin, rstd, g, sc, dres, gate=None):
    s, d = xin.shape
    tr = _tile(s, ROW_TILE)
    row, vec, col = _row_specs(s, d, tr)

    n_gate = 2 if gate is not None else 0

    def body(dh_ref, x_ref, r_ref, g_ref, sc_ref, dres_ref, *rest):
        gate_in, gate_out = rest[:n_gate], rest[n_gate + 4:]
        dx_ref, dsh_ref, dsc_ref, dg_ref = rest[n_gate:n_gate + 4]

        @pl.when(pl.program_id(0) == 0)
        def _():
            for ref in (dsh_ref, dsc_ref, dg_ref) + tuple(gate_out[1:]):
                ref[...] = jnp.zeros_like(ref)

        dh_v, xv, r, gv = dh_ref[...].astype(F32), x_ref[...], r_ref[...], g_ref[...]
        n0 = xv * r
        dsh_ref[...] += jnp.sum(dh_v, axis=0, keepdims=True)
        dsc_ref[...] += jnp.sum(dh_v * (n0 * gv), axis=0, keepdims=True)
        dn = dh_v * (1.0 + sc_ref[...])
        dg_ref[...] += jnp.sum(dn * n0, axis=0, keepdims=True)
        gy = dn * gv
        dot = jnp.mean(gy * xv, axis=-1, keepdims=True)
        dxv = dres_ref[...] + r * gy - xv * (r * r * r * dot)
        dx_ref[...] = dxv
        if gate is not None:
            y_ref, gt_ref = gate_in
            dy_ref, dgt_ref = gate_out
            dy_ref[...] = (gt_ref[...] * dxv).astype(BF16)
            dgt_ref[...] += jnp.sum(dxv * y_ref[...], axis=0, keepdims=True)

    vecs = SDS((1, d), F32)
    gate_args = tuple(gate) if gate is not None else ()
    return _rows_call(
        name, body, (s // tr,),
        [row, row, col, vec, vec, row] + ([row, vec] if gate is not None else []),
        (row, vec, vec, vec) + ((row, vec) if gate is not None else ()),
        (SDS((s, d), F32), vecs, vecs, vecs) + ((SDS((s, d), BF16), vecs) if gate is not None else ()),
        ("arbitrary",), (dh, xin, rstd, g, sc, dres, *gate_args))


def _final_loss(x2, gf, tgt, y, gt):
    s, d = x2.shape
    tr = _tile(s, ROW_TILE)
    row, vec, _ = _row_specs(s, d, tr)
    lrow = pl.BlockSpec((1, LANES), lambda i: (0, 0))

    def body(x_ref, g_ref, t_ref, y_ref, gt_ref, dx_ref, loss_ref, dg_ref, dy_ref, dgt_ref):
        @pl.when(pl.program_id(0) == 0)
        def _():
            loss_ref[...] = jnp.zeros_like(loss_ref)
            dg_ref[...] = jnp.zeros_like(dg_ref)
            dgt_ref[...] = jnp.zeros_like(dgt_ref)

        xv, gv = x_ref[...], g_ref[...]
        r = lax.rsqrt(jnp.mean(xv * xv, axis=-1, keepdims=True) + EPS)
        n0 = xv * r
        err = n0 * gv - t_ref[...]
        loss_ref[...] += jnp.sum(err * err) * (0.5 / d)
        dy = err * (1.0 / d)
        dg_ref[...] += jnp.sum(dy * n0, axis=0, keepdims=True)
        gy = dy * gv
        dot = jnp.mean(gy * xv, axis=-1, keepdims=True)
        dxv = r * gy - xv * (r * r * r * dot)
        dx_ref[...] = dxv
        dy_ref[...] = (gt_ref[...] * dxv).astype(BF16)
        dgt_ref[...] += jnp.sum(dxv * y_ref[...], axis=0, keepdims=True)

    return _rows_call(
        "final_loss", body, (s // tr,), [row, vec, row, row, vec], (row, lrow, vec, row, vec),
        (SDS((s, d), F32), SDS((1, LANES), F32), SDS((1, d), F32), SDS((s, d), BF16), SDS((1, d), F32)),
        ("arbitrary",), (x2, gf, tgt, y, gt))


def _mix_norm_fwd(oa, ol, ga, gl):
    s, w = oa.shape
    tr = _tile(s, ROW_TILE)
    row, vec, col = _row_specs(s, w, tr)

    def body(oa_ref, ol_ref, ga_ref, gl_ref, mx_ref, ra_ref, rl_ref):
        a, l = oa_ref[...], ol_ref[...]
        ra = lax.rsqrt(jnp.mean(a * a, axis=-1, keepdims=True) + EPS)
        rl = lax.rsqrt(jnp.mean(l * l, axis=-1, keepdims=True) + EPS)
        mx_ref[:, :w] = (a * ra * ga_ref[...]).astype(BF16)
        mx_ref[:, w:] = (l * rl * gl_ref[...]).astype(BF16)
        ra_ref[...] = ra
        rl_ref[...] = rl

    return _rows_call(
        "mix_norm_fwd", body, (s // tr,), [row, row, vec, vec], (row, col, col),
        (SDS((s, 2 * w), BF16), SDS((s, 1), F32), SDS((s, 1), F32)), ("parallel",), (oa, ol, ga, gl))


def _mix_norm_bwd(dmx, oa, ol, ra, rl, ga, gl):
    s, w = oa.shape
    tr = _tile(s, ROW_TILE)
    row, vec, col = _row_specs(s, w, tr)

    def body(dm_ref, oa_ref, ol_ref, ra_ref, rl_ref, ga_ref, gl_ref, doa_ref, dol_ref, dga_ref, dgl_ref):
        @pl.when(pl.program_id(0) == 0)
        def _():
            dga_ref[...] = jnp.zeros_like(dga_ref)
            dgl_ref[...] = jnp.zeros_like(dgl_ref)

        def one(dy, xv, r, gv, dg_ref):
            dg_ref[...] += jnp.sum(dy * (xv * r), axis=0, keepdims=True)
            gy = dy * gv
            dot = jnp.mean(gy * xv, axis=-1, keepdims=True)
            return r * gy - xv * (r * r * r * dot)

        dm = dm_ref[...].astype(F32)
        doa_ref[...] = one(dm[:, :w], oa_ref[...], ra_ref[...], ga_ref[...], dga_ref).astype(BF16)
        dol_ref[...] = one(dm[:, w:], ol_ref[...], rl_ref[...], gl_ref[...], dgl_ref)

    return _rows_call(
        "mix_norm_bwd", body, (s // tr,), [row, row, row, col, col, vec, vec], (row, row, vec, vec),
        (SDS((s, w), BF16), SDS((s, w), F32), SDS((1, w), F32), SDS((1, w), F32)), ("arbitrary",),
        (dmx, oa, ol, ra, rl, ga, gl))


def _attn_blocks(qs, ks, tri_after, csums, causal):
    zs = [_dot_nt(q, k) * (HEAD ** -0.5) for q, k in zip(qs, ks)]
    lbs = [_log_sigmoid(z) for z in zs]
    lss = [lb - z for lb, z in zip(lbs, zs)]
    if causal is not None:
        lss = [jnp.where(causal, ls, 0.0) for ls in lss]
    locs = [_split_dot(ls, tri_after) for ls in lss]
    ws = [jnp.exp(lb + (loc + cs)) for lb, loc, cs in zip(lbs, locs, csums)]
    if causal is not None:
        ws = [jnp.where(causal, w, 0.0) for w in ws]
    nxt = [cs + (loc[:, 0:1] + ls[:, 0:1]) for cs, loc, ls in zip(csums, locs, lss)]
    return lbs, ws, nxt


ATTN_HEADS_PER_STEP = 4


def _attn_tile(s):
    return 256 if s >= 1024 else 128


def _tri(t, after):
    r_i = lax.broadcasted_iota(jnp.int32, (t, t), 0)
    c_i = lax.broadcasted_iota(jnp.int32, (t, t), 1)
    return ((r_i > c_i) if after else (r_i < c_i)).astype(BF16)


def _attn_fwd(qkv, n_heads):
    s = qkv.shape[0]
    t = _attn_tile(s)
    hps = ATTN_HEADS_PER_STEP
    wid = hps * HEAD

    nq = s // t

    def body(q_ref, k_ref, v_ref, o_ref, w_ref, sg_ref):
        qi = pl.program_id(1)
        tri_after = _tri(t, True)
        causal = lax.broadcasted_iota(jnp.int32, (t, t), 1) < lax.broadcasted_iota(jnp.int32, (t, t), 0)
        lanes = [slice(a * HEAD, (a + 1) * HEAD) for a in range(hps)]
        qs = [q_ref[:, ln] for ln in lanes]

        def block(kb, carry, mask):
            off = pl.multiple_of(kb * t, t)
            ks = [k_ref[pl.ds(off, t), ln] for ln in lanes]
            lbs, ws, csums = _attn_blocks(qs, ks, tri_after, [cr[0] for cr in carry], mask)
            wbs = [w.astype(BF16) for w in ws]
            for a in range(hps):
                w_ref[a, kb] = wbs[a]
                sg_ref[a, kb] = jnp.exp(lbs[a]).astype(BF16)
            os_ = [cr[1] + _dot(wb, v_ref[pl.ds(off, t), ln]) for cr, wb, ln in zip(carry, wbs, lanes)]
            return tuple(zip(csums, os_))

        zero = tuple((jnp.zeros((t, 1), F32), jnp.zeros((t, HEAD), F32)) for _ in lanes)
        carry = block(qi, zero, causal)
        carry = lax.fori_loop(1, qi + 1, lambda it, cr: block(qi - it, cr, None), carry)
        for a, ln in enumerate(lanes):
            o_ref[:, ln] = carry[a][1]

    hb = n_heads // hps
    kept = pl.BlockSpec((None, hps, nq, t, t), lambda hh, i: (hh * nq + i, 0, 0, 0, 0))
    kept_shape = SDS((hb * nq, hps, nq, t, t), BF16)
    return pl.pallas_call(
        body, grid=(hb, nq),
        in_specs=[pl.BlockSpec((t, wid), lambda hh, i: (i, hh)),
                  pl.BlockSpec((s, wid), lambda hh, i: (0, hb + hh)),
                  pl.BlockSpec((s, wid), lambda hh, i: (0, 2 * hb + hh))],
        out_specs=(pl.BlockSpec((t, wid), lambda hh, i: (i, hh)), kept, kept),
        out_shape=(SDS((s, n_heads * HEAD), F32), kept_shape, kept_shape),
        compiler_params=_params(("parallel", "parallel")), name="attn_fwd")(qkv, qkv, qkv)


def _attn_bwd(qkv, do, w_kept, sg_kept, n_heads):
    s = qkv.shape[0]
    t = _attn_tile(s)
    nq = s // t
    scale = HEAD ** -0.5
    hps = ATTN_HEADS_PER_STEP
    wid = hps * HEAD

    def body(q_ref, k_ref, v_ref, do_ref, w_ref, sg_ref, dq_ref, dk_ref, dv_ref, dk_acc, dv_acc):
        qi = pl.program_id(1)

        @pl.when(qi == 0)
        def _():
            dk_acc[...] = jnp.zeros_like(dk_acc)
            dv_acc[...] = jnp.zeros_like(dv_acc)

        tri_before = _tri(t, False)
        causal = lax.broadcasted_iota(jnp.int32, (t, t), 1) < lax.broadcasted_iota(jnp.int32, (t, t), 0)
        lanes = [slice(a * HEAD, (a + 1) * HEAD) for a in range(hps)]
        qs = [q_ref[:, ln] for ln in lanes]
        douts = [do_ref[:, ln] for ln in lanes]

        def block(kb, carry, mask):
            off = pl.multiple_of(kb * t, t)
            wbs = [w_ref[a, kb] for a in range(hps)]
            dws = [_dot_nt(dout, v_ref[pl.ds(off, t), ln]) for dout, ln in zip(douts, lanes)]
            for a, ln in enumerate(lanes):
                dv_acc[pl.ds(off, t), ln] += _dot_tn(wbs[a], douts[a])
            es = [dw * wb.astype(F32) for dw, wb in zip(dws, wbs)]
            locs = [_split_dot(e, tri_before) for e in es]
            sgs = [sg_ref[a, kb].astype(F32) for a in range(hps)]
            stays = [(loc + cr[0]) * sg for loc, cr, sg in zip(locs, carry, sgs)]
            if mask is not None:
                stays = [jnp.where(mask, st, 0.0) for st in stays]
            dzbs = [((e * (1.0 - sg) - st) * scale).astype(BF16) for e, sg, st in zip(es, sgs, stays)]
            dqs = [cr[1] + _dot(dzb, k_ref[pl.ds(off, t), ln]) for cr, dzb, ln in zip(carry, dzbs, lanes)]
            for a, ln in enumerate(lanes):
                dk_acc[pl.ds(off, t), ln] += _dot_tn(dzbs[a], qs[a])
            esums = [cr[0] + (loc[:, t - 1:t] + e[:, t - 1:t]) for cr, loc, e in zip(carry, locs, es)]
            return tuple(zip(esums, dqs))

        zero = tuple((jnp.zeros((t, 1), F32), jnp.zeros((t, HEAD), F32)) for _ in lanes)
        carry = lax.fori_loop(0, qi, lambda kb, cr: block(kb, cr, None), zero)
        carry = block(qi, carry, causal)
        for a, ln in enumerate(lanes):
            dq_ref[:, ln] = carry[a][1].astype(BF16)

        @pl.when(qi == nq - 1)
        def _():
            dk_ref[...] = dk_acc[...].astype(BF16)
            dv_ref[...] = dv_acc[...].astype(BF16)

    hb = n_heads // hps
    blk = pl.BlockSpec((t, wid), lambda hh, i: (i, hh))
    full = pl.BlockSpec((s, wid), lambda hh, i: (0, hh))
    kept = pl.BlockSpec((None, hps, nq, t, t), lambda hh, i: (hh * nq + i, 0, 0, 0, 0))
    return pl.pallas_call(
        body, grid=(hb, nq),
        in_specs=[blk,
                  pl.BlockSpec((s, wid), lambda hh, i: (0, hb + hh)),
                  pl.BlockSpec((s, wid), lambda hh, i: (0, 2 * hb + hh)),
                  blk, kept, kept],
        out_specs=(blk, full, full),
        out_shape=(SDS((s, n_heads * HEAD), BF16),) * 3,
        scratch_shapes=[pltpu.VMEM((s, wid), F32), pltpu.VMEM((s, wid), F32)],
        compiler_params=_params(("parallel", "arbitrary")), name="attn_bwd")(qkv, qkv, qkv, do, w_kept, sg_kept)


def _lru_chunk(s):
    return 256 if s >= 1024 else 128


def _lru_gates(xc, wa, ba, wx, bx, sp):
    xb = xc.astype(BF16)
    r = _sigmoid(_dot(xb, wa) + ba)
    ig = _sigmoid(_dot(xb, wx) + bx)
    la = -LRU_C * r * sp
    a = jnp.exp(la)
    t = jnp.tanh(la)
    mult = jnp.sqrt(-2.0 * t / (1.0 - t))
    return r, ig, a, mult


def _softplus_neg(lam):
    return jnp.maximum(-lam, 0.0) + jnp.log(1.0 + jnp.exp(-jnp.abs(lam)))


LRU_BLOCKS_PER_STEP = 1
SCAN_GROUP = 4


def _lru_specs(s, n_blocks):
    bps = min(LRU_BLOCKS_PER_STEP, n_blocks)
    wid = bps * HEAD
    seq0 = pl.BlockSpec((s, wid), lambda h: (0, h))
    seq1 = pl.BlockSpec((s, wid), lambda h: (0, n_blocks // bps + h))
    taps = pl.BlockSpec((CONV_TAPS, wid), lambda h: (0, h))
    vec = pl.BlockSpec((1, wid), lambda h: (0, h))
    mat = pl.BlockSpec((bps, HEAD, HEAD), lambda h: (h, 0, 0))
    return bps, seq0, seq1, taps, vec, mat


def _per_block(one_block, n_2d, n_mat_pos, bps):
    def body(*refs):
        for a in range(bps):
            views = [r.at[a] if i in n_mat_pos else r.at[:, pl.ds(a * HEAD, HEAD)] for i, r in enumerate(refs[:n_2d])]
            one_block(*views, *refs[n_2d:])
    return body


def _lru_fwd(xrg, wconv, bconv, wa, ba, wx, bx, lam):
    s = xrg.shape[0]
    nb = wa.shape[0]
    tc = _lru_chunk(s)
    bps, seq0, seq1, taps, vec, mat = _lru_specs(s, nb)
    pad = SUBLANES

    def one_block(xr_ref, xg_ref, wc_ref, bc_ref, wa_ref, ba_ref, wx_ref, bx_ref, lam_ref, o_ref, h_ref, pad_s, a_s, u_s):
        pad_s[0:pad, :] = jnp.zeros((pad, HEAD), F32)
        pad_s[pad:pad + s, :] = xr_ref[...]
        wab, wxb = wa_ref[...].astype(BF16), wx_ref[...].astype(BF16)
        sp = _softplus_neg(lam_ref[...])
        for c in range(s // tc):
            base = c * tc
            xc = bc_ref[...] + sum(wc_ref[i:i + 1, :] * pad_s[pl.ds(base + pad - (CONV_TAPS - 1) + i, tc), :]
                                   for i in range(CONV_TAPS))
            _, ig, a, mult = _lru_gates(xc, wab, ba_ref[...], wxb, bx_ref[...], sp)
            a_s[base:base + tc, :] = a
            u_s[base:base + tc, :] = mult * (ig * xc)

        row = lax.broadcasted_iota(jnp.int32, (SUBLANES, HEAD), 0)
        last = SUBLANES - 1

        def group(gi, hprev):
            offs = [pl.multiple_of((gi * SCAN_GROUP + q) * SUBLANES, SUBLANES) for q in range(SCAN_GROUP)]
            ab = []
            for off in offs:
                a8, b8 = a_s[pl.ds(off, SUBLANES), :], u_s[pl.ds(off, SUBLANES), :]
                for d in (1, 2, 4):
                    a_sh = jnp.where(row < d, 1.0, pltpu.roll(a8, d, 0))
                    b_sh = jnp.where(row < d, 0.0, pltpu.roll(b8, d, 0))
                    b8 = a8 * b_sh + b8
                    a8 = a8 * a_sh
                ab.append((a8, b8))
            enters = []
            for a8, b8 in ab:
                enters.append(hprev)
                hprev = a8[last:, :] * hprev + b8[last:, :]
            for off, (a8, b8), h0 in zip(offs, ab, enters):
                h_ref[pl.ds(off, SUBLANES), :] = a8 * h0 + b8
            return hprev

        lax.fori_loop(0, s // (SUBLANES * SCAN_GROUP), group, jnp.zeros((1, HEAD), F32))
        for c in range(s // tc):
            sl = slice(c * tc, (c + 1) * tc)
            gel, _ = _gelu_parts(xg_ref[sl, :])
            o_ref[sl, :] = h_ref[sl, :] * gel

    return pl.pallas_call(
        _per_block(one_block, 11, (4, 6), bps), grid=(nb // bps,),
        in_specs=[seq0, seq1, taps, vec, mat, vec, mat, vec, vec],
        out_specs=(seq0, seq0),
        out_shape=(SDS((s, nb * HEAD), F32), SDS((s, nb * HEAD), F32)),
        scratch_shapes=[pltpu.VMEM((s + pad, HEAD), F32), pltpu.VMEM((s, HEAD), F32), pltpu.VMEM((s, HEAD), F32)],
        compiler_params=_params(("parallel",)), name="lru_fwd")(xrg, xrg, wconv, bconv, wa, ba, wx, bx, lam)


def _lru_bwd(xrg, dol, hseq, wconv, bconv, wa, ba, wx, bx, lam):
    s = xrg.shape[0]
    nb = wa.shape[0]
    tc = _lru_chunk(s)
    bps, seq0, seq1, taps, vec, mat = _lru_specs(s, nb)
    pad = SUBLANES

    def one_block(xr_ref, xg_ref, do_ref, h_ref, wc_ref, bc_ref, wa_ref, ba_ref, wx_ref, bx_ref, lam_ref,
             dxr_ref, dxg_ref, dwc_ref, dbc_ref, dwa_ref, dba_ref, dwx_ref, dbx_ref, dlam_ref,
             pad_s, hp_s, a_s, g_s, da_s, dxc_s):
        pad_s[0:pad, :] = jnp.zeros((pad, HEAD), F32)
        pad_s[pad:pad + s, :] = xr_ref[...]
        hp_s[0:pad, :] = jnp.zeros((pad, HEAD), F32)
        hp_s[pad:pad + s, :] = h_ref[...]
        a_s[s:s + pad, :] = jnp.zeros((pad, HEAD), F32)
        dxc_s[s:s + pad, :] = jnp.zeros((pad, HEAD), F32)
        wab, wxb = wa_ref[...].astype(BF16), wx_ref[...].astype(BF16)
        lam_v = lam_ref[...]
        sp = _softplus_neg(lam_v)

        def conv_in(c):
            base = c * tc
            wins = [pad_s[pl.ds(base + pad - (CONV_TAPS - 1) + i, tc), :] for i in range(CONV_TAPS)]
            xc = bc_ref[...] + sum(wc_ref[i:i + 1, :] * wins[i] for i in range(CONV_TAPS))
            return xc, wins

        for c in range(s // tc):
            sl = slice(c * tc, (c + 1) * tc)
            xc, _ = conv_in(c)
            _, _, a, _ = _lru_gates(xc, wab, ba_ref[...], wxb, bx_ref[...], sp)
            a_s[sl, :] = a
            gel, dgel = _gelu_parts(xg_ref[sl, :])
            dov = do_ref[sl, :]
            g_s[sl, :] = dov * gel
            dxg_ref[sl, :] = (dov * h_ref[sl, :] * dgel).astype(BF16)

        row = lax.broadcasted_iota(jnp.int32, (SUBLANES, HEAD), 0)
        n_chunks = s // SUBLANES

        def group(it, gnext):
            offs = [pl.multiple_of((n_chunks - 1 - (it * SCAN_GROUP + q)) * SUBLANES, SUBLANES) for q in range(SCAN_GROUP)]
            cg = []
            for off in offs:
                a8 = a_s[pl.ds(off, SUBLANES), :]
                a8n = a_s[pl.ds(off + SUBLANES, SUBLANES), :]
                c8 = pltpu.roll(jnp.where(row == 0, a8n, a8), SUBLANES - 1, 0)
                g8 = g_s[pl.ds(off, SUBLANES), :]
                for d in (1, 2, 4):
                    c_sh = jnp.where(row >= SUBLANES - d, 1.0, pltpu.roll(c8, SUBLANES - d, 0))
                    g_sh = jnp.where(row >= SUBLANES - d, 0.0, pltpu.roll(g8, SUBLANES - d, 0))
                    g8 = c8 * g_sh + g8
                    c8 = c8 * c_sh
                cg.append((c8, g8))
            enters = []
            for c8, g8 in cg:
                enters.append(gnext)
                gnext = g8[0:1, :] + c8[0:1, :] * gnext
            for off, (c8, g8), g0 in zip(offs, cg, enters):
                gv = g8 + c8 * g0
                g_s[pl.ds(off, SUBLANES), :] = gv
                h8 = hp_s[pl.ds(off + pad, SUBLANES), :]
                h8p = hp_s[pl.ds(off, SUBLANES), :]
                da_s[pl.ds(off, SUBLANES), :] = gv * pltpu.roll(jnp.where(row == SUBLANES - 1, h8p, h8), 1, 0)
            return gnext

        lax.fori_loop(0, n_chunks // SCAN_GROUP, group, jnp.zeros((1, HEAD), F32))

        dsp = jnp.zeros((1, HEAD), F32)
        dbc = jnp.zeros((1, HEAD), F32)
        dba = jnp.zeros((1, HEAD), F32)
        dbx = jnp.zeros((1, HEAD), F32)
        dwa = jnp.zeros((HEAD, HEAD), F32)
        dwx = jnp.zeros((HEAD, HEAD), F32)
        dwc = [jnp.zeros((1, HEAD), F32) for _ in range(CONV_TAPS)]
        for c in range(s // tc):
            sl = slice(c * tc, (c + 1) * tc)
            xc, wins = conv_in(c)
            r, ig, a, mult = _lru_gates(xc, wab, ba_ref[...], wxb, bx_ref[...], sp)
            du, da = g_s[sl, :], da_s[sl, :]
            d_ix = du * mult
            dla = da * a - (du * ig * xc) * (a * a / mult)
            dsp = dsp + jnp.sum(dla * r, axis=0, keepdims=True) * (-LRU_C)
            dpa = (dla * (-LRU_C * sp)) * r * (1.0 - r)
            dpx = (d_ix * xc) * ig * (1.0 - ig)
            dpab, dpxb, xb = dpa.astype(BF16), dpx.astype(BF16), xc.astype(BF16)
            dxc = d_ix * ig + _dot_nt(dpab, wab) + _dot_nt(dpxb, wxb)
            dwa = dwa + _dot_tn(xb, dpab)
            dwx = dwx + _dot_tn(xb, dpxb)
            dba = dba + jnp.sum(dpa, axis=0, keepdims=True)
            dbx = dbx + jnp.sum(dpx, axis=0, keepdims=True)
            dbc = dbc + jnp.sum(dxc, axis=0, keepdims=True)
            for i in range(CONV_TAPS):
                dwc[i] = dwc[i] + jnp.sum(dxc * wins[i], axis=0, keepdims=True)
            dxc_s[sl, :] = dxc

        for c in range(s // tc):
            base = c * tc
            dxr = sum(wc_ref[i:i + 1, :] * dxc_s[pl.ds(base + (CONV_TAPS - 1) - i, tc), :] for i in range(CONV_TAPS))
            dxr_ref[base:base + tc, :] = dxr.astype(BF16)

        for i in range(CONV_TAPS):
            dwc_ref[i:i + 1, :] = dwc[i]
        dbc_ref[...] = dbc
        dwa_ref[...] = dwa
        dwx_ref[...] = dwx
        dba_ref[...] = dba
        dbx_ref[...] = dbx
        dlam_ref[...] = dsp * (-_sigmoid(-lam_v))

    w = nb * HEAD
    return pl.pallas_call(
        _per_block(one_block, 20, (6, 8, 15, 17), bps), grid=(nb // bps,),
        in_specs=[seq0, seq1, seq0, seq0, taps, vec, mat, vec, mat, vec, vec],
        out_specs=(seq0, seq0, taps, vec, mat, vec, mat, vec, vec),
        out_shape=(SDS((s, w), BF16), SDS((s, w), BF16), SDS((CONV_TAPS, w), F32), SDS((1, w), F32),
                   SDS((nb, HEAD, HEAD), F32), SDS((1, w), F32), SDS((nb, HEAD, HEAD), F32), SDS((1, w), F32),
                   SDS((1, w), F32)),
        scratch_shapes=[pltpu.VMEM((s + pad, HEAD), F32), pltpu.VMEM((s + pad, HEAD), F32),
                        pltpu.VMEM((s + pad, HEAD), F32), pltpu.VMEM((s, HEAD), F32),
                        pltpu.VMEM((s, HEAD), F32), pltpu.VMEM((s + pad, HEAD), F32)],
        compiler_params=_params(("parallel",)), name="lru_bwd",
    )(xrg, xrg, dol, hseq, wconv, bconv, wa, ba, wx, bx, lam)


def _ada_mod(c_all, w_sh, b_sh):
    n_ex, d = c_all.shape
    n = w_sh.shape[1]
    tn = _tile(n, 512)

    def body(c_ref, w_ref, b_ref, mod_ref, act_ref):
        cv = c_ref[...]
        act = cv * _sigmoid(cv)
        act_ref[...] = act
        mod_ref[...] = _dot(act.astype(BF16), w_ref[...].astype(BF16)) + b_ref[...]

    return pl.pallas_call(
        body, grid=(n // tn,),
        in_specs=[pl.BlockSpec((n_ex, d), lambda j: (0, 0)), pl.BlockSpec((d, tn), lambda j: (0, j)),
                  pl.BlockSpec((1, tn), lambda j: (0, j))],
        out_specs=(pl.BlockSpec((n_ex, tn), lambda j: (0, j)), pl.BlockSpec((n_ex, d), lambda j: (0, 0))),
        out_shape=(SDS((n_ex, n), F32), SDS((n_ex, d), F32)),
        compiler_params=_params(("arbitrary",)), name="ada_mod")(c_all, w_sh, b_sh)


def _adamw_math(w, g, m, v):
    m = ADAM_B1 * m + (1.0 - ADAM_B1) * g
    v = ADAM_B2 * v + (1.0 - ADAM_B2) * (g * g)
    m_hat = m / (1.0 - ADAM_B1 ** ADAM_STEP)
    v_hat = v / (1.0 - ADAM_B2 ** ADAM_STEP)
    delta = -ADAM_LR * (m_hat / (jnp.sqrt(v_hat) + ADAM_EPS) + ADAM_WD * w)
    return delta, m, v


def _adamw_plain(name, w, g, m, v):
    def body(w_ref, g_ref, m_ref, v_ref, d_ref, mo_ref, vo_ref):
        d_ref[...], mo_ref[...], vo_ref[...] = _adamw_math(w_ref[...], g_ref[...], m_ref[...], v_ref[...])

    return pl.pallas_call(body, out_shape=(SDS(w.shape, F32),) * 3, name=name)(w, g, m, v)


def _adamw_halves(name, c_arr, w, m, v, g_own, g_recv):
    r, n = w.shape
    rh = r // 2
    tr = _tile(rh, 256)
    nh = rh // tr

    def body(c_ref, w_ref, m_ref, v_ref, go_ref, gr_ref, g_ref, d_ref, mo_ref, vo_ref):
        own = (pl.program_id(0) // nh) == c_ref[0]
        g = jnp.where(own, go_ref[...], gr_ref[...])
        g_ref[...] = g
        d_ref[...], mo_ref[...], vo_ref[...] = _adamw_math(w_ref[...], g, m_ref[...], v_ref[...])

    full = pl.BlockSpec((tr, n), lambda i, c: (i, 0))
    own = pl.BlockSpec((tr, n), lambda i, c: (jnp.where(i // nh == c[0], i % nh, 0), 0))
    recv = pl.BlockSpec((tr, n), lambda i, c: (jnp.where(i // nh == c[0], 0, i % nh), 0))
    return pl.pallas_call(
        body,
        grid_spec=pltpu.PrefetchScalarGridSpec(
            num_scalar_prefetch=1, grid=(2 * nh,), in_specs=[full, full, full, own, recv],
            out_specs=(full,) * 4),
        out_shape=(SDS((r, n), F32),) * 4,
        compiler_params=_params(("parallel",)), name=name)(c_arr, w, m, v, g_own, g_recv)


def _adamw_ada(w, m, v, act_t, dmod):
    d, n = w.shape
    n_ex = act_t.shape[1]
    tr = _tile(d, 256)

    def body(a_ref, dm_ref, w_ref, m_ref, v_ref, g_ref, d_ref, mo_ref, vo_ref):
        g = _dot(a_ref[...], dm_ref[...])
        g_ref[...] = g
        d_ref[...], mo_ref[...], vo_ref[...] = _adamw_math(w_ref[...], g, m_ref[...], v_ref[...])

    full = pl.BlockSpec((tr, n), lambda i: (i, 0))
    return pl.pallas_call(
        body, grid=(d // tr,),
        in_specs=[pl.BlockSpec((tr, n_ex), lambda i: (i, 0)), pl.BlockSpec((n_ex, n), lambda i: (0, 0)), full, full, full],
        out_specs=(full,) * 4, out_shape=(SDS((d, n), F32),) * 4,
        compiler_params=_params(("parallel",)), name="adamw_ada")(act_t, dmod, w, m, v)


def _small_reduce_adamw(parts, w, m, v):
    n_dev, r, _ = parts.shape
    tr = r if r <= PACK_ROWS else PACK_ROWS

    def body(p_ref, w_ref, m_ref, v_ref, g_ref, d_ref, mo_ref, vo_ref):
        g = p_ref[0]
        for k in range(1, n_dev):
            g = g + p_ref[k]
        g_ref[...] = g
        d_ref[...], mo_ref[...], vo_ref[...] = _adamw_math(w_ref[...], g, m_ref[...], v_ref[...])

    full = pl.BlockSpec((tr, LANES), lambda i: (i, 0))
    return pl.pallas_call(
        body, grid=(r // tr,),
        in_specs=[pl.BlockSpec((n_dev, tr, LANES), lambda i: (0, i, 0)), full, full, full],
        out_specs=(full,) * 4, out_shape=(SDS((r, LANES), F32),) * 4,
        compiler_params=_params(("parallel",)), name="small_reduce_adamw")(parts, w, m, v)


def _mesh_pos():
    return lax.axis_index("x"), lax.axis_index("y"), lax.axis_index("c")


def _other_chips(x, y):
    return [(1 - x, y), (x, 1 - y), (1 - x, 1 - y)]


def _all_gather_small(name, blk, after=()):
    r, n = blk.shape
    n_after = len(after)

    def body(x_ref, *rest):
        out_ref, send_sems, recv_sems, local_sem = rest[n_after:]
        x, y, c = _mesh_pos()
        me, sibling = (x, y, c), (x, y, 1 - c)
        chips = _other_chips(x, y)

        def rows(px, py, pc):
            return out_ref.at[4 * px + 2 * py + pc]

        def copy(k, block, to, src=None):
            return pltpu.make_async_remote_copy(
                src_ref=rows(*block) if src is None else src, dst_ref=rows(*block),
                send_sem=send_sems.at[k], recv_sem=recv_sems.at[k], device_id=to, device_id_type=MESH)

        mine = pltpu.make_async_copy(x_ref, rows(*me), local_sem)
        mine.start()
        first = [copy(0, me, sibling, src=x_ref)]
        first += [copy(1 + j, me, (*chip, c), src=x_ref) for j, chip in enumerate(chips)]
        for cp in first:
            cp.start()
        passed = [copy(4 + j, (*chip, c), sibling) for j, chip in enumerate(chips)]
        for j, chip in enumerate(chips):
            copy(1 + j, (*chip, c), me).wait_recv()
            passed[j].start()
        copy(0, sibling, me).wait_recv()
        for j, chip in enumerate(chips):
            copy(4 + j, (*chip, 1 - c), me).wait_recv()
        for cp in first + passed:
            cp.wait_send()
        mine.wait()

    return pl.pallas_call(
        body, out_shape=SDS((N_DEV, r, n), blk.dtype),
        in_specs=[pl.BlockSpec(memory_space=pltpu.VMEM)] + [pl.BlockSpec(memory_space=pl.ANY)] * n_after,
        out_specs=pl.BlockSpec(memory_space=pltpu.VMEM),
        scratch_shapes=[pltpu.SemaphoreType.DMA((7,)), pltpu.SemaphoreType.DMA((7,)), pltpu.SemaphoreType.DMA],
        compiler_params=pltpu.CompilerParams(vmem_limit_bytes=VMEM_LIMIT), name=name)(blk, *after)


_ANY = pl.BlockSpec(memory_space=pl.ANY)
_HBM = pl.BlockSpec(memory_space=pltpu.HBM)
_SEM = pl.BlockSpec(memory_space=pltpu.SEMAPHORE)
_EFFECT = pltpu.SideEffectType.DATAFLOW_SIDE_EFFECTING


def _hbm(a):
    return pltpu.with_memory_space_constraint(a, pltpu.HBM)


def _place_cast(name, j_arr, shard, kind, after):
    r, n = shard.shape
    tr = _tile(r, 256)
    nr = r // tr
    if kind == "col":
        out_shape, o_spec = (r, N_CHIP * n), pl.BlockSpec((tr, n), lambda i, j: (i, j[0]))
    else:
        out_shape, o_spec = (N_CHIP * r, n), pl.BlockSpec((tr, n), lambda i, j: (j[0] * nr + i, 0))

    def body(j_ref, s_ref, after_ref, o_ref, tok_ref):
        o_ref[...] = s_ref[...].astype(BF16)
        tok_ref[...] = jnp.zeros_like(tok_ref)

    return pl.pallas_call(
        body,
        grid_spec=pltpu.PrefetchScalarGridSpec(
            num_scalar_prefetch=1, grid=(nr,), in_specs=[pl.BlockSpec((tr, n), lambda i, j: (i, 0)), _ANY],
            out_specs=(o_spec, pl.BlockSpec((SUBLANES, LANES), lambda i, j: (0, 0)))),
        out_shape=(SDS(out_shape, BF16), SDS((SUBLANES, LANES), F32)),
        compiler_params=_params(("arbitrary",)), name=name)(j_arr, shard, after)


def _leg_direct(full, kind, x, y, c):
    mine = _full_region(full, kind, x, y, c)
    return [(mine, mine, (1 - x, y, c)), (mine, mine, (x, 1 - y, c))]


def _leg_relay(full, kind, x, y, c):
    fx, fy = jnp.where(c == 0, 1 - x, x), jnp.where(c == 0, y, 1 - y)
    tx, ty = jnp.where(c == 0, x, 1 - x), jnp.where(c == 0, 1 - y, y)
    got = _full_region(full, kind, fx, fy, c)
    return [(got, got, (tx, ty, c))]


def _leg_d2d(which):
    def leg(full, kind, x, y, c):
        chips = _other_chips(x, y)
        return [(_full_region(full, kind, *chips[k], c), _full_region(full, kind, *chips[k], c), (x, y, 1 - c))
                for k in which]
    return leg


_LEGS = {"direct": (_leg_direct, 2), "relay": (_leg_relay, 1), "d2d_near": (_leg_d2d((0, 1)), 2),
         "d2d_far": (_leg_d2d((2,)), 1)}


def _gather_call(name, fulls, kinds, waits, starts, after, thru):
    nw, n_wait, n_start = len(fulls), len(waits), len(starts)

    def body(*refs):
        wait_sems = refs[nw:nw + 2 * n_wait]
        outs = refs[nw + 2 * n_wait + 2:]
        full, start_sems = outs[:nw], outs[nw:nw + 2 * n_start]
        x, y, c = _mesh_pos()
        for i, (leg, ws, _, _) in enumerate(waits):
            fn, per = _LEGS[leg]
            for li, w in enumerate(ws):
                for k, (s_, d_, dev) in enumerate(fn(full[w], kinds[w], x, y, c)):
                    cp = pltpu.make_async_remote_copy(
                        src_ref=s_, dst_ref=d_, send_sem=wait_sems[2 * i].at[per * li + k],
                        recv_sem=wait_sems[2 * i + 1].at[per * li + k], device_id=dev, device_id_type=MESH)
                    cp.wait_recv()
                    cp.wait_send()
        for i, (leg, ws) in enumerate(starts):
            fn, per = _LEGS[leg]
            for li, w in enumerate(ws):
                for k, (s_, d_, dev) in enumerate(fn(full[w], kinds[w], x, y, c)):
                    pltpu.make_async_remote_copy(
                        src_ref=s_, dst_ref=d_, send_sem=start_sems[2 * i].at[per * li + k],
                        recv_sem=start_sems[2 * i + 1].at[per * li + k], device_id=dev, device_id_type=MESH).start()

    sems = []
    for leg, ws in starts:
        sems += [pltpu.SemaphoreType.DMA((_LEGS[leg][1] * len(ws),))] * 2
    wait_args = []
    for _, _, s_, r_ in waits:
        wait_args += [s_, r_]
    outs = pl.pallas_call(
        body,
        out_shape=tuple(pltpu.HBM(f_.shape, f_.dtype) for f_ in fulls) + tuple(sems) + (SDS(thru.shape, thru.dtype),),
        in_specs=[_HBM] * nw + [_SEM] * (2 * n_wait) + [_ANY, _ANY],
        out_specs=tuple([_HBM] * nw + [_SEM] * (2 * n_start) + [_ANY]),
        input_output_aliases={**{w: w for w in range(nw)}, nw + 2 * n_wait + 1: nw + 2 * n_start},
        compiler_params=pltpu.CompilerParams(has_side_effects=_EFFECT),
        name=name,
    )(*[_hbm(f_) for f_ in fulls], *wait_args, after, thru)
    pairs = [(outs[nw + 2 * i], outs[nw + 2 * i + 1]) for i in range(n_start)]
    return list(outs[:nw]), pairs, outs[nw + 2 * n_start]


def _full_region(full, kind, px, py, half):
    j = 2 * px + py
    if kind == "col":
        rh, cols = full.shape[0] // 2, full.shape[1] // N_CHIP
        return full.at[pl.ds(half * rh, rh), pl.ds(j * cols, cols)]
    rows = full.shape[0] // N_CHIP
    rh = rows // 2
    return full.at[pl.ds(j * rows + half * rh, rh), :]


def _plan_scatter(kind):
    def plan(src, land, x, y, c):
        out = []
        for k, (px, py) in enumerate(_other_chips(x, y)):
            j = 2 * px + py
            if kind == "col":
                n = src.shape[1] // N_CHIP
                blk = src.at[:, pl.ds(j * n, n)]
            else:
                blk = src.at[j]
            out.append((blk, land.at[k], (px, py, c)))
        return out
    return plan


def _plan_whole(src, land, x, y, c):
    return [(src, land, (x, y, 1 - c))]


def _split_start(name, src, land_shape, n, plan, thru):
    def body(src_in, land_in, thru_in, send, recv, src_ref, land_ref, thru_out):
        x, y, c = _mesh_pos()
        for k, (s_, d_, dev) in enumerate(plan(src_ref, land_ref, x, y, c)):
            pltpu.make_async_remote_copy(src_ref=s_, dst_ref=d_, send_sem=send.at[k], recv_sem=recv.at[k],
                                         device_id=dev, device_id_type=MESH).start()

    sem = pltpu.SemaphoreType.DMA((n,))
    return pl.pallas_call(
        body,
        out_shape=(sem, sem, pltpu.HBM(src.shape, src.dtype), pltpu.HBM(land_shape, src.dtype), SDS(thru.shape, thru.dtype)),
        in_specs=[_HBM, _HBM, _ANY], out_specs=(_SEM, _SEM, _HBM, _HBM, _ANY),
        input_output_aliases={0: 2, 1: 3, 2: 4},
        compiler_params=pltpu.CompilerParams(has_side_effects=_EFFECT), name=name,
    )(_hbm(src), _hbm(lax.empty(land_shape, src.dtype)), thru)


def _split_wait(name, send, recv, src, land, plan, after):
    def body(src_in, land_in, send_r, recv_r, after_r, src_ref, land_ref):
        x, y, c = _mesh_pos()
        for k, (s_, d_, dev) in enumerate(plan(src_ref, land_ref, x, y, c)):
            cp = pltpu.make_async_remote_copy(src_ref=s_, dst_ref=d_, send_sem=send_r.at[k], recv_sem=recv_r.at[k],
                                              device_id=dev, device_id_type=MESH)
            cp.wait_send()
            cp.wait_recv()

    return pl.pallas_call(
        body,
        out_shape=(pltpu.HBM(src.shape, src.dtype), pltpu.HBM(land.shape, land.dtype)),
        in_specs=[_HBM, _HBM, _SEM, _SEM, _ANY], out_specs=(_HBM, _HBM),
        input_output_aliases={0: 0, 1: 1},
        compiler_params=pltpu.CompilerParams(has_side_effects=_EFFECT), name=name,
    )(src, land, send, recv, after)


def _dev_row(buf, px, py, pc):
    return buf.at[4 * px + 2 * py + pc]


def _plan_gather_own(buf, land, x, y, c):
    own = _dev_row(buf, x, y, c)
    return [(own, own, (x, y, 1 - c))] + [(own, own, (px, py, c)) for px, py in _other_chips(x, y)]


def _plan_gather_pass(buf, land, x, y, c):
    return [(_dev_row(buf, px, py, c), _dev_row(buf, px, py, c), (x, y, 1 - c)) for px, py in _other_chips(x, y)]


def _split_start_inplace(name, buf, n, plan, thru):
    def body(buf_in, thru_in, send, recv, buf_ref, thru_out):
        x, y, c = _mesh_pos()
        for k, (s_, d_, dev) in enumerate(plan(buf_ref, buf_ref, x, y, c)):
            pltpu.make_async_remote_copy(src_ref=s_, dst_ref=d_, send_sem=send.at[k], recv_sem=recv.at[k],
                                         device_id=dev, device_id_type=MESH).start()

    sem = pltpu.SemaphoreType.DMA((n,))
    return pl.pallas_call(
        body, out_shape=(sem, sem, pltpu.HBM(buf.shape, buf.dtype), SDS(thru.shape, thru.dtype)),
        in_specs=[_HBM, _ANY], out_specs=(_SEM, _SEM, _HBM, _ANY), input_output_aliases={0: 2, 1: 3},
        compiler_params=pltpu.CompilerParams(has_side_effects=_EFFECT), name=name)(_hbm(buf), thru)


def _split_wait_inplace(name, send, recv, buf, plan, after):
    def body(buf_in, send_r, recv_r, after_r, buf_ref):
        x, y, c = _mesh_pos()
        for k, (s_, d_, dev) in enumerate(plan(buf_ref, buf_ref, x, y, c)):
            cp = pltpu.make_async_remote_copy(src_ref=s_, dst_ref=d_, send_sem=send_r.at[k], recv_sem=recv_r.at[k],
                                              device_id=dev, device_id_type=MESH)
            cp.wait_send()
            cp.wait_recv()

    return pl.pallas_call(
        body, out_shape=pltpu.HBM(buf.shape, buf.dtype), in_specs=[_HBM, _SEM, _SEM, _ANY], out_specs=_HBM,
        input_output_aliases={0: 0}, compiler_params=pltpu.CompilerParams(has_side_effects=_EFFECT), name=name,
    )(buf, send, recv, after)


def _place_row(name, me_arr, slab):
    r, n = slab.shape
    tr = r if r <= PACK_ROWS else PACK_ROWS

    def body(me_ref, s_ref, o_ref):
        o_ref[...] = s_ref[...]

    return pl.pallas_call(
        body,
        grid_spec=pltpu.PrefetchScalarGridSpec(
            num_scalar_prefetch=1, grid=(r // tr,), in_specs=[pl.BlockSpec((tr, n), lambda i, me: (i, 0))],
            out_specs=pl.BlockSpec((None, tr, n), lambda i, me: (me[0], i, 0))),
        out_shape=SDS((N_DEV, r, n), slab.dtype), compiler_params=_params(("parallel",)), name=name)(me_arr, slab)


def _sum_partials(name, j_arr, part, got, kind):
    _, rh, n = got.shape
    tr = _tile(rh, 256)
    if kind == "col":
        p_spec = pl.BlockSpec((tr, n), lambda i, j: (i, j[0]))
    else:
        p_spec = pl.BlockSpec((None, tr, n), lambda i, j: (j[0], i, 0))

    def body(j_ref, p_ref, r_ref, o_ref):
        o_ref[...] = ((p_ref[...].astype(F32) + r_ref[0].astype(F32)) + r_ref[1].astype(F32)) + r_ref[2].astype(F32)

    return pl.pallas_call(
        body,
        grid_spec=pltpu.PrefetchScalarGridSpec(
            num_scalar_prefetch=1, grid=(rh // tr,),
            in_specs=[p_spec, pl.BlockSpec((3, tr, n), lambda i, j: (0, i, 0))],
            out_specs=pl.BlockSpec((tr, n), lambda i, j: (i, 0))),
        out_shape=SDS((rh, n), F32),
        compiler_params=_params(("parallel",)), name=name)(j_arr, part, got)


def _swap_reduced(name, halves):
    nw = len(halves)

    def body(*refs):
        h, got = refs[:nw], refs[nw:2 * nw]
        send_sems, recv_sems = refs[2 * nw:]
        x, y, c = _mesh_pos()
        cps = []
        for w in range(nw):
            cp = pltpu.make_async_remote_copy(
                src_ref=h[w], dst_ref=got[w], send_sem=send_sems.at[w], recv_sem=recv_sems.at[w],
                device_id=(x, y, 1 - c), device_id_type=MESH)
            cp.start()
            cps.append(cp)
        for cp in cps:
            cp.wait()

    return pl.pallas_call(
        body, out_shape=tuple(SDS(h.shape, h.dtype) for h in halves),
        in_specs=[_ANY] * nw, out_specs=tuple([_ANY] * nw),
        scratch_shapes=[pltpu.SemaphoreType.DMA((nw,)), pltpu.SemaphoreType.DMA((nw,))],
        name=name)(*halves)


def _pack(arrays):
    flat = [a.reshape(-1).astype(F32) for a in arrays]
    flat = [jnp.pad(f, (0, (-f.shape[0]) % LANES)) for f in flat]
    sizes = [f.shape[0] for f in flat]
    total = sum(sizes)
    rows = total // LANES
    tail = LANES * ((-rows) % (PACK_ROWS if rows > PACK_ROWS else SUBLANES))
    if tail:
        flat.append(jnp.zeros((tail,), F32))
    return jnp.concatenate(flat).reshape(-1, LANES), sizes


def _unpack(slab, sizes, shapes, lead=()):
    flat = slab.reshape(lead + (-1,))
    out, off = [], 0
    for sz, shp in zip(sizes, shapes):
        n = math.prod(shp)
        out.append(flat[..., off:off + n].reshape(lead + tuple(shp)))
        off += sz
    return out


def kernel(x, c, w_ada, b_ada, g_norm_mix, w_in, w_conv, b_conv, w_rg_a, b_rg_a, w_rg_x, b_rg_x, lru_lambda, g_attn_out, g_lru_out, w_out, g_norm_mlp, w_mlp_in, w_mlp_out, g_norm_final, loss_target, m_w_ada, m_b_ada, m_g_norm_mix, m_w_in, m_w_conv, m_b_conv, m_w_rg_a, m_b_rg_a, m_w_rg_x, m_b_rg_x, m_lru_lambda, m_g_attn_out, m_g_lru_out, m_w_out, m_g_norm_mlp, m_w_mlp_in, m_w_mlp_out, m_g_norm_final, v_w_ada, v_b_ada, v_g_norm_mix, v_w_in, v_w_conv, v_b_conv, v_w_rg_a, v_b_rg_a, v_w_rg_x, v_b_rg_x, v_lru_lambda, v_g_attn_out, v_g_lru_out, v_w_out, v_g_norm_mlp, v_w_mlp_in, v_w_mlp_out, v_g_norm_final):
    s, d = x.shape[1], x.shape[2]
    aw = d // 2
    nh = aw // HEAD
    f = w_mlp_out.shape[1] * N_CHIP
    n_ada = w_ada.shape[2]
    n_cv = w_conv.shape[2]
    ix, iy, ic = lax.axis_index("x"), lax.axis_index("y"), lax.axis_index("c")
    chip = 2 * ix + iy
    me = 2 * chip + ic
    c_arr = jnp.reshape(ic, (1,)).astype(jnp.int32)
    j_arr = jnp.reshape(chip, (1,)).astype(jnp.int32)

    x2d, tgt = x[0], loss_target[0]

    k_in, k_out, k_mi, k_mo = kinds = ("col", "row", "col", "row")
    slab, sizes = _pack([c, w_conv])
    p_in, _ = _place_cast("place_cast_0", j_arr, w_in[0], k_in, c)
    (f_in,), (dir_in,), slab = _gather_call("gather_0", [p_in], [k_in], [], [("direct", [0])], c, slab)
    p_out, tok = _place_cast("place_cast_1", j_arr, w_out[0], k_out, slab)
    p_mi, tok = _place_cast("place_cast_2", j_arr, w_mlp_in[0], k_mi, tok)
    p_mo, tok = _place_cast("place_cast_3", j_arr, w_mlp_out[0], k_mo, tok)

    gathered = _all_gather_small("comm_gather_cond", slab, after=(tok,))
    c_parts, cv_parts = _unpack(gathered, sizes, [(d,), (CONV_TAPS, n_cv)], lead=(N_DEV,))
    c_all = c_parts
    w_conv_full = jnp.concatenate([cv_parts[2 * j] for j in range(N_CHIP)], axis=-1)
    b_sh = lax.dynamic_slice(b_ada, (0, chip * n_ada), (1, n_ada))
    (f_in,), (rel_in, near_in), c_all = _gather_call(
        "gather_1", [f_in], [k_in], [("direct", [0], *dir_in)], [("relay", [0]), ("d2d_near", [0])], gathered, c_all)
    mod_part, act_all = _ada_mod(c_all, w_ada[0], b_sh)
    mod_g = _all_gather_small("comm_gather_mod", mod_part.reshape(-1, LANES))
    mod_g = mod_g.reshape(N_DEV, N_DEV, n_ada)
    mod = jnp.concatenate([lax.dynamic_index_in_dim(mod_g[2 * j], me, 0, keepdims=True) for j in range(N_CHIP)], axis=-1)
    sh1, sc1, gt1, sh2, sc2, gt2 = [mod[:, k * d:(k + 1) * d] for k in range(N_MOD)]

    (f_in, f_out, f_mi, f_mo), (far_in, dir_om, dir_mo), sh1 = _gather_call(
        "gather_2", [f_in, p_out, p_mi, p_mo], kinds, [("relay", [0], *rel_in)],
        [("d2d_far", [0]), ("direct", [1, 2]), ("direct", [3])], mod, sh1)

    h1, rstd1 = _norm_mod_fwd("norm_mod_fwd1", x2d, g_norm_mix, sc1, sh1)
    (w_in_f,), _, h1 = _gather_call("gather_3", [f_in], [k_in],
                                    [("d2d_near", [0], *near_in), ("d2d_far", [0], *far_in)], [], rstd1, h1)
    (qkv,) = _matmul("mm_qkv", h1, w_in_f, "nn", s, 3 * aw, d, (BF16,))
    (xrg,) = _matmul("mm_xrg", h1, w_in_f, "nn", s, 2 * aw, d, (F32,), b_off=3 * aw)
    o_attn, attn_w, attn_sg = _attn_fwd(qkv, nh)
    (f_out, f_mi), (rel_om, near_om), xrg = _gather_call(
        "gather_4", [f_out, f_mi], [k_out, k_mi], [("direct", [0, 1], *dir_om)],
        [("relay", [0, 1]), ("d2d_near", [0, 1])], o_attn, xrg)
    wa3, wx3 = w_rg_a[0], w_rg_x[0]
    o_lru, hseq = _lru_fwd(xrg, w_conv_full, b_conv, wa3, b_rg_a, wx3, b_rg_x, lru_lambda)
    mixed, rstd_a, rstd_l = _mix_norm_fwd(o_attn, o_lru, g_attn_out, g_lru_out)
    (f_out, f_mi, f_mo), (far_om, rel_mo, near_mo), mixed = _gather_call(
        "gather_5", [f_out, f_mi, f_mo], [k_out, k_mi, k_mo], [("relay", [0, 1], *rel_om), ("direct", [2], *dir_mo)],
        [("d2d_far", [0, 1]), ("relay", [2]), ("d2d_near", [2])], rstd_a, mixed)
    (w_out_f, w_mi_f), _, mixed = _gather_call(
        "gather_6", [f_out, f_mi], [k_out, k_mi], [("d2d_near", [0, 1], *near_om), ("d2d_far", [0, 1], *far_om)], [],
        rstd_l, mixed)

    def residual(acc, xin, gt):
        return acc, xin + gt * acc

    y1, x1 = _matmul("mm_out", mixed, w_out_f, "nn", s, d, d, (BF16, F32), extras=(x2d, gt1),
                     extra_kinds=("tile", "row"), epilogue=residual)
    h2, rstd2 = _norm_mod_fwd("norm_mod_fwd2", x1, g_norm_mlp, sc2, sh2)
    (f_mo,), (far_mo,), h2 = _gather_call("gather_7", [f_mo], [k_mo], [("relay", [0], *rel_mo)],
                                          [("d2d_far", [0])], rstd2, h2)

    def sq_relu(acc):
        r = jnp.maximum(acc, 0.0)
        return 2.0 * r, r * r

    r2, hid = _matmul("mm_mlp_in", h2, w_mi_f, "nn", s, f, d, (BF16, BF16), epilogue=sq_relu)
    (w_mo_f,), _, hid = _gather_call("gather_8", [f_mo], [k_mo],
                                     [("d2d_near", [0], *near_mo), ("d2d_far", [0], *far_mo)], [], r2, hid)
    y2, x2 = _matmul("mm_mlp_out", hid, w_mo_f, "nn", s, d, f, (BF16, F32), extras=(x1, gt2),
                     extra_kinds=("tile", "row"), epilogue=residual)
    dx2, loss_row, dg_final, dy2, dgt2 = _final_loss(x2, g_norm_final.reshape(1, d), tgt, y2, gt2)

    oc_arr = 1 - c_arr

    def dw_half(name, st, h_arr, got=None):
        add = {} if got is None else dict(extras=(got,), extra_kinds=("tile",),
                                          epilogue=lambda acc, g_: (acc + g_.astype(F32),))
        (out,) = _matmul(name, st["a"], st["dy"], "tn", st["m"], st["n"], s, (BF16,), tm=st["tm"], m_half=h_arr, **add)
        return out

    def rs_begin(tag, kind, xa, dy, m, n, thru):
        st = {"tag": tag, "kind": kind, "a": xa, "dy": dy, "m": m, "n": n,
              "tm": m // (2 * N_CHIP) if kind == "row" else m // 2}
        first = dw_half("mm_dw_%s_a" % tag, st, oc_arr)
        send, recv, first, land, thru = _split_start("rs_swap_start_" + tag, first, first.shape, 1, _plan_whole, thru)
        st["swap"] = (send, recv, first, land)
        return st, thru

    def rs_mid(st, after, thru):
        tag, kind = st["tag"], st["kind"]
        _, got = _split_wait("rs_swap_wait_" + tag, *st["swap"], _plan_whole, after)
        part = dw_half("mm_dw_%s_b" % tag, st, c_arr, got)
        if kind == "row":
            part = part.reshape(N_CHIP, st["m"] // (2 * N_CHIP), st["n"])
        blk = (part.shape[0], part.shape[1] // N_CHIP) if kind == "col" else part.shape[1:]
        send, recv, part, land, thru = _split_start("rs_scatter_start_" + tag, part, (N_CHIP - 1,) + blk, N_CHIP - 1,
                                                    _plan_scatter(kind), thru)
        st["scatter"] = (send, recv, part, land)
        return thru

    def rs_end(st, after):
        tag, kind = st["tag"], st["kind"]
        part, got = _split_wait("rs_scatter_wait_" + tag, *st["scatter"], _plan_scatter(kind), after)
        return _sum_partials("sum_partials_" + tag, j_arr, part, got, kind)

    (dpre,) = _matmul("mm_dhid", dy2, w_mo_f, "nt", s, f, d, (BF16,), extras=(r2,), extra_kinds=("tile",),
                      epilogue=lambda acc, r: (acc * r.astype(F32),))
    st_mo, dpre = rs_begin("mo", "row", hid, dy2, f, d, dpre)
    (dh2,) = _matmul("mm_dh2", dpre, w_mi_f, "nt", s, d, f, (BF16,))
    dh2 = rs_mid(st_mo, dh2, dh2)
    st_mi, dh2 = rs_begin("mi", "col", h2, dpre, d, f, dh2)
    dx1, dsh2, dsc2, dg_mlp, dy1, dgt1 = _norm_mod_bwd("norm_mod_bwd2", dh2, x1, rstd2, g_norm_mlp, sc2, dx2,
                                                       gate=(y1, gt1))
    (dmixed,) = _matmul("mm_dmixed", dy1, w_out_f, "nt", s, d, d, (BF16,))
    dmixed = rs_mid(st_mi, dmixed, dmixed)
    st_out, dmixed = rs_begin("out", "row", mixed, dy1, d, d, dmixed)
    do_attn, do_lru, dg_attn, dg_lru = _mix_norm_bwd(dmixed, o_attn, o_lru, rstd_a, rstd_l, g_attn_out, g_lru_out)
    dq, dk, dv = _attn_bwd(qkv, do_attn, attn_w, attn_sg, nh)
    do_lru = rs_mid(st_out, dq, do_lru)
    dxr, dxg, dwconv, dbconv, dwa, dba, dwx, dbx, dlam = _lru_bwd(
        xrg, do_lru, hseq, w_conv_full, b_conv, wa3, b_rg_a, wx3, b_rg_x, lru_lambda)
    dproj = jnp.concatenate([dq, dk, dv, dxr, dxg], axis=-1)
    st_in, dproj = rs_begin("in", "col", h1, dproj, d, 5 * aw, dproj)
    st_in["dy"] = dproj
    (dh1,) = _matmul("mm_dh1", dproj, w_in_f, "nt", s, d, 5 * aw, (BF16,))
    dh1 = rs_mid(st_in, dh1, dh1)
    grad_x, dsh1, dsc1, dg_mix = _norm_mod_bwd("norm_mod_bwd1", dh1, x2d, rstd1, g_norm_mix, sc1, dx1)

    dmod = jnp.concatenate([dsh1, dsc1, dgt1, dsh2, dsc2, dgt2], axis=-1)
    small_names = ["b_ada", "g_norm_mix", "b_conv", "w_rg_a", "b_rg_a", "w_rg_x", "b_rg_x", "lru_lambda",
                   "g_attn_out", "g_lru_out", "g_norm_mlp", "g_norm_final"]
    small_g = [dmod, dg_mix, dbconv, dwa, dba, dwx, dbx, dlam, dg_attn, dg_lru, dg_mlp, dg_final]
    small_w = [b_ada, g_norm_mix, b_conv, w_rg_a, b_rg_a, w_rg_x, b_rg_x, lru_lambda, g_attn_out, g_lru_out, g_norm_mlp, g_norm_final]
    small_m = [m_b_ada, m_g_norm_mix, m_b_conv, m_w_rg_a, m_b_rg_a, m_w_rg_x, m_b_rg_x, m_lru_lambda, m_g_attn_out, m_g_lru_out, m_g_norm_mlp, m_g_norm_final]
    small_v = [v_b_ada, v_g_norm_mix, v_b_conv, v_w_rg_a, v_b_rg_a, v_w_rg_x, v_b_rg_x, v_lru_lambda, v_g_attn_out, v_g_lru_out, v_g_norm_mlp, v_g_norm_final]
    extra_zero = [jnp.zeros_like(dwconv), jnp.zeros((LANES,), F32)]
    g_slab, g_sizes = _pack(small_g + [dwconv, loss_row])
    w_slab, _ = _pack(small_w + extra_zero)
    m_slab, _ = _pack(small_m + extra_zero)
    v_slab, _ = _pack(small_v + extra_zero)
    me_arr = jnp.reshape(me, (1,)).astype(jnp.int32)
    g_buf = _place_row("place_small_grads", me_arr, g_slab)
    sg_send, sg_recv, g_buf, tok = _split_start_inplace("sg_gather_start", g_buf, N_CHIP, _plan_gather_own, loss_row)

    def reduced_begin(tag, half, tok_):
        send, recv, half, land, tok_ = _split_start("rs_reduced_start_" + tag, half, half.shape, 1, _plan_whole, tok_)
        return (send, recv, half, land), tok_

    def reduced_end(tag, st, after):
        return _split_wait("rs_reduced_wait_" + tag, *st, _plan_whole, after)

    sw_mo, tok = reduced_begin("mo", rs_end(st_mo, tok), tok)
    sw_mi, tok = reduced_begin("mi", rs_end(st_mi, tok), tok)
    sw_out, tok = reduced_begin("out", rs_end(st_out, tok), tok)
    half_mo, got_mo = reduced_end("mo", sw_mo, tok)
    big = {"w_mlp_out": _adamw_halves("adamw_w_mlp_out", c_arr, w_mlp_out[0], m_w_mlp_out[0], v_w_mlp_out[0],
                                      half_mo, got_mo)}
    half_mi, got_mi = reduced_end("mi", sw_mi, big["w_mlp_out"][1])
    big["w_mlp_in"] = _adamw_halves("adamw_w_mlp_in", c_arr, w_mlp_in[0], m_w_mlp_in[0], v_w_mlp_in[0], half_mi, got_mi)
    half_out, got_out = reduced_end("out", sw_out, big["w_mlp_in"][1])
    big["w_out"] = _adamw_halves("adamw_w_out", c_arr, w_out[0], m_w_out[0], v_w_out[0], half_out, got_out)
    g_buf = _split_wait_inplace("sg_gather_wait", sg_send, sg_recv, g_buf, _plan_gather_own, big["w_out"][1])
    sg_send, sg_recv, g_buf, tok = _split_start_inplace("sg_pass_start", g_buf, N_CHIP - 1, _plan_gather_pass, tok)
    half_in = rs_end(st_in, tok)
    g_all = _split_wait_inplace("sg_pass_wait", sg_send, sg_recv, g_buf, _plan_gather_pass, half_in)
    gs_slab, ds_slab, ms_slab, vs_slab = _small_reduce_adamw(g_all, w_slab, m_slab, v_slab)
    shapes = [w.shape for w in small_w] + [dwconv.shape, (LANES,)]
    gs = _unpack(gs_slab, g_sizes, shapes)
    ds = _unpack(ds_slab, g_sizes, shapes)
    ms = _unpack(ms_slab, g_sizes, shapes)
    vs = _unpack(vs_slab, g_sizes, shapes)
    small = {n: (gs[i], ds[i], ms[i], vs[i]) for i, n in enumerate(small_names)}
    loss = gs[-1][0]
    g_wconv = lax.dynamic_slice(gs[-2], (0, chip * n_cv), (CONV_TAPS, n_cv))
    d_wconv, m_wconv, v_wconv = _adamw_plain("adamw_conv", w_conv[0], g_wconv, m_w_conv[0], v_w_conv[0])
    small["w_conv"] = (g_wconv[None], d_wconv[None], m_wconv[None], v_wconv[None])

    dmod_all = g_all[:, :N_MOD * d // LANES, :].reshape(N_DEV, N_MOD * d)
    dmod_sel = lax.dynamic_slice(dmod_all, (0, chip * n_ada), (N_DEV, n_ada)).astype(BF16)
    act_t = act_all.T.astype(BF16)
    big["w_ada"] = _adamw_ada(w_ada[0], m_w_ada[0], v_w_ada[0], act_t, dmod_sel)

    (got_in,) = _swap_reduced("comm_swap_reduced_in", [half_in])
    big["w_in"] = _adamw_halves("adamw_w_in", c_arr, w_in[0], m_w_in[0], v_w_in[0], half_in, got_in)

    order = ["w_ada", "b_ada", "g_norm_mix", "w_in", "w_conv", "b_conv", "w_rg_a", "b_rg_a", "w_rg_x", "b_rg_x",
             "lru_lambda", "g_attn_out", "g_lru_out", "w_out", "g_norm_mlp", "w_mlp_in", "w_mlp_out", "g_norm_final"]
    res = {}
    for n in order:
        res[n] = tuple(t[None] for t in big[n]) if n in big else small[n]
    return (loss, grad_x[None],
            *[res[n][0] for n in order], *[res[n][1] for n in order],
            *[res[n][2] for n in order], *[res[n][3] for n in order])
```

```python
import functools
import math

import jax
import jax.numpy as jnp
from jax import lax
from jax.experimental import pallas as pl
from jax.experimental.pallas import tpu as pltpu

F32 = jnp.float32
BF16 = jnp.bfloat16
SDS = jax.ShapeDtypeStruct
MESH = pl.DeviceIdType.MESH

EPS = 1e-6
HEAD = 128
N_MOD = 6
CONV_TAPS = 4
LRU_C = 8.0
ADAM_LR, ADAM_B1, ADAM_B2, ADAM_EPS, ADAM_WD, ADAM_STEP = 0.001, 0.9, 0.999, 1e-08, 0.01, 10
N_DEV = 8
N_CHIP = 4
LANES = 128
SUBLANES = 8
VMEM_LIMIT = 56 * 1024 * 1024
PACK_ROWS = 256
MM_TILE_M, MM_TILE_N, MM_TILE_K = 1024, 1024, 2048
ROW_TILE = 256
ROW_SPLIT = 1


def _tile(dim, pref):
    t = min(dim, pref)
    while dim % t:
        t -= LANES
    return t


def _params(sem=None):
    return pltpu.CompilerParams(dimension_semantics=sem, vmem_limit_bytes=VMEM_LIMIT)


def _sigmoid(x):
    return 1.0 / (1.0 + jnp.exp(-x))


def _log_sigmoid(x):
    return jnp.minimum(x, 0.0) - jnp.log(1.0 + jnp.exp(-jnp.abs(x)))


def _gelu_parts(x):
    k0, k1 = math.sqrt(2.0 / math.pi), 0.044715
    t = jnp.tanh(k0 * (x + k1 * x * x * x))
    val = 0.5 * x * (1.0 + t)
    der = 0.5 * (1.0 + t) + 0.5 * x * (1.0 - t * t) * k0 * (1.0 + 3.0 * k1 * x * x)
    return val, der


def _dot(a, b):
    return jnp.dot(a, b, preferred_element_type=F32)


def _dot_nt(a, b):
    return lax.dot_general(a, b, (((1,), (1,)), ((), ())), preferred_element_type=F32)


def _dot_tn(a, b):
    return lax.dot_general(a, b, (((0,), (0,)), ((), ())), preferred_element_type=F32)


def _split_dot(x, tri):
    hi = x.astype(BF16)
    lo = (x - hi.astype(F32)).astype(BF16)
    return _dot(hi, tri) + _dot(lo, tri)


def _matmul(name, a, b, mode, m, n, k, out_dtypes, *, b_off=0, extras=(), extra_kinds=(), epilogue=None,
            tm=MM_TILE_M, tn=MM_TILE_N, tk=MM_TILE_K, m_half=None):
    tm, tn, tk = _tile(m, tm), _tile(math.gcd(n, b_off) if b_off else n, tn), _tile(k, tk)
    assert b_off % tn == 0
    nk = k // tk
    n_ex, n_out = len(extras), len(out_dtypes)
    dot = {"nn": _dot, "nt": _dot_nt, "tn": _dot_tn}[mode]
    n_pre = 0 if m_half is None else 1
    m_out = m if m_half is None else m // 2

    def body(*refs):
        a_ref, b_ref, *rest = refs[n_pre:]
        ex, outs = rest[:n_ex], rest[n_ex:n_ex + n_out]

        def finish(total):
            res = epilogue(total, *[e[...] for e in ex]) if epilogue else (total,)
            for o, r in zip(outs, res):
                o[...] = r.astype(o.dtype)

        if nk == 1:
            finish(dot(a_ref[...], b_ref[...]))
            return
        acc = rest[-1]
        kk = pl.program_id(2)

        @pl.when(kk == 0)
        def _():
            acc[...] = dot(a_ref[...], b_ref[...])

        @pl.when(jnp.logical_and(kk > 0, kk < nk - 1))
        def _():
            acc[...] += dot(a_ref[...], b_ref[...])

        @pl.when(kk == nk - 1)
        def _():
            finish(acc[...] + dot(a_ref[...], b_ref[...]))

    if mode == "nn":
        a_spec = pl.BlockSpec((tm, tk), lambda i, j, kk, *_: (i, kk))
        b_spec = pl.BlockSpec((tk, tn), lambda i, j, kk, *_: (kk, j + b_off // tn))
    elif mode == "nt":
        a_spec = pl.BlockSpec((tm, tk), lambda i, j, kk, *_: (i, kk))
        b_spec = pl.BlockSpec((tn, tk), lambda i, j, kk, *_: (j, kk + b_off // tk))
    elif m_half is None:
        a_spec = pl.BlockSpec((tk, tm), lambda i, j, kk: (kk, i))
        b_spec = pl.BlockSpec((tk, tn), lambda i, j, kk: (kk, j))
    else:
        a_spec = pl.BlockSpec((tk, tm), lambda i, j, kk, h: (kk, 2 * i + h[0]))
        b_spec = pl.BlockSpec((tk, tn), lambda i, j, kk, h: (kk, j))
    tile_spec = pl.BlockSpec((tm, tn), lambda i, j, kk, *_: (i, j))
    row_spec = pl.BlockSpec((1, tn), lambda i, j, kk, *_: (0, j))
    in_specs = [a_spec, b_spec] + [tile_spec if kind == "tile" else row_spec for kind in extra_kinds]
    out_specs = tuple(tile_spec for _ in out_dtypes)
    out_shape = tuple(SDS((m_out, n), dt) for dt in out_dtypes)
    scratch = [pltpu.VMEM((tm, tn), F32)] if nk > 1 else []
    grid = (m_out // tm, n // tn, nk)
    semantics = _params(("parallel", "parallel", "arbitrary"))
    if m_half is None:
        return pl.pallas_call(body, grid=grid, in_specs=in_specs, out_specs=out_specs, out_shape=out_shape,
                              scratch_shapes=scratch, compiler_params=semantics, name=name)(a, b, *extras)
    assert mode == "tn" and (m // tm) % 2 == 0
    return pl.pallas_call(
        body,
        grid_spec=pltpu.PrefetchScalarGridSpec(num_scalar_prefetch=1, grid=grid, in_specs=in_specs, out_specs=out_specs,
                                               scratch_shapes=scratch),
        out_shape=out_shape, compiler_params=semantics, name=name)(m_half, a, b, *extras)


def _row_specs(s, d, tr):
    row = "row"
    vec = pl.BlockSpec((1, d), lambda i: (0, 0))
    col = pl.BlockSpec((tr, 1), lambda i: (i, 0))
    return row, vec, col


class _ColChunks:
    def __init__(self, refs):
        self.refs = refs

    def __getitem__(self, idx):
        return jnp.concatenate([r[...] for r in self.refs], axis=-1)


def _rows_call(name, body, grid, in_specs, out_specs, out_shape, semantics, args):
    in_x, args_x, groups = [], [], []
    for spec, arr in zip(in_specs, args):
        if isinstance(spec, str):
            rows, d = arr.shape
            tr, dc = rows // grid[0], d // ROW_SPLIT
            in_x += [pl.BlockSpec((tr, dc), functools.partial(lambda i, jj: (i, jj), jj=j)) for j in range(ROW_SPLIT)]
            args_x += [arr] * ROW_SPLIT
            groups.append(ROW_SPLIT)
        else:
            in_x.append(spec)
            args_x.append(arr)
            groups.append(1)
    out_x = [pl.BlockSpec((sh.shape[0] // grid[0], sh.shape[1]), lambda i: (i, 0)) if isinstance(spec, str) else spec
             for spec, sh in zip(out_specs, out_shape)]

    def wrapped(*refs):
        views, k = [], 0
        for g in groups:
            views.append(_ColChunks(refs[k:k + g]) if g > 1 else refs[k])
            k += g
        body(*views, *refs[k:])

    return pl.pallas_call(
        wrapped, grid=grid, in_specs=in_x, out_specs=tuple(out_x), out_shape=tuple(out_shape),
        compiler_params=_params(semantics), name=name)(*args_x)


def _norm_mod_fwd(name, x, g, sc, sh):
    s, d = x.shape
    tr = _tile(s, ROW_TILE)
    row, vec, col = _row_specs(s, d, tr)

    def body(x_ref, g_ref, sc_ref, sh_ref, h_ref, r_ref):
        xv = x_ref[...]
        r = lax.rsqrt(jnp.mean(xv * xv, axis=-1, keepdims=True) + EPS)
        h_ref[...] = ((xv * r * g_ref[...]) * (1.0 + sc_ref[...]) + sh_ref[...]).astype(BF16)
        r_ref[...] = r

    return _rows_call(name, body, (s // tr,), [row, vec, vec, vec], (row, col),
                      (SDS((s, d), BF16), SDS((s, 1), F32)), ("parallel",), (x, g, sc, sh))


def _norm_mod_bwd(name, dh, xin, rstd, g, sc, dres, gate=None):
    s, d = xin.shape
    tr = _tile(s, ROW_TILE)
    row, vec, col = _row_specs(s, d, tr)

    n_gate = 2 if gate is not None else 0

    def body(dh_ref, x_ref, r_ref, g_ref, sc_ref, dres_ref, *rest):
        gate_in, gate_out = rest[:n_gate], rest[n_gate + 4:]
        dx_ref, dsh_ref, dsc_ref, dg_ref = rest[n_gate:n_gate + 4]

        @pl.when(pl.program_id(0) == 0)
        def _():
            for ref in (dsh_ref, dsc_ref, dg_ref) + tuple(gate_out[1:]):
                ref[...] = jnp.zeros_like(ref)

        dh_v, xv, r, gv = dh_ref[...].astype(F32), x_ref[...], r_ref[...], g_ref[...]
        n0 = xv * r
        dsh_ref[...] += jnp.sum(dh_v, axis=0, keepdims=True)
        dsc_ref[...] += jnp.sum(dh_v * (n0 * gv), axis=0, keepdims=True)
        dn = dh_v * (1.0 + sc_ref[...])
        dg_ref[...] += jnp.sum(dn * n0, axis=0, keepdims=True)
        gy = dn * gv
        dot = jnp.mean(gy * xv, axis=-1, keepdims=True)
        dxv = dres_ref[...] + r * gy - xv * (r * r * r * dot)
        dx_ref[...] = dxv
        if gate is not None:
            y_ref, gt_ref = gate_in
            dy_ref, dgt_ref = gate_out
            dy_ref[...] = (gt_ref[...] * dxv).astype(BF16)
            dgt_ref[...] += jnp.sum(dxv * y_ref[...], axis=0, keepdims=True)

    vecs = SDS((1, d), F32)
    gate_args = tuple(gate) if gate is not None else ()
    return _rows_call(
        name, body, (s // tr,),
        [row, row, col, vec, vec, row] + ([row, vec] if gate is not None else []),
        (row, vec, vec, vec) + ((row, vec) if gate is not None else ()),
        (SDS((s, d), F32), vecs, vecs, vecs) + ((SDS((s, d), BF16), vecs) if gate is not None else ()),
        ("arbitrary",), (dh, xin, rstd, g, sc, dres, *gate_args))


def _final_loss(x2, gf, tgt, y, gt):
    s, d = x2.shape
    tr = _tile(s, ROW_TILE)
    row, vec, _ = _row_specs(s, d, tr)
    lrow = pl.BlockSpec((1, LANES), lambda i: (0, 0))

    def body(x_ref, g_ref, t_ref, y_ref, gt_ref, dx_ref, loss_ref, dg_ref, dy_ref, dgt_ref):
        @pl.when(pl.program_id(0) == 0)
        def _():
            loss_ref[...] = jnp.zeros_like(loss_ref)
            dg_ref[...] = jnp.zeros_like(dg_ref)
            dgt_ref[...] = jnp.zeros_like(dgt_ref)

        xv, gv = x_ref[...], g_ref[...]
        r = lax.rsqrt(jnp.mean(xv * xv, axis=-1, keepdims=True) + EPS)
        n0 = xv * r
        err = n0 * gv - t_ref[...]
        loss_ref[...] += jnp.sum(err * err) * (0.5 / d)
        dy = err * (1.0 / d)
        dg_ref[...] += jnp.sum(dy * n0, axis=0, keepdims=True)
        gy = dy * gv
        dot = jnp.mean(gy * xv, axis=-1, keepdims=True)
        dxv = r * gy - xv * (r * r * r * dot)
        dx_ref[...] = dxv
        dy_ref[...] = (gt_ref[...] * dxv).astype(BF16)
        dgt_ref[...] += jnp.sum(dxv * y_ref[...], axis=0, keepdims=True)

    return _rows_call(
        "final_loss", body, (s // tr,), [row, vec, row, row, vec], (row, lrow, vec, row, vec),
        (SDS((s, d), F32), SDS((1, LANES), F32), SDS((1, d), F32), SDS((s, d), BF16), SDS((1, d), F32)),
        ("arbitrary",), (x2, gf, tgt, y, gt))


def _mix_norm_fwd(oa, ol, ga, gl):
    s, w = oa.shape
    tr = _tile(s, ROW_TILE)
    row, vec, col = _row_specs(s, w, tr)

    def body(oa_ref, ol_ref, ga_ref, gl_ref, mx_ref, ra_ref, rl_ref):
        a, l = oa_ref[...], ol_ref[...]
        ra = lax.rsqrt(jnp.mean(a * a, axis=-1, keepdims=True) + EPS)
        rl = lax.rsqrt(jnp.mean(l * l, axis=-1, keepdims=True) + EPS)
        mx_ref[:, :w] = (a * ra * ga_ref[...]).astype(BF16)
        mx_ref[:, w:] = (l * rl * gl_ref[...]).astype(BF16)
        ra_ref[...] = ra
        rl_ref[...] = rl

    return _rows_call(
        "mix_norm_fwd", body, (s // tr,), [row, row, vec, vec], (row, col, col),
        (SDS((s, 2 * w), BF16), SDS((s, 1), F32), SDS((s, 1), F32)), ("parallel",), (oa, ol, ga, gl))


def _mix_norm_bwd(dmx, oa, ol, ra, rl, ga, gl):
    s, w = oa.shape
    tr = _tile(s, ROW_TILE)
    row, vec, col = _row_specs(s, w, tr)

    def body(dm_ref, oa_ref, ol_ref, ra_ref, rl_ref, ga_ref, gl_ref, doa_ref, dol_ref, dga_ref, dgl_ref):
        @pl.when(pl.program_id(0) == 0)
        def _():
            dga_ref[...] = jnp.zeros_like(dga_ref)
            dgl_ref[...] = jnp.zeros_like(dgl_ref)

        def one(dy, xv, r, gv, dg_ref):
            dg_ref[...] += jnp.sum(dy * (xv * r), axis=0, keepdims=True)
            gy = dy * gv
            dot = jnp.mean(gy * xv, axis=-1, keepdims=True)
            return r * gy - xv * (r * r * r * dot)

        dm = dm_ref[...].astype(F32)
        doa_ref[...] = one(dm[:, :w], oa_ref[...], ra_ref[...], ga_ref[...], dga_ref).astype(BF16)
        dol_ref[...] = one(dm[:, w:], ol_ref[...], rl_ref[...], gl_ref[...], dgl_ref)

    return _rows_call(
        "mix_norm_bwd", body, (s // tr,), [row, row, row, col, col, vec, vec], (row, row, vec, vec),
        (SDS((s, w), BF16), SDS((s, w), F32), SDS((1, w), F32), SDS((1, w), F32)), ("arbitrary",),
        (dmx, oa, ol, ra, rl, ga, gl))


def _attn_blocks(qs, ks, tri_after, csums, causal):
    zs = [_dot_nt(q, k) * (HEAD ** -0.5) for q, k in zip(qs, ks)]
    lbs = [_log_sigmoid(z) for z in zs]
    lss = [lb - z for lb, z in zip(lbs, zs)]
    if causal is not None:
        lss = [jnp.where(causal, ls, 0.0) for ls in lss]
    locs = [_split_dot(ls, tri_after) for ls in lss]
    ws = [jnp.exp(lb + (loc + cs)) for lb, loc, cs in zip(lbs, locs, csums)]
    if causal is not None:
        ws = [jnp.where(causal, w, 0.0) for w in ws]
    nxt = [cs + (loc[:, 0:1] + ls[:, 0:1]) for cs, loc, ls in zip(csums, locs, lss)]
    return lbs, ws, nxt


ATTN_HEADS_PER_STEP = 4


def _attn_tile(s):
    return 256 if s >= 1024 else 128


def _tri(t, after):
    r_i = lax.broadcasted_iota(jnp.int32, (t, t), 0)
    c_i = lax.broadcasted_iota(jnp.int32, (t, t), 1)
    return ((r_i > c_i) if after else (r_i < c_i)).astype(BF16)


def _attn_fwd(qkv, n_heads):
    s = qkv.shape[0]
    t = _attn_tile(s)
    hps = ATTN_HEADS_PER_STEP
    wid = hps * HEAD

    nq = s // t

    def body(q_ref, k_ref, v_ref, o_ref, w_ref, sg_ref):
        qi = pl.program_id(1)
        tri_after = _tri(t, True)
        causal = lax.broadcasted_iota(jnp.int32, (t, t), 1) < lax.broadcasted_iota(jnp.int32, (t, t), 0)
        lanes = [slice(a * HEAD, (a + 1) * HEAD) for a in range(hps)]
        qs = [q_ref[:, ln] for ln in lanes]

        def block(kb, carry, mask):
            off = pl.multiple_of(kb * t, t)
            ks = [k_ref[pl.ds(off, t), ln] for ln in lanes]
            lbs, ws, csums = _attn_blocks(qs, ks, tri_after, [cr[0] for cr in carry], mask)
            wbs = [w.astype(BF16) for w in ws]
            for a in range(hps):
                w_ref[a, kb] = wbs[a]
                sg_ref[a, kb] = jnp.exp(lbs[a]).astype(BF16)
            os_ = [cr[1] + _dot(wb, v_ref[pl.ds(off, t), ln]) for cr, wb, ln in zip(carry, wbs, lanes)]
            return tuple(zip(csums, os_))

        zero = tuple((jnp.zeros((t, 1), F32), jnp.zeros((t, HEAD), F32)) for _ in lanes)
        carry = block(qi, zero, causal)
        carry = lax.fori_loop(1, qi + 1, lambda it, cr: block(qi - it, cr, None), carry)
        for a, ln in enumerate(lanes):
            o_ref[:, ln] = carry[a][1]

    hb = n_heads // hps
    kept = pl.BlockSpec((None, hps, nq, t, t), lambda hh, i: (hh * nq + i, 0, 0, 0, 0))
    kept_shape = SDS((hb * nq, hps, nq, t, t), BF16)
    return pl.pallas_call(
        body, grid=(hb, nq),
        in_specs=[pl.BlockSpec((t, wid), lambda hh, i: (i, hh)),
                  pl.BlockSpec((s, wid), lambda hh, i: (0, hb + hh)),
                  pl.BlockSpec((s, wid), lambda hh, i: (0, 2 * hb + hh))],
        out_specs=(pl.BlockSpec((t, wid), lambda hh, i: (i, hh)), kept, kept),
        out_shape=(SDS((s, n_heads * HEAD), F32), kept_shape, kept_shape),
        compiler_params=_params(("parallel", "parallel")), name="attn_fwd")(qkv, qkv, qkv)


def _attn_bwd(qkv, do, w_kept, sg_kept, n_heads):
    s = qkv.shape[0]
    t = _attn_tile(s)
    nq = s // t
    scale = HEAD ** -0.5
    hps = ATTN_HEADS_PER_STEP
    wid = hps * HEAD

    def body(q_ref, k_ref, v_ref, do_ref, w_ref, sg_ref, dq_ref, dk_ref, dv_ref, dk_acc, dv_acc):
        qi = pl.program_id(1)

        @pl.when(qi == 0)
        def _():
            dk_acc[...] = jnp.zeros_like(dk_acc)
            dv_acc[...] = jnp.zeros_like(dv_acc)

        tri_before = _tri(t, False)
        causal = lax.broadcasted_iota(jnp.int32, (t, t), 1) < lax.broadcasted_iota(jnp.int32, (t, t), 0)
        lanes = [slice(a * HEAD, (a + 1) * HEAD) for a in range(hps)]
        qs = [q_ref[:, ln] for ln in lanes]
        douts = [do_ref[:, ln] for ln in lanes]

        def block(kb, carry, mask):
            off = pl.multiple_of(kb * t, t)
            wbs = [w_ref[a, kb] for a in range(hps)]
            dws = [_dot_nt(dout, v_ref[pl.ds(off, t), ln]) for dout, ln in zip(douts, lanes)]
            for a, ln in enumerate(lanes):
                dv_acc[pl.ds(off, t), ln] += _dot_tn(wbs[a], douts[a])
            es = [dw * wb.astype(F32) for dw, wb in zip(dws, wbs)]
            locs = [_split_dot(e, tri_before) for e in es]
            sgs = [sg_ref[a, kb].astype(F32) for a in range(hps)]
            stays = [(loc + cr[0]) * sg for loc, cr, sg in zip(locs, carry, sgs)]
            if mask is not None:
                stays = [jnp.where(mask, st, 0.0) for st in stays]
            dzbs = [((e * (1.0 - sg) - st) * scale).astype(BF16) for e, sg, st in zip(es, sgs, stays)]
            dqs = [cr[1] + _dot(dzb, k_ref[pl.ds(off, t), ln]) for cr, dzb, ln in zip(carry, dzbs, lanes)]
            for a, ln in enumerate(lanes):
                dk_acc[pl.ds(off, t), ln] += _dot_tn(dzbs[a], qs[a])
            esums = [cr[0] + (loc[:, t - 1:t] + e[:, t - 1:t]) for cr, loc, e in zip(carry, locs, es)]
            return tuple(zip(esums, dqs))

        zero = tuple((jnp.zeros((t, 1), F32), jnp.zeros((t, HEAD), F32)) for _ in lanes)
        carry = lax.fori_loop(0, qi, lambda kb, cr: block(kb, cr, None), zero)
        carry = block(qi, carry, causal)
        for a, ln in enumerate(lanes):
            dq_ref[:, ln] = carry[a][1].astype(BF16)

        @pl.when(qi == nq - 1)
        def _():
            dk_ref[...] = dk_acc[...].astype(BF16)
            dv_ref[...] = dv_acc[...].astype(BF16)

    hb = n_heads // hps
    blk = pl.BlockSpec((t, wid), lambda hh, i: (i, hh))
    full = pl.BlockSpec((s, wid), lambda hh, i: (0, hh))
    kept = pl.BlockSpec((None, hps, nq, t, t), lambda hh, i: (hh * nq + i, 0, 0, 0, 0))
    return pl.pallas_call(
        body, grid=(hb, nq),
        in_specs=[blk,
                  pl.BlockSpec((s, wid), lambda hh, i: (0, hb + hh)),
                  pl.BlockSpec((s, wid), lambda hh, i: (0, 2 * hb + hh)),
                  blk, kept, kept],
        out_specs=(blk, full, full),
        out_shape=(SDS((s, n_heads * HEAD), BF16),) * 3,
        scratch_shapes=[pltpu.VMEM((s, wid), F32), pltpu.VMEM((s, wid), F32)],
        compiler_params=_params(("parallel", "arbitrary")), name="attn_bwd")(qkv, qkv, qkv, do, w_kept, sg_kept)


def _lru_chunk(s):
    return 128 if s >= 256 else s // 2


def _lru_gates(xc, wa, ba, wx, bx, sp):
    xb = xc.astype(BF16)
    r = _sigmoid(_dot(xb, wa) + ba)
    ig = _sigmoid(_dot(xb, wx) + bx)
    la = -LRU_C * r * sp
    a = jnp.exp(la)
    t = jnp.tanh(la)
    mult = jnp.sqrt(-2.0 * t / (1.0 - t))
    return r, ig, a, mult


def _softplus_neg(lam):
    return jnp.maximum(-lam, 0.0) + jnp.log(1.0 + jnp.exp(-jnp.abs(lam)))


LRU_BLOCKS_PER_STEP = 1
SCAN_GROUP = 4


def _lru_specs(s, n_blocks):
    bps = min(LRU_BLOCKS_PER_STEP, n_blocks)
    wid = bps * HEAD
    seq0 = pl.BlockSpec((s, wid), lambda h: (0, h))
    seq1 = pl.BlockSpec((s, wid), lambda h: (0, n_blocks // bps + h))
    taps = pl.BlockSpec((CONV_TAPS, wid), lambda h: (0, h))
    vec = pl.BlockSpec((1, wid), lambda h: (0, h))
    mat = pl.BlockSpec((bps, HEAD, HEAD), lambda h: (h, 0, 0))
    return bps, seq0, seq1, taps, vec, mat


def _per_block(one_block, n_2d, n_mat_pos, bps):
    def body(*refs):
        for a in range(bps):
            views = [r.at[a] if i in n_mat_pos else r.at[:, pl.ds(a * HEAD, HEAD)] for i, r in enumerate(refs[:n_2d])]
            one_block(*views, *refs[n_2d:])
    return body


def _lru_fwd(xrg, wconv, bconv, wa, ba, wx, bx, lam):
    s = xrg.shape[0]
    nb = wa.shape[0]
    tc = _lru_chunk(s)
    bps, seq0, seq1, taps, vec, mat = _lru_specs(s, nb)
    pad = SUBLANES

    def one_block(xr_ref, xg_ref, wc_ref, bc_ref, wa_ref, ba_ref, wx_ref, bx_ref, lam_ref, o_ref, h_ref, pad_s, a_s, u_s):
        pad_s[0:pad, :] = jnp.zeros((pad, HEAD), F32)
        pad_s[pad:pad + s, :] = xr_ref[...]
        wab, wxb = wa_ref[...].astype(BF16), wx_ref[...].astype(BF16)
        sp = _softplus_neg(lam_ref[...])
        for c in range(s // tc):
            base = c * tc
            xc = bc_ref[...] + sum(wc_ref[i:i + 1, :] * pad_s[pl.ds(base + pad - (CONV_TAPS - 1) + i, tc), :]
                                   for i in range(CONV_TAPS))
            _, ig, a, mult = _lru_gates(xc, wab, ba_ref[...], wxb, bx_ref[...], sp)
            a_s[base:base + tc, :] = a
            u_s[base:base + tc, :] = mult * (ig * xc)

        row = lax.broadcasted_iota(jnp.int32, (SUBLANES, HEAD), 0)
        last = SUBLANES - 1

        def group(gi, hprev):
            offs = [pl.multiple_of((gi * SCAN_GROUP + q) * SUBLANES, SUBLANES) for q in range(SCAN_GROUP)]
            ab = []
            for off in offs:
                a8, b8 = a_s[pl.ds(off, SUBLANES), :], u_s[pl.ds(off, SUBLANES), :]
                for d in (1, 2, 4):
                    a_sh = jnp.where(row < d, 1.0, pltpu.roll(a8, d, 0))
                    b_sh = jnp.where(row < d, 0.0, pltpu.roll(b8, d, 0))
                    b8 = a8 * b_sh + b8
                    a8 = a8 * a_sh
                ab.append((a8, b8))
            enters = []
            for a8, b8 in ab:
                enters.append(hprev)
                hprev = a8[last:, :] * hprev + b8[last:, :]
            for off, (a8, b8), h0 in zip(offs, ab, enters):
                h_ref[pl.ds(off, SUBLANES), :] = a8 * h0 + b8
            return hprev

        lax.fori_loop(0, s // (SUBLANES * SCAN_GROUP), group, jnp.zeros((1, HEAD), F32))
        for c in range(s // tc):
            sl = slice(c * tc, (c + 1) * tc)
            gel, _ = _gelu_parts(xg_ref[sl, :])
            o_ref[sl, :] = h_ref[sl, :] * gel

    return pl.pallas_call(
        _per_block(one_block, 11, (4, 6), bps), grid=(nb // bps,),
        in_specs=[seq0, seq1, taps, vec, mat, vec, mat, vec, vec],
        out_specs=(seq0, seq0),
        out_shape=(SDS((s, nb * HEAD), F32), SDS((s, nb * HEAD), F32)),
        scratch_shapes=[pltpu.VMEM((s + pad, HEAD), F32), pltpu.VMEM((s, HEAD), F32), pltpu.VMEM((s, HEAD), F32)],
        compiler_params=_params(("parallel",)), name="lru_fwd")(xrg, xrg, wconv, bconv, wa, ba, wx, bx, lam)


def _lru_bwd(xrg, dol, hseq, wconv, bconv, wa, ba, wx, bx, lam):
    s = xrg.shape[0]
    nb = wa.shape[0]
    tc = _lru_chunk(s)
    bps, seq0, seq1, taps, vec, mat = _lru_specs(s, nb)
    pad = SUBLANES

    def one_block(xr_ref, xg_ref, do_ref, h_ref, wc_ref, bc_ref, wa_ref, ba_ref, wx_ref, bx_ref, lam_ref,
             dxr_ref, dxg_ref, dwc_ref, dbc_ref, dwa_ref, dba_ref, dwx_ref, dbx_ref, dlam_ref,
             pad_s, hp_s, a_s, g_s, da_s, dxc_s, xc_s, r_s, ig_s, mult_s):
        pad_s[0:pad, :] = jnp.zeros((pad, HEAD), F32)
        pad_s[pad:pad + s, :] = xr_ref[...]
        hp_s[0:pad, :] = jnp.zeros((pad, HEAD), F32)
        hp_s[pad:pad + s, :] = h_ref[...]
        a_s[s:s + pad, :] = jnp.zeros((pad, HEAD), F32)
        dxc_s[s:s + pad, :] = jnp.zeros((pad, HEAD), F32)
        wab, wxb = wa_ref[...].astype(BF16), wx_ref[...].astype(BF16)
        lam_v = lam_ref[...]
        sp = _softplus_neg(lam_v)

        def conv_in(c):
            base = c * tc
            wins = [pad_s[pl.ds(base + pad - (CONV_TAPS - 1) + i, tc), :] for i in range(CONV_TAPS)]
            xc = bc_ref[...] + sum(wc_ref[i:i + 1, :] * wins[i] for i in range(CONV_TAPS))
            return xc, wins

        for c in range(s // tc):
            sl = slice(c * tc, (c + 1) * tc)
            xc, _ = conv_in(c)
            r, ig, a, mult = _lru_gates(xc, wab, ba_ref[...], wxb, bx_ref[...], sp)
            a_s[sl, :] = a
            xc_s[sl, :], r_s[sl, :], ig_s[sl, :], mult_s[sl, :] = xc, r, ig, mult
            gel, dgel = _gelu_parts(xg_ref[sl, :])
            dov = do_ref[sl, :]
            g_s[sl, :] = dov * gel
            dxg_ref[sl, :] = (dov * h_ref[sl, :] * dgel).astype(BF16)

        row = lax.broadcasted_iota(jnp.int32, (SUBLANES, HEAD), 0)
        n_chunks = s // SUBLANES

        def group(it, gnext):
            offs = [pl.multiple_of((n_chunks - 1 - (it * SCAN_GROUP + q)) * SUBLANES, SUBLANES) for q in range(SCAN_GROUP)]
            cg = []
            for off in offs:
                a8 = a_s[pl.ds(off, SUBLANES), :]
                a8n = a_s[pl.ds(off + SUBLANES, SUBLANES), :]
                c8 = pltpu.roll(jnp.where(row == 0, a8n, a8), SUBLANES - 1, 0)
                g8 = g_s[pl.ds(off, SUBLANES), :]
                for d in (1, 2, 4):
                    c_sh = jnp.where(row >= SUBLANES - d, 1.0, pltpu.roll(c8, SUBLANES - d, 0))
                    g_sh = jnp.where(row >= SUBLANES - d, 0.0, pltpu.roll(g8, SUBLANES - d, 0))
                    g8 = c8 * g_sh + g8
                    c8 = c8 * c_sh
                cg.append((c8, g8))
            enters = []
            for c8, g8 in cg:
                enters.append(gnext)
                gnext = g8[0:1, :] + c8[0:1, :] * gnext
            for off, (c8, g8), g0 in zip(offs, cg, enters):
                gv = g8 + c8 * g0
                g_s[pl.ds(off, SUBLANES), :] = gv
                h8 = hp_s[pl.ds(off + pad, SUBLANES), :]
                h8p = hp_s[pl.ds(off, SUBLANES), :]
                da_s[pl.ds(off, SUBLANES), :] = gv * pltpu.roll(jnp.where(row == SUBLANES - 1, h8p, h8), 1, 0)
            return gnext

        lax.fori_loop(0, n_chunks // SCAN_GROUP, group, jnp.zeros((1, HEAD), F32))

        dsp = jnp.zeros((1, HEAD), F32)
        dbc = jnp.zeros((1, HEAD), F32)
        dba = jnp.zeros((1, HEAD), F32)
        dbx = jnp.zeros((1, HEAD), F32)
        dwa = jnp.zeros((HEAD, HEAD), F32)
        dwx = jnp.zeros((HEAD, HEAD), F32)
        dwc = [jnp.zeros((1, HEAD), F32) for _ in range(CONV_TAPS)]
        for c in range(s // tc):
            sl = slice(c * tc, (c + 1) * tc)
            wins = [pad_s[pl.ds(c * tc + pad - (CONV_TAPS - 1) + i, tc), :] for i in range(CONV_TAPS)]
            xc, r, ig, a, mult = xc_s[sl, :], r_s[sl, :], ig_s[sl, :], a_s[sl, :], mult_s[sl, :]
            du, da = g_s[sl, :], da_s[sl, :]
            d_ix = du * mult
            dla = da * a - (du * ig * xc) * (a * a / mult)
            dsp = dsp + jnp.sum(dla * r, axis=0, keepdims=True) * (-LRU_C)
            dpa = (dla * (-LRU_C * sp)) * r * (1.0 - r)
            dpx = (d_ix * xc) * ig * (1.0 - ig)
            dpab, dpxb, xb = dpa.astype(BF16), dpx.astype(BF16), xc.astype(BF16)
            dxc = d_ix * ig + _dot_nt(dpab, wab) + _dot_nt(dpxb, wxb)
            dwa = dwa + _dot_tn(xb, dpab)
            dwx = dwx + _dot_tn(xb, dpxb)
            dba = dba + jnp.sum(dpa, axis=0, keepdims=True)
            dbx = dbx + jnp.sum(dpx, axis=0, keepdims=True)
            dbc = dbc + jnp.sum(dxc, axis=0, keepdims=True)
            for i in range(CONV_TAPS):
                dwc[i] = dwc[i] + jnp.sum(dxc * wins[i], axis=0, keepdims=True)
            dxc_s[sl, :] = dxc

        for c in range(s // tc):
            base = c * tc
            dxr = sum(wc_ref[i:i + 1, :] * dxc_s[pl.ds(base + (CONV_TAPS - 1) - i, tc), :] for i in range(CONV_TAPS))
            dxr_ref[base:base + tc, :] = dxr.astype(BF16)

        for i in range(CONV_TAPS):
            dwc_ref[i:i + 1, :] = dwc[i]
        dbc_ref[...] = dbc
        dwa_ref[...] = dwa
        dwx_ref[...] = dwx
        dba_ref[...] = dba
        dbx_ref[...] = dbx
        dlam_ref[...] = dsp * (-_sigmoid(-lam_v))

    w = nb * HEAD
    return pl.pallas_call(
        _per_block(one_block, 20, (6, 8, 15, 17), bps), grid=(nb // bps,),
        in_specs=[seq0, seq1, seq0, seq0, taps, vec, mat, vec, mat, vec, vec],
        out_specs=(seq0, seq0, taps, vec, mat, vec, mat, vec, vec),
        out_shape=(SDS((s, w), BF16), SDS((s, w), BF16), SDS((CONV_TAPS, w), F32), SDS((1, w), F32),
                   SDS((nb, HEAD, HEAD), F32), SDS((1, w), F32), SDS((nb, HEAD, HEAD), F32), SDS((1, w), F32),
                   SDS((1, w), F32)),
        scratch_shapes=[pltpu.VMEM((s + pad, HEAD), F32), pltpu.VMEM((s + pad, HEAD), F32),
                        pltpu.VMEM((s + pad, HEAD), F32), pltpu.VMEM((s, HEAD), F32),
                        pltpu.VMEM((s, HEAD), F32), pltpu.VMEM((s + pad, HEAD), F32)]
                       + [pltpu.VMEM((s, HEAD), F32)] * 4,
        compiler_params=_params(("parallel",)), name="lru_bwd",
    )(xrg, xrg, dol, hseq, wconv, bconv, wa, ba, wx, bx, lam)


def _ada_mod(c_all, w_sh, b_sh):
    n_ex, d = c_all.shape
    n = w_sh.shape[1]
    tn = _tile(n, 512)

    def body(c_ref, w_ref, b_ref, mod_ref, act_ref):
        cv = c_ref[...]
        act = cv * _sigmoid(cv)
        act_ref[...] = act
        mod_ref[...] = _dot(act.astype(BF16), w_ref[...].astype(BF16)) + b_ref[...]

    return pl.pallas_call(
        body, grid=(n // tn,),
        in_specs=[pl.BlockSpec((n_ex, d), lambda j: (0, 0)), pl.BlockSpec((d, tn), lambda j: (0, j)),
                  pl.BlockSpec((1, tn), lambda j: (0, j))],
        out_specs=(pl.BlockSpec((n_ex, tn), lambda j: (0, j)), pl.BlockSpec((n_ex, d), lambda j: (0, 0))),
        out_shape=(SDS((n_ex, n), F32), SDS((n_ex, d), F32)),
        compiler_params=_params(("arbitrary",)), name="ada_mod")(c_all, w_sh, b_sh)


def _adamw_math(w, g, m, v):
    m = ADAM_B1 * m + (1.0 - ADAM_B1) * g
    v = ADAM_B2 * v + (1.0 - ADAM_B2) * (g * g)
    m_hat = m / (1.0 - ADAM_B1 ** ADAM_STEP)
    v_hat = v / (1.0 - ADAM_B2 ** ADAM_STEP)
    delta = -ADAM_LR * (m_hat / (jnp.sqrt(v_hat) + ADAM_EPS) + ADAM_WD * w)
    return delta, m, v


def _adamw_plain(name, w, g, m, v):
    def body(w_ref, g_ref, m_ref, v_ref, d_ref, mo_ref, vo_ref):
        d_ref[...], mo_ref[...], vo_ref[...] = _adamw_math(w_ref[...], g_ref[...], m_ref[...], v_ref[...])

    return pl.pallas_call(body, out_shape=(SDS(w.shape, F32),) * 3, name=name)(w, g, m, v)


def _adamw_halves(name, c_arr, w, m, v, g_own, g_recv):
    r, n = w.shape
    rh = r // 2
    tr = _tile(rh, 256)
    nh = rh // tr

    def body(c_ref, w_ref, m_ref, v_ref, go_ref, gr_ref, g_ref, d_ref, mo_ref, vo_ref):
        own = (pl.program_id(0) // nh) == c_ref[0]
        g = jnp.where(own, go_ref[...], gr_ref[...])
        g_ref[...] = g
        d_ref[...], mo_ref[...], vo_ref[...] = _adamw_math(w_ref[...], g, m_ref[...], v_ref[...])

    full = pl.BlockSpec((tr, n), lambda i, c: (i, 0))
    own = pl.BlockSpec((tr, n), lambda i, c: (jnp.where(i // nh == c[0], i % nh, 0), 0))
    recv = pl.BlockSpec((tr, n), lambda i, c: (jnp.where(i // nh == c[0], 0, i % nh), 0))
    return pl.pallas_call(
        body,
        grid_spec=pltpu.PrefetchScalarGridSpec(
            num_scalar_prefetch=1, grid=(2 * nh,), in_specs=[full, full, full, own, recv],
            out_specs=(full,) * 4),
        out_shape=(SDS((r, n), F32),) * 4,
        compiler_params=_params(("parallel",)), name=name)(c_arr, w, m, v, g_own, g_recv)


def _adamw_ada(w, m, v, act_t, dmod):
    d, n = w.shape
    n_ex = act_t.shape[1]
    tr = _tile(d, 256)

    def body(a_ref, dm_ref, w_ref, m_ref, v_ref, g_ref, d_ref, mo_ref, vo_ref):
        g = _dot(a_ref[...], dm_ref[...])
        g_ref[...] = g
        d_ref[...], mo_ref[...], vo_ref[...] = _adamw_math(w_ref[...], g, m_ref[...], v_ref[...])

    full = pl.BlockSpec((tr, n), lambda i: (i, 0))
    return pl.pallas_call(
        body, grid=(d // tr,),
        in_specs=[pl.BlockSpec((tr, n_ex), lambda i: (i, 0)), pl.BlockSpec((n_ex, n), lambda i: (0, 0)), full, full, full],
        out_specs=(full,) * 4, out_shape=(SDS((d, n), F32),) * 4,
        compiler_params=_params(("parallel",)), name="adamw_ada")(act_t, dmod, w, m, v)


def _small_reduce_adamw(parts, w, m, v):
    n_dev, r, _ = parts.shape
    tr = r if r <= PACK_ROWS else PACK_ROWS

    def body(p_ref, w_ref, m_ref, v_ref, g_ref, d_ref, mo_ref, vo_ref):
        g = p_ref[0]
        for k in range(1, n_dev):
            g = g + p_ref[k]
        g_ref[...] = g
        d_ref[...], mo_ref[...], vo_ref[...] = _adamw_math(w_ref[...], g, m_ref[...], v_ref[...])

    full = pl.BlockSpec((tr, LANES), lambda i: (i, 0))
    return pl.pallas_call(
        body, grid=(r // tr,),
        in_specs=[pl.BlockSpec((n_dev, tr, LANES), lambda i: (0, i, 0)), full, full, full],
        out_specs=(full,) * 4, out_shape=(SDS((r, LANES), F32),) * 4,
        compiler_params=_params(("parallel",)), name="small_reduce_adamw")(parts, w, m, v)


def _mesh_pos():
    return lax.axis_index("x"), lax.axis_index("y"), lax.axis_index("c")


def _other_chips(x, y):
    return [(1 - x, y), (x, 1 - y), (1 - x, 1 - y)]


def _all_gather_small(name, blk, after=()):
    r, n = blk.shape
    n_after = len(after)

    def body(x_ref, *rest):
        out_ref, send_sems, recv_sems, local_sem = rest[n_after:]
        x, y, c = _mesh_pos()
        me, sibling = (x, y, c), (x, y, 1 - c)
        chips = _other_chips(x, y)

        def rows(px, py, pc):
            return out_ref.at[4 * px + 2 * py + pc]

        def copy(k, block, to, src=None):
            return pltpu.make_async_remote_copy(
                src_ref=rows(*block) if src is None else src, dst_ref=rows(*block),
                send_sem=send_sems.at[k], recv_sem=recv_sems.at[k], device_id=to, device_id_type=MESH)

        mine = pltpu.make_async_copy(x_ref, rows(*me), local_sem)
        mine.start()
        first = [copy(0, me, sibling, src=x_ref)]
        first += [copy(1 + j, me, (*chip, c), src=x_ref) for j, chip in enumerate(chips)]
        for cp in first:
            cp.start()
        passed = [copy(4 + j, (*chip, c), sibling) for j, chip in enumerate(chips)]
        for j, chip in enumerate(chips):
            copy(1 + j, (*chip, c), me).wait_recv()
            passed[j].start()
        copy(0, sibling, me).wait_recv()
        for j, chip in enumerate(chips):
            copy(4 + j, (*chip, 1 - c), me).wait_recv()
        for cp in first + passed:
            cp.wait_send()
        mine.wait()

    return pl.pallas_call(
        body, out_shape=SDS((N_DEV, r, n), blk.dtype),
        in_specs=[pl.BlockSpec(memory_space=pltpu.VMEM)] + [pl.BlockSpec(memory_space=pl.ANY)] * n_after,
        out_specs=pl.BlockSpec(memory_space=pltpu.VMEM),
        scratch_shapes=[pltpu.SemaphoreType.DMA((7,)), pltpu.SemaphoreType.DMA((7,)), pltpu.SemaphoreType.DMA],
        compiler_params=pltpu.CompilerParams(vmem_limit_bytes=VMEM_LIMIT), name=name)(blk, *after)


_ANY = pl.BlockSpec(memory_space=pl.ANY)
_HBM = pl.BlockSpec(memory_space=pltpu.HBM)
_SEM = pl.BlockSpec(memory_space=pltpu.SEMAPHORE)
_EFFECT = pltpu.SideEffectType.DATAFLOW_SIDE_EFFECTING


def _hbm(a):
    return pltpu.with_memory_space_constraint(a, pltpu.HBM)


def _place_cast(name, j_arr, shard, kind, after):
    r, n = shard.shape
    tr = _tile(r, 256)
    nr = r // tr
    if kind == "col":
        out_shape, o_spec = (r, N_CHIP * n), pl.BlockSpec((tr, n), lambda i, j: (i, j[0]))
    else:
        out_shape, o_spec = (N_CHIP * r, n), pl.BlockSpec((tr, n), lambda i, j: (j[0] * nr + i, 0))

    def body(j_ref, s_ref, after_ref, o_ref, tok_ref):
        o_ref[...] = s_ref[...].astype(BF16)
        tok_ref[...] = jnp.zeros_like(tok_ref)

    return pl.pallas_call(
        body,
        grid_spec=pltpu.PrefetchScalarGridSpec(
            num_scalar_prefetch=1, grid=(nr,), in_specs=[pl.BlockSpec((tr, n), lambda i, j: (i, 0)), _ANY],
            out_specs=(o_spec, pl.BlockSpec((SUBLANES, LANES), lambda i, j: (0, 0)))),
        out_shape=(SDS(out_shape, BF16), SDS((SUBLANES, LANES), F32)),
        compiler_params=_params(("arbitrary",)), name=name)(j_arr, shard, after)


def _leg_direct(full, kind, x, y, c):
    mine = _full_region(full, kind, x, y, c)
    return [(mine, mine, (1 - x, y, c)), (mine, mine, (x, 1 - y, c))]


def _leg_relay(full, kind, x, y, c):
    fx, fy = jnp.where(c == 0, 1 - x, x), jnp.where(c == 0, y, 1 - y)
    tx, ty = jnp.where(c == 0, x, 1 - x), jnp.where(c == 0, 1 - y, y)
    got = _full_region(full, kind, fx, fy, c)
    return [(got, got, (tx, ty, c))]


def _leg_d2d(which):
    def leg(full, kind, x, y, c):
        chips = _other_chips(x, y)
        return [(_full_region(full, kind, *chips[k], c), _full_region(full, kind, *chips[k], c), (x, y, 1 - c))
                for k in which]
    return leg


_LEGS = {"direct": (_leg_direct, 2), "relay": (_leg_relay, 1), "d2d_near": (_leg_d2d((0, 1)), 2),
         "d2d_far": (_leg_d2d((2,)), 1)}


def _gather_call(name, fulls, kinds, waits, starts, after, thru):
    nw, n_wait, n_start = len(fulls), len(waits), len(starts)

    def body(*refs):
        wait_sems = refs[nw:nw + 2 * n_wait]
        outs = refs[nw + 2 * n_wait + 2:]
        full, start_sems = outs[:nw], outs[nw:nw + 2 * n_start]
        x, y, c = _mesh_pos()
        for i, (leg, ws, _, _) in enumerate(waits):
            fn, per = _LEGS[leg]
            for li, w in enumerate(ws):
                for k, (s_, d_, dev) in enumerate(fn(full[w], kinds[w], x, y, c)):
                    cp = pltpu.make_async_remote_copy(
                        src_ref=s_, dst_ref=d_, send_sem=wait_sems[2 * i].at[per * li + k],
                        recv_sem=wait_sems[2 * i + 1].at[per * li + k], device_id=dev, device_id_type=MESH)
                    cp.wait_recv()
                    cp.wait_send()
        for i, (leg, ws) in enumerate(starts):
            fn, per = _LEGS[leg]
            for li, w in enumerate(ws):
                for k, (s_, d_, dev) in enumerate(fn(full[w], kinds[w], x, y, c)):
                    pltpu.make_async_remote_copy(
                        src_ref=s_, dst_ref=d_, send_sem=start_sems[2 * i].at[per * li + k],
                        recv_sem=start_sems[2 * i + 1].at[per * li + k], device_id=dev, device_id_type=MESH).start()

    sems = []
    for leg, ws in starts:
        sems += [pltpu.SemaphoreType.DMA((_LEGS[leg][1] * len(ws),))] * 2
    wait_args = []
    for _, _, s_, r_ in waits:
        wait_args += [s_, r_]
    outs = pl.pallas_call(
        body,
        out_shape=tuple(pltpu.HBM(f_.shape, f_.dtype) for f_ in fulls) + tuple(sems) + (SDS(thru.shape, thru.dtype),),
        in_specs=[_HBM] * nw + [_SEM] * (2 * n_wait) + [_ANY, _ANY],
        out_specs=tuple([_HBM] * nw + [_SEM] * (2 * n_start) + [_ANY]),
        input_output_aliases={**{w: w for w in range(nw)}, nw + 2 * n_wait + 1: nw + 2 * n_start},
        compiler_params=pltpu.CompilerParams(has_side_effects=_EFFECT),
        name=name,
    )(*[_hbm(f_) for f_ in fulls], *wait_args, after, thru)
    pairs = [(outs[nw + 2 * i], outs[nw + 2 * i + 1]) for i in range(n_start)]
    return list(outs[:nw]), pairs, outs[nw + 2 * n_start]


def _full_region(full, kind, px, py, half):
    j = 2 * px + py
    if kind == "col":
        rh, cols = full.shape[0] // 2, full.shape[1] // N_CHIP
        return full.at[pl.ds(half * rh, rh), pl.ds(j * cols, cols)]
    rows = full.shape[0] // N_CHIP
    rh = rows // 2
    return full.at[pl.ds(j * rows + half * rh, rh), :]


def _plan_scatter(kind):
    def plan(src, land, x, y, c):
        out = []
        for k, (px, py) in enumerate(_other_chips(x, y)):
            j = 2 * px + py
            if kind == "col":
                n = src.shape[1] // N_CHIP
                blk = src.at[:, pl.ds(j * n, n)]
            else:
                blk = src.at[j]
            out.append((blk, land.at[k], (px, py, c)))
        return out
    return plan


def _plan_whole(src, land, x, y, c):
    return [(src, land, (x, y, 1 - c))]


def _split_start(name, src, land_shape, n, plan, thru):
    def body(src_in, land_in, thru_in, send, recv, src_ref, land_ref, thru_out):
        x, y, c = _mesh_pos()
        for k, (s_, d_, dev) in enumerate(plan(src_ref, land_ref, x, y, c)):
            pltpu.make_async_remote_copy(src_ref=s_, dst_ref=d_, send_sem=send.at[k], recv_sem=recv.at[k],
                                         device_id=dev, device_id_type=MESH).start()

    sem = pltpu.SemaphoreType.DMA((n,))
    return pl.pallas_call(
        body,
        out_shape=(sem, sem, pltpu.HBM(src.shape, src.dtype), pltpu.HBM(land_shape, src.dtype), SDS(thru.shape, thru.dtype)),
        in_specs=[_HBM, _HBM, _ANY], out_specs=(_SEM, _SEM, _HBM, _HBM, _ANY),
        input_output_aliases={0: 2, 1: 3, 2: 4},
        compiler_params=pltpu.CompilerParams(has_side_effects=_EFFECT), name=name,
    )(_hbm(src), _hbm(lax.empty(land_shape, src.dtype)), thru)


def _split_wait(name, send, recv, src, land, plan, after):
    def body(src_in, land_in, send_r, recv_r, after_r, src_ref, land_ref):
        x, y, c = _mesh_pos()
        for k, (s_, d_, dev) in enumerate(plan(src_ref, land_ref, x, y, c)):
            cp = pltpu.make_async_remote_copy(src_ref=s_, dst_ref=d_, send_sem=send_r.at[k], recv_sem=recv_r.at[k],
                                              device_id=dev, device_id_type=MESH)
            cp.wait_send()
            cp.wait_recv()

    return pl.pallas_call(
        body,
        out_shape=(pltpu.HBM(src.shape, src.dtype), pltpu.HBM(land.shape, land.dtype)),
        in_specs=[_HBM, _HBM, _SEM, _SEM, _ANY], out_specs=(_HBM, _HBM),
        input_output_aliases={0: 0, 1: 1},
        compiler_params=pltpu.CompilerParams(has_side_effects=_EFFECT), name=name,
    )(src, land, send, recv, after)


def _dev_row(buf, px, py, pc):
    return buf.at[4 * px + 2 * py + pc]


def _plan_gather_own(buf, land, x, y, c):
    own = _dev_row(buf, x, y, c)
    return [(own, own, (x, y, 1 - c))] + [(own, own, (px, py, c)) for px, py in _other_chips(x, y)]


def _plan_gather_pass(buf, land, x, y, c):
    return [(_dev_row(buf, px, py, c), _dev_row(buf, px, py, c), (x, y, 1 - c)) for px, py in _other_chips(x, y)]


def _split_start_inplace(name, buf, n, plan, thru):
    def body(buf_in, thru_in, send, recv, buf_ref, thru_out):
        x, y, c = _mesh_pos()
        for k, (s_, d_, dev) in enumerate(plan(buf_ref, buf_ref, x, y, c)):
            pltpu.make_async_remote_copy(src_ref=s_, dst_ref=d_, send_sem=send.at[k], recv_sem=recv.at[k],
                                         device_id=dev, device_id_type=MESH).start()

    sem = pltpu.SemaphoreType.DMA((n,))
    return pl.pallas_call(
        body, out_shape=(sem, sem, pltpu.HBM(buf.shape, buf.dtype), SDS(thru.shape, thru.dtype)),
        in_specs=[_HBM, _ANY], out_specs=(_SEM, _SEM, _HBM, _ANY), input_output_aliases={0: 2, 1: 3},
        compiler_params=pltpu.CompilerParams(has_side_effects=_EFFECT), name=name)(_hbm(buf), thru)


def _split_wait_inplace(name, send, recv, buf, plan, after):
    def body(buf_in, send_r, recv_r, after_r, buf_ref):
        x, y, c = _mesh_pos()
        for k, (s_, d_, dev) in enumerate(plan(buf_ref, buf_ref, x, y, c)):
            cp = pltpu.make_async_remote_copy(src_ref=s_, dst_ref=d_, send_sem=send_r.at[k], recv_sem=recv_r.at[k],
                                              device_id=dev, device_id_type=MESH)
            cp.wait_send()
            cp.wait_recv()

    return pl.pallas_call(
        body, out_shape=pltpu.HBM(buf.shape, buf.dtype), in_specs=[_HBM, _SEM, _SEM, _ANY], out_specs=_HBM,
        input_output_aliases={0: 0}, compiler_params=pltpu.CompilerParams(has_side_effects=_EFFECT), name=name,
    )(buf, send, recv, after)


def _place_row(name, me_arr, slab):
    r, n = slab.shape
    tr = r if r <= PACK_ROWS else PACK_ROWS

    def body(me_ref, s_ref, o_ref):
        o_ref[...] = s_ref[...]

    return pl.pallas_call(
        body,
        grid_spec=pltpu.PrefetchScalarGridSpec(
            num_scalar_prefetch=1, grid=(r // tr,), in_specs=[pl.BlockSpec((tr, n), lambda i, me: (i, 0))],
            out_specs=pl.BlockSpec((None, tr, n), lambda i, me: (me[0], i, 0))),
        out_shape=SDS((N_DEV, r, n), slab.dtype), compiler_params=_params(("parallel",)), name=name)(me_arr, slab)


def _sum_partials(name, j_arr, part, got, kind):
    _, rh, n = got.shape
    tr = _tile(rh, 256)
    if kind == "col":
        p_spec = pl.BlockSpec((tr, n), lambda i, j: (i, j[0]))
    else:
        p_spec = pl.BlockSpec((None, tr, n), lambda i, j: (j[0], i, 0))

    def body(j_ref, p_ref, r_ref, o_ref):
        o_ref[...] = ((p_ref[...].astype(F32) + r_ref[0].astype(F32)) + r_ref[1].astype(F32)) + r_ref[2].astype(F32)

    return pl.pallas_call(
        body,
        grid_spec=pltpu.PrefetchScalarGridSpec(
            num_scalar_prefetch=1, grid=(rh // tr,),
            in_specs=[p_spec, pl.BlockSpec((3, tr, n), lambda i, j: (0, i, 0))],
            out_specs=pl.BlockSpec((tr, n), lambda i, j: (i, 0))),
        out_shape=SDS((rh, n), F32),
        compiler_params=_params(("parallel",)), name=name)(j_arr, part, got)


def _swap_reduced(name, halves):
    nw = len(halves)

    def body(*refs):
        h, got = refs[:nw], refs[nw:2 * nw]
        send_sems, recv_sems = refs[2 * nw:]
        x, y, c = _mesh_pos()
        cps = []
        for w in range(nw):
            cp = pltpu.make_async_remote_copy(
                src_ref=h[w], dst_ref=got[w], send_sem=send_sems.at[w], recv_sem=recv_sems.at[w],
                device_id=(x, y, 1 - c), device_id_type=MESH)
            cp.start()
            cps.append(cp)
        for cp in cps:
            cp.wait()

    return pl.pallas_call(
        body, out_shape=tuple(SDS(h.shape, h.dtype) for h in halves),
        in_specs=[_ANY] * nw, out_specs=tuple([_ANY] * nw),
        scratch_shapes=[pltpu.SemaphoreType.DMA((nw,)), pltpu.SemaphoreType.DMA((nw,))],
        name=name)(*halves)


def _pack(arrays):
    flat = [a.reshape(-1).astype(F32) for a in arrays]
    flat = [jnp.pad(f, (0, (-f.shape[0]) % LANES)) for f in flat]
    sizes = [f.shape[0] for f in flat]
    total = sum(sizes)
    rows = total // LANES
    tail = LANES * ((-rows) % (PACK_ROWS if rows > PACK_ROWS else SUBLANES))
    if tail:
        flat.append(jnp.zeros((tail,), F32))
    return jnp.concatenate(flat).reshape(-1, LANES), sizes


def _unpack(slab, sizes, shapes, lead=()):
    flat = slab.reshape(lead + (-1,))
    out, off = [], 0
    for sz, shp in zip(sizes, shapes):
        n = math.prod(shp)
        out.append(flat[..., off:off + n].reshape(lead + tuple(shp)))
        off += sz
    return out


def kernel(x, c, w_ada, b_ada, g_norm_mix, w_in, w_conv, b_conv, w_rg_a, b_rg_a, w_rg_x, b_rg_x, lru_lambda, g_attn_out, g_lru_out, w_out, g_norm_mlp, w_mlp_in, w_mlp_out, g_norm_final, loss_target, m_w_ada, m_b_ada, m_g_norm_mix, m_w_in, m_w_conv, m_b_conv, m_w_rg_a, m_b_rg_a, m_w_rg_x, m_b_rg_x, m_lru_lambda, m_g_attn_out, m_g_lru_out, m_w_out, m_g_norm_mlp, m_w_mlp_in, m_w_mlp_out, m_g_norm_final, v_w_ada, v_b_ada, v_g_norm_mix, v_w_in, v_w_conv, v_b_conv, v_w_rg_a, v_b_rg_a, v_w_rg_x, v_b_rg_x, v_lru_lambda, v_g_attn_out, v_g_lru_out, v_w_out, v_g_norm_mlp, v_w_mlp_in, v_w_mlp_out, v_g_norm_final):
    s, d = x.shape[1], x.shape[2]
    aw = d // 2
    nh = aw // HEAD
    f = w_mlp_out.shape[1] * N_CHIP
    n_ada = w_ada.shape[2]
    n_cv = w_conv.shape[2]
    ix, iy, ic = lax.axis_index("x"), lax.axis_index("y"), lax.axis_index("c")
    chip = 2 * ix + iy
    me = 2 * chip + ic
    c_arr = jnp.reshape(ic, (1,)).astype(jnp.int32)
    j_arr = jnp.reshape(chip, (1,)).astype(jnp.int32)

    x2d, tgt = x[0], loss_target[0]

    k_in, k_out, k_mi, k_mo = kinds = ("col", "row", "col", "row")
    slab, sizes = _pack([c, w_conv])
    p_in, _ = _place_cast("place_cast_0", j_arr, w_in[0], k_in, c)
    (f_in,), (dir_in,), slab = _gather_call("gather_0", [p_in], [k_in], [], [("direct", [0])], c, slab)
    p_out, tok = _place_cast("place_cast_1", j_arr, w_out[0], k_out, slab)
    p_mi, tok = _place_cast("place_cast_2", j_arr, w_mlp_in[0], k_mi, tok)
    p_mo, tok = _place_cast("place_cast_3", j_arr, w_mlp_out[0], k_mo, tok)

    gathered = _all_gather_small("comm_gather_cond", slab, after=(tok,))
    c_parts, cv_parts = _unpack(gathered, sizes, [(d,), (CONV_TAPS, n_cv)], lead=(N_DEV,))
    c_all = c_parts
    w_conv_full = jnp.concatenate([cv_parts[2 * j] for j in range(N_CHIP)], axis=-1)
    b_sh = lax.dynamic_slice(b_ada, (0, chip * n_ada), (1, n_ada))
    (f_in,), (rel_in, near_in), c_all = _gather_call(
        "gather_1", [f_in], [k_in], [("direct", [0], *dir_in)], [("relay", [0]), ("d2d_near", [0])], gathered, c_all)
    mod_part, act_all = _ada_mod(c_all, w_ada[0], b_sh)
    mod_g = _all_gather_small("comm_gather_mod", mod_part.reshape(-1, LANES))
    mod_g = mod_g.reshape(N_DEV, N_DEV, n_ada)
    mod = jnp.concatenate([lax.dynamic_index_in_dim(mod_g[2 * j], me, 0, keepdims=True) for j in range(N_CHIP)], axis=-1)
    sh1, sc1, gt1, sh2, sc2, gt2 = [mod[:, k * d:(k + 1) * d] for k in range(N_MOD)]

    (f_in, f_out, f_mi, f_mo), (far_in, dir_om, dir_mo), sh1 = _gather_call(
        "gather_2", [f_in, p_out, p_mi, p_mo], kinds, [("relay", [0], *rel_in)],
        [("d2d_far", [0]), ("direct", [1, 2]), ("direct", [3])], mod, sh1)

    h1, rstd1 = _norm_mod_fwd("norm_mod_fwd1", x2d, g_norm_mix, sc1, sh1)
    (w_in_f,), _, h1 = _gather_call("gather_3", [f_in], [k_in],
                                    [("d2d_near", [0], *near_in), ("d2d_far", [0], *far_in)], [], rstd1, h1)
    (qkv,) = _matmul("mm_qkv", h1, w_in_f, "nn", s, 3 * aw, d, (BF16,))
    (xrg,) = _matmul("mm_xrg", h1, w_in_f, "nn", s, 2 * aw, d, (F32,), b_off=3 * aw)
    o_attn, attn_w, attn_sg = _attn_fwd(qkv, nh)
    (f_out, f_mi), (rel_om, near_om), xrg = _gather_call(
        "gather_4", [f_out, f_mi], [k_out, k_mi], [("direct", [0, 1], *dir_om)],
        [("relay", [0, 1]), ("d2d_near", [0, 1])], o_attn, xrg)
    wa3, wx3 = w_rg_a[0], w_rg_x[0]
    o_lru, hseq = _lru_fwd(xrg, w_conv_full, b_conv, wa3, b_rg_a, wx3, b_rg_x, lru_lambda)
    mixed, rstd_a, rstd_l = _mix_norm_fwd(o_attn, o_lru, g_attn_out, g_lru_out)
    (f_out, f_mi, f_mo), (far_om, rel_mo, near_mo), mixed = _gather_call(
        "gather_5", [f_out, f_mi, f_mo], [k_out, k_mi, k_mo], [("relay", [0, 1], *rel_om), ("direct", [2], *dir_mo)],
        [("d2d_far", [0, 1]), ("relay", [2]), ("d2d_near", [2])], rstd_a, mixed)
    (w_out_f, w_mi_f), _, mixed = _gather_call(
        "gather_6", [f_out, f_mi], [k_out, k_mi], [("d2d_near", [0, 1], *near_om), ("d2d_far", [0, 1], *far_om)], [],
        rstd_l, mixed)

    def residual(acc, xin, gt):
        return acc, xin + gt * acc

    y1, x1 = _matmul("mm_out", mixed, w_out_f, "nn", s, d, d, (BF16, F32), extras=(x2d, gt1),
                     extra_kinds=("tile", "row"), epilogue=residual)
    h2, rstd2 = _norm_mod_fwd("norm_mod_fwd2", x1, g_norm_mlp, sc2, sh2)
    (f_mo,), (far_mo,), h2 = _gather_call("gather_7", [f_mo], [k_mo], [("relay", [0], *rel_mo)],
                                          [("d2d_far", [0])], rstd2, h2)

    def sq_relu(acc):
        r = jnp.maximum(acc, 0.0)
        return 2.0 * r, r * r

    r2, hid = _matmul("mm_mlp_in", h2, w_mi_f, "nn", s, f, d, (BF16, BF16), epilogue=sq_relu)
    (w_mo_f,), _, hid = _gather_call("gather_8", [f_mo], [k_mo],
                                     [("d2d_near", [0], *near_mo), ("d2d_far", [0], *far_mo)], [], r2, hid)
    y2, x2 = _matmul("mm_mlp_out", hid, w_mo_f, "nn", s, d, f, (BF16, F32), extras=(x1, gt2),
                     extra_kinds=("tile", "row"), epilogue=residual)
    dx2, loss_row, dg_final, dy2, dgt2 = _final_loss(x2, g_norm_final.reshape(1, d), tgt, y2, gt2)

    oc_arr = 1 - c_arr

    def dw_half(name, st, h_arr, got=None):
        add = {} if got is None else dict(extras=(got,), extra_kinds=("tile",),
                                          epilogue=lambda acc, g_: (acc + g_.astype(F32),))
        (out,) = _matmul(name, st["a"], st["dy"], "tn", st["m"], st["n"], s, (BF16,), tm=st["tm"], m_half=h_arr, **add)
        return out

    def rs_begin(tag, kind, xa, dy, m, n, thru):
        st = {"tag": tag, "kind": kind, "a": xa, "dy": dy, "m": m, "n": n,
              "tm": m // (2 * N_CHIP) if kind == "row" else m // 2}
        first = dw_half("mm_dw_%s_a" % tag, st, oc_arr)
        send, recv, first, land, thru = _split_start("rs_swap_start_" + tag, first, first.shape, 1, _plan_whole, thru)
        st["swap"] = (send, recv, first, land)
        return st, thru

    def rs_mid(st, after, thru):
        tag, kind = st["tag"], st["kind"]
        _, got = _split_wait("rs_swap_wait_" + tag, *st["swap"], _plan_whole, after)
        part = dw_half("mm_dw_%s_b" % tag, st, c_arr, got)
        if kind == "row":
            part = part.reshape(N_CHIP, st["m"] // (2 * N_CHIP), st["n"])
        blk = (part.shape[0], part.shape[1] // N_CHIP) if kind == "col" else part.shape[1:]
        send, recv, part, land, thru = _split_start("rs_scatter_start_" + tag, part, (N_CHIP - 1,) + blk, N_CHIP - 1,
                                                    _plan_scatter(kind), thru)
        st["scatter"] = (send, recv, part, land)
        return thru

    def rs_end(st, after):
        tag, kind = st["tag"], st["kind"]
        part, got = _split_wait("rs_scatter_wait_" + tag, *st["scatter"], _plan_scatter(kind), after)
        return _sum_partials("sum_partials_" + tag, j_arr, part, got, kind)

    (dpre,) = _matmul("mm_dhid", dy2, w_mo_f, "nt", s, f, d, (BF16,), extras=(r2,), extra_kinds=("tile",),
                      epilogue=lambda acc, r: (acc * r.astype(F32),))
    st_mo, dpre = rs_begin("mo", "row", hid, dy2, f, d, dpre)
    (dh2,) = _matmul("mm_dh2", dpre, w_mi_f, "nt", s, d, f, (BF16,))
    dh2 = rs_mid(st_mo, dh2, dh2)
    st_mi, dh2 = rs_begin("mi", "col", h2, dpre, d, f, dh2)
    dx1, dsh2, dsc2, dg_mlp, dy1, dgt1 = _norm_mod_bwd("norm_mod_bwd2", dh2, x1, rstd2, g_norm_mlp, sc2, dx2,
                                                       gate=(y1, gt1))
    (dmixed,) = _matmul("mm_dmixed", dy1, w_out_f, "nt", s, d, d, (BF16,))
    dmixed = rs_mid(st_mi, dmixed, dmixed)
    st_out, dmixed = rs_begin("out", "row", mixed, dy1, d, d, dmixed)
    do_attn, do_lru, dg_attn, dg_lru = _mix_norm_bwd(dmixed, o_attn, o_lru, rstd_a, rstd_l, g_attn_out, g_lru_out)
    dq, dk, dv = _attn_bwd(qkv, do_attn, attn_w, attn_sg, nh)
    do_lru = rs_mid(st_out, dq, do_lru)
    dxr, dxg, dwconv, dbconv, dwa, dba, dwx, dbx, dlam = _lru_bwd(
        xrg, do_lru, hseq, w_conv_full, b_conv, wa3, b_rg_a, wx3, b_rg_x, lru_lambda)
    dproj = jnp.concatenate([dq, dk, dv, dxr, dxg], axis=-1)
    st_in, dproj = rs_begin("in", "col", h1, dproj, d, 5 * aw, dproj)
    st_in["dy"] = dproj
    (dh1,) = _matmul("mm_dh1", dproj, w_in_f, "nt", s, d, 5 * aw, (BF16,))
    dh1 = rs_mid(st_in, dh1, dh1)
    grad_x, dsh1, dsc1, dg_mix = _norm_mod_bwd("norm_mod_bwd1", dh1, x2d, rstd1, g_norm_mix, sc1, dx1)

    dmod = jnp.concatenate([dsh1, dsc1, dgt1, dsh2, dsc2, dgt2], axis=-1)
    small_names = ["b_ada", "g_norm_mix", "b_conv", "w_rg_a", "b_rg_a", "w_rg_x", "b_rg_x", "lru_lambda",
                   "g_attn_out", "g_lru_out", "g_norm_mlp", "g_norm_final"]
    small_g = [dmod, dg_mix, dbconv, dwa, dba, dwx, dbx, dlam, dg_attn, dg_lru, dg_mlp, dg_final]
    small_w = [b_ada, g_norm_mix, b_conv, w_rg_a, b_rg_a, w_rg_x, b_rg_x, lru_lambda, g_attn_out, g_lru_out, g_norm_mlp, g_norm_final]
    small_m = [m_b_ada, m_g_norm_mix, m_b_conv, m_w_rg_a, m_b_rg_a, m_w_rg_x, m_b_rg_x, m_lru_lambda, m_g_attn_out, m_g_lru_out, m_g_norm_mlp, m_g_norm_final]
    small_v = [v_b_ada, v_g_norm_mix, v_b_conv, v_w_rg_a, v_b_rg_a, v_w_rg_x, v_b_rg_x, v_lru_lambda, v_g_attn_out, v_g_lru_out, v_g_norm_mlp, v_g_norm_final]
    extra_zero = [jnp.zeros_like(dwconv), jnp.zeros((LANES,), F32)]
    g_slab, g_sizes = _pack(small_g + [dwconv, loss_row])
    w_slab, _ = _pack(small_w + extra_zero)
    m_slab, _ = _pack(small_m + extra_zero)
    v_slab, _ = _pack(small_v + extra_zero)
    me_arr = jnp.reshape(me, (1,)).astype(jnp.int32)
    g_buf = _place_row("place_small_grads", me_arr, g_slab)
    sg_send, sg_recv, g_buf, tok = _split_start_inplace("sg_gather_start", g_buf, N_CHIP, _plan_gather_own, loss_row)

    def reduced_begin(tag, half, tok_):
        send, recv, half, land, tok_ = _split_start("rs_reduced_start_" + tag, half, half.shape, 1, _plan_whole, tok_)
        return (send, recv, half, land), tok_

    def reduced_end(tag, st, after):
        return _split_wait("rs_reduced_wait_" + tag, *st, _plan_whole, after)

    sw_mo, tok = reduced_begin("mo", rs_end(st_mo, tok), tok)
    sw_mi, tok = reduced_begin("mi", rs_end(st_mi, tok), tok)
    sw_out, tok = reduced_begin("out", rs_end(st_out, tok), tok)
    half_mo, got_mo = reduced_end("mo", sw_mo, tok)
    big = {"w_mlp_out": _adamw_halves("adamw_w_mlp_out", c_arr, w_mlp_out[0], m_w_mlp_out[0], v_w_mlp_out[0],
                                      half_mo, got_mo)}
    half_mi, got_mi = reduced_end("mi", sw_mi, big["w_mlp_out"][1])
    big["w_mlp_in"] = _adamw_halves("adamw_w_mlp_in", c_arr, w_mlp_in[0], m_w_mlp_in[0], v_w_mlp_in[0], half_mi, got_mi)
    half_out, got_out = reduced_end("out", sw_out, big["w_mlp_in"][1])
    big["w_out"] = _adamw_halves("adamw_w_out", c_arr, w_out[0], m_w_out[0], v_w_out[0], half_out, got_out)
    g_buf = _split_wait_inplace("sg_gather_wait", sg_send, sg_recv, g_buf, _plan_gather_own, big["w_out"][1])
    sg_send, sg_recv, g_buf, tok = _split_start_inplace("sg_pass_start", g_buf, N_CHIP - 1, _plan_gather_pass, tok)
    half_in = rs_end(st_in, tok)
    g_all = _split_wait_inplace("sg_pass_wait", sg_send, sg_recv, g_buf, _plan_gather_pass, half_in)
    gs_slab, ds_slab, ms_slab, vs_slab = _small_reduce_adamw(g_all, w_slab, m_slab, v_slab)
    shapes = [w.shape for w in small_w] + [dwconv.shape, (LANES,)]
    gs = _unpack(gs_slab, g_sizes, shapes)
    ds = _unpack(ds_slab, g_sizes, shapes)
    ms = _unpack(ms_slab, g_sizes, shapes)
    vs = _unpack(vs_slab, g_sizes, shapes)
    small = {n: (gs[i], ds[i], ms[i], vs[i]) for i, n in enumerate(small_names)}
    loss = gs[-1][0]
    g_wconv = lax.dynamic_slice(gs[-2], (0, chip * n_cv), (CONV_TAPS, n_cv))
    d_wconv, m_wconv, v_wconv = _adamw_plain("adamw_conv", w_conv[0], g_wconv, m_w_conv[0], v_w_conv[0])
    small["w_conv"] = (g_wconv[None], d_wconv[None], m_wconv[None], v_wconv[None])

    dmod_all = g_all[:, :N_MOD * d // LANES, :].reshape(N_DEV, N_MOD * d)
    dmod_sel = lax.dynamic_slice(dmod_all, (0, chip * n_ada), (N_DEV, n_ada)).astype(BF16)
    act_t = act_all.T.astype(BF16)
    big["w_ada"] = _adamw_ada(w_ada[0], m_w_ada[0], v_w_ada[0], act_t, dmod_sel)

    (got_in,) = _swap_reduced("comm_swap_reduced_in", [half_in])
    big["w_in"] = _adamw_halves("adamw_w_in", c_arr, w_in[0], m_w_in[0], v_w_in[0], half_in, got_in)

    order = ["w_ada", "b_ada", "g_norm_mix", "w_in", "w_conv", "b_conv", "w_rg_a", "b_rg_a", "w_rg_x", "b_rg_x",
             "lru_lambda", "g_attn_out", "g_lru_out", "w_out", "g_norm_mlp", "w_mlp_in", "w_mlp_out", "g_norm_final"]
    res = {}
    for n in order:
        res[n] = tuple(t[None] for t in big[n]) if n in big else small[n]
    return (loss, grad_x[None],
            *[res[n][0] for n in order], *[res[n][1] for n in order],
            *[res[n][2] for n in order], *[res[n][3] for n in order])
```

```python
import functools
import math

import jax
import jax.numpy as jnp
from jax import lax
from jax.experimental import pallas as pl
from jax.experimental.pallas import tpu as pltpu

F32 = jnp.float32
BF16 = jnp.bfloat16
SDS = jax.ShapeDtypeStruct
MESH = pl.DeviceIdType.MESH

EPS = 1e-6
HEAD = 128
N_MOD = 6
CONV_TAPS = 4
LRU_C = 8.0
ADAM_LR, ADAM_B1, ADAM_B2, ADAM_EPS, ADAM_WD, ADAM_STEP = 0.001, 0.9, 0.999, 1e-08, 0.01, 10
N_DEV = 8
N_CHIP = 4
LANES = 128
SUBLANES = 8
VMEM_LIMIT = 56 * 1024 * 1024
PACK_ROWS = 256
MM_TILE_M, MM_TILE_N, MM_TILE_K = 1024, 1024, 2048
ROW_TILE = 256
ROW_SPLIT = 1


def _tile(dim, pref):
    t = min(dim, pref)
    while dim % t:
        t -= LANES
    return t


def _params(sem=None):
    return pltpu.CompilerParams(dimension_semantics=sem, vmem_limit_bytes=VMEM_LIMIT)


def _sigmoid(x):
    return 1.0 / (1.0 + jnp.exp(-x))


def _log_sigmoid(x):
    return jnp.minimum(x, 0.0) - jnp.log(1.0 + jnp.exp(-jnp.abs(x)))


def _gelu_parts(x):
    k0, k1 = math.sqrt(2.0 / math.pi), 0.044715
    t = jnp.tanh(k0 * (x + k1 * x * x * x))
    val = 0.5 * x * (1.0 + t)
    der = 0.5 * (1.0 + t) + 0.5 * x * (1.0 - t * t) * k0 * (1.0 + 3.0 * k1 * x * x)
    return val, der


def _dot(a, b):
    return jnp.dot(a, b, preferred_element_type=F32)


def _dot_nt(a, b):
    return lax.dot_general(a, b, (((1,), (1,)), ((), ())), preferred_element_type=F32)


def _dot_tn(a, b):
    return lax.dot_general(a, b, (((0,), (0,)), ((), ())), preferred_element_type=F32)


def _split_dot(x, tri):
    hi = x.astype(BF16)
    lo = (x - hi.astype(F32)).astype(BF16)
    return _dot(hi, tri) + _dot(lo, tri)


def _matmul(name, a, b, mode, m, n, k, out_dtypes, *, b_off=0, extras=(), extra_kinds=(), epilogue=None,
            tm=MM_TILE_M, tn=MM_TILE_N, tk=MM_TILE_K, m_half=None):
    tm, tn, tk = _tile(m, tm), _tile(math.gcd(n, b_off) if b_off else n, tn), _tile(k, tk)
    assert b_off % tn == 0
    nk = k // tk
    n_ex, n_out = len(extras), len(out_dtypes)
    dot = {"nn": _dot, "nt": _dot_nt, "tn": _dot_tn}[mode]
    n_pre = 0 if m_half is None else 1
    m_out = m if m_half is None else m // 2

    def body(*refs):
        a_ref, b_ref, *rest = refs[n_pre:]
        ex, outs = rest[:n_ex], rest[n_ex:n_ex + n_out]

        def finish(total):
            res = epilogue(total, *[e[...] for e in ex]) if epilogue else (total,)
            for o, r in zip(outs, res):
                o[...] = r.astype(o.dtype)

        if nk == 1:
            finish(dot(a_ref[...], b_ref[...]))
            return
        acc = rest[-1]
        kk = pl.program_id(2)

        @pl.when(kk == 0)
        def _():
            acc[...] = dot(a_ref[...], b_ref[...])

        @pl.when(jnp.logical_and(kk > 0, kk < nk - 1))
        def _():
            acc[...] += dot(a_ref[...], b_ref[...])

        @pl.when(kk == nk - 1)
        def _():
            finish(acc[...] + dot(a_ref[...], b_ref[...]))

    if mode == "nn":
        a_spec = pl.BlockSpec((tm, tk), lambda i, j, kk, *_: (i, kk))
        b_spec = pl.BlockSpec((tk, tn), lambda i, j, kk, *_: (kk, j + b_off // tn))
    elif mode == "nt":
        a_spec = pl.BlockSpec((tm, tk), lambda i, j, kk, *_: (i, kk))
        b_spec = pl.BlockSpec((tn, tk), lambda i, j, kk, *_: (j, kk + b_off // tk))
    elif m_half is None:
        a_spec = pl.BlockSpec((tk, tm), lambda i, j, kk: (kk, i))
        b_spec = pl.BlockSpec((tk, tn), lambda i, j, kk: (kk, j))
    else:
        a_spec = pl.BlockSpec((tk, tm), lambda i, j, kk, h: (kk, 2 * i + h[0]))
        b_spec = pl.BlockSpec((tk, tn), lambda i, j, kk, h: (kk, j))
    tile_spec = pl.BlockSpec((tm, tn), lambda i, j, kk, *_: (i, j))
    row_spec = pl.BlockSpec((1, tn), lambda i, j, kk, *_: (0, j))
    in_specs = [a_spec, b_spec] + [tile_spec if kind == "tile" else row_spec for kind in extra_kinds]
    out_specs = tuple(tile_spec for _ in out_dtypes)
    out_shape = tuple(SDS((m_out, n), dt) for dt in out_dtypes)
    scratch = [pltpu.VMEM((tm, tn), F32)] if nk > 1 else []
    grid = (m_out // tm, n // tn, nk)
    semantics = _params(("parallel", "parallel", "arbitrary"))
    if m_half is None:
        return pl.pallas_call(body, grid=grid, in_specs=in_specs, out_specs=out_specs, out_shape=out_shape,
                              scratch_shapes=scratch, compiler_params=semantics, name=name)(a, b, *extras)
    assert mode == "tn" and (m // tm) % 2 == 0
    return pl.pallas_call(
        body,
        grid_spec=pltpu.PrefetchScalarGridSpec(num_scalar_prefetch=1, grid=grid, in_specs=in_specs, out_specs=out_specs,
                                               scratch_shapes=scratch),
        out_shape=out_shape, compiler_params=semantics, name=name)(m_half, a, b, *extras)


def _row_specs(s, d, tr):
    row = "row"
    vec = pl.BlockSpec((1, d), lambda i: (0, 0))
    col = pl.BlockSpec((tr, 1), lambda i: (i, 0))
    return row, vec, col


class _ColChunks:
    def __init__(self, refs):
        self.refs = refs

    def __getitem__(self, idx):
        return jnp.concatenate([r[...] for r in self.refs], axis=-1)


def _rows_call(name, body, grid, in_specs, out_specs, out_shape, semantics, args):
    in_x, args_x, groups = [], [], []
    for spec, arr in zip(in_specs, args):
        if isinstance(spec, str):
            rows, d = arr.shape
            tr, dc = rows // grid[0], d // ROW_SPLIT
            in_x += [pl.BlockSpec((tr, dc), functools.partial(lambda i, jj: (i, jj), jj=j)) for j in range(ROW_SPLIT)]
            args_x += [arr] * ROW_SPLIT
            groups.append(ROW_SPLIT)
        else:
            in_x.append(spec)
            args_x.append(arr)
            groups.append(1)
    out_x = [pl.BlockSpec((sh.shape[0] // grid[0], sh.shape[1]), lambda i: (i, 0)) if isinstance(spec, str) else spec
             for spec, sh in zip(out_specs, out_shape)]

    def wrapped(*refs):
        views, k = [], 0
        for g in groups:
            views.append(_ColChunks(refs[k:k + g]) if g > 1 else refs[k])
            k += g
        body(*views, *refs[k:])

    return pl.pallas_call(
        wrapped, grid=grid, in_specs=in_x, out_specs=tuple(out_x), out_shape=tuple(out_shape),
        compiler_params=_params(semantics), name=name)(*args_x)


def _norm_mod_fwd(name, x, g, sc, sh):
    s, d = x.shape
    tr = _tile(s, ROW_TILE)
    row, vec, col = _row_specs(s, d, tr)

    def body(x_ref, g_ref, sc_ref, sh_ref, h_ref, r_ref):
        xv = x_ref[...]
        r = lax.rsqrt(jnp.mean(xv * xv, axis=-1, keepdims=True) + EPS)
        h_ref[...] = ((xv * r * g_ref[...]) * (1.0 + sc_ref[...]) + sh_ref[...]).astype(BF16)
        r_ref[...] = r

    return _rows_call(name, body, (s // tr,), [row, vec, vec, vec], (row, col),
                      (SDS((s, d), BF16), SDS((s, 1), F32)), ("parallel",), (x, g, sc, sh))


def _norm_mod_bwd(name, dh, xin, rstd, g, sc, dres, gate=None):
    s, d = xin.shape
    tr = _tile(s, ROW_TILE)
    row, vec, col = _row_specs(s, d, tr)

    n_gate = 2 if gate is not None else 0

    def body(dh_ref, x_ref, r_ref, g_ref, sc_ref, dres_ref, *rest):
        gate_in, gate_out = rest[:n_gate], rest[n_gate + 4:]
        dx_ref, dsh_ref, dsc_ref, dg_ref = rest[n_gate:n_gate + 4]

        @pl.when(pl.program_id(0) == 0)
        def _():
            for ref in (dsh_ref, dsc_ref, dg_ref) + tuple(gate_out[1:]):
                ref[...] = jnp.zeros_like(ref)

        dh_v, xv, r, gv = dh_ref[...].astype(F32), x_ref[...], r_ref[...], g_ref[...]
        n0 = xv * r
        dsh_ref[...] += jnp.sum(dh_v, axis=0, keepdims=True)
        dsc_ref[...] += jnp.sum(dh_v * (n0 * gv), axis=0, keepdims=True)
        dn = dh_v * (1.0 + sc_ref[...])
        dg_ref[...] += jnp.sum(dn * n0, axis=0, keepdims=True)
        gy = dn * gv
        dot = jnp.mean(gy * xv, axis=-1, keepdims=True)
        dxv = dres_ref[...] + r * gy - xv * (r * r * r * dot)
        dx_ref[...] = dxv
        if gate is not None:
            y_ref, gt_ref = gate_in
            dy_ref, dgt_ref = gate_out
            dy_ref[...] = (gt_ref[...] * dxv).astype(BF16)
            dgt_ref[...] += jnp.sum(dxv * y_ref[...], axis=0, keepdims=True)

    vecs = SDS((1, d), F32)
    gate_args = tuple(gate) if gate is not None else ()
    return _rows_call(
        name, body, (s // tr,),
        [row, row, col, vec, vec, row] + ([row, vec] if gate is not None else []),
        (row, vec, vec, vec) + ((row, vec) if gate is not None else ()),
        (SDS((s, d), F32), vecs, vecs, vecs) + ((SDS((s, d), BF16), vecs) if gate is not None else ()),
        ("arbitrary",), (dh, xin, rstd, g, sc, dres, *gate_args))


def _final_loss(x2, gf, tgt, y, gt):
    s, d = x2.shape
    tr = _tile(s, ROW_TILE)
    row, vec, _ = _row_specs(s, d, tr)
    lrow = pl.BlockSpec((1, LANES), lambda i: (0, 0))

    def body(x_ref, g_ref, t_ref, y_ref, gt_ref, dx_ref, loss_ref, dg_ref, dy_ref, dgt_ref):
        @pl.when(pl.program_id(0) == 0)
        def _():
            loss_ref[...] = jnp.zeros_like(loss_ref)
            dg_ref[...] = jnp.zeros_like(dg_ref)
            dgt_ref[...] = jnp.zeros_like(dgt_ref)

        xv, gv = x_ref[...], g_ref[...]
        r = lax.rsqrt(jnp.mean(xv * xv, axis=-1, keepdims=True) + EPS)
        n0 = xv * r
        err = n0 * gv - t_ref[...]
        loss_ref[...] += jnp.sum(err * err) * (0.5 / d)
        dy = err * (1.0 / d)
        dg_ref[...] += jnp.sum(dy * n0, axis=0, keepdims=True)
        gy = dy * gv
        dot = jnp.mean(gy * xv, axis=-1, keepdims=True)
        dxv = r * gy - xv * (r * r * r * dot)
        dx_ref[...] = dxv
        dy_ref[...] = (gt_ref[...] * dxv).astype(BF16)
        dgt_ref[...] += jnp.sum(dxv * y_ref[...], axis=0, keepdims=True)

    return _rows_call(
        "final_loss", body, (s // tr,), [row, vec, row, row, vec], (row, lrow, vec, row, vec),
        (SDS((s, d), F32), SDS((1, LANES), F32), SDS((1, d), F32), SDS((s, d), BF16), SDS((1, d), F32)),
        ("arbitrary",), (x2, gf, tgt, y, gt))


def _mix_norm_fwd(oa, ol, ga, gl):
    s, w = oa.shape
    tr = _tile(s, ROW_TILE)
    row, vec, col = _row_specs(s, w, tr)

    def body(oa_ref, ol_ref, ga_ref, gl_ref, mx_ref, ra_ref, rl_ref):
        a, l = oa_ref[...], ol_ref[...]
        ra = lax.rsqrt(jnp.mean(a * a, axis=-1, keepdims=True) + EPS)
        rl = lax.rsqrt(jnp.mean(l * l, axis=-1, keepdims=True) + EPS)
        mx_ref[:, :w] = (a * ra * ga_ref[...]).astype(BF16)
        mx_ref[:, w:] = (l * rl * gl_ref[...]).astype(BF16)
        ra_ref[...] = ra
        rl_ref[...] = rl

    return _rows_call(
        "mix_norm_fwd", body, (s // tr,), [row, row, vec, vec], (row, col, col),
        (SDS((s, 2 * w), BF16), SDS((s, 1), F32), SDS((s, 1), F32)), ("parallel",), (oa, ol, ga, gl))


def _mix_norm_bwd(dmx, oa, ol, ra, rl, ga, gl):
    s, w = oa.shape
    tr = _tile(s, ROW_TILE)
    row, vec, col = _row_specs(s, w, tr)

    def body(dm_ref, oa_ref, ol_ref, ra_ref, rl_ref, ga_ref, gl_ref, doa_ref, dol_ref, dga_ref, dgl_ref):
        @pl.when(pl.program_id(0) == 0)
        def _():
            dga_ref[...] = jnp.zeros_like(dga_ref)
            dgl_ref[...] = jnp.zeros_like(dgl_ref)

        def one(dy, xv, r, gv, dg_ref):
            dg_ref[...] += jnp.sum(dy * (xv * r), axis=0, keepdims=True)
            gy = dy * gv
            dot = jnp.mean(gy * xv, axis=-1, keepdims=True)
            return r * gy - xv * (r * r * r * dot)

        dm = dm_ref[...].astype(F32)
        doa_ref[...] = one(dm[:, :w], oa_ref[...], ra_ref[...], ga_ref[...], dga_ref).astype(BF16)
        dol_ref[...] = one(dm[:, w:], ol_ref[...], rl_ref[...], gl_ref[...], dgl_ref)

    return _rows_call(
        "mix_norm_bwd", body, (s // tr,), [row, row, row, col, col, vec, vec], (row, row, vec, vec),
        (SDS((s, w), BF16), SDS((s, w), F32), SDS((1, w), F32), SDS((1, w), F32)), ("arbitrary",),
        (dmx, oa, ol, ra, rl, ga, gl))


def _attn_blocks(qs, ks, tri_after, csums, causal):
    zs = [_dot_nt(q, k) * (HEAD ** -0.5) for q, k in zip(qs, ks)]
    lbs = [_log_sigmoid(z) for z in zs]
    lss = [lb - z for lb, z in zip(lbs, zs)]
    if causal is not None:
        lss = [jnp.where(causal, ls, 0.0) for ls in lss]
    locs = [_split_dot(ls, tri_after) for ls in lss]
    ws = [jnp.exp(lb + (loc + cs)) for lb, loc, cs in zip(lbs, locs, csums)]
    if causal is not None:
        ws = [jnp.where(causal, w, 0.0) for w in ws]
    nxt = [cs + (loc[:, 0:1] + ls[:, 0:1]) for cs, loc, ls in zip(csums, locs, lss)]
    return lbs, ws, nxt


ATTN_HEADS_PER_STEP = 4


def _attn_tile(s):
    return 256 if s >= 1024 else 128


def _tri(t, after):
    r_i = lax.broadcasted_iota(jnp.int32, (t, t), 0)
    c_i = lax.broadcasted_iota(jnp.int32, (t, t), 1)
    return ((r_i > c_i) if after else (r_i < c_i)).astype(BF16)


def _attn_fwd(qkv, n_heads):
    s = qkv.shape[0]
    t = _attn_tile(s)
    hps = ATTN_HEADS_PER_STEP
    wid = hps * HEAD

    nq = s // t

    def body(q_ref, k_ref, v_ref, o_ref, w_ref, sg_ref):
        qi = pl.program_id(1)
        tri_after = _tri(t, True)
        causal = lax.broadcasted_iota(jnp.int32, (t, t), 1) < lax.broadcasted_iota(jnp.int32, (t, t), 0)
        lanes = [slice(a * HEAD, (a + 1) * HEAD) for a in range(hps)]
        qs = [q_ref[:, ln] for ln in lanes]

        def block(kb, carry, mask):
            off = pl.multiple_of(kb * t, t)
            ks = [k_ref[pl.ds(off, t), ln] for ln in lanes]
            lbs, ws, csums = _attn_blocks(qs, ks, tri_after, [cr[0] for cr in carry], mask)
            wbs = [w.astype(BF16) for w in ws]
            for a in range(hps):
                w_ref[a, kb] = wbs[a]
                sg_ref[a, kb] = jnp.exp(lbs[a]).astype(BF16)
            os_ = [cr[1] + _dot(wb, v_ref[pl.ds(off, t), ln]) for cr, wb, ln in zip(carry, wbs, lanes)]
            return tuple(zip(csums, os_))

        zero = tuple((jnp.zeros((t, 1), F32), jnp.zeros((t, HEAD), F32)) for _ in lanes)
        carry = block(qi, zero, causal)
        carry = lax.fori_loop(1, qi + 1, lambda it, cr: block(qi - it, cr, None), carry)
        for a, ln in enumerate(lanes):
            o_ref[:, ln] = carry[a][1]

    hb = n_heads // hps
    kept = pl.BlockSpec((None, hps, nq, t, t), lambda hh, i: (hh * nq + i, 0, 0, 0, 0))
    kept_shape = SDS((hb * nq, hps, nq, t, t), BF16)
    return pl.pallas_call(
        body, grid=(hb, nq),
        in_specs=[pl.BlockSpec((t, wid), lambda hh, i: (i, hh)),
                  pl.BlockSpec((s, wid), lambda hh, i: (0, hb + hh)),
                  pl.BlockSpec((s, wid), lambda hh, i: (0, 2 * hb + hh))],
        out_specs=(pl.BlockSpec((t, wid), lambda hh, i: (i, hh)), kept, kept),
        out_shape=(SDS((s, n_heads * HEAD), F32), kept_shape, kept_shape),
        compiler_params=_params(("parallel", "parallel")), name="attn_fwd")(qkv, qkv, qkv)


def _attn_bwd(qkv, do, w_kept, sg_kept, n_heads):
    s = qkv.shape[0]
    t = _attn_tile(s)
    nq = s // t
    scale = HEAD ** -0.5
    hps = ATTN_HEADS_PER_STEP
    wid = hps * HEAD

    def body(q_ref, k_ref, v_ref, do_ref, w_ref, sg_ref, dq_ref, dk_ref, dv_ref, dk_acc, dv_acc):
        qi = pl.program_id(1)

        @pl.when(qi == 0)
        def _():
            dk_acc[...] = jnp.zeros_like(dk_acc)
            dv_acc[...] = jnp.zeros_like(dv_acc)

        tri_before = _tri(t, False)
        causal = lax.broadcasted_iota(jnp.int32, (t, t), 1) < lax.broadcasted_iota(jnp.int32, (t, t), 0)
        lanes = [slice(a * HEAD, (a + 1) * HEAD) for a in range(hps)]
        qs = [q_ref[:, ln] for ln in lanes]
        douts = [do_ref[:, ln] for ln in lanes]

        def block(kb, carry, mask):
            off = pl.multiple_of(kb * t, t)
            wbs = [w_ref[a, kb] for a in range(hps)]
            dws = [_dot_nt(dout, v_ref[pl.ds(off, t), ln]) for dout, ln in zip(douts, lanes)]
            for a, ln in enumerate(lanes):
                dv_acc[pl.ds(off, t), ln] += _dot_tn(wbs[a], douts[a])
            es = [dw * wb.astype(F32) for dw, wb in zip(dws, wbs)]
            locs = [_split_dot(e, tri_before) for e in es]
            sgs = [sg_ref[a, kb].astype(F32) for a in range(hps)]
            stays = [(loc + cr[0]) * sg for loc, cr, sg in zip(locs, carry, sgs)]
            if mask is not None:
                stays = [jnp.where(mask, st, 0.0) for st in stays]
            dzbs = [((e * (1.0 - sg) - st) * scale).astype(BF16) for e, sg, st in zip(es, sgs, stays)]
            dqs = [cr[1] + _dot(dzb, k_ref[pl.ds(off, t), ln]) for cr, dzb, ln in zip(carry, dzbs, lanes)]
            for a, ln in enumerate(lanes):
                dk_acc[pl.ds(off, t), ln] += _dot_tn(dzbs[a], qs[a])
            esums = [cr[0] + (loc[:, t - 1:t] + e[:, t - 1:t]) for cr, loc, e in zip(carry, locs, es)]
            return tuple(zip(esums, dqs))

        zero = tuple((jnp.zeros((t, 1), F32), jnp.zeros((t, HEAD), F32)) for _ in lanes)
        carry = lax.fori_loop(0, qi, lambda kb, cr: block(kb, cr, None), zero)
        carry = block(qi, carry, causal)
        for a, ln in enumerate(lanes):
            dq_ref[:, ln] = carry[a][1].astype(BF16)

        @pl.when(qi == nq - 1)
        def _():
            dk_ref[...] = dk_acc[...].astype(BF16)
            dv_ref[...] = dv_acc[...].astype(BF16)

    hb = n_heads // hps
    blk = pl.BlockSpec((t, wid), lambda hh, i: (i, hh))
    full = pl.BlockSpec((s, wid), lambda hh, i: (0, hh))
    kept = pl.BlockSpec((None, hps, nq, t, t), lambda hh, i: (hh * nq + i, 0, 0, 0, 0))
    return pl.pallas_call(
        body, grid=(hb, nq),
        in_specs=[blk,
                  pl.BlockSpec((s, wid), lambda hh, i: (0, hb + hh)),
                  pl.BlockSpec((s, wid), lambda hh, i: (0, 2 * hb + hh)),
                  blk, kept, kept],
        out_specs=(blk, full, full),
        out_shape=(SDS((s, n_heads * HEAD), BF16),) * 3,
        scratch_shapes=[pltpu.VMEM((s, wid), F32), pltpu.VMEM((s, wid), F32)],
        compiler_params=_params(("parallel", "arbitrary")), name="attn_bwd")(qkv, qkv, qkv, do, w_kept, sg_kept)


def _lru_chunk(s):
    return 128 if s >= 256 else s // 2


def _lru_gates(xc, wa, ba, wx, bx, sp):
    xb = xc.astype(BF16)
    r = _sigmoid(_dot(xb, wa) + ba)
    ig = _sigmoid(_dot(xb, wx) + bx)
    la = -LRU_C * r * sp
    a = jnp.exp(la)
    t = jnp.tanh(la)
    mult = jnp.sqrt(-2.0 * t / (1.0 - t))
    return r, ig, a, mult


def _softplus_neg(lam):
    return jnp.maximum(-lam, 0.0) + jnp.log(1.0 + jnp.exp(-jnp.abs(lam)))


LRU_BLOCKS_PER_STEP = 1
SCAN_GROUP = 4


def _lru_specs(s, n_blocks):
    bps = min(LRU_BLOCKS_PER_STEP, n_blocks)
    wid = bps * HEAD
    seq0 = pl.BlockSpec((s, wid), lambda h: (0, h))
    seq1 = pl.BlockSpec((s, wid), lambda h: (0, n_blocks // bps + h))
    taps = pl.BlockSpec((CONV_TAPS, wid), lambda h: (0, h))
    vec = pl.BlockSpec((1, wid), lambda h: (0, h))
    mat = pl.BlockSpec((bps, HEAD, HEAD), lambda h: (h, 0, 0))
    return bps, seq0, seq1, taps, vec, mat


def _per_block(one_block, n_2d, n_mat_pos, bps):
    def body(*refs):
        for a in range(bps):
            views = [r.at[a] if i in n_mat_pos else r.at[:, pl.ds(a * HEAD, HEAD)] for i, r in enumerate(refs[:n_2d])]
            one_block(*views, *refs[n_2d:])
    return body


def _lru_fwd(xrg, wconv, bconv, wa, ba, wx, bx, lam):
    s = xrg.shape[0]
    nb = wa.shape[0]
    tc = _lru_chunk(s)
    bps, seq0, seq1, taps, vec, mat = _lru_specs(s, nb)
    pad = SUBLANES

    def one_block(xr_ref, xg_ref, wc_ref, bc_ref, wa_ref, ba_ref, wx_ref, bx_ref, lam_ref, o_ref, h_ref, pad_s, a_s, u_s):
        pad_s[0:pad, :] = jnp.zeros((pad, HEAD), F32)
        pad_s[pad:pad + s, :] = xr_ref[...]
        wab, wxb = wa_ref[...].astype(BF16), wx_ref[...].astype(BF16)
        sp = _softplus_neg(lam_ref[...])
        for c in range(s // tc):
            base = c * tc
            xc = bc_ref[...] + sum(wc_ref[i:i + 1, :] * pad_s[pl.ds(base + pad - (CONV_TAPS - 1) + i, tc), :]
                                   for i in range(CONV_TAPS))
            _, ig, a, mult = _lru_gates(xc, wab, ba_ref[...], wxb, bx_ref[...], sp)
            a_s[base:base + tc, :] = a
            u_s[base:base + tc, :] = mult * (ig * xc)

        row = lax.broadcasted_iota(jnp.int32, (SUBLANES, HEAD), 0)
        last = SUBLANES - 1

        def group(gi, hprev):
            offs = [pl.multiple_of((gi * SCAN_GROUP + q) * SUBLANES, SUBLANES) for q in range(SCAN_GROUP)]
            ab = []
            for off in offs:
                a8, b8 = a_s[pl.ds(off, SUBLANES), :], u_s[pl.ds(off, SUBLANES), :]
                for d in (1, 2, 4):
                    a_sh = jnp.where(row < d, 1.0, pltpu.roll(a8, d, 0))
                    b_sh = jnp.where(row < d, 0.0, pltpu.roll(b8, d, 0))
                    b8 = a8 * b_sh + b8
                    a8 = a8 * a_sh
                ab.append((a8, b8))
            enters = []
            for a8, b8 in ab:
                enters.append(hprev)
                hprev = a8[last:, :] * hprev + b8[last:, :]
            for off, (a8, b8), h0 in zip(offs, ab, enters):
                h_ref[pl.ds(off, SUBLANES), :] = a8 * h0 + b8
            return hprev

        lax.fori_loop(0, s // (SUBLANES * SCAN_GROUP), group, jnp.zeros((1, HEAD), F32))
        for c in range(s // tc):
            sl = slice(c * tc, (c + 1) * tc)
            gel, _ = _gelu_parts(xg_ref[sl, :])
            o_ref[sl, :] = h_ref[sl, :] * gel

    return pl.pallas_call(
        _per_block(one_block, 11, (4, 6), bps), grid=(nb // bps,),
        in_specs=[seq0, seq1, taps, vec, mat, vec, mat, vec, vec],
        out_specs=(seq0, seq0),
        out_shape=(SDS((s, nb * HEAD), F32), SDS((s, nb * HEAD), F32)),
        scratch_shapes=[pltpu.VMEM((s + pad, HEAD), F32), pltpu.VMEM((s, HEAD), F32), pltpu.VMEM((s, HEAD), F32)],
        compiler_params=_params(("parallel",)), name="lru_fwd")(xrg, xrg, wconv, bconv, wa, ba, wx, bx, lam)


def _lru_bwd(xrg, dol, hseq, wconv, bconv, wa, ba, wx, bx, lam):
    s = xrg.shape[0]
    nb = wa.shape[0]
    tc = _lru_chunk(s)
    bps, seq0, seq1, taps, vec, mat = _lru_specs(s, nb)
    pad = SUBLANES

    def one_block(xr_ref, xg_ref, do_ref, h_ref, wc_ref, bc_ref, wa_ref, ba_ref, wx_ref, bx_ref, lam_ref,
             dxr_ref, dxg_ref, dwc_ref, dbc_ref, dwa_ref, dba_ref, dwx_ref, dbx_ref, dlam_ref,
             pad_s, hp_s, a_s, g_s, da_s, dxc_s, xc_s, r_s, ig_s, mult_s):
        pad_s[0:pad, :] = jnp.zeros((pad, HEAD), F32)
        pad_s[pad:pad + s, :] = xr_ref[...]
        hp_s[0:pad, :] = jnp.zeros((pad, HEAD), F32)
        hp_s[pad:pad + s, :] = h_ref[...]
        a_s[s:s + pad, :] = jnp.zeros((pad, HEAD), F32)
        dxc_s[s:s + pad, :] = jnp.zeros((pad, HEAD), F32)
        wab, wxb = wa_ref[...].astype(BF16), wx_ref[...].astype(BF16)
        lam_v = lam_ref[...]
        sp = _softplus_neg(lam_v)

        def conv_in(c):
            base = c * tc
            wins = [pad_s[pl.ds(base + pad - (CONV_TAPS - 1) + i, tc), :] for i in range(CONV_TAPS)]
            xc = bc_ref[...] + sum(wc_ref[i:i + 1, :] * wins[i] for i in range(CONV_TAPS))
            return xc, wins

        for c in range(s // tc):
            sl = slice(c * tc, (c + 1) * tc)
            xc, _ = conv_in(c)
            r, ig, a, mult = _lru_gates(xc, wab, ba_ref[...], wxb, bx_ref[...], sp)
            a_s[sl, :] = a
            xc_s[sl, :], r_s[sl, :], ig_s[sl, :], mult_s[sl, :] = xc, r, ig, mult
            gel, dgel = _gelu_parts(xg_ref[sl, :])
            dov = do_ref[sl, :]
            g_s[sl, :] = dov * gel
            dxg_ref[sl, :] = (dov * h_ref[sl, :] * dgel).astype(BF16)

        row = lax.broadcasted_iota(jnp.int32, (SUBLANES, HEAD), 0)
        n_chunks = s // SUBLANES

        def group(it, gnext):
            offs = [pl.multiple_of((n_chunks - 1 - (it * SCAN_GROUP + q)) * SUBLANES, SUBLANES) for q in range(SCAN_GROUP)]
            cg = []
            for off in offs:
                a8 = a_s[pl.ds(off, SUBLANES), :]
                a8n = a_s[pl.ds(off + SUBLANES, SUBLANES), :]
                c8 = pltpu.roll(jnp.where(row == 0, a8n, a8), SUBLANES - 1, 0)
                g8 = g_s[pl.ds(off, SUBLANES), :]
                for d in (1, 2, 4):
                    c_sh = jnp.where(row >= SUBLANES - d, 1.0, pltpu.roll(c8, SUBLANES - d, 0))
                    g_sh = jnp.where(row >= SUBLANES - d, 0.0, pltpu.roll(g8, SUBLANES - d, 0))
                    g8 = c8 * g_sh + g8
                    c8 = c8 * c_sh
                cg.append((c8, g8))
            enters = []
            for c8, g8 in cg:
                enters.append(gnext)
                gnext = g8[0:1, :] + c8[0:1, :] * gnext
            for off, (c8, g8), g0 in zip(offs, cg, enters):
                gv = g8 + c8 * g0
                g_s[pl.ds(off, SUBLANES), :] = gv
                h8 = hp_s[pl.ds(off + pad, SUBLANES), :]
                h8p = hp_s[pl.ds(off, SUBLANES), :]
                da_s[pl.ds(off, SUBLANES), :] = gv * pltpu.roll(jnp.where(row == SUBLANES - 1, h8p, h8), 1, 0)
            return gnext

        lax.fori_loop(0, n_chunks // SCAN_GROUP, group, jnp.zeros((1, HEAD), F32))

        dsp = jnp.zeros((1, HEAD), F32)
        dbc = jnp.zeros((1, HEAD), F32)
        dba = jnp.zeros((1, HEAD), F32)
        dbx = jnp.zeros((1, HEAD), F32)
        dwa = jnp.zeros((HEAD, HEAD), F32)
        dwx = jnp.zeros((HEAD, HEAD), F32)
        dwc = [jnp.zeros((1, HEAD), F32) for _ in range(CONV_TAPS)]
        for c in range(s // tc):
            sl = slice(c * tc, (c + 1) * tc)
            wins = [pad_s[pl.ds(c * tc + pad - (CONV_TAPS - 1) + i, tc), :] for i in range(CONV_TAPS)]
            xc, r, ig, a, mult = xc_s[sl, :], r_s[sl, :], ig_s[sl, :], a_s[sl, :], mult_s[sl, :]
            du, da = g_s[sl, :], da_s[sl, :]
            d_ix = du * mult
            dla = da * a - (du * ig * xc) * (a * a / mult)
            dsp = dsp + jnp.sum(dla * r, axis=0, keepdims=True) * (-LRU_C)
            dpa = (dla * (-LRU_C * sp)) * r * (1.0 - r)
            dpx = (d_ix * xc) * ig * (1.0 - ig)
            dpab, dpxb, xb = dpa.astype(BF16), dpx.astype(BF16), xc.astype(BF16)
            dxc = d_ix * ig + _dot_nt(dpab, wab) + _dot_nt(dpxb, wxb)
            dwa = dwa + _dot_tn(xb, dpab)
            dwx = dwx + _dot_tn(xb, dpxb)
            dba = dba + jnp.sum(dpa, axis=0, keepdims=True)
            dbx = dbx + jnp.sum(dpx, axis=0, keepdims=True)
            dbc = dbc + jnp.sum(dxc, axis=0, keepdims=True)
            for i in range(CONV_TAPS):
                dwc[i] = dwc[i] + jnp.sum(dxc * wins[i], axis=0, keepdims=True)
            dxc_s[sl, :] = dxc

        for c in range(s // tc):
            base = c * tc
            dxr = sum(wc_ref[i:i + 1, :] * dxc_s[pl.ds(base + (CONV_TAPS - 1) - i, tc), :] for i in range(CONV_TAPS))
            dxr_ref[base:base + tc, :] = dxr.astype(BF16)

        for i in range(CONV_TAPS):
            dwc_ref[i:i + 1, :] = dwc[i]
        dbc_ref[...] = dbc
        dwa_ref[...] = dwa
        dwx_ref[...] = dwx
        dba_ref[...] = dba
        dbx_ref[...] = dbx
        dlam_ref[...] = dsp * (-_sigmoid(-lam_v))

    w = nb * HEAD
    return pl.pallas_call(
        _per_block(one_block, 20, (6, 8, 15, 17), bps), grid=(nb // bps,),
        in_specs=[seq0, seq1, seq0, seq0, taps, vec, mat, vec, mat, vec, vec],
        out_specs=(seq0, seq0, taps, vec, mat, vec, mat, vec, vec),
        out_shape=(SDS((s, w), BF16), SDS((s, w), BF16), SDS((CONV_TAPS, w), F32), SDS((1, w), F32),
                   SDS((nb, HEAD, HEAD), F32), SDS((1, w), F32), SDS((nb, HEAD, HEAD), F32), SDS((1, w), F32),
                   SDS((1, w), F32)),
        scratch_shapes=[pltpu.VMEM((s + pad, HEAD), F32), pltpu.VMEM((s + pad, HEAD), F32),
                        pltpu.VMEM((s + pad, HEAD), F32), pltpu.VMEM((s, HEAD), F32),
                        pltpu.VMEM((s, HEAD), F32), pltpu.VMEM((s + pad, HEAD), F32)]
                       + [pltpu.VMEM((s, HEAD), F32)] * 4,
        compiler_params=_params(("parallel",)), name="lru_bwd",
    )(xrg, xrg, dol, hseq, wconv, bconv, wa, ba, wx, bx, lam)


def _ada_mod(c_all, w_sh, b_sh):
    n_ex, d = c_all.shape
    n = w_sh.shape[1]
    tn = _tile(n, 512)

    def body(c_ref, w_ref, b_ref, mod_ref, act_ref):
        cv = c_ref[...]
        act = cv * _sigmoid(cv)
        act_ref[...] = act
        mod_ref[...] = _dot(act.astype(BF16), w_ref[...].astype(BF16)) + b_ref[...]

    return pl.pallas_call(
        body, grid=(n // tn,),
        in_specs=[pl.BlockSpec((n_ex, d), lambda j: (0, 0)), pl.BlockSpec((d, tn), lambda j: (0, j)),
                  pl.BlockSpec((1, tn), lambda j: (0, j))],
        out_specs=(pl.BlockSpec((n_ex, tn), lambda j: (0, j)), pl.BlockSpec((n_ex, d), lambda j: (0, 0))),
        out_shape=(SDS((n_ex, n), F32), SDS((n_ex, d), F32)),
        compiler_params=_params(("arbitrary",)), name="ada_mod")(c_all, w_sh, b_sh)


def _adamw_math(w, g, m, v):
    m = ADAM_B1 * m + (1.0 - ADAM_B1) * g
    v = ADAM_B2 * v + (1.0 - ADAM_B2) * (g * g)
    m_hat = m / (1.0 - ADAM_B1 ** ADAM_STEP)
    v_hat = v / (1.0 - ADAM_B2 ** ADAM_STEP)
    delta = -ADAM_LR * (m_hat / (jnp.sqrt(v_hat) + ADAM_EPS) + ADAM_WD * w)
    return delta, m, v


def _adamw_plain(name, w, g, m, v):
    def body(w_ref, g_ref, m_ref, v_ref, d_ref, mo_ref, vo_ref):
        d_ref[...], mo_ref[...], vo_ref[...] = _adamw_math(w_ref[...], g_ref[...], m_ref[...], v_ref[...])

    return pl.pallas_call(body, out_shape=(SDS(w.shape, F32),) * 3, name=name)(w, g, m, v)


def _adamw_halves(name, c_arr, w, m, v, g_own, g_recv):
    r, n = w.shape
    rh = r // 2
    tr = _tile(rh, 256)
    nh = rh // tr

    def body(c_ref, w_ref, m_ref, v_ref, go_ref, gr_ref, g_ref, d_ref, mo_ref, vo_ref):
        own = (pl.program_id(0) // nh) == c_ref[0]
        g = jnp.where(own, go_ref[...], gr_ref[...])
        g_ref[...] = g
        d_ref[...], mo_ref[...], vo_ref[...] = _adamw_math(w_ref[...], g, m_ref[...], v_ref[...])

    full = pl.BlockSpec((tr, n), lambda i, c: (i, 0))
    own = pl.BlockSpec((tr, n), lambda i, c: (jnp.where(i // nh == c[0], i % nh, 0), 0))
    recv = pl.BlockSpec((tr, n), lambda i, c: (jnp.where(i // nh == c[0], 0, i % nh), 0))
    return pl.pallas_call(
        body,
        grid_spec=pltpu.PrefetchScalarGridSpec(
            num_scalar_prefetch=1, grid=(2 * nh,), in_specs=[full, full, full, own, recv],
            out_specs=(full,) * 4),
        out_shape=(SDS((r, n), F32),) * 4,
        compiler_params=_params(("parallel",)), name=name)(c_arr, w, m, v, g_own, g_recv)


def _adamw_ada(w, m, v, act_t, dmod):
    d, n = w.shape
    n_ex = act_t.shape[1]
    tr = _tile(d, 256)

    def body(a_ref, dm_ref, w_ref, m_ref, v_ref, g_ref, d_ref, mo_ref, vo_ref):
        g = _dot(a_ref[...], dm_ref[...])
        g_ref[...] = g
        d_ref[...], mo_ref[...], vo_ref[...] = _adamw_math(w_ref[...], g, m_ref[...], v_ref[...])

    full = pl.BlockSpec((tr, n), lambda i: (i, 0))
    return pl.pallas_call(
        body, grid=(d // tr,),
        in_specs=[pl.BlockSpec((tr, n_ex), lambda i: (i, 0)), pl.BlockSpec((n_ex, n), lambda i: (0, 0)), full, full, full],
        out_specs=(full,) * 4, out_shape=(SDS((d, n), F32),) * 4,
        compiler_params=_params(("parallel",)), name="adamw_ada")(act_t, dmod, w, m, v)


def _small_reduce_adamw(parts, w, m, v):
    n_dev, r, _ = parts.shape
    tr = r if r <= PACK_ROWS else PACK_ROWS

    def body(p_ref, w_ref, m_ref, v_ref, g_ref, d_ref, mo_ref, vo_ref):
        g = p_ref[0]
        for k in range(1, n_dev):
            g = g + p_ref[k]
        g_ref[...] = g
        d_ref[...], mo_ref[...], vo_ref[...] = _adamw_math(w_ref[...], g, m_ref[...], v_ref[...])

    full = pl.BlockSpec((tr, LANES), lambda i: (i, 0))
    return pl.pallas_call(
        body, grid=(r // tr,),
        in_specs=[pl.BlockSpec((n_dev, tr, LANES), lambda i: (0, i, 0)), full, full, full],
        out_specs=(full,) * 4, out_shape=(SDS((r, LANES), F32),) * 4,
        compiler_params=_params(("parallel",)), name="small_reduce_adamw")(parts, w, m, v)


def _mesh_pos():
    return lax.axis_index("x"), lax.axis_index("y"), lax.axis_index("c")


def _other_chips(x, y):
    return [(1 - x, y), (x, 1 - y), (1 - x, 1 - y)]


def _all_gather_small(name, blk, after=()):
    r, n = blk.shape
    n_after = len(after)

    def body(x_ref, *rest):
        out_ref, send_sems, recv_sems, local_sem = rest[n_after:]
        x, y, c = _mesh_pos()
        me, sibling = (x, y, c), (x, y, 1 - c)
        chips = _other_chips(x, y)

        def rows(px, py, pc):
            return out_ref.at[4 * px + 2 * py + pc]

        def copy(k, block, to, src=None):
            return pltpu.make_async_remote_copy(
                src_ref=rows(*block) if src is None else src, dst_ref=rows(*block),
                send_sem=send_sems.at[k], recv_sem=recv_sems.at[k], device_id=to, device_id_type=MESH)

        mine = pltpu.make_async_copy(x_ref, rows(*me), local_sem)
        mine.start()
        first = [copy(0, me, sibling, src=x_ref)]
        first += [copy(1 + j, me, (*chip, c), src=x_ref) for j, chip in enumerate(chips)]
        for cp in first:
            cp.start()
        passed = [copy(4 + j, (*chip, c), sibling) for j, chip in enumerate(chips)]
        for j, chip in enumerate(chips):
            copy(1 + j, (*chip, c), me).wait_recv()
            passed[j].start()
        copy(0, sibling, me).wait_recv()
        for j, chip in enumerate(chips):
            copy(4 + j, (*chip, 1 - c), me).wait_recv()
        for cp in first + passed:
            cp.wait_send()
        mine.wait()

    return pl.pallas_call(
        body, out_shape=SDS((N_DEV, r, n), blk.dtype),
        in_specs=[pl.BlockSpec(memory_space=pltpu.VMEM)] + [pl.BlockSpec(memory_space=pl.ANY)] * n_after,
        out_specs=pl.BlockSpec(memory_space=pltpu.VMEM),
        scratch_shapes=[pltpu.SemaphoreType.DMA((7,)), pltpu.SemaphoreType.DMA((7,)), pltpu.SemaphoreType.DMA],
        compiler_params=pltpu.CompilerParams(vmem_limit_bytes=VMEM_LIMIT), name=name)(blk, *after)


_ANY = pl.BlockSpec(memory_space=pl.ANY)
_HBM = pl.BlockSpec(memory_space=pltpu.HBM)
_SEM = pl.BlockSpec(memory_space=pltpu.SEMAPHORE)
_EFFECT = pltpu.SideEffectType.DATAFLOW_SIDE_EFFECTING


def _hbm(a):
    return pltpu.with_memory_space_constraint(a, pltpu.HBM)


def _place_cast(name, j_arr, shard, kind, after):
    r, n = shard.shape
    tr = _tile(r, 256)
    nr = r // tr
    if kind == "col":
        out_shape, o_spec = (r, N_CHIP * n), pl.BlockSpec((tr, n), lambda i, j: (i, j[0]))
    else:
        out_shape, o_spec = (N_CHIP * r, n), pl.BlockSpec((tr, n), lambda i, j: (j[0] * nr + i, 0))

    def body(j_ref, s_ref, after_ref, o_ref, tok_ref):
        o_ref[...] = s_ref[...].astype(BF16)
        tok_ref[...] = jnp.zeros_like(tok_ref)

    return pl.pallas_call(
        body,
        grid_spec=pltpu.PrefetchScalarGridSpec(
            num_scalar_prefetch=1, grid=(nr,), in_specs=[pl.BlockSpec((tr, n), lambda i, j: (i, 0)), _ANY],
            out_specs=(o_spec, pl.BlockSpec((SUBLANES, LANES), lambda i, j: (0, 0)))),
        out_shape=(SDS(out_shape, BF16), SDS((SUBLANES, LANES), F32)),
        compiler_params=_params(("arbitrary",)), name=name)(j_arr, shard, after)


def _leg_direct(full, kind, x, y, c):
    mine = _full_region(full, kind, x, y, c)
    return [(mine, mine, (1 - x, y, c)), (mine, mine, (x, 1 - y, c))]


def _leg_relay(full, kind, x, y, c):
    fx, fy = jnp.where(c == 0, 1 - x, x), jnp.where(c == 0, y, 1 - y)
    tx, ty = jnp.where(c == 0, x, 1 - x), jnp.where(c == 0, 1 - y, y)
    got = _full_region(full, kind, fx, fy, c)
    return [(got, got, (tx, ty, c))]


def _leg_d2d(which):
    def leg(full, kind, x, y, c):
        chips = _other_chips(x, y)
        return [(_full_region(full, kind, *chips[k], c), _full_region(full, kind, *chips[k], c), (x, y, 1 - c))
                for k in which]
    return leg


_LEGS = {"direct": (_leg_direct, 2), "relay": (_leg_relay, 1), "d2d_near": (_leg_d2d((0, 1)), 2),
         "d2d_far": (_leg_d2d((2,)), 1)}


def _gather_call(name, fulls, kinds, waits, starts, after, thru):
    nw, n_wait, n_start = len(fulls), len(waits), len(starts)

    def body(*refs):
        wait_sems = refs[nw:nw + 2 * n_wait]
        outs = refs[nw + 2 * n_wait + 2:]
        full, start_sems = outs[:nw], outs[nw:nw + 2 * n_start]
        x, y, c = _mesh_pos()
        for i, (leg, ws, _, _) in enumerate(waits):
            fn, per = _LEGS[leg]
            for li, w in enumerate(ws):
                for k, (s_, d_, dev) in enumerate(fn(full[w], kinds[w], x, y, c)):
                    cp = pltpu.make_async_remote_copy(
                        src_ref=s_, dst_ref=d_, send_sem=wait_sems[2 * i].at[per * li + k],
                        recv_sem=wait_sems[2 * i + 1].at[per * li + k], device_id=dev, device_id_type=MESH)
                    cp.wait_recv()
                    cp.wait_send()
        for i, (leg, ws) in enumerate(starts):
            fn, per = _LEGS[leg]
            for li, w in enumerate(ws):
                for k, (s_, d_, dev) in enumerate(fn(full[w], kinds[w], x, y, c)):
                    pltpu.make_async_remote_copy(
                        src_ref=s_, dst_ref=d_, send_sem=start_sems[2 * i].at[per * li + k],
                        recv_sem=start_sems[2 * i + 1].at[per * li + k], device_id=dev, device_id_type=MESH).start()

    sems = []
    for leg, ws in starts:
        sems += [pltpu.SemaphoreType.DMA((_LEGS[leg][1] * len(ws),))] * 2
    wait_args = []
    for _, _, s_, r_ in waits:
        wait_args += [s_, r_]
    outs = pl.pallas_call(
        body,
        out_shape=tuple(pltpu.HBM(f_.shape, f_.dtype) for f_ in fulls) + tuple(sems) + (SDS(thru.shape, thru.dtype),),
        in_specs=[_HBM] * nw + [_SEM] * (2 * n_wait) + [_ANY, _ANY],
        out_specs=tuple([_HBM] * nw + [_SEM] * (2 * n_start) + [_ANY]),
        input_output_aliases={**{w: w for w in range(nw)}, nw + 2 * n_wait + 1: nw + 2 * n_start},
        compiler_params=pltpu.CompilerParams(has_side_effects=_EFFECT),
        name=name,
    )(*[_hbm(f_) for f_ in fulls], *wait_args, after, thru)
    pairs = [(outs[nw + 2 * i], outs[nw + 2 * i + 1]) for i in range(n_start)]
    return list(outs[:nw]), pairs, outs[nw + 2 * n_start]


def _full_region(full, kind, px, py, half):
    j = 2 * px + py
    if kind == "col":
        rh, cols = full.shape[0] // 2, full.shape[1] // N_CHIP
        return full.at[pl.ds(half * rh, rh), pl.ds(j * cols, cols)]
    rows = full.shape[0] // N_CHIP
    rh = rows // 2
    return full.at[pl.ds(j * rows + half * rh, rh), :]


def _plan_scatter(kind):
    def plan(src, land, x, y, c):
        out = []
        for k, (px, py) in enumerate(_other_chips(x, y)):
            j = 2 * px + py
            if kind == "col":
                n = src.shape[1] // N_CHIP
                blk = src.at[:, pl.ds(j * n, n)]
            else:
                blk = src.at[j]
            out.append((blk, land.at[k], (px, py, c)))
        return out
    return plan


def _plan_whole(src, land, x, y, c):
    return [(src, land, (x, y, 1 - c))]


def _split_start(name, src, land_shape, n, plan, thru):
    def body(src_in, land_in, thru_in, send, recv, src_ref, land_ref, thru_out):
        x, y, c = _mesh_pos()
        for k, (s_, d_, dev) in enumerate(plan(src_ref, land_ref, x, y, c)):
            pltpu.make_async_remote_copy(src_ref=s_, dst_ref=d_, send_sem=send.at[k], recv_sem=recv.at[k],
                                         device_id=dev, device_id_type=MESH).start()

    sem = pltpu.SemaphoreType.DMA((n,))
    return pl.pallas_call(
        body,
        out_shape=(sem, sem, pltpu.HBM(src.shape, src.dtype), pltpu.HBM(land_shape, src.dtype), SDS(thru.shape, thru.dtype)),
        in_specs=[_HBM, _HBM, _ANY], out_specs=(_SEM, _SEM, _HBM, _HBM, _ANY),
        input_output_aliases={0: 2, 1: 3, 2: 4},
        compiler_params=pltpu.CompilerParams(has_side_effects=_EFFECT), name=name,
    )(_hbm(src), _hbm(lax.empty(land_shape, src.dtype)), thru)


def _split_wait(name, send, recv, src, land, plan, after):
    def body(src_in, land_in, send_r, recv_r, after_r, src_ref, land_ref):
        x, y, c = _mesh_pos()
        for k, (s_, d_, dev) in enumerate(plan(src_ref, land_ref, x, y, c)):
            cp = pltpu.make_async_remote_copy(src_ref=s_, dst_ref=d_, send_sem=send_r.at[k], recv_sem=recv_r.at[k],
                                              device_id=dev, device_id_type=MESH)
            cp.wait_send()
            cp.wait_recv()

    return pl.pallas_call(
        body,
        out_shape=(pltpu.HBM(src.shape, src.dtype), pltpu.HBM(land.shape, land.dtype)),
        in_specs=[_HBM, _HBM, _SEM, _SEM, _ANY], out_specs=(_HBM, _HBM),
        input_output_aliases={0: 0, 1: 1},
        compiler_params=pltpu.CompilerParams(has_side_effects=_EFFECT), name=name,
    )(src, land, send, recv, after)


def _dev_row(buf, px, py, pc):
    return buf.at[4 * px + 2 * py + pc]


def _plan_gather_own(buf, land, x, y, c):
    own = _dev_row(buf, x, y, c)
    return [(own, own, (x, y, 1 - c))] + [(own, own, (px, py, c)) for px, py in _other_chips(x, y)]


def _plan_gather_pass(buf, land, x, y, c):
    return [(_dev_row(buf, px, py, c), _dev_row(buf, px, py, c), (x, y, 1 - c)) for px, py in _other_chips(x, y)]


def _split_start_inplace(name, buf, n, plan, thru):
    def body(buf_in, thru_in, send, recv, buf_ref, thru_out):
        x, y, c = _mesh_pos()
        for k, (s_, d_, dev) in enumerate(plan(buf_ref, buf_ref, x, y, c)):
            pltpu.make_async_remote_copy(src_ref=s_, dst_ref=d_, send_sem=send.at[k], recv_sem=recv.at[k],
                                         device_id=dev, device_id_type=MESH).start()

    sem = pltpu.SemaphoreType.DMA((n,))
    return pl.pallas_call(
        body, out_shape=(sem, sem, pltpu.HBM(buf.shape, buf.dtype), SDS(thru.shape, thru.dtype)),
        in_specs=[_HBM, _ANY], out_specs=(_SEM, _SEM, _HBM, _ANY), input_output_aliases={0: 2, 1: 3},
        compiler_params=pltpu.CompilerParams(has_side_effects=_EFFECT), name=name)(_hbm(buf), thru)


def _split_wait_inplace(name, send, recv, buf, plan, after):
    def body(buf_in, send_r, recv_r, after_r, buf_ref):
        x, y, c = _mesh_pos()
        for k, (s_, d_, dev) in enumerate(plan(buf_ref, buf_ref, x, y, c)):
            cp = pltpu.make_async_remote_copy(src_ref=s_, dst_ref=d_, send_sem=send_r.at[k], recv_sem=recv_r.at[k],
                                              device_id=dev, device_id_type=MESH)
            cp.wait_send()
            cp.wait_recv()

    return pl.pallas_call(
        body, out_shape=pltpu.HBM(buf.shape, buf.dtype), in_specs=[_HBM, _SEM, _SEM, _ANY], out_specs=_HBM,
        input_output_aliases={0: 0}, compiler_params=pltpu.CompilerParams(has_side_effects=_EFFECT), name=name,
    )(buf, send, recv, after)


def _place_row(name, me_arr, slab):
    r, n = slab.shape
    tr = r if r <= PACK_ROWS else PACK_ROWS

    def body(me_ref, s_ref, o_ref):
        o_ref[...] = s_ref[...]

    return pl.pallas_call(
        body,
        grid_spec=pltpu.PrefetchScalarGridSpec(
            num_scalar_prefetch=1, grid=(r // tr,), in_specs=[pl.BlockSpec((tr, n), lambda i, me: (i, 0))],
            out_specs=pl.BlockSpec((None, tr, n), lambda i, me: (me[0], i, 0))),
        out_shape=SDS((N_DEV, r, n), slab.dtype), compiler_params=_params(("parallel",)), name=name)(me_arr, slab)


def _sum_partials(name, j_arr, part, got, kind):
    _, rh, n = got.shape
    tr = _tile(rh, 256)
    if kind == "col":
        p_spec = pl.BlockSpec((tr, n), lambda i, j: (i, j[0]))
    else:
        p_spec = pl.BlockSpec((None, tr, n), lambda i, j: (j[0], i, 0))

    def body(j_ref, p_ref, r_ref, o_ref):
        o_ref[...] = ((p_ref[...].astype(F32) + r_ref[0].astype(F32)) + r_ref[1].astype(F32)) + r_ref[2].astype(F32)

    return pl.pallas_call(
        body,
        grid_spec=pltpu.PrefetchScalarGridSpec(
            num_scalar_prefetch=1, grid=(rh // tr,),
            in_specs=[p_spec, pl.BlockSpec((3, tr, n), lambda i, j: (0, i, 0))],
            out_specs=pl.BlockSpec((tr, n), lambda i, j: (i, 0))),
        out_shape=SDS((rh, n), F32),
        compiler_params=_params(("parallel",)), name=name)(j_arr, part, got)


def _pack(arrays):
    flat = [a.reshape(-1).astype(F32) for a in arrays]
    flat = [jnp.pad(f, (0, (-f.shape[0]) % LANES)) for f in flat]
    sizes = [f.shape[0] for f in flat]
    total = sum(sizes)
    rows = total // LANES
    tail = LANES * ((-rows) % (PACK_ROWS if rows > PACK_ROWS else SUBLANES))
    if tail:
        flat.append(jnp.zeros((tail,), F32))
    return jnp.concatenate(flat).reshape(-1, LANES), sizes


def _unpack(slab, sizes, shapes, lead=()):
    flat = slab.reshape(lead + (-1,))
    out, off = [], 0
    for sz, shp in zip(sizes, shapes):
        n = math.prod(shp)
        out.append(flat[..., off:off + n].reshape(lead + tuple(shp)))
        off += sz
    return out


def kernel(x, c, w_ada, b_ada, g_norm_mix, w_in, w_conv, b_conv, w_rg_a, b_rg_a, w_rg_x, b_rg_x, lru_lambda, g_attn_out, g_lru_out, w_out, g_norm_mlp, w_mlp_in, w_mlp_out, g_norm_final, loss_target, m_w_ada, m_b_ada, m_g_norm_mix, m_w_in, m_w_conv, m_b_conv, m_w_rg_a, m_b_rg_a, m_w_rg_x, m_b_rg_x, m_lru_lambda, m_g_attn_out, m_g_lru_out, m_w_out, m_g_norm_mlp, m_w_mlp_in, m_w_mlp_out, m_g_norm_final, v_w_ada, v_b_ada, v_g_norm_mix, v_w_in, v_w_conv, v_b_conv, v_w_rg_a, v_b_rg_a, v_w_rg_x, v_b_rg_x, v_lru_lambda, v_g_attn_out, v_g_lru_out, v_w_out, v_g_norm_mlp, v_w_mlp_in, v_w_mlp_out, v_g_norm_final):
    s, d = x.shape[1], x.shape[2]
    aw = d // 2
    nh = aw // HEAD
    f = w_mlp_out.shape[1] * N_CHIP
    n_ada = w_ada.shape[2]
    n_cv = w_conv.shape[2]
    ix, iy, ic = lax.axis_index("x"), lax.axis_index("y"), lax.axis_index("c")
    chip = 2 * ix + iy
    me = 2 * chip + ic
    c_arr = jnp.reshape(ic, (1,)).astype(jnp.int32)
    j_arr = jnp.reshape(chip, (1,)).astype(jnp.int32)

    x2d, tgt = x[0], loss_target[0]

    k_in, k_out, k_mi, k_mo = kinds = ("col", "row", "col", "row")
    slab, sizes = _pack([c, w_conv])
    p_in, _ = _place_cast("place_cast_0", j_arr, w_in[0], k_in, c)
    (f_in,), (dir_in,), slab = _gather_call("gather_0", [p_in], [k_in], [], [("direct", [0])], c, slab)
    p_out, tok = _place_cast("place_cast_1", j_arr, w_out[0], k_out, slab)
    p_mi, tok = _place_cast("place_cast_2", j_arr, w_mlp_in[0], k_mi, tok)
    p_mo, tok = _place_cast("place_cast_3", j_arr, w_mlp_out[0], k_mo, tok)

    gathered = _all_gather_small("comm_gather_cond", slab, after=(tok,))
    c_parts, cv_parts = _unpack(gathered, sizes, [(d,), (CONV_TAPS, n_cv)], lead=(N_DEV,))
    c_all = c_parts
    w_conv_full = jnp.concatenate([cv_parts[2 * j] for j in range(N_CHIP)], axis=-1)
    b_sh = lax.dynamic_slice(b_ada, (0, chip * n_ada), (1, n_ada))
    (f_in,), (rel_in, near_in), c_all = _gather_call(
        "gather_1", [f_in], [k_in], [("direct", [0], *dir_in)], [("relay", [0]), ("d2d_near", [0])], gathered, c_all)
    mod_part, act_all = _ada_mod(c_all, w_ada[0], b_sh)
    mod_g = _all_gather_small("comm_gather_mod", mod_part.reshape(-1, LANES))
    mod_g = mod_g.reshape(N_DEV, N_DEV, n_ada)
    mod = jnp.concatenate([lax.dynamic_index_in_dim(mod_g[2 * j], me, 0, keepdims=True) for j in range(N_CHIP)], axis=-1)
    sh1, sc1, gt1, sh2, sc2, gt2 = [mod[:, k * d:(k + 1) * d] for k in range(N_MOD)]

    (f_in, f_out, f_mi, f_mo), (far_in, dir_om, dir_mo), sh1 = _gather_call(
        "gather_2", [f_in, p_out, p_mi, p_mo], kinds, [("relay", [0], *rel_in)],
        [("d2d_far", [0]), ("direct", [1, 2]), ("direct", [3])], mod, sh1)

    h1, rstd1 = _norm_mod_fwd("norm_mod_fwd1", x2d, g_norm_mix, sc1, sh1)
    (w_in_f,), _, h1 = _gather_call("gather_3", [f_in], [k_in],
                                    [("d2d_near", [0], *near_in), ("d2d_far", [0], *far_in)], [], rstd1, h1)
    (qkv,) = _matmul("mm_qkv", h1, w_in_f, "nn", s, 3 * aw, d, (BF16,))
    (xrg,) = _matmul("mm_xrg", h1, w_in_f, "nn", s, 2 * aw, d, (F32,), b_off=3 * aw)
    o_attn, attn_w, attn_sg = _attn_fwd(qkv, nh)
    (f_out, f_mi), (rel_om, near_om), xrg = _gather_call(
        "gather_4", [f_out, f_mi], [k_out, k_mi], [("direct", [0, 1], *dir_om)],
        [("relay", [0, 1]), ("d2d_near", [0, 1])], o_attn, xrg)
    wa3, wx3 = w_rg_a[0], w_rg_x[0]
    o_lru, hseq = _lru_fwd(xrg, w_conv_full, b_conv, wa3, b_rg_a, wx3, b_rg_x, lru_lambda)
    mixed, rstd_a, rstd_l = _mix_norm_fwd(o_attn, o_lru, g_attn_out, g_lru_out)
    (f_out, f_mi, f_mo), (far_om, rel_mo, near_mo), mixed = _gather_call(
        "gather_5", [f_out, f_mi, f_mo], [k_out, k_mi, k_mo], [("relay", [0, 1], *rel_om), ("direct", [2], *dir_mo)],
        [("d2d_far", [0, 1]), ("relay", [2]), ("d2d_near", [2])], rstd_a, mixed)
    (w_out_f, w_mi_f), _, mixed = _gather_call(
        "gather_6", [f_out, f_mi], [k_out, k_mi], [("d2d_near", [0, 1], *near_om), ("d2d_far", [0, 1], *far_om)], [],
        rstd_l, mixed)

    def residual(acc, xin, gt):
        return acc, xin + gt * acc

    y1, x1 = _matmul("mm_out", mixed, w_out_f, "nn", s, d, d, (BF16, F32), extras=(x2d, gt1),
                     extra_kinds=("tile", "row"), epilogue=residual)
    h2, rstd2 = _norm_mod_fwd("norm_mod_fwd2", x1, g_norm_mlp, sc2, sh2)
    (f_mo,), (far_mo,), h2 = _gather_call("gather_7", [f_mo], [k_mo], [("relay", [0], *rel_mo)],
                                          [("d2d_far", [0])], rstd2, h2)

    def sq_relu(acc):
        r = jnp.maximum(acc, 0.0)
        return 2.0 * r, r * r

    r2, hid = _matmul("mm_mlp_in", h2, w_mi_f, "nn", s, f, d, (BF16, BF16), epilogue=sq_relu)
    (w_mo_f,), _, hid = _gather_call("gather_8", [f_mo], [k_mo],
                                     [("d2d_near", [0], *near_mo), ("d2d_far", [0], *far_mo)], [], r2, hid)
    y2, x2 = _matmul("mm_mlp_out", hid, w_mo_f, "nn", s, d, f, (BF16, F32), extras=(x1, gt2),
                     extra_kinds=("tile", "row"), epilogue=residual)
    dx2, loss_row, dg_final, dy2, dgt2 = _final_loss(x2, g_norm_final.reshape(1, d), tgt, y2, gt2)

    oc_arr = 1 - c_arr

    def dw_half(name, st, h_arr, got=None):
        add = {} if got is None else dict(extras=(got,), extra_kinds=("tile",),
                                          epilogue=lambda acc, g_: (acc + g_.astype(F32),))
        tn = st["n"] if st["tm"] * 4 <= MM_TILE_M else MM_TILE_N
        (out,) = _matmul(name, st["a"], st["dy"], "tn", st["m"], st["n"], s, (BF16,), tm=st["tm"], tn=tn,
                         m_half=h_arr, **add)
        return out

    def rs_begin(tag, kind, xa, dy, m, n, thru):
        st = {"tag": tag, "kind": kind, "a": xa, "dy": dy, "m": m, "n": n,
              "tm": m // (2 * N_CHIP) if kind == "row" else m // 2}
        first = dw_half("mm_dw_%s_a" % tag, st, oc_arr)
        send, recv, first, land, thru = _split_start("rs_swap_start_" + tag, first, first.shape, 1, _plan_whole, thru)
        st["swap"] = (send, recv, first, land)
        return st, thru

    def rs_mid(st, after, thru):
        tag, kind = st["tag"], st["kind"]
        _, got = _split_wait("rs_swap_wait_" + tag, *st["swap"], _plan_whole, after)
        part = dw_half("mm_dw_%s_b" % tag, st, c_arr, got)
        if kind == "row":
            part = part.reshape(N_CHIP, st["m"] // (2 * N_CHIP), st["n"])
        blk = (part.shape[0], part.shape[1] // N_CHIP) if kind == "col" else part.shape[1:]
        send, recv, part, land, thru = _split_start("rs_scatter_start_" + tag, part, (N_CHIP - 1,) + blk, N_CHIP - 1,
                                                    _plan_scatter(kind), thru)
        st["scatter"] = (send, recv, part, land)
        return thru

    def rs_end(st, after):
        tag, kind = st["tag"], st["kind"]
        part, got = _split_wait("rs_scatter_wait_" + tag, *st["scatter"], _plan_scatter(kind), after)
        return _sum_partials("sum_partials_" + tag, j_arr, part, got, kind)

    (dpre,) = _matmul("mm_dhid", dy2, w_mo_f, "nt", s, f, d, (BF16,), extras=(r2,), extra_kinds=("tile",),
                      epilogue=lambda acc, r: (acc * r.astype(F32),))
    st_mo, dpre = rs_begin("mo", "row", hid, dy2, f, d, dpre)
    (dh2,) = _matmul("mm_dh2", dpre, w_mi_f, "nt", s, d, f, (BF16,))
    dh2 = rs_mid(st_mo, dh2, dh2)
    st_mi, dh2 = rs_begin("mi", "col", h2, dpre, d, f, dh2)
    dx1, dsh2, dsc2, dg_mlp, dy1, dgt1 = _norm_mod_bwd("norm_mod_bwd2", dh2, x1, rstd2, g_norm_mlp, sc2, dx2,
                                                       gate=(y1, gt1))
    (dmixed,) = _matmul("mm_dmixed", dy1, w_out_f, "nt", s, d, d, (BF16,))
    dmixed = rs_mid(st_mi, dmixed, dmixed)
    st_out, dmixed = rs_begin("out", "row", mixed, dy1, d, d, dmixed)
    do_attn, do_lru, dg_attn, dg_lru = _mix_norm_bwd(dmixed, o_attn, o_lru, rstd_a, rstd_l, g_attn_out, g_lru_out)
    dq, dk, dv = _attn_bwd(qkv, do_attn, attn_w, attn_sg, nh)
    do_lru = rs_mid(st_out, dq, do_lru)
    dxr, dxg, dwconv, dbconv, dwa, dba, dwx, dbx, dlam = _lru_bwd(
        xrg, do_lru, hseq, w_conv_full, b_conv, wa3, b_rg_a, wx3, b_rg_x, lru_lambda)
    dproj = jnp.concatenate([dq, dk, dv, dxr, dxg], axis=-1)
    st_in, dproj = rs_begin("in", "col", h1, dproj, d, 5 * aw, dproj)
    st_in["dy"] = dproj
    (dh1,) = _matmul("mm_dh1", dproj, w_in_f, "nt", s, d, 5 * aw, (BF16,))
    dh1 = rs_mid(st_in, dh1, dh1)
    grad_x, dsh1, dsc1, dg_mix = _norm_mod_bwd("norm_mod_bwd1", dh1, x2d, rstd1, g_norm_mix, sc1, dx1)

    dmod = jnp.concatenate([dsh1, dsc1, dgt1, dsh2, dsc2, dgt2], axis=-1)
    small_names = ["b_ada", "g_norm_mix", "b_conv", "w_rg_a", "b_rg_a", "w_rg_x", "b_rg_x", "lru_lambda",
                   "g_attn_out", "g_lru_out", "g_norm_mlp", "g_norm_final"]
    small_g = [dmod, dg_mix, dbconv, dwa, dba, dwx, dbx, dlam, dg_attn, dg_lru, dg_mlp, dg_final]
    small_w = [b_ada, g_norm_mix, b_conv, w_rg_a, b_rg_a, w_rg_x, b_rg_x, lru_lambda, g_attn_out, g_lru_out, g_norm_mlp, g_norm_final]
    small_m = [m_b_ada, m_g_norm_mix, m_b_conv, m_w_rg_a, m_b_rg_a, m_w_rg_x, m_b_rg_x, m_lru_lambda, m_g_attn_out, m_g_lru_out, m_g_norm_mlp, m_g_norm_final]
    small_v = [v_b_ada, v_g_norm_mix, v_b_conv, v_w_rg_a, v_b_rg_a, v_w_rg_x, v_b_rg_x, v_lru_lambda, v_g_attn_out, v_g_lru_out, v_g_norm_mlp, v_g_norm_final]
    extra_zero = [jnp.zeros_like(dwconv), jnp.zeros((LANES,), F32)]
    g_slab, g_sizes = _pack(small_g + [dwconv, loss_row])
    w_slab, _ = _pack(small_w + extra_zero)
    m_slab, _ = _pack(small_m + extra_zero)
    v_slab, _ = _pack(small_v + extra_zero)
    me_arr = jnp.reshape(me, (1,)).astype(jnp.int32)
    g_buf = _place_row("place_small_grads", me_arr, g_slab)
    sg_send, sg_recv, g_buf, tok = _split_start_inplace("sg_gather_start", g_buf, N_CHIP, _plan_gather_own, loss_row)

    def reduced_begin(tag, half, tok_):
        send, recv, half, land, tok_ = _split_start("rs_reduced_start_" + tag, half, half.shape, 1, _plan_whole, tok_)
        return (send, recv, half, land), tok_

    def reduced_end(tag, st, after):
        return _split_wait("rs_reduced_wait_" + tag, *st, _plan_whole, after)

    sw_mo, tok = reduced_begin("mo", rs_end(st_mo, tok), tok)
    sw_mi, tok = reduced_begin("mi", rs_end(st_mi, tok), tok)
    sw_out, tok = reduced_begin("out", rs_end(st_out, tok), tok)
    half_mo, got_mo = reduced_end("mo", sw_mo, tok)
    big = {"w_mlp_out": _adamw_halves("adamw_w_mlp_out", c_arr, w_mlp_out[0], m_w_mlp_out[0], v_w_mlp_out[0],
                                      half_mo, got_mo)}
    half_mi, got_mi = reduced_end("mi", sw_mi, big["w_mlp_out"][1])
    big["w_mlp_in"] = _adamw_halves("adamw_w_mlp_in", c_arr, w_mlp_in[0], m_w_mlp_in[0], v_w_mlp_in[0], half_mi, got_mi)
    half_out, got_out = reduced_end("out", sw_out, big["w_mlp_in"][1])
    big["w_out"] = _adamw_halves("adamw_w_out", c_arr, w_out[0], m_w_out[0], v_w_out[0], half_out, got_out)
    g_buf = _split_wait_inplace("sg_gather_wait", sg_send, sg_recv, g_buf, _plan_gather_own, big["w_out"][1])
    sg_send, sg_recv, g_buf, tok = _split_start_inplace("sg_pass_start", g_buf, N_CHIP - 1, _plan_gather_pass, tok)
    sw_in, tok = reduced_begin("in", rs_end(st_in, tok), tok)
    g_all = _split_wait_inplace("sg_pass_wait", sg_send, sg_recv, g_buf, _plan_gather_pass, tok)
    gs_slab, ds_slab, ms_slab, vs_slab = _small_reduce_adamw(g_all, w_slab, m_slab, v_slab)
    shapes = [w.shape for w in small_w] + [dwconv.shape, (LANES,)]
    gs = _unpack(gs_slab, g_sizes, shapes)
    ds = _unpack(ds_slab, g_sizes, shapes)
    ms = _unpack(ms_slab, g_sizes, shapes)
    vs = _unpack(vs_slab, g_sizes, shapes)
    small = {n: (gs[i], ds[i], ms[i], vs[i]) for i, n in enumerate(small_names)}
    loss = gs[-1][0]
    g_wconv = lax.dynamic_slice(gs[-2], (0, chip * n_cv), (CONV_TAPS, n_cv))
    d_wconv, m_wconv, v_wconv = _adamw_plain("adamw_conv", w_conv[0], g_wconv, m_w_conv[0], v_w_conv[0])
    small["w_conv"] = (g_wconv[None], d_wconv[None], m_wconv[None], v_wconv[None])

    dmod_all = g_all[:, :N_MOD * d // LANES, :].reshape(N_DEV, N_MOD * d)
    dmod_sel = lax.dynamic_slice(dmod_all, (0, chip * n_ada), (N_DEV, n_ada)).astype(BF16)
    act_t = act_all.T.astype(BF16)
    big["w_ada"] = _adamw_ada(w_ada[0], m_w_ada[0], v_w_ada[0], act_t, dmod_sel)

    half_in, got_in = reduced_end("in", sw_in, big["w_ada"][1])
    big["w_in"] = _adamw_halves("adamw_w_in", c_arr, w_in[0], m_w_in[0], v_w_in[0], half_in, got_in)

    order = ["w_ada", "b_ada", "g_norm_mix", "w_in", "w_conv", "b_conv", "w_rg_a", "b_rg_a", "w_rg_x", "b_rg_x",
             "lru_lambda", "g_attn_out", "g_lru_out", "w_out", "g_norm_mlp", "w_mlp_in", "w_mlp_out", "g_norm_final"]
    res = {}
    for n in order:
        res[n] = tuple(t[None] for t in big[n]) if n in big else small[n]
    return (loss, grad_x[None],
            *[res[n][0] for n in order], *[res[n][1] for n in order],
            *[res[n][2] for n in order], *[res[n][3] for n in order])
```

```python
import functools
import math

import jax
import jax.numpy as jnp
from jax import lax
from jax.experimental import pallas as pl
from jax.experimental.pallas import tpu as pltpu

F32 = jnp.float32
BF16 = jnp.bfloat16
SDS = jax.ShapeDtypeStruct
MESH = pl.DeviceIdType.MESH

EPS = 1e-6
HEAD = 128
N_MOD = 6
CONV_TAPS = 4
LRU_C = 8.0
ADAM_LR, ADAM_B1, ADAM_B2, ADAM_EPS, ADAM_WD, ADAM_STEP = 0.001, 0.9, 0.999, 1e-08, 0.01, 10
N_DEV = 8
N_CHIP = 4
LANES = 128
SUBLANES = 8
VMEM_LIMIT = 56 * 1024 * 1024
PACK_ROWS = 256
MM_TILE_M, MM_TILE_N, MM_TILE_K = 1024, 1024, 2048
ROW_TILE = 256
ROW_SPLIT = 1


def _tile(dim, pref):
    t = min(dim, pref)
    while dim % t:
        t -= LANES
    return t


def _params(sem=None):
    return pltpu.CompilerParams(dimension_semantics=sem, vmem_limit_bytes=VMEM_LIMIT)


def _sigmoid(x):
    return 1.0 / (1.0 + jnp.exp(-x))


def _log_sigmoid(x):
    return jnp.minimum(x, 0.0) - jnp.log(1.0 + jnp.exp(-jnp.abs(x)))


def _gelu_parts(x):
    k0, k1 = math.sqrt(2.0 / math.pi), 0.044715
    t = jnp.tanh(k0 * (x + k1 * x * x * x))
    val = 0.5 * x * (1.0 + t)
    der = 0.5 * (1.0 + t) + 0.5 * x * (1.0 - t * t) * k0 * (1.0 + 3.0 * k1 * x * x)
    return val, der


def _dot(a, b):
    return jnp.dot(a, b, preferred_element_type=F32)


def _dot_nt(a, b):
    return lax.dot_general(a, b, (((1,), (1,)), ((), ())), preferred_element_type=F32)


def _dot_tn(a, b):
    return lax.dot_general(a, b, (((0,), (0,)), ((), ())), preferred_element_type=F32)


def _split_dot(x, tri):
    hi = x.astype(BF16)
    lo = (x - hi.astype(F32)).astype(BF16)
    return _dot(hi, tri) + _dot(lo, tri)


def _matmul(name, a, b, mode, m, n, k, out_dtypes, *, b_off=0, extras=(), extra_kinds=(), epilogue=None,
            tm=MM_TILE_M, tn=MM_TILE_N, tk=MM_TILE_K, m_half=None):
    tm, tn, tk = _tile(m, tm), _tile(math.gcd(n, b_off) if b_off else n, tn), _tile(k, tk)
    assert b_off % tn == 0
    nk = k // tk
    n_ex, n_out = len(extras), len(out_dtypes)
    dot = {"nn": _dot, "nt": _dot_nt, "tn": _dot_tn}[mode]
    n_pre = 0 if m_half is None else 1
    m_out = m if m_half is None else m // 2

    def body(*refs):
        a_ref, b_ref, *rest = refs[n_pre:]
        ex, outs = rest[:n_ex], rest[n_ex:n_ex + n_out]

        def finish(total):
            res = epilogue(total, *[e[...] for e in ex]) if epilogue else (total,)
            for o, r in zip(outs, res):
                o[...] = r.astype(o.dtype)

        if nk == 1:
            finish(dot(a_ref[...], b_ref[...]))
            return
        acc = rest[-1]
        kk = pl.program_id(2)

        @pl.when(kk == 0)
        def _():
            acc[...] = dot(a_ref[...], b_ref[...])

        @pl.when(jnp.logical_and(kk > 0, kk < nk - 1))
        def _():
            acc[...] += dot(a_ref[...], b_ref[...])

        @pl.when(kk == nk - 1)
        def _():
            finish(acc[...] + dot(a_ref[...], b_ref[...]))

    if mode == "nn":
        a_spec = pl.BlockSpec((tm, tk), lambda i, j, kk, *_: (i, kk))
        b_spec = pl.BlockSpec((tk, tn), lambda i, j, kk, *_: (kk, j + b_off // tn))
    elif mode == "nt":
        a_spec = pl.BlockSpec((tm, tk), lambda i, j, kk, *_: (i, kk))
        b_spec = pl.BlockSpec((tn, tk), lambda i, j, kk, *_: (j, kk + b_off // tk))
    elif m_half is None:
        a_spec = pl.BlockSpec((tk, tm), lambda i, j, kk: (kk, i))
        b_spec = pl.BlockSpec((tk, tn), lambda i, j, kk: (kk, j))
    else:
        a_spec = pl.BlockSpec((tk, tm), lambda i, j, kk, h: (kk, 2 * i + h[0]))
        b_spec = pl.BlockSpec((tk, tn), lambda i, j, kk, h: (kk, j))
    tile_spec = pl.BlockSpec((tm, tn), lambda i, j, kk, *_: (i, j))
    row_spec = pl.BlockSpec((1, tn), lambda i, j, kk, *_: (0, j))
    in_specs = [a_spec, b_spec] + [tile_spec if kind == "tile" else row_spec for kind in extra_kinds]
    out_specs = tuple(tile_spec for _ in out_dtypes)
    out_shape = tuple(SDS((m_out, n), dt) for dt in out_dtypes)
    scratch = [pltpu.VMEM((tm, tn), F32)] if nk > 1 else []
    grid = (m_out // tm, n // tn, nk)
    semantics = _params(("parallel", "parallel", "arbitrary"))
    if m_half is None:
        return pl.pallas_call(body, grid=grid, in_specs=in_specs, out_specs=out_specs, out_shape=out_shape,
                              scratch_shapes=scratch, compiler_params=semantics, name=name)(a, b, *extras)
    assert mode == "tn" and (m // tm) % 2 == 0
    return pl.pallas_call(
        body,
        grid_spec=pltpu.PrefetchScalarGridSpec(num_scalar_prefetch=1, grid=grid, in_specs=in_specs, out_specs=out_specs,
                                               scratch_shapes=scratch),
        out_shape=out_shape, compiler_params=semantics, name=name)(m_half, a, b, *extras)


def _row_specs(s, d, tr):
    row = "row"
    vec = pl.BlockSpec((1, d), lambda i: (0, 0))
    col = pl.BlockSpec((tr, 1), lambda i: (i, 0))
    return row, vec, col


class _ColChunks:
    def __init__(self, refs):
        self.refs = refs

    def __getitem__(self, idx):
        return jnp.concatenate([r[...] for r in self.refs], axis=-1)


def _rows_call(name, body, grid, in_specs, out_specs, out_shape, semantics, args):
    in_x, args_x, groups = [], [], []
    for spec, arr in zip(in_specs, args):
        if isinstance(spec, str):
            rows, d = arr.shape
            tr, dc = rows // grid[0], d // ROW_SPLIT
            in_x += [pl.BlockSpec((tr, dc), functools.partial(lambda i, jj: (i, jj), jj=j)) for j in range(ROW_SPLIT)]
            args_x += [arr] * ROW_SPLIT
            groups.append(ROW_SPLIT)
        else:
            in_x.append(spec)
            args_x.append(arr)
            groups.append(1)
    out_x = [pl.BlockSpec((sh.shape[0] // grid[0], sh.shape[1]), lambda i: (i, 0)) if isinstance(spec, str) else spec
             for spec, sh in zip(out_specs, out_shape)]

    def wrapped(*refs):
        views, k = [], 0
        for g in groups:
            views.append(_ColChunks(refs[k:k + g]) if g > 1 else refs[k])
            k += g
        body(*views, *refs[k:])

    return pl.pallas_call(
        wrapped, grid=grid, in_specs=in_x, out_specs=tuple(out_x), out_shape=tuple(out_shape),
        compiler_params=_params(semantics), name=name)(*args_x)


def _norm_mod_fwd(name, x, g, sc, sh):
    s, d = x.shape
    tr = _tile(s, ROW_TILE)
    row, vec, col = _row_specs(s, d, tr)

    def body(x_ref, g_ref, sc_ref, sh_ref, h_ref, r_ref):
        xv = x_ref[...]
        r = lax.rsqrt(jnp.mean(xv * xv, axis=-1, keepdims=True) + EPS)
        h_ref[...] = ((xv * r * g_ref[...]) * (1.0 + sc_ref[...]) + sh_ref[...]).astype(BF16)
        r_ref[...] = r

    return _rows_call(name, body, (s // tr,), [row, vec, vec, vec], (row, col),
                      (SDS((s, d), BF16), SDS((s, 1), F32)), ("parallel",), (x, g, sc, sh))


def _norm_mod_bwd(name, dh, xin, rstd, g, sc, dres, gate=None):
    s, d = xin.shape
    tr = _tile(s, ROW_TILE)
    row, vec, col = _row_specs(s, d, tr)

    n_gate = 2 if gate is not None else 0

    def body(dh_ref, x_ref, r_ref, g_ref, sc_ref, dres_ref, *rest):
        gate_in, gate_out = rest[:n_gate], rest[n_gate + 4:]
        dx_ref, dsh_ref, dsc_ref, dg_ref = rest[n_gate:n_gate + 4]

        @pl.when(pl.program_id(0) == 0)
        def _():
            for ref in (dsh_ref, dsc_ref, dg_ref) + tuple(gate_out[1:]):
                ref[...] = jnp.zeros_like(ref)

        dh_v, xv, r, gv = dh_ref[...].astype(F32), x_ref[...], r_ref[...], g_ref[...]
        n0 = xv * r
        dsh_ref[...] += jnp.sum(dh_v, axis=0, keepdims=True)
        dsc_ref[...] += jnp.sum(dh_v * (n0 * gv), axis=0, keepdims=True)
        dn = dh_v * (1.0 + sc_ref[...])
        dg_ref[...] += jnp.sum(dn * n0, axis=0, keepdims=True)
        gy = dn * gv
        dot = jnp.mean(gy * xv, axis=-1, keepdims=True)
        dxv = dres_ref[...] + r * gy - xv * (r * r * r * dot)
        dx_ref[...] = dxv
        if gate is not None:
            y_ref, gt_ref = gate_in
            dy_ref, dgt_ref = gate_out
            dy_ref[...] = (gt_ref[...] * dxv).astype(BF16)
            dgt_ref[...] += jnp.sum(dxv * y_ref[...], axis=0, keepdims=True)

    vecs = SDS((1, d), F32)
    gate_args = tuple(gate) if gate is not None else ()
    return _rows_call(
        name, body, (s // tr,),
        [row, row, col, vec, vec, row] + ([row, vec] if gate is not None else []),
        (row, vec, vec, vec) + ((row, vec) if gate is not None else ()),
        (SDS((s, d), F32), vecs, vecs, vecs) + ((SDS((s, d), BF16), vecs) if gate is not None else ()),
        ("arbitrary",), (dh, xin, rstd, g, sc, dres, *gate_args))


def _final_loss(x2, gf, tgt, y, gt):
    s, d = x2.shape
    tr = _tile(s, ROW_TILE)
    row, vec, _ = _row_specs(s, d, tr)
    lrow = pl.BlockSpec((1, LANES), lambda i: (0, 0))

    def body(x_ref, g_ref, t_ref, y_ref, gt_ref, dx_ref, loss_ref, dg_ref, dy_ref, dgt_ref):
        @pl.when(pl.program_id(0) == 0)
        def _():
            loss_ref[...] = jnp.zeros_like(loss_ref)
            dg_ref[...] = jnp.zeros_like(dg_ref)
            dgt_ref[...] = jnp.zeros_like(dgt_ref)

        xv, gv = x_ref[...], g_ref[...]
        r = lax.rsqrt(jnp.mean(xv * xv, axis=-1, keepdims=True) + EPS)
        n0 = xv * r
        err = n0 * gv - t_ref[...]
        loss_ref[...] += jnp.sum(err * err) * (0.5 / d)
        dy = err * (1.0 / d)
        dg_ref[...] += jnp.sum(dy * n0, axis=0, keepdims=True)
        gy = dy * gv
        dot = jnp.mean(gy * xv, axis=-1, keepdims=True)
        dxv = r * gy - xv * (r * r * r * dot)
        dx_ref[...] = dxv
        dy_ref[...] = (gt_ref[...] * dxv).astype(BF16)
        dgt_ref[...] += jnp.sum(dxv * y_ref[...], axis=0, keepdims=True)

    return _rows_call(
        "final_loss", body, (s // tr,), [row, vec, row, row, vec], (row, lrow, vec, row, vec),
        (SDS((s, d), F32), SDS((1, LANES), F32), SDS((1, d), F32), SDS((s, d), BF16), SDS((1, d), F32)),
        ("arbitrary",), (x2, gf, tgt, y, gt))


def _mix_norm_fwd(oa, ol, ga, gl):
    s, w = oa.shape
    tr = _tile(s, ROW_TILE)
    row, vec, col = _row_specs(s, w, tr)

    def body(oa_ref, ol_ref, ga_ref, gl_ref, mx_ref, ra_ref, rl_ref):
        a, l = oa_ref[...], ol_ref[...]
        ra = lax.rsqrt(jnp.mean(a * a, axis=-1, keepdims=True) + EPS)
        rl = lax.rsqrt(jnp.mean(l * l, axis=-1, keepdims=True) + EPS)
        mx_ref[:, :w] = (a * ra * ga_ref[...]).astype(BF16)
        mx_ref[:, w:] = (l * rl * gl_ref[...]).astype(BF16)
        ra_ref[...] = ra
        rl_ref[...] = rl

    return _rows_call(
        "mix_norm_fwd", body, (s // tr,), [row, row, vec, vec], (row, col, col),
        (SDS((s, 2 * w), BF16), SDS((s, 1), F32), SDS((s, 1), F32)), ("parallel",), (oa, ol, ga, gl))


def _mix_norm_bwd(dmx, oa, ol, ra, rl, ga, gl):
    s, w = oa.shape
    tr = _tile(s, ROW_TILE)
    row, vec, col = _row_specs(s, w, tr)

    def body(dm_ref, oa_ref, ol_ref, ra_ref, rl_ref, ga_ref, gl_ref, doa_ref, dol_ref, dga_ref, dgl_ref):
        @pl.when(pl.program_id(0) == 0)
        def _():
            dga_ref[...] = jnp.zeros_like(dga_ref)
            dgl_ref[...] = jnp.zeros_like(dgl_ref)

        def one(dy, xv, r, gv, dg_ref):
            dg_ref[...] += jnp.sum(dy * (xv * r), axis=0, keepdims=True)
            gy = dy * gv
            dot = jnp.mean(gy * xv, axis=-1, keepdims=True)
            return r * gy - xv * (r * r * r * dot)

        dm = dm_ref[...].astype(F32)
        doa_ref[...] = one(dm[:, :w], oa_ref[...], ra_ref[...], ga_ref[...], dga_ref).astype(BF16)
        dol_ref[...] = one(dm[:, w:], ol_ref[...], rl_ref[...], gl_ref[...], dgl_ref)

    return _rows_call(
        "mix_norm_bwd", body, (s // tr,), [row, row, row, col, col, vec, vec], (row, row, vec, vec),
        (SDS((s, w), BF16), SDS((s, w), F32), SDS((1, w), F32), SDS((1, w), F32)), ("arbitrary",),
        (dmx, oa, ol, ra, rl, ga, gl))


def _attn_blocks(qs, ks, tri_after, csums, causal):
    zs = [_dot_nt(q, k) * (HEAD ** -0.5) for q, k in zip(qs, ks)]
    lbs = [_log_sigmoid(z) for z in zs]
    lss = [lb - z for lb, z in zip(lbs, zs)]
    if causal is not None:
        lss = [jnp.where(causal, ls, 0.0) for ls in lss]
    locs = [_split_dot(ls, tri_after) for ls in lss]
    ws = [jnp.exp(lb + (loc + cs)) for lb, loc, cs in zip(lbs, locs, csums)]
    if causal is not None:
        ws = [jnp.where(causal, w, 0.0) for w in ws]
    nxt = [cs + (loc[:, 0:1] + ls[:, 0:1]) for cs, loc, ls in zip(csums, locs, lss)]
    return lbs, ws, nxt


ATTN_HEADS_PER_STEP = 4


def _attn_tile(s):
    return 256 if s >= 1024 else 128


def _tri(t, after):
    r_i = lax.broadcasted_iota(jnp.int32, (t, t), 0)
    c_i = lax.broadcasted_iota(jnp.int32, (t, t), 1)
    return ((r_i > c_i) if after else (r_i < c_i)).astype(BF16)


def _attn_fwd(qkv, n_heads):
    s = qkv.shape[0]
    t = _attn_tile(s)
    hps = ATTN_HEADS_PER_STEP
    wid = hps * HEAD

    nq = s // t

    def body(q_ref, k_ref, v_ref, o_ref, w_ref, sg_ref):
        qi = pl.program_id(1)
        tri_after = _tri(t, True)
        causal = lax.broadcasted_iota(jnp.int32, (t, t), 1) < lax.broadcasted_iota(jnp.int32, (t, t), 0)
        lanes = [slice(a * HEAD, (a + 1) * HEAD) for a in range(hps)]
        qs = [q_ref[:, ln] for ln in lanes]

        def block(kb, carry, mask):
            off = pl.multiple_of(kb * t, t)
            ks = [k_ref[pl.ds(off, t), ln] for ln in lanes]
            lbs, ws, csums = _attn_blocks(qs, ks, tri_after, [cr[0] for cr in carry], mask)
            wbs = [w.astype(BF16) for w in ws]
            for a in range(hps):
                w_ref[a, kb] = wbs[a]
                sg_ref[a, kb] = jnp.exp(lbs[a]).astype(BF16)
            os_ = [cr[1] + _dot(wb, v_ref[pl.ds(off, t), ln]) for cr, wb, ln in zip(carry, wbs, lanes)]
            return tuple(zip(csums, os_))

        zero = tuple((jnp.zeros((t, 1), F32), jnp.zeros((t, HEAD), F32)) for _ in lanes)
        carry = block(qi, zero, causal)
        carry = lax.fori_loop(1, qi + 1, lambda it, cr: block(qi - it, cr, None), carry)
        for a, ln in enumerate(lanes):
            o_ref[:, ln] = carry[a][1]

    hb = n_heads // hps
    kept = pl.BlockSpec((None, hps, nq, t, t), lambda hh, i: (hh * nq + i, 0, 0, 0, 0))
    kept_shape = SDS((hb * nq, hps, nq, t, t), BF16)
    return pl.pallas_call(
        body, grid=(hb, nq),
        in_specs=[pl.BlockSpec((t, wid), lambda hh, i: (i, hh)),
                  pl.BlockSpec((s, wid), lambda hh, i: (0, hb + hh)),
                  pl.BlockSpec((s, wid), lambda hh, i: (0, 2 * hb + hh))],
        out_specs=(pl.BlockSpec((t, wid), lambda hh, i: (i, hh)), kept, kept),
        out_shape=(SDS((s, n_heads * HEAD), F32), kept_shape, kept_shape),
        compiler_params=_params(("parallel", "parallel")), name="attn_fwd")(qkv, qkv, qkv)


def _attn_bwd(qkv, do, w_kept, sg_kept, n_heads):
    s = qkv.shape[0]
    t = _attn_tile(s)
    nq = s // t
    scale = HEAD ** -0.5
    hps = ATTN_HEADS_PER_STEP
    wid = hps * HEAD

    def body(q_ref, k_ref, v_ref, do_ref, w_ref, sg_ref, dq_ref, dk_ref, dv_ref, dk_acc, dv_acc):
        qi = pl.program_id(1)

        @pl.when(qi == 0)
        def _():
            dk_acc[...] = jnp.zeros_like(dk_acc)
            dv_acc[...] = jnp.zeros_like(dv_acc)

        tri_before = _tri(t, False)
        causal = lax.broadcasted_iota(jnp.int32, (t, t), 1) < lax.broadcasted_iota(jnp.int32, (t, t), 0)
        lanes = [slice(a * HEAD, (a + 1) * HEAD) for a in range(hps)]
        qs = [q_ref[:, ln] for ln in lanes]
        douts = [do_ref[:, ln] for ln in lanes]

        def block(kb, carry, mask):
            off = pl.multiple_of(kb * t, t)
            wbs = [w_ref[a, kb] for a in range(hps)]
            dws = [_dot_nt(dout, v_ref[pl.ds(off, t), ln]) for dout, ln in zip(douts, lanes)]
            for a, ln in enumerate(lanes):
                dv_acc[pl.ds(off, t), ln] += _dot_tn(wbs[a], douts[a])
            es = [dw * wb.astype(F32) for dw, wb in zip(dws, wbs)]
            locs = [_split_dot(e, tri_before) for e in es]
            sgs = [sg_ref[a, kb].astype(F32) for a in range(hps)]
            stays = [(loc + cr[0]) * sg for loc, cr, sg in zip(locs, carry, sgs)]
            if mask is not None:
                stays = [jnp.where(mask, st, 0.0) for st in stays]
            dzbs = [((e * (1.0 - sg) - st) * scale).astype(BF16) for e, sg, st in zip(es, sgs, stays)]
            dqs = [cr[1] + _dot(dzb, k_ref[pl.ds(off, t), ln]) for cr, dzb, ln in zip(carry, dzbs, lanes)]
            for a, ln in enumerate(lanes):
                dk_acc[pl.ds(off, t), ln] += _dot_tn(dzbs[a], qs[a])
            esums = [cr[0] + (loc[:, t - 1:t] + e[:, t - 1:t]) for cr, loc, e in zip(carry, locs, es)]
            return tuple(zip(esums, dqs))

        zero = tuple((jnp.zeros((t, 1), F32), jnp.zeros((t, HEAD), F32)) for _ in lanes)
        carry = lax.fori_loop(0, qi, lambda kb, cr: block(kb, cr, None), zero)
        carry = block(qi, carry, causal)
        for a, ln in enumerate(lanes):
            dq_ref[:, ln] = carry[a][1].astype(BF16)

        @pl.when(qi == nq - 1)
        def _():
            dk_ref[...] = dk_acc[...].astype(BF16)
            dv_ref[...] = dv_acc[...].astype(BF16)

    hb = n_heads // hps
    blk = pl.BlockSpec((t, wid), lambda hh, i: (i, hh))
    full = pl.BlockSpec((s, wid), lambda hh, i: (0, hh))
    kept = pl.BlockSpec((None, hps, nq, t, t), lambda hh, i: (hh * nq + i, 0, 0, 0, 0))
    return pl.pallas_call(
        body, grid=(hb, nq),
        in_specs=[blk,
                  pl.BlockSpec((s, wid), lambda hh, i: (0, hb + hh)),
                  pl.BlockSpec((s, wid), lambda hh, i: (0, 2 * hb + hh)),
                  blk, kept, kept],
        out_specs=(blk, full, full),
        out_shape=(SDS((s, n_heads * HEAD), BF16),) * 3,
        scratch_shapes=[pltpu.VMEM((s, wid), F32), pltpu.VMEM((s, wid), F32)],
        compiler_params=_params(("parallel", "arbitrary")), name="attn_bwd")(qkv, qkv, qkv, do, w_kept, sg_kept)


def _lru_chunk(s):
    return 128 if s >= 256 else s // 2


def _lru_gates(xc, wa, ba, wx, bx, sp):
    xb = xc.astype(BF16)
    r = _sigmoid(_dot(xb, wa) + ba)
    ig = _sigmoid(_dot(xb, wx) + bx)
    la = -LRU_C * r * sp
    a = jnp.exp(la)
    t = jnp.tanh(la)
    mult = jnp.sqrt(-2.0 * t / (1.0 - t))
    return r, ig, a, mult


def _softplus_neg(lam):
    return jnp.maximum(-lam, 0.0) + jnp.log(1.0 + jnp.exp(-jnp.abs(lam)))


LRU_BLOCKS_PER_STEP = 1
SCAN_GROUP = 4


def _lru_specs(s, n_blocks):
    bps = min(LRU_BLOCKS_PER_STEP, n_blocks)
    wid = bps * HEAD
    seq0 = pl.BlockSpec((s, wid), lambda h: (0, h))
    seq1 = pl.BlockSpec((s, wid), lambda h: (0, n_blocks // bps + h))
    taps = pl.BlockSpec((CONV_TAPS, wid), lambda h: (0, h))
    vec = pl.BlockSpec((1, wid), lambda h: (0, h))
    mat = pl.BlockSpec((bps, HEAD, HEAD), lambda h: (h, 0, 0))
    return bps, seq0, seq1, taps, vec, mat


def _per_block(one_block, n_2d, n_mat_pos, bps):
    def body(*refs):
        for a in range(bps):
            views = [r.at[a] if i in n_mat_pos else r.at[:, pl.ds(a * HEAD, HEAD)] for i, r in enumerate(refs[:n_2d])]
            one_block(*views, *refs[n_2d:])
    return body


def _lru_fwd(xrg, wconv, bconv, wa, ba, wx, bx, lam):
    s = xrg.shape[0]
    nb = wa.shape[0]
    tc = _lru_chunk(s)
    bps, seq0, seq1, taps, vec, mat = _lru_specs(s, nb)
    pad = SUBLANES

    def one_block(xr_ref, xg_ref, wc_ref, bc_ref, wa_ref, ba_ref, wx_ref, bx_ref, lam_ref, o_ref, h_ref, pad_s, a_s, u_s):
        pad_s[0:pad, :] = jnp.zeros((pad, HEAD), F32)
        pad_s[pad:pad + s, :] = xr_ref[...]
        wab, wxb = wa_ref[...].astype(BF16), wx_ref[...].astype(BF16)
        sp = _softplus_neg(lam_ref[...])
        for c in range(s // tc):
            base = c * tc
            xc = bc_ref[...] + sum(wc_ref[i:i + 1, :] * pad_s[pl.ds(base + pad - (CONV_TAPS - 1) + i, tc), :]
                                   for i in range(CONV_TAPS))
            _, ig, a, mult = _lru_gates(xc, wab, ba_ref[...], wxb, bx_ref[...], sp)
            a_s[base:base + tc, :] = a
            u_s[base:base + tc, :] = mult * (ig * xc)

        row = lax.broadcasted_iota(jnp.int32, (SUBLANES, HEAD), 0)
        last = SUBLANES - 1

        def group(gi, hprev):
            offs = [pl.multiple_of((gi * SCAN_GROUP + q) * SUBLANES, SUBLANES) for q in range(SCAN_GROUP)]
            ab = []
            for off in offs:
                a8, b8 = a_s[pl.ds(off, SUBLANES), :], u_s[pl.ds(off, SUBLANES), :]
                for d in (1, 2, 4):
                    a_sh = jnp.where(row < d, 1.0, pltpu.roll(a8, d, 0))
                    b_sh = jnp.where(row < d, 0.0, pltpu.roll(b8, d, 0))
                    b8 = a8 * b_sh + b8
                    a8 = a8 * a_sh
                ab.append((a8, b8))
            enters = []
            for a8, b8 in ab:
                enters.append(hprev)
                hprev = a8[last:, :] * hprev + b8[last:, :]
            for off, (a8, b8), h0 in zip(offs, ab, enters):
                h_ref[pl.ds(off, SUBLANES), :] = a8 * h0 + b8
            return hprev

        lax.fori_loop(0, s // (SUBLANES * SCAN_GROUP), group, jnp.zeros((1, HEAD), F32))
        for c in range(s // tc):
            sl = slice(c * tc, (c + 1) * tc)
            gel, _ = _gelu_parts(xg_ref[sl, :])
            o_ref[sl, :] = h_ref[sl, :] * gel

    return pl.pallas_call(
        _per_block(one_block, 11, (4, 6), bps), grid=(nb // bps,),
        in_specs=[seq0, seq1, taps, vec, mat, vec, mat, vec, vec],
        out_specs=(seq0, seq0),
        out_shape=(SDS((s, nb * HEAD), F32), SDS((s, nb * HEAD), F32)),
        scratch_shapes=[pltpu.VMEM((s + pad, HEAD), F32), pltpu.VMEM((s, HEAD), F32), pltpu.VMEM((s, HEAD), F32)],
        compiler_params=_params(("parallel",)), name="lru_fwd")(xrg, xrg, wconv, bconv, wa, ba, wx, bx, lam)


def _lru_bwd(xrg, dol, hseq, wconv, bconv, wa, ba, wx, bx, lam):
    s = xrg.shape[0]
    nb = wa.shape[0]
    tc = _lru_chunk(s)
    bps, seq0, seq1, taps, vec, mat = _lru_specs(s, nb)
    pad = SUBLANES

    def one_block(xr_ref, xg_ref, do_ref, h_ref, wc_ref, bc_ref, wa_ref, ba_ref, wx_ref, bx_ref, lam_ref,
             dxr_ref, dxg_ref, dwc_ref, dbc_ref, dwa_ref, dba_ref, dwx_ref, dbx_ref, dlam_ref,
             pad_s, hp_s, a_s, g_s, da_s, dxc_s, xc_s, r_s, ig_s, mult_s):
        pad_s[0:pad, :] = jnp.zeros((pad, HEAD), F32)
        pad_s[pad:pad + s, :] = xr_ref[...]
        hp_s[0:pad, :] = jnp.zeros((pad, HEAD), F32)
        hp_s[pad:pad + s, :] = h_ref[...]
        a_s[s:s + pad, :] = jnp.zeros((pad, HEAD), F32)
        dxc_s[s:s + pad, :] = jnp.zeros((pad, HEAD), F32)
        wab, wxb = wa_ref[...].astype(BF16), wx_ref[...].astype(BF16)
        lam_v = lam_ref[...]
        sp = _softplus_neg(lam_v)

        def conv_in(c):
            base = c * tc
            wins = [pad_s[pl.ds(base + pad - (CONV_TAPS - 1) + i, tc), :] for i in range(CONV_TAPS)]
            xc = bc_ref[...] + sum(wc_ref[i:i + 1, :] * wins[i] for i in range(CONV_TAPS))
            return xc, wins

        for c in range(s // tc):
            sl = slice(c * tc, (c + 1) * tc)
            xc, _ = conv_in(c)
            r, ig, a, mult = _lru_gates(xc, wab, ba_ref[...], wxb, bx_ref[...], sp)
            a_s[sl, :] = a
            xc_s[sl, :], r_s[sl, :], ig_s[sl, :], mult_s[sl, :] = xc, r, ig, mult
            gel, dgel = _gelu_parts(xg_ref[sl, :])
            dov = do_ref[sl, :]
            g_s[sl, :] = dov * gel
            dxg_ref[sl, :] = (dov * h_ref[sl, :] * dgel).astype(BF16)

        row = lax.broadcasted_iota(jnp.int32, (SUBLANES, HEAD), 0)
        n_chunks = s // SUBLANES

        def group(it, gnext):
            offs = [pl.multiple_of((n_chunks - 1 - (it * SCAN_GROUP + q)) * SUBLANES, SUBLANES) for q in range(SCAN_GROUP)]
            cg = []
            for off in offs:
                a8 = a_s[pl.ds(off, SUBLANES), :]
                a8n = a_s[pl.ds(off + SUBLANES, SUBLANES), :]
                c8 = pltpu.roll(jnp.where(row == 0, a8n, a8), SUBLANES - 1, 0)
                g8 = g_s[pl.ds(off, SUBLANES), :]
                for d in (1, 2, 4):
                    c_sh = jnp.where(row >= SUBLANES - d, 1.0, pltpu.roll(c8, SUBLANES - d, 0))
                    g_sh = jnp.where(row >= SUBLANES - d, 0.0, pltpu.roll(g8, SUBLANES - d, 0))
                    g8 = c8 * g_sh + g8
                    c8 = c8 * c_sh
                cg.append((c8, g8))
            enters = []
            for c8, g8 in cg:
                enters.append(gnext)
                gnext = g8[0:1, :] + c8[0:1, :] * gnext
            for off, (c8, g8), g0 in zip(offs, cg, enters):
                gv = g8 + c8 * g0
                g_s[pl.ds(off, SUBLANES), :] = gv
                h8 = hp_s[pl.ds(off + pad, SUBLANES), :]
                h8p = hp_s[pl.ds(off, SUBLANES), :]
                da_s[pl.ds(off, SUBLANES), :] = gv * pltpu.roll(jnp.where(row == SUBLANES - 1, h8p, h8), 1, 0)
            return gnext

        lax.fori_loop(0, n_chunks // SCAN_GROUP, group, jnp.zeros((1, HEAD), F32))

        dsp = jnp.zeros((1, HEAD), F32)
        dbc = jnp.zeros((1, HEAD), F32)
        dba = jnp.zeros((1, HEAD), F32)
        dbx = jnp.zeros((1, HEAD), F32)
        dwa = jnp.zeros((HEAD, HEAD), F32)
        dwx = jnp.zeros((HEAD, HEAD), F32)
        dwc = [jnp.zeros((1, HEAD), F32) for _ in range(CONV_TAPS)]
        for c in range(s // tc):
            sl = slice(c * tc, (c + 1) * tc)
            wins = [pad_s[pl.ds(c * tc + pad - (CONV_TAPS - 1) + i, tc), :] for i in range(CONV_TAPS)]
            xc, r, ig, a, mult = xc_s[sl, :], r_s[sl, :], ig_s[sl, :], a_s[sl, :], mult_s[sl, :]
            du, da = g_s[sl, :], da_s[sl, :]
            d_ix = du * mult
            dla = da * a - (du * ig * xc) * (a * a / mult)
            dsp = dsp + jnp.sum(dla * r, axis=0, keepdims=True) * (-LRU_C)
            dpa = (dla * (-LRU_C * sp)) * r * (1.0 - r)
            dpx = (d_ix * xc) * ig * (1.0 - ig)
            dpab, dpxb, xb = dpa.astype(BF16), dpx.astype(BF16), xc.astype(BF16)
            dxc = d_ix * ig + _dot_nt(dpab, wab) + _dot_nt(dpxb, wxb)
            dwa = dwa + _dot_tn(xb, dpab)
            dwx = dwx + _dot_tn(xb, dpxb)
            dba = dba + jnp.sum(dpa, axis=0, keepdims=True)
            dbx = dbx + jnp.sum(dpx, axis=0, keepdims=True)
            dbc = dbc + jnp.sum(dxc, axis=0, keepdims=True)
            for i in range(CONV_TAPS):
                dwc[i] = dwc[i] + jnp.sum(dxc * wins[i], axis=0, keepdims=True)
            dxc_s[sl, :] = dxc

        for c in range(s // tc):
            base = c * tc
            dxr = sum(wc_ref[i:i + 1, :] * dxc_s[pl.ds(base + (CONV_TAPS - 1) - i, tc), :] for i in range(CONV_TAPS))
            dxr_ref[base:base + tc, :] = dxr.astype(BF16)

        for i in range(CONV_TAPS):
            dwc_ref[i:i + 1, :] = dwc[i]
        dbc_ref[...] = dbc
        dwa_ref[...] = dwa
        dwx_ref[...] = dwx
        dba_ref[...] = dba
        dbx_ref[...] = dbx
        dlam_ref[...] = dsp * (-_sigmoid(-lam_v))

    w = nb * HEAD
    return pl.pallas_call(
        _per_block(one_block, 20, (6, 8, 15, 17), bps), grid=(nb // bps,),
        in_specs=[seq0, seq1, seq0, seq0, taps, vec, mat, vec, mat, vec, vec],
        out_specs=(seq0, seq0, taps, vec, mat, vec, mat, vec, vec),
        out_shape=(SDS((s, w), BF16), SDS((s, w), BF16), SDS((CONV_TAPS, w), F32), SDS((1, w), F32),
                   SDS((nb, HEAD, HEAD), F32), SDS((1, w), F32), SDS((nb, HEAD, HEAD), F32), SDS((1, w), F32),
                   SDS((1, w), F32)),
        scratch_shapes=[pltpu.VMEM((s + pad, HEAD), F32), pltpu.VMEM((s + pad, HEAD), F32),
                        pltpu.VMEM((s + pad, HEAD), F32), pltpu.VMEM((s, HEAD), F32),
                        pltpu.VMEM((s, HEAD), F32), pltpu.VMEM((s + pad, HEAD), F32)]
                       + [pltpu.VMEM((s, HEAD), F32)] * 4,
        compiler_params=_params(("parallel",)), name="lru_bwd",
    )(xrg, xrg, dol, hseq, wconv, bconv, wa, ba, wx, bx, lam)


def _ada_mod(c_all, w_sh, b_sh):
    n_ex, d = c_all.shape
    n = w_sh.shape[1]
    tn = _tile(n, 512)

    def body(c_ref, w_ref, b_ref, mod_ref, act_ref):
        cv = c_ref[...]
        act = cv * _sigmoid(cv)
        act_ref[...] = act
        mod_ref[...] = _dot(act.astype(BF16), w_ref[...].astype(BF16)) + b_ref[...]

    return pl.pallas_call(
        body, grid=(n // tn,),
        in_specs=[pl.BlockSpec((n_ex, d), lambda j: (0, 0)), pl.BlockSpec((d, tn), lambda j: (0, j)),
                  pl.BlockSpec((1, tn), lambda j: (0, j))],
        out_specs=(pl.BlockSpec((n_ex, tn), lambda j: (0, j)), pl.BlockSpec((n_ex, d), lambda j: (0, 0))),
        out_shape=(SDS((n_ex, n), F32), SDS((n_ex, d), F32)),
        compiler_params=_params(("arbitrary",)), name="ada_mod")(c_all, w_sh, b_sh)


def _adamw_math(w, g, m, v):
    m = ADAM_B1 * m + (1.0 - ADAM_B1) * g
    v = ADAM_B2 * v + (1.0 - ADAM_B2) * (g * g)
    m_hat = m / (1.0 - ADAM_B1 ** ADAM_STEP)
    v_hat = v / (1.0 - ADAM_B2 ** ADAM_STEP)
    delta = -ADAM_LR * (m_hat / (jnp.sqrt(v_hat) + ADAM_EPS) + ADAM_WD * w)
    return delta, m, v


def _adamw_plain(name, w, g, m, v):
    def body(w_ref, g_ref, m_ref, v_ref, d_ref, mo_ref, vo_ref):
        d_ref[...], mo_ref[...], vo_ref[...] = _adamw_math(w_ref[...], g_ref[...], m_ref[...], v_ref[...])

    return pl.pallas_call(body, out_shape=(SDS(w.shape, F32),) * 3, name=name)(w, g, m, v)


def _adamw_halves(name, c_arr, w, m, v, g_own, g_recv):
    r, n = w.shape
    rh = r // 2
    tr = _tile(rh, 256)
    nh = rh // tr

    def body(c_ref, w_ref, m_ref, v_ref, go_ref, gr_ref, g_ref, d_ref, mo_ref, vo_ref):
        own = (pl.program_id(0) // nh) == c_ref[0]
        g = jnp.where(own, go_ref[...], gr_ref[...])
        g_ref[...] = g
        d_ref[...], mo_ref[...], vo_ref[...] = _adamw_math(w_ref[...], g, m_ref[...], v_ref[...])

    full = pl.BlockSpec((tr, n), lambda i, c: (i, 0))
    own = pl.BlockSpec((tr, n), lambda i, c: (jnp.where(i // nh == c[0], i % nh, 0), 0))
    recv = pl.BlockSpec((tr, n), lambda i, c: (jnp.where(i // nh == c[0], 0, i % nh), 0))
    return pl.pallas_call(
        body,
        grid_spec=pltpu.PrefetchScalarGridSpec(
            num_scalar_prefetch=1, grid=(2 * nh,), in_specs=[full, full, full, own, recv],
            out_specs=(full,) * 4),
        out_shape=(SDS((r, n), F32),) * 4,
        compiler_params=_params(("parallel",)), name=name)(c_arr, w, m, v, g_own, g_recv)


def _adamw_ada(w, m, v, act_t, dmod):
    d, n = w.shape
    n_ex = act_t.shape[1]
    tr = _tile(d, 256)

    def body(a_ref, dm_ref, w_ref, m_ref, v_ref, g_ref, d_ref, mo_ref, vo_ref):
        g = _dot(a_ref[...], dm_ref[...])
        g_ref[...] = g
        d_ref[...], mo_ref[...], vo_ref[...] = _adamw_math(w_ref[...], g, m_ref[...], v_ref[...])

    full = pl.BlockSpec((tr, n), lambda i: (i, 0))
    return pl.pallas_call(
        body, grid=(d // tr,),
        in_specs=[pl.BlockSpec((tr, n_ex), lambda i: (i, 0)), pl.BlockSpec((n_ex, n), lambda i: (0, 0)), full, full, full],
        out_specs=(full,) * 4, out_shape=(SDS((d, n), F32),) * 4,
        compiler_params=_params(("parallel",)), name="adamw_ada")(act_t, dmod, w, m, v)


def _small_reduce_adamw(parts, w, m, v):
    n_dev, r, _ = parts.shape
    tr = r if r <= PACK_ROWS else PACK_ROWS

    def body(p_ref, w_ref, m_ref, v_ref, g_ref, d_ref, mo_ref, vo_ref):
        g = p_ref[0]
        for k in range(1, n_dev):
            g = g + p_ref[k]
        g_ref[...] = g
        d_ref[...], mo_ref[...], vo_ref[...] = _adamw_math(w_ref[...], g, m_ref[...], v_ref[...])

    full = pl.BlockSpec((tr, LANES), lambda i: (i, 0))
    return pl.pallas_call(
        body, grid=(r // tr,),
        in_specs=[pl.BlockSpec((n_dev, tr, LANES), lambda i: (0, i, 0)), full, full, full],
        out_specs=(full,) * 4, out_shape=(SDS((r, LANES), F32),) * 4,
        compiler_params=_params(("parallel",)), name="small_reduce_adamw")(parts, w, m, v)


def _mesh_pos():
    return lax.axis_index("x"), lax.axis_index("y"), lax.axis_index("c")


def _other_chips(x, y):
    return [(1 - x, y), (x, 1 - y), (1 - x, 1 - y)]


def _all_gather_small(name, blk, after=()):
    r, n = blk.shape
    n_after = len(after)

    def body(x_ref, *rest):
        out_ref, send_sems, recv_sems, local_sem = rest[n_after:]
        x, y, c = _mesh_pos()
        me, sibling = (x, y, c), (x, y, 1 - c)
        chips = _other_chips(x, y)

        def rows(px, py, pc):
            return out_ref.at[4 * px + 2 * py + pc]

        def copy(k, block, to, src=None):
            return pltpu.make_async_remote_copy(
                src_ref=rows(*block) if src is None else src, dst_ref=rows(*block),
                send_sem=send_sems.at[k], recv_sem=recv_sems.at[k], device_id=to, device_id_type=MESH)

        mine = pltpu.make_async_copy(x_ref, rows(*me), local_sem)
        mine.start()
        first = [copy(0, me, sibling, src=x_ref)]
        first += [copy(1 + j, me, (*chip, c), src=x_ref) for j, chip in enumerate(chips)]
        for cp in first:
            cp.start()
        passed = [copy(4 + j, (*chip, c), sibling) for j, chip in enumerate(chips)]
        for j, chip in enumerate(chips):
            copy(1 + j, (*chip, c), me).wait_recv()
            passed[j].start()
        copy(0, sibling, me).wait_recv()
        for j, chip in enumerate(chips):
            copy(4 + j, (*chip, 1 - c), me).wait_recv()
        for cp in first + passed:
            cp.wait_send()
        mine.wait()

    return pl.pallas_call(
        body, out_shape=SDS((N_DEV, r, n), blk.dtype),
        in_specs=[pl.BlockSpec(memory_space=pltpu.VMEM)] + [pl.BlockSpec(memory_space=pl.ANY)] * n_after,
        out_specs=pl.BlockSpec(memory_space=pltpu.VMEM),
        scratch_shapes=[pltpu.SemaphoreType.DMA((7,)), pltpu.SemaphoreType.DMA((7,)), pltpu.SemaphoreType.DMA],
        compiler_params=pltpu.CompilerParams(vmem_limit_bytes=VMEM_LIMIT), name=name)(blk, *after)


_ANY = pl.BlockSpec(memory_space=pl.ANY)
_HBM = pl.BlockSpec(memory_space=pltpu.HBM)
_SEM = pl.BlockSpec(memory_space=pltpu.SEMAPHORE)
_EFFECT = pltpu.SideEffectType.DATAFLOW_SIDE_EFFECTING


def _hbm(a):
    return pltpu.with_memory_space_constraint(a, pltpu.HBM)


def _place_cast(name, j_arr, shard, kind, after):
    r, n = shard.shape
    tr = _tile(r, 256)
    nr = r // tr
    if kind == "col":
        out_shape, o_spec = (r, N_CHIP * n), pl.BlockSpec((tr, n), lambda i, j: (i, j[0]))
    else:
        out_shape, o_spec = (N_CHIP * r, n), pl.BlockSpec((tr, n), lambda i, j: (j[0] * nr + i, 0))

    def body(j_ref, s_ref, after_ref, o_ref, tok_ref):
        o_ref[...] = s_ref[...].astype(BF16)
        tok_ref[...] = jnp.zeros_like(tok_ref)

    return pl.pallas_call(
        body,
        grid_spec=pltpu.PrefetchScalarGridSpec(
            num_scalar_prefetch=1, grid=(nr,), in_specs=[pl.BlockSpec((tr, n), lambda i, j: (i, 0)), _ANY],
            out_specs=(o_spec, pl.BlockSpec((SUBLANES, LANES), lambda i, j: (0, 0)))),
        out_shape=(SDS(out_shape, BF16), SDS((SUBLANES, LANES), F32)),
        compiler_params=_params(("arbitrary",)), name=name)(j_arr, shard, after)


def _leg_direct(full, kind, x, y, c):
    mine = _full_region(full, kind, x, y, c)
    return [(mine, mine, (1 - x, y, c)), (mine, mine, (x, 1 - y, c))]


def _leg_relay(full, kind, x, y, c):
    fx, fy = jnp.where(c == 0, 1 - x, x), jnp.where(c == 0, y, 1 - y)
    tx, ty = jnp.where(c == 0, x, 1 - x), jnp.where(c == 0, 1 - y, y)
    got = _full_region(full, kind, fx, fy, c)
    return [(got, got, (tx, ty, c))]


def _leg_d2d(which):
    def leg(full, kind, x, y, c):
        chips = _other_chips(x, y)
        return [(_full_region(full, kind, *chips[k], c), _full_region(full, kind, *chips[k], c), (x, y, 1 - c))
                for k in which]
    return leg


_LEGS = {"direct": (_leg_direct, 2), "relay": (_leg_relay, 1), "d2d_near": (_leg_d2d((0, 1)), 2),
         "d2d_far": (_leg_d2d((2,)), 1)}


def _gather_call(name, fulls, kinds, waits, starts, after, thru):
    nw, n_wait, n_start = len(fulls), len(waits), len(starts)

    def body(*refs):
        wait_sems = refs[nw:nw + 2 * n_wait]
        outs = refs[nw + 2 * n_wait + 2:]
        full, start_sems = outs[:nw], outs[nw:nw + 2 * n_start]
        x, y, c = _mesh_pos()
        for i, (leg, ws, _, _) in enumerate(waits):
            fn, per = _LEGS[leg]
            for li, w in enumerate(ws):
                for k, (s_, d_, dev) in enumerate(fn(full[w], kinds[w], x, y, c)):
                    cp = pltpu.make_async_remote_copy(
                        src_ref=s_, dst_ref=d_, send_sem=wait_sems[2 * i].at[per * li + k],
                        recv_sem=wait_sems[2 * i + 1].at[per * li + k], device_id=dev, device_id_type=MESH)
                    cp.wait_recv()
                    cp.wait_send()
        for i, (leg, ws) in enumerate(starts):
            fn, per = _LEGS[leg]
            for li, w in enumerate(ws):
                for k, (s_, d_, dev) in enumerate(fn(full[w], kinds[w], x, y, c)):
                    pltpu.make_async_remote_copy(
                        src_ref=s_, dst_ref=d_, send_sem=start_sems[2 * i].at[per * li + k],
                        recv_sem=start_sems[2 * i + 1].at[per * li + k], device_id=dev, device_id_type=MESH).start()

    sems = []
    for leg, ws in starts:
        sems += [pltpu.SemaphoreType.DMA((_LEGS[leg][1] * len(ws),))] * 2
    wait_args = []
    for _, _, s_, r_ in waits:
        wait_args += [s_, r_]
    outs = pl.pallas_call(
        body,
        out_shape=tuple(pltpu.HBM(f_.shape, f_.dtype) for f_ in fulls) + tuple(sems) + (SDS(thru.shape, thru.dtype),),
        in_specs=[_HBM] * nw + [_SEM] * (2 * n_wait) + [_ANY, _ANY],
        out_specs=tuple([_HBM] * nw + [_SEM] * (2 * n_start) + [_ANY]),
        input_output_aliases={**{w: w for w in range(nw)}, nw + 2 * n_wait + 1: nw + 2 * n_start},
        compiler_params=pltpu.CompilerParams(has_side_effects=_EFFECT),
        name=name,
    )(*[_hbm(f_) for f_ in fulls], *wait_args, after, thru)
    pairs = [(outs[nw + 2 * i], outs[nw + 2 * i + 1]) for i in range(n_start)]
    return list(outs[:nw]), pairs, outs[nw + 2 * n_start]


def _full_region(full, kind, px, py, half):
    j = 2 * px + py
    if kind == "col":
        rh, cols = full.shape[0] // 2, full.shape[1] // N_CHIP
        return full.at[pl.ds(half * rh, rh), pl.ds(j * cols, cols)]
    rows = full.shape[0] // N_CHIP
    rh = rows // 2
    return full.at[pl.ds(j * rows + half * rh, rh), :]


def _plan_scatter(kind):
    def plan(src, land, x, y, c):
        out = []
        for k, (px, py) in enumerate(_other_chips(x, y)):
            j = 2 * px + py
            if kind == "col":
                n = src.shape[1] // N_CHIP
                blk = src.at[:, pl.ds(j * n, n)]
            else:
                blk = src.at[j]
            out.append((blk, land.at[k], (px, py, c)))
        return out
    return plan


def _plan_whole(src, land, x, y, c):
    return [(src, land, (x, y, 1 - c))]


def _split_start(name, src, land_shape, n, plan, thru):
    def body(src_in, land_in, thru_in, send, recv, src_ref, land_ref, thru_out):
        x, y, c = _mesh_pos()
        for k, (s_, d_, dev) in enumerate(plan(src_ref, land_ref, x, y, c)):
            pltpu.make_async_remote_copy(src_ref=s_, dst_ref=d_, send_sem=send.at[k], recv_sem=recv.at[k],
                                         device_id=dev, device_id_type=MESH).start()

    sem = pltpu.SemaphoreType.DMA((n,))
    return pl.pallas_call(
        body,
        out_shape=(sem, sem, pltpu.HBM(src.shape, src.dtype), pltpu.HBM(land_shape, src.dtype), SDS(thru.shape, thru.dtype)),
        in_specs=[_HBM, _HBM, _ANY], out_specs=(_SEM, _SEM, _HBM, _HBM, _ANY),
        input_output_aliases={0: 2, 1: 3, 2: 4},
        compiler_params=pltpu.CompilerParams(has_side_effects=_EFFECT), name=name,
    )(_hbm(src), _hbm(lax.empty(land_shape, src.dtype)), thru)


def _split_wait(name, send, recv, src, land, plan, after):
    def body(src_in, land_in, send_r, recv_r, after_r, src_ref, land_ref):
        x, y, c = _mesh_pos()
        for k, (s_, d_, dev) in enumerate(plan(src_ref, land_ref, x, y, c)):
            cp = pltpu.make_async_remote_copy(src_ref=s_, dst_ref=d_, send_sem=send_r.at[k], recv_sem=recv_r.at[k],
                                              device_id=dev, device_id_type=MESH)
            cp.wait_send()
            cp.wait_recv()

    return pl.pallas_call(
        body,
        out_shape=(pltpu.HBM(src.shape, src.dtype), pltpu.HBM(land.shape, land.dtype)),
        in_specs=[_HBM, _HBM, _SEM, _SEM, _ANY], out_specs=(_HBM, _HBM),
        input_output_aliases={0: 0, 1: 1},
        compiler_params=pltpu.CompilerParams(has_side_effects=_EFFECT), name=name,
    )(src, land, send, recv, after)


def _dev_row(buf, px, py, pc):
    return buf.at[4 * px + 2 * py + pc]


def _plan_gather_own(buf, land, x, y, c):
    own = _dev_row(buf, x, y, c)
    return [(own, own, (x, y, 1 - c))] + [(own, own, (px, py, c)) for px, py in _other_chips(x, y)]


def _plan_gather_pass(buf, land, x, y, c):
    return [(_dev_row(buf, px, py, c), _dev_row(buf, px, py, c), (x, y, 1 - c)) for px, py in _other_chips(x, y)]


def _split_start_inplace(name, buf, n, plan, thru):
    def body(buf_in, thru_in, send, recv, buf_ref, thru_out):
        x, y, c = _mesh_pos()
        for k, (s_, d_, dev) in enumerate(plan(buf_ref, buf_ref, x, y, c)):
            pltpu.make_async_remote_copy(src_ref=s_, dst_ref=d_, send_sem=send.at[k], recv_sem=recv.at[k],
                                         device_id=dev, device_id_type=MESH).start()

    sem = pltpu.SemaphoreType.DMA((n,))
    return pl.pallas_call(
        body, out_shape=(sem, sem, pltpu.HBM(buf.shape, buf.dtype), SDS(thru.shape, thru.dtype)),
        in_specs=[_HBM, _ANY], out_specs=(_SEM, _SEM, _HBM, _ANY), input_output_aliases={0: 2, 1: 3},
        compiler_params=pltpu.CompilerParams(has_side_effects=_EFFECT), name=name)(_hbm(buf), thru)


def _split_wait_inplace(name, send, recv, buf, plan, after):
    def body(buf_in, send_r, recv_r, after_r, buf_ref):
        x, y, c = _mesh_pos()
        for k, (s_, d_, dev) in enumerate(plan(buf_ref, buf_ref, x, y, c)):
            cp = pltpu.make_async_remote_copy(src_ref=s_, dst_ref=d_, send_sem=send_r.at[k], recv_sem=recv_r.at[k],
                                              device_id=dev, device_id_type=MESH)
            cp.wait_send()
            cp.wait_recv()

    return pl.pallas_call(
        body, out_shape=pltpu.HBM(buf.shape, buf.dtype), in_specs=[_HBM, _SEM, _SEM, _ANY], out_specs=_HBM,
        input_output_aliases={0: 0}, compiler_params=pltpu.CompilerParams(has_side_effects=_EFFECT), name=name,
    )(buf, send, recv, after)


def _place_row(name, me_arr, slab):
    r, n = slab.shape
    tr = r if r <= PACK_ROWS else PACK_ROWS

    def body(me_ref, s_ref, o_ref):
        o_ref[...] = s_ref[...]

    return pl.pallas_call(
        body,
        grid_spec=pltpu.PrefetchScalarGridSpec(
            num_scalar_prefetch=1, grid=(r // tr,), in_specs=[pl.BlockSpec((tr, n), lambda i, me: (i, 0))],
            out_specs=pl.BlockSpec((None, tr, n), lambda i, me: (me[0], i, 0))),
        out_shape=SDS((N_DEV, r, n), slab.dtype), compiler_params=_params(("parallel",)), name=name)(me_arr, slab)


def _sum_partials(name, j_arr, part, got, kind):
    _, rh, n = got.shape
    tr = _tile(rh, 256)
    if kind == "col":
        p_spec = pl.BlockSpec((tr, n), lambda i, j: (i, j[0]))
    else:
        p_spec = pl.BlockSpec((None, tr, n), lambda i, j: (j[0], i, 0))

    def body(j_ref, p_ref, r_ref, o_ref):
        o_ref[...] = ((p_ref[...].astype(F32) + r_ref[0].astype(F32)) + r_ref[1].astype(F32)) + r_ref[2].astype(F32)

    return pl.pallas_call(
        body,
        grid_spec=pltpu.PrefetchScalarGridSpec(
            num_scalar_prefetch=1, grid=(rh // tr,),
            in_specs=[p_spec, pl.BlockSpec((3, tr, n), lambda i, j: (0, i, 0))],
            out_specs=pl.BlockSpec((tr, n), lambda i, j: (i, 0))),
        out_shape=SDS((rh, n), F32),
        compiler_params=_params(("parallel",)), name=name)(j_arr, part, got)


def _pack(arrays):
    flat = [a.reshape(-1).astype(F32) for a in arrays]
    flat = [jnp.pad(f, (0, (-f.shape[0]) % LANES)) for f in flat]
    sizes = [f.shape[0] for f in flat]
    total = sum(sizes)
    rows = total // LANES
    tail = LANES * ((-rows) % (PACK_ROWS if rows > PACK_ROWS else SUBLANES))
    if tail:
        flat.append(jnp.zeros((tail,), F32))
    return jnp.concatenate(flat).reshape(-1, LANES), sizes


def _unpack(slab, sizes, shapes, lead=()):
    flat = slab.reshape(lead + (-1,))
    out, off = [], 0
    for sz, shp in zip(sizes, shapes):
        n = math.prod(shp)
        out.append(flat[..., off:off + n].reshape(lead + tuple(shp)))
        off += sz
    return out


def kernel(x, c, w_ada, b_ada, g_norm_mix, w_in, w_conv, b_conv, w_rg_a, b_rg_a, w_rg_x, b_rg_x, lru_lambda, g_attn_out, g_lru_out, w_out, g_norm_mlp, w_mlp_in, w_mlp_out, g_norm_final, loss_target, m_w_ada, m_b_ada, m_g_norm_mix, m_w_in, m_w_conv, m_b_conv, m_w_rg_a, m_b_rg_a, m_w_rg_x, m_b_rg_x, m_lru_lambda, m_g_attn_out, m_g_lru_out, m_w_out, m_g_norm_mlp, m_w_mlp_in, m_w_mlp_out, m_g_norm_final, v_w_ada, v_b_ada, v_g_norm_mix, v_w_in, v_w_conv, v_b_conv, v_w_rg_a, v_b_rg_a, v_w_rg_x, v_b_rg_x, v_lru_lambda, v_g_attn_out, v_g_lru_out, v_w_out, v_g_norm_mlp, v_w_mlp_in, v_w_mlp_out, v_g_norm_final):
    s, d = x.shape[1], x.shape[2]
    aw = d // 2
    nh = aw // HEAD
    f = w_mlp_out.shape[1] * N_CHIP
    n_ada = w_ada.shape[2]
    n_cv = w_conv.shape[2]
    ix, iy, ic = lax.axis_index("x"), lax.axis_index("y"), lax.axis_index("c")
    chip = 2 * ix + iy
    me = 2 * chip + ic
    c_arr = jnp.reshape(ic, (1,)).astype(jnp.int32)
    j_arr = jnp.reshape(chip, (1,)).astype(jnp.int32)

    x2d, tgt = x[0], loss_target[0]

    k_in, k_out, k_mi, k_mo = kinds = ("col", "row", "col", "row")
    slab, sizes = _pack([c, w_conv])
    p_in, _ = _place_cast("place_cast_0", j_arr, w_in[0], k_in, c)
    (f_in,), (dir_in,), slab = _gather_call("gather_0", [p_in], [k_in], [], [("direct", [0])], c, slab)
    p_out, tok = _place_cast("place_cast_1", j_arr, w_out[0], k_out, slab)
    p_mi, tok = _place_cast("place_cast_2", j_arr, w_mlp_in[0], k_mi, tok)
    p_mo, tok = _place_cast("place_cast_3", j_arr, w_mlp_out[0], k_mo, tok)

    gathered = _all_gather_small("comm_gather_cond", slab, after=(tok,))
    c_parts, cv_parts = _unpack(gathered, sizes, [(d,), (CONV_TAPS, n_cv)], lead=(N_DEV,))
    c_all = c_parts
    w_conv_full = jnp.concatenate([cv_parts[2 * j] for j in range(N_CHIP)], axis=-1)
    b_sh = lax.dynamic_slice(b_ada, (0, chip * n_ada), (1, n_ada))
    (f_in,), (rel_in, near_in), c_all = _gather_call(
        "gather_1", [f_in], [k_in], [("direct", [0], *dir_in)], [("relay", [0]), ("d2d_near", [0])], gathered, c_all)
    mod_part, act_all = _ada_mod(c_all, w_ada[0], b_sh)
    mod_g = _all_gather_small("comm_gather_mod", mod_part.reshape(-1, LANES))
    mod_g = mod_g.reshape(N_DEV, N_DEV, n_ada)
    mod = jnp.concatenate([lax.dynamic_index_in_dim(mod_g[2 * j], me, 0, keepdims=True) for j in range(N_CHIP)], axis=-1)
    sh1, sc1, gt1, sh2, sc2, gt2 = [mod[:, k * d:(k + 1) * d] for k in range(N_MOD)]

    (f_in, f_out, f_mi), (far_in, dir_out, dir_mi), sh1 = _gather_call(
        "gather_2", [f_in, p_out, p_mi], kinds[:3], [("relay", [0], *rel_in)],
        [("d2d_far", [0]), ("direct", [1]), ("direct", [2])], mod, sh1)

    h1, rstd1 = _norm_mod_fwd("norm_mod_fwd1", x2d, g_norm_mix, sc1, sh1)
    (w_in_f,), _, h1 = _gather_call("gather_3", [f_in], [k_in],
                                    [("d2d_near", [0], *near_in), ("d2d_far", [0], *far_in)], [], rstd1, h1)
    (qkv,) = _matmul("mm_qkv", h1, w_in_f, "nn", s, 3 * aw, d, (BF16,))
    (xrg,) = _matmul("mm_xrg", h1, w_in_f, "nn", s, 2 * aw, d, (F32,), b_off=3 * aw)
    (f_out, f_mo), (rel_out, near_out, dir_mo), qkv = _gather_call(
        "gather_3b", [f_out, p_mo], [k_out, k_mo], [("direct", [0], *dir_out)],
        [("relay", [0]), ("d2d_near", [0]), ("direct", [1])], xrg, qkv)
    o_attn, attn_w, attn_sg = _attn_fwd(qkv, nh)
    (f_out, f_mi), (far_out, rel_mi, near_mi), xrg = _gather_call(
        "gather_4", [f_out, f_mi], [k_out, k_mi], [("relay", [0], *rel_out), ("direct", [1], *dir_mi)],
        [("d2d_far", [0]), ("relay", [1]), ("d2d_near", [1])], o_attn, xrg)
    wa3, wx3 = w_rg_a[0], w_rg_x[0]
    o_lru, hseq = _lru_fwd(xrg, w_conv_full, b_conv, wa3, b_rg_a, wx3, b_rg_x, lru_lambda)
    mixed, rstd_a, rstd_l = _mix_norm_fwd(o_attn, o_lru, g_attn_out, g_lru_out)
    (f_mo,), (rel_mo, near_mo), mixed = _gather_call(
        "gather_5", [f_mo], [k_mo], [("direct", [0], *dir_mo)], [("relay", [0]), ("d2d_near", [0])], rstd_a, mixed)
    (w_out_f,), _, mixed = _gather_call(
        "gather_6", [f_out], [k_out], [("d2d_near", [0], *near_out), ("d2d_far", [0], *far_out)], [], rstd_l, mixed)

    def residual(acc, xin, gt):
        return acc, xin + gt * acc

    y1, x1 = _matmul("mm_out", mixed, w_out_f, "nn", s, d, d, (BF16, F32), extras=(x2d, gt1),
                     extra_kinds=("tile", "row"), epilogue=residual)
    (f_mi,), (far_mi,), x1 = _gather_call("gather_6b", [f_mi], [k_mi], [("relay", [0], *rel_mi)],
                                          [("d2d_far", [0])], y1, x1)
    h2, rstd2 = _norm_mod_fwd("norm_mod_fwd2", x1, g_norm_mlp, sc2, sh2)
    (w_mi_f, f_mo), (far_mo,), h2 = _gather_call(
        "gather_7", [f_mi, f_mo], [k_mi, k_mo],
        [("d2d_near", [0], *near_mi), ("d2d_far", [0], *far_mi), ("relay", [1], *rel_mo)], [("d2d_far", [1])], rstd2, h2)

    def sq_relu(acc):
        r = jnp.maximum(acc, 0.0)
        return 2.0 * r, r * r

    r2, hid = _matmul("mm_mlp_in", h2, w_mi_f, "nn", s, f, d, (BF16, BF16), epilogue=sq_relu)
    (w_mo_f,), _, hid = _gather_call("gather_8", [f_mo], [k_mo],
                                     [("d2d_near", [0], *near_mo), ("d2d_far", [0], *far_mo)], [], r2, hid)
    y2, x2 = _matmul("mm_mlp_out", hid, w_mo_f, "nn", s, d, f, (BF16, F32), extras=(x1, gt2),
                     extra_kinds=("tile", "row"), epilogue=residual)
    dx2, loss_row, dg_final, dy2, dgt2 = _final_loss(x2, g_norm_final.reshape(1, d), tgt, y2, gt2)

    oc_arr = 1 - c_arr

    def dw_half(name, st, h_arr, got=None):
        add = {} if got is None else dict(extras=(got,), extra_kinds=("tile",),
                                          epilogue=lambda acc, g_: (acc + g_.astype(F32),))
        tn = st["n"] if st["tm"] * 4 <= MM_TILE_M else MM_TILE_N
        (out,) = _matmul(name, st["a"], st["dy"], "tn", st["m"], st["n"], s, (BF16,), tm=st["tm"], tn=tn,
                         m_half=h_arr, **add)
        return out

    def rs_begin(tag, kind, xa, dy, m, n, thru):
        st = {"tag": tag, "kind": kind, "a": xa, "dy": dy, "m": m, "n": n,
              "tm": m // (2 * N_CHIP) if kind == "row" else m // 2}
        first = dw_half("mm_dw_%s_a" % tag, st, oc_arr)
        send, recv, first, land, thru = _split_start("rs_swap_start_" + tag, first, first.shape, 1, _plan_whole, thru)
        st["swap"] = (send, recv, first, land)
        return st, thru

    def rs_mid(st, after, thru):
        tag, kind = st["tag"], st["kind"]
        _, got = _split_wait("rs_swap_wait_" + tag, *st["swap"], _plan_whole, after)
        part = dw_half("mm_dw_%s_b" % tag, st, c_arr, got)
        if kind == "row":
            part = part.reshape(N_CHIP, st["m"] // (2 * N_CHIP), st["n"])
        blk = (part.shape[0], part.shape[1] // N_CHIP) if kind == "col" else part.shape[1:]
        send, recv, part, land, thru = _split_start("rs_scatter_start_" + tag, part, (N_CHIP - 1,) + blk, N_CHIP - 1,
                                                    _plan_scatter(kind), thru)
        st["scatter"] = (send, recv, part, land)
        return thru

    def rs_end(st, after):
        tag, kind = st["tag"], st["kind"]
        part, got = _split_wait("rs_scatter_wait_" + tag, *st["scatter"], _plan_scatter(kind), after)
        return _sum_partials("sum_partials_" + tag, j_arr, part, got, kind)

    (dpre,) = _matmul("mm_dhid", dy2, w_mo_f, "nt", s, f, d, (BF16,), extras=(r2,), extra_kinds=("tile",),
                      epilogue=lambda acc, r: (acc * r.astype(F32),))
    st_mo, dpre = rs_begin("mo", "row", hid, dy2, f, d, dpre)
    (dh2,) = _matmul("mm_dh2", dpre, w_mi_f, "nt", s, d, f, (BF16,))
    dh2 = rs_mid(st_mo, dh2, dh2)
    st_mi, dh2 = rs_begin("mi", "col", h2, dpre, d, f, dh2)
    dx1, dsh2, dsc2, dg_mlp, dy1, dgt1 = _norm_mod_bwd("norm_mod_bwd2", dh2, x1, rstd2, g_norm_mlp, sc2, dx2,
                                                       gate=(y1, gt1))
    (dmixed,) = _matmul("mm_dmixed", dy1, w_out_f, "nt", s, d, d, (BF16,))
    dmixed = rs_mid(st_mi, dmixed, dmixed)
    st_out, dmixed = rs_begin("out", "row", mixed, dy1, d, d, dmixed)
    do_attn, do_lru, dg_attn, dg_lru = _mix_norm_bwd(dmixed, o_attn, o_lru, rstd_a, rstd_l, g_attn_out, g_lru_out)
    dq, dk, dv = _attn_bwd(qkv, do_attn, attn_w, attn_sg, nh)
    do_lru = rs_mid(st_out, dq, do_lru)
    dxr, dxg, dwconv, dbconv, dwa, dba, dwx, dbx, dlam = _lru_bwd(
        xrg, do_lru, hseq, w_conv_full, b_conv, wa3, b_rg_a, wx3, b_rg_x, lru_lambda)
    dproj = jnp.concatenate([dq, dk, dv, dxr, dxg], axis=-1)
    st_in, dproj = rs_begin("in", "col", h1, dproj, d, 5 * aw, dproj)
    st_in["dy"] = dproj
    (dh1,) = _matmul("mm_dh1", dproj, w_in_f, "nt", s, d, 5 * aw, (BF16,))
    dh1 = rs_mid(st_in, dh1, dh1)
    grad_x, dsh1, dsc1, dg_mix = _norm_mod_bwd("norm_mod_bwd1", dh1, x2d, rstd1, g_norm_mix, sc1, dx1)

    dmod = jnp.concatenate([dsh1, dsc1, dgt1, dsh2, dsc2, dgt2], axis=-1)
    small_names = ["b_ada", "g_norm_mix", "b_conv", "w_rg_a", "b_rg_a", "w_rg_x", "b_rg_x", "lru_lambda",
                   "g_attn_out", "g_lru_out", "g_norm_mlp", "g_norm_final"]
    small_g = [dmod, dg_mix, dbconv, dwa, dba, dwx, dbx, dlam, dg_attn, dg_lru, dg_mlp, dg_final]
    small_w = [b_ada, g_norm_mix, b_conv, w_rg_a, b_rg_a, w_rg_x, b_rg_x, lru_lambda, g_attn_out, g_lru_out, g_norm_mlp, g_norm_final]
    small_m = [m_b_ada, m_g_norm_mix, m_b_conv, m_w_rg_a, m_b_rg_a, m_w_rg_x, m_b_rg_x, m_lru_lambda, m_g_attn_out, m_g_lru_out, m_g_norm_mlp, m_g_norm_final]
    small_v = [v_b_ada, v_g_norm_mix, v_b_conv, v_w_rg_a, v_b_rg_a, v_w_rg_x, v_b_rg_x, v_lru_lambda, v_g_attn_out, v_g_lru_out, v_g_norm_mlp, v_g_norm_final]
    extra_zero = [jnp.zeros_like(dwconv), jnp.zeros((LANES,), F32)]
    g_slab, g_sizes = _pack(small_g + [dwconv, loss_row])
    w_slab, _ = _pack(small_w + extra_zero)
    m_slab, _ = _pack(small_m + extra_zero)
    v_slab, _ = _pack(small_v + extra_zero)
    me_arr = jnp.reshape(me, (1,)).astype(jnp.int32)
    g_buf = _place_row("place_small_grads", me_arr, g_slab)
    sg_send, sg_recv, g_buf, tok = _split_start_inplace("sg_gather_start", g_buf, N_CHIP, _plan_gather_own, loss_row)

    def reduced_begin(tag, half, tok_):
        send, recv, half, land, tok_ = _split_start("rs_reduced_start_" + tag, half, half.shape, 1, _plan_whole, tok_)
        return (send, recv, half, land), tok_

    def reduced_end(tag, st, after):
        return _split_wait("rs_reduced_wait_" + tag, *st, _plan_whole, after)

    sw_mo, tok = reduced_begin("mo", rs_end(st_mo, tok), tok)
    sw_mi, tok = reduced_begin("mi", rs_end(st_mi, tok), tok)
    sw_out, tok = reduced_begin("out", rs_end(st_out, tok), tok)
    half_mo, got_mo = reduced_end("mo", sw_mo, tok)
    big = {"w_mlp_out": _adamw_halves("adamw_w_mlp_out", c_arr, w_mlp_out[0], m_w_mlp_out[0], v_w_mlp_out[0],
                                      half_mo, got_mo)}
    half_mi, got_mi = reduced_end("mi", sw_mi, big["w_mlp_out"][1])
    big["w_mlp_in"] = _adamw_halves("adamw_w_mlp_in", c_arr, w_mlp_in[0], m_w_mlp_in[0], v_w_mlp_in[0], half_mi, got_mi)
    half_out, got_out = reduced_end("out", sw_out, big["w_mlp_in"][1])
    big["w_out"] = _adamw_halves("adamw_w_out", c_arr, w_out[0], m_w_out[0], v_w_out[0], half_out, got_out)
    g_buf = _split_wait_inplace("sg_gather_wait", sg_send, sg_recv, g_buf, _plan_gather_own, big["w_out"][1])
    sg_send, sg_recv, g_buf, tok = _split_start_inplace("sg_pass_start", g_buf, N_CHIP - 1, _plan_gather_pass, tok)
    sw_in, tok = reduced_begin("in", rs_end(st_in, tok), tok)
    g_all = _split_wait_inplace("sg_pass_wait", sg_send, sg_recv, g_buf, _plan_gather_pass, tok)
    gs_slab, ds_slab, ms_slab, vs_slab = _small_reduce_adamw(g_all, w_slab, m_slab, v_slab)
    shapes = [w.shape for w in small_w] + [dwconv.shape, (LANES,)]
    gs = _unpack(gs_slab, g_sizes, shapes)
    ds = _unpack(ds_slab, g_sizes, shapes)
    ms = _unpack(ms_slab, g_sizes, shapes)
    vs = _unpack(vs_slab, g_sizes, shapes)
    small = {n: (gs[i], ds[i], ms[i], vs[i]) for i, n in enumerate(small_names)}
    loss = gs[-1][0]
    g_wconv = lax.dynamic_slice(gs[-2], (0, chip * n_cv), (CONV_TAPS, n_cv))
    d_wconv, m_wconv, v_wconv = _adamw_plain("adamw_conv", w_conv[0], g_wconv, m_w_conv[0], v_w_conv[0])
    small["w_conv"] = (g_wconv[None], d_wconv[None], m_wconv[None], v_wconv[None])

    dmod_all = g_all[:, :N_MOD * d // LANES, :].reshape(N_DEV, N_MOD * d)
    dmod_sel = lax.dynamic_slice(dmod_all, (0, chip * n_ada), (N_DEV, n_ada)).astype(BF16)
    act_t = act_all.T.astype(BF16)
    big["w_ada"] = _adamw_ada(w_ada[0], m_w_ada[0], v_w_ada[0], act_t, dmod_sel)

    half_in, got_in = reduced_end("in", sw_in, big["w_ada"][1])
    big["w_in"] = _adamw_halves("adamw_w_in", c_arr, w_in[0], m_w_in[0], v_w_in[0], half_in, got_in)

    order = ["w_ada", "b_ada", "g_norm_mix", "w_in", "w_conv", "b_conv", "w_rg_a", "b_rg_a", "w_rg_x", "b_rg_x",
             "lru_lambda", "g_attn_out", "g_lru_out", "w_out", "g_norm_mlp", "w_mlp_in", "w_mlp_out", "g_norm_final"]
    res = {}
    for n in order:
        res[n] = tuple(t[None] for t in big[n]) if n in big else small[n]
    return (loss, grad_x[None],
            *[res[n][0] for n in order], *[res[n][1] for n in order],
            *[res[n][2] for n in order], *[res[n][3] for n in order])
```

```python
import functools
import math

import jax
import jax.numpy as jnp
from jax import lax
from jax.experimental import pallas as pl
from jax.experimental.pallas import tpu as pltpu

F32 = jnp.float32
BF16 = jnp.bfloat16
SDS = jax.ShapeDtypeStruct
MESH = pl.DeviceIdType.MESH

EPS = 1e-6
HEAD = 128
N_MOD = 6
CONV_TAPS = 4
LRU_C = 8.0
ADAM_LR, ADAM_B1, ADAM_B2, ADAM_EPS, ADAM_WD, ADAM_STEP = 0.001, 0.9, 0.999, 1e-08, 0.01, 10
N_DEV = 8
N_CHIP = 4
LANES = 128
SUBLANES = 8
VMEM_LIMIT = 56 * 1024 * 1024
PACK_ROWS = 256
MM_TILE_M, MM_TILE_N, MM_TILE_K = 1024, 1024, 2048
ROW_TILE = 256
ROW_SPLIT = 1


def _tile(dim, pref):
    t = min(dim, pref)
    while dim % t:
        t -= LANES
    return t


def _params(sem=None):
    return pltpu.CompilerParams(dimension_semantics=sem, vmem_limit_bytes=VMEM_LIMIT)


def _sigmoid(x):
    return 1.0 / (1.0 + jnp.exp(-x))


def _log_sigmoid(x):
    return jnp.minimum(x, 0.0) - jnp.log(1.0 + jnp.exp(-jnp.abs(x)))


def _gelu_parts(x):
    k0, k1 = math.sqrt(2.0 / math.pi), 0.044715
    t = jnp.tanh(k0 * (x + k1 * x * x * x))
    val = 0.5 * x * (1.0 + t)
    der = 0.5 * (1.0 + t) + 0.5 * x * (1.0 - t * t) * k0 * (1.0 + 3.0 * k1 * x * x)
    return val, der


def _dot(a, b):
    return jnp.dot(a, b, preferred_element_type=F32)


def _dot_nt(a, b):
    return lax.dot_general(a, b, (((1,), (1,)), ((), ())), preferred_element_type=F32)


def _dot_tn(a, b):
    return lax.dot_general(a, b, (((0,), (0,)), ((), ())), preferred_element_type=F32)


def _split_dot(x, tri):
    hi = x.astype(BF16)
    lo = (x - hi.astype(F32)).astype(BF16)
    return _dot(hi, tri) + _dot(lo, tri)


def _matmul(name, a, b, mode, m, n, k, out_dtypes, *, b_off=0, extras=(), extra_kinds=(), epilogue=None,
            tm=MM_TILE_M, tn=MM_TILE_N, tk=MM_TILE_K, m_half=None):
    tm, tn, tk = _tile(m, tm), _tile(math.gcd(n, b_off) if b_off else n, tn), _tile(k, tk)
    assert b_off % tn == 0
    nk = k // tk
    n_ex, n_out = len(extras), len(out_dtypes)
    dot = {"nn": _dot, "nt": _dot_nt, "tn": _dot_tn}[mode]
    n_pre = 0 if m_half is None else 1
    m_out = m if m_half is None else m // 2

    def body(*refs):
        a_ref, b_ref, *rest = refs[n_pre:]
        ex, outs = rest[:n_ex], rest[n_ex:n_ex + n_out]

        def finish(total):
            res = epilogue(total, *[e[...] for e in ex]) if epilogue else (total,)
            for o, r in zip(outs, res):
                o[...] = r.astype(o.dtype)

        if nk == 1:
            finish(dot(a_ref[...], b_ref[...]))
            return
        acc = rest[-1]
        kk = pl.program_id(2)

        @pl.when(kk == 0)
        def _():
            acc[...] = dot(a_ref[...], b_ref[...])

        @pl.when(jnp.logical_and(kk > 0, kk < nk - 1))
        def _():
            acc[...] += dot(a_ref[...], b_ref[...])

        @pl.when(kk == nk - 1)
        def _():
            finish(acc[...] + dot(a_ref[...], b_ref[...]))

    if mode == "nn":
        a_spec = pl.BlockSpec((tm, tk), lambda i, j, kk, *_: (i, kk))
        b_spec = pl.BlockSpec((tk, tn), lambda i, j, kk, *_: (kk, j + b_off // tn))
    elif mode == "nt":
        a_spec = pl.BlockSpec((tm, tk), lambda i, j, kk, *_: (i, kk))
        b_spec = pl.BlockSpec((tn, tk), lambda i, j, kk, *_: (j, kk + b_off // tk))
    elif m_half is None:
        a_spec = pl.BlockSpec((tk, tm), lambda i, j, kk: (kk, i))
        b_spec = pl.BlockSpec((tk, tn), lambda i, j, kk: (kk, j))
    else:
        a_spec = pl.BlockSpec((tk, tm), lambda i, j, kk, h: (kk, 2 * i + h[0]))
        b_spec = pl.BlockSpec((tk, tn), lambda i, j, kk, h: (kk, j))
    tile_spec = pl.BlockSpec((tm, tn), lambda i, j, kk, *_: (i, j))
    row_spec = pl.BlockSpec((1, tn), lambda i, j, kk, *_: (0, j))
    in_specs = [a_spec, b_spec] + [tile_spec if kind == "tile" else row_spec for kind in extra_kinds]
    out_specs = tuple(tile_spec for _ in out_dtypes)
    out_shape = tuple(SDS((m_out, n), dt) for dt in out_dtypes)
    scratch = [pltpu.VMEM((tm, tn), F32)] if nk > 1 else []
    grid = (m_out // tm, n // tn, nk)
    semantics = _params(("parallel", "parallel", "arbitrary"))
    if m_half is None:
        return pl.pallas_call(body, grid=grid, in_specs=in_specs, out_specs=out_specs, out_shape=out_shape,
                              scratch_shapes=scratch, compiler_params=semantics, name=name)(a, b, *extras)
    assert mode == "tn" and (m // tm) % 2 == 0
    return pl.pallas_call(
        body,
        grid_spec=pltpu.PrefetchScalarGridSpec(num_scalar_prefetch=1, grid=grid, in_specs=in_specs, out_specs=out_specs,
                                               scratch_shapes=scratch),
        out_shape=out_shape, compiler_params=semantics, name=name)(m_half, a, b, *extras)


def _row_specs(s, d, tr):
    row = "row"
    vec = pl.BlockSpec((1, d), lambda i: (0, 0))
    col = pl.BlockSpec((tr, 1), lambda i: (i, 0))
    return row, vec, col


class _ColChunks:
    def __init__(self, refs):
        self.refs = refs

    def __getitem__(self, idx):
        return jnp.concatenate([r[...] for r in self.refs], axis=-1)


def _rows_call(name, body, grid, in_specs, out_specs, out_shape, semantics, args):
    in_x, args_x, groups = [], [], []
    for spec, arr in zip(in_specs, args):
        if isinstance(spec, str):
            rows, d = arr.shape
            tr, dc = rows // grid[0], d // ROW_SPLIT
            in_x += [pl.BlockSpec((tr, dc), functools.partial(lambda i, jj: (i, jj), jj=j)) for j in range(ROW_SPLIT)]
            args_x += [arr] * ROW_SPLIT
            groups.append(ROW_SPLIT)
        else:
            in_x.append(spec)
            args_x.append(arr)
            groups.append(1)
    out_x = [pl.BlockSpec((sh.shape[0] // grid[0], sh.shape[1]), lambda i: (i, 0)) if isinstance(spec, str) else spec
             for spec, sh in zip(out_specs, out_shape)]

    def wrapped(*refs):
        views, k = [], 0
        for g in groups:
            views.append(_ColChunks(refs[k:k + g]) if g > 1 else refs[k])
            k += g
        body(*views, *refs[k:])

    return pl.pallas_call(
        wrapped, grid=grid, in_specs=in_x, out_specs=tuple(out_x), out_shape=tuple(out_shape),
        compiler_params=_params(semantics), name=name)(*args_x)


def _norm_mod_fwd(name, x, g, sc, sh):
    s, d = x.shape
    tr = _tile(s, ROW_TILE)
    row, vec, col = _row_specs(s, d, tr)

    def body(x_ref, g_ref, sc_ref, sh_ref, h_ref, r_ref):
        xv = x_ref[...]
        r = lax.rsqrt(jnp.mean(xv * xv, axis=-1, keepdims=True) + EPS)
        h_ref[...] = ((xv * r * g_ref[...]) * (1.0 + sc_ref[...]) + sh_ref[...]).astype(BF16)
        r_ref[...] = r

    return _rows_call(name, body, (s // tr,), [row, vec, vec, vec], (row, col),
                      (SDS((s, d), BF16), SDS((s, 1), F32)), ("parallel",), (x, g, sc, sh))


def _norm_mod_bwd(name, dh, xin, rstd, g, sc, dres, gate=None):
    s, d = xin.shape
    tr = _tile(s, ROW_TILE)
    row, vec, col = _row_specs(s, d, tr)

    n_gate = 2 if gate is not None else 0

    def body(dh_ref, x_ref, r_ref, g_ref, sc_ref, dres_ref, *rest):
        gate_in, gate_out = rest[:n_gate], rest[n_gate + 4:]
        dx_ref, dsh_ref, dsc_ref, dg_ref = rest[n_gate:n_gate + 4]

        @pl.when(pl.program_id(0) == 0)
        def _():
            for ref in (dsh_ref, dsc_ref, dg_ref) + tuple(gate_out[1:]):
                ref[...] = jnp.zeros_like(ref)

        dh_v, xv, r, gv = dh_ref[...].astype(F32), x_ref[...], r_ref[...], g_ref[...]
        n0 = xv * r
        dsh_ref[...] += jnp.sum(dh_v, axis=0, keepdims=True)
        dsc_ref[...] += jnp.sum(dh_v * (n0 * gv), axis=0, keepdims=True)
        dn = dh_v * (1.0 + sc_ref[...])
        dg_ref[...] += jnp.sum(dn * n0, axis=0, keepdims=True)
        gy = dn * gv
        dot = jnp.mean(gy * xv, axis=-1, keepdims=True)
        dxv = dres_ref[...] + r * gy - xv * (r * r * r * dot)
        dx_ref[...] = dxv
        if gate is not None:
            y_ref, gt_ref = gate_in
            dy_ref, dgt_ref = gate_out
            dy_ref[...] = (gt_ref[...] * dxv).astype(BF16)
            dgt_ref[...] += jnp.sum(dxv * y_ref[...], axis=0, keepdims=True)

    vecs = SDS((1, d), F32)
    gate_args = tuple(gate) if gate is not None else ()
    return _rows_call(
        name, body, (s // tr,),
        [row, row, col, vec, vec, row] + ([row, vec] if gate is not None else []),
        (row, vec, vec, vec) + ((row, vec) if gate is not None else ()),
        (SDS((s, d), F32), vecs, vecs, vecs) + ((SDS((s, d), BF16), vecs) if gate is not None else ()),
        ("arbitrary",), (dh, xin, rstd, g, sc, dres, *gate_args))


def _final_loss(x2, gf, tgt, y, gt):
    s, d = x2.shape
    tr = _tile(s, ROW_TILE)
    row, vec, _ = _row_specs(s, d, tr)
    lrow = pl.BlockSpec((1, LANES), lambda i: (0, 0))

    def body(x_ref, g_ref, t_ref, y_ref, gt_ref, dx_ref, loss_ref, dg_ref, dy_ref, dgt_ref):
        @pl.when(pl.program_id(0) == 0)
        def _():
            loss_ref[...] = jnp.zeros_like(loss_ref)
            dg_ref[...] = jnp.zeros_like(dg_ref)
            dgt_ref[...] = jnp.zeros_like(dgt_ref)

        xv, gv = x_ref[...], g_ref[...]
        r = lax.rsqrt(jnp.mean(xv * xv, axis=-1, keepdims=True) + EPS)
        n0 = xv * r
        err = n0 * gv - t_ref[...]
        loss_ref[...] += jnp.sum(err * err) * (0.5 / d)
        dy = err * (1.0 / d)
        dg_ref[...] += jnp.sum(dy * n0, axis=0, keepdims=True)
        gy = dy * gv
        dot = jnp.mean(gy * xv, axis=-1, keepdims=True)
        dxv = r * gy - xv * (r * r * r * dot)
        dx_ref[...] = dxv
        dy_ref[...] = (gt_ref[...] * dxv).astype(BF16)
        dgt_ref[...] += jnp.sum(dxv * y_ref[...], axis=0, keepdims=True)

    return _rows_call(
        "final_loss", body, (s // tr,), [row, vec, row, row, vec], (row, lrow, vec, row, vec),
        (SDS((s, d), F32), SDS((1, LANES), F32), SDS((1, d), F32), SDS((s, d), BF16), SDS((1, d), F32)),
        ("arbitrary",), (x2, gf, tgt, y, gt))


def _mix_norm_fwd(oa, ol, ga, gl):
    s, w = oa.shape
    tr = _tile(s, ROW_TILE)
    row, vec, col = _row_specs(s, w, tr)

    def body(oa_ref, ol_ref, ga_ref, gl_ref, mx_ref, ra_ref, rl_ref):
        a, l = oa_ref[...], ol_ref[...]
        ra = lax.rsqrt(jnp.mean(a * a, axis=-1, keepdims=True) + EPS)
        rl = lax.rsqrt(jnp.mean(l * l, axis=-1, keepdims=True) + EPS)
        mx_ref[:, :w] = (a * ra * ga_ref[...]).astype(BF16)
        mx_ref[:, w:] = (l * rl * gl_ref[...]).astype(BF16)
        ra_ref[...] = ra
        rl_ref[...] = rl

    return _rows_call(
        "mix_norm_fwd", body, (s // tr,), [row, row, vec, vec], (row, col, col),
        (SDS((s, 2 * w), BF16), SDS((s, 1), F32), SDS((s, 1), F32)), ("parallel",), (oa, ol, ga, gl))


def _mix_norm_bwd(dmx, oa, ol, ra, rl, ga, gl):
    s, w = oa.shape
    tr = _tile(s, ROW_TILE)
    row, vec, col = _row_specs(s, w, tr)

    def body(dm_ref, oa_ref, ol_ref, ra_ref, rl_ref, ga_ref, gl_ref, doa_ref, dol_ref, dga_ref, dgl_ref):
        @pl.when(pl.program_id(0) == 0)
        def _():
            dga_ref[...] = jnp.zeros_like(dga_ref)
            dgl_ref[...] = jnp.zeros_like(dgl_ref)

        def one(dy, xv, r, gv, dg_ref):
            dg_ref[...] += jnp.sum(dy * (xv * r), axis=0, keepdims=True)
            gy = dy * gv
            dot = jnp.mean(gy * xv, axis=-1, keepdims=True)
            return r * gy - xv * (r * r * r * dot)

        dm = dm_ref[...].astype(F32)
        doa_ref[...] = one(dm[:, :w], oa_ref[...], ra_ref[...], ga_ref[...], dga_ref).astype(BF16)
        dol_ref[...] = one(dm[:, w:], ol_ref[...], rl_ref[...], gl_ref[...], dgl_ref)

    return _rows_call(
        "mix_norm_bwd", body, (s // tr,), [row, row, row, col, col, vec, vec], (row, row, vec, vec),
        (SDS((s, w), BF16), SDS((s, w), F32), SDS((1, w), F32), SDS((1, w), F32)), ("arbitrary",),
        (dmx, oa, ol, ra, rl, ga, gl))


def _attn_blocks(qs, ks, tri_after, csums, causal):
    zs = [_dot_nt(q, k) * (HEAD ** -0.5) for q, k in zip(qs, ks)]
    lbs = [_log_sigmoid(z) for z in zs]
    lss = [lb - z for lb, z in zip(lbs, zs)]
    if causal is not None:
        lss = [jnp.where(causal, ls, 0.0) for ls in lss]
    locs = [_split_dot(ls, tri_after) for ls in lss]
    ws = [jnp.exp(lb + (loc + cs)) for lb, loc, cs in zip(lbs, locs, csums)]
    if causal is not None:
        ws = [jnp.where(causal, w, 0.0) for w in ws]
    nxt = [cs + (loc[:, 0:1] + ls[:, 0:1]) for cs, loc, ls in zip(csums, locs, lss)]
    return lbs, ws, nxt


ATTN_HEADS_PER_STEP = 4


def _attn_tile(s):
    return 256 if s >= 1024 else 128


def _tri(t, after):
    r_i = lax.broadcasted_iota(jnp.int32, (t, t), 0)
    c_i = lax.broadcasted_iota(jnp.int32, (t, t), 1)
    return ((r_i > c_i) if after else (r_i < c_i)).astype(BF16)


def _attn_fwd(qkv, n_heads):
    s = qkv.shape[0]
    t = _attn_tile(s)
    hps = ATTN_HEADS_PER_STEP
    wid = hps * HEAD

    nq = s // t

    def body(q_ref, k_ref, v_ref, o_ref, w_ref, sg_ref):
        qi = pl.program_id(1)
        tri_after = _tri(t, True)
        causal = lax.broadcasted_iota(jnp.int32, (t, t), 1) < lax.broadcasted_iota(jnp.int32, (t, t), 0)
        lanes = [slice(a * HEAD, (a + 1) * HEAD) for a in range(hps)]
        qs = [q_ref[:, ln] for ln in lanes]

        def block(kb, carry, mask):
            off = pl.multiple_of(kb * t, t)
            ks = [k_ref[pl.ds(off, t), ln] for ln in lanes]
            lbs, ws, csums = _attn_blocks(qs, ks, tri_after, [cr[0] for cr in carry], mask)
            wbs = [w.astype(BF16) for w in ws]
            for a in range(hps):
                w_ref[a, kb] = wbs[a]
                sg_ref[a, kb] = jnp.exp(lbs[a]).astype(BF16)
            os_ = [cr[1] + _dot(wb, v_ref[pl.ds(off, t), ln]) for cr, wb, ln in zip(carry, wbs, lanes)]
            return tuple(zip(csums, os_))

        zero = tuple((jnp.zeros((t, 1), F32), jnp.zeros((t, HEAD), F32)) for _ in lanes)
        carry = block(qi, zero, causal)
        carry = lax.fori_loop(1, qi + 1, lambda it, cr: block(qi - it, cr, None), carry)
        for a, ln in enumerate(lanes):
            o_ref[:, ln] = carry[a][1]

    hb = n_heads // hps
    kept = pl.BlockSpec((None, hps, nq, t, t), lambda hh, i: (hh * nq + i, 0, 0, 0, 0))
    kept_shape = SDS((hb * nq, hps, nq, t, t), BF16)
    return pl.pallas_call(
        body, grid=(hb, nq),
        in_specs=[pl.BlockSpec((t, wid), lambda hh, i: (i, hh)),
                  pl.BlockSpec((s, wid), lambda hh, i: (0, hb + hh)),
                  pl.BlockSpec((s, wid), lambda hh, i: (0, 2 * hb + hh))],
        out_specs=(pl.BlockSpec((t, wid), lambda hh, i: (i, hh)), kept, kept),
        out_shape=(SDS((s, n_heads * HEAD), F32), kept_shape, kept_shape),
        compiler_params=_params(("parallel", "parallel")), name="attn_fwd")(qkv, qkv, qkv)


def _attn_bwd(qkv, do, w_kept, sg_kept, n_heads):
    s = qkv.shape[0]
    t = _attn_tile(s)
    nq = s // t
    scale = HEAD ** -0.5
    hps = ATTN_HEADS_PER_STEP
    wid = hps * HEAD

    def body(q_ref, k_ref, v_ref, do_ref, w_ref, sg_ref, dq_ref, dk_ref, dv_ref, dk_acc, dv_acc):
        qi = pl.program_id(1)

        @pl.when(qi == 0)
        def _():
            dk_acc[...] = jnp.zeros_like(dk_acc)
            dv_acc[...] = jnp.zeros_like(dv_acc)

        tri_before = _tri(t, False)
        causal = lax.broadcasted_iota(jnp.int32, (t, t), 1) < lax.broadcasted_iota(jnp.int32, (t, t), 0)
        lanes = [slice(a * HEAD, (a + 1) * HEAD) for a in range(hps)]
        qs = [q_ref[:, ln] for ln in lanes]
        douts = [do_ref[:, ln] for ln in lanes]

        def block(kb, carry, mask):
            off = pl.multiple_of(kb * t, t)
            wbs = [w_ref[a, kb] for a in range(hps)]
            dws = [_dot_nt(dout, v_ref[pl.ds(off, t), ln]) for dout, ln in zip(douts, lanes)]
            for a, ln in enumerate(lanes):
                dv_acc[pl.ds(off, t), ln] += _dot_tn(wbs[a], douts[a])
            es = [dw * wb.astype(F32) for dw, wb in zip(dws, wbs)]
            locs = [_split_dot(e, tri_before) for e in es]
            sgs = [sg_ref[a, kb].astype(F32) for a in range(hps)]
            stays = [(loc + cr[0]) * sg for loc, cr, sg in zip(locs, carry, sgs)]
            if mask is not None:
                stays = [jnp.where(mask, st, 0.0) for st in stays]
            dzbs = [((e * (1.0 - sg) - st) * scale).astype(BF16) for e, sg, st in zip(es, sgs, stays)]
            dqs = [cr[1] + _dot(dzb, k_ref[pl.ds(off, t), ln]) for cr, dzb, ln in zip(carry, dzbs, lanes)]
            for a, ln in enumerate(lanes):
                dk_acc[pl.ds(off, t), ln] += _dot_tn(dzbs[a], qs[a])
            esums = [cr[0] + (loc[:, t - 1:t] + e[:, t - 1:t]) for cr, loc, e in zip(carry, locs, es)]
            return tuple(zip(esums, dqs))

        zero = tuple((jnp.zeros((t, 1), F32), jnp.zeros((t, HEAD), F32)) for _ in lanes)
        carry = lax.fori_loop(0, qi, lambda kb, cr: block(kb, cr, None), zero)
        carry = block(qi, carry, causal)
        for a, ln in enumerate(lanes):
            dq_ref[:, ln] = carry[a][1].astype(BF16)

        @pl.when(qi == nq - 1)
        def _():
            dk_ref[...] = dk_acc[...].astype(BF16)
            dv_ref[...] = dv_acc[...].astype(BF16)

    hb = n_heads // hps
    blk = pl.BlockSpec((t, wid), lambda hh, i: (i, hh))
    full = pl.BlockSpec((s, wid), lambda hh, i: (0, hh))
    kept = pl.BlockSpec((None, hps, nq, t, t), lambda hh, i: (hh * nq + i, 0, 0, 0, 0))
    return pl.pallas_call(
        body, grid=(hb, nq),
        in_specs=[blk,
                  pl.BlockSpec((s, wid), lambda hh, i: (0, hb + hh)),
                  pl.BlockSpec((s, wid), lambda hh, i: (0, 2 * hb + hh)),
                  blk, kept, kept],
        out_specs=(blk, full, full),
        out_shape=(SDS((s, n_heads * HEAD), BF16),) * 3,
        scratch_shapes=[pltpu.VMEM((s, wid), F32), pltpu.VMEM((s, wid), F32)],
        compiler_params=_params(("parallel", "arbitrary")), name="attn_bwd")(qkv, qkv, qkv, do, w_kept, sg_kept)


def _lru_chunk(s):
    return 128 if s >= 256 else s // 2


def _lru_gates(xc, wa, ba, wx, bx, sp):
    xb = xc.astype(BF16)
    r = _sigmoid(_dot(xb, wa) + ba)
    ig = _sigmoid(_dot(xb, wx) + bx)
    la = -LRU_C * r * sp
    a = jnp.exp(la)
    t = jnp.tanh(la)
    mult = jnp.sqrt(-2.0 * t / (1.0 - t))
    return r, ig, a, mult


def _softplus_neg(lam):
    return jnp.maximum(-lam, 0.0) + jnp.log(1.0 + jnp.exp(-jnp.abs(lam)))


LRU_BLOCKS_PER_STEP = 1
SCAN_GROUP = 4


def _lru_specs(s, n_blocks):
    bps = min(LRU_BLOCKS_PER_STEP, n_blocks)
    wid = bps * HEAD
    seq0 = pl.BlockSpec((s, wid), lambda h: (0, h))
    seq1 = pl.BlockSpec((s, wid), lambda h: (0, n_blocks // bps + h))
    taps = pl.BlockSpec((CONV_TAPS, wid), lambda h: (0, h))
    vec = pl.BlockSpec((1, wid), lambda h: (0, h))
    mat = pl.BlockSpec((bps, HEAD, HEAD), lambda h: (h, 0, 0))
    return bps, seq0, seq1, taps, vec, mat


def _per_block(one_block, n_2d, n_mat_pos, bps):
    def body(*refs):
        for a in range(bps):
            views = [r.at[a] if i in n_mat_pos else r.at[:, pl.ds(a * HEAD, HEAD)] for i, r in enumerate(refs[:n_2d])]
            one_block(*views, *refs[n_2d:])
    return body


def _lru_fwd(xrg, wconv, bconv, wa, ba, wx, bx, lam):
    s = xrg.shape[0]
    nb = wa.shape[0]
    tc = _lru_chunk(s)
    bps, seq0, seq1, taps, vec, mat = _lru_specs(s, nb)
    pad = SUBLANES

    def one_block(xr_ref, xg_ref, wc_ref, bc_ref, wa_ref, ba_ref, wx_ref, bx_ref, lam_ref, o_ref, h_ref, pad_s, a_s, u_s):
        pad_s[0:pad, :] = jnp.zeros((pad, HEAD), F32)
        pad_s[pad:pad + s, :] = xr_ref[...]
        wab, wxb = wa_ref[...].astype(BF16), wx_ref[...].astype(BF16)
        sp = _softplus_neg(lam_ref[...])
        for c in range(s // tc):
            base = c * tc
            xc = bc_ref[...] + sum(wc_ref[i:i + 1, :] * pad_s[pl.ds(base + pad - (CONV_TAPS - 1) + i, tc), :]
                                   for i in range(CONV_TAPS))
            _, ig, a, mult = _lru_gates(xc, wab, ba_ref[...], wxb, bx_ref[...], sp)
            a_s[base:base + tc, :] = a
            u_s[base:base + tc, :] = mult * (ig * xc)

        row = lax.broadcasted_iota(jnp.int32, (SUBLANES, HEAD), 0)
        last = SUBLANES - 1

        def group(gi, hprev):
            offs = [pl.multiple_of((gi * SCAN_GROUP + q) * SUBLANES, SUBLANES) for q in range(SCAN_GROUP)]
            ab = []
            for off in offs:
                a8, b8 = a_s[pl.ds(off, SUBLANES), :], u_s[pl.ds(off, SUBLANES), :]
                for d in (1, 2, 4):
                    a_sh = jnp.where(row < d, 1.0, pltpu.roll(a8, d, 0))
                    b_sh = jnp.where(row < d, 0.0, pltpu.roll(b8, d, 0))
                    b8 = a8 * b_sh + b8
                    a8 = a8 * a_sh
                ab.append((a8, b8))
            enters = []
            for a8, b8 in ab:
                enters.append(hprev)
                hprev = a8[last:, :] * hprev + b8[last:, :]
            for off, (a8, b8), h0 in zip(offs, ab, enters):
                h_ref[pl.ds(off, SUBLANES), :] = a8 * h0 + b8
            return hprev

        lax.fori_loop(0, s // (SUBLANES * SCAN_GROUP), group, jnp.zeros((1, HEAD), F32))
        for c in range(s // tc):
            sl = slice(c * tc, (c + 1) * tc)
            gel, _ = _gelu_parts(xg_ref[sl, :])
            o_ref[sl, :] = h_ref[sl, :] * gel

    return pl.pallas_call(
        _per_block(one_block, 11, (4, 6), bps), grid=(nb // bps,),
        in_specs=[seq0, seq1, taps, vec, mat, vec, mat, vec, vec],
        out_specs=(seq0, seq0),
        out_shape=(SDS((s, nb * HEAD), F32), SDS((s, nb * HEAD), F32)),
        scratch_shapes=[pltpu.VMEM((s + pad, HEAD), F32), pltpu.VMEM((s, HEAD), F32), pltpu.VMEM((s, HEAD), F32)],
        compiler_params=_params(("parallel",)), name="lru_fwd")(xrg, xrg, wconv, bconv, wa, ba, wx, bx, lam)


def _lru_bwd(xrg, dol, hseq, wconv, bconv, wa, ba, wx, bx, lam):
    s = xrg.shape[0]
    nb = wa.shape[0]
    tc = _lru_chunk(s)
    bps, seq0, seq1, taps, vec, mat = _lru_specs(s, nb)
    pad = SUBLANES

    def one_block(xr_ref, xg_ref, do_ref, h_ref, wc_ref, bc_ref, wa_ref, ba_ref, wx_ref, bx_ref, lam_ref,
             dxr_ref, dxg_ref, dwc_ref, dbc_ref, dwa_ref, dba_ref, dwx_ref, dbx_ref, dlam_ref,
             pad_s, hp_s, a_s, g_s, da_s, dxc_s, xc_s, r_s, ig_s, mult_s):
        pad_s[0:pad, :] = jnp.zeros((pad, HEAD), F32)
        pad_s[pad:pad + s, :] = xr_ref[...]
        hp_s[0:pad, :] = jnp.zeros((pad, HEAD), F32)
        hp_s[pad:pad + s, :] = h_ref[...]
        a_s[s:s + pad, :] = jnp.zeros((pad, HEAD), F32)
        dxc_s[s:s + pad, :] = jnp.zeros((pad, HEAD), F32)
        wab, wxb = wa_ref[...].astype(BF16), wx_ref[...].astype(BF16)
        lam_v = lam_ref[...]
        sp = _softplus_neg(lam_v)

        def conv_in(c):
            base = c * tc
            wins = [pad_s[pl.ds(base + pad - (CONV_TAPS - 1) + i, tc), :] for i in range(CONV_TAPS)]
            xc = bc_ref[...] + sum(wc_ref[i:i + 1, :] * wins[i] for i in range(CONV_TAPS))
            return xc, wins

        for c in range(s // tc):
            sl = slice(c * tc, (c + 1) * tc)
            xc, _ = conv_in(c)
            r, ig, a, mult = _lru_gates(xc, wab, ba_ref[...], wxb, bx_ref[...], sp)
            a_s[sl, :] = a
            xc_s[sl, :], r_s[sl, :], ig_s[sl, :], mult_s[sl, :] = xc, r, ig, mult
            gel, dgel = _gelu_parts(xg_ref[sl, :])
            dov = do_ref[sl, :]
            g_s[sl, :] = dov * gel
            dxg_ref[sl, :] = (dov * h_ref[sl, :] * dgel).astype(BF16)

        row = lax.broadcasted_iota(jnp.int32, (SUBLANES, HEAD), 0)
        n_chunks = s // SUBLANES

        def group(it, gnext):
            offs = [pl.multiple_of((n_chunks - 1 - (it * SCAN_GROUP + q)) * SUBLANES, SUBLANES) for q in range(SCAN_GROUP)]
            cg = []
            for off in offs:
                a8 = a_s[pl.ds(off, SUBLANES), :]
                a8n = a_s[pl.ds(off + SUBLANES, SUBLANES), :]
                c8 = pltpu.roll(jnp.where(row == 0, a8n, a8), SUBLANES - 1, 0)
                g8 = g_s[pl.ds(off, SUBLANES), :]
                for d in (1, 2, 4):
                    c_sh = jnp.where(row >= SUBLANES - d, 1.0, pltpu.roll(c8, SUBLANES - d, 0))
                    g_sh = jnp.where(row >= SUBLANES - d, 0.0, pltpu.roll(g8, SUBLANES - d, 0))
                    g8 = c8 * g_sh + g8
                    c8 = c8 * c_sh
                cg.append((c8, g8))
            enters = []
            for c8, g8 in cg:
                enters.append(gnext)
                gnext = g8[0:1, :] + c8[0:1, :] * gnext
            for off, (c8, g8), g0 in zip(offs, cg, enters):
                gv = g8 + c8 * g0
                g_s[pl.ds(off, SUBLANES), :] = gv
                h8 = hp_s[pl.ds(off + pad, SUBLANES), :]
                h8p = hp_s[pl.ds(off, SUBLANES), :]
                da_s[pl.ds(off, SUBLANES), :] = gv * pltpu.roll(jnp.where(row == SUBLANES - 1, h8p, h8), 1, 0)
            return gnext

        lax.fori_loop(0, n_chunks // SCAN_GROUP, group, jnp.zeros((1, HEAD), F32))

        dsp = jnp.zeros((1, HEAD), F32)
        dbc = jnp.zeros((1, HEAD), F32)
        dba = jnp.zeros((1, HEAD), F32)
        dbx = jnp.zeros((1, HEAD), F32)
        dwa = jnp.zeros((HEAD, HEAD), F32)
        dwx = jnp.zeros((HEAD, HEAD), F32)
        dwc = [jnp.zeros((1, HEAD), F32) for _ in range(CONV_TAPS)]
        for c in range(s // tc):
            sl = slice(c * tc, (c + 1) * tc)
            wins = [pad_s[pl.ds(c * tc + pad - (CONV_TAPS - 1) + i, tc), :] for i in range(CONV_TAPS)]
            xc, r, ig, a, mult = xc_s[sl, :], r_s[sl, :], ig_s[sl, :], a_s[sl, :], mult_s[sl, :]
            du, da = g_s[sl, :], da_s[sl, :]
            d_ix = du * mult
            dla = da * a - (du * ig * xc) * (a * a / mult)
            dsp = dsp + jnp.sum(dla * r, axis=0, keepdims=True) * (-LRU_C)
            dpa = (dla * (-LRU_C * sp)) * r * (1.0 - r)
            dpx = (d_ix * xc) * ig * (1.0 - ig)
            dpab, dpxb, xb = dpa.astype(BF16), dpx.astype(BF16), xc.astype(BF16)
            dxc = d_ix * ig + _dot_nt(dpab, wab) + _dot_nt(dpxb, wxb)
            dwa = dwa + _dot_tn(xb, dpab)
            dwx = dwx + _dot_tn(xb, dpxb)
            dba = dba + jnp.sum(dpa, axis=0, keepdims=True)
            dbx = dbx + jnp.sum(dpx, axis=0, keepdims=True)
            dbc = dbc + jnp.sum(dxc, axis=0, keepdims=True)
            for i in range(CONV_TAPS):
                dwc[i] = dwc[i] + jnp.sum(dxc * wins[i], axis=0, keepdims=True)
            dxc_s[sl, :] = dxc

        for c in range(s // tc):
            base = c * tc
            dxr = sum(wc_ref[i:i + 1, :] * dxc_s[pl.ds(base + (CONV_TAPS - 1) - i, tc), :] for i in range(CONV_TAPS))
            dxr_ref[base:base + tc, :] = dxr.astype(BF16)

        for i in range(CONV_TAPS):
            dwc_ref[i:i + 1, :] = dwc[i]
        dbc_ref[...] = dbc
        dwa_ref[...] = dwa
        dwx_ref[...] = dwx
        dba_ref[...] = dba
        dbx_ref[...] = dbx
        dlam_ref[...] = dsp * (-_sigmoid(-lam_v))

    w = nb * HEAD
    return pl.pallas_call(
        _per_block(one_block, 20, (6, 8, 15, 17), bps), grid=(nb // bps,),
        in_specs=[seq0, seq1, seq0, seq0, taps, vec, mat, vec, mat, vec, vec],
        out_specs=(seq0, seq0, taps, vec, mat, vec, mat, vec, vec),
        out_shape=(SDS((s, w), BF16), SDS((s, w), BF16), SDS((CONV_TAPS, w), F32), SDS((1, w), F32),
                   SDS((nb, HEAD, HEAD), F32), SDS((1, w), F32), SDS((nb, HEAD, HEAD), F32), SDS((1, w), F32),
                   SDS((1, w), F32)),
        scratch_shapes=[pltpu.VMEM((s + pad, HEAD), F32), pltpu.VMEM((s + pad, HEAD), F32),
                        pltpu.VMEM((s + pad, HEAD), F32), pltpu.VMEM((s, HEAD), F32),
                        pltpu.VMEM((s, HEAD), F32), pltpu.VMEM((s + pad, HEAD), F32)]
                       + [pltpu.VMEM((s, HEAD), F32)] * 4,
        compiler_params=_params(("parallel",)), name="lru_bwd",
    )(xrg, xrg, dol, hseq, wconv, bconv, wa, ba, wx, bx, lam)


def _ada_mod(c_all, w_sh, b_sh):
    n_ex, d = c_all.shape
    n = w_sh.shape[1]
    tn = _tile(n, 512)

    def body(c_ref, w_ref, b_ref, mod_ref, act_ref):
        cv = c_ref[...]
        act = cv * _sigmoid(cv)
        act_ref[...] = act
        mod_ref[...] = _dot(act.astype(BF16), w_ref[...].astype(BF16)) + b_ref[...]

    return pl.pallas_call(
        body, grid=(n // tn,),
        in_specs=[pl.BlockSpec((n_ex, d), lambda j: (0, 0)), pl.BlockSpec((d, tn), lambda j: (0, j)),
                  pl.BlockSpec((1, tn), lambda j: (0, j))],
        out_specs=(pl.BlockSpec((n_ex, tn), lambda j: (0, j)), pl.BlockSpec((n_ex, d), lambda j: (0, 0))),
        out_shape=(SDS((n_ex, n), F32), SDS((n_ex, d), F32)),
        compiler_params=_params(("arbitrary",)), name="ada_mod")(c_all, w_sh, b_sh)


def _adamw_math(w, g, m, v):
    m = ADAM_B1 * m + (1.0 - ADAM_B1) * g
    v = ADAM_B2 * v + (1.0 - ADAM_B2) * (g * g)
    m_hat = m / (1.0 - ADAM_B1 ** ADAM_STEP)
    v_hat = v / (1.0 - ADAM_B2 ** ADAM_STEP)
    delta = -ADAM_LR * (m_hat / (jnp.sqrt(v_hat) + ADAM_EPS) + ADAM_WD * w)
    return delta, m, v


def _adamw_plain(name, w, g, m, v):
    def body(w_ref, g_ref, m_ref, v_ref, d_ref, mo_ref, vo_ref):
        d_ref[...], mo_ref[...], vo_ref[...] = _adamw_math(w_ref[...], g_ref[...], m_ref[...], v_ref[...])

    return pl.pallas_call(body, out_shape=(SDS(w.shape, F32),) * 3, name=name)(w, g, m, v)


def _adamw_halves(name, c_arr, w, m, v, g_own, g_recv):
    r, n = w.shape
    rh = r // 2
    tr = _tile(rh, 256)
    nh = rh // tr

    def body(c_ref, w_ref, m_ref, v_ref, go_ref, gr_ref, g_ref, d_ref, mo_ref, vo_ref):
        own = (pl.program_id(0) // nh) == c_ref[0]
        g = jnp.where(own, go_ref[...], gr_ref[...])
        g_ref[...] = g
        d_ref[...], mo_ref[...], vo_ref[...] = _adamw_math(w_ref[...], g, m_ref[...], v_ref[...])

    full = pl.BlockSpec((tr, n), lambda i, c: (i, 0))
    own = pl.BlockSpec((tr, n), lambda i, c: (jnp.where(i // nh == c[0], i % nh, 0), 0))
    recv = pl.BlockSpec((tr, n), lambda i, c: (jnp.where(i // nh == c[0], 0, i % nh), 0))
    return pl.pallas_call(
        body,
        grid_spec=pltpu.PrefetchScalarGridSpec(
            num_scalar_prefetch=1, grid=(2 * nh,), in_specs=[full, full, full, own, recv],
            out_specs=(full,) * 4),
        out_shape=(SDS((r, n), F32),) * 4,
        compiler_params=_params(("parallel",)), name=name)(c_arr, w, m, v, g_own, g_recv)


def _adamw_ada(w, m, v, act_t, dmod):
    d, n = w.shape
    n_ex = act_t.shape[1]
    tr = _tile(d, 256)

    def body(a_ref, dm_ref, w_ref, m_ref, v_ref, g_ref, d_ref, mo_ref, vo_ref):
        g = _dot(a_ref[...], dm_ref[...])
        g_ref[...] = g
        d_ref[...], mo_ref[...], vo_ref[...] = _adamw_math(w_ref[...], g, m_ref[...], v_ref[...])

    full = pl.BlockSpec((tr, n), lambda i: (i, 0))
    return pl.pallas_call(
        body, grid=(d // tr,),
        in_specs=[pl.BlockSpec((tr, n_ex), lambda i: (i, 0)), pl.BlockSpec((n_ex, n), lambda i: (0, 0)), full, full, full],
        out_specs=(full,) * 4, out_shape=(SDS((d, n), F32),) * 4,
        compiler_params=_params(("parallel",)), name="adamw_ada")(act_t, dmod, w, m, v)


def _small_reduce_adamw(parts, w, m, v):
    n_dev, r, _ = parts.shape
    tr = r if r <= PACK_ROWS else PACK_ROWS

    def body(p_ref, w_ref, m_ref, v_ref, g_ref, d_ref, mo_ref, vo_ref):
        g = p_ref[0]
        for k in range(1, n_dev):
            g = g + p_ref[k]
        g_ref[...] = g
        d_ref[...], mo_ref[...], vo_ref[...] = _adamw_math(w_ref[...], g, m_ref[...], v_ref[...])

    full = pl.BlockSpec((tr, LANES), lambda i: (i, 0))
    return pl.pallas_call(
        body, grid=(r // tr,),
        in_specs=[pl.BlockSpec((n_dev, tr, LANES), lambda i: (0, i, 0)), full, full, full],
        out_specs=(full,) * 4, out_shape=(SDS((r, LANES), F32),) * 4,
        compiler_params=_params(("parallel",)), name="small_reduce_adamw")(parts, w, m, v)


def _mesh_pos():
    return lax.axis_index("x"), lax.axis_index("y"), lax.axis_index("c")


def _other_chips(x, y):
    return [(1 - x, y), (x, 1 - y), (1 - x, 1 - y)]


def _all_gather_small(name, blk, after=()):
    r, n = blk.shape
    n_after = len(after)

    def body(x_ref, *rest):
        out_ref, send_sems, recv_sems, local_sem = rest[n_after:]
        x, y, c = _mesh_pos()
        me, sibling = (x, y, c), (x, y, 1 - c)
        chips = _other_chips(x, y)

        def rows(px, py, pc):
            return out_ref.at[4 * px + 2 * py + pc]

        def copy(k, block, to, src=None):
            return pltpu.make_async_remote_copy(
                src_ref=rows(*block) if src is None else src, dst_ref=rows(*block),
                send_sem=send_sems.at[k], recv_sem=recv_sems.at[k], device_id=to, device_id_type=MESH)

        mine = pltpu.make_async_copy(x_ref, rows(*me), local_sem)
        mine.start()
        first = [copy(0, me, sibling, src=x_ref)]
        first += [copy(1 + j, me, (*chip, c), src=x_ref) for j, chip in enumerate(chips)]
        for cp in first:
            cp.start()
        passed = [copy(4 + j, (*chip, c), sibling) for j, chip in enumerate(chips)]
        for j, chip in enumerate(chips):
            copy(1 + j, (*chip, c), me).wait_recv()
            passed[j].start()
        copy(0, sibling, me).wait_recv()
        for j, chip in enumerate(chips):
            copy(4 + j, (*chip, 1 - c), me).wait_recv()
        for cp in first + passed:
            cp.wait_send()
        mine.wait()

    return pl.pallas_call(
        body, out_shape=SDS((N_DEV, r, n), blk.dtype),
        in_specs=[pl.BlockSpec(memory_space=pltpu.VMEM)] + [pl.BlockSpec(memory_space=pl.ANY)] * n_after,
        out_specs=pl.BlockSpec(memory_space=pltpu.VMEM),
        scratch_shapes=[pltpu.SemaphoreType.DMA((7,)), pltpu.SemaphoreType.DMA((7,)), pltpu.SemaphoreType.DMA],
        compiler_params=pltpu.CompilerParams(vmem_limit_bytes=VMEM_LIMIT), name=name)(blk, *after)


_ANY = pl.BlockSpec(memory_space=pl.ANY)
_HBM = pl.BlockSpec(memory_space=pltpu.HBM)
_SEM = pl.BlockSpec(memory_space=pltpu.SEMAPHORE)
_EFFECT = pltpu.SideEffectType.DATAFLOW_SIDE_EFFECTING


def _hbm(a):
    return pltpu.with_memory_space_constraint(a, pltpu.HBM)


def _place_cast(name, j_arr, shard, kind, after):
    r, n = shard.shape
    tr = _tile(r, 256)
    nr = r // tr
    if kind == "col":
        out_shape, o_spec = (r, N_CHIP * n), pl.BlockSpec((tr, n), lambda i, j: (i, j[0]))
    else:
        out_shape, o_spec = (N_CHIP * r, n), pl.BlockSpec((tr, n), lambda i, j: (j[0] * nr + i, 0))

    def body(j_ref, s_ref, after_ref, o_ref, tok_ref):
        o_ref[...] = s_ref[...].astype(BF16)
        tok_ref[...] = jnp.zeros_like(tok_ref)

    return pl.pallas_call(
        body,
        grid_spec=pltpu.PrefetchScalarGridSpec(
            num_scalar_prefetch=1, grid=(nr,), in_specs=[pl.BlockSpec((tr, n), lambda i, j: (i, 0)), _ANY],
            out_specs=(o_spec, pl.BlockSpec((SUBLANES, LANES), lambda i, j: (0, 0)))),
        out_shape=(SDS(out_shape, BF16), SDS((SUBLANES, LANES), F32)),
        compiler_params=_params(("arbitrary",)), name=name)(j_arr, shard, after)


def _leg_direct(full, kind, x, y, c):
    mine = _full_region(full, kind, x, y, c)
    return [(mine, mine, (1 - x, y, c)), (mine, mine, (x, 1 - y, c))]


def _leg_relay(full, kind, x, y, c):
    fx, fy = jnp.where(c == 0, 1 - x, x), jnp.where(c == 0, y, 1 - y)
    tx, ty = jnp.where(c == 0, x, 1 - x), jnp.where(c == 0, 1 - y, y)
    got = _full_region(full, kind, fx, fy, c)
    return [(got, got, (tx, ty, c))]


def _leg_d2d(which):
    def leg(full, kind, x, y, c):
        chips = _other_chips(x, y)
        return [(_full_region(full, kind, *chips[k], c), _full_region(full, kind, *chips[k], c), (x, y, 1 - c))
                for k in which]
    return leg


_LEGS = {"direct": (_leg_direct, 2), "relay": (_leg_relay, 1), "d2d_near": (_leg_d2d((0, 1)), 2),
         "d2d_far": (_leg_d2d((2,)), 1)}


def _gather_call(name, fulls, kinds, waits, starts, after, thru):
    nw, n_wait, n_start = len(fulls), len(waits), len(starts)

    def body(*refs):
        wait_sems = refs[nw:nw + 2 * n_wait]
        outs = refs[nw + 2 * n_wait + 2:]
        full, start_sems = outs[:nw], outs[nw:nw + 2 * n_start]
        x, y, c = _mesh_pos()
        for i, (leg, ws, _, _) in enumerate(waits):
            fn, per = _LEGS[leg]
            for li, w in enumerate(ws):
                for k, (s_, d_, dev) in enumerate(fn(full[w], kinds[w], x, y, c)):
                    cp = pltpu.make_async_remote_copy(
                        src_ref=s_, dst_ref=d_, send_sem=wait_sems[2 * i].at[per * li + k],
                        recv_sem=wait_sems[2 * i + 1].at[per * li + k], device_id=dev, device_id_type=MESH)
                    cp.wait_recv()
                    cp.wait_send()
        for i, (leg, ws) in enumerate(starts):
            fn, per = _LEGS[leg]
            for li, w in enumerate(ws):
                for k, (s_, d_, dev) in enumerate(fn(full[w], kinds[w], x, y, c)):
                    pltpu.make_async_remote_copy(
                        src_ref=s_, dst_ref=d_, send_sem=start_sems[2 * i].at[per * li + k],
                        recv_sem=start_sems[2 * i + 1].at[per * li + k], device_id=dev, device_id_type=MESH).start()

    sems = []
    for leg, ws in starts:
        sems += [pltpu.SemaphoreType.DMA((_LEGS[leg][1] * len(ws),))] * 2
    wait_args = []
    for _, _, s_, r_ in waits:
        wait_args += [s_, r_]
    outs = pl.pallas_call(
        body,
        out_shape=tuple(pltpu.HBM(f_.shape, f_.dtype) for f_ in fulls) + tuple(sems) + (SDS(thru.shape, thru.dtype),),
        in_specs=[_HBM] * nw + [_SEM] * (2 * n_wait) + [_ANY, _ANY],
        out_specs=tuple([_HBM] * nw + [_SEM] * (2 * n_start) + [_ANY]),
        input_output_aliases={**{w: w for w in range(nw)}, nw + 2 * n_wait + 1: nw + 2 * n_start},
        compiler_params=pltpu.CompilerParams(has_side_effects=_EFFECT),
        name=name,
    )(*[_hbm(f_) for f_ in fulls], *wait_args, after, thru)
    pairs = [(outs[nw + 2 * i], outs[nw + 2 * i + 1]) for i in range(n_start)]
    return list(outs[:nw]), pairs, outs[nw + 2 * n_start]


def _full_region(full, kind, px, py, half):
    j = 2 * px + py
    if kind == "col":
        rh, cols = full.shape[0] // 2, full.shape[1] // N_CHIP
        return full.at[pl.ds(half * rh, rh), pl.ds(j * cols, cols)]
    rows = full.shape[0] // N_CHIP
    rh = rows // 2
    return full.at[pl.ds(j * rows + half * rh, rh), :]


def _plan_scatter(kind):
    def plan(src, land, x, y, c):
        out = []
        for k, (px, py) in enumerate(_other_chips(x, y)):
            j = 2 * px + py
            if kind == "col":
                n = src.shape[1] // N_CHIP
                blk = src.at[:, pl.ds(j * n, n)]
            else:
                blk = src.at[j]
            out.append((blk, land.at[k], (px, py, c)))
        return out
    return plan


def _plan_whole(src, land, x, y, c):
    return [(src, land, (x, y, 1 - c))]


def _split_start(name, src, land_shape, n, plan, thru):
    def body(src_in, land_in, thru_in, send, recv, src_ref, land_ref, thru_out):
        x, y, c = _mesh_pos()
        for k, (s_, d_, dev) in enumerate(plan(src_ref, land_ref, x, y, c)):
            pltpu.make_async_remote_copy(src_ref=s_, dst_ref=d_, send_sem=send.at[k], recv_sem=recv.at[k],
                                         device_id=dev, device_id_type=MESH).start()

    sem = pltpu.SemaphoreType.DMA((n,))
    return pl.pallas_call(
        body,
        out_shape=(sem, sem, pltpu.HBM(src.shape, src.dtype), pltpu.HBM(land_shape, src.dtype), SDS(thru.shape, thru.dtype)),
        in_specs=[_HBM, _HBM, _ANY], out_specs=(_SEM, _SEM, _HBM, _HBM, _ANY),
        input_output_aliases={0: 2, 1: 3, 2: 4},
        compiler_params=pltpu.CompilerParams(has_side_effects=_EFFECT), name=name,
    )(_hbm(src), _hbm(lax.empty(land_shape, src.dtype)), thru)


def _split_wait(name, send, recv, src, land, plan, after):
    def body(src_in, land_in, send_r, recv_r, after_r, src_ref, land_ref):
        x, y, c = _mesh_pos()
        for k, (s_, d_, dev) in enumerate(plan(src_ref, land_ref, x, y, c)):
            cp = pltpu.make_async_remote_copy(src_ref=s_, dst_ref=d_, send_sem=send_r.at[k], recv_sem=recv_r.at[k],
                                              device_id=dev, device_id_type=MESH)
            cp.wait_send()
            cp.wait_recv()

    return pl.pallas_call(
        body,
        out_shape=(pltpu.HBM(src.shape, src.dtype), pltpu.HBM(land.shape, land.dtype)),
        in_specs=[_HBM, _HBM, _SEM, _SEM, _ANY], out_specs=(_HBM, _HBM),
        input_output_aliases={0: 0, 1: 1},
        compiler_params=pltpu.CompilerParams(has_side_effects=_EFFECT), name=name,
    )(src, land, send, recv, after)


def _dev_row(buf, px, py, pc):
    return buf.at[4 * px + 2 * py + pc]


def _plan_gather_own(buf, land, x, y, c):
    own = _dev_row(buf, x, y, c)
    return [(own, own, (x, y, 1 - c))] + [(own, own, (px, py, c)) for px, py in _other_chips(x, y)]


def _plan_gather_pass(buf, land, x, y, c):
    return [(_dev_row(buf, px, py, c), _dev_row(buf, px, py, c), (x, y, 1 - c)) for px, py in _other_chips(x, y)]


def _split_start_inplace(name, buf, n, plan, thru):
    def body(buf_in, thru_in, send, recv, buf_ref, thru_out):
        x, y, c = _mesh_pos()
        for k, (s_, d_, dev) in enumerate(plan(buf_ref, buf_ref, x, y, c)):
            pltpu.make_async_remote_copy(src_ref=s_, dst_ref=d_, send_sem=send.at[k], recv_sem=recv.at[k],
                                         device_id=dev, device_id_type=MESH).start()

    sem = pltpu.SemaphoreType.DMA((n,))
    return pl.pallas_call(
        body, out_shape=(sem, sem, pltpu.HBM(buf.shape, buf.dtype), SDS(thru.shape, thru.dtype)),
        in_specs=[_HBM, _ANY], out_specs=(_SEM, _SEM, _HBM, _ANY), input_output_aliases={0: 2, 1: 3},
        compiler_params=pltpu.CompilerParams(has_side_effects=_EFFECT), name=name)(_hbm(buf), thru)


def _split_wait_inplace(name, send, recv, buf, plan, after):
    def body(buf_in, send_r, recv_r, after_r, buf_ref):
        x, y, c = _mesh_pos()
        for k, (s_, d_, dev) in enumerate(plan(buf_ref, buf_ref, x, y, c)):
            cp = pltpu.make_async_remote_copy(src_ref=s_, dst_ref=d_, send_sem=send_r.at[k], recv_sem=recv_r.at[k],
                                              device_id=dev, device_id_type=MESH)
            cp.wait_send()
            cp.wait_recv()

    return pl.pallas_call(
        body, out_shape=pltpu.HBM(buf.shape, buf.dtype), in_specs=[_HBM, _SEM, _SEM, _ANY], out_specs=_HBM,
        input_output_aliases={0: 0}, compiler_params=pltpu.CompilerParams(has_side_effects=_EFFECT), name=name,
    )(buf, send, recv, after)


def _place_row(name, me_arr, slab):
    r, n = slab.shape
    tr = r if r <= PACK_ROWS else PACK_ROWS

    def body(me_ref, s_ref, o_ref):
        o_ref[...] = s_ref[...]

    return pl.pallas_call(
        body,
        grid_spec=pltpu.PrefetchScalarGridSpec(
            num_scalar_prefetch=1, grid=(r // tr,), in_specs=[pl.BlockSpec((tr, n), lambda i, me: (i, 0))],
            out_specs=pl.BlockSpec((None, tr, n), lambda i, me: (me[0], i, 0))),
        out_shape=SDS((N_DEV, r, n), slab.dtype), compiler_params=_params(("parallel",)), name=name)(me_arr, slab)


def _sum_partials(name, j_arr, part, got, kind):
    _, rh, n = got.shape
    tr = _tile(rh, 256)
    if kind == "col":
        p_spec = pl.BlockSpec((tr, n), lambda i, j: (i, j[0]))
    else:
        p_spec = pl.BlockSpec((None, tr, n), lambda i, j: (j[0], i, 0))

    def body(j_ref, p_ref, r_ref, o_ref):
        o_ref[...] = ((p_ref[...].astype(F32) + r_ref[0].astype(F32)) + r_ref[1].astype(F32)) + r_ref[2].astype(F32)

    return pl.pallas_call(
        body,
        grid_spec=pltpu.PrefetchScalarGridSpec(
            num_scalar_prefetch=1, grid=(rh // tr,),
            in_specs=[p_spec, pl.BlockSpec((3, tr, n), lambda i, j: (0, i, 0))],
            out_specs=pl.BlockSpec((tr, n), lambda i, j: (i, 0))),
        out_shape=SDS((rh, n), F32),
        compiler_params=_params(("parallel",)), name=name)(j_arr, part, got)


def _pack(arrays):
    flat = [a.reshape(-1).astype(F32) for a in arrays]
    flat = [jnp.pad(f, (0, (-f.shape[0]) % LANES)) for f in flat]
    sizes = [f.shape[0] for f in flat]
    total = sum(sizes)
    rows = total // LANES
    tail = LANES * ((-rows) % (PACK_ROWS if rows > PACK_ROWS else SUBLANES))
    if tail:
        flat.append(jnp.zeros((tail,), F32))
    return jnp.concatenate(flat).reshape(-1, LANES), sizes


def _unpack(slab, sizes, shapes, lead=()):
    flat = slab.reshape(lead + (-1,))
    out, off = [], 0
    for sz, shp in zip(sizes, shapes):
        n = math.prod(shp)
        out.append(flat[..., off:off + n].reshape(lead + tuple(shp)))
        off += sz
    return out


def kernel(x, c, w_ada, b_ada, g_norm_mix, w_in, w_conv, b_conv, w_rg_a, b_rg_a, w_rg_x, b_rg_x, lru_lambda, g_attn_out, g_lru_out, w_out, g_norm_mlp, w_mlp_in, w_mlp_out, g_norm_final, loss_target, m_w_ada, m_b_ada, m_g_norm_mix, m_w_in, m_w_conv, m_b_conv, m_w_rg_a, m_b_rg_a, m_w_rg_x, m_b_rg_x, m_lru_lambda, m_g_attn_out, m_g_lru_out, m_w_out, m_g_norm_mlp, m_w_mlp_in, m_w_mlp_out, m_g_norm_final, v_w_ada, v_b_ada, v_g_norm_mix, v_w_in, v_w_conv, v_b_conv, v_w_rg_a, v_b_rg_a, v_w_rg_x, v_b_rg_x, v_lru_lambda, v_g_attn_out, v_g_lru_out, v_w_out, v_g_norm_mlp, v_w_mlp_in, v_w_mlp_out, v_g_norm_final):
    s, d = x.shape[1], x.shape[2]
    aw = d // 2
    nh = aw // HEAD
    f = w_mlp_out.shape[1] * N_CHIP
    n_ada = w_ada.shape[2]
    n_cv = w_conv.shape[2]
    ix, iy, ic = lax.axis_index("x"), lax.axis_index("y"), lax.axis_index("c")
    chip = 2 * ix + iy
    me = 2 * chip + ic
    c_arr = jnp.reshape(ic, (1,)).astype(jnp.int32)
    j_arr = jnp.reshape(chip, (1,)).astype(jnp.int32)

    x2d, tgt = x[0], loss_target[0]

    k_in, k_out, k_mi, k_mo = kinds = ("col", "row", "col", "row")
    me_arr = jnp.reshape(me, (1,)).astype(jnp.int32)
    slab, sizes = _pack([c, w_conv])
    c_buf = _place_row("place_cond", me_arr, slab)
    cg_send, cg_recv, c_buf, tok = _split_start_inplace("cond_gather_start", c_buf, N_CHIP, _plan_gather_own, slab)

    p_in, tok = _place_cast("place_cast_0", j_arr, w_in[0], k_in, tok)
    (f_in,), (dir_in,), tok = _gather_call("gather_0", [p_in], [k_in], [], [("direct", [0])], c, tok)
    c_buf = _split_wait_inplace("cond_gather_wait", cg_send, cg_recv, c_buf, _plan_gather_own, tok)
    cp_send, cp_recv, c_buf, tok = _split_start_inplace("cond_pass_start", c_buf, N_CHIP - 1, _plan_gather_pass, tok)
    gathered = _split_wait_inplace("cond_pass_wait", cp_send, cp_recv, c_buf, _plan_gather_pass, tok)
    c_parts, cv_parts = _unpack(gathered, sizes, [(d,), (CONV_TAPS, n_cv)], lead=(N_DEV,))
    c_all = c_parts
    w_conv_full = jnp.concatenate([cv_parts[2 * j] for j in range(N_CHIP)], axis=-1)
    b_sh = lax.dynamic_slice(b_ada, (0, chip * n_ada), (1, n_ada))
    mod_part, act_all = _ada_mod(c_all, w_ada[0], b_sh)
    p_out, tok = _place_cast("place_cast_1", j_arr, w_out[0], k_out, mod_part)
    p_mi, tok = _place_cast("place_cast_2", j_arr, w_mlp_in[0], k_mi, tok)
    mod_g = _all_gather_small("comm_gather_mod", mod_part.reshape(-1, LANES), after=(tok,))
    mod_g = mod_g.reshape(N_DEV, N_DEV, n_ada)
    mod = jnp.concatenate([lax.dynamic_index_in_dim(mod_g[2 * j], me, 0, keepdims=True) for j in range(N_CHIP)], axis=-1)
    sh1, sc1, gt1, sh2, sc2, gt2 = [mod[:, k * d:(k + 1) * d] for k in range(N_MOD)]
    (f_in,), (rel_in, near_in), sh1 = _gather_call(
        "gather_1", [f_in], [k_in], [("direct", [0], *dir_in)], [("relay", [0]), ("d2d_near", [0])], mod, sh1)

    h1, rstd1 = _norm_mod_fwd("norm_mod_fwd1", x2d, g_norm_mix, sc1, sh1)
    p_mo, tok = _place_cast("place_cast_3", j_arr, w_mlp_out[0], k_mo, rstd1)
    (f_in, f_out, f_mi), (far_in, dir_out, dir_mi), h1 = _gather_call(
        "gather_2", [f_in, p_out, p_mi], kinds[:3], [("relay", [0], *rel_in)],
        [("d2d_far", [0]), ("direct", [1]), ("direct", [2])], tok, h1)
    (w_in_f,), _, h1 = _gather_call("gather_3", [f_in], [k_in],
                                    [("d2d_near", [0], *near_in), ("d2d_far", [0], *far_in)], [], rstd1, h1)
    (qkv,) = _matmul("mm_qkv", h1, w_in_f, "nn", s, 3 * aw, d, (BF16,))
    (xrg,) = _matmul("mm_xrg", h1, w_in_f, "nn", s, 2 * aw, d, (F32,), b_off=3 * aw)
    (f_out, f_mo), (rel_out, near_out, dir_mo), qkv = _gather_call(
        "gather_3b", [f_out, p_mo], [k_out, k_mo], [("direct", [0], *dir_out)],
        [("relay", [0]), ("d2d_near", [0]), ("direct", [1])], xrg, qkv)
    o_attn, attn_w, attn_sg = _attn_fwd(qkv, nh)
    (f_out, f_mi), (far_out, rel_mi, near_mi), xrg = _gather_call(
        "gather_4", [f_out, f_mi], [k_out, k_mi], [("relay", [0], *rel_out), ("direct", [1], *dir_mi)],
        [("d2d_far", [0]), ("relay", [1]), ("d2d_near", [1])], o_attn, xrg)
    wa3, wx3 = w_rg_a[0], w_rg_x[0]
    o_lru, hseq = _lru_fwd(xrg, w_conv_full, b_conv, wa3, b_rg_a, wx3, b_rg_x, lru_lambda)
    mixed, rstd_a, rstd_l = _mix_norm_fwd(o_attn, o_lru, g_attn_out, g_lru_out)
    (f_mo,), (rel_mo, near_mo), mixed = _gather_call(
        "gather_5", [f_mo], [k_mo], [("direct", [0], *dir_mo)], [("relay", [0]), ("d2d_near", [0])], rstd_a, mixed)
    (w_out_f,), _, mixed = _gather_call(
        "gather_6", [f_out], [k_out], [("d2d_near", [0], *near_out), ("d2d_far", [0], *far_out)], [], rstd_l, mixed)

    def residual(acc, xin, gt):
        return acc, xin + gt * acc

    y1, x1 = _matmul("mm_out", mixed, w_out_f, "nn", s, d, d, (BF16, F32), extras=(x2d, gt1),
                     extra_kinds=("tile", "row"), epilogue=residual)
    (f_mi,), (far_mi,), x1 = _gather_call("gather_6b", [f_mi], [k_mi], [("relay", [0], *rel_mi)],
                                          [("d2d_far", [0])], y1, x1)
    h2, rstd2 = _norm_mod_fwd("norm_mod_fwd2", x1, g_norm_mlp, sc2, sh2)
    (w_mi_f, f_mo), (far_mo,), h2 = _gather_call(
        "gather_7", [f_mi, f_mo], [k_mi, k_mo],
        [("d2d_near", [0], *near_mi), ("d2d_far", [0], *far_mi), ("relay", [1], *rel_mo)], [("d2d_far", [1])], rstd2, h2)

    def sq_relu(acc):
        r = jnp.maximum(acc, 0.0)
        return 2.0 * r, r * r

    r2, hid = _matmul("mm_mlp_in", h2, w_mi_f, "nn", s, f, d, (BF16, BF16), epilogue=sq_relu)
    (w_mo_f,), _, hid = _gather_call("gather_8", [f_mo], [k_mo],
                                     [("d2d_near", [0], *near_mo), ("d2d_far", [0], *far_mo)], [], r2, hid)
    y2, x2 = _matmul("mm_mlp_out", hid, w_mo_f, "nn", s, d, f, (BF16, F32), extras=(x1, gt2),
                     extra_kinds=("tile", "row"), epilogue=residual)
    dx2, loss_row, dg_final, dy2, dgt2 = _final_loss(x2, g_norm_final.reshape(1, d), tgt, y2, gt2)

    oc_arr = 1 - c_arr

    def dw_half(name, st, h_arr, got=None):
        add = {} if got is None else dict(extras=(got,), extra_kinds=("tile",),
                                          epilogue=lambda acc, g_: (acc + g_.astype(F32),))
        tn = st["n"] if st["tm"] * 4 <= MM_TILE_M else MM_TILE_N
        (out,) = _matmul(name, st["a"], st["dy"], "tn", st["m"], st["n"], s, (BF16,), tm=st["tm"], tn=tn,
                         m_half=h_arr, **add)
        return out

    def rs_begin(tag, kind, xa, dy, m, n, thru):
        st = {"tag": tag, "kind": kind, "a": xa, "dy": dy, "m": m, "n": n,
              "tm": m // (2 * N_CHIP) if kind == "row" else m // 2}
        first = dw_half("mm_dw_%s_a" % tag, st, oc_arr)
        send, recv, first, land, thru = _split_start("rs_swap_start_" + tag, first, first.shape, 1, _plan_whole, thru)
        st["swap"] = (send, recv, first, land)
        return st, thru

    def rs_mid(st, after, thru):
        tag, kind = st["tag"], st["kind"]
        _, got = _split_wait("rs_swap_wait_" + tag, *st["swap"], _plan_whole, after)
        part = dw_half("mm_dw_%s_b" % tag, st, c_arr, got)
        if kind == "row":
            part = part.reshape(N_CHIP, st["m"] // (2 * N_CHIP), st["n"])
        blk = (part.shape[0], part.shape[1] // N_CHIP) if kind == "col" else part.shape[1:]
        send, recv, part, land, thru = _split_start("rs_scatter_start_" + tag, part, (N_CHIP - 1,) + blk, N_CHIP - 1,
                                                    _plan_scatter(kind), thru)
        st["scatter"] = (send, recv, part, land)
        return thru

    def rs_end(st, after):
        tag, kind = st["tag"], st["kind"]
        part, got = _split_wait("rs_scatter_wait_" + tag, *st["scatter"], _plan_scatter(kind), after)
        return _sum_partials("sum_partials_" + tag, j_arr, part, got, kind)

    (dpre,) = _matmul("mm_dhid", dy2, w_mo_f, "nt", s, f, d, (BF16,), extras=(r2,), extra_kinds=("tile",),
                      epilogue=lambda acc, r: (acc * r.astype(F32),))
    st_mo, dpre = rs_begin("mo", "row", hid, dy2, f, d, dpre)
    (dh2,) = _matmul("mm_dh2", dpre, w_mi_f, "nt", s, d, f, (BF16,))
    dh2 = rs_mid(st_mo, dh2, dh2)
    st_mi, dh2 = rs_begin("mi", "col", h2, dpre, d, f, dh2)
    dx1, dsh2, dsc2, dg_mlp, dy1, dgt1 = _norm_mod_bwd("norm_mod_bwd2", dh2, x1, rstd2, g_norm_mlp, sc2, dx2,
                                                       gate=(y1, gt1))
    (dmixed,) = _matmul("mm_dmixed", dy1, w_out_f, "nt", s, d, d, (BF16,))
    dmixed = rs_mid(st_mi, dmixed, dmixed)
    st_out, dmixed = rs_begin("out", "row", mixed, dy1, d, d, dmixed)
    do_attn, do_lru, dg_attn, dg_lru = _mix_norm_bwd(dmixed, o_attn, o_lru, rstd_a, rstd_l, g_attn_out, g_lru_out)
    dq, dk, dv = _attn_bwd(qkv, do_attn, attn_w, attn_sg, nh)
    do_lru = rs_mid(st_out, dq, do_lru)
    dxr, dxg, dwconv, dbconv, dwa, dba, dwx, dbx, dlam = _lru_bwd(
        xrg, do_lru, hseq, w_conv_full, b_conv, wa3, b_rg_a, wx3, b_rg_x, lru_lambda)
    dproj = jnp.concatenate([dq, dk, dv, dxr, dxg], axis=-1)
    st_in, dproj = rs_begin("in", "col", h1, dproj, d, 5 * aw, dproj)
    st_in["dy"] = dproj
    (dh1,) = _matmul("mm_dh1", dproj, w_in_f, "nt", s, d, 5 * aw, (BF16,))
    dh1 = rs_mid(st_in, dh1, dh1)
    grad_x, dsh1, dsc1, dg_mix = _norm_mod_bwd("norm_mod_bwd1", dh1, x2d, rstd1, g_norm_mix, sc1, dx1)

    dmod = jnp.concatenate([dsh1, dsc1, dgt1, dsh2, dsc2, dgt2], axis=-1)
    small_names = ["b_ada", "g_norm_mix", "b_conv", "w_rg_a", "b_rg_a", "w_rg_x", "b_rg_x", "lru_lambda",
                   "g_attn_out", "g_lru_out", "g_norm_mlp", "g_norm_final"]
    small_g = [dmod, dg_mix, dbconv, dwa, dba, dwx, dbx, dlam, dg_attn, dg_lru, dg_mlp, dg_final]
    small_w = [b_ada, g_norm_mix, b_conv, w_rg_a, b_rg_a, w_rg_x, b_rg_x, lru_lambda, g_attn_out, g_lru_out, g_norm_mlp, g_norm_final]
    small_m = [m_b_ada, m_g_norm_mix, m_b_conv, m_w_rg_a, m_b_rg_a, m_w_rg_x, m_b_rg_x, m_lru_lambda, m_g_attn_out, m_g_lru_out, m_g_norm_mlp, m_g_norm_final]
    small_v = [v_b_ada, v_g_norm_mix, v_b_conv, v_w_rg_a, v_b_rg_a, v_w_rg_x, v_b_rg_x, v_lru_lambda, v_g_attn_out, v_g_lru_out, v_g_norm_mlp, v_g_norm_final]
    extra_zero = [jnp.zeros_like(dwconv), jnp.zeros((LANES,), F32)]
    g_slab, g_sizes = _pack(small_g + [dwconv, loss_row])
    w_slab, _ = _pack(small_w + extra_zero)
    m_slab, _ = _pack(small_m + extra_zero)
    v_slab, _ = _pack(small_v + extra_zero)
    g_buf = _place_row("place_small_grads", me_arr, g_slab)
    sg_send, sg_recv, g_buf, tok = _split_start_inplace("sg_gather_start", g_buf, N_CHIP, _plan_gather_own, loss_row)

    def reduced_begin(tag, half, tok_):
        send, recv, half, land, tok_ = _split_start("rs_reduced_start_" + tag, half, half.shape, 1, _plan_whole, tok_)
        return (send, recv, half, land), tok_

    def reduced_end(tag, st, after):
        return _split_wait("rs_reduced_wait_" + tag, *st, _plan_whole, after)

    sw_mo, tok = reduced_begin("mo", rs_end(st_mo, tok), tok)
    sw_mi, tok = reduced_begin("mi", rs_end(st_mi, tok), tok)
    sw_out, tok = reduced_begin("out", rs_end(st_out, tok), tok)
    half_mo, got_mo = reduced_end("mo", sw_mo, tok)
    big = {"w_mlp_out": _adamw_halves("adamw_w_mlp_out", c_arr, w_mlp_out[0], m_w_mlp_out[0], v_w_mlp_out[0],
                                      half_mo, got_mo)}
    half_mi, got_mi = reduced_end("mi", sw_mi, big["w_mlp_out"][1])
    big["w_mlp_in"] = _adamw_halves("adamw_w_mlp_in", c_arr, w_mlp_in[0], m_w_mlp_in[0], v_w_mlp_in[0], half_mi, got_mi)
    half_out, got_out = reduced_end("out", sw_out, big["w_mlp_in"][1])
    big["w_out"] = _adamw_halves("adamw_w_out", c_arr, w_out[0], m_w_out[0], v_w_out[0], half_out, got_out)
    g_buf = _split_wait_inplace("sg_gather_wait", sg_send, sg_recv, g_buf, _plan_gather_own, big["w_out"][1])
    sg_send, sg_recv, g_buf, tok = _split_start_inplace("sg_pass_start", g_buf, N_CHIP - 1, _plan_gather_pass, tok)
    sw_in, tok = reduced_begin("in", rs_end(st_in, tok), tok)
    g_all = _split_wait_inplace("sg_pass_wait", sg_send, sg_recv, g_buf, _plan_gather_pass, tok)
    gs_slab, ds_slab, ms_slab, vs_slab = _small_reduce_adamw(g_all, w_slab, m_slab, v_slab)
    shapes = [w.shape for w in small_w] + [dwconv.shape, (LANES,)]
    gs = _unpack(gs_slab, g_sizes, shapes)
    ds = _unpack(ds_slab, g_sizes, shapes)
    ms = _unpack(ms_slab, g_sizes, shapes)
    vs = _unpack(vs_slab, g_sizes, shapes)
    small = {n: (gs[i], ds[i], ms[i], vs[i]) for i, n in enumerate(small_names)}
    loss = gs[-1][0]
    g_wconv = lax.dynamic_slice(gs[-2], (0, chip * n_cv), (CONV_TAPS, n_cv))
    d_wconv, m_wconv, v_wconv = _adamw_plain("adamw_conv", w_conv[0], g_wconv, m_w_conv[0], v_w_conv[0])
    small["w_conv"] = (g_wconv[None], d_wconv[None], m_wconv[None], v_wconv[None])

    dmod_all = g_all[:, :N_MOD * d // LANES, :].reshape(N_DEV, N_MOD * d)
    dmod_sel = lax.dynamic_slice(dmod_all, (0, chip * n_ada), (N_DEV, n_ada)).astype(BF16)
    act_t = act_all.T.astype(BF16)
    big["w_ada"] = _adamw_ada(w_ada[0], m_w_ada[0], v_w_ada[0], act_t, dmod_sel)

    half_in, got_in = reduced_end("in", sw_in, big["w_ada"][1])
    big["w_in"] = _adamw_halves("adamw_w_in", c_arr, w_in[0], m_w_in[0], v_w_in[0], half_in, got_in)

    order = ["w_ada", "b_ada", "g_norm_mix", "w_in", "w_conv", "b_conv", "w_rg_a", "b_rg_a", "w_rg_x", "b_rg_x",
             "lru_lambda", "g_attn_out", "g_lru_out", "w_out", "g_norm_mlp", "w_mlp_in", "w_mlp_out", "g_norm_final"]
    res = {}
    for n in order:
        res[n] = tuple(t[None] for t in big[n]) if n in big else small[n]
    return (loss, grad_x[None],
            *[res[n][0] for n in order], *[res[n][1] for n in order],
            *[res[n][2] for n in order], *[res[n][3] for n in order])
```

```python
import functools
import math

import jax
import jax.numpy as jnp
from jax import lax
from jax.experimental import pallas as pl
from jax.experimental.pallas import tpu as pltpu

F32 = jnp.float32
BF16 = jnp.bfloat16
SDS = jax.ShapeDtypeStruct
MESH = pl.DeviceIdType.MESH

EPS = 1e-6
HEAD = 128
N_MOD = 6
CONV_TAPS = 4
LRU_C = 8.0
ADAM_LR, ADAM_B1, ADAM_B2, ADAM_EPS, ADAM_WD, ADAM_STEP = 0.001, 0.9, 0.999, 1e-08, 0.01, 10
N_DEV = 8
N_CHIP = 4
LANES = 128
SUBLANES = 8
VMEM_LIMIT = 56 * 1024 * 1024
PACK_ROWS = 256
MM_TILE_M, MM_TILE_N, MM_TILE_K = 1024, 1024, 2048
ROW_TILE = 256
ROW_SPLIT = 1


def _tile(dim, pref):
    t = min(dim, pref)
    while dim % t:
        t -= LANES
    return t


def _params(sem=None):
    return pltpu.CompilerParams(dimension_semantics=sem, vmem_limit_bytes=VMEM_LIMIT)


def _sigmoid(x):
    return 1.0 / (1.0 + jnp.exp(-x))


def _log_sigmoid(x):
    return jnp.minimum(x, 0.0) - jnp.log(1.0 + jnp.exp(-jnp.abs(x)))


def _gelu_parts(x):
    k0, k1 = math.sqrt(2.0 / math.pi), 0.044715
    t = jnp.tanh(k0 * (x + k1 * x * x * x))
    val = 0.5 * x * (1.0 + t)
    der = 0.5 * (1.0 + t) + 0.5 * x * (1.0 - t * t) * k0 * (1.0 + 3.0 * k1 * x * x)
    return val, der


def _dot(a, b):
    return jnp.dot(a, b, preferred_element_type=F32)


def _dot_nt(a, b):
    return lax.dot_general(a, b, (((1,), (1,)), ((), ())), preferred_element_type=F32)


def _dot_tn(a, b):
    return lax.dot_general(a, b, (((0,), (0,)), ((), ())), preferred_element_type=F32)


def _split_dot(x, tri):
    hi = x.astype(BF16)
    lo = (x - hi.astype(F32)).astype(BF16)
    return _dot(hi, tri) + _dot(lo, tri)


def _matmul(name, a, b, mode, m, n, k, out_dtypes, *, b_off=0, extras=(), extra_kinds=(), epilogue=None,
            tm=MM_TILE_M, tn=MM_TILE_N, tk=MM_TILE_K, m_half=None):
    tm, tn, tk = _tile(m, tm), _tile(math.gcd(n, b_off) if b_off else n, tn), _tile(k, tk)
    assert b_off % tn == 0
    nk = k // tk
    n_ex, n_out = len(extras), len(out_dtypes)
    dot = {"nn": _dot, "nt": _dot_nt, "tn": _dot_tn}[mode]
    n_pre = 0 if m_half is None else 1
    m_out = m if m_half is None else m // 2

    def body(*refs):
        a_ref, b_ref, *rest = refs[n_pre:]
        ex, outs = rest[:n_ex], rest[n_ex:n_ex + n_out]

        def finish(total):
            res = epilogue(total, *[e[...] for e in ex]) if epilogue else (total,)
            for o, r in zip(outs, res):
                o[...] = r.astype(o.dtype)

        if nk == 1:
            finish(dot(a_ref[...], b_ref[...]))
            return
        acc = rest[-1]
        kk = pl.program_id(2)

        @pl.when(kk == 0)
        def _():
            acc[...] = dot(a_ref[...], b_ref[...])

        @pl.when(jnp.logical_and(kk > 0, kk < nk - 1))
        def _():
            acc[...] += dot(a_ref[...], b_ref[...])

        @pl.when(kk == nk - 1)
        def _():
            finish(acc[...] + dot(a_ref[...], b_ref[...]))

    if mode == "nn":
        a_spec = pl.BlockSpec((tm, tk), lambda i, j, kk, *_: (i, kk))
        b_spec = pl.BlockSpec((tk, tn), lambda i, j, kk, *_: (kk, j + b_off // tn))
    elif mode == "nt":
        a_spec = pl.BlockSpec((tm, tk), lambda i, j, kk, *_: (i, kk))
        b_spec = pl.BlockSpec((tn, tk), lambda i, j, kk, *_: (j, kk + b_off // tk))
    elif m_half is None:
        a_spec = pl.BlockSpec((tk, tm), lambda i, j, kk: (kk, i))
        b_spec = pl.BlockSpec((tk, tn), lambda i, j, kk: (kk, j))
    else:
        a_spec = pl.BlockSpec((tk, tm), lambda i, j, kk, h: (kk, 2 * i + h[0]))
        b_spec = pl.BlockSpec((tk, tn), lambda i, j, kk, h: (kk, j))
    tile_spec = pl.BlockSpec((tm, tn), lambda i, j, kk, *_: (i, j))
    row_spec = pl.BlockSpec((1, tn), lambda i, j, kk, *_: (0, j))
    in_specs = [a_spec, b_spec] + [tile_spec if kind == "tile" else row_spec for kind in extra_kinds]
    out_specs = tuple(tile_spec for _ in out_dtypes)
    out_shape = tuple(SDS((m_out, n), dt) for dt in out_dtypes)
    scratch = [pltpu.VMEM((tm, tn), F32)] if nk > 1 else []
    grid = (m_out // tm, n // tn, nk)
    semantics = _params(("parallel", "parallel", "arbitrary"))
    if m_half is None:
        return pl.pallas_call(body, grid=grid, in_specs=in_specs, out_specs=out_specs, out_shape=out_shape,
                              scratch_shapes=scratch, compiler_params=semantics, name=name)(a, b, *extras)
    assert mode == "tn" and (m // tm) % 2 == 0
    return pl.pallas_call(
        body,
        grid_spec=pltpu.PrefetchScalarGridSpec(num_scalar_prefetch=1, grid=grid, in_specs=in_specs, out_specs=out_specs,
                                               scratch_shapes=scratch),
        out_shape=out_shape, compiler_params=semantics, name=name)(m_half, a, b, *extras)


def _row_specs(s, d, tr):
    row = "row"
    vec = pl.BlockSpec((1, d), lambda i: (0, 0))
    col = pl.BlockSpec((tr, 1), lambda i: (i, 0))
    return row, vec, col


class _ColChunks:
    def __init__(self, refs):
        self.refs = refs

    def __getitem__(self, idx):
        return jnp.concatenate([r[...] for r in self.refs], axis=-1)


def _rows_call(name, body, grid, in_specs, out_specs, out_shape, semantics, args):
    in_x, args_x, groups = [], [], []
    for spec, arr in zip(in_specs, args):
        if isinstance(spec, str):
            rows, d = arr.shape
            tr, dc = rows // grid[0], d // ROW_SPLIT
            in_x += [pl.BlockSpec((tr, dc), functools.partial(lambda i, jj: (i, jj), jj=j)) for j in range(ROW_SPLIT)]
            args_x += [arr] * ROW_SPLIT
            groups.append(ROW_SPLIT)
        else:
            in_x.append(spec)
            args_x.append(arr)
            groups.append(1)
    out_x = [pl.BlockSpec((sh.shape[0] // grid[0], sh.shape[1]), lambda i: (i, 0)) if isinstance(spec, str) else spec
             for spec, sh in zip(out_specs, out_shape)]

    def wrapped(*refs):
        views, k = [], 0
        for g in groups:
            views.append(_ColChunks(refs[k:k + g]) if g > 1 else refs[k])
            k += g
        body(*views, *refs[k:])

    return pl.pallas_call(
        wrapped, grid=grid, in_specs=in_x, out_specs=tuple(out_x), out_shape=tuple(out_shape),
        compiler_params=_params(semantics), name=name)(*args_x)


ROW_RING = 3


def _norm_mod_fwd(name, x, g, sc, sh):
    s, d = x.shape
    tr = _tile(s, ROW_TILE)
    n = s // tr

    def body(x_hbm, g_ref, sc_ref, sh_ref, h_hbm, r_ref, xbuf, hbuf, in_sem, out_sem):
        def read(i):
            return pltpu.make_async_copy(x_hbm.at[pl.ds(i * tr, tr), :], xbuf.at[i % ROW_RING], in_sem.at[i % ROW_RING])

        def write(i):
            return pltpu.make_async_copy(hbuf.at[i % 2], h_hbm.at[pl.ds(i * tr, tr), :], out_sem.at[i % 2])

        for i in range(min(ROW_RING, n)):
            read(i).start()
        for i in range(n):
            read(i).wait()
            xv = xbuf[i % ROW_RING]
            r = lax.rsqrt(jnp.mean(xv * xv, axis=-1, keepdims=True) + EPS)
            hv = ((xv * r * g_ref[...]) * (1.0 + sc_ref[...]) + sh_ref[...]).astype(BF16)
            if i >= 2:
                write(i - 2).wait()
            hbuf[i % 2] = hv
            r_ref[i * tr:(i + 1) * tr, :] = r
            write(i).start()
            if i + ROW_RING < n:
                read(i + ROW_RING).start()
        for i in range(max(n - 2, 0), n):
            write(i).wait()

    vmem = pl.BlockSpec(memory_space=pltpu.VMEM)
    return pl.pallas_call(
        body, in_specs=[_ANY, vmem, vmem, vmem], out_specs=(_ANY, vmem),
        out_shape=(SDS((s, d), BF16), SDS((s, 1), F32)),
        scratch_shapes=[pltpu.VMEM((ROW_RING, tr, d), F32), pltpu.VMEM((2, tr, d), BF16),
                        pltpu.SemaphoreType.DMA((ROW_RING,)), pltpu.SemaphoreType.DMA((2,))],
        compiler_params=pltpu.CompilerParams(vmem_limit_bytes=VMEM_LIMIT), name=name)(x, g, sc, sh)


def _norm_mod_bwd(name, dh, xin, rstd, g, sc, dres, gate=None):
    s, d = xin.shape
    tr = _tile(s, ROW_TILE)
    row, vec, col = _row_specs(s, d, tr)

    n_gate = 2 if gate is not None else 0

    def body(dh_ref, x_ref, r_ref, g_ref, sc_ref, dres_ref, *rest):
        gate_in, gate_out = rest[:n_gate], rest[n_gate + 4:]
        dx_ref, dsh_ref, dsc_ref, dg_ref = rest[n_gate:n_gate + 4]

        @pl.when(pl.program_id(0) == 0)
        def _():
            for ref in (dsh_ref, dsc_ref, dg_ref) + tuple(gate_out[1:]):
                ref[...] = jnp.zeros_like(ref)

        dh_v, xv, r, gv = dh_ref[...].astype(F32), x_ref[...], r_ref[...], g_ref[...]
        n0 = xv * r
        dsh_ref[...] += jnp.sum(dh_v, axis=0, keepdims=True)
        dsc_ref[...] += jnp.sum(dh_v * (n0 * gv), axis=0, keepdims=True)
        dn = dh_v * (1.0 + sc_ref[...])
        dg_ref[...] += jnp.sum(dn * n0, axis=0, keepdims=True)
        gy = dn * gv
        dot = jnp.mean(gy * xv, axis=-1, keepdims=True)
        dxv = dres_ref[...] + r * gy - xv * (r * r * r * dot)
        dx_ref[...] = dxv
        if gate is not None:
            y_ref, gt_ref = gate_in
            dy_ref, dgt_ref = gate_out
            dy_ref[...] = (gt_ref[...] * dxv).astype(BF16)
            dgt_ref[...] += jnp.sum(dxv * y_ref[...], axis=0, keepdims=True)

    vecs = SDS((1, d), F32)
    gate_args = tuple(gate) if gate is not None else ()
    return _rows_call(
        name, body, (s // tr,),
        [row, row, col, vec, vec, row] + ([row, vec] if gate is not None else []),
        (row, vec, vec, vec) + ((row, vec) if gate is not None else ()),
        (SDS((s, d), F32), vecs, vecs, vecs) + ((SDS((s, d), BF16), vecs) if gate is not None else ()),
        ("arbitrary",), (dh, xin, rstd, g, sc, dres, *gate_args))


def _final_loss(x2, gf, tgt, y, gt):
    s, d = x2.shape
    tr = _tile(s, ROW_TILE)
    row, vec, _ = _row_specs(s, d, tr)
    lrow = pl.BlockSpec((1, LANES), lambda i: (0, 0))

    def body(x_ref, g_ref, t_ref, y_ref, gt_ref, dx_ref, loss_ref, dg_ref, dy_ref, dgt_ref):
        @pl.when(pl.program_id(0) == 0)
        def _():
            loss_ref[...] = jnp.zeros_like(loss_ref)
            dg_ref[...] = jnp.zeros_like(dg_ref)
            dgt_ref[...] = jnp.zeros_like(dgt_ref)

        xv, gv = x_ref[...], g_ref[...]
        r = lax.rsqrt(jnp.mean(xv * xv, axis=-1, keepdims=True) + EPS)
        n0 = xv * r
        err = n0 * gv - t_ref[...]
        loss_ref[...] += jnp.sum(err * err) * (0.5 / d)
        dy = err * (1.0 / d)
        dg_ref[...] += jnp.sum(dy * n0, axis=0, keepdims=True)
        gy = dy * gv
        dot = jnp.mean(gy * xv, axis=-1, keepdims=True)
        dxv = r * gy - xv * (r * r * r * dot)
        dx_ref[...] = dxv
        dy_ref[...] = (gt_ref[...] * dxv).astype(BF16)
        dgt_ref[...] += jnp.sum(dxv * y_ref[...], axis=0, keepdims=True)

    return _rows_call(
        "final_loss", body, (s // tr,), [row, vec, row, row, vec], (row, lrow, vec, row, vec),
        (SDS((s, d), F32), SDS((1, LANES), F32), SDS((1, d), F32), SDS((s, d), BF16), SDS((1, d), F32)),
        ("arbitrary",), (x2, gf, tgt, y, gt))


def _mix_norm_fwd(oa, ol, ga, gl):
    s, w = oa.shape
    tr = _tile(s, ROW_TILE)
    row, vec, col = _row_specs(s, w, tr)

    def body(oa_ref, ol_ref, ga_ref, gl_ref, mx_ref, ra_ref, rl_ref):
        a, l = oa_ref[...], ol_ref[...]
        ra = lax.rsqrt(jnp.mean(a * a, axis=-1, keepdims=True) + EPS)
        rl = lax.rsqrt(jnp.mean(l * l, axis=-1, keepdims=True) + EPS)
        mx_ref[:, :w] = (a * ra * ga_ref[...]).astype(BF16)
        mx_ref[:, w:] = (l * rl * gl_ref[...]).astype(BF16)
        ra_ref[...] = ra
        rl_ref[...] = rl

    return _rows_call(
        "mix_norm_fwd", body, (s // tr,), [row, row, vec, vec], (row, col, col),
        (SDS((s, 2 * w), BF16), SDS((s, 1), F32), SDS((s, 1), F32)), ("parallel",), (oa, ol, ga, gl))


def _mix_norm_bwd(dmx, oa, ol, ra, rl, ga, gl):
    s, w = oa.shape
    tr = _tile(s, ROW_TILE)
    row, vec, col = _row_specs(s, w, tr)

    def body(dm_ref, oa_ref, ol_ref, ra_ref, rl_ref, ga_ref, gl_ref, doa_ref, dol_ref, dga_ref, dgl_ref):
        @pl.when(pl.program_id(0) == 0)
        def _():
            dga_ref[...] = jnp.zeros_like(dga_ref)
            dgl_ref[...] = jnp.zeros_like(dgl_ref)

        def one(dy, xv, r, gv, dg_ref):
            dg_ref[...] += jnp.sum(dy * (xv * r), axis=0, keepdims=True)
            gy = dy * gv
            dot = jnp.mean(gy * xv, axis=-1, keepdims=True)
            return r * gy - xv * (r * r * r * dot)

        dm = dm_ref[...].astype(F32)
        doa_ref[...] = one(dm[:, :w], oa_ref[...], ra_ref[...], ga_ref[...], dga_ref).astype(BF16)
        dol_ref[...] = one(dm[:, w:], ol_ref[...], rl_ref[...], gl_ref[...], dgl_ref)

    return _rows_call(
        "mix_norm_bwd", body, (s // tr,), [row, row, row, col, col, vec, vec], (row, row, vec, vec),
        (SDS((s, w), BF16), SDS((s, w), F32), SDS((1, w), F32), SDS((1, w), F32)), ("arbitrary",),
        (dmx, oa, ol, ra, rl, ga, gl))


def _attn_blocks(qs, ks, tri_after, csums, causal):
    zs = [_dot_nt(q, k) * (HEAD ** -0.5) for q, k in zip(qs, ks)]
    lbs = [_log_sigmoid(z) for z in zs]
    lss = [lb - z for lb, z in zip(lbs, zs)]
    if causal is not None:
        lss = [jnp.where(causal, ls, 0.0) for ls in lss]
    locs = [_split_dot(ls, tri_after) for ls in lss]
    ws = [jnp.exp(lb + (loc + cs)) for lb, loc, cs in zip(lbs, locs, csums)]
    if causal is not None:
        ws = [jnp.where(causal, w, 0.0) for w in ws]
    nxt = [cs + (loc[:, 0:1] + ls[:, 0:1]) for cs, loc, ls in zip(csums, locs, lss)]
    return lbs, ws, nxt


ATTN_HEADS_PER_STEP = 4


def _attn_tile(s):
    return 256 if s >= 1024 else 128


def _tri(t, after):
    r_i = lax.broadcasted_iota(jnp.int32, (t, t), 0)
    c_i = lax.broadcasted_iota(jnp.int32, (t, t), 1)
    return ((r_i > c_i) if after else (r_i < c_i)).astype(BF16)


def _attn_fwd(qkv, n_heads):
    s = qkv.shape[0]
    t = _attn_tile(s)
    hps = ATTN_HEADS_PER_STEP
    wid = hps * HEAD

    nq = s // t

    def body(q_ref, k_ref, v_ref, o_ref, w_ref, sg_ref):
        qi = pl.program_id(1)
        tri_after = _tri(t, True)
        causal = lax.broadcasted_iota(jnp.int32, (t, t), 1) < lax.broadcasted_iota(jnp.int32, (t, t), 0)
        lanes = [slice(a * HEAD, (a + 1) * HEAD) for a in range(hps)]
        qs = [q_ref[:, ln] for ln in lanes]

        def block(kb, carry, mask):
            off = pl.multiple_of(kb * t, t)
            ks = [k_ref[pl.ds(off, t), ln] for ln in lanes]
            lbs, ws, csums = _attn_blocks(qs, ks, tri_after, [cr[0] for cr in carry], mask)
            wbs = [w.astype(BF16) for w in ws]
            for a in range(hps):
                w_ref[a, kb] = wbs[a]
                sg_ref[a, kb] = jnp.exp(lbs[a]).astype(BF16)
            os_ = [cr[1] + _dot(wb, v_ref[pl.ds(off, t), ln]) for cr, wb, ln in zip(carry, wbs, lanes)]
            return tuple(zip(csums, os_))

        zero = tuple((jnp.zeros((t, 1), F32), jnp.zeros((t, HEAD), F32)) for _ in lanes)
        carry = block(qi, zero, causal)
        carry = lax.fori_loop(1, qi + 1, lambda it, cr: block(qi - it, cr, None), carry)
        for a, ln in enumerate(lanes):
            o_ref[:, ln] = carry[a][1]

    hb = n_heads // hps
    kept = pl.BlockSpec((None, hps, nq, t, t), lambda hh, i: (hh * nq + i, 0, 0, 0, 0))
    kept_shape = SDS((hb * nq, hps, nq, t, t), BF16)
    return pl.pallas_call(
        body, grid=(hb, nq),
        in_specs=[pl.BlockSpec((t, wid), lambda hh, i: (i, hh)),
                  pl.BlockSpec((s, wid), lambda hh, i: (0, hb + hh)),
                  pl.BlockSpec((s, wid), lambda hh, i: (0, 2 * hb + hh))],
        out_specs=(pl.BlockSpec((t, wid), lambda hh, i: (i, hh)), kept, kept),
        out_shape=(SDS((s, n_heads * HEAD), F32), kept_shape, kept_shape),
        compiler_params=_params(("parallel", "parallel")), name="attn_fwd")(qkv, qkv, qkv)


def _attn_bwd(qkv, do, w_kept, sg_kept, n_heads):
    s = qkv.shape[0]
    t = _attn_tile(s)
    nq = s // t
    scale = HEAD ** -0.5
    hps = ATTN_HEADS_PER_STEP
    wid = hps * HEAD

    def body(q_ref, k_ref, v_ref, do_ref, w_ref, sg_ref, dq_ref, dk_ref, dv_ref, dk_acc, dv_acc):
        qi = pl.program_id(1)

        @pl.when(qi == 0)
        def _():
            dk_acc[...] = jnp.zeros_like(dk_acc)
            dv_acc[...] = jnp.zeros_like(dv_acc)

        tri_before = _tri(t, False)
        causal = lax.broadcasted_iota(jnp.int32, (t, t), 1) < lax.broadcasted_iota(jnp.int32, (t, t), 0)
        lanes = [slice(a * HEAD, (a + 1) * HEAD) for a in range(hps)]
        qs = [q_ref[:, ln] for ln in lanes]
        douts = [do_ref[:, ln] for ln in lanes]

        def block(kb, carry, mask):
            off = pl.multiple_of(kb * t, t)
            wbs = [w_ref[a, kb] for a in range(hps)]
            dws = [_dot_nt(dout, v_ref[pl.ds(off, t), ln]) for dout, ln in zip(douts, lanes)]
            for a, ln in enumerate(lanes):
                dv_acc[pl.ds(off, t), ln] += _dot_tn(wbs[a], douts[a])
            es = [dw * wb.astype(F32) for dw, wb in zip(dws, wbs)]
            locs = [_split_dot(e, tri_before) for e in es]
            sgs = [sg_ref[a, kb].astype(F32) for a in range(hps)]
            stays = [(loc + cr[0]) * sg for loc, cr, sg in zip(locs, carry, sgs)]
            if mask is not None:
                stays = [jnp.where(mask, st, 0.0) for st in stays]
            dzbs = [((e * (1.0 - sg) - st) * scale).astype(BF16) for e, sg, st in zip(es, sgs, stays)]
            dqs = [cr[1] + _dot(dzb, k_ref[pl.ds(off, t), ln]) for cr, dzb, ln in zip(carry, dzbs, lanes)]
            for a, ln in enumerate(lanes):
                dk_acc[pl.ds(off, t), ln] += _dot_tn(dzbs[a], qs[a])
            esums = [cr[0] + (loc[:, t - 1:t] + e[:, t - 1:t]) for cr, loc, e in zip(carry, locs, es)]
            return tuple(zip(esums, dqs))

        zero = tuple((jnp.zeros((t, 1), F32), jnp.zeros((t, HEAD), F32)) for _ in lanes)
        carry = lax.fori_loop(0, qi, lambda kb, cr: block(kb, cr, None), zero)
        carry = block(qi, carry, causal)
        for a, ln in enumerate(lanes):
            dq_ref[:, ln] = carry[a][1].astype(BF16)

        @pl.when(qi == nq - 1)
        def _():
            dk_ref[...] = dk_acc[...].astype(BF16)
            dv_ref[...] = dv_acc[...].astype(BF16)

    hb = n_heads // hps
    blk = pl.BlockSpec((t, wid), lambda hh, i: (i, hh))
    full = pl.BlockSpec((s, wid), lambda hh, i: (0, hh))
    kept = pl.BlockSpec((None, hps, nq, t, t), lambda hh, i: (hh * nq + i, 0, 0, 0, 0))
    return pl.pallas_call(
        body, grid=(hb, nq),
        in_specs=[blk,
                  pl.BlockSpec((s, wid), lambda hh, i: (0, hb + hh)),
                  pl.BlockSpec((s, wid), lambda hh, i: (0, 2 * hb + hh)),
                  blk, kept, kept],
        out_specs=(blk, full, full),
        out_shape=(SDS((s, n_heads * HEAD), BF16),) * 3,
        scratch_shapes=[pltpu.VMEM((s, wid), F32), pltpu.VMEM((s, wid), F32)],
        compiler_params=_params(("parallel", "arbitrary")), name="attn_bwd")(qkv, qkv, qkv, do, w_kept, sg_kept)


def _lru_chunk(s):
    return 128 if s >= 256 else s // 2


def _lru_gates(xc, wa, ba, wx, bx, sp):
    xb = xc.astype(BF16)
    r = _sigmoid(_dot(xb, wa) + ba)
    ig = _sigmoid(_dot(xb, wx) + bx)
    la = -LRU_C * r * sp
    a = jnp.exp(la)
    t = jnp.tanh(la)
    mult = jnp.sqrt(-2.0 * t / (1.0 - t))
    return r, ig, a, mult


def _softplus_neg(lam):
    return jnp.maximum(-lam, 0.0) + jnp.log(1.0 + jnp.exp(-jnp.abs(lam)))


LRU_BLOCKS_PER_STEP = 1
SCAN_GROUP = 4


def _lru_specs(s, n_blocks):
    bps = min(LRU_BLOCKS_PER_STEP, n_blocks)
    wid = bps * HEAD
    seq0 = pl.BlockSpec((s, wid), lambda h: (0, h))
    seq1 = pl.BlockSpec((s, wid), lambda h: (0, n_blocks // bps + h))
    taps = pl.BlockSpec((CONV_TAPS, wid), lambda h: (0, h))
    vec = pl.BlockSpec((1, wid), lambda h: (0, h))
    mat = pl.BlockSpec((bps, HEAD, HEAD), lambda h: (h, 0, 0))
    return bps, seq0, seq1, taps, vec, mat


def _per_block(one_block, n_2d, n_mat_pos, bps):
    def body(*refs):
        for a in range(bps):
            views = [r.at[a] if i in n_mat_pos else r.at[:, pl.ds(a * HEAD, HEAD)] for i, r in enumerate(refs[:n_2d])]
            one_block(*views, *refs[n_2d:])
    return body


def _lru_fwd(xrg, wconv, bconv, wa, ba, wx, bx, lam):
    s = xrg.shape[0]
    nb = wa.shape[0]
    tc = _lru_chunk(s)
    bps, seq0, seq1, taps, vec, mat = _lru_specs(s, nb)
    pad = SUBLANES

    def one_block(xr_ref, xg_ref, wc_ref, bc_ref, wa_ref, ba_ref, wx_ref, bx_ref, lam_ref, o_ref, h_ref, pad_s, a_s, u_s):
        pad_s[0:pad, :] = jnp.zeros((pad, HEAD), F32)
        pad_s[pad:pad + s, :] = xr_ref[...]
        wab, wxb = wa_ref[...].astype(BF16), wx_ref[...].astype(BF16)
        sp = _softplus_neg(lam_ref[...])
        for c in range(s // tc):
            base = c * tc
            xc = bc_ref[...] + sum(wc_ref[i:i + 1, :] * pad_s[pl.ds(base + pad - (CONV_TAPS - 1) + i, tc), :]
                                   for i in range(CONV_TAPS))
            _, ig, a, mult = _lru_gates(xc, wab, ba_ref[...], wxb, bx_ref[...], sp)
            a_s[base:base + tc, :] = a
            u_s[base:base + tc, :] = mult * (ig * xc)

        row = lax.broadcasted_iota(jnp.int32, (SUBLANES, HEAD), 0)
        last = SUBLANES - 1

        def group(gi, hprev):
            offs = [pl.multiple_of((gi * SCAN_GROUP + q) * SUBLANES, SUBLANES) for q in range(SCAN_GROUP)]
            ab = []
            for off in offs:
                a8, b8 = a_s[pl.ds(off, SUBLANES), :], u_s[pl.ds(off, SUBLANES), :]
                for d in (1, 2, 4):
                    a_sh = jnp.where(row < d, 1.0, pltpu.roll(a8, d, 0))
                    b_sh = jnp.where(row < d, 0.0, pltpu.roll(b8, d, 0))
                    b8 = a8 * b_sh + b8
                    a8 = a8 * a_sh
                ab.append((a8, b8))
            enters = []
            for a8, b8 in ab:
                enters.append(hprev)
                hprev = a8[last:, :] * hprev + b8[last:, :]
            for off, (a8, b8), h0 in zip(offs, ab, enters):
                h_ref[pl.ds(off, SUBLANES), :] = a8 * h0 + b8
            return hprev

        lax.fori_loop(0, s // (SUBLANES * SCAN_GROUP), group, jnp.zeros((1, HEAD), F32))
        for c in range(s // tc):
            sl = slice(c * tc, (c + 1) * tc)
            gel, _ = _gelu_parts(xg_ref[sl, :])
            o_ref[sl, :] = h_ref[sl, :] * gel

    return pl.pallas_call(
        _per_block(one_block, 11, (4, 6), bps), grid=(nb // bps,),
        in_specs=[seq0, seq1, taps, vec, mat, vec, mat, vec, vec],
        out_specs=(seq0, seq0),
        out_shape=(SDS((s, nb * HEAD), F32), SDS((s, nb * HEAD), F32)),
        scratch_shapes=[pltpu.VMEM((s + pad, HEAD), F32), pltpu.VMEM((s, HEAD), F32), pltpu.VMEM((s, HEAD), F32)],
        compiler_params=_params(("parallel",)), name="lru_fwd")(xrg, xrg, wconv, bconv, wa, ba, wx, bx, lam)


def _lru_bwd(xrg, dol, hseq, wconv, bconv, wa, ba, wx, bx, lam):
    s = xrg.shape[0]
    nb = wa.shape[0]
    tc = _lru_chunk(s)
    bps, seq0, seq1, taps, vec, mat = _lru_specs(s, nb)
    pad = SUBLANES

    def one_block(xr_ref, xg_ref, do_ref, h_ref, wc_ref, bc_ref, wa_ref, ba_ref, wx_ref, bx_ref, lam_ref,
             dxr_ref, dxg_ref, dwc_ref, dbc_ref, dwa_ref, dba_ref, dwx_ref, dbx_ref, dlam_ref,
             pad_s, hp_s, a_s, g_s, da_s, dxc_s, xc_s, r_s, ig_s, mult_s):
        pad_s[0:pad, :] = jnp.zeros((pad, HEAD), F32)
        pad_s[pad:pad + s, :] = xr_ref[...]
        hp_s[0:pad, :] = jnp.zeros((pad, HEAD), F32)
        hp_s[pad:pad + s, :] = h_ref[...]
        a_s[s:s + pad, :] = jnp.zeros((pad, HEAD), F32)
        dxc_s[s:s + pad, :] = jnp.zeros((pad, HEAD), F32)
        wab, wxb = wa_ref[...].astype(BF16), wx_ref[...].astype(BF16)
        lam_v = lam_ref[...]
        sp = _softplus_neg(lam_v)

        def conv_in(c):
            base = c * tc
            wins = [pad_s[pl.ds(base + pad - (CONV_TAPS - 1) + i, tc), :] for i in range(CONV_TAPS)]
            xc = bc_ref[...] + sum(wc_ref[i:i + 1, :] * wins[i] for i in range(CONV_TAPS))
            return xc, wins

        for c in range(s // tc):
            sl = slice(c * tc, (c + 1) * tc)
            xc, _ = conv_in(c)
            r, ig, a, mult = _lru_gates(xc, wab, ba_ref[...], wxb, bx_ref[...], sp)
            a_s[sl, :] = a
            xc_s[sl, :], r_s[sl, :], ig_s[sl, :], mult_s[sl, :] = xc, r, ig, mult
            gel, dgel = _gelu_parts(xg_ref[sl, :])
            dov = do_ref[sl, :]
            g_s[sl, :] = dov * gel
            dxg_ref[sl, :] = (dov * h_ref[sl, :] * dgel).astype(BF16)

        row = lax.broadcasted_iota(jnp.int32, (SUBLANES, HEAD), 0)
        n_chunks = s // SUBLANES

        def group(it, gnext):
            offs = [pl.multiple_of((n_chunks - 1 - (it * SCAN_GROUP + q)) * SUBLANES, SUBLANES) for q in range(SCAN_GROUP)]
            cg = []
            for off in offs:
                a8 = a_s[pl.ds(off, SUBLANES), :]
                a8n = a_s[pl.ds(off + SUBLANES, SUBLANES), :]
                c8 = pltpu.roll(jnp.where(row == 0, a8n, a8), SUBLANES - 1, 0)
                g8 = g_s[pl.ds(off, SUBLANES), :]
                for d in (1, 2, 4):
                    c_sh = jnp.where(row >= SUBLANES - d, 1.0, pltpu.roll(c8, SUBLANES - d, 0))
                    g_sh = jnp.where(row >= SUBLANES - d, 0.0, pltpu.roll(g8, SUBLANES - d, 0))
                    g8 = c8 * g_sh + g8
                    c8 = c8 * c_sh
                cg.append((c8, g8))
            enters = []
            for c8, g8 in cg:
                enters.append(gnext)
                gnext = g8[0:1, :] + c8[0:1, :] * gnext
            for off, (c8, g8), g0 in zip(offs, cg, enters):
                gv = g8 + c8 * g0
                g_s[pl.ds(off, SUBLANES), :] = gv
                h8 = hp_s[pl.ds(off + pad, SUBLANES), :]
                h8p = hp_s[pl.ds(off, SUBLANES), :]
                da_s[pl.ds(off, SUBLANES), :] = gv * pltpu.roll(jnp.where(row == SUBLANES - 1, h8p, h8), 1, 0)
            return gnext

        lax.fori_loop(0, n_chunks // SCAN_GROUP, group, jnp.zeros((1, HEAD), F32))

        dsp = jnp.zeros((1, HEAD), F32)
        dbc = jnp.zeros((1, HEAD), F32)
        dba = jnp.zeros((1, HEAD), F32)
        dbx = jnp.zeros((1, HEAD), F32)
        dwa = jnp.zeros((HEAD, HEAD), F32)
        dwx = jnp.zeros((HEAD, HEAD), F32)
        dwc = [jnp.zeros((1, HEAD), F32) for _ in range(CONV_TAPS)]
        for c in range(s // tc):
            sl = slice(c * tc, (c + 1) * tc)
            wins = [pad_s[pl.ds(c * tc + pad - (CONV_TAPS - 1) + i, tc), :] for i in range(CONV_TAPS)]
            xc, r, ig, a, mult = xc_s[sl, :], r_s[sl, :], ig_s[sl, :], a_s[sl, :], mult_s[sl, :]
            du, da = g_s[sl, :], da_s[sl, :]
            d_ix = du * mult
            dla = da * a - (du * ig * xc) * (a * a / mult)
            dsp = dsp + jnp.sum(dla * r, axis=0, keepdims=True) * (-LRU_C)
            dpa = (dla * (-LRU_C * sp)) * r * (1.0 - r)
            dpx = (d_ix * xc) * ig * (1.0 - ig)
            dpab, dpxb, xb = dpa.astype(BF16), dpx.astype(BF16), xc.astype(BF16)
            dxc = d_ix * ig + _dot_nt(dpab, wab) + _dot_nt(dpxb, wxb)
            dwa = dwa + _dot_tn(xb, dpab)
            dwx = dwx + _dot_tn(xb, dpxb)
            dba = dba + jnp.sum(dpa, axis=0, keepdims=True)
            dbx = dbx + jnp.sum(dpx, axis=0, keepdims=True)
            dbc = dbc + jnp.sum(dxc, axis=0, keepdims=True)
            for i in range(CONV_TAPS):
                dwc[i] = dwc[i] + jnp.sum(dxc * wins[i], axis=0, keepdims=True)
            dxc_s[sl, :] = dxc

        for c in range(s // tc):
            base = c * tc
            dxr = sum(wc_ref[i:i + 1, :] * dxc_s[pl.ds(base + (CONV_TAPS - 1) - i, tc), :] for i in range(CONV_TAPS))
            dxr_ref[base:base + tc, :] = dxr.astype(BF16)

        for i in range(CONV_TAPS):
            dwc_ref[i:i + 1, :] = dwc[i]
        dbc_ref[...] = dbc
        dwa_ref[...] = dwa
        dwx_ref[...] = dwx
        dba_ref[...] = dba
        dbx_ref[...] = dbx
        dlam_ref[...] = dsp * (-_sigmoid(-lam_v))

    w = nb * HEAD
    return pl.pallas_call(
        _per_block(one_block, 20, (6, 8, 15, 17), bps), grid=(nb // bps,),
        in_specs=[seq0, seq1, seq0, seq0, taps, vec, mat, vec, mat, vec, vec],
        out_specs=(seq0, seq0, taps, vec, mat, vec, mat, vec, vec),
        out_shape=(SDS((s, w), BF16), SDS((s, w), BF16), SDS((CONV_TAPS, w), F32), SDS((1, w), F32),
                   SDS((nb, HEAD, HEAD), F32), SDS((1, w), F32), SDS((nb, HEAD, HEAD), F32), SDS((1, w), F32),
                   SDS((1, w), F32)),
        scratch_shapes=[pltpu.VMEM((s + pad, HEAD), F32), pltpu.VMEM((s + pad, HEAD), F32),
                        pltpu.VMEM((s + pad, HEAD), F32), pltpu.VMEM((s, HEAD), F32),
                        pltpu.VMEM((s, HEAD), F32), pltpu.VMEM((s + pad, HEAD), F32)]
                       + [pltpu.VMEM((s, HEAD), F32)] * 4,
        compiler_params=_params(("parallel",)), name="lru_bwd",
    )(xrg, xrg, dol, hseq, wconv, bconv, wa, ba, wx, bx, lam)


def _ada_mod(c_all, w_sh, b_sh):
    n_ex, d = c_all.shape
    n = w_sh.shape[1]
    tn = _tile(n, 512)

    def body(c_ref, w_ref, b_ref, mod_ref, act_ref):
        cv = c_ref[...]
        act = cv * _sigmoid(cv)
        act_ref[...] = act
        mod_ref[...] = _dot(act.astype(BF16), w_ref[...].astype(BF16)) + b_ref[...]

    return pl.pallas_call(
        body, grid=(n // tn,),
        in_specs=[pl.BlockSpec((n_ex, d), lambda j: (0, 0)), pl.BlockSpec((d, tn), lambda j: (0, j)),
                  pl.BlockSpec((1, tn), lambda j: (0, j))],
        out_specs=(pl.BlockSpec((n_ex, tn), lambda j: (0, j)), pl.BlockSpec((n_ex, d), lambda j: (0, 0))),
        out_shape=(SDS((n_ex, n), F32), SDS((n_ex, d), F32)),
        compiler_params=_params(("arbitrary",)), name="ada_mod")(c_all, w_sh, b_sh)


def _adamw_math(w, g, m, v):
    m = ADAM_B1 * m + (1.0 - ADAM_B1) * g
    v = ADAM_B2 * v + (1.0 - ADAM_B2) * (g * g)
    m_hat = m / (1.0 - ADAM_B1 ** ADAM_STEP)
    v_hat = v / (1.0 - ADAM_B2 ** ADAM_STEP)
    delta = -ADAM_LR * (m_hat / (jnp.sqrt(v_hat) + ADAM_EPS) + ADAM_WD * w)
    return delta, m, v


def _adamw_plain(name, w, g, m, v):
    def body(w_ref, g_ref, m_ref, v_ref, d_ref, mo_ref, vo_ref):
        d_ref[...], mo_ref[...], vo_ref[...] = _adamw_math(w_ref[...], g_ref[...], m_ref[...], v_ref[...])

    return pl.pallas_call(body, out_shape=(SDS(w.shape, F32),) * 3, name=name)(w, g, m, v)


def _adamw_halves(name, c_arr, w, m, v, g_own, g_recv):
    r, n = w.shape
    rh = r // 2
    tr = _tile(rh, 256)
    nh = rh // tr

    def body(c_ref, w_ref, m_ref, v_ref, go_ref, gr_ref, g_ref, d_ref, mo_ref, vo_ref):
        own = (pl.program_id(0) // nh) == c_ref[0]
        g = jnp.where(own, go_ref[...], gr_ref[...])
        g_ref[...] = g
        d_ref[...], mo_ref[...], vo_ref[...] = _adamw_math(w_ref[...], g, m_ref[...], v_ref[...])

    full = pl.BlockSpec((tr, n), lambda i, c: (i, 0))
    own = pl.BlockSpec((tr, n), lambda i, c: (jnp.where(i // nh == c[0], i % nh, 0), 0))
    recv = pl.BlockSpec((tr, n), lambda i, c: (jnp.where(i // nh == c[0], 0, i % nh), 0))
    return pl.pallas_call(
        body,
        grid_spec=pltpu.PrefetchScalarGridSpec(
            num_scalar_prefetch=1, grid=(2 * nh,), in_specs=[full, full, full, own, recv],
            out_specs=(full,) * 4),
        out_shape=(SDS((r, n), F32),) * 4,
        compiler_params=_params(("parallel",)), name=name)(c_arr, w, m, v, g_own, g_recv)


def _adamw_ada(w, m, v, act_t, dmod):
    d, n = w.shape
    n_ex = act_t.shape[1]
    tr = _tile(d, 256)

    def body(a_ref, dm_ref, w_ref, m_ref, v_ref, g_ref, d_ref, mo_ref, vo_ref):
        g = _dot(a_ref[...], dm_ref[...])
        g_ref[...] = g
        d_ref[...], mo_ref[...], vo_ref[...] = _adamw_math(w_ref[...], g, m_ref[...], v_ref[...])

    full = pl.BlockSpec((tr, n), lambda i: (i, 0))
    return pl.pallas_call(
        body, grid=(d // tr,),
        in_specs=[pl.BlockSpec((tr, n_ex), lambda i: (i, 0)), pl.BlockSpec((n_ex, n), lambda i: (0, 0)), full, full, full],
        out_specs=(full,) * 4, out_shape=(SDS((d, n), F32),) * 4,
        compiler_params=_params(("parallel",)), name="adamw_ada")(act_t, dmod, w, m, v)


def _small_reduce_adamw(parts, w, m, v):
    n_dev, r, _ = parts.shape
    tr = r if r <= PACK_ROWS else PACK_ROWS

    def body(p_ref, w_ref, m_ref, v_ref, g_ref, d_ref, mo_ref, vo_ref):
        g = p_ref[0]
        for k in range(1, n_dev):
            g = g + p_ref[k]
        g_ref[...] = g
        d_ref[...], mo_ref[...], vo_ref[...] = _adamw_math(w_ref[...], g, m_ref[...], v_ref[...])

    full = pl.BlockSpec((tr, LANES), lambda i: (i, 0))
    return pl.pallas_call(
        body, grid=(r // tr,),
        in_specs=[pl.BlockSpec((n_dev, tr, LANES), lambda i: (0, i, 0)), full, full, full],
        out_specs=(full,) * 4, out_shape=(SDS((r, LANES), F32),) * 4,
        compiler_params=_params(("parallel",)), name="small_reduce_adamw")(parts, w, m, v)


def _mesh_pos():
    return lax.axis_index("x"), lax.axis_index("y"), lax.axis_index("c")


def _other_chips(x, y):
    return [(1 - x, y), (x, 1 - y), (1 - x, 1 - y)]


def _all_gather_small(name, blk, after=()):
    r, n = blk.shape
    n_after = len(after)

    def body(x_ref, *rest):
        out_ref, send_sems, recv_sems, local_sem = rest[n_after:]
        x, y, c = _mesh_pos()
        me, sibling = (x, y, c), (x, y, 1 - c)
        chips = _other_chips(x, y)

        def rows(px, py, pc):
            return out_ref.at[4 * px + 2 * py + pc]

        def copy(k, block, to, src=None):
            return pltpu.make_async_remote_copy(
                src_ref=rows(*block) if src is None else src, dst_ref=rows(*block),
                send_sem=send_sems.at[k], recv_sem=recv_sems.at[k], device_id=to, device_id_type=MESH)

        mine = pltpu.make_async_copy(x_ref, rows(*me), local_sem)
        mine.start()
        first = [copy(0, me, sibling, src=x_ref)]
        first += [copy(1 + j, me, (*chip, c), src=x_ref) for j, chip in enumerate(chips)]
        for cp in first:
            cp.start()
        passed = [copy(4 + j, (*chip, c), sibling) for j, chip in enumerate(chips)]
        for j, chip in enumerate(chips):
            copy(1 + j, (*chip, c), me).wait_recv()
            passed[j].start()
        copy(0, sibling, me).wait_recv()
        for j, chip in enumerate(chips):
            copy(4 + j, (*chip, 1 - c), me).wait_recv()
        for cp in first + passed:
            cp.wait_send()
        mine.wait()

    return pl.pallas_call(
        body, out_shape=SDS((N_DEV, r, n), blk.dtype),
        in_specs=[pl.BlockSpec(memory_space=pltpu.VMEM)] + [pl.BlockSpec(memory_space=pl.ANY)] * n_after,
        out_specs=pl.BlockSpec(memory_space=pltpu.VMEM),
        scratch_shapes=[pltpu.SemaphoreType.DMA((7,)), pltpu.SemaphoreType.DMA((7,)), pltpu.SemaphoreType.DMA],
        compiler_params=pltpu.CompilerParams(vmem_limit_bytes=VMEM_LIMIT), name=name)(blk, *after)


_ANY = pl.BlockSpec(memory_space=pl.ANY)
_HBM = pl.BlockSpec(memory_space=pltpu.HBM)
_SEM = pl.BlockSpec(memory_space=pltpu.SEMAPHORE)
_EFFECT = pltpu.SideEffectType.DATAFLOW_SIDE_EFFECTING


def _hbm(a):
    return pltpu.with_memory_space_constraint(a, pltpu.HBM)


def _place_cast(name, j_arr, shard, kind, after):
    r, n = shard.shape
    tr = _tile(r, 256)
    nr = r // tr
    if kind == "col":
        out_shape, o_spec = (r, N_CHIP * n), pl.BlockSpec((tr, n), lambda i, j: (i, j[0]))
    else:
        out_shape, o_spec = (N_CHIP * r, n), pl.BlockSpec((tr, n), lambda i, j: (j[0] * nr + i, 0))

    def body(j_ref, s_ref, after_ref, o_ref, tok_ref):
        o_ref[...] = s_ref[...].astype(BF16)
        tok_ref[...] = jnp.zeros_like(tok_ref)

    return pl.pallas_call(
        body,
        grid_spec=pltpu.PrefetchScalarGridSpec(
            num_scalar_prefetch=1, grid=(nr,), in_specs=[pl.BlockSpec((tr, n), lambda i, j: (i, 0)), _ANY],
            out_specs=(o_spec, pl.BlockSpec((SUBLANES, LANES), lambda i, j: (0, 0)))),
        out_shape=(SDS(out_shape, BF16), SDS((SUBLANES, LANES), F32)),
        compiler_params=_params(("arbitrary",)), name=name)(j_arr, shard, after)


def _leg_direct(full, kind, x, y, c):
    mine = _full_region(full, kind, x, y, c)
    return [(mine, mine, (1 - x, y, c)), (mine, mine, (x, 1 - y, c))]


def _leg_relay(full, kind, x, y, c):
    fx, fy = jnp.where(c == 0, 1 - x, x), jnp.where(c == 0, y, 1 - y)
    tx, ty = jnp.where(c == 0, x, 1 - x), jnp.where(c == 0, 1 - y, y)
    got = _full_region(full, kind, fx, fy, c)
    return [(got, got, (tx, ty, c))]


def _leg_d2d(which):
    def leg(full, kind, x, y, c):
        chips = _other_chips(x, y)
        return [(_full_region(full, kind, *chips[k], c), _full_region(full, kind, *chips[k], c), (x, y, 1 - c))
                for k in which]
    return leg


_LEGS = {"direct": (_leg_direct, 2), "relay": (_leg_relay, 1), "d2d_near": (_leg_d2d((0, 1)), 2),
         "d2d_far": (_leg_d2d((2,)), 1)}


def _gather_call(name, fulls, kinds, waits, starts, after, thru):
    nw, n_wait, n_start = len(fulls), len(waits), len(starts)

    def body(*refs):
        wait_sems = refs[nw:nw + 2 * n_wait]
        outs = refs[nw + 2 * n_wait + 2:]
        full, start_sems = outs[:nw], outs[nw:nw + 2 * n_start]
        x, y, c = _mesh_pos()
        for i, (leg, ws, _, _) in enumerate(waits):
            fn, per = _LEGS[leg]
            for li, w in enumerate(ws):
                for k, (s_, d_, dev) in enumerate(fn(full[w], kinds[w], x, y, c)):
                    cp = pltpu.make_async_remote_copy(
                        src_ref=s_, dst_ref=d_, send_sem=wait_sems[2 * i].at[per * li + k],
                        recv_sem=wait_sems[2 * i + 1].at[per * li + k], device_id=dev, device_id_type=MESH)
                    cp.wait_recv()
                    cp.wait_send()
        for i, (leg, ws) in enumerate(starts):
            fn, per = _LEGS[leg]
            for li, w in enumerate(ws):
                for k, (s_, d_, dev) in enumerate(fn(full[w], kinds[w], x, y, c)):
                    pltpu.make_async_remote_copy(
                        src_ref=s_, dst_ref=d_, send_sem=start_sems[2 * i].at[per * li + k],
                        recv_sem=start_sems[2 * i + 1].at[per * li + k], device_id=dev, device_id_type=MESH).start()

    sems = []
    for leg, ws in starts:
        sems += [pltpu.SemaphoreType.DMA((_LEGS[leg][1] * len(ws),))] * 2
    wait_args = []
    for _, _, s_, r_ in waits:
        wait_args += [s_, r_]
    outs = pl.pallas_call(
        body,
        out_shape=tuple(pltpu.HBM(f_.shape, f_.dtype) for f_ in fulls) + tuple(sems) + (SDS(thru.shape, thru.dtype),),
        in_specs=[_HBM] * nw + [_SEM] * (2 * n_wait) + [_ANY, _ANY],
        out_specs=tuple([_HBM] * nw + [_SEM] * (2 * n_start) + [_ANY]),
        input_output_aliases={**{w: w for w in range(nw)}, nw + 2 * n_wait + 1: nw + 2 * n_start},
        compiler_params=pltpu.CompilerParams(has_side_effects=_EFFECT),
        name=name,
    )(*[_hbm(f_) for f_ in fulls], *wait_args, after, thru)
    pairs = [(outs[nw + 2 * i], outs[nw + 2 * i + 1]) for i in range(n_start)]
    return list(outs[:nw]), pairs, outs[nw + 2 * n_start]


def _full_region(full, kind, px, py, half):
    j = 2 * px + py
    if kind == "col":
        rh, cols = full.shape[0] // 2, full.shape[1] // N_CHIP
        return full.at[pl.ds(half * rh, rh), pl.ds(j * cols, cols)]
    rows = full.shape[0] // N_CHIP
    rh = rows // 2
    return full.at[pl.ds(j * rows + half * rh, rh), :]


def _plan_scatter(kind):
    def plan(src, land, x, y, c):
        out = []
        for k, (px, py) in enumerate(_other_chips(x, y)):
            j = 2 * px + py
            if kind == "col":
                n = src.shape[1] // N_CHIP
                blk = src.at[:, pl.ds(j * n, n)]
            else:
                blk = src.at[j]
            out.append((blk, land.at[k], (px, py, c)))
        return out
    return plan


def _plan_whole(src, land, x, y, c):
    return [(src, land, (x, y, 1 - c))]


def _split_start(name, src, land_shape, n, plan, thru):
    def body(src_in, land_in, thru_in, send, recv, src_ref, land_ref, thru_out):
        x, y, c = _mesh_pos()
        for k, (s_, d_, dev) in enumerate(plan(src_ref, land_ref, x, y, c)):
            pltpu.make_async_remote_copy(src_ref=s_, dst_ref=d_, send_sem=send.at[k], recv_sem=recv.at[k],
                                         device_id=dev, device_id_type=MESH).start()

    sem = pltpu.SemaphoreType.DMA((n,))
    return pl.pallas_call(
        body,
        out_shape=(sem, sem, pltpu.HBM(src.shape, src.dtype), pltpu.HBM(land_shape, src.dtype), SDS(thru.shape, thru.dtype)),
        in_specs=[_HBM, _HBM, _ANY], out_specs=(_SEM, _SEM, _HBM, _HBM, _ANY),
        input_output_aliases={0: 2, 1: 3, 2: 4},
        compiler_params=pltpu.CompilerParams(has_side_effects=_EFFECT), name=name,
    )(_hbm(src), _hbm(lax.empty(land_shape, src.dtype)), thru)


def _split_wait(name, send, recv, src, land, plan, after):
    def body(src_in, land_in, send_r, recv_r, after_r, src_ref, land_ref):
        x, y, c = _mesh_pos()
        for k, (s_, d_, dev) in enumerate(plan(src_ref, land_ref, x, y, c)):
            cp = pltpu.make_async_remote_copy(src_ref=s_, dst_ref=d_, send_sem=send_r.at[k], recv_sem=recv_r.at[k],
                                              device_id=dev, device_id_type=MESH)
            cp.wait_send()
            cp.wait_recv()

    return pl.pallas_call(
        body,
        out_shape=(pltpu.HBM(src.shape, src.dtype), pltpu.HBM(land.shape, land.dtype)),
        in_specs=[_HBM, _HBM, _SEM, _SEM, _ANY], out_specs=(_HBM, _HBM),
        input_output_aliases={0: 0, 1: 1},
        compiler_params=pltpu.CompilerParams(has_side_effects=_EFFECT), name=name,
    )(src, land, send, recv, after)


def _dev_row(buf, px, py, pc):
    return buf.at[4 * px + 2 * py + pc]


def _plan_gather_own(buf, land, x, y, c):
    own = _dev_row(buf, x, y, c)
    return [(own, own, (x, y, 1 - c))] + [(own, own, (px, py, c)) for px, py in _other_chips(x, y)]


def _plan_gather_pass(buf, land, x, y, c):
    return [(_dev_row(buf, px, py, c), _dev_row(buf, px, py, c), (x, y, 1 - c)) for px, py in _other_chips(x, y)]


def _split_start_inplace(name, buf, n, plan, thru):
    def body(buf_in, thru_in, send, recv, buf_ref, thru_out):
        x, y, c = _mesh_pos()
        for k, (s_, d_, dev) in enumerate(plan(buf_ref, buf_ref, x, y, c)):
            pltpu.make_async_remote_copy(src_ref=s_, dst_ref=d_, send_sem=send.at[k], recv_sem=recv.at[k],
                                         device_id=dev, device_id_type=MESH).start()

    sem = pltpu.SemaphoreType.DMA((n,))
    return pl.pallas_call(
        body, out_shape=(sem, sem, pltpu.HBM(buf.shape, buf.dtype), SDS(thru.shape, thru.dtype)),
        in_specs=[_HBM, _ANY], out_specs=(_SEM, _SEM, _HBM, _ANY), input_output_aliases={0: 2, 1: 3},
        compiler_params=pltpu.CompilerParams(has_side_effects=_EFFECT), name=name)(_hbm(buf), thru)


def _split_wait_inplace(name, send, recv, buf, plan, after):
    def body(buf_in, send_r, recv_r, after_r, buf_ref):
        x, y, c = _mesh_pos()
        for k, (s_, d_, dev) in enumerate(plan(buf_ref, buf_ref, x, y, c)):
            cp = pltpu.make_async_remote_copy(src_ref=s_, dst_ref=d_, send_sem=send_r.at[k], recv_sem=recv_r.at[k],
                                              device_id=dev, device_id_type=MESH)
            cp.wait_send()
            cp.wait_recv()

    return pl.pallas_call(
        body, out_shape=pltpu.HBM(buf.shape, buf.dtype), in_specs=[_HBM, _SEM, _SEM, _ANY], out_specs=_HBM,
        input_output_aliases={0: 0}, compiler_params=pltpu.CompilerParams(has_side_effects=_EFFECT), name=name,
    )(buf, send, recv, after)


def _place_row(name, me_arr, slab):
    r, n = slab.shape
    tr = r if r <= PACK_ROWS else PACK_ROWS

    def body(me_ref, s_ref, o_ref):
        o_ref[...] = s_ref[...]

    return pl.pallas_call(
        body,
        grid_spec=pltpu.PrefetchScalarGridSpec(
            num_scalar_prefetch=1, grid=(r // tr,), in_specs=[pl.BlockSpec((tr, n), lambda i, me: (i, 0))],
            out_specs=pl.BlockSpec((None, tr, n), lambda i, me: (me[0], i, 0))),
        out_shape=SDS((N_DEV, r, n), slab.dtype), compiler_params=_params(("parallel",)), name=name)(me_arr, slab)


def _sum_partials(name, j_arr, part, got, kind):
    _, rh, n = got.shape
    tr = _tile(rh, 256)
    if kind == "col":
        p_spec = pl.BlockSpec((tr, n), lambda i, j: (i, j[0]))
    else:
        p_spec = pl.BlockSpec((None, tr, n), lambda i, j: (j[0], i, 0))

    def body(j_ref, p_ref, r_ref, o_ref):
        o_ref[...] = ((p_ref[...].astype(F32) + r_ref[0].astype(F32)) + r_ref[1].astype(F32)) + r_ref[2].astype(F32)

    return pl.pallas_call(
        body,
        grid_spec=pltpu.PrefetchScalarGridSpec(
            num_scalar_prefetch=1, grid=(rh // tr,),
            in_specs=[p_spec, pl.BlockSpec((3, tr, n), lambda i, j: (0, i, 0))],
            out_specs=pl.BlockSpec((tr, n), lambda i, j: (i, 0))),
        out_shape=SDS((rh, n), F32),
        compiler_params=_params(("parallel",)), name=name)(j_arr, part, got)


def _pack(arrays):
    flat = [a.reshape(-1).astype(F32) for a in arrays]
    flat = [jnp.pad(f, (0, (-f.shape[0]) % LANES)) for f in flat]
    sizes = [f.shape[0] for f in flat]
    total = sum(sizes)
    rows = total // LANES
    tail = LANES * ((-rows) % (PACK_ROWS if rows > PACK_ROWS else SUBLANES))
    if tail:
        flat.append(jnp.zeros((tail,), F32))
    return jnp.concatenate(flat).reshape(-1, LANES), sizes


def _unpack(slab, sizes, shapes, lead=()):
    flat = slab.reshape(lead + (-1,))
    out, off = [], 0
    for sz, shp in zip(sizes, shapes):
        n = math.prod(shp)
        out.append(flat[..., off:off + n].reshape(lead + tuple(shp)))
        off += sz
    return out


def kernel(x, c, w_ada, b_ada, g_norm_mix, w_in, w_conv, b_conv, w_rg_a, b_rg_a, w_rg_x, b_rg_x, lru_lambda, g_attn_out, g_lru_out, w_out, g_norm_mlp, w_mlp_in, w_mlp_out, g_norm_final, loss_target, m_w_ada, m_b_ada, m_g_norm_mix, m_w_in, m_w_conv, m_b_conv, m_w_rg_a, m_b_rg_a, m_w_rg_x, m_b_rg_x, m_lru_lambda, m_g_attn_out, m_g_lru_out, m_w_out, m_g_norm_mlp, m_w_mlp_in, m_w_mlp_out, m_g_norm_final, v_w_ada, v_b_ada, v_g_norm_mix, v_w_in, v_w_conv, v_b_conv, v_w_rg_a, v_b_rg_a, v_w_rg_x, v_b_rg_x, v_lru_lambda, v_g_attn_out, v_g_lru_out, v_w_out, v_g_norm_mlp, v_w_mlp_in, v_w_mlp_out, v_g_norm_final):
    s, d = x.shape[1], x.shape[2]
    aw = d // 2
    nh = aw // HEAD
    f = w_mlp_out.shape[1] * N_CHIP
    n_ada = w_ada.shape[2]
    n_cv = w_conv.shape[2]
    ix, iy, ic = lax.axis_index("x"), lax.axis_index("y"), lax.axis_index("c")
    chip = 2 * ix + iy
    me = 2 * chip + ic
    c_arr = jnp.reshape(ic, (1,)).astype(jnp.int32)
    j_arr = jnp.reshape(chip, (1,)).astype(jnp.int32)

    x2d, tgt = x[0], loss_target[0]

    k_in, k_out, k_mi, k_mo = kinds = ("col", "row", "col", "row")
    me_arr = jnp.reshape(me, (1,)).astype(jnp.int32)
    slab, sizes = _pack([c, w_conv])
    c_buf = _place_row("place_cond", me_arr, slab)
    cg_send, cg_recv, c_buf, tok = _split_start_inplace("cond_gather_start", c_buf, N_CHIP, _plan_gather_own, slab)

    p_in, tok = _place_cast("place_cast_0", j_arr, w_in[0], k_in, tok)
    (f_in,), (dir_in,), tok = _gather_call("gather_0", [p_in], [k_in], [], [("direct", [0])], c, tok)
    c_buf = _split_wait_inplace("cond_gather_wait", cg_send, cg_recv, c_buf, _plan_gather_own, tok)
    cp_send, cp_recv, c_buf, tok = _split_start_inplace("cond_pass_start", c_buf, N_CHIP - 1, _plan_gather_pass, tok)
    gathered = _split_wait_inplace("cond_pass_wait", cp_send, cp_recv, c_buf, _plan_gather_pass, tok)
    c_parts, cv_parts = _unpack(gathered, sizes, [(d,), (CONV_TAPS, n_cv)], lead=(N_DEV,))
    c_all = c_parts
    w_conv_full = jnp.concatenate([cv_parts[2 * j] for j in range(N_CHIP)], axis=-1)
    b_sh = lax.dynamic_slice(b_ada, (0, chip * n_ada), (1, n_ada))
    mod_part, act_all = _ada_mod(c_all, w_ada[0], b_sh)
    p_out, tok = _place_cast("place_cast_1", j_arr, w_out[0], k_out, mod_part)
    p_mi, tok = _place_cast("place_cast_2", j_arr, w_mlp_in[0], k_mi, tok)
    mod_g = _all_gather_small("comm_gather_mod", mod_part.reshape(-1, LANES), after=(tok,))
    mod_g = mod_g.reshape(N_DEV, N_DEV, n_ada)
    mod = jnp.concatenate([lax.dynamic_index_in_dim(mod_g[2 * j], me, 0, keepdims=True) for j in range(N_CHIP)], axis=-1)
    sh1, sc1, gt1, sh2, sc2, gt2 = [mod[:, k * d:(k + 1) * d] for k in range(N_MOD)]
    (f_in,), (rel_in, near_in), sh1 = _gather_call(
        "gather_1", [f_in], [k_in], [("direct", [0], *dir_in)], [("relay", [0]), ("d2d_near", [0])], mod, sh1)

    h1, rstd1 = _norm_mod_fwd("norm_mod_fwd1", x2d, g_norm_mix, sc1, sh1)
    p_mo, tok = _place_cast("place_cast_3", j_arr, w_mlp_out[0], k_mo, rstd1)
    (f_in, f_out, f_mi), (far_in, dir_out, dir_mi), h1 = _gather_call(
        "gather_2", [f_in, p_out, p_mi], kinds[:3], [("relay", [0], *rel_in)],
        [("d2d_far", [0]), ("direct", [1]), ("direct", [2])], tok, h1)
    (w_in_f,), _, h1 = _gather_call("gather_3", [f_in], [k_in],
                                    [("d2d_near", [0], *near_in), ("d2d_far", [0], *far_in)], [], rstd1, h1)
    (qkv,) = _matmul("mm_qkv", h1, w_in_f, "nn", s, 3 * aw, d, (BF16,))
    (xrg,) = _matmul("mm_xrg", h1, w_in_f, "nn", s, 2 * aw, d, (F32,), b_off=3 * aw)
    (f_out, f_mo), (rel_out, near_out, dir_mo), qkv = _gather_call(
        "gather_3b", [f_out, p_mo], [k_out, k_mo], [("direct", [0], *dir_out)],
        [("relay", [0]), ("d2d_near", [0]), ("direct", [1])], xrg, qkv)
    o_attn, attn_w, attn_sg = _attn_fwd(qkv, nh)
    (f_out, f_mi), (far_out, rel_mi, near_mi), xrg = _gather_call(
        "gather_4", [f_out, f_mi], [k_out, k_mi], [("relay", [0], *rel_out), ("direct", [1], *dir_mi)],
        [("d2d_far", [0]), ("relay", [1]), ("d2d_near", [1])], o_attn, xrg)
    wa3, wx3 = w_rg_a[0], w_rg_x[0]
    o_lru, hseq = _lru_fwd(xrg, w_conv_full, b_conv, wa3, b_rg_a, wx3, b_rg_x, lru_lambda)
    mixed, rstd_a, rstd_l = _mix_norm_fwd(o_attn, o_lru, g_attn_out, g_lru_out)
    (f_mo,), (rel_mo, near_mo), mixed = _gather_call(
        "gather_5", [f_mo], [k_mo], [("direct", [0], *dir_mo)], [("relay", [0]), ("d2d_near", [0])], rstd_a, mixed)
    (w_out_f,), _, mixed = _gather_call(
        "gather_6", [f_out], [k_out], [("d2d_near", [0], *near_out), ("d2d_far", [0], *far_out)], [], rstd_l, mixed)

    def residual(acc, xin, gt):
        return acc, xin + gt * acc

    y1, x1 = _matmul("mm_out", mixed, w_out_f, "nn", s, d, d, (BF16, F32), extras=(x2d, gt1),
                     extra_kinds=("tile", "row"), epilogue=residual)
    (f_mi,), (far_mi,), x1 = _gather_call("gather_6b", [f_mi], [k_mi], [("relay", [0], *rel_mi)],
                                          [("d2d_far", [0])], y1, x1)
    h2, rstd2 = _norm_mod_fwd("norm_mod_fwd2", x1, g_norm_mlp, sc2, sh2)
    (w_mi_f, f_mo), (far_mo,), h2 = _gather_call(
        "gather_7", [f_mi, f_mo], [k_mi, k_mo],
        [("d2d_near", [0], *near_mi), ("d2d_far", [0], *far_mi), ("relay", [1], *rel_mo)], [("d2d_far", [1])], rstd2, h2)

    def sq_relu(acc):
        r = jnp.maximum(acc, 0.0)
        return 2.0 * r, r * r

    r2, hid = _matmul("mm_mlp_in", h2, w_mi_f, "nn", s, f, d, (BF16, BF16), epilogue=sq_relu)
    (w_mo_f,), _, hid = _gather_call("gather_8", [f_mo], [k_mo],
                                     [("d2d_near", [0], *near_mo), ("d2d_far", [0], *far_mo)], [], r2, hid)
    y2, x2 = _matmul("mm_mlp_out", hid, w_mo_f, "nn", s, d, f, (BF16, F32), extras=(x1, gt2),
                     extra_kinds=("tile", "row"), epilogue=residual)
    dx2, loss_row, dg_final, dy2, dgt2 = _final_loss(x2, g_norm_final.reshape(1, d), tgt, y2, gt2)

    oc_arr = 1 - c_arr

    def dw_half(name, st, h_arr, got=None):
        add = {} if got is None else dict(extras=(got,), extra_kinds=("tile",),
                                          epilogue=lambda acc, g_: (acc + g_.astype(F32),))
        tn = st["n"] if st["tm"] * 4 <= MM_TILE_M else MM_TILE_N
        (out,) = _matmul(name, st["a"], st["dy"], "tn", st["m"], st["n"], s, (BF16,), tm=st["tm"], tn=tn,
                         m_half=h_arr, **add)
        return out

    def rs_begin(tag, kind, xa, dy, m, n, thru):
        st = {"tag": tag, "kind": kind, "a": xa, "dy": dy, "m": m, "n": n,
              "tm": m // (2 * N_CHIP) if kind == "row" else m // 2}
        first = dw_half("mm_dw_%s_a" % tag, st, oc_arr)
        send, recv, first, land, thru = _split_start("rs_swap_start_" + tag, first, first.shape, 1, _plan_whole, thru)
        st["swap"] = (send, recv, first, land)
        return st, thru

    def rs_mid(st, after, thru):
        tag, kind = st["tag"], st["kind"]
        _, got = _split_wait("rs_swap_wait_" + tag, *st["swap"], _plan_whole, after)
        part = dw_half("mm_dw_%s_b" % tag, st, c_arr, got)
        if kind == "row":
            part = part.reshape(N_CHIP, st["m"] // (2 * N_CHIP), st["n"])
        blk = (part.shape[0], part.shape[1] // N_CHIP) if kind == "col" else part.shape[1:]
        send, recv, part, land, thru = _split_start("rs_scatter_start_" + tag, part, (N_CHIP - 1,) + blk, N_CHIP - 1,
                                                    _plan_scatter(kind), thru)
        st["scatter"] = (send, recv, part, land)
        return thru

    def rs_end(st, after):
        tag, kind = st["tag"], st["kind"]
        part, got = _split_wait("rs_scatter_wait_" + tag, *st["scatter"], _plan_scatter(kind), after)
        return _sum_partials("sum_partials_" + tag, j_arr, part, got, kind)

    (dpre,) = _matmul("mm_dhid", dy2, w_mo_f, "nt", s, f, d, (BF16,), extras=(r2,), extra_kinds=("tile",),
                      epilogue=lambda acc, r: (acc * r.astype(F32),))
    st_mo, dpre = rs_begin("mo", "row", hid, dy2, f, d, dpre)
    (dh2,) = _matmul("mm_dh2", dpre, w_mi_f, "nt", s, d, f, (BF16,))
    dh2 = rs_mid(st_mo, dh2, dh2)
    st_mi, dh2 = rs_begin("mi", "col", h2, dpre, d, f, dh2)
    dx1, dsh2, dsc2, dg_mlp, dy1, dgt1 = _norm_mod_bwd("norm_mod_bwd2", dh2, x1, rstd2, g_norm_mlp, sc2, dx2,
                                                       gate=(y1, gt1))
    (dmixed,) = _matmul("mm_dmixed", dy1, w_out_f, "nt", s, d, d, (BF16,))
    dmixed = rs_mid(st_mi, dmixed, dmixed)
    st_out, dmixed = rs_begin("out", "row", mixed, dy1, d, d, dmixed)
    do_attn, do_lru, dg_attn, dg_lru = _mix_norm_bwd(dmixed, o_attn, o_lru, rstd_a, rstd_l, g_attn_out, g_lru_out)
    dq, dk, dv = _attn_bwd(qkv, do_attn, attn_w, attn_sg, nh)
    do_lru = rs_mid(st_out, dq, do_lru)
    dxr, dxg, dwconv, dbconv, dwa, dba, dwx, dbx, dlam = _lru_bwd(
        xrg, do_lru, hseq, w_conv_full, b_conv, wa3, b_rg_a, wx3, b_rg_x, lru_lambda)
    dproj = jnp.concatenate([dq, dk, dv, dxr, dxg], axis=-1)
    st_in, dproj = rs_begin("in", "col", h1, dproj, d, 5 * aw, dproj)
    st_in["dy"] = dproj
    (dh1,) = _matmul("mm_dh1", dproj, w_in_f, "nt", s, d, 5 * aw, (BF16,))
    dh1 = rs_mid(st_in, dh1, dh1)
    grad_x, dsh1, dsc1, dg_mix = _norm_mod_bwd("norm_mod_bwd1", dh1, x2d, rstd1, g_norm_mix, sc1, dx1)

    dmod = jnp.concatenate([dsh1, dsc1, dgt1, dsh2, dsc2, dgt2], axis=-1)
    small_names = ["b_ada", "g_norm_mix", "b_conv", "w_rg_a", "b_rg_a", "w_rg_x", "b_rg_x", "lru_lambda",
                   "g_attn_out", "g_lru_out", "g_norm_mlp", "g_norm_final"]
    small_g = [dmod, dg_mix, dbconv, dwa, dba, dwx, dbx, dlam, dg_attn, dg_lru, dg_mlp, dg_final]
    small_w = [b_ada, g_norm_mix, b_conv, w_rg_a, b_rg_a, w_rg_x, b_rg_x, lru_lambda, g_attn_out, g_lru_out, g_norm_mlp, g_norm_final]
    small_m = [m_b_ada, m_g_norm_mix, m_b_conv, m_w_rg_a, m_b_rg_a, m_w_rg_x, m_b_rg_x, m_lru_lambda, m_g_attn_out, m_g_lru_out, m_g_norm_mlp, m_g_norm_final]
    small_v = [v_b_ada, v_g_norm_mix, v_b_conv, v_w_rg_a, v_b_rg_a, v_w_rg_x, v_b_rg_x, v_lru_lambda, v_g_attn_out, v_g_lru_out, v_g_norm_mlp, v_g_norm_final]
    extra_zero = [jnp.zeros_like(dwconv), jnp.zeros((LANES,), F32)]
    g_slab, g_sizes = _pack(small_g + [dwconv, loss_row])
    w_slab, _ = _pack(small_w + extra_zero)
    m_slab, _ = _pack(small_m + extra_zero)
    v_slab, _ = _pack(small_v + extra_zero)
    g_buf = _place_row("place_small_grads", me_arr, g_slab)
    sg_send, sg_recv, g_buf, tok = _split_start_inplace("sg_gather_start", g_buf, N_CHIP, _plan_gather_own, loss_row)

    def reduced_begin(tag, half, tok_):
        send, recv, half, land, tok_ = _split_start("rs_reduced_start_" + tag, half, half.shape, 1, _plan_whole, tok_)
        return (send, recv, half, land), tok_

    def reduced_end(tag, st, after):
        return _split_wait("rs_reduced_wait_" + tag, *st, _plan_whole, after)

    sw_mo, tok = reduced_begin("mo", rs_end(st_mo, tok), tok)
    sw_mi, tok = reduced_begin("mi", rs_end(st_mi, tok), tok)
    sw_out, tok = reduced_begin("out", rs_end(st_out, tok), tok)
    half_mo, got_mo = reduced_end("mo", sw_mo, tok)
    big = {"w_mlp_out": _adamw_halves("adamw_w_mlp_out", c_arr, w_mlp_out[0], m_w_mlp_out[0], v_w_mlp_out[0],
                                      half_mo, got_mo)}
    half_mi, got_mi = reduced_end("mi", sw_mi, big["w_mlp_out"][1])
    big["w_mlp_in"] = _adamw_halves("adamw_w_mlp_in", c_arr, w_mlp_in[0], m_w_mlp_in[0], v_w_mlp_in[0], half_mi, got_mi)
    half_out, got_out = reduced_end("out", sw_out, big["w_mlp_in"][1])
    big["w_out"] = _adamw_halves("adamw_w_out", c_arr, w_out[0], m_w_out[0], v_w_out[0], half_out, got_out)
    g_buf = _split_wait_inplace("sg_gather_wait", sg_send, sg_recv, g_buf, _plan_gather_own, big["w_out"][1])
    sg_send, sg_recv, g_buf, tok = _split_start_inplace("sg_pass_start", g_buf, N_CHIP - 1, _plan_gather_pass, tok)
    sw_in, tok = reduced_begin("in", rs_end(st_in, tok), tok)
    g_all = _split_wait_inplace("sg_pass_wait", sg_send, sg_recv, g_buf, _plan_gather_pass, tok)
    gs_slab, ds_slab, ms_slab, vs_slab = _small_reduce_adamw(g_all, w_slab, m_slab, v_slab)
    shapes = [w.shape for w in small_w] + [dwconv.shape, (LANES,)]
    gs = _unpack(gs_slab, g_sizes, shapes)
    ds = _unpack(ds_slab, g_sizes, shapes)
    ms = _unpack(ms_slab, g_sizes, shapes)
    vs = _unpack(vs_slab, g_sizes, shapes)
    small = {n: (gs[i], ds[i], ms[i], vs[i]) for i, n in enumerate(small_names)}
    loss = gs[-1][0]
    g_wconv = lax.dynamic_slice(gs[-2], (0, chip * n_cv), (CONV_TAPS, n_cv))
    d_wconv, m_wconv, v_wconv = _adamw_plain("adamw_conv", w_conv[0], g_wconv, m_w_conv[0], v_w_conv[0])
    small["w_conv"] = (g_wconv[None], d_wconv[None], m_wconv[None], v_wconv[None])

    dmod_all = g_all[:, :N_MOD * d // LANES, :].reshape(N_DEV, N_MOD * d)
    dmod_sel = lax.dynamic_slice(dmod_all, (0, chip * n_ada), (N_DEV, n_ada)).astype(BF16)
    act_t = act_all.T.astype(BF16)
    big["w_ada"] = _adamw_ada(w_ada[0], m_w_ada[0], v_w_ada[0], act_t, dmod_sel)

    half_in, got_in = reduced_end("in", sw_in, big["w_ada"][1])
    big["w_in"] = _adamw_halves("adamw_w_in", c_arr, w_in[0], m_w_in[0], v_w_in[0], half_in, got_in)

    order = ["w_ada", "b_ada", "g_norm_mix", "w_in", "w_conv", "b_conv", "w_rg_a", "b_rg_a", "w_rg_x", "b_rg_x",
             "lru_lambda", "g_attn_out", "g_lru_out", "w_out", "g_norm_mlp", "w_mlp_in", "w_mlp_out", "g_norm_final"]
    res = {}
    for n in order:
        res[n] = tuple(t[None] for t in big[n]) if n in big else small[n]
    return (loss, grad_x[None],
            *[res[n][0] for n in order], *[res[n][1] for n in order],
            *[res[n][2] for n in order], *[res[n][3] for n in order])
```

```python
import functools
import math

import jax
import jax.numpy as jnp
from jax import lax
from jax.experimental import pallas as pl
from jax.experimental.pallas import tpu as pltpu

F32 = jnp.float32
BF16 = jnp.bfloat16
SDS = jax.ShapeDtypeStruct
MESH = pl.DeviceIdType.MESH

EPS = 1e-6
HEAD = 128
N_MOD = 6
CONV_TAPS = 4
LRU_C = 8.0
ADAM_LR, ADAM_B1, ADAM_B2, ADAM_EPS, ADAM_WD, ADAM_STEP = 0.001, 0.9, 0.999, 1e-08, 0.01, 10
N_DEV = 8
N_CHIP = 4
LANES = 128
SUBLANES = 8
VMEM_LIMIT = 56 * 1024 * 1024
PACK_ROWS = 256
MM_TILE_M, MM_TILE_N, MM_TILE_K = 1024, 1024, 2048
ROW_TILE = 256
ROW_SPLIT = 1


def _tile(dim, pref):
    t = min(dim, pref)
    while dim % t:
        t -= LANES
    return t


def _params(sem=None):
    return pltpu.CompilerParams(dimension_semantics=sem, vmem_limit_bytes=VMEM_LIMIT)


def _sigmoid(x):
    return 1.0 / (1.0 + jnp.exp(-x))


def _log_sigmoid(x):
    return jnp.minimum(x, 0.0) - jnp.log(1.0 + jnp.exp(-jnp.abs(x)))


def _gelu_parts(x):
    k0, k1 = math.sqrt(2.0 / math.pi), 0.044715
    t = jnp.tanh(k0 * (x + k1 * x * x * x))
    val = 0.5 * x * (1.0 + t)
    der = 0.5 * (1.0 + t) + 0.5 * x * (1.0 - t * t) * k0 * (1.0 + 3.0 * k1 * x * x)
    return val, der


def _dot(a, b):
    return jnp.dot(a, b, preferred_element_type=F32)


def _dot_nt(a, b):
    return lax.dot_general(a, b, (((1,), (1,)), ((), ())), preferred_element_type=F32)


def _dot_tn(a, b):
    return lax.dot_general(a, b, (((0,), (0,)), ((), ())), preferred_element_type=F32)


def _split_dot(x, tri):
    hi = x.astype(BF16)
    lo = (x - hi.astype(F32)).astype(BF16)
    return _dot(hi, tri) + _dot(lo, tri)


def _matmul(name, a, b, mode, m, n, k, out_dtypes, *, b_off=0, extras=(), extra_kinds=(), epilogue=None,
            tm=MM_TILE_M, tn=MM_TILE_N, tk=MM_TILE_K, m_half=None):
    tm, tn, tk = _tile(m, tm), _tile(math.gcd(n, b_off) if b_off else n, tn), _tile(k, tk)
    assert b_off % tn == 0
    nk = k // tk
    n_ex, n_out = len(extras), len(out_dtypes)
    dot = {"nn": _dot, "nt": _dot_nt, "tn": _dot_tn}[mode]
    n_pre = 0 if m_half is None else 1
    m_out = m if m_half is None else m // 2

    def body(*refs):
        a_ref, b_ref, *rest = refs[n_pre:]
        ex, outs = rest[:n_ex], rest[n_ex:n_ex + n_out]

        def finish(total):
            res = epilogue(total, *[e[...] for e in ex]) if epilogue else (total,)
            for o, r in zip(outs, res):
                o[...] = r.astype(o.dtype)

        if nk == 1:
            finish(dot(a_ref[...], b_ref[...]))
            return
        acc = rest[-1]
        kk = pl.program_id(2)

        @pl.when(kk == 0)
        def _():
            acc[...] = dot(a_ref[...], b_ref[...])

        @pl.when(jnp.logical_and(kk > 0, kk < nk - 1))
        def _():
            acc[...] += dot(a_ref[...], b_ref[...])

        @pl.when(kk == nk - 1)
        def _():
            finish(acc[...] + dot(a_ref[...], b_ref[...]))

    if mode == "nn":
        a_spec = pl.BlockSpec((tm, tk), lambda i, j, kk, *_: (i, kk))
        b_spec = pl.BlockSpec((tk, tn), lambda i, j, kk, *_: (kk, j + b_off // tn))
    elif mode == "nt":
        a_spec = pl.BlockSpec((tm, tk), lambda i, j, kk, *_: (i, kk))
        b_spec = pl.BlockSpec((tn, tk), lambda i, j, kk, *_: (j, kk + b_off // tk))
    elif m_half is None:
        a_spec = pl.BlockSpec((tk, tm), lambda i, j, kk: (kk, i))
        b_spec = pl.BlockSpec((tk, tn), lambda i, j, kk: (kk, j))
    else:
        a_spec = pl.BlockSpec((tk, tm), lambda i, j, kk, h: (kk, 2 * i + h[0]))
        b_spec = pl.BlockSpec((tk, tn), lambda i, j, kk, h: (kk, j))
    tile_spec = pl.BlockSpec((tm, tn), lambda i, j, kk, *_: (i, j))
    row_spec = pl.BlockSpec((1, tn), lambda i, j, kk, *_: (0, j))
    in_specs = [a_spec, b_spec] + [tile_spec if kind == "tile" else row_spec for kind in extra_kinds]
    out_specs = tuple(tile_spec for _ in out_dtypes)
    out_shape = tuple(SDS((m_out, n), dt) for dt in out_dtypes)
    scratch = [pltpu.VMEM((tm, tn), F32)] if nk > 1 else []
    grid = (m_out // tm, n // tn, nk)
    semantics = _params(("parallel", "parallel", "arbitrary"))
    if m_half is None:
        return pl.pallas_call(body, grid=grid, in_specs=in_specs, out_specs=out_specs, out_shape=out_shape,
                              scratch_shapes=scratch, compiler_params=semantics, name=name)(a, b, *extras)
    assert mode == "tn" and (m // tm) % 2 == 0
    return pl.pallas_call(
        body,
        grid_spec=pltpu.PrefetchScalarGridSpec(num_scalar_prefetch=1, grid=grid, in_specs=in_specs, out_specs=out_specs,
                                               scratch_shapes=scratch),
        out_shape=out_shape, compiler_params=semantics, name=name)(m_half, a, b, *extras)


def _row_specs(s, d, tr):
    row = "row"
    vec = pl.BlockSpec((1, d), lambda i: (0, 0))
    col = pl.BlockSpec((tr, 1), lambda i: (i, 0))
    return row, vec, col


class _ColChunks:
    def __init__(self, refs):
        self.refs = refs

    def __getitem__(self, idx):
        return jnp.concatenate([r[...] for r in self.refs], axis=-1)


def _rows_call(name, body, grid, in_specs, out_specs, out_shape, semantics, args):
    in_x, args_x, groups = [], [], []
    for spec, arr in zip(in_specs, args):
        if isinstance(spec, str):
            rows, d = arr.shape
            tr, dc = rows // grid[0], d // ROW_SPLIT
            in_x += [pl.BlockSpec((tr, dc), functools.partial(lambda i, jj: (i, jj), jj=j)) for j in range(ROW_SPLIT)]
            args_x += [arr] * ROW_SPLIT
            groups.append(ROW_SPLIT)
        else:
            in_x.append(spec)
            args_x.append(arr)
            groups.append(1)
    out_x = [pl.BlockSpec((sh.shape[0] // grid[0], sh.shape[1]), lambda i: (i, 0)) if isinstance(spec, str) else spec
             for spec, sh in zip(out_specs, out_shape)]

    def wrapped(*refs):
        views, k = [], 0
        for g in groups:
            views.append(_ColChunks(refs[k:k + g]) if g > 1 else refs[k])
            k += g
        body(*views, *refs[k:])

    return pl.pallas_call(
        wrapped, grid=grid, in_specs=in_x, out_specs=tuple(out_x), out_shape=tuple(out_shape),
        compiler_params=_params(semantics), name=name)(*args_x)


ROW_RING = 3


def _norm_mod_fwd(name, x, g, sc, sh):
    s, d = x.shape
    tr = _tile(s, ROW_TILE)
    n = s // tr

    def body(x_hbm, g_ref, sc_ref, sh_ref, h_hbm, r_ref, xbuf, hbuf, in_sem, out_sem):
        def read(i):
            return pltpu.make_async_copy(x_hbm.at[pl.ds(i * tr, tr), :], xbuf.at[i % ROW_RING], in_sem.at[i % ROW_RING])

        def write(i):
            return pltpu.make_async_copy(hbuf.at[i % 2], h_hbm.at[pl.ds(i * tr, tr), :], out_sem.at[i % 2])

        for i in range(min(ROW_RING, n)):
            read(i).start()
        for i in range(n):
            read(i).wait()
            xv = xbuf[i % ROW_RING]
            r = lax.rsqrt(jnp.mean(xv * xv, axis=-1, keepdims=True) + EPS)
            hv = ((xv * r * g_ref[...]) * (1.0 + sc_ref[...]) + sh_ref[...]).astype(BF16)
            if i >= 2:
                write(i - 2).wait()
            hbuf[i % 2] = hv
            r_ref[i * tr:(i + 1) * tr, :] = r
            write(i).start()
            if i + ROW_RING < n:
                read(i + ROW_RING).start()
        for i in range(max(n - 2, 0), n):
            write(i).wait()

    vmem = pl.BlockSpec(memory_space=pltpu.VMEM)
    return pl.pallas_call(
        body, in_specs=[_ANY, vmem, vmem, vmem], out_specs=(_ANY, vmem),
        out_shape=(SDS((s, d), BF16), SDS((s, 1), F32)),
        scratch_shapes=[pltpu.VMEM((ROW_RING, tr, d), F32), pltpu.VMEM((2, tr, d), BF16),
                        pltpu.SemaphoreType.DMA((ROW_RING,)), pltpu.SemaphoreType.DMA((2,))],
        compiler_params=pltpu.CompilerParams(vmem_limit_bytes=VMEM_LIMIT), name=name)(x, g, sc, sh)


def _norm_mod_bwd(name, dh, xin, rstd, g, sc, dres, gate=None):
    s, d = xin.shape
    tr = _tile(s, ROW_TILE)
    row, vec, col = _row_specs(s, d, tr)

    n_gate = 2 if gate is not None else 0

    def body(dh_ref, x_ref, r_ref, g_ref, sc_ref, dres_ref, *rest):
        gate_in, gate_out = rest[:n_gate], rest[n_gate + 4:]
        dx_ref, dsh_ref, dsc_ref, dg_ref = rest[n_gate:n_gate + 4]

        @pl.when(pl.program_id(0) == 0)
        def _():
            for ref in (dsh_ref, dsc_ref, dg_ref) + tuple(gate_out[1:]):
                ref[...] = jnp.zeros_like(ref)

        dh_v, xv, r, gv = dh_ref[...].astype(F32), x_ref[...], r_ref[...], g_ref[...]
        n0 = xv * r
        dsh_ref[...] += jnp.sum(dh_v, axis=0, keepdims=True)
        dsc_ref[...] += jnp.sum(dh_v * (n0 * gv), axis=0, keepdims=True)
        dn = dh_v * (1.0 + sc_ref[...])
        dg_ref[...] += jnp.sum(dn * n0, axis=0, keepdims=True)
        gy = dn * gv
        dot = jnp.mean(gy * xv, axis=-1, keepdims=True)
        dxv = dres_ref[...] + r * gy - xv * (r * r * r * dot)
        dx_ref[...] = dxv
        if gate is not None:
            y_ref, gt_ref = gate_in
            dy_ref, dgt_ref = gate_out
            dy_ref[...] = (gt_ref[...] * dxv).astype(BF16)
            dgt_ref[...] += jnp.sum(dxv * y_ref[...], axis=0, keepdims=True)

    vecs = SDS((1, d), F32)
    gate_args = tuple(gate) if gate is not None else ()
    return _rows_call(
        name, body, (s // tr,),
        [row, row, col, vec, vec, row] + ([row, vec] if gate is not None else []),
        (row, vec, vec, vec) + ((row, vec) if gate is not None else ()),
        (SDS((s, d), F32), vecs, vecs, vecs) + ((SDS((s, d), BF16), vecs) if gate is not None else ()),
        ("arbitrary",), (dh, xin, rstd, g, sc, dres, *gate_args))


def _final_loss(x2, gf, tgt, y, gt):
    s, d = x2.shape
    tr = _tile(s, ROW_TILE)
    row, vec, _ = _row_specs(s, d, tr)
    lrow = pl.BlockSpec((1, LANES), lambda i: (0, 0))

    def body(x_ref, g_ref, t_ref, y_ref, gt_ref, dx_ref, loss_ref, dg_ref, dy_ref, dgt_ref):
        @pl.when(pl.program_id(0) == 0)
        def _():
            loss_ref[...] = jnp.zeros_like(loss_ref)
            dg_ref[...] = jnp.zeros_like(dg_ref)
            dgt_ref[...] = jnp.zeros_like(dgt_ref)

        xv, gv = x_ref[...], g_ref[...]
        r = lax.rsqrt(jnp.mean(xv * xv, axis=-1, keepdims=True) + EPS)
        n0 = xv * r
        err = n0 * gv - t_ref[...]
        loss_ref[...] += jnp.sum(err * err) * (0.5 / d)
        dy = err * (1.0 / d)
        dg_ref[...] += jnp.sum(dy * n0, axis=0, keepdims=True)
        gy = dy * gv
        dot = jnp.mean(gy * xv, axis=-1, keepdims=True)
        dxv = r * gy - xv * (r * r * r * dot)
        dx_ref[...] = dxv
        dy_ref[...] = (gt_ref[...] * dxv).astype(BF16)
        dgt_ref[...] += jnp.sum(dxv * y_ref[...], axis=0, keepdims=True)

    return _rows_call(
        "final_loss", body, (s // tr,), [row, vec, row, row, vec], (row, lrow, vec, row, vec),
        (SDS((s, d), F32), SDS((1, LANES), F32), SDS((1, d), F32), SDS((s, d), BF16), SDS((1, d), F32)),
        ("arbitrary",), (x2, gf, tgt, y, gt))


def _mix_norm_fwd(oa, ol, ga, gl):
    s, w = oa.shape
    tr = _tile(s, ROW_TILE)
    row, vec, col = _row_specs(s, w, tr)

    def body(oa_ref, ol_ref, ga_ref, gl_ref, mx_ref, ra_ref, rl_ref):
        a, l = oa_ref[...], ol_ref[...]
        ra = lax.rsqrt(jnp.mean(a * a, axis=-1, keepdims=True) + EPS)
        rl = lax.rsqrt(jnp.mean(l * l, axis=-1, keepdims=True) + EPS)
        mx_ref[:, :w] = (a * ra * ga_ref[...]).astype(BF16)
        mx_ref[:, w:] = (l * rl * gl_ref[...]).astype(BF16)
        ra_ref[...] = ra
        rl_ref[...] = rl

    return _rows_call(
        "mix_norm_fwd", body, (s // tr,), [row, row, vec, vec], (row, col, col),
        (SDS((s, 2 * w), BF16), SDS((s, 1), F32), SDS((s, 1), F32)), ("parallel",), (oa, ol, ga, gl))


def _mix_norm_bwd(dmx, oa, ol, ra, rl, ga, gl):
    s, w = oa.shape
    tr = _tile(s, ROW_TILE)
    row, vec, col = _row_specs(s, w, tr)

    def body(dm_ref, oa_ref, ol_ref, ra_ref, rl_ref, ga_ref, gl_ref, doa_ref, dol_ref, dga_ref, dgl_ref):
        @pl.when(pl.program_id(0) == 0)
        def _():
            dga_ref[...] = jnp.zeros_like(dga_ref)
            dgl_ref[...] = jnp.zeros_like(dgl_ref)

        def one(dy, xv, r, gv, dg_ref):
            dg_ref[...] += jnp.sum(dy * (xv * r), axis=0, keepdims=True)
            gy = dy * gv
            dot = jnp.mean(gy * xv, axis=-1, keepdims=True)
            return r * gy - xv * (r * r * r * dot)

        dm = dm_ref[...].astype(F32)
        doa_ref[...] = one(dm[:, :w], oa_ref[...], ra_ref[...], ga_ref[...], dga_ref).astype(BF16)
        dol_ref[...] = one(dm[:, w:], ol_ref[...], rl_ref[...], gl_ref[...], dgl_ref)

    return _rows_call(
        "mix_norm_bwd", body, (s // tr,), [row, row, row, col, col, vec, vec], (row, row, vec, vec),
        (SDS((s, w), BF16), SDS((s, w), F32), SDS((1, w), F32), SDS((1, w), F32)), ("arbitrary",),
        (dmx, oa, ol, ra, rl, ga, gl))


def _attn_blocks(qs, ks, tri_after, csums, causal):
    zs = [_dot_nt(q, k) * (HEAD ** -0.5) for q, k in zip(qs, ks)]
    lbs = [_log_sigmoid(z) for z in zs]
    lss = [lb - z for lb, z in zip(lbs, zs)]
    if causal is not None:
        lss = [jnp.where(causal, ls, 0.0) for ls in lss]
    locs = [_split_dot(ls, tri_after) for ls in lss]
    ws = [jnp.exp(lb + (loc + cs)) for lb, loc, cs in zip(lbs, locs, csums)]
    if causal is not None:
        ws = [jnp.where(causal, w, 0.0) for w in ws]
    nxt = [cs + (loc[:, 0:1] + ls[:, 0:1]) for cs, loc, ls in zip(csums, locs, lss)]
    return lbs, ws, nxt


ATTN_HEADS_PER_STEP = 4


def _attn_tile(s):
    return 256 if s >= 1024 else 128


def _tri(t, after):
    r_i = lax.broadcasted_iota(jnp.int32, (t, t), 0)
    c_i = lax.broadcasted_iota(jnp.int32, (t, t), 1)
    return ((r_i > c_i) if after else (r_i < c_i)).astype(BF16)


def _attn_fwd(qkv, n_heads):
    s = qkv.shape[0]
    t = _attn_tile(s)
    hps = ATTN_HEADS_PER_STEP
    wid = hps * HEAD

    nq = s // t

    def body(q_ref, k_ref, v_ref, o_ref, w_ref, sg_ref):
        qi = pl.program_id(1)
        tri_after = _tri(t, True)
        causal = lax.broadcasted_iota(jnp.int32, (t, t), 1) < lax.broadcasted_iota(jnp.int32, (t, t), 0)
        lanes = [slice(a * HEAD, (a + 1) * HEAD) for a in range(hps)]
        qs = [q_ref[:, ln] for ln in lanes]

        def block(kb, carry, mask):
            off = pl.multiple_of(kb * t, t)
            ks = [k_ref[pl.ds(off, t), ln] for ln in lanes]
            lbs, ws, csums = _attn_blocks(qs, ks, tri_after, [cr[0] for cr in carry], mask)
            wbs = [w.astype(BF16) for w in ws]
            for a in range(hps):
                w_ref[a, kb] = wbs[a]
                sg_ref[a, kb] = jnp.exp(lbs[a]).astype(BF16)
            os_ = [cr[1] + _dot(wb, v_ref[pl.ds(off, t), ln]) for cr, wb, ln in zip(carry, wbs, lanes)]
            return tuple(zip(csums, os_))

        zero = tuple((jnp.zeros((t, 1), F32), jnp.zeros((t, HEAD), F32)) for _ in lanes)
        carry = block(qi, zero, causal)
        carry = lax.fori_loop(1, qi + 1, lambda it, cr: block(qi - it, cr, None), carry)
        for a, ln in enumerate(lanes):
            o_ref[:, ln] = carry[a][1]

    hb = n_heads // hps
    kept = pl.BlockSpec((None, hps, nq, t, t), lambda hh, i: (hh * nq + i, 0, 0, 0, 0))
    kept_shape = SDS((hb * nq, hps, nq, t, t), BF16)
    return pl.pallas_call(
        body, grid=(hb, nq),
        in_specs=[pl.BlockSpec((t, wid), lambda hh, i: (i, hh)),
                  pl.BlockSpec((s, wid), lambda hh, i: (0, hb + hh)),
                  pl.BlockSpec((s, wid), lambda hh, i: (0, 2 * hb + hh))],
        out_specs=(pl.BlockSpec((t, wid), lambda hh, i: (i, hh)), kept, kept),
        out_shape=(SDS((s, n_heads * HEAD), F32), kept_shape, kept_shape),
        compiler_params=_params(("parallel", "parallel")), name="attn_fwd")(qkv, qkv, qkv)


def _attn_bwd(qkv, do, w_kept, sg_kept, n_heads):
    s = qkv.shape[0]
    t = _attn_tile(s)
    nq = s // t
    scale = HEAD ** -0.5
    hps = ATTN_HEADS_PER_STEP
    wid = hps * HEAD

    def body(q_ref, k_ref, v_ref, do_ref, w_ref, sg_ref, dq_ref, dk_ref, dv_ref, dk_acc, dv_acc):
        qi = pl.program_id(1)

        @pl.when(qi == 0)
        def _():
            dk_acc[...] = jnp.zeros_like(dk_acc)
            dv_acc[...] = jnp.zeros_like(dv_acc)

        tri_before = _tri(t, False)
        causal = lax.broadcasted_iota(jnp.int32, (t, t), 1) < lax.broadcasted_iota(jnp.int32, (t, t), 0)
        lanes = [slice(a * HEAD, (a + 1) * HEAD) for a in range(hps)]
        qs = [q_ref[:, ln] for ln in lanes]
        douts = [do_ref[:, ln] for ln in lanes]

        def block(kb, carry, mask):
            off = pl.multiple_of(kb * t, t)
            wbs = [w_ref[a, kb] for a in range(hps)]
            dws = [_dot_nt(dout, v_ref[pl.ds(off, t), ln]) for dout, ln in zip(douts, lanes)]
            for a, ln in enumerate(lanes):
                dv_acc[pl.ds(off, t), ln] += _dot_tn(wbs[a], douts[a])
            es = [dw * wb.astype(F32) for dw, wb in zip(dws, wbs)]
            locs = [_split_dot(e, tri_before) for e in es]
            sgs = [sg_ref[a, kb].astype(F32) for a in range(hps)]
            stays = [(loc + cr[0]) * sg for loc, cr, sg in zip(locs, carry, sgs)]
            if mask is not None:
                stays = [jnp.where(mask, st, 0.0) for st in stays]
            dzbs = [((e * (1.0 - sg) - st) * scale).astype(BF16) for e, sg, st in zip(es, sgs, stays)]
            dqs = [cr[1] + _dot(dzb, k_ref[pl.ds(off, t), ln]) for cr, dzb, ln in zip(carry, dzbs, lanes)]
            for a, ln in enumerate(lanes):
                dk_acc[pl.ds(off, t), ln] += _dot_tn(dzbs[a], qs[a])
            esums = [cr[0] + (loc[:, t - 1:t] + e[:, t - 1:t]) for cr, loc, e in zip(carry, locs, es)]
            return tuple(zip(esums, dqs))

        zero = tuple((jnp.zeros((t, 1), F32), jnp.zeros((t, HEAD), F32)) for _ in lanes)
        carry = lax.fori_loop(0, qi, lambda kb, cr: block(kb, cr, None), zero)
        carry = block(qi, carry, causal)
        for a, ln in enumerate(lanes):
            dq_ref[:, ln] = carry[a][1].astype(BF16)

        @pl.when(qi == nq - 1)
        def _():
            dk_ref[...] = dk_acc[...].astype(BF16)
            dv_ref[...] = dv_acc[...].astype(BF16)

    hb = n_heads // hps
    blk = pl.BlockSpec((t, wid), lambda hh, i: (i, hh))
    full = pl.BlockSpec((s, wid), lambda hh, i: (0, hh))
    kept = pl.BlockSpec((None, hps, nq, t, t), lambda hh, i: (hh * nq + i, 0, 0, 0, 0))
    return pl.pallas_call(
        body, grid=(hb, nq),
        in_specs=[blk,
                  pl.BlockSpec((s, wid), lambda hh, i: (0, hb + hh)),
                  pl.BlockSpec((s, wid), lambda hh, i: (0, 2 * hb + hh)),
                  blk, kept, kept],
        out_specs=(blk, full, full),
        out_shape=(SDS((s, n_heads * HEAD), BF16),) * 3,
        scratch_shapes=[pltpu.VMEM((s, wid), F32), pltpu.VMEM((s, wid), F32)],
        compiler_params=_params(("parallel", "arbitrary")), name="attn_bwd")(qkv, qkv, qkv, do, w_kept, sg_kept)


def _lru_chunk(s):
    return 128 if s >= 256 else s // 2


def _lru_gates(xc, wa, ba, wx, bx, sp):
    xb = xc.astype(BF16)
    r = _sigmoid(_dot(xb, wa) + ba)
    ig = _sigmoid(_dot(xb, wx) + bx)
    la = -LRU_C * r * sp
    a = jnp.exp(la)
    t = jnp.tanh(la)
    mult = jnp.sqrt(-2.0 * t / (1.0 - t))
    return r, ig, a, mult


def _softplus_neg(lam):
    return jnp.maximum(-lam, 0.0) + jnp.log(1.0 + jnp.exp(-jnp.abs(lam)))


LRU_BLOCKS_PER_STEP = 1
SCAN_GROUP = 4


def _lru_specs(s, n_blocks):
    bps = min(LRU_BLOCKS_PER_STEP, n_blocks)
    wid = bps * HEAD
    seq0 = pl.BlockSpec((s, wid), lambda h: (0, h))
    seq1 = pl.BlockSpec((s, wid), lambda h: (0, n_blocks // bps + h))
    taps = pl.BlockSpec((CONV_TAPS, wid), lambda h: (0, h))
    vec = pl.BlockSpec((1, wid), lambda h: (0, h))
    mat = pl.BlockSpec((bps, HEAD, HEAD), lambda h: (h, 0, 0))
    return bps, seq0, seq1, taps, vec, mat


def _per_block(one_block, n_2d, n_mat_pos, bps):
    def body(*refs):
        for a in range(bps):
            views = [r.at[a] if i in n_mat_pos else r.at[:, pl.ds(a * HEAD, HEAD)] for i, r in enumerate(refs[:n_2d])]
            one_block(*views, *refs[n_2d:])
    return body


def _lru_fwd(xrg, wconv, bconv, wa, ba, wx, bx, lam):
    s = xrg.shape[0]
    nb = wa.shape[0]
    tc = _lru_chunk(s)
    bps, seq0, seq1, taps, vec, mat = _lru_specs(s, nb)
    pad = SUBLANES

    def one_block(xr_ref, xg_ref, wc_ref, bc_ref, wa_ref, ba_ref, wx_ref, bx_ref, lam_ref, o_ref, h_ref, pad_s, a_s, u_s):
        pad_s[0:pad, :] = jnp.zeros((pad, HEAD), F32)
        pad_s[pad:pad + s, :] = xr_ref[...]
        wab, wxb = wa_ref[...].astype(BF16), wx_ref[...].astype(BF16)
        sp = _softplus_neg(lam_ref[...])
        for c in range(s // tc):
            base = c * tc
            xc = bc_ref[...] + sum(wc_ref[i:i + 1, :] * pad_s[pl.ds(base + pad - (CONV_TAPS - 1) + i, tc), :]
                                   for i in range(CONV_TAPS))
            _, ig, a, mult = _lru_gates(xc, wab, ba_ref[...], wxb, bx_ref[...], sp)
            a_s[base:base + tc, :] = a
            u_s[base:base + tc, :] = mult * (ig * xc)

        row = lax.broadcasted_iota(jnp.int32, (SUBLANES, HEAD), 0)
        last = SUBLANES - 1

        def group(gi, hprev):
            offs = [pl.multiple_of((gi * SCAN_GROUP + q) * SUBLANES, SUBLANES) for q in range(SCAN_GROUP)]
            ab = []
            for off in offs:
                a8, b8 = a_s[pl.ds(off, SUBLANES), :], u_s[pl.ds(off, SUBLANES), :]
                for d in (1, 2, 4):
                    a_sh = jnp.where(row < d, 1.0, pltpu.roll(a8, d, 0))
                    b_sh = jnp.where(row < d, 0.0, pltpu.roll(b8, d, 0))
                    b8 = a8 * b_sh + b8
                    a8 = a8 * a_sh
                ab.append((a8, b8))
            enters = []
            for a8, b8 in ab:
                enters.append(hprev)
                hprev = a8[last:, :] * hprev + b8[last:, :]
            for off, (a8, b8), h0 in zip(offs, ab, enters):
                h_ref[pl.ds(off, SUBLANES), :] = a8 * h0 + b8
            return hprev

        lax.fori_loop(0, s // (SUBLANES * SCAN_GROUP), group, jnp.zeros((1, HEAD), F32))
        for c in range(s // tc):
            sl = slice(c * tc, (c + 1) * tc)
            gel, _ = _gelu_parts(xg_ref[sl, :])
            o_ref[sl, :] = h_ref[sl, :] * gel

    return pl.pallas_call(
        _per_block(one_block, 11, (4, 6), bps), grid=(nb // bps,),
        in_specs=[seq0, seq1, taps, vec, mat, vec, mat, vec, vec],
        out_specs=(seq0, seq0),
        out_shape=(SDS((s, nb * HEAD), F32), SDS((s, nb * HEAD), F32)),
        scratch_shapes=[pltpu.VMEM((s + pad, HEAD), F32), pltpu.VMEM((s, HEAD), F32), pltpu.VMEM((s, HEAD), F32)],
        compiler_params=_params(("parallel",)), name="lru_fwd")(xrg, xrg, wconv, bconv, wa, ba, wx, bx, lam)


def _lru_bwd(xrg, dol, hseq, wconv, bconv, wa, ba, wx, bx, lam):
    s = xrg.shape[0]
    nb = wa.shape[0]
    tc = _lru_chunk(s)
    bps, seq0, seq1, taps, vec, mat = _lru_specs(s, nb)
    pad = SUBLANES

    def one_block(xr_ref, xg_ref, do_ref, h_ref, wc_ref, bc_ref, wa_ref, ba_ref, wx_ref, bx_ref, lam_ref,
             dxr_ref, dxg_ref, dwc_ref, dbc_ref, dwa_ref, dba_ref, dwx_ref, dbx_ref, dlam_ref,
             pad_s, hp_s, a_s, g_s, da_s, dxc_s, xc_s, r_s, ig_s, mult_s):
        pad_s[0:pad, :] = jnp.zeros((pad, HEAD), F32)
        pad_s[pad:pad + s, :] = xr_ref[...]
        hp_s[0:pad, :] = jnp.zeros((pad, HEAD), F32)
        hp_s[pad:pad + s, :] = h_ref[...]
        a_s[s:s + pad, :] = jnp.zeros((pad, HEAD), F32)
        dxc_s[s:s + pad, :] = jnp.zeros((pad, HEAD), F32)
        wab, wxb = wa_ref[...].astype(BF16), wx_ref[...].astype(BF16)
        lam_v = lam_ref[...]
        sp = _softplus_neg(lam_v)

        def conv_in(c):
            base = c * tc
            wins = [pad_s[pl.ds(base + pad - (CONV_TAPS - 1) + i, tc), :] for i in range(CONV_TAPS)]
            xc = bc_ref[...] + sum(wc_ref[i:i + 1, :] * wins[i] for i in range(CONV_TAPS))
            return xc, wins

        for c in range(s // tc):
            sl = slice(c * tc, (c + 1) * tc)
            xc, _ = conv_in(c)
            r, ig, a, mult = _lru_gates(xc, wab, ba_ref[...], wxb, bx_ref[...], sp)
            a_s[sl, :] = a
            xc_s[sl, :], r_s[sl, :], ig_s[sl, :], mult_s[sl, :] = xc, r, ig, mult
            gel, dgel = _gelu_parts(xg_ref[sl, :])
            dov = do_ref[sl, :]
            g_s[sl, :] = dov * gel
            dxg_ref[sl, :] = (dov * h_ref[sl, :] * dgel).astype(BF16)

        row = lax.broadcasted_iota(jnp.int32, (SUBLANES, HEAD), 0)
        n_chunks = s // SUBLANES

        def group(it, gnext):
            offs = [pl.multiple_of((n_chunks - 1 - (it * SCAN_GROUP + q)) * SUBLANES, SUBLANES) for q in range(SCAN_GROUP)]
            cg = []
            for off in offs:
                a8 = a_s[pl.ds(off, SUBLANES), :]
                a8n = a_s[pl.ds(off + SUBLANES, SUBLANES), :]
                c8 = pltpu.roll(jnp.where(row == 0, a8n, a8), SUBLANES - 1, 0)
                g8 = g_s[pl.ds(off, SUBLANES), :]
                for d in (1, 2, 4):
                    c_sh = jnp.where(row >= SUBLANES - d, 1.0, pltpu.roll(c8, SUBLANES - d, 0))
                    g_sh = jnp.where(row >= SUBLANES - d, 0.0, pltpu.roll(g8, SUBLANES - d, 0))
                    g8 = c8 * g_sh + g8
                    c8 = c8 * c_sh
                cg.append((c8, g8))
            enters = []
            for c8, g8 in cg:
                enters.append(gnext)
                gnext = g8[0:1, :] + c8[0:1, :] * gnext
            for off, (c8, g8), g0 in zip(offs, cg, enters):
                gv = g8 + c8 * g0
                g_s[pl.ds(off, SUBLANES), :] = gv
                h8 = hp_s[pl.ds(off + pad, SUBLANES), :]
                h8p = hp_s[pl.ds(off, SUBLANES), :]
                da_s[pl.ds(off, SUBLANES), :] = gv * pltpu.roll(jnp.where(row == SUBLANES - 1, h8p, h8), 1, 0)
            return gnext

        lax.fori_loop(0, n_chunks // SCAN_GROUP, group, jnp.zeros((1, HEAD), F32))

        dsp = jnp.zeros((1, HEAD), F32)
        dbc = jnp.zeros((1, HEAD), F32)
        dba = jnp.zeros((1, HEAD), F32)
        dbx = jnp.zeros((1, HEAD), F32)
        dwa = jnp.zeros((HEAD, HEAD), F32)
        dwx = jnp.zeros((HEAD, HEAD), F32)
        dwc = [jnp.zeros((1, HEAD), F32) for _ in range(CONV_TAPS)]
        for c in range(s // tc):
            sl = slice(c * tc, (c + 1) * tc)
            wins = [pad_s[pl.ds(c * tc + pad - (CONV_TAPS - 1) + i, tc), :] for i in range(CONV_TAPS)]
            xc, r, ig, a, mult = xc_s[sl, :], r_s[sl, :], ig_s[sl, :], a_s[sl, :], mult_s[sl, :]
            du, da = g_s[sl, :], da_s[sl, :]
            d_ix = du * mult
            dla = da * a - (du * ig * xc) * (a * a / mult)
            dsp = dsp + jnp.sum(dla * r, axis=0, keepdims=True) * (-LRU_C)
            dpa = (dla * (-LRU_C * sp)) * r * (1.0 - r)
            dpx = (d_ix * xc) * ig * (1.0 - ig)
            dpab, dpxb, xb = dpa.astype(BF16), dpx.astype(BF16), xc.astype(BF16)
            dxc = d_ix * ig + _dot_nt(dpab, wab) + _dot_nt(dpxb, wxb)
            dwa = dwa + _dot_tn(xb, dpab)
            dwx = dwx + _dot_tn(xb, dpxb)
            dba = dba + jnp.sum(dpa, axis=0, keepdims=True)
            dbx = dbx + jnp.sum(dpx, axis=0, keepdims=True)
            dbc = dbc + jnp.sum(dxc, axis=0, keepdims=True)
            for i in range(CONV_TAPS):
                dwc[i] = dwc[i] + jnp.sum(dxc * wins[i], axis=0, keepdims=True)
            dxc_s[sl, :] = dxc

        for c in range(s // tc):
            base = c * tc
            dxr = sum(wc_ref[i:i + 1, :] * dxc_s[pl.ds(base + (CONV_TAPS - 1) - i, tc), :] for i in range(CONV_TAPS))
            dxr_ref[base:base + tc, :] = dxr.astype(BF16)

        for i in range(CONV_TAPS):
            dwc_ref[i:i + 1, :] = dwc[i]
        dbc_ref[...] = dbc
        dwa_ref[...] = dwa
        dwx_ref[...] = dwx
        dba_ref[...] = dba
        dbx_ref[...] = dbx
        dlam_ref[...] = dsp * (-_sigmoid(-lam_v))

    w = nb * HEAD
    return pl.pallas_call(
        _per_block(one_block, 20, (6, 8, 15, 17), bps), grid=(nb // bps,),
        in_specs=[seq0, seq1, seq0, seq0, taps, vec, mat, vec, mat, vec, vec],
        out_specs=(seq0, seq0, taps, vec, mat, vec, mat, vec, vec),
        out_shape=(SDS((s, w), BF16), SDS((s, w), BF16), SDS((CONV_TAPS, w), F32), SDS((1, w), F32),
                   SDS((nb, HEAD, HEAD), F32), SDS((1, w), F32), SDS((nb, HEAD, HEAD), F32), SDS((1, w), F32),
                   SDS((1, w), F32)),
        scratch_shapes=[pltpu.VMEM((s + pad, HEAD), F32), pltpu.VMEM((s + pad, HEAD), F32),
                        pltpu.VMEM((s + pad, HEAD), F32), pltpu.VMEM((s, HEAD), F32),
                        pltpu.VMEM((s, HEAD), F32), pltpu.VMEM((s + pad, HEAD), F32)]
                       + [pltpu.VMEM((s, HEAD), F32)] * 4,
        compiler_params=_params(("parallel",)), name="lru_bwd",
    )(xrg, xrg, dol, hseq, wconv, bconv, wa, ba, wx, bx, lam)


def _ada_mod(c_all, w_sh, b_sh):
    n_ex, d = c_all.shape
    n = w_sh.shape[1]
    tn = _tile(n, 512)

    def body(c_ref, w_ref, b_ref, mod_ref, act_ref):
        cv = c_ref[...]
        act = cv * _sigmoid(cv)
        act_ref[...] = act
        mod_ref[...] = _dot(act.astype(BF16), w_ref[...].astype(BF16)) + b_ref[...]

    return pl.pallas_call(
        body, grid=(n // tn,),
        in_specs=[pl.BlockSpec((n_ex, d), lambda j: (0, 0)), pl.BlockSpec((d, tn), lambda j: (0, j)),
                  pl.BlockSpec((1, tn), lambda j: (0, j))],
        out_specs=(pl.BlockSpec((n_ex, tn), lambda j: (0, j)), pl.BlockSpec((n_ex, d), lambda j: (0, 0))),
        out_shape=(SDS((n_ex, n), F32), SDS((n_ex, d), F32)),
        compiler_params=_params(("arbitrary",)), name="ada_mod")(c_all, w_sh, b_sh)


def _adamw_math(w, g, m, v):
    m = ADAM_B1 * m + (1.0 - ADAM_B1) * g
    v = ADAM_B2 * v + (1.0 - ADAM_B2) * (g * g)
    m_hat = m / (1.0 - ADAM_B1 ** ADAM_STEP)
    v_hat = v / (1.0 - ADAM_B2 ** ADAM_STEP)
    delta = -ADAM_LR * (m_hat / (jnp.sqrt(v_hat) + ADAM_EPS) + ADAM_WD * w)
    return delta, m, v


def _adamw_plain(name, w, g, m, v):
    def body(w_ref, g_ref, m_ref, v_ref, d_ref, mo_ref, vo_ref):
        d_ref[...], mo_ref[...], vo_ref[...] = _adamw_math(w_ref[...], g_ref[...], m_ref[...], v_ref[...])

    return pl.pallas_call(body, out_shape=(SDS(w.shape, F32),) * 3, name=name)(w, g, m, v)


def _adamw_halves(name, c_arr, w, m, v, g_own, g_recv):
    r, n = w.shape
    rh = r // 2
    tr = _tile(rh, 256)
    nh = rh // tr

    def body(c_ref, w_ref, m_ref, v_ref, go_ref, gr_ref, g_ref, d_ref, mo_ref, vo_ref):
        own = (pl.program_id(0) // nh) == c_ref[0]
        g = jnp.where(own, go_ref[...], gr_ref[...])
        g_ref[...] = g
        d_ref[...], mo_ref[...], vo_ref[...] = _adamw_math(w_ref[...], g, m_ref[...], v_ref[...])

    full = pl.BlockSpec((tr, n), lambda i, c: (i, 0))
    own = pl.BlockSpec((tr, n), lambda i, c: (jnp.where(i // nh == c[0], i % nh, 0), 0))
    recv = pl.BlockSpec((tr, n), lambda i, c: (jnp.where(i // nh == c[0], 0, i % nh), 0))
    return pl.pallas_call(
        body,
        grid_spec=pltpu.PrefetchScalarGridSpec(
            num_scalar_prefetch=1, grid=(2 * nh,), in_specs=[full, full, full, own, recv],
            out_specs=(full,) * 4),
        out_shape=(SDS((r, n), F32),) * 4,
        compiler_params=_params(("parallel",)), name=name)(c_arr, w, m, v, g_own, g_recv)


def _adamw_ada(w, m, v, act_t, dmod):
    d, n = w.shape
    n_ex = act_t.shape[1]
    tr = _tile(d, 256)

    def body(a_ref, dm_ref, w_ref, m_ref, v_ref, g_ref, d_ref, mo_ref, vo_ref):
        g = _dot(a_ref[...], dm_ref[...])
        g_ref[...] = g
        d_ref[...], mo_ref[...], vo_ref[...] = _adamw_math(w_ref[...], g, m_ref[...], v_ref[...])

    full = pl.BlockSpec((tr, n), lambda i: (i, 0))
    return pl.pallas_call(
        body, grid=(d // tr,),
        in_specs=[pl.BlockSpec((tr, n_ex), lambda i: (i, 0)), pl.BlockSpec((n_ex, n), lambda i: (0, 0)), full, full, full],
        out_specs=(full,) * 4, out_shape=(SDS((d, n), F32),) * 4,
        compiler_params=_params(("parallel",)), name="adamw_ada")(act_t, dmod, w, m, v)


def _small_reduce_adamw(parts, w, m, v):
    n_dev, r, _ = parts.shape
    tr = r if r <= PACK_ROWS else PACK_ROWS

    def body(p_ref, w_ref, m_ref, v_ref, g_ref, d_ref, mo_ref, vo_ref):
        g = p_ref[0]
        for k in range(1, n_dev):
            g = g + p_ref[k]
        g_ref[...] = g
        d_ref[...], mo_ref[...], vo_ref[...] = _adamw_math(w_ref[...], g, m_ref[...], v_ref[...])

    full = pl.BlockSpec((tr, LANES), lambda i: (i, 0))
    return pl.pallas_call(
        body, grid=(r // tr,),
        in_specs=[pl.BlockSpec((n_dev, tr, LANES), lambda i: (0, i, 0)), full, full, full],
        out_specs=(full,) * 4, out_shape=(SDS((r, LANES), F32),) * 4,
        compiler_params=_params(("parallel",)), name="small_reduce_adamw")(parts, w, m, v)


def _mesh_pos():
    return lax.axis_index("x"), lax.axis_index("y"), lax.axis_index("c")


def _other_chips(x, y):
    return [(1 - x, y), (x, 1 - y), (1 - x, 1 - y)]


def _all_gather_small(name, blk, after=()):
    r, n = blk.shape
    n_after = len(after)

    def body(x_ref, *rest):
        out_ref, send_sems, recv_sems, local_sem = rest[n_after:]
        x, y, c = _mesh_pos()
        me, sibling = (x, y, c), (x, y, 1 - c)
        chips = _other_chips(x, y)

        def rows(px, py, pc):
            return out_ref.at[4 * px + 2 * py + pc]

        def copy(k, block, to, src=None):
            return pltpu.make_async_remote_copy(
                src_ref=rows(*block) if src is None else src, dst_ref=rows(*block),
                send_sem=send_sems.at[k], recv_sem=recv_sems.at[k], device_id=to, device_id_type=MESH)

        mine = pltpu.make_async_copy(x_ref, rows(*me), local_sem)
        mine.start()
        first = [copy(0, me, sibling, src=x_ref)]
        first += [copy(1 + j, me, (*chip, c), src=x_ref) for j, chip in enumerate(chips)]
        for cp in first:
            cp.start()
        passed = [copy(4 + j, (*chip, c), sibling) for j, chip in enumerate(chips)]
        for j, chip in enumerate(chips):
            copy(1 + j, (*chip, c), me).wait_recv()
            passed[j].start()
        copy(0, sibling, me).wait_recv()
        for j, chip in enumerate(chips):
            copy(4 + j, (*chip, 1 - c), me).wait_recv()
        for cp in first + passed:
            cp.wait_send()
        mine.wait()

    return pl.pallas_call(
        body, out_shape=SDS((N_DEV, r, n), blk.dtype),
        in_specs=[pl.BlockSpec(memory_space=pltpu.VMEM)] + [pl.BlockSpec(memory_space=pl.ANY)] * n_after,
        out_specs=pl.BlockSpec(memory_space=pltpu.VMEM),
        scratch_shapes=[pltpu.SemaphoreType.DMA((7,)), pltpu.SemaphoreType.DMA((7,)), pltpu.SemaphoreType.DMA],
        compiler_params=pltpu.CompilerParams(vmem_limit_bytes=VMEM_LIMIT), name=name)(blk, *after)


_ANY = pl.BlockSpec(memory_space=pl.ANY)
_HBM = pl.BlockSpec(memory_space=pltpu.HBM)
_SEM = pl.BlockSpec(memory_space=pltpu.SEMAPHORE)
_EFFECT = pltpu.SideEffectType.DATAFLOW_SIDE_EFFECTING


def _hbm(a):
    return pltpu.with_memory_space_constraint(a, pltpu.HBM)


def _place_cast(name, j_arr, shard, kind, after):
    r, n = shard.shape
    tr = _tile(r, 256)
    nr = r // tr
    if kind == "col":
        out_shape, o_spec = (r, N_CHIP * n), pl.BlockSpec((tr, n), lambda i, j: (i, j[0]))
    else:
        out_shape, o_spec = (N_CHIP * r, n), pl.BlockSpec((tr, n), lambda i, j: (j[0] * nr + i, 0))

    def body(j_ref, s_ref, after_ref, o_ref, tok_ref):
        o_ref[...] = s_ref[...].astype(BF16)
        tok_ref[...] = jnp.zeros_like(tok_ref)

    return pl.pallas_call(
        body,
        grid_spec=pltpu.PrefetchScalarGridSpec(
            num_scalar_prefetch=1, grid=(nr,), in_specs=[pl.BlockSpec((tr, n), lambda i, j: (i, 0)), _ANY],
            out_specs=(o_spec, pl.BlockSpec((SUBLANES, LANES), lambda i, j: (0, 0)))),
        out_shape=(SDS(out_shape, BF16), SDS((SUBLANES, LANES), F32)),
        compiler_params=_params(("arbitrary",)), name=name)(j_arr, shard, after)


def _leg_direct(full, kind, x, y, c):
    mine = _full_region(full, kind, x, y, c)
    return [(mine, mine, (1 - x, y, c)), (mine, mine, (x, 1 - y, c))]


def _leg_relay(full, kind, x, y, c):
    fx, fy = jnp.where(c == 0, 1 - x, x), jnp.where(c == 0, y, 1 - y)
    tx, ty = jnp.where(c == 0, x, 1 - x), jnp.where(c == 0, 1 - y, y)
    got = _full_region(full, kind, fx, fy, c)
    return [(got, got, (tx, ty, c))]


def _leg_d2d(which):
    def leg(full, kind, x, y, c):
        chips = _other_chips(x, y)
        return [(_full_region(full, kind, *chips[k], c), _full_region(full, kind, *chips[k], c), (x, y, 1 - c))
                for k in which]
    return leg


_LEGS = {"direct": (_leg_direct, 2), "relay": (_leg_relay, 1), "d2d_near": (_leg_d2d((0, 1)), 2),
         "d2d_far": (_leg_d2d((2,)), 1)}


def _gather_call(name, fulls, kinds, waits, starts, after, thru):
    nw, n_wait, n_start = len(fulls), len(waits), len(starts)

    def body(*refs):
        wait_sems = refs[nw:nw + 2 * n_wait]
        outs = refs[nw + 2 * n_wait + 2:]
        full, start_sems = outs[:nw], outs[nw:nw + 2 * n_start]
        x, y, c = _mesh_pos()
        for i, (leg, ws, _, _) in enumerate(waits):
            fn, per = _LEGS[leg]
            for li, w in enumerate(ws):
                for k, (s_, d_, dev) in enumerate(fn(full[w], kinds[w], x, y, c)):
                    cp = pltpu.make_async_remote_copy(
                        src_ref=s_, dst_ref=d_, send_sem=wait_sems[2 * i].at[per * li + k],
                        recv_sem=wait_sems[2 * i + 1].at[per * li + k], device_id=dev, device_id_type=MESH)
                    cp.wait_recv()
                    cp.wait_send()
        for i, (leg, ws) in enumerate(starts):
            fn, per = _LEGS[leg]
            for li, w in enumerate(ws):
                for k, (s_, d_, dev) in enumerate(fn(full[w], kinds[w], x, y, c)):
                    pltpu.make_async_remote_copy(
                        src_ref=s_, dst_ref=d_, send_sem=start_sems[2 * i].at[per * li + k],
                        recv_sem=start_sems[2 * i + 1].at[per * li + k], device_id=dev, device_id_type=MESH).start()

    sems = []
    for leg, ws in starts:
        sems += [pltpu.SemaphoreType.DMA((_LEGS[leg][1] * len(ws),))] * 2
    wait_args = []
    for _, _, s_, r_ in waits:
        wait_args += [s_, r_]
    outs = pl.pallas_call(
        body,
        out_shape=tuple(pltpu.HBM(f_.shape, f_.dtype) for f_ in fulls) + tuple(sems) + (SDS(thru.shape, thru.dtype),),
        in_specs=[_HBM] * nw + [_SEM] * (2 * n_wait) + [_ANY, _ANY],
        out_specs=tuple([_HBM] * nw + [_SEM] * (2 * n_start) + [_ANY]),
        input_output_aliases={**{w: w for w in range(nw)}, nw + 2 * n_wait + 1: nw + 2 * n_start},
        compiler_params=pltpu.CompilerParams(has_side_effects=_EFFECT),
        name=name,
    )(*[_hbm(f_) for f_ in fulls], *wait_args, after, thru)
    pairs = [(outs[nw + 2 * i], outs[nw + 2 * i + 1]) for i in range(n_start)]
    return list(outs[:nw]), pairs, outs[nw + 2 * n_start]


def _full_region(full, kind, px, py, half):
    j = 2 * px + py
    if kind == "col":
        rh, cols = full.shape[0] // 2, full.shape[1] // N_CHIP
        return full.at[pl.ds(half * rh, rh), pl.ds(j * cols, cols)]
    rows = full.shape[0] // N_CHIP
    rh = rows // 2
    return full.at[pl.ds(j * rows + half * rh, rh), :]


def _plan_scatter(kind):
    def plan(src, land, x, y, c):
        out = []
        for k, (px, py) in enumerate(_other_chips(x, y)):
            j = 2 * px + py
            if kind == "col":
                n = src.shape[1] // N_CHIP
                blk = src.at[:, pl.ds(j * n, n)]
            else:
                blk = src.at[j]
            out.append((blk, land.at[k], (px, py, c)))
        return out
    return plan


def _plan_whole(src, land, x, y, c):
    return [(src, land, (x, y, 1 - c))]


def _split_start(name, src, land_shape, n, plan, thru):
    def body(src_in, land_in, thru_in, send, recv, src_ref, land_ref, thru_out):
        x, y, c = _mesh_pos()
        for k, (s_, d_, dev) in enumerate(plan(src_ref, land_ref, x, y, c)):
            pltpu.make_async_remote_copy(src_ref=s_, dst_ref=d_, send_sem=send.at[k], recv_sem=recv.at[k],
                                         device_id=dev, device_id_type=MESH).start()

    sem = pltpu.SemaphoreType.DMA((n,))
    return pl.pallas_call(
        body,
        out_shape=(sem, sem, pltpu.HBM(src.shape, src.dtype), pltpu.HBM(land_shape, src.dtype), SDS(thru.shape, thru.dtype)),
        in_specs=[_HBM, _HBM, _ANY], out_specs=(_SEM, _SEM, _HBM, _HBM, _ANY),
        input_output_aliases={0: 2, 1: 3, 2: 4},
        compiler_params=pltpu.CompilerParams(has_side_effects=_EFFECT), name=name,
    )(_hbm(src), _hbm(lax.empty(land_shape, src.dtype)), thru)


def _split_start_pair(name, first, second, thru):
    (src_a, shape_a, n_a, plan_a), (src_b, shape_b, n_b, plan_b) = first, second

    def body(sa_in, la_in, sb_in, lb_in, thru_in, send_a, recv_a, send_b, recv_b, sa, la, sb, lb, thru_out):
        x, y, c = _mesh_pos()
        for plan, s_ref, l_ref, send, recv in ((plan_a, sa, la, send_a, recv_a), (plan_b, sb, lb, send_b, recv_b)):
            for k, (s_, d_, dev) in enumerate(plan(s_ref, l_ref, x, y, c)):
                pltpu.make_async_remote_copy(src_ref=s_, dst_ref=d_, send_sem=send.at[k], recv_sem=recv.at[k],
                                             device_id=dev, device_id_type=MESH).start()

    sem_a, sem_b = pltpu.SemaphoreType.DMA((n_a,)), pltpu.SemaphoreType.DMA((n_b,))
    outs = pl.pallas_call(
        body,
        out_shape=(sem_a, sem_a, sem_b, sem_b, pltpu.HBM(src_a.shape, src_a.dtype), pltpu.HBM(shape_a, src_a.dtype),
                   pltpu.HBM(src_b.shape, src_b.dtype), pltpu.HBM(shape_b, src_b.dtype), SDS(thru.shape, thru.dtype)),
        in_specs=[_HBM] * 4 + [_ANY], out_specs=tuple([_SEM] * 4 + [_HBM] * 4 + [_ANY]),
        input_output_aliases={0: 4, 1: 5, 2: 6, 3: 7, 4: 8},
        compiler_params=pltpu.CompilerParams(has_side_effects=_EFFECT), name=name,
    )(_hbm(src_a), _hbm(lax.empty(shape_a, src_a.dtype)), _hbm(src_b), _hbm(lax.empty(shape_b, src_b.dtype)), thru)
    return (outs[0], outs[1], outs[4], outs[5]), (outs[2], outs[3], outs[6], outs[7]), outs[8]


def _split_wait(name, send, recv, src, land, plan, after):
    def body(src_in, land_in, send_r, recv_r, after_r, src_ref, land_ref):
        x, y, c = _mesh_pos()
        for k, (s_, d_, dev) in enumerate(plan(src_ref, land_ref, x, y, c)):
            cp = pltpu.make_async_remote_copy(src_ref=s_, dst_ref=d_, send_sem=send_r.at[k], recv_sem=recv_r.at[k],
                                              device_id=dev, device_id_type=MESH)
            cp.wait_send()
            cp.wait_recv()

    return pl.pallas_call(
        body,
        out_shape=(pltpu.HBM(src.shape, src.dtype), pltpu.HBM(land.shape, land.dtype)),
        in_specs=[_HBM, _HBM, _SEM, _SEM, _ANY], out_specs=(_HBM, _HBM),
        input_output_aliases={0: 0, 1: 1},
        compiler_params=pltpu.CompilerParams(has_side_effects=_EFFECT), name=name,
    )(src, land, send, recv, after)


def _dev_row(buf, px, py, pc):
    return buf.at[4 * px + 2 * py + pc]


def _plan_gather_own(buf, land, x, y, c):
    own = _dev_row(buf, x, y, c)
    return [(own, own, (x, y, 1 - c))] + [(own, own, (px, py, c)) for px, py in _other_chips(x, y)]


def _plan_gather_pass(buf, land, x, y, c):
    return [(_dev_row(buf, px, py, c), _dev_row(buf, px, py, c), (x, y, 1 - c)) for px, py in _other_chips(x, y)]


def _split_start_inplace(name, buf, n, plan, thru):
    def body(buf_in, thru_in, send, recv, buf_ref, thru_out):
        x, y, c = _mesh_pos()
        for k, (s_, d_, dev) in enumerate(plan(buf_ref, buf_ref, x, y, c)):
            pltpu.make_async_remote_copy(src_ref=s_, dst_ref=d_, send_sem=send.at[k], recv_sem=recv.at[k],
                                         device_id=dev, device_id_type=MESH).start()

    sem = pltpu.SemaphoreType.DMA((n,))
    return pl.pallas_call(
        body, out_shape=(sem, sem, pltpu.HBM(buf.shape, buf.dtype), SDS(thru.shape, thru.dtype)),
        in_specs=[_HBM, _ANY], out_specs=(_SEM, _SEM, _HBM, _ANY), input_output_aliases={0: 2, 1: 3},
        compiler_params=pltpu.CompilerParams(has_side_effects=_EFFECT), name=name)(_hbm(buf), thru)


def _split_wait_inplace(name, send, recv, buf, plan, after):
    def body(buf_in, send_r, recv_r, after_r, buf_ref):
        x, y, c = _mesh_pos()
        for k, (s_, d_, dev) in enumerate(plan(buf_ref, buf_ref, x, y, c)):
            cp = pltpu.make_async_remote_copy(src_ref=s_, dst_ref=d_, send_sem=send_r.at[k], recv_sem=recv_r.at[k],
                                              device_id=dev, device_id_type=MESH)
            cp.wait_send()
            cp.wait_recv()

    return pl.pallas_call(
        body, out_shape=pltpu.HBM(buf.shape, buf.dtype), in_specs=[_HBM, _SEM, _SEM, _ANY], out_specs=_HBM,
        input_output_aliases={0: 0}, compiler_params=pltpu.CompilerParams(has_side_effects=_EFFECT), name=name,
    )(buf, send, recv, after)


def _place_row(name, me_arr, slab):
    r, n = slab.shape
    tr = r if r <= PACK_ROWS else PACK_ROWS

    def body(me_ref, s_ref, o_ref):
        o_ref[...] = s_ref[...]

    return pl.pallas_call(
        body,
        grid_spec=pltpu.PrefetchScalarGridSpec(
            num_scalar_prefetch=1, grid=(r // tr,), in_specs=[pl.BlockSpec((tr, n), lambda i, me: (i, 0))],
            out_specs=pl.BlockSpec((None, tr, n), lambda i, me: (me[0], i, 0))),
        out_shape=SDS((N_DEV, r, n), slab.dtype), compiler_params=_params(("parallel",)), name=name)(me_arr, slab)


def _sum_partials(name, j_arr, part, got, kind):
    _, rh, n = got.shape
    tr = _tile(rh, 256)
    if kind == "col":
        p_spec = pl.BlockSpec((tr, n), lambda i, j: (i, j[0]))
    else:
        p_spec = pl.BlockSpec((None, tr, n), lambda i, j: (j[0], i, 0))

    def body(j_ref, p_ref, r_ref, o_ref):
        o_ref[...] = ((p_ref[...].astype(F32) + r_ref[0].astype(F32)) + r_ref[1].astype(F32)) + r_ref[2].astype(F32)

    return pl.pallas_call(
        body,
        grid_spec=pltpu.PrefetchScalarGridSpec(
            num_scalar_prefetch=1, grid=(rh // tr,),
            in_specs=[p_spec, pl.BlockSpec((3, tr, n), lambda i, j: (0, i, 0))],
            out_specs=pl.BlockSpec((tr, n), lambda i, j: (i, 0))),
        out_shape=SDS((rh, n), F32),
        compiler_params=_params(("parallel",)), name=name)(j_arr, part, got)


def _pack(arrays):
    flat = [a.reshape(-1).astype(F32) for a in arrays]
    flat = [jnp.pad(f, (0, (-f.shape[0]) % LANES)) for f in flat]
    sizes = [f.shape[0] for f in flat]
    total = sum(sizes)
    rows = total // LANES
    tail = LANES * ((-rows) % (PACK_ROWS if rows > PACK_ROWS else SUBLANES))
    if tail:
        flat.append(jnp.zeros((tail,), F32))
    return jnp.concatenate(flat).reshape(-1, LANES), sizes


def _unpack(slab, sizes, shapes, lead=()):
    flat = slab.reshape(lead + (-1,))
    out, off = [], 0
    for sz, shp in zip(sizes, shapes):
        n = math.prod(shp)
        out.append(flat[..., off:off + n].reshape(lead + tuple(shp)))
        off += sz
    return out


def kernel(x, c, w_ada, b_ada, g_norm_mix, w_in, w_conv, b_conv, w_rg_a, b_rg_a, w_rg_x, b_rg_x, lru_lambda, g_attn_out, g_lru_out, w_out, g_norm_mlp, w_mlp_in, w_mlp_out, g_norm_final, loss_target, m_w_ada, m_b_ada, m_g_norm_mix, m_w_in, m_w_conv, m_b_conv, m_w_rg_a, m_b_rg_a, m_w_rg_x, m_b_rg_x, m_lru_lambda, m_g_attn_out, m_g_lru_out, m_w_out, m_g_norm_mlp, m_w_mlp_in, m_w_mlp_out, m_g_norm_final, v_w_ada, v_b_ada, v_g_norm_mix, v_w_in, v_w_conv, v_b_conv, v_w_rg_a, v_b_rg_a, v_w_rg_x, v_b_rg_x, v_lru_lambda, v_g_attn_out, v_g_lru_out, v_w_out, v_g_norm_mlp, v_w_mlp_in, v_w_mlp_out, v_g_norm_final):
    s, d = x.shape[1], x.shape[2]
    aw = d // 2
    nh = aw // HEAD
    f = w_mlp_out.shape[1] * N_CHIP
    n_ada = w_ada.shape[2]
    n_cv = w_conv.shape[2]
    ix, iy, ic = lax.axis_index("x"), lax.axis_index("y"), lax.axis_index("c")
    chip = 2 * ix + iy
    me = 2 * chip + ic
    c_arr = jnp.reshape(ic, (1,)).astype(jnp.int32)
    j_arr = jnp.reshape(chip, (1,)).astype(jnp.int32)

    x2d, tgt = x[0], loss_target[0]

    k_in, k_out, k_mi, k_mo = kinds = ("col", "row", "col", "row")
    me_arr = jnp.reshape(me, (1,)).astype(jnp.int32)
    slab, sizes = _pack([c, w_conv])
    c_buf = _place_row("place_cond", me_arr, slab)
    cg_send, cg_recv, c_buf, tok = _split_start_inplace("cond_gather_start", c_buf, N_CHIP, _plan_gather_own, slab)

    p_in, tok = _place_cast("place_cast_0", j_arr, w_in[0], k_in, tok)
    (f_in,), (dir_in,), tok = _gather_call("gather_0", [p_in], [k_in], [], [("direct", [0])], c, tok)
    c_buf = _split_wait_inplace("cond_gather_wait", cg_send, cg_recv, c_buf, _plan_gather_own, tok)
    cp_send, cp_recv, c_buf, tok = _split_start_inplace("cond_pass_start", c_buf, N_CHIP - 1, _plan_gather_pass, tok)
    gathered = _split_wait_inplace("cond_pass_wait", cp_send, cp_recv, c_buf, _plan_gather_pass, tok)
    c_parts, cv_parts = _unpack(gathered, sizes, [(d,), (CONV_TAPS, n_cv)], lead=(N_DEV,))
    c_all = c_parts
    w_conv_full = jnp.concatenate([cv_parts[2 * j] for j in range(N_CHIP)], axis=-1)
    b_sh = lax.dynamic_slice(b_ada, (0, chip * n_ada), (1, n_ada))
    mod_part, act_all = _ada_mod(c_all, w_ada[0], b_sh)
    p_out, tok = _place_cast("place_cast_1", j_arr, w_out[0], k_out, mod_part)
    p_mi, tok = _place_cast("place_cast_2", j_arr, w_mlp_in[0], k_mi, tok)
    mod_g = _all_gather_small("comm_gather_mod", mod_part.reshape(-1, LANES), after=(tok,))
    mod_g = mod_g.reshape(N_DEV, N_DEV, n_ada)
    mod = jnp.concatenate([lax.dynamic_index_in_dim(mod_g[2 * j], me, 0, keepdims=True) for j in range(N_CHIP)], axis=-1)
    sh1, sc1, gt1, sh2, sc2, gt2 = [mod[:, k * d:(k + 1) * d] for k in range(N_MOD)]
    (f_in,), (rel_in, near_in), sh1 = _gather_call(
        "gather_1", [f_in], [k_in], [("direct", [0], *dir_in)], [("relay", [0]), ("d2d_near", [0])], mod, sh1)

    h1, rstd1 = _norm_mod_fwd("norm_mod_fwd1", x2d, g_norm_mix, sc1, sh1)
    p_mo, tok = _place_cast("place_cast_3", j_arr, w_mlp_out[0], k_mo, rstd1)
    (f_in, f_out, f_mi), (far_in, dir_out, dir_mi), h1 = _gather_call(
        "gather_2", [f_in, p_out, p_mi], kinds[:3], [("relay", [0], *rel_in)],
        [("d2d_far", [0]), ("direct", [1]), ("direct", [2])], tok, h1)
    (w_in_f,), _, h1 = _gather_call("gather_3", [f_in], [k_in],
                                    [("d2d_near", [0], *near_in), ("d2d_far", [0], *far_in)], [], rstd1, h1)
    (qkv,) = _matmul("mm_qkv", h1, w_in_f, "nn", s, 3 * aw, d, (BF16,))
    (xrg,) = _matmul("mm_xrg", h1, w_in_f, "nn", s, 2 * aw, d, (F32,), b_off=3 * aw)
    (f_out, f_mo), (rel_out, near_out, dir_mo), qkv = _gather_call(
        "gather_3b", [f_out, p_mo], [k_out, k_mo], [("direct", [0], *dir_out)],
        [("relay", [0]), ("d2d_near", [0]), ("direct", [1])], xrg, qkv)
    o_attn, attn_w, attn_sg = _attn_fwd(qkv, nh)
    (f_out, f_mi), (far_out, rel_mi, near_mi), xrg = _gather_call(
        "gather_4", [f_out, f_mi], [k_out, k_mi], [("relay", [0], *rel_out), ("direct", [1], *dir_mi)],
        [("d2d_far", [0]), ("relay", [1]), ("d2d_near", [1])], o_attn, xrg)
    wa3, wx3 = w_rg_a[0], w_rg_x[0]
    o_lru, hseq = _lru_fwd(xrg, w_conv_full, b_conv, wa3, b_rg_a, wx3, b_rg_x, lru_lambda)
    mixed, rstd_a, rstd_l = _mix_norm_fwd(o_attn, o_lru, g_attn_out, g_lru_out)
    (f_mo,), (rel_mo, near_mo), mixed = _gather_call(
        "gather_5", [f_mo], [k_mo], [("direct", [0], *dir_mo)], [("relay", [0]), ("d2d_near", [0])], rstd_a, mixed)
    (w_out_f,), _, mixed = _gather_call(
        "gather_6", [f_out], [k_out], [("d2d_near", [0], *near_out), ("d2d_far", [0], *far_out)], [], rstd_l, mixed)

    def residual(acc, xin, gt):
        return acc, xin + gt * acc

    y1, x1 = _matmul("mm_out", mixed, w_out_f, "nn", s, d, d, (BF16, F32), extras=(x2d, gt1),
                     extra_kinds=("tile", "row"), epilogue=residual)
    (f_mi,), (far_mi,), x1 = _gather_call("gather_6b", [f_mi], [k_mi], [("relay", [0], *rel_mi)],
                                          [("d2d_far", [0])], y1, x1)
    h2, rstd2 = _norm_mod_fwd("norm_mod_fwd2", x1, g_norm_mlp, sc2, sh2)
    (w_mi_f, f_mo), (far_mo,), h2 = _gather_call(
        "gather_7", [f_mi, f_mo], [k_mi, k_mo],
        [("d2d_near", [0], *near_mi), ("d2d_far", [0], *far_mi), ("relay", [1], *rel_mo)], [("d2d_far", [1])], rstd2, h2)

    def sq_relu(acc):
        r = jnp.maximum(acc, 0.0)
        return 2.0 * r, r * r

    r2, hid = _matmul("mm_mlp_in", h2, w_mi_f, "nn", s, f, d, (BF16, BF16), epilogue=sq_relu)
    (w_mo_f,), _, hid = _gather_call("gather_8", [f_mo], [k_mo],
                                     [("d2d_near", [0], *near_mo), ("d2d_far", [0], *far_mo)], [], r2, hid)
    y2, x2 = _matmul("mm_mlp_out", hid, w_mo_f, "nn", s, d, f, (BF16, F32), extras=(x1, gt2),
                     extra_kinds=("tile", "row"), epilogue=residual)
    dx2, loss_row, dg_final, dy2, dgt2 = _final_loss(x2, g_norm_final.reshape(1, d), tgt, y2, gt2)

    oc_arr = 1 - c_arr

    def dw_half(name, st, h_arr, got=None):
        add = {} if got is None else dict(extras=(got,), extra_kinds=("tile",),
                                          epilogue=lambda acc, g_: (acc + g_.astype(F32),))
        tn = st["n"] if st["tm"] * 4 <= MM_TILE_M else MM_TILE_N
        (out,) = _matmul(name, st["a"], st["dy"], "tn", st["m"], st["n"], s, (BF16,), tm=st["tm"], tn=tn,
                         m_half=h_arr, **add)
        return out

    def rs_begin(tag, kind, xa, dy, m, n, thru, prev=None):
        st = {"tag": tag, "kind": kind, "a": xa, "dy": dy, "m": m, "n": n,
              "tm": m // (2 * N_CHIP) if kind == "row" else m // 2}
        first = dw_half("mm_dw_%s_a" % tag, st, oc_arr)
        swap = (first, first.shape, 1, _plan_whole)
        if prev is None:
            send, recv, first, land, thru = _split_start("rs_swap_start_" + tag, *swap, thru)
            st["swap"] = (send, recv, first, land)
        else:
            prev["scatter"], st["swap"], thru = _split_start_pair(
                "rs_scatter_%s_swap_%s_start" % (prev["tag"], tag), prev.pop("to_scatter"), swap, thru)
        return st, thru

    def rs_mid(st, after, thru, last=False):
        tag, kind = st["tag"], st["kind"]
        _, got = _split_wait("rs_swap_wait_" + tag, *st["swap"], _plan_whole, after)
        part = dw_half("mm_dw_%s_b" % tag, st, c_arr, got)
        if kind == "row":
            part = part.reshape(N_CHIP, st["m"] // (2 * N_CHIP), st["n"])
        blk = (part.shape[0], part.shape[1] // N_CHIP) if kind == "col" else part.shape[1:]
        scatter = (part, (N_CHIP - 1,) + blk, N_CHIP - 1, _plan_scatter(kind))
        if last:
            send, recv, part, land, thru = _split_start("rs_scatter_start_" + tag, *scatter, thru)
            st["scatter"] = (send, recv, part, land)
        else:
            st["to_scatter"] = scatter
        return thru

    def rs_end(st, after):
        tag, kind = st["tag"], st["kind"]
        part, got = _split_wait("rs_scatter_wait_" + tag, *st["scatter"], _plan_scatter(kind), after)
        return _sum_partials("sum_partials_" + tag, j_arr, part, got, kind)

    (dpre,) = _matmul("mm_dhid", dy2, w_mo_f, "nt", s, f, d, (BF16,), extras=(r2,), extra_kinds=("tile",),
                      epilogue=lambda acc, r: (acc * r.astype(F32),))
    st_mo, dpre = rs_begin("mo", "row", hid, dy2, f, d, dpre)
    (dh2,) = _matmul("mm_dh2", dpre, w_mi_f, "nt", s, d, f, (BF16,))
    dh2 = rs_mid(st_mo, dh2, dh2)
    st_mi, dh2 = rs_begin("mi", "col", h2, dpre, d, f, dh2, prev=st_mo)
    dx1, dsh2, dsc2, dg_mlp, dy1, dgt1 = _norm_mod_bwd("norm_mod_bwd2", dh2, x1, rstd2, g_norm_mlp, sc2, dx2,
                                                       gate=(y1, gt1))
    (dmixed,) = _matmul("mm_dmixed", dy1, w_out_f, "nt", s, d, d, (BF16,))
    dmixed = rs_mid(st_mi, dmixed, dmixed)
    st_out, dmixed = rs_begin("out", "row", mixed, dy1, d, d, dmixed, prev=st_mi)
    do_attn, do_lru, dg_attn, dg_lru = _mix_norm_bwd(dmixed, o_attn, o_lru, rstd_a, rstd_l, g_attn_out, g_lru_out)
    dq, dk, dv = _attn_bwd(qkv, do_attn, attn_w, attn_sg, nh)
    do_lru = rs_mid(st_out, dq, do_lru)
    dxr, dxg, dwconv, dbconv, dwa, dba, dwx, dbx, dlam = _lru_bwd(
        xrg, do_lru, hseq, w_conv_full, b_conv, wa3, b_rg_a, wx3, b_rg_x, lru_lambda)
    dproj = jnp.concatenate([dq, dk, dv, dxr, dxg], axis=-1)
    st_in, dproj = rs_begin("in", "col", h1, dproj, d, 5 * aw, dproj, prev=st_out)
    st_in["dy"] = dproj
    (dh1,) = _matmul("mm_dh1", dproj, w_in_f, "nt", s, d, 5 * aw, (BF16,))
    dh1 = rs_mid(st_in, dh1, dh1, last=True)
    grad_x, dsh1, dsc1, dg_mix = _norm_mod_bwd("norm_mod_bwd1", dh1, x2d, rstd1, g_norm_mix, sc1, dx1)

    dmod = jnp.concatenate([dsh1, dsc1, dgt1, dsh2, dsc2, dgt2], axis=-1)
    small_names = ["b_ada", "g_norm_mix", "b_conv", "w_rg_a", "b_rg_a", "w_rg_x", "b_rg_x", "lru_lambda",
                   "g_attn_out", "g_lru_out", "g_norm_mlp", "g_norm_final"]
    small_g = [dmod, dg_mix, dbconv, dwa, dba, dwx, dbx, dlam, dg_attn, dg_lru, dg_mlp, dg_final]
    small_w = [b_ada, g_norm_mix, b_conv, w_rg_a, b_rg_a, w_rg_x, b_rg_x, lru_lambda, g_attn_out, g_lru_out, g_norm_mlp, g_norm_final]
    small_m = [m_b_ada, m_g_norm_mix, m_b_conv, m_w_rg_a, m_b_rg_a, m_w_rg_x, m_b_rg_x, m_lru_lambda, m_g_attn_out, m_g_lru_out, m_g_norm_mlp, m_g_norm_final]
    small_v = [v_b_ada, v_g_norm_mix, v_b_conv, v_w_rg_a, v_b_rg_a, v_w_rg_x, v_b_rg_x, v_lru_lambda, v_g_attn_out, v_g_lru_out, v_g_norm_mlp, v_g_norm_final]
    extra_zero = [jnp.zeros_like(dwconv), jnp.zeros((LANES,), F32)]
    g_slab, g_sizes = _pack(small_g + [dwconv, loss_row])
    w_slab, _ = _pack(small_w + extra_zero)
    m_slab, _ = _pack(small_m + extra_zero)
    v_slab, _ = _pack(small_v + extra_zero)
    g_buf = _place_row("place_small_grads", me_arr, g_slab)
    sg_send, sg_recv, g_buf, tok = _split_start_inplace("sg_gather_start", g_buf, N_CHIP, _plan_gather_own, loss_row)

    def reduced_begin(tag, half, tok_):
        send, recv, half, land, tok_ = _split_start("rs_reduced_start_" + tag, half, half.shape, 1, _plan_whole, tok_)
        return (send, recv, half, land), tok_

    def reduced_end(tag, st, after):
        return _split_wait("rs_reduced_wait_" + tag, *st, _plan_whole, after)

    sw_mo, tok = reduced_begin("mo", rs_end(st_mo, tok), tok)
    sw_mi, tok = reduced_begin("mi", rs_end(st_mi, tok), tok)
    sw_out, tok = reduced_begin("out", rs_end(st_out, tok), tok)
    half_mo, got_mo = reduced_end("mo", sw_mo, tok)
    big = {"w_mlp_out": _adamw_halves("adamw_w_mlp_out", c_arr, w_mlp_out[0], m_w_mlp_out[0], v_w_mlp_out[0],
                                      half_mo, got_mo)}
    half_mi, got_mi = reduced_end("mi", sw_mi, big["w_mlp_out"][1])
    big["w_mlp_in"] = _adamw_halves("adamw_w_mlp_in", c_arr, w_mlp_in[0], m_w_mlp_in[0], v_w_mlp_in[0], half_mi, got_mi)
    half_out, got_out = reduced_end("out", sw_out, big["w_mlp_in"][1])
    big["w_out"] = _adamw_halves("adamw_w_out", c_arr, w_out[0], m_w_out[0], v_w_out[0], half_out, got_out)
    g_buf = _split_wait_inplace("sg_gather_wait", sg_send, sg_recv, g_buf, _plan_gather_own, big["w_out"][1])
    sg_send, sg_recv, g_buf, tok = _split_start_inplace("sg_pass_start", g_buf, N_CHIP - 1, _plan_gather_pass, tok)
    sw_in, tok = reduced_begin("in", rs_end(st_in, tok), tok)
    g_all = _split_wait_inplace("sg_pass_wait", sg_send, sg_recv, g_buf, _plan_gather_pass, tok)
    gs_slab, ds_slab, ms_slab, vs_slab = _small_reduce_adamw(g_all, w_slab, m_slab, v_slab)
    shapes = [w.shape for w in small_w] + [dwconv.shape, (LANES,)]
    gs = _unpack(gs_slab, g_sizes, shapes)
    ds = _unpack(ds_slab, g_sizes, shapes)
    ms = _unpack(ms_slab, g_sizes, shapes)
    vs = _unpack(vs_slab, g_sizes, shapes)
    small = {n: (gs[i], ds[i], ms[i], vs[i]) for i, n in enumerate(small_names)}
    loss = gs[-1][0]
    g_wconv = lax.dynamic_slice(gs[-2], (0, chip * n_cv), (CONV_TAPS, n_cv))
    d_wconv, m_wconv, v_wconv = _adamw_plain("adamw_conv", w_conv[0], g_wconv, m_w_conv[0], v_w_conv[0])
    small["w_conv"] = (g_wconv[None], d_wconv[None], m_wconv[None], v_wconv[None])

    dmod_all = g_all[:, :N_MOD * d // LANES, :].reshape(N_DEV, N_MOD * d)
    dmod_sel = lax.dynamic_slice(dmod_all, (0, chip * n_ada), (N_DEV, n_ada)).astype(BF16)
    act_t = act_all.T.astype(BF16)
    big["w_ada"] = _adamw_ada(w_ada[0], m_w_ada[0], v_w_ada[0], act_t, dmod_sel)

    half_in, got_in = reduced_end("in", sw_in, big["w_ada"][1])
    big["w_in"] = _adamw_halves("adamw_w_in", c_arr, w_in[0], m_w_in[0], v_w_in[0], half_in, got_in)

    order = ["w_ada", "b_ada", "g_norm_mix", "w_in", "w_conv", "b_conv", "w_rg_a", "b_rg_a", "w_rg_x", "b_rg_x",
             "lru_lambda", "g_attn_out", "g_lru_out", "w_out", "g_norm_mlp", "w_mlp_in", "w_mlp_out", "g_norm_final"]
    res = {}
    for n in order:
        res[n] = tuple(t[None] for t in big[n]) if n in big else small[n]
    return (loss, grad_x[None],
            *[res[n][0] for n in order], *[res[n][1] for n in order],
            *[res[n][2] for n in order], *[res[n][3] for n in order])
```

```python
import functools
import math

import jax
import jax.numpy as jnp
from jax import lax
from jax.experimental import pallas as pl
from jax.experimental.pallas import tpu as pltpu

F32 = jnp.float32
BF16 = jnp.bfloat16
SDS = jax.ShapeDtypeStruct
MESH = pl.DeviceIdType.MESH

EPS = 1e-6
HEAD = 128
N_MOD = 6
CONV_TAPS = 4
LRU_C = 8.0
ADAM_LR, ADAM_B1, ADAM_B2, ADAM_EPS, ADAM_WD, ADAM_STEP = 0.001, 0.9, 0.999, 1e-08, 0.01, 10
N_DEV = 8
N_CHIP = 4
LANES = 128
SUBLANES = 8
VMEM_LIMIT = 56 * 1024 * 1024
PACK_ROWS = 256
MM_TILE_M, MM_TILE_N, MM_TILE_K = 1024, 1024, 2048
ROW_TILE = 256
ROW_SPLIT = 1


def _tile(dim, pref):
    t = min(dim, pref)
    while dim % t:
        t -= LANES
    return t


def _params(sem=None):
    return pltpu.CompilerParams(dimension_semantics=sem, vmem_limit_bytes=VMEM_LIMIT)


def _sigmoid(x):
    return 1.0 / (1.0 + jnp.exp(-x))


def _log_sigmoid(x):
    return jnp.minimum(x, 0.0) - jnp.log(1.0 + jnp.exp(-jnp.abs(x)))


def _gelu_parts(x):
    k0, k1 = math.sqrt(2.0 / math.pi), 0.044715
    t = jnp.tanh(k0 * (x + k1 * x * x * x))
    val = 0.5 * x * (1.0 + t)
    der = 0.5 * (1.0 + t) + 0.5 * x * (1.0 - t * t) * k0 * (1.0 + 3.0 * k1 * x * x)
    return val, der


def _dot(a, b):
    return jnp.dot(a, b, preferred_element_type=F32)


def _dot_nt(a, b):
    return lax.dot_general(a, b, (((1,), (1,)), ((), ())), preferred_element_type=F32)


def _dot_tn(a, b):
    return lax.dot_general(a, b, (((0,), (0,)), ((), ())), preferred_element_type=F32)


def _split_dot(x, tri):
    hi = x.astype(BF16)
    lo = (x - hi.astype(F32)).astype(BF16)
    return _dot(hi, tri) + _dot(lo, tri)


def _matmul(name, a, b, mode, m, n, k, out_dtypes, *, b_off=0, extras=(), extra_kinds=(), epilogue=None,
            tm=MM_TILE_M, tn=MM_TILE_N, tk=MM_TILE_K, m_half=None):
    tm, tn, tk = _tile(m, tm), _tile(math.gcd(n, b_off) if b_off else n, tn), _tile(k, tk)
    assert b_off % tn == 0
    nk = k // tk
    n_ex, n_out = len(extras), len(out_dtypes)
    dot = {"nn": _dot, "nt": _dot_nt, "tn": _dot_tn}[mode]
    n_pre = 0 if m_half is None else 1
    m_out = m if m_half is None else m // 2

    def body(*refs):
        a_ref, b_ref, *rest = refs[n_pre:]
        ex, outs = rest[:n_ex], rest[n_ex:n_ex + n_out]

        def finish(total):
            res = epilogue(total, *[e[...] for e in ex]) if epilogue else (total,)
            for o, r in zip(outs, res):
                o[...] = r.astype(o.dtype)

        if nk == 1:
            finish(dot(a_ref[...], b_ref[...]))
            return
        acc = rest[-1]
        kk = pl.program_id(2)

        @pl.when(kk == 0)
        def _():
            acc[...] = dot(a_ref[...], b_ref[...])

        @pl.when(jnp.logical_and(kk > 0, kk < nk - 1))
        def _():
            acc[...] += dot(a_ref[...], b_ref[...])

        @pl.when(kk == nk - 1)
        def _():
            finish(acc[...] + dot(a_ref[...], b_ref[...]))

    if mode == "nn":
        a_spec = pl.BlockSpec((tm, tk), lambda i, j, kk, *_: (i, kk))
        b_spec = pl.BlockSpec((tk, tn), lambda i, j, kk, *_: (kk, j + b_off // tn))
    elif mode == "nt":
        a_spec = pl.BlockSpec((tm, tk), lambda i, j, kk, *_: (i, kk))
        b_spec = pl.BlockSpec((tn, tk), lambda i, j, kk, *_: (j, kk + b_off // tk))
    elif m_half is None:
        a_spec = pl.BlockSpec((tk, tm), lambda i, j, kk: (kk, i))
        b_spec = pl.BlockSpec((tk, tn), lambda i, j, kk: (kk, j))
    else:
        a_spec = pl.BlockSpec((tk, tm), lambda i, j, kk, h: (kk, 2 * i + h[0]))
        b_spec = pl.BlockSpec((tk, tn), lambda i, j, kk, h: (kk, j))
    tile_spec = pl.BlockSpec((tm, tn), lambda i, j, kk, *_: (i, j))
    row_spec = pl.BlockSpec((1, tn), lambda i, j, kk, *_: (0, j))
    in_specs = [a_spec, b_spec] + [tile_spec if kind == "tile" else row_spec for kind in extra_kinds]
    out_specs = tuple(tile_spec for _ in out_dtypes)
    out_shape = tuple(SDS((m_out, n), dt) for dt in out_dtypes)
    scratch = [pltpu.VMEM((tm, tn), F32)] if nk > 1 else []
    grid = (m_out // tm, n // tn, nk)
    semantics = _params(("parallel", "parallel", "arbitrary"))
    if m_half is None:
        return pl.pallas_call(body, grid=grid, in_specs=in_specs, out_specs=out_specs, out_shape=out_shape,
                              scratch_shapes=scratch, compiler_params=semantics, name=name)(a, b, *extras)
    assert mode == "tn" and (m // tm) % 2 == 0
    return pl.pallas_call(
        body,
        grid_spec=pltpu.PrefetchScalarGridSpec(num_scalar_prefetch=1, grid=grid, in_specs=in_specs, out_specs=out_specs,
                                               scratch_shapes=scratch),
        out_shape=out_shape, compiler_params=semantics, name=name)(m_half, a, b, *extras)


def _row_specs(s, d, tr):
    row = "row"
    vec = pl.BlockSpec((1, d), lambda i: (0, 0))
    col = pl.BlockSpec((tr, 1), lambda i: (i, 0))
    return row, vec, col


class _ColChunks:
    def __init__(self, refs):
        self.refs = refs

    def __getitem__(self, idx):
        return jnp.concatenate([r[...] for r in self.refs], axis=-1)


def _rows_call(name, body, grid, in_specs, out_specs, out_shape, semantics, args):
    in_x, args_x, groups = [], [], []
    for spec, arr in zip(in_specs, args):
        if isinstance(spec, str):
            rows, d = arr.shape
            tr, dc = rows // grid[0], d // ROW_SPLIT
            in_x += [pl.BlockSpec((tr, dc), functools.partial(lambda i, jj: (i, jj), jj=j)) for j in range(ROW_SPLIT)]
            args_x += [arr] * ROW_SPLIT
            groups.append(ROW_SPLIT)
        else:
            in_x.append(spec)
            args_x.append(arr)
            groups.append(1)
    out_x = [pl.BlockSpec((sh.shape[0] // grid[0], sh.shape[1]), lambda i: (i, 0)) if isinstance(spec, str) else spec
             for spec, sh in zip(out_specs, out_shape)]

    def wrapped(*refs):
        views, k = [], 0
        for g in groups:
            views.append(_ColChunks(refs[k:k + g]) if g > 1 else refs[k])
            k += g
        body(*views, *refs[k:])

    return pl.pallas_call(
        wrapped, grid=grid, in_specs=in_x, out_specs=tuple(out_x), out_shape=tuple(out_shape),
        compiler_params=_params(semantics), name=name)(*args_x)


ROW_RING = 3


def _norm_mod_fwd(name, x, g, sc, sh):
    s, d = x.shape
    tr = _tile(s, ROW_TILE)
    n = s // tr

    def body(x_hbm, g_ref, sc_ref, sh_ref, h_hbm, r_ref, xbuf, hbuf, in_sem, out_sem):
        def read(i):
            return pltpu.make_async_copy(x_hbm.at[pl.ds(i * tr, tr), :], xbuf.at[i % ROW_RING], in_sem.at[i % ROW_RING])

        def write(i):
            return pltpu.make_async_copy(hbuf.at[i % 2], h_hbm.at[pl.ds(i * tr, tr), :], out_sem.at[i % 2])

        for i in range(min(ROW_RING, n)):
            read(i).start()
        for i in range(n):
            read(i).wait()
            xv = xbuf[i % ROW_RING]
            r = lax.rsqrt(jnp.mean(xv * xv, axis=-1, keepdims=True) + EPS)
            hv = ((xv * r * g_ref[...]) * (1.0 + sc_ref[...]) + sh_ref[...]).astype(BF16)
            if i >= 2:
                write(i - 2).wait()
            hbuf[i % 2] = hv
            r_ref[i * tr:(i + 1) * tr, :] = r
            write(i).start()
            if i + ROW_RING < n:
                read(i + ROW_RING).start()
        for i in range(max(n - 2, 0), n):
            write(i).wait()

    vmem = pl.BlockSpec(memory_space=pltpu.VMEM)
    return pl.pallas_call(
        body, in_specs=[_ANY, vmem, vmem, vmem], out_specs=(_ANY, vmem),
        out_shape=(SDS((s, d), BF16), SDS((s, 1), F32)),
        scratch_shapes=[pltpu.VMEM((ROW_RING, tr, d), F32), pltpu.VMEM((2, tr, d), BF16),
                        pltpu.SemaphoreType.DMA((ROW_RING,)), pltpu.SemaphoreType.DMA((2,))],
        compiler_params=pltpu.CompilerParams(vmem_limit_bytes=VMEM_LIMIT), name=name)(x, g, sc, sh)


def _norm_mod_bwd(name, dh, xin, rstd, g, sc, dres, gate=None):
    s, d = xin.shape
    tr = _tile(s, ROW_TILE)
    row, vec, col = _row_specs(s, d, tr)

    n_gate = 2 if gate is not None else 0

    def body(dh_ref, x_ref, r_ref, g_ref, sc_ref, dres_ref, *rest):
        gate_in, gate_out = rest[:n_gate], rest[n_gate + 4:]
        dx_ref, dsh_ref, dsc_ref, dg_ref = rest[n_gate:n_gate + 4]

        @pl.when(pl.program_id(0) == 0)
        def _():
            for ref in (dsh_ref, dsc_ref, dg_ref) + tuple(gate_out[1:]):
                ref[...] = jnp.zeros_like(ref)

        dh_v, xv, r, gv = dh_ref[...].astype(F32), x_ref[...], r_ref[...], g_ref[...]
        n0 = xv * r
        dsh_ref[...] += jnp.sum(dh_v, axis=0, keepdims=True)
        dsc_ref[...] += jnp.sum(dh_v * (n0 * gv), axis=0, keepdims=True)
        dn = dh_v * (1.0 + sc_ref[...])
        dg_ref[...] += jnp.sum(dn * n0, axis=0, keepdims=True)
        gy = dn * gv
        dot = jnp.mean(gy * xv, axis=-1, keepdims=True)
        dxv = dres_ref[...] + r * gy - xv * (r * r * r * dot)
        dx_ref[...] = dxv
        if gate is not None:
            y_ref, gt_ref = gate_in
            dy_ref, dgt_ref = gate_out
            dy_ref[...] = (gt_ref[...] * dxv).astype(BF16)
            dgt_ref[...] += jnp.sum(dxv * y_ref[...], axis=0, keepdims=True)

    vecs = SDS((1, d), F32)
    gate_args = tuple(gate) if gate is not None else ()
    return _rows_call(
        name, body, (s // tr,),
        [row, row, col, vec, vec, row] + ([row, vec] if gate is not None else []),
        (row, vec, vec, vec) + ((row, vec) if gate is not None else ()),
        (SDS((s, d), F32), vecs, vecs, vecs) + ((SDS((s, d), BF16), vecs) if gate is not None else ()),
        ("arbitrary",), (dh, xin, rstd, g, sc, dres, *gate_args))


def _final_loss(x2, gf, tgt, y, gt):
    s, d = x2.shape
    tr = _tile(s, ROW_TILE)
    row, vec, _ = _row_specs(s, d, tr)
    lrow = pl.BlockSpec((1, LANES), lambda i: (0, 0))

    def body(x_ref, g_ref, t_ref, y_ref, gt_ref, dx_ref, loss_ref, dg_ref, dy_ref, dgt_ref):
        @pl.when(pl.program_id(0) == 0)
        def _():
            loss_ref[...] = jnp.zeros_like(loss_ref)
            dg_ref[...] = jnp.zeros_like(dg_ref)
            dgt_ref[...] = jnp.zeros_like(dgt_ref)

        xv, gv = x_ref[...], g_ref[...]
        r = lax.rsqrt(jnp.mean(xv * xv, axis=-1, keepdims=True) + EPS)
        n0 = xv * r
        err = n0 * gv - t_ref[...]
        loss_ref[...] += jnp.sum(err * err) * (0.5 / d)
        dy = err * (1.0 / d)
        dg_ref[...] += jnp.sum(dy * n0, axis=0, keepdims=True)
        gy = dy * gv
        dot = jnp.mean(gy * xv, axis=-1, keepdims=True)
        dxv = r * gy - xv * (r * r * r * dot)
        dx_ref[...] = dxv
        dy_ref[...] = (gt_ref[...] * dxv).astype(BF16)
        dgt_ref[...] += jnp.sum(dxv * y_ref[...], axis=0, keepdims=True)

    return _rows_call(
        "final_loss", body, (s // tr,), [row, vec, row, row, vec], (row, lrow, vec, row, vec),
        (SDS((s, d), F32), SDS((1, LANES), F32), SDS((1, d), F32), SDS((s, d), BF16), SDS((1, d), F32)),
        ("arbitrary",), (x2, gf, tgt, y, gt))


def _mix_norm_fwd(oa, ol, ga, gl):
    s, w = oa.shape
    tr = _tile(s, ROW_TILE)
    row, vec, col = _row_specs(s, w, tr)

    def body(oa_ref, ol_ref, ga_ref, gl_ref, mx_ref, ra_ref, rl_ref):
        a, l = oa_ref[...], ol_ref[...]
        ra = lax.rsqrt(jnp.mean(a * a, axis=-1, keepdims=True) + EPS)
        rl = lax.rsqrt(jnp.mean(l * l, axis=-1, keepdims=True) + EPS)
        mx_ref[:, :w] = (a * ra * ga_ref[...]).astype(BF16)
        mx_ref[:, w:] = (l * rl * gl_ref[...]).astype(BF16)
        ra_ref[...] = ra
        rl_ref[...] = rl

    return _rows_call(
        "mix_norm_fwd", body, (s // tr,), [row, row, vec, vec], (row, col, col),
        (SDS((s, 2 * w), BF16), SDS((s, 1), F32), SDS((s, 1), F32)), ("parallel",), (oa, ol, ga, gl))


def _mix_norm_bwd(dmx, oa, ol, ra, rl, ga, gl):
    s, w = oa.shape
    tr = _tile(s, ROW_TILE)
    row, vec, col = _row_specs(s, w, tr)

    def body(dm_ref, oa_ref, ol_ref, ra_ref, rl_ref, ga_ref, gl_ref, doa_ref, dol_ref, dga_ref, dgl_ref):
        @pl.when(pl.program_id(0) == 0)
        def _():
            dga_ref[...] = jnp.zeros_like(dga_ref)
            dgl_ref[...] = jnp.zeros_like(dgl_ref)

        def one(dy, xv, r, gv, dg_ref):
            dg_ref[...] += jnp.sum(dy * (xv * r), axis=0, keepdims=True)
            gy = dy * gv
            dot = jnp.mean(gy * xv, axis=-1, keepdims=True)
            return r * gy - xv * (r * r * r * dot)

        dm = dm_ref[...].astype(F32)
        doa_ref[...] = one(dm[:, :w], oa_ref[...], ra_ref[...], ga_ref[...], dga_ref).astype(BF16)
        dol_ref[...] = one(dm[:, w:], ol_ref[...], rl_ref[...], gl_ref[...], dgl_ref)

    return _rows_call(
        "mix_norm_bwd", body, (s // tr,), [row, row, row, col, col, vec, vec], (row, row, vec, vec),
        (SDS((s, w), BF16), SDS((s, w), F32), SDS((1, w), F32), SDS((1, w), F32)), ("arbitrary",),
        (dmx, oa, ol, ra, rl, ga, gl))


def _attn_blocks(qs, ks, tri_after, csums, causal):
    zs = [_dot_nt(q, k) * (HEAD ** -0.5) for q, k in zip(qs, ks)]
    lbs = [_log_sigmoid(z) for z in zs]
    lss = [lb - z for lb, z in zip(lbs, zs)]
    if causal is not None:
        lss = [jnp.where(causal, ls, 0.0) for ls in lss]
    locs = [_split_dot(ls, tri_after) for ls in lss]
    ws = [jnp.exp(lb + (loc + cs)) for lb, loc, cs in zip(lbs, locs, csums)]
    if causal is not None:
        ws = [jnp.where(causal, w, 0.0) for w in ws]
    nxt = [cs + (loc[:, 0:1] + ls[:, 0:1]) for cs, loc, ls in zip(csums, locs, lss)]
    return lbs, ws, nxt


ATTN_HEADS_PER_STEP = 4


def _attn_tile(s):
    return 256 if s >= 1024 else 128


def _tri(t, after):
    r_i = lax.broadcasted_iota(jnp.int32, (t, t), 0)
    c_i = lax.broadcasted_iota(jnp.int32, (t, t), 1)
    return ((r_i > c_i) if after else (r_i < c_i)).astype(BF16)


def _attn_fwd(qkv, n_heads):
    s = qkv.shape[0]
    t = _attn_tile(s)
    hps = ATTN_HEADS_PER_STEP
    wid = hps * HEAD

    nq = s // t

    def body(q_ref, k_ref, v_ref, o_ref, w_ref, sg_ref):
        qi = pl.program_id(1)
        tri_after = _tri(t, True)
        causal = lax.broadcasted_iota(jnp.int32, (t, t), 1) < lax.broadcasted_iota(jnp.int32, (t, t), 0)
        lanes = [slice(a * HEAD, (a + 1) * HEAD) for a in range(hps)]
        qs = [q_ref[:, ln] for ln in lanes]

        def block(kb, carry, mask):
            off = pl.multiple_of(kb * t, t)
            ks = [k_ref[pl.ds(off, t), ln] for ln in lanes]
            lbs, ws, csums = _attn_blocks(qs, ks, tri_after, [cr[0] for cr in carry], mask)
            wbs = [w.astype(BF16) for w in ws]
            for a in range(hps):
                w_ref[a, kb] = wbs[a]
                sg_ref[a, kb] = jnp.exp(lbs[a]).astype(BF16)
            os_ = [cr[1] + _dot(wb, v_ref[pl.ds(off, t), ln]) for cr, wb, ln in zip(carry, wbs, lanes)]
            return tuple(zip(csums, os_))

        zero = tuple((jnp.zeros((t, 1), F32), jnp.zeros((t, HEAD), F32)) for _ in lanes)
        carry = block(qi, zero, causal)
        carry = lax.fori_loop(1, qi + 1, lambda it, cr: block(qi - it, cr, None), carry)
        for a, ln in enumerate(lanes):
            o_ref[:, ln] = carry[a][1]

    hb = n_heads // hps
    kept = pl.BlockSpec((None, hps, nq, t, t), lambda hh, i: (hh * nq + i, 0, 0, 0, 0))
    kept_shape = SDS((hb * nq, hps, nq, t, t), BF16)
    return pl.pallas_call(
        body, grid=(hb, nq),
        in_specs=[pl.BlockSpec((t, wid), lambda hh, i: (i, hh)),
                  pl.BlockSpec((s, wid), lambda hh, i: (0, hb + hh)),
                  pl.BlockSpec((s, wid), lambda hh, i: (0, 2 * hb + hh))],
        out_specs=(pl.BlockSpec((t, wid), lambda hh, i: (i, hh)), kept, kept),
        out_shape=(SDS((s, n_heads * HEAD), F32), kept_shape, kept_shape),
        compiler_params=_params(("parallel", "parallel")), name="attn_fwd")(qkv, qkv, qkv)


def _attn_bwd(qkv, do, w_kept, sg_kept, n_heads):
    s = qkv.shape[0]
    t = _attn_tile(s)
    nq = s // t
    scale = HEAD ** -0.5
    hps = ATTN_HEADS_PER_STEP
    wid = hps * HEAD

    def body(q_ref, k_ref, v_ref, do_ref, w_ref, sg_ref, dq_ref, dk_ref, dv_ref, dk_acc, dv_acc):
        qi = pl.program_id(1)

        @pl.when(qi == 0)
        def _():
            dk_acc[...] = jnp.zeros_like(dk_acc)
            dv_acc[...] = jnp.zeros_like(dv_acc)

        tri_before = _tri(t, False)
        causal = lax.broadcasted_iota(jnp.int32, (t, t), 1) < lax.broadcasted_iota(jnp.int32, (t, t), 0)
        lanes = [slice(a * HEAD, (a + 1) * HEAD) for a in range(hps)]
        qs = [q_ref[:, ln] for ln in lanes]
        douts = [do_ref[:, ln] for ln in lanes]

        def block(kb, carry, mask):
            off = pl.multiple_of(kb * t, t)
            wbs = [w_ref[a, kb] for a in range(hps)]
            dws = [_dot_nt(dout, v_ref[pl.ds(off, t), ln]) for dout, ln in zip(douts, lanes)]
            for a, ln in enumerate(lanes):
                dv_acc[pl.ds(off, t), ln] += _dot_tn(wbs[a], douts[a])
            es = [dw * wb.astype(F32) for dw, wb in zip(dws, wbs)]
            locs = [_split_dot(e, tri_before) for e in es]
            sgs = [sg_ref[a, kb].astype(F32) for a in range(hps)]
            stays = [(loc + cr[0]) * sg for loc, cr, sg in zip(locs, carry, sgs)]
            if mask is not None:
                stays = [jnp.where(mask, st, 0.0) for st in stays]
            dzbs = [((e * (1.0 - sg) - st) * scale).astype(BF16) for e, sg, st in zip(es, sgs, stays)]
            dqs = [cr[1] + _dot(dzb, k_ref[pl.ds(off, t), ln]) for cr, dzb, ln in zip(carry, dzbs, lanes)]
            for a, ln in enumerate(lanes):
                dk_acc[pl.ds(off, t), ln] += _dot_tn(dzbs[a], qs[a])
            esums = [cr[0] + (loc[:, t - 1:t] + e[:, t - 1:t]) for cr, loc, e in zip(carry, locs, es)]
            return tuple(zip(esums, dqs))

        zero = tuple((jnp.zeros((t, 1), F32), jnp.zeros((t, HEAD), F32)) for _ in lanes)
        carry = lax.fori_loop(0, qi, lambda kb, cr: block(kb, cr, None), zero)
        carry = block(qi, carry, causal)
        for a, ln in enumerate(lanes):
            dq_ref[:, ln] = carry[a][1].astype(BF16)

        @pl.when(qi == nq - 1)
        def _():
            dk_ref[...] = dk_acc[...].astype(BF16)
            dv_ref[...] = dv_acc[...].astype(BF16)

    hb = n_heads // hps
    blk = pl.BlockSpec((t, wid), lambda hh, i: (i, hh))
    full = pl.BlockSpec((s, wid), lambda hh, i: (0, hh))
    kept = pl.BlockSpec((None, hps, nq, t, t), lambda hh, i: (hh * nq + i, 0, 0, 0, 0))
    return pl.pallas_call(
        body, grid=(hb, nq),
        in_specs=[blk,
                  pl.BlockSpec((s, wid), lambda hh, i: (0, hb + hh)),
                  pl.BlockSpec((s, wid), lambda hh, i: (0, 2 * hb + hh)),
                  blk, kept, kept],
        out_specs=(blk, full, full),
        out_shape=(SDS((s, n_heads * HEAD), BF16),) * 3,
        scratch_shapes=[pltpu.VMEM((s, wid), F32), pltpu.VMEM((s, wid), F32)],
        compiler_params=_params(("parallel", "arbitrary")), name="attn_bwd")(qkv, qkv, qkv, do, w_kept, sg_kept)


def _lru_chunk(s):
    return 128 if s >= 256 else s // 2


def _lru_gates(xc, wa, ba, wx, bx, sp):
    xb = xc.astype(BF16)
    r = _sigmoid(_dot(xb, wa) + ba)
    ig = _sigmoid(_dot(xb, wx) + bx)
    la = -LRU_C * r * sp
    a = jnp.exp(la)
    t = jnp.tanh(la)
    mult = jnp.sqrt(-2.0 * t / (1.0 - t))
    return r, ig, a, mult


def _softplus_neg(lam):
    return jnp.maximum(-lam, 0.0) + jnp.log(1.0 + jnp.exp(-jnp.abs(lam)))


LRU_BLOCKS_PER_STEP = 1
SCAN_GROUP = 4


def _lru_specs(s, n_blocks):
    bps = min(LRU_BLOCKS_PER_STEP, n_blocks)
    wid = bps * HEAD
    seq0 = pl.BlockSpec((s, wid), lambda h: (0, h))
    seq1 = pl.BlockSpec((s, wid), lambda h: (0, n_blocks // bps + h))
    taps = pl.BlockSpec((CONV_TAPS, wid), lambda h: (0, h))
    vec = pl.BlockSpec((1, wid), lambda h: (0, h))
    mat = pl.BlockSpec((bps, HEAD, HEAD), lambda h: (h, 0, 0))
    return bps, seq0, seq1, taps, vec, mat


def _per_block(one_block, n_2d, n_mat_pos, bps):
    def body(*refs):
        for a in range(bps):
            views = [r.at[a] if i in n_mat_pos else r.at[:, pl.ds(a * HEAD, HEAD)] for i, r in enumerate(refs[:n_2d])]
            one_block(*views, *refs[n_2d:])
    return body


def _lru_fwd(xrg, wconv, bconv, wa, ba, wx, bx, lam):
    s = xrg.shape[0]
    nb = wa.shape[0]
    tc = _lru_chunk(s)
    bps, seq0, seq1, taps, vec, mat = _lru_specs(s, nb)
    pad = SUBLANES

    def one_block(xr_ref, xg_ref, wc_ref, bc_ref, wa_ref, ba_ref, wx_ref, bx_ref, lam_ref, o_ref, h_ref, pad_s, a_s, u_s):
        pad_s[0:pad, :] = jnp.zeros((pad, HEAD), F32)
        pad_s[pad:pad + s, :] = xr_ref[...]
        wab, wxb = wa_ref[...].astype(BF16), wx_ref[...].astype(BF16)
        sp = _softplus_neg(lam_ref[...])
        for c in range(s // tc):
            base = c * tc
            xc = bc_ref[...] + sum(wc_ref[i:i + 1, :] * pad_s[pl.ds(base + pad - (CONV_TAPS - 1) + i, tc), :]
                                   for i in range(CONV_TAPS))
            _, ig, a, mult = _lru_gates(xc, wab, ba_ref[...], wxb, bx_ref[...], sp)
            a_s[base:base + tc, :] = a
            u_s[base:base + tc, :] = mult * (ig * xc)

        row = lax.broadcasted_iota(jnp.int32, (SUBLANES, HEAD), 0)
        last = SUBLANES - 1

        def group(gi, hprev):
            offs = [pl.multiple_of((gi * SCAN_GROUP + q) * SUBLANES, SUBLANES) for q in range(SCAN_GROUP)]
            ab = []
            for off in offs:
                a8, b8 = a_s[pl.ds(off, SUBLANES), :], u_s[pl.ds(off, SUBLANES), :]
                for d in (1, 2, 4):
                    a_sh = jnp.where(row < d, 1.0, pltpu.roll(a8, d, 0))
                    b_sh = jnp.where(row < d, 0.0, pltpu.roll(b8, d, 0))
                    b8 = a8 * b_sh + b8
                    a8 = a8 * a_sh
                ab.append((a8, b8))
            enters = []
            for a8, b8 in ab:
                enters.append(hprev)
                hprev = a8[last:, :] * hprev + b8[last:, :]
            for off, (a8, b8), h0 in zip(offs, ab, enters):
                h_ref[pl.ds(off, SUBLANES), :] = a8 * h0 + b8
            return hprev

        lax.fori_loop(0, s // (SUBLANES * SCAN_GROUP), group, jnp.zeros((1, HEAD), F32))
        for c in range(s // tc):
            sl = slice(c * tc, (c + 1) * tc)
            gel, _ = _gelu_parts(xg_ref[sl, :])
            o_ref[sl, :] = h_ref[sl, :] * gel

    return pl.pallas_call(
        _per_block(one_block, 11, (4, 6), bps), grid=(nb // bps,),
        in_specs=[seq0, seq1, taps, vec, mat, vec, mat, vec, vec],
        out_specs=(seq0, seq0),
        out_shape=(SDS((s, nb * HEAD), F32), SDS((s, nb * HEAD), F32)),
        scratch_shapes=[pltpu.VMEM((s + pad, HEAD), F32), pltpu.VMEM((s, HEAD), F32), pltpu.VMEM((s, HEAD), F32)],
        compiler_params=_params(("parallel",)), name="lru_fwd")(xrg, xrg, wconv, bconv, wa, ba, wx, bx, lam)


def _lru_bwd(xrg, dol, hseq, wconv, bconv, wa, ba, wx, bx, lam):
    s = xrg.shape[0]
    nb = wa.shape[0]
    tc = _lru_chunk(s)
    bps, seq0, seq1, taps, vec, mat = _lru_specs(s, nb)
    pad = SUBLANES

    def one_block(xr_ref, xg_ref, do_ref, h_ref, wc_ref, bc_ref, wa_ref, ba_ref, wx_ref, bx_ref, lam_ref,
             dxr_ref, dxg_ref, dwc_ref, dbc_ref, dwa_ref, dba_ref, dwx_ref, dbx_ref, dlam_ref,
             pad_s, hp_s, a_s, g_s, da_s, dxc_s, xc_s, r_s, ig_s, mult_s):
        pad_s[0:pad, :] = jnp.zeros((pad, HEAD), F32)
        pad_s[pad:pad + s, :] = xr_ref[...]
        hp_s[0:pad, :] = jnp.zeros((pad, HEAD), F32)
        hp_s[pad:pad + s, :] = h_ref[...]
        a_s[s:s + pad, :] = jnp.zeros((pad, HEAD), F32)
        dxc_s[s:s + pad, :] = jnp.zeros((pad, HEAD), F32)
        wab, wxb = wa_ref[...].astype(BF16), wx_ref[...].astype(BF16)
        lam_v = lam_ref[...]
        sp = _softplus_neg(lam_v)

        def conv_in(c):
            base = c * tc
            wins = [pad_s[pl.ds(base + pad - (CONV_TAPS - 1) + i, tc), :] for i in range(CONV_TAPS)]
            xc = bc_ref[...] + sum(wc_ref[i:i + 1, :] * wins[i] for i in range(CONV_TAPS))
            return xc, wins

        for c in range(s // tc):
            sl = slice(c * tc, (c + 1) * tc)
            xc, _ = conv_in(c)
            r, ig, a, mult = _lru_gates(xc, wab, ba_ref[...], wxb, bx_ref[...], sp)
            a_s[sl, :] = a
            xc_s[sl, :], r_s[sl, :], ig_s[sl, :], mult_s[sl, :] = xc, r, ig, mult
            gel, dgel = _gelu_parts(xg_ref[sl, :])
            dov = do_ref[sl, :]
            g_s[sl, :] = dov * gel
            dxg_ref[sl, :] = (dov * h_ref[sl, :] * dgel).astype(BF16)

        row = lax.broadcasted_iota(jnp.int32, (SUBLANES, HEAD), 0)
        n_chunks = s // SUBLANES

        def group(it, gnext):
            offs = [pl.multiple_of((n_chunks - 1 - (it * SCAN_GROUP + q)) * SUBLANES, SUBLANES) for q in range(SCAN_GROUP)]
            cg = []
            for off in offs:
                a8 = a_s[pl.ds(off, SUBLANES), :]
                a8n = a_s[pl.ds(off + SUBLANES, SUBLANES), :]
                c8 = pltpu.roll(jnp.where(row == 0, a8n, a8), SUBLANES - 1, 0)
                g8 = g_s[pl.ds(off, SUBLANES), :]
                for d in (1, 2, 4):
                    c_sh = jnp.where(row >= SUBLANES - d, 1.0, pltpu.roll(c8, SUBLANES - d, 0))
                    g_sh = jnp.where(row >= SUBLANES - d, 0.0, pltpu.roll(g8, SUBLANES - d, 0))
                    g8 = c8 * g_sh + g8
                    c8 = c8 * c_sh
                cg.append((c8, g8))
            enters = []
            for c8, g8 in cg:
                enters.append(gnext)
                gnext = g8[0:1, :] + c8[0:1, :] * gnext
            for off, (c8, g8), g0 in zip(offs, cg, enters):
                gv = g8 + c8 * g0
                g_s[pl.ds(off, SUBLANES), :] = gv
                h8 = hp_s[pl.ds(off + pad, SUBLANES), :]
                h8p = hp_s[pl.ds(off, SUBLANES), :]
                da_s[pl.ds(off, SUBLANES), :] = gv * pltpu.roll(jnp.where(row == SUBLANES - 1, h8p, h8), 1, 0)
            return gnext

        lax.fori_loop(0, n_chunks // SCAN_GROUP, group, jnp.zeros((1, HEAD), F32))

        dsp = jnp.zeros((1, HEAD), F32)
        dbc = jnp.zeros((1, HEAD), F32)
        dba = jnp.zeros((1, HEAD), F32)
        dbx = jnp.zeros((1, HEAD), F32)
        dwa = jnp.zeros((HEAD, HEAD), F32)
        dwx = jnp.zeros((HEAD, HEAD), F32)
        dwc = [jnp.zeros((1, HEAD), F32) for _ in range(CONV_TAPS)]
        for c in range(s // tc):
            sl = slice(c * tc, (c + 1) * tc)
            wins = [pad_s[pl.ds(c * tc + pad - (CONV_TAPS - 1) + i, tc), :] for i in range(CONV_TAPS)]
            xc, r, ig, a, mult = xc_s[sl, :], r_s[sl, :], ig_s[sl, :], a_s[sl, :], mult_s[sl, :]
            du, da = g_s[sl, :], da_s[sl, :]
            d_ix = du * mult
            dla = da * a - (du * ig * xc) * (a * a / mult)
            dsp = dsp + jnp.sum(dla * r, axis=0, keepdims=True) * (-LRU_C)
            dpa = (dla * (-LRU_C * sp)) * r * (1.0 - r)
            dpx = (d_ix * xc) * ig * (1.0 - ig)
            dpab, dpxb, xb = dpa.astype(BF16), dpx.astype(BF16), xc.astype(BF16)
            dxc = d_ix * ig + _dot_nt(dpab, wab) + _dot_nt(dpxb, wxb)
            dwa = dwa + _dot_tn(xb, dpab)
            dwx = dwx + _dot_tn(xb, dpxb)
            dba = dba + jnp.sum(dpa, axis=0, keepdims=True)
            dbx = dbx + jnp.sum(dpx, axis=0, keepdims=True)
            dbc = dbc + jnp.sum(dxc, axis=0, keepdims=True)
            for i in range(CONV_TAPS):
                dwc[i] = dwc[i] + jnp.sum(dxc * wins[i], axis=0, keepdims=True)
            dxc_s[sl, :] = dxc

        for c in range(s // tc):
            base = c * tc
            dxr = sum(wc_ref[i:i + 1, :] * dxc_s[pl.ds(base + (CONV_TAPS - 1) - i, tc), :] for i in range(CONV_TAPS))
            dxr_ref[base:base + tc, :] = dxr.astype(BF16)

        for i in range(CONV_TAPS):
            dwc_ref[i:i + 1, :] = dwc[i]
        dbc_ref[...] = dbc
        dwa_ref[...] = dwa
        dwx_ref[...] = dwx
        dba_ref[...] = dba
        dbx_ref[...] = dbx
        dlam_ref[...] = dsp * (-_sigmoid(-lam_v))

    w = nb * HEAD
    return pl.pallas_call(
        _per_block(one_block, 20, (6, 8, 15, 17), bps), grid=(nb // bps,),
        in_specs=[seq0, seq1, seq0, seq0, taps, vec, mat, vec, mat, vec, vec],
        out_specs=(seq0, seq0, taps, vec, mat, vec, mat, vec, vec),
        out_shape=(SDS((s, w), BF16), SDS((s, w), BF16), SDS((CONV_TAPS, w), F32), SDS((1, w), F32),
                   SDS((nb, HEAD, HEAD), F32), SDS((1, w), F32), SDS((nb, HEAD, HEAD), F32), SDS((1, w), F32),
                   SDS((1, w), F32)),
        scratch_shapes=[pltpu.VMEM((s + pad, HEAD), F32), pltpu.VMEM((s + pad, HEAD), F32),
                        pltpu.VMEM((s + pad, HEAD), F32), pltpu.VMEM((s, HEAD), F32),
                        pltpu.VMEM((s, HEAD), F32), pltpu.VMEM((s + pad, HEAD), F32)]
                       + [pltpu.VMEM((s, HEAD), F32)] * 4,
        compiler_params=_params(("parallel",)), name="lru_bwd",
    )(xrg, xrg, dol, hseq, wconv, bconv, wa, ba, wx, bx, lam)


def _ada_mod(c_all, w_sh, b_sh):
    n_ex, d = c_all.shape
    n = w_sh.shape[1]
    tn = _tile(n, 512)

    def body(c_ref, w_ref, b_ref, mod_ref, act_ref):
        cv = c_ref[...]
        act = cv * _sigmoid(cv)
        act_ref[...] = act
        mod_ref[...] = _dot(act.astype(BF16), w_ref[...].astype(BF16)) + b_ref[...]

    return pl.pallas_call(
        body, grid=(n // tn,),
        in_specs=[pl.BlockSpec((n_ex, d), lambda j: (0, 0)), pl.BlockSpec((d, tn), lambda j: (0, j)),
                  pl.BlockSpec((1, tn), lambda j: (0, j))],
        out_specs=(pl.BlockSpec((n_ex, tn), lambda j: (0, j)), pl.BlockSpec((n_ex, d), lambda j: (0, 0))),
        out_shape=(SDS((n_ex, n), F32), SDS((n_ex, d), F32)),
        compiler_params=_params(("arbitrary",)), name="ada_mod")(c_all, w_sh, b_sh)


def _adamw_math(w, g, m, v):
    m = ADAM_B1 * m + (1.0 - ADAM_B1) * g
    v = ADAM_B2 * v + (1.0 - ADAM_B2) * (g * g)
    m_hat = m / (1.0 - ADAM_B1 ** ADAM_STEP)
    v_hat = v / (1.0 - ADAM_B2 ** ADAM_STEP)
    delta = -ADAM_LR * (m_hat / (jnp.sqrt(v_hat) + ADAM_EPS) + ADAM_WD * w)
    return delta, m, v


def _adamw_plain(name, w, g, m, v):
    def body(w_ref, g_ref, m_ref, v_ref, d_ref, mo_ref, vo_ref):
        d_ref[...], mo_ref[...], vo_ref[...] = _adamw_math(w_ref[...], g_ref[...], m_ref[...], v_ref[...])

    return pl.pallas_call(body, out_shape=(SDS(w.shape, F32),) * 3, name=name)(w, g, m, v)


def _adamw_halves(name, c_arr, w, m, v, g_own, g_recv):
    r, n = w.shape
    rh = r // 2
    tr = _tile(rh, 256)
    nh = rh // tr

    def body(c_ref, w_ref, m_ref, v_ref, go_ref, gr_ref, g_ref, d_ref, mo_ref, vo_ref):
        own = (pl.program_id(0) // nh) == c_ref[0]
        g = jnp.where(own, go_ref[...], gr_ref[...])
        g_ref[...] = g
        d_ref[...], mo_ref[...], vo_ref[...] = _adamw_math(w_ref[...], g, m_ref[...], v_ref[...])

    full = pl.BlockSpec((tr, n), lambda i, c: (i, 0))
    own = pl.BlockSpec((tr, n), lambda i, c: (jnp.where(i // nh == c[0], i % nh, 0), 0))
    recv = pl.BlockSpec((tr, n), lambda i, c: (jnp.where(i // nh == c[0], 0, i % nh), 0))
    return pl.pallas_call(
        body,
        grid_spec=pltpu.PrefetchScalarGridSpec(
            num_scalar_prefetch=1, grid=(2 * nh,), in_specs=[full, full, full, own, recv],
            out_specs=(full,) * 4),
        out_shape=(SDS((r, n), F32),) * 4,
        compiler_params=_params(("parallel",)), name=name)(c_arr, w, m, v, g_own, g_recv)


def _adamw_ada(w, m, v, act_t, dmod):
    d, n = w.shape
    n_ex = act_t.shape[1]
    tr = _tile(d, 256)

    def body(a_ref, dm_ref, w_ref, m_ref, v_ref, g_ref, d_ref, mo_ref, vo_ref):
        g = _dot(a_ref[...], dm_ref[...])
        g_ref[...] = g
        d_ref[...], mo_ref[...], vo_ref[...] = _adamw_math(w_ref[...], g, m_ref[...], v_ref[...])

    full = pl.BlockSpec((tr, n), lambda i: (i, 0))
    return pl.pallas_call(
        body, grid=(d // tr,),
        in_specs=[pl.BlockSpec((tr, n_ex), lambda i: (i, 0)), pl.BlockSpec((n_ex, n), lambda i: (0, 0)), full, full, full],
        out_specs=(full,) * 4, out_shape=(SDS((d, n), F32),) * 4,
        compiler_params=_params(("parallel",)), name="adamw_ada")(act_t, dmod, w, m, v)


def _small_reduce_adamw(parts, w, m, v):
    n_dev, r, _ = parts.shape
    tr = r if r <= PACK_ROWS else PACK_ROWS

    def body(p_ref, w_ref, m_ref, v_ref, g_ref, d_ref, mo_ref, vo_ref):
        g = p_ref[0]
        for k in range(1, n_dev):
            g = g + p_ref[k]
        g_ref[...] = g
        d_ref[...], mo_ref[...], vo_ref[...] = _adamw_math(w_ref[...], g, m_ref[...], v_ref[...])

    full = pl.BlockSpec((tr, LANES), lambda i: (i, 0))
    return pl.pallas_call(
        body, grid=(r // tr,),
        in_specs=[pl.BlockSpec((n_dev, tr, LANES), lambda i: (0, i, 0)), full, full, full],
        out_specs=(full,) * 4, out_shape=(SDS((r, LANES), F32),) * 4,
        compiler_params=_params(("parallel",)), name="small_reduce_adamw")(parts, w, m, v)


def _mesh_pos():
    return lax.axis_index("x"), lax.axis_index("y"), lax.axis_index("c")


def _other_chips(x, y):
    return [(1 - x, y), (x, 1 - y), (1 - x, 1 - y)]


def _all_gather_small(name, blk, after=()):
    r, n = blk.shape
    n_after = len(after)

    def body(x_ref, *rest):
        out_ref, send_sems, recv_sems, local_sem = rest[n_after:]
        x, y, c = _mesh_pos()
        me, sibling = (x, y, c), (x, y, 1 - c)
        chips = _other_chips(x, y)

        def rows(px, py, pc):
            return out_ref.at[4 * px + 2 * py + pc]

        def copy(k, block, to, src=None):
            return pltpu.make_async_remote_copy(
                src_ref=rows(*block) if src is None else src, dst_ref=rows(*block),
                send_sem=send_sems.at[k], recv_sem=recv_sems.at[k], device_id=to, device_id_type=MESH)

        mine = pltpu.make_async_copy(x_ref, rows(*me), local_sem)
        mine.start()
        first = [copy(0, me, sibling, src=x_ref)]
        first += [copy(1 + j, me, (*chip, c), src=x_ref) for j, chip in enumerate(chips)]
        for cp in first:
            cp.start()
        passed = [copy(4 + j, (*chip, c), sibling) for j, chip in enumerate(chips)]
        for j, chip in enumerate(chips):
            copy(1 + j, (*chip, c), me).wait_recv()
            passed[j].start()
        copy(0, sibling, me).wait_recv()
        for j, chip in enumerate(chips):
            copy(4 + j, (*chip, 1 - c), me).wait_recv()
        for cp in first + passed:
            cp.wait_send()
        mine.wait()

    return pl.pallas_call(
        body, out_shape=SDS((N_DEV, r, n), blk.dtype),
        in_specs=[pl.BlockSpec(memory_space=pltpu.VMEM)] + [pl.BlockSpec(memory_space=pl.ANY)] * n_after,
        out_specs=pl.BlockSpec(memory_space=pltpu.VMEM),
        scratch_shapes=[pltpu.SemaphoreType.DMA((7,)), pltpu.SemaphoreType.DMA((7,)), pltpu.SemaphoreType.DMA],
        compiler_params=pltpu.CompilerParams(vmem_limit_bytes=VMEM_LIMIT), name=name)(blk, *after)


_ANY = pl.BlockSpec(memory_space=pl.ANY)
_HBM = pl.BlockSpec(memory_space=pltpu.HBM)
_SEM = pl.BlockSpec(memory_space=pltpu.SEMAPHORE)
_EFFECT = pltpu.SideEffectType.DATAFLOW_SIDE_EFFECTING


def _hbm(a):
    return pltpu.with_memory_space_constraint(a, pltpu.HBM)


def _place_cast(name, j_arr, shard, kind, after):
    r, n = shard.shape
    tr = _tile(r, 256)
    nr = r // tr
    if kind == "col":
        out_shape, o_spec = (r, N_CHIP * n), pl.BlockSpec((tr, n), lambda i, j: (i, j[0]))
    else:
        out_shape, o_spec = (N_CHIP * r, n), pl.BlockSpec((tr, n), lambda i, j: (j[0] * nr + i, 0))

    def body(j_ref, s_ref, after_ref, o_ref, tok_ref):
        o_ref[...] = s_ref[...].astype(BF16)
        tok_ref[...] = jnp.zeros_like(tok_ref)

    return pl.pallas_call(
        body,
        grid_spec=pltpu.PrefetchScalarGridSpec(
            num_scalar_prefetch=1, grid=(nr,), in_specs=[pl.BlockSpec((tr, n), lambda i, j: (i, 0)), _ANY],
            out_specs=(o_spec, pl.BlockSpec((SUBLANES, LANES), lambda i, j: (0, 0)))),
        out_shape=(SDS(out_shape, BF16), SDS((SUBLANES, LANES), F32)),
        compiler_params=_params(("arbitrary",)), name=name)(j_arr, shard, after)


def _leg_direct(full, kind, x, y, c):
    mine = _full_region(full, kind, x, y, c)
    return [(mine, mine, (1 - x, y, c)), (mine, mine, (x, 1 - y, c))]


def _leg_relay(full, kind, x, y, c):
    fx, fy = jnp.where(c == 0, 1 - x, x), jnp.where(c == 0, y, 1 - y)
    tx, ty = jnp.where(c == 0, x, 1 - x), jnp.where(c == 0, 1 - y, y)
    got = _full_region(full, kind, fx, fy, c)
    return [(got, got, (tx, ty, c))]


def _leg_d2d(which):
    def leg(full, kind, x, y, c):
        chips = _other_chips(x, y)
        return [(_full_region(full, kind, *chips[k], c), _full_region(full, kind, *chips[k], c), (x, y, 1 - c))
                for k in which]
    return leg


_LEGS = {"direct": (_leg_direct, 2), "relay": (_leg_relay, 1), "d2d_near": (_leg_d2d((0, 1)), 2),
         "d2d_far": (_leg_d2d((2,)), 1)}


def _gather_call(name, fulls, kinds, waits, starts, after, thru):
    nw, n_wait, n_start = len(fulls), len(waits), len(starts)

    def body(*refs):
        wait_sems = refs[nw:nw + 2 * n_wait]
        outs = refs[nw + 2 * n_wait + 2:]
        full, start_sems = outs[:nw], outs[nw:nw + 2 * n_start]
        x, y, c = _mesh_pos()
        for i, (leg, ws, _, _) in enumerate(waits):
            fn, per = _LEGS[leg]
            for li, w in enumerate(ws):
                for k, (s_, d_, dev) in enumerate(fn(full[w], kinds[w], x, y, c)):
                    cp = pltpu.make_async_remote_copy(
                        src_ref=s_, dst_ref=d_, send_sem=wait_sems[2 * i].at[per * li + k],
                        recv_sem=wait_sems[2 * i + 1].at[per * li + k], device_id=dev, device_id_type=MESH)
                    cp.wait_recv()
                    cp.wait_send()
        for i, (leg, ws) in enumerate(starts):
            fn, per = _LEGS[leg]
            for li, w in enumerate(ws):
                for k, (s_, d_, dev) in enumerate(fn(full[w], kinds[w], x, y, c)):
                    pltpu.make_async_remote_copy(
                        src_ref=s_, dst_ref=d_, send_sem=start_sems[2 * i].at[per * li + k],
                        recv_sem=start_sems[2 * i + 1].at[per * li + k], device_id=dev, device_id_type=MESH).start()

    sems = []
    for leg, ws in starts:
        sems += [pltpu.SemaphoreType.DMA((_LEGS[leg][1] * len(ws),))] * 2
    wait_args = []
    for _, _, s_, r_ in waits:
        wait_args += [s_, r_]
    outs = pl.pallas_call(
        body,
        out_shape=tuple(pltpu.HBM(f_.shape, f_.dtype) for f_ in fulls) + tuple(sems) + (SDS(thru.shape, thru.dtype),),
        in_specs=[_HBM] * nw + [_SEM] * (2 * n_wait) + [_ANY, _ANY],
        out_specs=tuple([_HBM] * nw + [_SEM] * (2 * n_start) + [_ANY]),
        input_output_aliases={**{w: w for w in range(nw)}, nw + 2 * n_wait + 1: nw + 2 * n_start},
        compiler_params=pltpu.CompilerParams(has_side_effects=_EFFECT),
        name=name,
    )(*[_hbm(f_) for f_ in fulls], *wait_args, after, thru)
    pairs = [(outs[nw + 2 * i], outs[nw + 2 * i + 1]) for i in range(n_start)]
    return list(outs[:nw]), pairs, outs[nw + 2 * n_start]


def _full_region(full, kind, px, py, half):
    j = 2 * px + py
    if kind == "col":
        rh, cols = full.shape[0] // 2, full.shape[1] // N_CHIP
        return full.at[pl.ds(half * rh, rh), pl.ds(j * cols, cols)]
    rows = full.shape[0] // N_CHIP
    rh = rows // 2
    return full.at[pl.ds(j * rows + half * rh, rh), :]


def _plan_scatter(kind):
    def plan(src, land, x, y, c):
        out = []
        for k, (px, py) in enumerate(_other_chips(x, y)):
            j = 2 * px + py
            if kind == "col":
                n = src.shape[1] // N_CHIP
                blk = src.at[:, pl.ds(j * n, n)]
            else:
                blk = src.at[j]
            out.append((blk, land.at[k], (px, py, c)))
        return out
    return plan


def _plan_whole(src, land, x, y, c):
    return [(src, land, (x, y, 1 - c))]


def _split_start(name, src, land_shape, n, plan, thru):
    def body(src_in, land_in, thru_in, send, recv, src_ref, land_ref, thru_out):
        x, y, c = _mesh_pos()
        for k, (s_, d_, dev) in enumerate(plan(src_ref, land_ref, x, y, c)):
            pltpu.make_async_remote_copy(src_ref=s_, dst_ref=d_, send_sem=send.at[k], recv_sem=recv.at[k],
                                         device_id=dev, device_id_type=MESH).start()

    sem = pltpu.SemaphoreType.DMA((n,))
    return pl.pallas_call(
        body,
        out_shape=(sem, sem, pltpu.HBM(src.shape, src.dtype), pltpu.HBM(land_shape, src.dtype), SDS(thru.shape, thru.dtype)),
        in_specs=[_HBM, _HBM, _ANY], out_specs=(_SEM, _SEM, _HBM, _HBM, _ANY),
        input_output_aliases={0: 2, 1: 3, 2: 4},
        compiler_params=pltpu.CompilerParams(has_side_effects=_EFFECT), name=name,
    )(_hbm(src), _hbm(lax.empty(land_shape, src.dtype)), thru)


def _split_start_pair(name, first, second, thru):
    (src_a, shape_a, n_a, plan_a), (src_b, shape_b, n_b, plan_b) = first, second

    def body(sa_in, la_in, sb_in, lb_in, thru_in, send_a, recv_a, send_b, recv_b, sa, la, sb, lb, thru_out):
        x, y, c = _mesh_pos()
        for plan, s_ref, l_ref, send, recv in ((plan_a, sa, la, send_a, recv_a), (plan_b, sb, lb, send_b, recv_b)):
            for k, (s_, d_, dev) in enumerate(plan(s_ref, l_ref, x, y, c)):
                pltpu.make_async_remote_copy(src_ref=s_, dst_ref=d_, send_sem=send.at[k], recv_sem=recv.at[k],
                                             device_id=dev, device_id_type=MESH).start()

    sem_a, sem_b = pltpu.SemaphoreType.DMA((n_a,)), pltpu.SemaphoreType.DMA((n_b,))
    outs = pl.pallas_call(
        body,
        out_shape=(sem_a, sem_a, sem_b, sem_b, pltpu.HBM(src_a.shape, src_a.dtype), pltpu.HBM(shape_a, src_a.dtype),
                   pltpu.HBM(src_b.shape, src_b.dtype), pltpu.HBM(shape_b, src_b.dtype), SDS(thru.shape, thru.dtype)),
        in_specs=[_HBM] * 4 + [_ANY], out_specs=tuple([_SEM] * 4 + [_HBM] * 4 + [_ANY]),
        input_output_aliases={0: 4, 1: 5, 2: 6, 3: 7, 4: 8},
        compiler_params=pltpu.CompilerParams(has_side_effects=_EFFECT), name=name,
    )(_hbm(src_a), _hbm(lax.empty(shape_a, src_a.dtype)), _hbm(src_b), _hbm(lax.empty(shape_b, src_b.dtype)), thru)
    return (outs[0], outs[1], outs[4], outs[5]), (outs[2], outs[3], outs[6], outs[7]), outs[8]


def _split_wait(name, send, recv, src, land, plan, after):
    def body(src_in, land_in, send_r, recv_r, after_r, src_ref, land_ref):
        x, y, c = _mesh_pos()
        for k, (s_, d_, dev) in enumerate(plan(src_ref, land_ref, x, y, c)):
            cp = pltpu.make_async_remote_copy(src_ref=s_, dst_ref=d_, send_sem=send_r.at[k], recv_sem=recv_r.at[k],
                                              device_id=dev, device_id_type=MESH)
            cp.wait_send()
            cp.wait_recv()

    return pl.pallas_call(
        body,
        out_shape=(pltpu.HBM(src.shape, src.dtype), pltpu.HBM(land.shape, land.dtype)),
        in_specs=[_HBM, _HBM, _SEM, _SEM, _ANY], out_specs=(_HBM, _HBM),
        input_output_aliases={0: 0, 1: 1},
        compiler_params=pltpu.CompilerParams(has_side_effects=_EFFECT), name=name,
    )(src, land, send, recv, after)


def _dev_row(buf, px, py, pc):
    return buf.at[4 * px + 2 * py + pc]


def _plan_gather_own(buf, land, x, y, c):
    own = _dev_row(buf, x, y, c)
    return [(own, own, (x, y, 1 - c))] + [(own, own, (px, py, c)) for px, py in _other_chips(x, y)]


def _plan_gather_pass(buf, land, x, y, c):
    return [(_dev_row(buf, px, py, c), _dev_row(buf, px, py, c), (x, y, 1 - c)) for px, py in _other_chips(x, y)]


def _split_start_inplace(name, buf, n, plan, thru):
    def body(buf_in, thru_in, send, recv, buf_ref, thru_out):
        x, y, c = _mesh_pos()
        for k, (s_, d_, dev) in enumerate(plan(buf_ref, buf_ref, x, y, c)):
            pltpu.make_async_remote_copy(src_ref=s_, dst_ref=d_, send_sem=send.at[k], recv_sem=recv.at[k],
                                         device_id=dev, device_id_type=MESH).start()

    sem = pltpu.SemaphoreType.DMA((n,))
    return pl.pallas_call(
        body, out_shape=(sem, sem, pltpu.HBM(buf.shape, buf.dtype), SDS(thru.shape, thru.dtype)),
        in_specs=[_HBM, _ANY], out_specs=(_SEM, _SEM, _HBM, _ANY), input_output_aliases={0: 2, 1: 3},
        compiler_params=pltpu.CompilerParams(has_side_effects=_EFFECT), name=name)(_hbm(buf), thru)


def _split_wait_inplace(name, send, recv, buf, plan, after):
    def body(buf_in, send_r, recv_r, after_r, buf_ref):
        x, y, c = _mesh_pos()
        for k, (s_, d_, dev) in enumerate(plan(buf_ref, buf_ref, x, y, c)):
            cp = pltpu.make_async_remote_copy(src_ref=s_, dst_ref=d_, send_sem=send_r.at[k], recv_sem=recv_r.at[k],
                                              device_id=dev, device_id_type=MESH)
            cp.wait_send()
            cp.wait_recv()

    return pl.pallas_call(
        body, out_shape=pltpu.HBM(buf.shape, buf.dtype), in_specs=[_HBM, _SEM, _SEM, _ANY], out_specs=_HBM,
        input_output_aliases={0: 0}, compiler_params=pltpu.CompilerParams(has_side_effects=_EFFECT), name=name,
    )(buf, send, recv, after)


def _place_row(name, me_arr, slab):
    r, n = slab.shape
    tr = r if r <= PACK_ROWS else PACK_ROWS

    def body(me_ref, s_ref, o_ref):
        o_ref[...] = s_ref[...]

    return pl.pallas_call(
        body,
        grid_spec=pltpu.PrefetchScalarGridSpec(
            num_scalar_prefetch=1, grid=(r // tr,), in_specs=[pl.BlockSpec((tr, n), lambda i, me: (i, 0))],
            out_specs=pl.BlockSpec((None, tr, n), lambda i, me: (me[0], i, 0))),
        out_shape=SDS((N_DEV, r, n), slab.dtype), compiler_params=_params(("parallel",)), name=name)(me_arr, slab)


def _sum_partials(name, j_arr, part, got, kind):
    _, rh, n = got.shape
    tr = _tile(rh, 256)
    if kind == "col":
        p_spec = pl.BlockSpec((tr, n), lambda i, j: (i, j[0]))
    else:
        p_spec = pl.BlockSpec((None, tr, n), lambda i, j: (j[0], i, 0))

    def body(j_ref, p_ref, r_ref, o_ref):
        o_ref[...] = ((p_ref[...].astype(F32) + r_ref[0].astype(F32)) + r_ref[1].astype(F32)) + r_ref[2].astype(F32)

    return pl.pallas_call(
        body,
        grid_spec=pltpu.PrefetchScalarGridSpec(
            num_scalar_prefetch=1, grid=(rh // tr,),
            in_specs=[p_spec, pl.BlockSpec((3, tr, n), lambda i, j: (0, i, 0))],
            out_specs=pl.BlockSpec((tr, n), lambda i, j: (i, 0))),
        out_shape=SDS((rh, n), F32),
        compiler_params=_params(("parallel",)), name=name)(j_arr, part, got)


def _pack(arrays):
    flat = [a.reshape(-1).astype(F32) for a in arrays]
    flat = [jnp.pad(f, (0, (-f.shape[0]) % LANES)) for f in flat]
    sizes = [f.shape[0] for f in flat]
    total = sum(sizes)
    rows = total // LANES
    tail = LANES * ((-rows) % (PACK_ROWS if rows > PACK_ROWS else SUBLANES))
    if tail:
        flat.append(jnp.zeros((tail,), F32))
    return jnp.concatenate(flat).reshape(-1, LANES), sizes


def _unpack(slab, sizes, shapes, lead=()):
    flat = slab.reshape(lead + (-1,))
    out, off = [], 0
    for sz, shp in zip(sizes, shapes):
        n = math.prod(shp)
        out.append(flat[..., off:off + n].reshape(lead + tuple(shp)))
        off += sz
    return out


def kernel(x, c, w_ada, b_ada, g_norm_mix, w_in, w_conv, b_conv, w_rg_a, b_rg_a, w_rg_x, b_rg_x, lru_lambda, g_attn_out, g_lru_out, w_out, g_norm_mlp, w_mlp_in, w_mlp_out, g_norm_final, loss_target, m_w_ada, m_b_ada, m_g_norm_mix, m_w_in, m_w_conv, m_b_conv, m_w_rg_a, m_b_rg_a, m_w_rg_x, m_b_rg_x, m_lru_lambda, m_g_attn_out, m_g_lru_out, m_w_out, m_g_norm_mlp, m_w_mlp_in, m_w_mlp_out, m_g_norm_final, v_w_ada, v_b_ada, v_g_norm_mix, v_w_in, v_w_conv, v_b_conv, v_w_rg_a, v_b_rg_a, v_w_rg_x, v_b_rg_x, v_lru_lambda, v_g_attn_out, v_g_lru_out, v_w_out, v_g_norm_mlp, v_w_mlp_in, v_w_mlp_out, v_g_norm_final):
    s, d = x.shape[1], x.shape[2]
    aw = d // 2
    nh = aw // HEAD
    f = w_mlp_out.shape[1] * N_CHIP
    n_ada = w_ada.shape[2]
    n_cv = w_conv.shape[2]
    ix, iy, ic = lax.axis_index("x"), lax.axis_index("y"), lax.axis_index("c")
    chip = 2 * ix + iy
    me = 2 * chip + ic
    c_arr = jnp.reshape(ic, (1,)).astype(jnp.int32)
    j_arr = jnp.reshape(chip, (1,)).astype(jnp.int32)

    x2d, tgt = x[0], loss_target[0]

    k_in, k_out, k_mi, k_mo = kinds = ("col", "row", "col", "row")
    me_arr = jnp.reshape(me, (1,)).astype(jnp.int32)
    slab, sizes = _pack([c, w_conv])
    c_buf = _place_row("place_cond", me_arr, slab)
    cg_send, cg_recv, c_buf, tok = _split_start_inplace("cond_gather_start", c_buf, N_CHIP, _plan_gather_own, slab)

    p_in, tok = _place_cast("place_cast_0", j_arr, w_in[0], k_in, tok)
    (f_in,), (dir_in,), tok = _gather_call("gather_0", [p_in], [k_in], [], [("direct", [0])], c, tok)
    c_buf = _split_wait_inplace("cond_gather_wait", cg_send, cg_recv, c_buf, _plan_gather_own, tok)
    cp_send, cp_recv, c_buf, tok = _split_start_inplace("cond_pass_start", c_buf, N_CHIP - 1, _plan_gather_pass, tok)
    gathered = _split_wait_inplace("cond_pass_wait", cp_send, cp_recv, c_buf, _plan_gather_pass, tok)
    c_parts, cv_parts = _unpack(gathered, sizes, [(d,), (CONV_TAPS, n_cv)], lead=(N_DEV,))
    c_all = c_parts
    w_conv_full = jnp.concatenate([cv_parts[2 * j] for j in range(N_CHIP)], axis=-1)
    b_sh = lax.dynamic_slice(b_ada, (0, chip * n_ada), (1, n_ada))
    mod_part, act_all = _ada_mod(c_all, w_ada[0], b_sh)
    p_out, tok = _place_cast("place_cast_1", j_arr, w_out[0], k_out, mod_part)
    p_mi, tok = _place_cast("place_cast_2", j_arr, w_mlp_in[0], k_mi, tok)
    mod_g = _all_gather_small("comm_gather_mod", mod_part.reshape(-1, LANES), after=(tok,))
    mod_g = mod_g.reshape(N_DEV, N_DEV, n_ada)
    mod = jnp.concatenate([lax.dynamic_index_in_dim(mod_g[2 * j], me, 0, keepdims=True) for j in range(N_CHIP)], axis=-1)
    sh1, sc1, gt1, sh2, sc2, gt2 = [mod[:, k * d:(k + 1) * d] for k in range(N_MOD)]
    (f_in,), (rel_in, near_in), sh1 = _gather_call(
        "gather_1", [f_in], [k_in], [("direct", [0], *dir_in)], [("relay", [0]), ("d2d_near", [0])], mod, sh1)

    h1, rstd1 = _norm_mod_fwd("norm_mod_fwd1", x2d, g_norm_mix, sc1, sh1)
    p_mo, tok = _place_cast("place_cast_3", j_arr, w_mlp_out[0], k_mo, rstd1)
    (f_in, f_out, f_mi), (far_in, dir_out, dir_mi), h1 = _gather_call(
        "gather_2", [f_in, p_out, p_mi], kinds[:3], [("relay", [0], *rel_in)],
        [("d2d_far", [0]), ("direct", [1]), ("direct", [2])], tok, h1)
    (w_in_f,), _, h1 = _gather_call("gather_3", [f_in], [k_in],
                                    [("d2d_near", [0], *near_in), ("d2d_far", [0], *far_in)], [], rstd1, h1)
    (qkv,) = _matmul("mm_qkv", h1, w_in_f, "nn", s, 3 * aw, d, (BF16,))
    (xrg,) = _matmul("mm_xrg", h1, w_in_f, "nn", s, 2 * aw, d, (F32,), b_off=3 * aw)
    (f_out, f_mo), (rel_out, near_out, dir_mo), qkv = _gather_call(
        "gather_3b", [f_out, p_mo], [k_out, k_mo], [("direct", [0], *dir_out)],
        [("relay", [0]), ("d2d_near", [0]), ("direct", [1])], xrg, qkv)
    o_attn, attn_w, attn_sg = _attn_fwd(qkv, nh)
    (f_out, f_mi), (far_out, rel_mi, near_mi), xrg = _gather_call(
        "gather_4", [f_out, f_mi], [k_out, k_mi], [("relay", [0], *rel_out), ("direct", [1], *dir_mi)],
        [("d2d_far", [0]), ("relay", [1]), ("d2d_near", [1])], o_attn, xrg)
    wa3, wx3 = w_rg_a[0], w_rg_x[0]
    o_lru, hseq = _lru_fwd(xrg, w_conv_full, b_conv, wa3, b_rg_a, wx3, b_rg_x, lru_lambda)
    mixed, rstd_a, rstd_l = _mix_norm_fwd(o_attn, o_lru, g_attn_out, g_lru_out)
    (f_mo,), (rel_mo, near_mo), mixed = _gather_call(
        "gather_5", [f_mo], [k_mo], [("direct", [0], *dir_mo)], [("relay", [0]), ("d2d_near", [0])], rstd_a, mixed)
    (w_out_f,), _, mixed = _gather_call(
        "gather_6", [f_out], [k_out], [("d2d_near", [0], *near_out), ("d2d_far", [0], *far_out)], [], rstd_l, mixed)

    def residual(acc, xin, gt):
        return acc, xin + gt * acc

    y1, x1 = _matmul("mm_out", mixed, w_out_f, "nn", s, d, d, (BF16, F32), extras=(x2d, gt1),
                     extra_kinds=("tile", "row"), epilogue=residual)
    (f_mi,), (far_mi,), x1 = _gather_call("gather_6b", [f_mi], [k_mi], [("relay", [0], *rel_mi)],
                                          [("d2d_far", [0])], y1, x1)
    h2, rstd2 = _norm_mod_fwd("norm_mod_fwd2", x1, g_norm_mlp, sc2, sh2)
    (w_mi_f, f_mo), (far_mo,), h2 = _gather_call(
        "gather_7", [f_mi, f_mo], [k_mi, k_mo],
        [("d2d_near", [0], *near_mi), ("d2d_far", [0], *far_mi), ("relay", [1], *rel_mo)], [("d2d_far", [1])], rstd2, h2)

    def sq_relu(acc):
        r = jnp.maximum(acc, 0.0)
        return 2.0 * r, r * r

    r2, hid = _matmul("mm_mlp_in", h2, w_mi_f, "nn", s, f, d, (BF16, BF16), epilogue=sq_relu)
    (w_mo_f,), _, hid = _gather_call("gather_8", [f_mo], [k_mo],
                                     [("d2d_near", [0], *near_mo), ("d2d_far", [0], *far_mo)], [], r2, hid)
    y2, x2 = _matmul("mm_mlp_out", hid, w_mo_f, "nn", s, d, f, (BF16, F32), extras=(x1, gt2),
                     extra_kinds=("tile", "row"), epilogue=residual)
    dx2, loss_row, dg_final, dy2, dgt2 = _final_loss(x2, g_norm_final.reshape(1, d), tgt, y2, gt2)

    oc_arr = 1 - c_arr

    def dw_half(name, st, h_arr, got=None):
        add = {} if got is None else dict(extras=(got,), extra_kinds=("tile",),
                                          epilogue=lambda acc, g_: (acc + g_.astype(F32),))
        tn = st["n"] if st["tm"] * 4 <= MM_TILE_M else MM_TILE_N
        (out,) = _matmul(name, st["a"], st["dy"], "tn", st["m"], st["n"], s, (BF16,), tm=st["tm"], tn=tn,
                         m_half=h_arr, **add)
        return out

    def rs_begin(tag, kind, xa, dy, m, n, thru, prev=None):
        st = {"tag": tag, "kind": kind, "a": xa, "dy": dy, "m": m, "n": n,
              "tm": m // (2 * N_CHIP) if kind == "row" else m // 2}
        first = dw_half("mm_dw_%s_a" % tag, st, oc_arr)
        swap = (first, first.shape, 1, _plan_whole)
        if prev is None:
            send, recv, first, land, thru = _split_start("rs_swap_start_" + tag, *swap, thru)
            st["swap"] = (send, recv, first, land)
        else:
            prev["scatter"], st["swap"], thru = _split_start_pair(
                "rs_scatter_%s_swap_%s_start" % (prev["tag"], tag), prev.pop("to_scatter"), swap, thru)
        return st, thru

    def rs_mid(st, after, thru, last=False):
        tag, kind = st["tag"], st["kind"]
        _, got = _split_wait("rs_swap_wait_" + tag, *st["swap"], _plan_whole, after)
        part = dw_half("mm_dw_%s_b" % tag, st, c_arr, got)
        if kind == "row":
            part = part.reshape(N_CHIP, st["m"] // (2 * N_CHIP), st["n"])
        blk = (part.shape[0], part.shape[1] // N_CHIP) if kind == "col" else part.shape[1:]
        scatter = (part, (N_CHIP - 1,) + blk, N_CHIP - 1, _plan_scatter(kind))
        if last:
            send, recv, part, land, thru = _split_start("rs_scatter_start_" + tag, *scatter, thru)
            st["scatter"] = (send, recv, part, land)
        else:
            st["to_scatter"] = scatter
        return thru

    def rs_end(st, after):
        tag, kind = st["tag"], st["kind"]
        part, got = _split_wait("rs_scatter_wait_" + tag, *st["scatter"], _plan_scatter(kind), after)
        return _sum_partials("sum_partials_" + tag, j_arr, part, got, kind)

    (dpre,) = _matmul("mm_dhid", dy2, w_mo_f, "nt", s, f, d, (BF16,), extras=(r2,), extra_kinds=("tile",),
                      epilogue=lambda acc, r: (acc * r.astype(F32),))
    st_mo, dpre = rs_begin("mo", "row", hid, dy2, f, d, dpre)
    (dh2,) = _matmul("mm_dh2", dpre, w_mi_f, "nt", s, d, f, (BF16,))
    dh2 = rs_mid(st_mo, dh2, dh2)
    st_mi, dh2 = rs_begin("mi", "col", h2, dpre, d, f, dh2, prev=st_mo)
    dx1, dsh2, dsc2, dg_mlp, dy1, dgt1 = _norm_mod_bwd("norm_mod_bwd2", dh2, x1, rstd2, g_norm_mlp, sc2, dx2,
                                                       gate=(y1, gt1))
    (dmixed,) = _matmul("mm_dmixed", dy1, w_out_f, "nt", s, d, d, (BF16,))
    dmixed = rs_mid(st_mi, dmixed, dmixed)
    st_out, dmixed = rs_begin("out", "row", mixed, dy1, d, d, dmixed, prev=st_mi)
    do_attn, do_lru, dg_attn, dg_lru = _mix_norm_bwd(dmixed, o_attn, o_lru, rstd_a, rstd_l, g_attn_out, g_lru_out)
    dq, dk, dv = _attn_bwd(qkv, do_attn, attn_w, attn_sg, nh)
    do_lru = rs_mid(st_out, dq, do_lru)
    dxr, dxg, dwconv, dbconv, dwa, dba, dwx, dbx, dlam = _lru_bwd(
        xrg, do_lru, hseq, w_conv_full, b_conv, wa3, b_rg_a, wx3, b_rg_x, lru_lambda)
    dproj = jnp.concatenate([dq, dk, dv, dxr, dxg], axis=-1)
    st_in, dproj = rs_begin("in", "col", h1, dproj, d, 5 * aw, dproj, prev=st_out)
    st_in["dy"] = dproj
    (dh1,) = _matmul("mm_dh1", dproj, w_in_f, "nt", s, d, 5 * aw, (BF16,))
    dh1 = rs_mid(st_in, dh1, dh1, last=True)
    grad_x, dsh1, dsc1, dg_mix = _norm_mod_bwd("norm_mod_bwd1", dh1, x2d, rstd1, g_norm_mix, sc1, dx1)

    dmod = jnp.concatenate([dsh1, dsc1, dgt1, dsh2, dsc2, dgt2], axis=-1)
    small_names = ["b_ada", "g_norm_mix", "b_conv", "w_rg_a", "b_rg_a", "w_rg_x", "b_rg_x", "lru_lambda",
                   "g_attn_out", "g_lru_out", "g_norm_mlp", "g_norm_final"]
    small_g = [dmod, dg_mix, dbconv, dwa, dba, dwx, dbx, dlam, dg_attn, dg_lru, dg_mlp, dg_final]
    small_w = [b_ada, g_norm_mix, b_conv, w_rg_a, b_rg_a, w_rg_x, b_rg_x, lru_lambda, g_attn_out, g_lru_out, g_norm_mlp, g_norm_final]
    small_m = [m_b_ada, m_g_norm_mix, m_b_conv, m_w_rg_a, m_b_rg_a, m_w_rg_x, m_b_rg_x, m_lru_lambda, m_g_attn_out, m_g_lru_out, m_g_norm_mlp, m_g_norm_final]
    small_v = [v_b_ada, v_g_norm_mix, v_b_conv, v_w_rg_a, v_b_rg_a, v_w_rg_x, v_b_rg_x, v_lru_lambda, v_g_attn_out, v_g_lru_out, v_g_norm_mlp, v_g_norm_final]
    extra_zero = [jnp.zeros_like(dwconv), jnp.zeros((LANES,), F32)]
    g_slab, g_sizes = _pack(small_g + [dwconv, loss_row])
    w_slab, _ = _pack(small_w + extra_zero)
    m_slab, _ = _pack(small_m + extra_zero)
    v_slab, _ = _pack(small_v + extra_zero)
    g_buf = _place_row("place_small_grads", me_arr, g_slab)
    sg_send, sg_recv, g_buf, tok = _split_start_inplace("sg_gather_start", g_buf, N_CHIP, _plan_gather_own, loss_row)

    def reduced_begin(tag, half, tok_):
        send, recv, half, land, tok_ = _split_start("rs_reduced_start_" + tag, half, half.shape, 1, _plan_whole, tok_)
        return (send, recv, half, land), tok_

    def reduced_end(tag, st, after):
        return _split_wait("rs_reduced_wait_" + tag, *st, _plan_whole, after)

    sw_mo, tok = reduced_begin("mo", rs_end(st_mo, tok), tok)
    half_mi, half_out = rs_end(st_mi, tok), rs_end(st_out, tok)
    sw_mi, sw_out, tok = _split_start_pair("rs_reduced_start_mi_out", (half_mi, half_mi.shape, 1, _plan_whole),
                                           (half_out, half_out.shape, 1, _plan_whole), tok)
    half_mo, got_mo = reduced_end("mo", sw_mo, tok)
    big = {"w_mlp_out": _adamw_halves("adamw_w_mlp_out", c_arr, w_mlp_out[0], m_w_mlp_out[0], v_w_mlp_out[0],
                                      half_mo, got_mo)}
    half_mi, got_mi = reduced_end("mi", sw_mi, big["w_mlp_out"][1])
    big["w_mlp_in"] = _adamw_halves("adamw_w_mlp_in", c_arr, w_mlp_in[0], m_w_mlp_in[0], v_w_mlp_in[0], half_mi, got_mi)
    half_out, got_out = reduced_end("out", sw_out, big["w_mlp_in"][1])
    big["w_out"] = _adamw_halves("adamw_w_out", c_arr, w_out[0], m_w_out[0], v_w_out[0], half_out, got_out)
    g_buf = _split_wait_inplace("sg_gather_wait", sg_send, sg_recv, g_buf, _plan_gather_own, big["w_out"][1])
    sg_send, sg_recv, g_buf, tok = _split_start_inplace("sg_pass_start", g_buf, N_CHIP - 1, _plan_gather_pass, tok)
    sw_in, tok = reduced_begin("in", rs_end(st_in, tok), tok)
    g_all = _split_wait_inplace("sg_pass_wait", sg_send, sg_recv, g_buf, _plan_gather_pass, tok)
    gs_slab, ds_slab, ms_slab, vs_slab = _small_reduce_adamw(g_all, w_slab, m_slab, v_slab)
    shapes = [w.shape for w in small_w] + [dwconv.shape, (LANES,)]
    gs = _unpack(gs_slab, g_sizes, shapes)
    ds = _unpack(ds_slab, g_sizes, shapes)
    ms = _unpack(ms_slab, g_sizes, shapes)
    vs = _unpack(vs_slab, g_sizes, shapes)
    small = {n: (gs[i], ds[i], ms[i], vs[i]) for i, n in enumerate(small_names)}
    loss = gs[-1][0]
    g_wconv = lax.dynamic_slice(gs[-2], (0, chip * n_cv), (CONV_TAPS, n_cv))
    d_wconv, m_wconv, v_wconv = _adamw_plain("adamw_conv", w_conv[0], g_wconv, m_w_conv[0], v_w_conv[0])
    small["w_conv"] = (g_wconv[None], d_wconv[None], m_wconv[None], v_wconv[None])

    dmod_all = g_all[:, :N_MOD * d // LANES, :].reshape(N_DEV, N_MOD * d)
    dmod_sel = lax.dynamic_slice(dmod_all, (0, chip * n_ada), (N_DEV, n_ada)).astype(BF16)
    act_t = act_all.T.astype(BF16)
    big["w_ada"] = _adamw_ada(w_ada[0], m_w_ada[0], v_w_ada[0], act_t, dmod_sel)

    half_in, got_in = reduced_end("in", sw_in, big["w_ada"][1])
    big["w_in"] = _adamw_halves("adamw_w_in", c_arr, w_in[0], m_w_in[0], v_w_in[0], half_in, got_in)

    order = ["w_ada", "b_ada", "g_norm_mix", "w_in", "w_conv", "b_conv", "w_rg_a", "b_rg_a", "w_rg_x", "b_rg_x",
             "lru_lambda", "g_attn_out", "g_lru_out", "w_out", "g_norm_mlp", "w_mlp_in", "w_mlp_out", "g_norm_final"]
    res = {}
    for n in order:
        res[n] = tuple(t[None] for t in big[n]) if n in big else small[n]
    return (loss, grad_x[None],
            *[res[n][0] for n in order], *[res[n][1] for n in order],
            *[res[n][2] for n in order], *[res[n][3] for n in order])
```
